```python
import jax, jax.numpy as jnp
from jax import lax
import numpy as np

D_MODEL = 1024
BATCH = 1
SEQ = 16384
DEPTH = 1
DEC_BATCH = 8
DEC_SEQ = 64
PAST_LEN = 1024

CHUNK = 64
MIX_WIDTH = D_MODEL
ATTN_WIDTH = MIX_WIDTH // 2
POOL_WIDTH = MIX_WIDTH - ATTN_WIDTH
HEAD_DIM = 64
N_HEADS = ATTN_WIDTH // HEAD_DIM
POOL_WINDOWS = (2, 4, 8, 16)
N_POOL_GROUPS = len(POOL_WINDOWS)
POOL_GROUP = POOL_WIDTH // N_POOL_GROUPS
POOL_HIST = max(POOL_WINDOWS) - 1
Q_BLOCK = 128
EPS = 1e-6
IN_COLS = 4 * ATTN_WIDTH + N_HEADS + 2 * POOL_WIDTH

kernel_name = "fox_pool_hybrid_stream_step"


def rmsnorm(x, g):
    xf = x.astype(jnp.float32)
    y = xf * lax.rsqrt(jnp.mean(xf * xf, axis=-1, keepdims=True) + EPS)
    return (y * g.astype(jnp.float32)).astype(x.dtype)


def ada_terms(c, w_ada, b_ada):
    m = jax.nn.silu(c) @ w_ada + b_ada
    shift, scale, gate = jnp.split(m, 3, axis=-1)
    return shift, scale, gate


def project(x, c, norm_g, w_ada, b_ada, w_in, b_f):
    shift, scale, gate = ada_terms(c, w_ada, b_ada)
    h = rmsnorm(x, norm_g) * (1.0 + scale[:, None, :]) + shift[:, None, :]
    p = h @ w_in
    B, L, _ = p.shape
    A = ATTN_WIDTH
    q = p[..., 0:A].reshape(B, L, N_HEADS, HEAD_DIM)
    k = p[..., A:2 * A].reshape(B, L, N_HEADS, HEAD_DIM)
    v = p[..., 2 * A:3 * A].reshape(B, L, N_HEADS, HEAD_DIM)
    ag = p[..., 3 * A:4 * A]
    o = 4 * A
    logf = jax.nn.log_sigmoid(p[..., o:o + N_HEADS].astype(jnp.float32) + b_f.astype(jnp.float32))
    o += N_HEADS
    u = p[..., o:o + POOL_WIDTH]
    pg = p[..., o + POOL_WIDTH:o + 2 * POOL_WIDTH]
    return q, k, v, ag, logf, u, pg, gate


def fox_prompt(q, k, v, logf):
    B, L, H, dh = q.shape
    nb = L // Q_BLOCK
    Ft = jnp.cumsum(logf, axis=1).transpose(0, 2, 1)
    qb = q.reshape(B, nb, Q_BLOCK, H, dh).transpose(1, 0, 2, 3, 4)
    Fb = Ft.reshape(B, H, nb, Q_BLOCK).transpose(2, 0, 1, 3)
    kpos = jnp.arange(L)
    sc = 1.0 / np.sqrt(dh).astype(np.float32)

    def blk(args):
        qi, Fi, i = args
        qpos = i * Q_BLOCK + jnp.arange(Q_BLOCK)
        s = jnp.einsum('bqhd,bkhd->bhqk', qi, k, preferred_element_type=jnp.float32) * sc
        s = s + (Fi[:, :, :, None] - Ft[:, :, None, :])
        s = jnp.where(kpos[None, :] <= qpos[:, None], s, -jnp.inf)
        pr = jax.nn.softmax(s, axis=-1)
        return jnp.einsum('bhqk,bkhd->bqhd', pr.astype(v.dtype), v)

    out = lax.map(blk, (qb, Fb, jnp.arange(nb)))
    return out.transpose(1, 0, 2, 3, 4).reshape(B, L, H * dh)


def fox_sample(q, k_new, v_new, logf_new, cache_k, cache_v, cache_logf):
    B, L, H, dh = q.shape
    P = cache_k.shape[1]
    k = jnp.concatenate([cache_k.astype(k_new.dtype), k_new], axis=1)
    v = jnp.concatenate([cache_v.astype(v_new.dtype), v_new], axis=1)
    lf = jnp.concatenate([cache_logf.astype(jnp.float32), logf_new], axis=1)
    Ft = jnp.cumsum(lf, axis=1).transpose(0, 2, 1)
    sc = 1.0 / np.sqrt(dh).astype(np.float32)
    s = jnp.einsum('bqhd,bkhd->bhqk', q, k, preferred_element_type=jnp.float32) * sc
    s = s + (Ft[:, :, P:, None] - Ft[:, :, None, :])
    qpos = P + jnp.arange(L)
    kpos = jnp.arange(P + L)
    s = jnp.where(kpos[None, :] <= qpos[:, None], s, -jnp.inf)
    pr = jax.nn.softmax(s, axis=-1)
    return jnp.einsum('bhqk,bkhd->bqhd', pr.astype(v.dtype), v).reshape(B, L, H * dh)


def pool_mix(u_ext, start_pos, w_pool, pool_scale):
    B, E, C = u_ext.shape
    L = E - POOL_HIST
    uf = u_ext.astype(jnp.float32)
    cs = jnp.concatenate([jnp.zeros((B, 1, C), jnp.float32), jnp.cumsum(uf, axis=1)], axis=1)
    hi = cs[:, POOL_HIST + 1:POOL_HIST + 1 + L]
    pos = start_pos + jnp.arange(L)
    means = []
    for g, w in enumerate(POOL_WINDOWS):
        sl = slice(g * POOL_GROUP, (g + 1) * POOL_GROUP)
        lo = cs[:, POOL_HIST + 1 - w:POOL_HIST + 1 - w + L, sl]
        cnt = jnp.minimum(pos + 1, w).astype(jnp.float32)[None, :, None]
        means.append((hi[..., sl] - lo) / cnt)
    d = (jnp.concatenate(means, axis=-1) - uf[:, POOL_HIST:]).astype(u_ext.dtype)
    d = d.reshape(B, L, N_POOL_GROUPS, POOL_GROUP)
    y = jnp.einsum('blgc,gcd->blgd', d, w_pool).reshape(B, L, C)
    return y * pool_scale


def merge(x, attn, ag, pool, pg, gate, w_out):
    z = jnp.concatenate([attn * jax.nn.silu(ag), pool * jax.nn.silu(pg)], axis=-1)
    return x + gate[:, None, :] * (z @ w_out)


def setup_inputs(seed: int = 0) -> dict:
    key = jax.random.key(seed)
    ks = jax.random.split(key, 20)
    f32 = jnp.float32
    D = D_MODEL
    nrm = lambda k, s: jax.random.normal(k, s, f32)
    return {
        "x_prompt": nrm(ks[0], (BATCH, SEQ, D)),
        "x_sample": nrm(ks[1], (DEC_BATCH, DEC_SEQ, D)),
        "c_prompt": nrm(ks[2], (BATCH, D)),
        "c_sample": nrm(ks[3], (DEC_BATCH, D)),
        "cache_k": nrm(ks[4], (DEPTH, DEC_BATCH, PAST_LEN, N_HEADS, HEAD_DIM)),
        "cache_v": nrm(ks[5], (DEPTH, DEC_BATCH, PAST_LEN, N_HEADS, HEAD_DIM)),
        "cache_logf": jax.nn.log_sigmoid(nrm(ks[6], (DEPTH, DEC_BATCH, PAST_LEN, N_HEADS)) + 2.0),
        "state_pool": nrm(ks[7], (DEPTH, DEC_BATCH, POOL_HIST, POOL_WIDTH)),
        "norm_g": 1.0 + 0.05 * nrm(ks[8], (DEPTH, D)),
        "w_ada": 0.3 * D ** -0.5 * nrm(ks[9], (DEPTH, D, 3 * D)),
        "b_ada": 0.02 * nrm(ks[10], (DEPTH, 3 * D)),
        "w_in": D ** -0.5 * nrm(ks[11], (DEPTH, D, IN_COLS)),
        "b_f": 0.1 * nrm(ks[12], (DEPTH, N_HEADS)),
        "w_pool": POOL_GROUP ** -0.5 * nrm(ks[13], (DEPTH, N_POOL_GROUPS, POOL_GROUP, POOL_GROUP)),
        "pool_scale": 1.0 + 0.1 * nrm(ks[14], (DEPTH, POOL_WIDTH)),
        "w_out": MIX_WIDTH ** -0.5 * nrm(ks[15], (DEPTH, MIX_WIDTH, D)),
        "final_g": 1.0 + 0.05 * nrm(ks[16], (D,)),
    }


def reference(x_prompt, x_sample, c_prompt, c_sample, cache_k, cache_v, cache_logf, state_pool,
              norm_g, w_ada, b_ada, w_in, b_f, w_pool, pool_scale, w_out, final_g):
    xp, xs = x_prompt, x_sample
    past = cache_k.shape[2]
    kp_l, vp_l, fp_l, pp_l, ks_l, vs_l, fs_l, ps_l = [], [], [], [], [], [], [], []
    for l in range(DEPTH):
        q, k, v, ag, logf, u, pg, gate = project(xp, c_prompt, norm_g[l], w_ada[l], b_ada[l], w_in[l], b_f[l])
        attn = fox_prompt(q, k, v, logf)
        u_ext = jnp.pad(u, ((0, 0), (POOL_HIST, 0), (0, 0)))
        pool = pool_mix(u_ext, 0, w_pool[l], pool_scale[l])
        xp = merge(xp, attn, ag, pool, pg, gate, w_out[l])
        kp_l.append(k); vp_l.append(v); fp_l.append(logf); pp_l.append(u_ext[:, -POOL_HIST:])

        q, k, v, ag, logf, u, pg, gate = project(xs, c_sample, norm_g[l], w_ada[l], b_ada[l], w_in[l], b_f[l])
        attn = fox_sample(q, k, v, logf, cache_k[l], cache_v[l], cache_logf[l])
        u_ext = jnp.concatenate([state_pool[l].astype(u.dtype), u], axis=1)
        pool = pool_mix(u_ext, past, w_pool[l], pool_scale[l])
        xs = merge(xs, attn, ag, pool, pg, gate, w_out[l])
        ks_l.append(k); vs_l.append(v); fs_l.append(logf); ps_l.append(u_ext[:, -POOL_HIST:])

    y_prompt = rmsnorm(xp, final_g)
    y_sample = rmsnorm(xs, final_g)
    k_prompt = jnp.stack(kp_l); v_prompt = jnp.stack(vp_l)
    logf_prompt = jnp.stack(fp_l); pool_prompt = jnp.stack(pp_l)
    k_sample = jnp.stack(ks_l); v_sample = jnp.stack(vs_l)
    logf_sample = jnp.stack(fs_l); pool_sample = jnp.stack(ps_l)
    return (y_prompt, y_sample, k_prompt, v_prompt, logf_prompt, pool_prompt,
            k_sample, v_sample, logf_sample, pool_sample)
```

```python
import functools

import jax
import jax.numpy as jnp
from jax import lax
from jax.experimental import pallas as pl
from jax.experimental.pallas import tpu as pltpu

HEAD_DIM = 64
POOL_WINDOWS = (2, 4, 8, 16)
EPS = 1e-6

LANES = 128
PAIR = LANES // HEAD_DIM
ATT_BLK = 128
NEAR_BLOCKS = 3
EXP_UNDERFLOW = 104.0
NORM_SLACK = 1.01
HIST_PAD = 32
POOL_HIST = max(POOL_WINDOWS) - 1
NEG = -1e30
VMEM_LIMIT = 60 * 1024 * 1024

F32 = jnp.float32
BF16 = jnp.bfloat16


def _silu(x):
    return x * jax.nn.sigmoid(x)


def _dot(a, b):
    return jnp.dot(a, b, preferred_element_type=F32)


def _dot_nt(a, b):
    return lax.dot_general(a, b, (((1,), (1,)), ((), ())), preferred_element_type=F32)


def _split3(x):
    hi = x.astype(BF16).astype(F32)
    r1 = x - hi
    mid = r1.astype(BF16).astype(F32)
    return hi, mid, r1 - mid


def _resident(shape):
    return pl.BlockSpec(shape, lambda *_: (0,) * len(shape), pipeline_mode=pl.Buffered(1))


def _ada_kernel(c_ref, w_ref, b_ref, o_ref):
    a = _silu(c_ref[...]).astype(BF16)
    o_ref[...] = _dot(a, w_ref[...].astype(BF16)) + b_ref[...]


def _ada_terms(c_all, w_ada, b_ada):
    rows, d = c_all.shape
    n = w_ada.shape[1]
    bn = 1024
    return pl.pallas_call(
        _ada_kernel,
        grid=(n // bn,),
        in_specs=[pl.BlockSpec((rows, d), lambda j: (0, 0)),
                  pl.BlockSpec((d, bn), lambda j: (0, j)),
                  pl.BlockSpec((1, bn), lambda j: (0, j))],
        out_specs=pl.BlockSpec((rows, bn), lambda j: (0, j)),
        out_shape=jax.ShapeDtypeStruct((rows, n), F32),
        compiler_params=pltpu.CompilerParams(dimension_semantics=("arbitrary",), vmem_limit_bytes=VMEM_LIMIT),
        name="ada",
    )(c_all, w_ada, b_ada)


def _proj_kernel(x_ref, s1_ref, sh_ref, ng_ref, wq_ref, wf_ref, bf_ref, wu_ref, wp_ref, ps_ref, h0_ref,
                 q_ref, k32_ref, v32_ref, kb_ref, vb_ref, sa_ref, zp_ref, lf_ref, cq_ref, ck_ref, st_ref, ho_ref,
                 e_ref, t2_ref, t4_ref, t8_ref, *, bm, sb, steps_per_stream, start_pos, n_heads):
    a_w = n_heads * HEAD_DIM
    sis = pl.program_id(0) % steps_per_stream
    x = x_ref[...]
    ms = jnp.mean(x * x, axis=-1, keepdims=True)
    xn = x * lax.rsqrt(ms + EPS) * ng_ref[...]
    h = (xn * s1_ref[0] + sh_ref[0]).astype(BF16)

    sc = 1.0 / (HEAD_DIM ** 0.5)
    qs = _dot(h, wq_ref[:, 0:a_w]) * sc
    q_ref[...] = qs.astype(BF16)
    pk = _dot(h, wq_ref[:, a_w:2 * a_w])
    k32_ref[...] = pk
    kb_ref[...] = pk.astype(BF16)
    pv = _dot(h, wq_ref[:, 2 * a_w:3 * a_w])
    v32_ref[...] = pv
    vb_ref[...] = pv.astype(BF16)
    pa = _dot(h, wq_ref[:, 3 * a_w:4 * a_w])
    sa_ref[...] = _silu(pa).astype(BF16)

    z = _dot(h, wf_ref[...]) + bf_ref[...]
    lf = jnp.minimum(z, 0.0) - jnp.log1p(jnp.exp(-jnp.abs(z)))
    lf_ref[...] = lf[:, 0:n_heads]

    sel = (lax.broadcasted_iota(jnp.int32, (a_w, LANES), 0) // HEAD_DIM
           == lax.broadcasted_iota(jnp.int32, (a_w, LANES), 1)).astype(BF16)
    nq = jnp.sqrt(_dot((qs * qs).astype(BF16), sel)) * NORM_SLACK
    nk = jnp.sqrt(_dot((pk * pk).astype(BF16), sel)) * NORM_SLACK

    tri = (lax.broadcasted_iota(jnp.int32, (sb, sb), 1)
           <= lax.broadcasted_iota(jnp.int32, (sb, sb), 0)).astype(BF16)
    tots, qmx, kmx = [], [], []
    for s in range(bm // sb):
        rows = slice(s * sb, (s + 1) * sb)
        hi, mid, lo = _split3(lf[rows])
        cb = _dot(tri, hi.astype(BF16)) + _dot(tri, mid.astype(BF16)) + _dot(tri, lo.astype(BF16))
        cq_ref[rows, :] = cb[:, 0:n_heads]
        if sb < LANES:
            cbt = jnp.concatenate([cb, jnp.zeros((LANES - sb, LANES), F32)], axis=0).T[0:n_heads, 0:sb]
        else:
            cbt = cb.T[0:n_heads, :]
        ck_ref[s] = cbt
        tots.append(cb[sb - 1:sb, :])
        qmx.append(jnp.max(nq[rows], axis=0, keepdims=True))
        kmx.append(jnp.max(nk[rows], axis=0, keepdims=True))
    pad = [jnp.zeros((8 - bm // sb, LANES), F32)] if bm // sb < 8 else []
    st_ref[0, 0] = jnp.concatenate(tots + pad, axis=0)
    st_ref[0, 1] = jnp.concatenate(qmx + pad, axis=0)
    st_ref[0, 2] = jnp.concatenate(kmx + pad, axis=0)

    pw = len(POOL_WINDOWS) * LANES
    pu = _dot(h, wu_ref[:, 0:pw])

    @pl.when(sis == 0)
    def _():
        e_ref[0:HIST_PAD, :] = h0_ref[0]

    n = bm + HIST_PAD
    e_ref[HIST_PAD:n, :] = pu
    t2_ref[8:n, :] = e_ref[8:n, :] + e_ref[7:n - 1, :]
    t4_ref[16:n, :] = t2_ref[16:n, LANES:] + t2_ref[14:n - 2, LANES:]
    t8_ref[24:n, :] = t4_ref[24:n, LANES:] + t4_ref[20:n - 4, LANES:]
    s16 = t8_ref[32:n, LANES:] + t8_ref[24:n - 8, LANES:]
    sums = [t2_ref[HIST_PAD:n, 0:LANES], t4_ref[HIST_PAD:n, 0:LANES], t8_ref[HIST_PAD:n, 0:LANES], s16]
    pos1 = start_pos + sis * bm + lax.broadcasted_iota(jnp.int32, (bm, 1), 0) + 1
    for g, w in enumerate(POOL_WINDOWS):
        cols = slice(g * LANES, (g + 1) * LANES)
        rc = 1.0 / jnp.minimum(pos1, w).astype(F32)
        d = sums[g] * rc - pu[:, cols]
        y = _dot(d.astype(BF16), wp_ref[g]) * ps_ref[:, cols]
        pg = _dot(h, wu_ref[:, pw + g * LANES:pw + (g + 1) * LANES])
        zp_ref[:, cols] = (y * _silu(pg)).astype(BF16)
    ho_ref[0] = e_ref[n - 16:n, :]
    e_ref[0:HIST_PAD, :] = e_ref[bm:n, :]


def _project(x2, s1, sh, norm_g, wq, wf, bfp, wu, wp, ps, hist0, *, bm, sb, steps_per_stream, start_pos, n_heads):
    rows, d = x2.shape
    a_w = n_heads * HEAD_DIM
    pw = len(POOL_WINDOWS) * LANES
    n_steps = rows // bm
    n_streams = n_steps // steps_per_stream
    nsb = bm // sb
    stream = lambda i: i // steps_per_stream
    row_blk = lambda w: pl.BlockSpec((bm, w), lambda i: (i, 0))
    kern = functools.partial(_proj_kernel, bm=bm, sb=sb, steps_per_stream=steps_per_stream,
                             start_pos=start_pos, n_heads=n_heads)
    out_shape = (
        jax.ShapeDtypeStruct((rows, a_w), BF16),
        jax.ShapeDtypeStruct((rows, a_w), F32),
        jax.ShapeDtypeStruct((rows, a_w), F32),
        jax.ShapeDtypeStruct((rows, a_w), BF16),
        jax.ShapeDtypeStruct((rows, a_w), BF16),
        jax.ShapeDtypeStruct((rows, a_w), BF16),
        jax.ShapeDtypeStruct((rows, pw), BF16),
        jax.ShapeDtypeStruct((rows, n_heads), F32),
        jax.ShapeDtypeStruct((rows, n_heads), F32),
        jax.ShapeDtypeStruct((rows // sb, n_heads, sb), F32),
        jax.ShapeDtypeStruct((n_steps, 3, 8, LANES), F32),
        jax.ShapeDtypeStruct((n_streams, 16, pw), F32),
    )
    out_specs = (
        row_blk(a_w), row_blk(a_w), row_blk(a_w), row_blk(a_w), row_blk(a_w), row_blk(a_w), row_blk(pw),
        row_blk(n_heads), row_blk(n_heads),
        pl.BlockSpec((nsb, n_heads, sb), lambda i: (i, 0, 0)),
        pl.BlockSpec((1, 3, 8, LANES), lambda i: (i, 0, 0, 0)),
        pl.BlockSpec((1, 16, pw), lambda i: (stream(i), 0, 0)),
    )
    in_specs = [
        row_blk(d),
        pl.BlockSpec((1, 1, d), lambda i: (stream(i), 0, 0)),
        pl.BlockSpec((1, 1, d), lambda i: (stream(i), 0, 0)),
        _resident((1, d)),
        _resident(wq.shape), _resident(wf.shape), _resident(bfp.shape), _resident(wu.shape),
        _resident(wp.shape), _resident(ps.shape),
        pl.BlockSpec((1, HIST_PAD, pw), lambda i: (stream(i), 0, 0)),
    ]
    return pl.pallas_call(
        kern,
        grid=(n_steps,),
        in_specs=in_specs,
        out_specs=out_specs,
        out_shape=out_shape,
        scratch_shapes=[pltpu.VMEM((bm + HIST_PAD, pw), F32),
                        pltpu.VMEM((bm + HIST_PAD, pw), F32),
                        pltpu.VMEM((bm + HIST_PAD, pw - LANES), F32),
                        pltpu.VMEM((bm + HIST_PAD, pw - 2 * LANES), F32)],
        compiler_params=pltpu.CompilerParams(dimension_semantics=("arbitrary",), vmem_limit_bytes=VMEM_LIMIT),
        name="proj",
    )(x2, s1, sh, norm_g, wq, wf, bfp, wu, wp, ps, hist0)


def _merge_norm(x, gate, za, zp, wo_ref, fg, a_w):
    dy = _dot(za, wo_ref[0:a_w, :]) + _dot(zp, wo_ref[a_w:, :])
    out = x + gate * dy
    ms = jnp.mean(out * out, axis=-1, keepdims=True)
    return out * lax.rsqrt(ms + EPS) * fg


def _attn_kernel(tot_ref, qkb_ref,
                 q_ref, cq_ref, ck_ref, k_ref, v_ref, sa_ref, zp_ref, x_ref, gate_ref, wo_ref, fg_ref,
                 y_ref,
                 z_ref, m_ref, l_ref, acc_ref, *, tm, n_heads):
    a_w = n_heads * HEAD_DIM
    nsub = tm // ATT_BLK
    w_near = NEAR_BLOCKS
    cw = w_near * ATT_BLK
    step = pl.program_id(0)
    lane = lax.broadcasted_iota(jnp.int32, (ATT_BLK, LANES), 1)
    half = [lane < HEAD_DIM, lane >= HEAD_DIM]
    col = lax.broadcasted_iota(jnp.int32, (ATT_BLK, cw), 1)
    delta = col - lax.broadcasted_iota(jnp.int32, (ATT_BLK, cw), 0)

    def tot_at(b, hd):
        return jnp.where(b >= 0, tot_ref[jnp.maximum(b, 0) * n_heads + hd], 0.0)

    def scores(r0, hd, j0, offs):
        g = hd // PAIR
        cols = slice(g * LANES, (g + 1) * LANES)
        q2 = q_ref[pl.ds(r0, ATT_BLK), cols]
        qm = jnp.where(half[hd % PAIR], q2, jnp.zeros_like(q2))
        kc = k_ref[pl.ds(pl.multiple_of(j0 * ATT_BLK, ATT_BLK), cw), cols]
        s = _dot_nt(qm, kc)
        dec = jnp.concatenate([offs[p] - ck_ref[j0 + p, hd:hd + 1, :] for p in range(w_near)], axis=1)
        return s + cq_ref[pl.ds(r0, ATT_BLK), hd:hd + 1] + dec

    def values(hd, j0):
        g = hd // PAIR
        return v_ref[pl.ds(pl.multiple_of(j0 * ATT_BLK, ATT_BLK), cw), g * LANES:(g + 1) * LANES]

    def sub_body(sub, carry):
        i = step * nsub + sub
        r0 = pl.multiple_of(sub * ATT_BLK, ATT_BLK)

        j0 = jnp.maximum(i - (w_near - 1), 0)
        keep = delta <= (i - j0) * ATT_BLK
        offs_far = []
        for hd in range(n_heads):
            back = [tot_at(i - dd, hd) for dd in range(1, w_near)]
            offs = []
            for p in range(w_near):
                behind = i - (j0 + p)
                o = jnp.float32(0.0)
                for dd in range(1, w_near):
                    o = o + jnp.where(behind >= dd, back[dd - 1], 0.0)
                offs.append(o)
            s = jnp.where(keep, scores(r0, hd, j0, offs), NEG)
            m = jnp.max(s, axis=1, keepdims=True)
            p_ = jnp.exp(s - m)
            m_ref[hd] = m
            l_ref[hd] = jnp.sum(p_, axis=1, keepdims=True)
            acc_ref[hd] = _dot(p_.astype(BF16), values(hd, j0))
            o = jnp.float32(0.0)
            for t in back:
                o = o + t
            offs_far.append(o)

        qk = [qkb_ref[i * n_heads + hd] for hd in range(n_heads)]

        def far_cond(c):
            top = i - c[0] * w_near
            need = qk[0] + c[1] >= -EXP_UNDERFLOW
            for hd in range(1, n_heads):
                need = jnp.logical_or(need, qk[hd] + c[1 + hd] >= -EXP_UNDERFLOW)
            return jnp.logical_and(top >= 0, need)

        def far_body(c):
            top = i - c[0] * w_near
            jc = jnp.maximum(top - (w_near - 1), 0)
            keepc = col < (top + 1 - jc) * ATT_BLK
            new = [c[0] + 1]
            for hd in range(n_heads):
                tt = [tot_at(top - e, hd) for e in range(w_near)]
                offs = []
                for p in range(w_near):
                    behind = top - (jc + p)
                    o = c[1 + hd]
                    for e in range(w_near):
                        o = o + jnp.where(behind >= e, tt[e], 0.0)
                    offs.append(o)
                s = jnp.where(keepc, scores(r0, hd, jc, offs), NEG)
                m_old = m_ref[hd]
                m_new = jnp.maximum(m_old, jnp.max(s, axis=1, keepdims=True))
                alpha = jnp.exp(m_old - m_new)
                p_ = jnp.exp(s - m_new)
                m_ref[hd] = m_new
                l_ref[hd] = alpha * l_ref[hd] + jnp.sum(p_, axis=1, keepdims=True)
                acc_ref[hd] = alpha * acc_ref[hd] + _dot(p_.astype(BF16), values(hd, jc))
                o = c[1 + hd]
                for t in tt:
                    o = o + t
                new.append(o)
            return tuple(new)

        lax.while_loop(far_cond, far_body, (jnp.int32(1),) + tuple(offs_far))

        for g in range(n_heads // PAIR):
            cols = slice(g * LANES, (g + 1) * LANES)
            o0 = acc_ref[PAIR * g] / l_ref[PAIR * g]
            o1 = acc_ref[PAIR * g + 1] / l_ref[PAIR * g + 1]
            o = jnp.where(half[0], o0, o1)
            z_ref[pl.ds(r0, ATT_BLK), cols] = (o * sa_ref[pl.ds(r0, ATT_BLK), cols].astype(F32)).astype(BF16)
        return carry

    lax.fori_loop(0, nsub, sub_body, 0)
    y_ref[...] = _merge_norm(x_ref[...], gate_ref[...], z_ref[...], zp_ref[...], wo_ref, fg_ref[...], a_w)


def _prompt_attention(tot, qkb, q, cq, ck, kb, vb, sa, zp, x2, gate, wo, fg, *, tm, n_heads):
    rows, d = x2.shape
    a_w = n_heads * HEAD_DIM
    assert rows % tm == 0 and tm % ATT_BLK == 0 and rows // ATT_BLK >= NEAR_BLOCKS
    row_blk = lambda w: pl.BlockSpec((tm, w), lambda i, *_: (i, 0))
    grid_spec = pltpu.PrefetchScalarGridSpec(
        num_scalar_prefetch=2,
        grid=(rows // tm,),
        in_specs=[row_blk(a_w), row_blk(n_heads), _resident(ck.shape), _resident(kb.shape), _resident(vb.shape),
                  row_blk(a_w), row_blk(zp.shape[1]), row_blk(d), _resident((1, d)), _resident(wo.shape),
                  _resident((1, d))],
        out_specs=row_blk(d),
        scratch_shapes=[pltpu.VMEM((tm, a_w), BF16),
                        pltpu.VMEM((n_heads, ATT_BLK, 1), F32),
                        pltpu.VMEM((n_heads, ATT_BLK, 1), F32),
                        pltpu.VMEM((n_heads, ATT_BLK, LANES), F32)],
    )
    return pl.pallas_call(
        functools.partial(_attn_kernel, tm=tm, n_heads=n_heads),
        grid_spec=grid_spec,
        out_shape=jax.ShapeDtypeStruct((rows, d), F32),
        compiler_params=pltpu.CompilerParams(dimension_semantics=("arbitrary",), vmem_limit_bytes=VMEM_LIMIT),
        name="attn",
    )(tot, qkb, q, cq, ck, kb, vb, sa, zp, x2, gate, wo, fg)


def _sattn_kernel(q_ref, cq_ref, ckn_ref, kn_ref, vn_ref, kc_ref, vc_ref, lfc_ref, sa_ref, zp_ref, x_ref,
                  gate_ref, wo_ref, fg_ref, y_ref, *, n_heads):
    a_w = n_heads * HEAD_DIM
    ln = q_ref.shape[0]
    past = kc_ref.shape[1]
    nb = past // LANES

    lfc = lfc_ref[0]
    triu = (lax.broadcasted_iota(jnp.int32, (LANES, LANES), 0)
            <= lax.broadcasted_iota(jnp.int32, (LANES, LANES), 1)).astype(BF16)
    zeros = jnp.zeros((8, LANES), F32)
    parts = []
    for b in range(nb):
        parts.extend(_split3(lfc[:, b * LANES:(b + 1) * LANES]))
        parts.append(zeros)
    cs = _dot(jnp.concatenate(parts, axis=0).astype(BF16), triu)
    after = jnp.zeros((n_heads, 1), F32)
    suffix = [None] * nb
    for b in reversed(range(nb)):
        cb = cs[32 * b:32 * b + 8] + cs[32 * b + 8:32 * b + 16] + cs[32 * b + 16:32 * b + 24]
        tot = cb[:, LANES - 1:LANES]
        suffix[b] = (tot - cb) + after
        after = after + tot
    dec_c = jnp.concatenate(suffix, axis=1)

    lane = lax.broadcasted_iota(jnp.int32, (ln, LANES), 1)
    half = [lane < HEAD_DIM, lane >= HEAD_DIM]
    causal = lax.broadcasted_iota(jnp.int32, (ln, ln), 1) <= lax.broadcasted_iota(jnp.int32, (ln, ln), 0)
    outs = []
    for g in range(n_heads // PAIR):
        cols = slice(g * LANES, (g + 1) * LANES)
        q2 = q_ref[:, cols]
        kc = kc_ref[0, :, cols].astype(BF16)
        vc = vc_ref[0, :, cols].astype(BF16)
        kn = kn_ref[:, cols]
        vn = vn_ref[:, cols]
        o_pair = []
        for e in range(PAIR):
            hd = PAIR * g + e
            qm = jnp.where(half[e], q2, jnp.zeros_like(q2))
            cqh = cq_ref[:, hd:hd + 1]
            s_c = _dot_nt(qm, kc) + cqh + dec_c[hd:hd + 1, :]
            s_n = jnp.where(causal, _dot_nt(qm, kn) + cqh - ckn_ref[0, hd:hd + 1, :], NEG)
            m = jnp.maximum(jnp.max(s_c, axis=1, keepdims=True), jnp.max(s_n, axis=1, keepdims=True))
            p_c = jnp.exp(s_c - m)
            p_n = jnp.exp(s_n - m)
            l = jnp.sum(p_c, axis=1, keepdims=True) + jnp.sum(p_n, axis=1, keepdims=True)
            o_pair.append((_dot(p_c.astype(BF16), vc) + _dot(p_n.astype(BF16), vn)) / l)
        o = jnp.where(half[0], o_pair[0], o_pair[1])
        outs.append((o * sa_ref[:, cols].astype(F32)).astype(BF16))
    za = jnp.concatenate(outs, axis=1)
    y_ref[...] = _merge_norm(x_ref[...], gate_ref[0], za, zp_ref[...], wo_ref, fg_ref[...], a_w)


def _sample_attention(q, cq, ckn, kb, vb, cache_k, cache_v, lfc, sa, zp, x2, gate, wo, fg, *, ln, n_heads):
    rows, d = x2.shape
    nbatch = rows // ln
    a_w = n_heads * HEAD_DIM
    past = cache_k.shape[1]
    row_blk = lambda w: pl.BlockSpec((ln, w), lambda b: (b, 0))
    per_b = lambda s: pl.BlockSpec((1,) + s, lambda b: (b, 0, 0))
    return pl.pallas_call(
        functools.partial(_sattn_kernel, n_heads=n_heads),
        grid=(nbatch,),
        in_specs=[row_blk(a_w), row_blk(n_heads), per_b((n_heads, ln)), row_blk(a_w), row_blk(a_w),
                  per_b((past, a_w)), per_b((past, a_w)), per_b((n_heads, past)),
                  row_blk(a_w), row_blk(zp.shape[1]), row_blk(d), per_b((1, d)), _resident(wo.shape),
                  _resident((1, d))],
        out_specs=row_blk(d),
        out_shape=jax.ShapeDtypeStruct((rows, d), F32),
        compiler_params=pltpu.CompilerParams(dimension_semantics=("arbitrary",), vmem_limit_bytes=VMEM_LIMIT),
        name="sattn",
    )(q, cq, ckn, kb, vb, cache_k, cache_v, lfc, sa, zp, x2, gate, wo, fg)


def kernel(x_prompt, x_sample, c_prompt, c_sample, cache_k, cache_v, cache_logf, state_pool, norm_g, w_ada, b_ada,
           w_in, b_f, w_pool, pool_scale, w_out, final_g):
    depth = norm_g.shape[0]
    assert depth == 1
    bp, seq, d = x_prompt.shape
    bs, ln, _ = x_sample.shape
    assert bp == 1
    n_heads = cache_k.shape[3]
    past = cache_k.shape[2]
    a_w = n_heads * HEAD_DIM
    pw = state_pool.shape[3]
    assert pw == len(POOL_WINDOWS) * LANES and cache_k.shape[4] == HEAD_DIM and n_heads <= 8

    n_c = bp + bs
    c_all = jnp.concatenate([c_prompt, c_sample, jnp.zeros((16 - n_c, d), F32)], axis=0)
    m_all = _ada_terms(c_all, w_ada[0], b_ada[0][None, :])
    shift, scale, gate = m_all[:, 0:d], m_all[:, d:2 * d], m_all[:, 2 * d:3 * d]
    s1 = (1.0 + scale)[:, None, :]
    sh = shift[:, None, :]

    wi = w_in[0]
    wq = wi[:, 0:4 * a_w].astype(BF16)
    wf = jnp.pad(wi[:, 4 * a_w:4 * a_w + n_heads], ((0, 0), (0, LANES - n_heads))).astype(BF16)
    wu = wi[:, 4 * a_w + n_heads:].astype(BF16)
    bfp = jnp.pad(b_f[0][None, :], ((0, 0), (0, LANES - n_heads)))
    wp = w_pool[0].astype(BF16)
    ps = pool_scale[0][None, :]
    wo = w_out[0].astype(BF16)
    ng = norm_g[0][None, :]
    fg = final_g[None, :]

    bm = 512
    xp2 = x_prompt.reshape(seq, d)
    hist_p = jnp.zeros((1, HIST_PAD, pw), F32)
    (q_p, k_p, v_p, kb_p, vb_p, sa_p, zp_p, lf_p, cq_p, ck_p, st_p, ho_p) = _project(
        xp2, s1[0:1], sh[0:1], ng, wq, wf, bfp, wu, wp, ps, hist_p,
        bm=bm, sb=ATT_BLK, steps_per_stream=seq // bm, start_pos=0, n_heads=n_heads)
    nsb = bm // ATT_BLK
    st = st_p[:, :, 0:nsb, 0:n_heads]
    tot = st[:, 0].reshape(-1)
    qn = st[:, 1].reshape(-1, n_heads)
    kn = st[:, 2].reshape(-1, n_heads)
    qkb = (2.0 * qn * jnp.max(kn, axis=0, keepdims=True)).reshape(-1)
    y_p = _prompt_attention(tot, qkb, q_p, cq_p, ck_p, kb_p, vb_p, sa_p, zp_p, xp2, gate[0:1], wo, fg,
                            tm=512, n_heads=n_heads)

    xs2 = x_sample.reshape(bs * ln, d)
    hist_s = jnp.pad(state_pool[0], ((0, 0), (HIST_PAD - POOL_HIST, 0), (0, 0)))
    (q_s, k_s, v_s, kb_s, vb_s, sa_s, zp_s, lf_s, cq_s, ck_s, _, ho_s) = _project(
        xs2, s1[bp:n_c], sh[bp:n_c], ng, wq, wf, bfp, wu, wp, ps, hist_s,
        bm=ln, sb=ln, steps_per_stream=1, start_pos=past, n_heads=n_heads)
    lfc = jnp.swapaxes(cache_logf[0], 1, 2)
    y_s = _sample_attention(q_s, cq_s, ck_s, kb_s, vb_s, cache_k[0].reshape(bs, past, a_w),
                            cache_v[0].reshape(bs, past, a_w), lfc, sa_s, zp_s, xs2, gate[bp:n_c, None, :], wo, fg,
                            ln=ln, n_heads=n_heads)

    hd = (n_heads, HEAD_DIM)
    return (y_p.reshape(bp, seq, d), y_s.reshape(bs, ln, d),
            k_p.reshape((1, bp, seq) + hd), v_p.reshape((1, bp, seq) + hd), lf_p.reshape(1, bp, seq, n_heads),
            ho_p[:, 16 - POOL_HIST:, :][None],
            k_s.reshape((1, bs, ln) + hd), v_s.reshape((1, bs, ln) + hd), lf_s.reshape(1, bs, ln, n_heads),
            ho_s[:, 16 - POOL_HIST:, :][None])
```

```python
import functools

import jax
import jax.numpy as jnp
from jax import lax
from jax.experimental import pallas as pl
from jax.experimental.pallas import tpu as pltpu

HEAD_DIM = 64
POOL_WINDOWS = (2, 4, 8, 16)
EPS = 1e-6

LANES = 128
PAIR = LANES // HEAD_DIM
ATT_BLK = 128
NEAR_BLOCKS = 3
EXP_UNDERFLOW = 104.0
NORM_SLACK = 1.01
HIST_PAD = 32
POOL_HIST = max(POOL_WINDOWS) - 1
NEG = -1e30
VMEM_LIMIT = 60 * 1024 * 1024

F32 = jnp.float32
BF16 = jnp.bfloat16


def _silu(x):
    return x * jax.nn.sigmoid(x)


def _dot(a, b):
    return jnp.dot(a, b, preferred_element_type=F32)


def _dot_nt(a, b):
    return lax.dot_general(a, b, (((1,), (1,)), ((), ())), preferred_element_type=F32)


def _split3(x):
    hi = x.astype(BF16).astype(F32)
    r1 = x - hi
    mid = r1.astype(BF16).astype(F32)
    return hi, mid, r1 - mid


def _resident(shape):
    return pl.BlockSpec(shape, lambda *_: (0,) * len(shape), pipeline_mode=pl.Buffered(1))


def _ada_kernel(c_ref, w_ref, b_ref, o_ref):
    a = _silu(c_ref[...]).astype(BF16)
    o_ref[...] = _dot(a, w_ref[...].astype(BF16)) + b_ref[...]


def _ada_terms(c_all, w_ada, b_ada):
    rows, d = c_all.shape
    n = w_ada.shape[1]
    bn = 1024
    return pl.pallas_call(
        _ada_kernel,
        grid=(n // bn,),
        in_specs=[pl.BlockSpec((rows, d), lambda j: (0, 0)),
                  pl.BlockSpec((d, bn), lambda j: (0, j)),
                  pl.BlockSpec((1, bn), lambda j: (0, j))],
        out_specs=pl.BlockSpec((rows, bn), lambda j: (0, j)),
        out_shape=jax.ShapeDtypeStruct((rows, n), F32),
        compiler_params=pltpu.CompilerParams(dimension_semantics=("arbitrary",), vmem_limit_bytes=VMEM_LIMIT),
        name="ada",
    )(c_all, w_ada, b_ada)


def _proj_kernel(x_ref, s1_ref, sh_ref, ng_ref, wq_ref, wf_ref, bf_ref, wu_ref, wp_ref, ps_ref, h0_ref,
                 q_ref, k32_ref, v32_ref, kb_ref, vb_ref, sa_ref, zp_ref, lf_ref, cq_ref, ck_ref, st_ref, ho_ref,
                 e_ref, t2_ref, t4_ref, t8_ref, *, bm, sb, steps_per_stream, start_pos, n_heads):
    a_w = n_heads * HEAD_DIM
    sis = pl.program_id(0) % steps_per_stream
    x = x_ref[...]
    ms = jnp.mean(x * x, axis=-1, keepdims=True)
    xn = x * lax.rsqrt(ms + EPS) * ng_ref[...]
    h = (xn * s1_ref[0] + sh_ref[0]).astype(BF16)

    sc = 1.0 / (HEAD_DIM ** 0.5)
    qs = _dot(h, wq_ref[:, 0:a_w]) * sc
    q_ref[...] = qs.astype(BF16)
    pk = _dot(h, wq_ref[:, a_w:2 * a_w])
    k32_ref[...] = pk
    kb_ref[...] = pk.astype(BF16)
    pv = _dot(h, wq_ref[:, 2 * a_w:3 * a_w])
    v32_ref[...] = pv
    vb_ref[...] = pv.astype(BF16)
    pa = _dot(h, wq_ref[:, 3 * a_w:4 * a_w])
    sa_ref[...] = _silu(pa).astype(BF16)

    z = _dot(h, wf_ref[...]) + bf_ref[...]
    lf = jnp.minimum(z, 0.0) - jnp.log1p(jnp.exp(-jnp.abs(z)))
    lf_ref[...] = lf[:, 0:n_heads]

    sel = (lax.broadcasted_iota(jnp.int32, (a_w, LANES), 0) // HEAD_DIM
           == lax.broadcasted_iota(jnp.int32, (a_w, LANES), 1)).astype(BF16)
    nq = jnp.sqrt(_dot((qs * qs).astype(BF16), sel)) * NORM_SLACK
    nk = jnp.sqrt(_dot((pk * pk).astype(BF16), sel)) * NORM_SLACK

    tri = (lax.broadcasted_iota(jnp.int32, (sb, sb), 1)
           <= lax.broadcasted_iota(jnp.int32, (sb, sb), 0)).astype(BF16)
    tots, qmx, kmx = [], [], []
    for s in range(bm // sb):
        rows = slice(s * sb, (s + 1) * sb)
        hi, mid, lo = _split3(lf[rows])
        cb = _dot(tri, hi.astype(BF16)) + _dot(tri, mid.astype(BF16)) + _dot(tri, lo.astype(BF16))
        cq_ref[rows, :] = cb[:, 0:n_heads]
        if sb < LANES:
            cbt = jnp.concatenate([cb, jnp.zeros((LANES - sb, LANES), F32)], axis=0).T[0:n_heads, 0:sb]
        else:
            cbt = cb.T[0:n_heads, :]
        ck_ref[s] = cbt
        tots.append(cb[sb - 1:sb, :])
        qmx.append(jnp.max(nq[rows], axis=0, keepdims=True))
        kmx.append(jnp.max(nk[rows], axis=0, keepdims=True))
    pad = [jnp.zeros((8 - bm // sb, LANES), F32)] if bm // sb < 8 else []
    st_ref[0, 0] = jnp.concatenate(tots + pad, axis=0)
    st_ref[0, 1] = jnp.concatenate(qmx + pad, axis=0)
    st_ref[0, 2] = jnp.concatenate(kmx + pad, axis=0)

    pw = len(POOL_WINDOWS) * LANES
    pu = _dot(h, wu_ref[:, 0:pw])

    @pl.when(sis == 0)
    def _():
        e_ref[0:HIST_PAD, :] = h0_ref[0]

    n = bm + HIST_PAD
    e_ref[HIST_PAD:n, :] = pu
    t2_ref[8:n, :] = e_ref[8:n, :] + e_ref[7:n - 1, :]
    t4_ref[16:n, :] = t2_ref[16:n, LANES:] + t2_ref[14:n - 2, LANES:]
    t8_ref[24:n, :] = t4_ref[24:n, LANES:] + t4_ref[20:n - 4, LANES:]
    s16 = t8_ref[32:n, LANES:] + t8_ref[24:n - 8, LANES:]
    sums = [t2_ref[HIST_PAD:n, 0:LANES], t4_ref[HIST_PAD:n, 0:LANES], t8_ref[HIST_PAD:n, 0:LANES], s16]
    pos1 = start_pos + sis * bm + lax.broadcasted_iota(jnp.int32, (bm, 1), 0) + 1
    for g, w in enumerate(POOL_WINDOWS):
        cols = slice(g * LANES, (g + 1) * LANES)
        rc = 1.0 / jnp.minimum(pos1, w).astype(F32)
        d = sums[g] * rc - pu[:, cols]
        y = _dot(d.astype(BF16), wp_ref[g]) * ps_ref[:, cols]
        pg = _dot(h, wu_ref[:, pw + g * LANES:pw + (g + 1) * LANES])
        zp_ref[:, cols] = (y * _silu(pg)).astype(BF16)
    ho_ref[0] = e_ref[n - 16:n, :]
    e_ref[0:HIST_PAD, :] = e_ref[bm:n, :]


def _project(x2, s1, sh, norm_g, wq, wf, bfp, wu, wp, ps, hist0, *, bm, sb, steps_per_stream, start_pos, n_heads):
    rows, d = x2.shape
    a_w = n_heads * HEAD_DIM
    pw = len(POOL_WINDOWS) * LANES
    n_steps = rows // bm
    n_streams = n_steps // steps_per_stream
    nsb = bm // sb
    stream = lambda i: i // steps_per_stream
    row_blk = lambda w: pl.BlockSpec((bm, w), lambda i: (i, 0))
    kern = functools.partial(_proj_kernel, bm=bm, sb=sb, steps_per_stream=steps_per_stream,
                             start_pos=start_pos, n_heads=n_heads)
    out_shape = (
        jax.ShapeDtypeStruct((rows, a_w), BF16),
        jax.ShapeDtypeStruct((rows, a_w), F32),
        jax.ShapeDtypeStruct((rows, a_w), F32),
        jax.ShapeDtypeStruct((rows, a_w), BF16),
        jax.ShapeDtypeStruct((rows, a_w), BF16),
        jax.ShapeDtypeStruct((rows, a_w), BF16),
        jax.ShapeDtypeStruct((rows, pw), BF16),
        jax.ShapeDtypeStruct((rows, n_heads), F32),
        jax.ShapeDtypeStruct((rows, n_heads), F32),
        jax.ShapeDtypeStruct((rows // sb, n_heads, sb), F32),
        jax.ShapeDtypeStruct((n_steps, 3, 8, LANES), F32),
        jax.ShapeDtypeStruct((n_streams, 16, pw), F32),
    )
    out_specs = (
        row_blk(a_w), row_blk(a_w), row_blk(a_w), row_blk(a_w), row_blk(a_w), row_blk(a_w), row_blk(pw),
        row_blk(n_heads), row_blk(n_heads),
        pl.BlockSpec((nsb, n_heads, sb), lambda i: (i, 0, 0)),
        pl.BlockSpec((1, 3, 8, LANES), lambda i: (i, 0, 0, 0)),
        pl.BlockSpec((1, 16, pw), lambda i: (stream(i), 0, 0)),
    )
    in_specs = [
        row_blk(d),
        pl.BlockSpec((1, 1, d), lambda i: (stream(i), 0, 0)),
        pl.BlockSpec((1, 1, d), lambda i: (stream(i), 0, 0)),
        _resident((1, d)),
        _resident(wq.shape), _resident(wf.shape), _resident(bfp.shape), _resident(wu.shape),
        _resident(wp.shape), _resident(ps.shape),
        pl.BlockSpec((1, HIST_PAD, pw), lambda i: (stream(i), 0, 0)),
    ]
    return pl.pallas_call(
        kern,
        grid=(n_steps,),
        in_specs=in_specs,
        out_specs=out_specs,
        out_shape=out_shape,
        scratch_shapes=[pltpu.VMEM((bm + HIST_PAD, pw), F32),
                        pltpu.VMEM((bm + HIST_PAD, pw), F32),
                        pltpu.VMEM((bm + HIST_PAD, pw - LANES), F32),
                        pltpu.VMEM((bm + HIST_PAD, pw - 2 * LANES), F32)],
        compiler_params=pltpu.CompilerParams(dimension_semantics=("arbitrary",), vmem_limit_bytes=VMEM_LIMIT),
        name="proj",
    )(x2, s1, sh, norm_g, wq, wf, bfp, wu, wp, ps, hist0)


def _merge_norm(x, gate, za, zp, wo_ref, fg, a_w):
    dy = _dot(za, wo_ref[0:a_w, :]) + _dot(zp, wo_ref[a_w:, :])
    out = x + gate * dy
    ms = jnp.mean(out * out, axis=-1, keepdims=True)
    return out * lax.rsqrt(ms + EPS) * fg


def _attn_kernel(tot_ref, qkb_ref,
                 q_ref, cq_ref, ck_ref, k_ref, v_ref, sa_ref, zp_ref, x_ref, gate_ref, wo_ref, fg_ref,
                 y_ref,
                 z_ref, s_ref, p_ref, m_ref, l_ref, acc_ref, *, tm, n_heads):
    a_w = n_heads * HEAD_DIM
    nsub = tm // ATT_BLK
    w_near = NEAR_BLOCKS
    cw = w_near * ATT_BLK
    step = pl.program_id(0)
    lane = lax.broadcasted_iota(jnp.int32, (ATT_BLK, LANES), 1)
    half = [lane < HEAD_DIM, lane >= HEAD_DIM]
    col = lax.broadcasted_iota(jnp.int32, (ATT_BLK, cw), 1)

    def tot_at(b, hd):
        return jnp.where(b >= 0, tot_ref[jnp.maximum(b, 0) * n_heads + hd], 0.0)

    def q_masked(r0, hd):
        q2 = q_ref[pl.ds(r0, ATT_BLK), (hd // PAIR) * LANES:(hd // PAIR + 1) * LANES]
        return jnp.where(half[hd % PAIR], q2, jnp.zeros_like(q2))

    def rows_of(ref, blocks, hd):
        g = hd // PAIR
        return jnp.concatenate(
            [ref[pl.ds(pl.multiple_of(b * ATT_BLK, ATT_BLK), ATT_BLK), g * LANES:(g + 1) * LANES] for b in blocks],
            axis=0)

    def scores(r0, hd, j0, offs):
        s = _dot_nt(q_masked(r0, hd), rows_of(k_ref, [j0 + p for p in range(w_near)], hd))
        dec = jnp.concatenate([offs[p] - ck_ref[j0 + p, hd:hd + 1, :] for p in range(w_near)], axis=1)
        return s + cq_ref[pl.ds(r0, ATT_BLK), hd:hd + 1] + dec

    def values(hd, j0):
        return rows_of(v_ref, [j0 + p for p in range(w_near)], hd)

    tri = (lax.broadcasted_iota(jnp.int32, (ATT_BLK, ATT_BLK), 1)
           <= lax.broadcasted_iota(jnp.int32, (ATT_BLK, ATT_BLK), 0))
    tri_bias = jnp.where(tri, 0.0, NEG).astype(F32)
    ones_v = jnp.ones((cw, LANES), BF16)

    def sub_body(sub, carry):
        i = step * nsub + sub
        r0 = pl.multiple_of(sub * ATT_BLK, ATT_BLK)

        near = [i - (w_near - 1) + p for p in range(w_near)]
        near_c = [jnp.maximum(b, 0) for b in near]
        offs_far = []
        for hd in range(n_heads):
            back = [tot_at(i - dd, hd) for dd in range(1, w_near)]
            s = _dot_nt(q_masked(r0, hd), rows_of(k_ref, near_c, hd))
            cqh = cq_ref[pl.ds(r0, ATT_BLK), hd:hd + 1]
            pieces = []
            for p in range(w_near):
                o = jnp.float32(0.0)
                for dd in range(1, w_near - p):
                    o = o + back[dd - 1]
                o = jnp.where(near[p] >= 0, o, NEG)
                sp = s[:, p * ATT_BLK:(p + 1) * ATT_BLK] + cqh + (o - ck_ref[near_c[p], hd:hd + 1, :])
                pieces.append(sp + tri_bias if p == w_near - 1 else sp)
            s_ref[hd] = jnp.concatenate(pieces, axis=1)
            o = jnp.float32(0.0)
            for t in back:
                o = o + t
            offs_far.append(o)
        for hd in range(n_heads):
            s = s_ref[hd]
            m = jnp.max(s, axis=1, keepdims=True)
            m_ref[hd] = m
            p_ref[hd] = jnp.exp(s - m).astype(BF16)
        for hd in range(n_heads):
            r = _dot(p_ref[hd], jnp.concatenate([rows_of(v_ref, near_c, hd), ones_v], axis=1))
            acc_ref[hd] = r[:, 0:LANES]
            l_ref[hd] = r[:, LANES:]

        qk = [qkb_ref[i * n_heads + hd] for hd in range(n_heads)]

        def far_cond(c):
            top = i - c[0] * w_near
            need = qk[0] + c[1] >= -EXP_UNDERFLOW
            for hd in range(1, n_heads):
                need = jnp.logical_or(need, qk[hd] + c[1 + hd] >= -EXP_UNDERFLOW)
            return jnp.logical_and(top >= 0, need)

        def far_body(c):
            top = i - c[0] * w_near
            jc = jnp.maximum(top - (w_near - 1), 0)
            keepc = col < (top + 1 - jc) * ATT_BLK
            new = [c[0] + 1]
            for hd in range(n_heads):
                tt = [tot_at(top - e, hd) for e in range(w_near)]
                offs = []
                for p in range(w_near):
                    behind = top - (jc + p)
                    o = c[1 + hd]
                    for e in range(w_near):
                        o = o + jnp.where(behind >= e, tt[e], 0.0)
                    offs.append(o)
                s = jnp.where(keepc, scores(r0, hd, jc, offs), NEG)
                m_old = m_ref[hd]
                m_new = jnp.maximum(m_old, jnp.max(s, axis=1, keepdims=True))
                alpha = jnp.exp(m_old - m_new)
                p_ = jnp.exp(s - m_new)
                m_ref[hd] = m_new
                l_ref[hd] = alpha * l_ref[hd] + jnp.sum(p_, axis=1, keepdims=True)
                acc_ref[hd] = alpha * acc_ref[hd] + _dot(p_.astype(BF16), values(hd, jc))
                o = c[1 + hd]
                for t in tt:
                    o = o + t
                new.append(o)
            return tuple(new)

        lax.while_loop(far_cond, far_body, (jnp.int32(1),) + tuple(offs_far))

        for g in range(n_heads // PAIR):
            cols = slice(g * LANES, (g + 1) * LANES)
            o0 = acc_ref[PAIR * g] / l_ref[PAIR * g]
            o1 = acc_ref[PAIR * g + 1] / l_ref[PAIR * g + 1]
            o = jnp.where(half[0], o0, o1)
            z_ref[pl.ds(r0, ATT_BLK), cols] = (o * sa_ref[pl.ds(r0, ATT_BLK), cols].astype(F32)).astype(BF16)
        return carry

    lax.fori_loop(0, nsub, sub_body, 0)
    y_ref[...] = _merge_norm(x_ref[...], gate_ref[...], z_ref[...], zp_ref[...], wo_ref, fg_ref[...], a_w)


def _prompt_attention(tot, qkb, q, cq, ck, kb, vb, sa, zp, x2, gate, wo, fg, *, tm, n_heads):
    rows, d = x2.shape
    a_w = n_heads * HEAD_DIM
    assert rows % tm == 0 and tm % ATT_BLK == 0 and rows // ATT_BLK >= NEAR_BLOCKS
    row_blk = lambda w: pl.BlockSpec((tm, w), lambda i, *_: (i, 0))
    grid_spec = pltpu.PrefetchScalarGridSpec(
        num_scalar_prefetch=2,
        grid=(rows // tm,),
        in_specs=[row_blk(a_w), row_blk(n_heads), _resident(ck.shape), _resident(kb.shape), _resident(vb.shape),
                  row_blk(a_w), row_blk(zp.shape[1]), row_blk(d), _resident((1, d)), _resident(wo.shape),
                  _resident((1, d))],
        out_specs=row_blk(d),
        scratch_shapes=[pltpu.VMEM((tm, a_w), BF16),
                        pltpu.VMEM((n_heads, ATT_BLK, NEAR_BLOCKS * ATT_BLK), F32),
                        pltpu.VMEM((n_heads, ATT_BLK, NEAR_BLOCKS * ATT_BLK), BF16),
                        pltpu.VMEM((n_heads, ATT_BLK, 1), F32),
                        pltpu.VMEM((n_heads, ATT_BLK, LANES), F32),
                        pltpu.VMEM((n_heads, ATT_BLK, LANES), F32)],
    )
    return pl.pallas_call(
        functools.partial(_attn_kernel, tm=tm, n_heads=n_heads),
        grid_spec=grid_spec,
        out_shape=jax.ShapeDtypeStruct((rows, d), F32),
        compiler_params=pltpu.CompilerParams(dimension_semantics=("arbitrary",), vmem_limit_bytes=VMEM_LIMIT),
        name="attn",
    )(tot, qkb, q, cq, ck, kb, vb, sa, zp, x2, gate, wo, fg)


def _sattn_kernel(q_ref, cq_ref, ckn_ref, kn_ref, vn_ref, kc_ref, vc_ref, lfc_ref, sa_ref, zp_ref, x_ref,
                  gate_ref, wo_ref, fg_ref, y_ref, *, n_heads):
    a_w = n_heads * HEAD_DIM
    ln = q_ref.shape[0]
    past = kc_ref.shape[1]
    nb = past // LANES

    lfc = lfc_ref[0]
    triu = (lax.broadcasted_iota(jnp.int32, (LANES, LANES), 0)
            <= lax.broadcasted_iota(jnp.int32, (LANES, LANES), 1)).astype(BF16)
    zeros = jnp.zeros((8, LANES), F32)
    parts = []
    for b in range(nb):
        parts.extend(_split3(lfc[:, b * LANES:(b + 1) * LANES]))
        parts.append(zeros)
    cs = _dot(jnp.concatenate(parts, axis=0).astype(BF16), triu)
    after = jnp.zeros((n_heads, 1), F32)
    suffix = [None] * nb
    for b in reversed(range(nb)):
        cb = cs[32 * b:32 * b + 8] + cs[32 * b + 8:32 * b + 16] + cs[32 * b + 16:32 * b + 24]
        tot = cb[:, LANES - 1:LANES]
        suffix[b] = (tot - cb) + after
        after = after + tot
    dec_c = jnp.concatenate(suffix, axis=1)

    lane = lax.broadcasted_iota(jnp.int32, (ln, LANES), 1)
    half = [lane < HEAD_DIM, lane >= HEAD_DIM]
    causal = lax.broadcasted_iota(jnp.int32, (ln, ln), 1) <= lax.broadcasted_iota(jnp.int32, (ln, ln), 0)
    outs = []
    for g in range(n_heads // PAIR):
        cols = slice(g * LANES, (g + 1) * LANES)
        q2 = q_ref[:, cols]
        kc = kc_ref[0, :, cols].astype(BF16)
        vc = vc_ref[0, :, cols].astype(BF16)
        kn = kn_ref[:, cols]
        vn = vn_ref[:, cols]
        o_pair = []
        for e in range(PAIR):
            hd = PAIR * g + e
            qm = jnp.where(half[e], q2, jnp.zeros_like(q2))
            cqh = cq_ref[:, hd:hd + 1]
            s_c = _dot_nt(qm, kc) + cqh + dec_c[hd:hd + 1, :]
            s_n = jnp.where(causal, _dot_nt(qm, kn) + cqh - ckn_ref[0, hd:hd + 1, :], NEG)
            m = jnp.maximum(jnp.max(s_c, axis=1, keepdims=True), jnp.max(s_n, axis=1, keepdims=True))
            p_c = jnp.exp(s_c - m)
            p_n = jnp.exp(s_n - m)
            l = jnp.sum(p_c, axis=1, keepdims=True) + jnp.sum(p_n, axis=1, keepdims=True)
            o_pair.append((_dot(p_c.astype(BF16), vc) + _dot(p_n.astype(BF16), vn)) / l)
        o = jnp.where(half[0], o_pair[0], o_pair[1])
        outs.append((o * sa_ref[:, cols].astype(F32)).astype(BF16))
    za = jnp.concatenate(outs, axis=1)
    y_ref[...] = _merge_norm(x_ref[...], gate_ref[0], za, zp_ref[...], wo_ref, fg_ref[...], a_w)


def _sample_attention(q, cq, ckn, kb, vb, cache_k, cache_v, lfc, sa, zp, x2, gate, wo, fg, *, ln, n_heads):
    rows, d = x2.shape
    nbatch = rows // ln
    a_w = n_heads * HEAD_DIM
    past = cache_k.shape[1]
    row_blk = lambda w: pl.BlockSpec((ln, w), lambda b: (b, 0))
    per_b = lambda s: pl.BlockSpec((1,) + s, lambda b: (b, 0, 0))
    return pl.pallas_call(
        functools.partial(_sattn_kernel, n_heads=n_heads),
        grid=(nbatch,),
        in_specs=[row_blk(a_w), row_blk(n_heads), per_b((n_heads, ln)), row_blk(a_w), row_blk(a_w),
                  per_b((past, a_w)), per_b((past, a_w)), per_b((n_heads, past)),
                  row_blk(a_w), row_blk(zp.shape[1]), row_blk(d), per_b((1, d)), _resident(wo.shape),
                  _resident((1, d))],
        out_specs=row_blk(d),
        out_shape=jax.ShapeDtypeStruct((rows, d), F32),
        compiler_params=pltpu.CompilerParams(dimension_semantics=("arbitrary",), vmem_limit_bytes=VMEM_LIMIT),
        name="sattn",
    )(q, cq, ckn, kb, vb, cache_k, cache_v, lfc, sa, zp, x2, gate, wo, fg)


def kernel(x_prompt, x_sample, c_prompt, c_sample, cache_k, cache_v, cache_logf, state_pool, norm_g, w_ada, b_ada,
           w_in, b_f, w_pool, pool_scale, w_out, final_g):
    depth = norm_g.shape[0]
    assert depth == 1
    bp, seq, d = x_prompt.shape
    bs, ln, _ = x_sample.shape
    assert bp == 1
    n_heads = cache_k.shape[3]
    past = cache_k.shape[2]
    a_w = n_heads * HEAD_DIM
    pw = state_pool.shape[3]
    assert pw == len(POOL_WINDOWS) * LANES and cache_k.shape[4] == HEAD_DIM and n_heads <= 8

    n_c = bp + bs
    c_all = jnp.concatenate([c_prompt, c_sample, jnp.zeros((16 - n_c, d), F32)], axis=0)
    m_all = _ada_terms(c_all, w_ada[0], b_ada[0][None, :])
    shift, scale, gate = m_all[:, 0:d], m_all[:, d:2 * d], m_all[:, 2 * d:3 * d]
    s1 = (1.0 + scale)[:, None, :]
    sh = shift[:, None, :]

    wi = w_in[0]
    wq = wi[:, 0:4 * a_w].astype(BF16)
    wf = jnp.pad(wi[:, 4 * a_w:4 * a_w + n_heads], ((0, 0), (0, LANES - n_heads))).astype(BF16)
    wu = wi[:, 4 * a_w + n_heads:].astype(BF16)
    bfp = jnp.pad(b_f[0][None, :], ((0, 0), (0, LANES - n_heads)))
    wp = w_pool[0].astype(BF16)
    ps = pool_scale[0][None, :]
    wo = w_out[0].astype(BF16)
    ng = norm_g[0][None, :]
    fg = final_g[None, :]

    bm = 512
    xp2 = x_prompt.reshape(seq, d)
    hist_p = jnp.zeros((1, HIST_PAD, pw), F32)
    (q_p, k_p, v_p, kb_p, vb_p, sa_p, zp_p, lf_p, cq_p, ck_p, st_p, ho_p) = _project(
        xp2, s1[0:1], sh[0:1], ng, wq, wf, bfp, wu, wp, ps, hist_p,
        bm=bm, sb=ATT_BLK, steps_per_stream=seq // bm, start_pos=0, n_heads=n_heads)
    nsb = bm // ATT_BLK
    st = st_p[:, :, 0:nsb, 0:n_heads]
    tot = st[:, 0].reshape(-1)
    qn = st[:, 1].reshape(-1, n_heads)
    kn = st[:, 2].reshape(-1, n_heads)
    qkb = (2.0 * qn * jnp.max(kn, axis=0, keepdims=True)).reshape(-1)
    y_p = _prompt_attention(tot, qkb, q_p, cq_p, ck_p, kb_p, vb_p, sa_p, zp_p, xp2, gate[0:1], wo, fg,
                            tm=512, n_heads=n_heads)

    xs2 = x_sample.reshape(bs * ln, d)
    hist_s = jnp.pad(state_pool[0], ((0, 0), (HIST_PAD - POOL_HIST, 0), (0, 0)))
    (q_s, k_s, v_s, kb_s, vb_s, sa_s, zp_s, lf_s, cq_s, ck_s, _, ho_s) = _project(
        xs2, s1[bp:n_c], sh[bp:n_c], ng, wq, wf, bfp, wu, wp, ps, hist_s,
        bm=ln, sb=ln, steps_per_stream=1, start_pos=past, n_heads=n_heads)
    lfc = jnp.swapaxes(cache_logf[0], 1, 2)
    y_s = _sample_attention(q_s, cq_s, ck_s, kb_s, vb_s, cache_k[0].reshape(bs, past, a_w),
                            cache_v[0].reshape(bs, past, a_w), lfc, sa_s, zp_s, xs2, gate[bp:n_c, None, :], wo, fg,
                            ln=ln, n_heads=n_heads)

    hd = (n_heads, HEAD_DIM)
    return (y_p.reshape(bp, seq, d), y_s.reshape(bs, ln, d),
            k_p.reshape((1, bp, seq) + hd), v_p.reshape((1, bp, seq) + hd), lf_p.reshape(1, bp, seq, n_heads),
            ho_p[:, 16 - POOL_HIST:, :][None],
            k_s.reshape((1, bs, ln) + hd), v_s.reshape((1, bs, ln) + hd), lf_s.reshape(1, bs, ln, n_heads),
            ho_s[:, 16 - POOL_HIST:, :][None])
```

```python
import functools

import jax
import jax.numpy as jnp
from jax import lax
from jax.experimental import pallas as pl
from jax.experimental.pallas import tpu as pltpu

HEAD_DIM = 64
POOL_WINDOWS = (2, 4, 8, 16)
EPS = 1e-6

LANES = 128
PAIR = LANES // HEAD_DIM
ATT_BLK = 128
NEAR_BLOCKS = 3
EXP_UNDERFLOW = 104.0
NORM_SLACK = 1.01
HIST_PAD = 32
POOL_HIST = max(POOL_WINDOWS) - 1
NEG = -1e30
VMEM_LIMIT = 60 * 1024 * 1024

F32 = jnp.float32
BF16 = jnp.bfloat16


def _silu(x):
    return x * jax.nn.sigmoid(x)


def _dot(a, b):
    return jnp.dot(a, b, preferred_element_type=F32)


def _dot_nt(a, b):
    return lax.dot_general(a, b, (((1,), (1,)), ((), ())), preferred_element_type=F32)


def _split3(x):
    hi = x.astype(BF16).astype(F32)
    r1 = x - hi
    mid = r1.astype(BF16).astype(F32)
    return hi, mid, r1 - mid


def _rows_to_lanes(x, n):
    rows = x.shape[0]
    if rows < LANES:
        x = jnp.concatenate([x, jnp.zeros((LANES - rows, LANES), x.dtype)], axis=0)
    return x.T[0:n, 0:rows]


def _resident(shape):
    return pl.BlockSpec(shape, lambda *_: (0,) * len(shape), pipeline_mode=pl.Buffered(1))


def _ada_kernel(c_ref, w_ref, b_ref, o_ref):
    a = _silu(c_ref[...]).astype(BF16)
    o_ref[...] = _dot(a, w_ref[...].astype(BF16)) + b_ref[...]


def _ada_terms(c_all, w_ada, b_ada):
    rows, d = c_all.shape
    n = w_ada.shape[1]
    bn = 1024
    return pl.pallas_call(
        _ada_kernel,
        grid=(n // bn,),
        in_specs=[pl.BlockSpec((rows, d), lambda j: (0, 0)),
                  pl.BlockSpec((d, bn), lambda j: (0, j)),
                  pl.BlockSpec((1, bn), lambda j: (0, j))],
        out_specs=pl.BlockSpec((rows, bn), lambda j: (0, j)),
        out_shape=jax.ShapeDtypeStruct((rows, n), F32),
        compiler_params=pltpu.CompilerParams(dimension_semantics=("arbitrary",), vmem_limit_bytes=VMEM_LIMIT),
        name="ada",
    )(c_all, w_ada, b_ada)


def _proj_kernel(x_ref, s1_ref, sh_ref, ng_ref, wq_ref, wf_ref, bf_ref, wu_ref, wp_ref, ps_ref, h0_ref,
                 q_ref, k32_ref, v32_ref, kb_ref, vb_ref, sa_ref, zp_ref, lf_ref, cq_ref, ck_ref, st_ref, ho_ref,
                 e_ref, t2_ref, t4_ref, t8_ref, *, bm, sb, steps_per_stream, start_pos, n_heads):
    a_w = n_heads * HEAD_DIM
    sis = pl.program_id(0) % steps_per_stream
    x = x_ref[...]
    ms = jnp.mean(x * x, axis=-1, keepdims=True)
    xn = x * lax.rsqrt(ms + EPS) * ng_ref[...]
    h = (xn * s1_ref[0] + sh_ref[0]).astype(BF16)

    sc = 1.0 / (HEAD_DIM ** 0.5)
    qs = _dot_nt(h, wq_ref[0:a_w, :]) * sc
    q_ref[...] = qs.astype(BF16)
    pk = _dot_nt(h, wq_ref[a_w:2 * a_w, :])
    k32_ref[...] = pk
    kb_ref[...] = pk.astype(BF16)
    pv = _dot_nt(h, wq_ref[2 * a_w:3 * a_w, :])
    v32_ref[...] = pv
    vb_ref[...] = pv.astype(BF16)
    pa = _dot_nt(h, wq_ref[3 * a_w:4 * a_w, :])
    sa_ref[...] = _silu(pa).astype(BF16)

    z = _dot_nt(h, wf_ref[...]) + bf_ref[...]
    lf = jnp.minimum(z, 0.0) - jnp.log1p(jnp.exp(-jnp.abs(z)))

    sel = (lax.broadcasted_iota(jnp.int32, (a_w, LANES), 0) // HEAD_DIM
           == lax.broadcasted_iota(jnp.int32, (a_w, LANES), 1)).astype(BF16)
    nq = jnp.sqrt(_dot((qs * qs).astype(BF16), sel)) * NORM_SLACK
    nk = jnp.sqrt(_dot((pk * pk).astype(BF16), sel)) * NORM_SLACK

    tri = (lax.broadcasted_iota(jnp.int32, (sb, sb), 1)
           <= lax.broadcasted_iota(jnp.int32, (sb, sb), 0)).astype(BF16)
    tots, qmx, kmx = [], [], []
    for s in range(bm // sb):
        rows = slice(s * sb, (s + 1) * sb)
        hi, mid, lo = _split3(lf[rows])
        cb = _dot(tri, hi.astype(BF16)) + _dot(tri, mid.astype(BF16)) + _dot(tri, lo.astype(BF16))
        cq_ref[rows, :] = cb[:, 0:n_heads]
        ck_ref[s] = _rows_to_lanes(cb, n_heads)
        lf_ref[0, :, rows] = _rows_to_lanes(lf[rows], n_heads)
        tots.append(cb[sb - 1:sb, :])
        qmx.append(jnp.max(nq[rows], axis=0, keepdims=True))
        kmx.append(jnp.max(nk[rows], axis=0, keepdims=True))
    pad = [jnp.zeros((8 - bm // sb, LANES), F32)] if bm // sb < 8 else []
    st_ref[0, 0] = jnp.concatenate(tots + pad, axis=0)
    st_ref[0, 1] = jnp.concatenate(qmx + pad, axis=0)
    st_ref[0, 2] = jnp.concatenate(kmx + pad, axis=0)

    pw = len(POOL_WINDOWS) * LANES
    pu = _dot_nt(h, wu_ref[0:pw, :])
    spg = _silu(_dot_nt(h, wu_ref[pw:2 * pw, :]))

    @pl.when(sis == 0)
    def _():
        e_ref[0:HIST_PAD, :] = h0_ref[0]

    n = bm + HIST_PAD
    e_ref[HIST_PAD:n, :] = pu
    t2_ref[8:n, :] = e_ref[8:n, :] + e_ref[7:n - 1, :]
    t4_ref[16:n, :] = t2_ref[16:n, LANES:] + t2_ref[14:n - 2, LANES:]
    t8_ref[24:n, :] = t4_ref[24:n, LANES:] + t4_ref[20:n - 4, LANES:]
    s16 = t8_ref[32:n, LANES:] + t8_ref[24:n - 8, LANES:]
    sums = [t2_ref[HIST_PAD:n, 0:LANES], t4_ref[HIST_PAD:n, 0:LANES], t8_ref[HIST_PAD:n, 0:LANES], s16]
    pos1 = start_pos + sis * bm + lax.broadcasted_iota(jnp.int32, (bm, 1), 0) + 1
    for g, w in enumerate(POOL_WINDOWS):
        cols = slice(g * LANES, (g + 1) * LANES)
        rc = 1.0 / jnp.minimum(pos1, w).astype(F32)
        d = sums[g] * rc - pu[:, cols]
        y = _dot(d.astype(BF16), wp_ref[g]) * ps_ref[:, cols]
        zp_ref[:, cols] = (y * spg[:, cols]).astype(BF16)
    ho_ref[0] = e_ref[n - 16:n, :]
    e_ref[0:HIST_PAD, :] = e_ref[bm:n, :]


def _project(x2, s1, sh, norm_g, wq, wf, bfp, wu, wp, ps, hist0, *, bm, sb, steps_per_stream, start_pos, n_heads):
    rows, d = x2.shape
    a_w = n_heads * HEAD_DIM
    pw = len(POOL_WINDOWS) * LANES
    n_steps = rows // bm
    n_streams = n_steps // steps_per_stream
    nsb = bm // sb
    stream = lambda i: i // steps_per_stream
    row_blk = lambda w: pl.BlockSpec((bm, w), lambda i: (i, 0))
    kern = functools.partial(_proj_kernel, bm=bm, sb=sb, steps_per_stream=steps_per_stream,
                             start_pos=start_pos, n_heads=n_heads)
    out_shape = (
        jax.ShapeDtypeStruct((rows, a_w), BF16),
        jax.ShapeDtypeStruct((rows, a_w), F32),
        jax.ShapeDtypeStruct((rows, a_w), F32),
        jax.ShapeDtypeStruct((rows, a_w), BF16),
        jax.ShapeDtypeStruct((rows, a_w), BF16),
        jax.ShapeDtypeStruct((rows, a_w), BF16),
        jax.ShapeDtypeStruct((rows, pw), BF16),
        jax.ShapeDtypeStruct((n_streams, n_heads, rows // n_streams), F32),
        jax.ShapeDtypeStruct((rows, n_heads), F32),
        jax.ShapeDtypeStruct((rows // sb, n_heads, sb), F32),
        jax.ShapeDtypeStruct((n_steps, 3, 8, LANES), F32),
        jax.ShapeDtypeStruct((n_streams, 16, pw), F32),
    )
    out_specs = (
        row_blk(a_w), row_blk(a_w), row_blk(a_w), row_blk(a_w), row_blk(a_w), row_blk(a_w), row_blk(pw),
        pl.BlockSpec((1, n_heads, bm), lambda i: (stream(i), 0, i % steps_per_stream)),
        row_blk(n_heads),
        pl.BlockSpec((nsb, n_heads, sb), lambda i: (i, 0, 0)),
        pl.BlockSpec((1, 3, 8, LANES), lambda i: (i, 0, 0, 0)),
        pl.BlockSpec((1, 16, pw), lambda i: (stream(i), 0, 0)),
    )
    in_specs = [
        row_blk(d),
        pl.BlockSpec((1, 1, d), lambda i: (stream(i), 0, 0)),
        pl.BlockSpec((1, 1, d), lambda i: (stream(i), 0, 0)),
        _resident((1, d)),
        _resident(wq.shape), _resident(wf.shape), _resident(bfp.shape), _resident(wu.shape),
        _resident(wp.shape), _resident(ps.shape),
        pl.BlockSpec((1, HIST_PAD, pw), lambda i: (stream(i), 0, 0)),
    ]
    return pl.pallas_call(
        kern,
        grid=(n_steps,),
        in_specs=in_specs,
        out_specs=out_specs,
        out_shape=out_shape,
        scratch_shapes=[pltpu.VMEM((bm + HIST_PAD, pw), F32),
                        pltpu.VMEM((bm + HIST_PAD, pw), F32),
                        pltpu.VMEM((bm + HIST_PAD, pw - LANES), F32),
                        pltpu.VMEM((bm + HIST_PAD, pw - 2 * LANES), F32)],
        compiler_params=pltpu.CompilerParams(dimension_semantics=("arbitrary",), vmem_limit_bytes=VMEM_LIMIT),
        name="proj",
    )(x2, s1, sh, norm_g, wq, wf, bfp, wu, wp, ps, hist0)


def _merge_norm(x, gate, za, zp, wo_ref, fg, a_w):
    dy = _dot(za, wo_ref[0:a_w, :]) + _dot(zp, wo_ref[a_w:, :])
    out = x + gate * dy
    ms = jnp.mean(out * out, axis=-1, keepdims=True)
    return out * lax.rsqrt(ms + EPS) * fg


def _attn_kernel(tot_ref, qkb_ref,
                 q_ref, cq_ref, ck_ref, k_ref, v_ref, sa_ref, zp_ref, x_ref, gate_ref, wo_ref, fg_ref,
                 y_ref,
                 z_ref, s_ref, p_ref, m_ref, l_ref, acc_ref, *, tm, n_heads):
    a_w = n_heads * HEAD_DIM
    nsub = tm // ATT_BLK
    w_near = NEAR_BLOCKS
    cw = w_near * ATT_BLK
    step = pl.program_id(0)
    lane = lax.broadcasted_iota(jnp.int32, (ATT_BLK, LANES), 1)
    half = [lane < HEAD_DIM, lane >= HEAD_DIM]
    col = lax.broadcasted_iota(jnp.int32, (ATT_BLK, cw), 1)

    def tot_at(b, hd):
        return jnp.where(b >= 0, tot_ref[jnp.maximum(b, 0) * n_heads + hd], 0.0)

    def q_masked(r0, hd):
        q2 = q_ref[pl.ds(r0, ATT_BLK), (hd // PAIR) * LANES:(hd // PAIR + 1) * LANES]
        return jnp.where(half[hd % PAIR], q2, jnp.zeros_like(q2))

    def rows_of(ref, blocks, hd):
        g = hd // PAIR
        return jnp.concatenate(
            [ref[pl.ds(pl.multiple_of(b * ATT_BLK, ATT_BLK), ATT_BLK), g * LANES:(g + 1) * LANES] for b in blocks],
            axis=0)

    def scores(r0, hd, j0, offs):
        s = _dot_nt(q_masked(r0, hd), rows_of(k_ref, [j0 + p for p in range(w_near)], hd))
        dec = jnp.concatenate([offs[p] - ck_ref[j0 + p, hd:hd + 1, :] for p in range(w_near)], axis=1)
        return s + cq_ref[pl.ds(r0, ATT_BLK), hd:hd + 1] + dec

    def values(hd, j0):
        return rows_of(v_ref, [j0 + p for p in range(w_near)], hd)

    tri = (lax.broadcasted_iota(jnp.int32, (ATT_BLK, ATT_BLK), 1)
           <= lax.broadcasted_iota(jnp.int32, (ATT_BLK, ATT_BLK), 0))
    tri_bias = jnp.where(tri, 0.0, NEG).astype(F32)
    ones_v = jnp.ones((cw, LANES), BF16)

    def sub_body(sub, carry):
        i = step * nsub + sub
        r0 = pl.multiple_of(sub * ATT_BLK, ATT_BLK)

        near = [i - (w_near - 1) + p for p in range(w_near)]
        near_c = [jnp.maximum(b, 0) for b in near]
        offs_far = []
        for hd in range(n_heads):
            back = [tot_at(i - dd, hd) for dd in range(1, w_near)]
            s = _dot_nt(q_masked(r0, hd), rows_of(k_ref, near_c, hd))
            cqh = cq_ref[pl.ds(r0, ATT_BLK), hd:hd + 1]
            pieces = []
            for p in range(w_near):
                o = jnp.float32(0.0)
                for dd in range(1, w_near - p):
                    o = o + back[dd - 1]
                o = jnp.where(near[p] >= 0, o, NEG)
                sp = s[:, p * ATT_BLK:(p + 1) * ATT_BLK] + cqh + (o - ck_ref[near_c[p], hd:hd + 1, :])
                pieces.append(sp + tri_bias if p == w_near - 1 else sp)
            s_ref[hd] = jnp.concatenate(pieces, axis=1)
            o = jnp.float32(0.0)
            for t in back:
                o = o + t
            offs_far.append(o)
        for hd in range(n_heads):
            s = s_ref[hd]
            m = jnp.max(s, axis=1, keepdims=True)
            m_ref[hd] = m
            p_ref[hd] = jnp.exp(s - m).astype(BF16)
        for hd in range(n_heads):
            r = _dot(p_ref[hd], jnp.concatenate([rows_of(v_ref, near_c, hd), ones_v], axis=1))
            acc_ref[hd] = r[:, 0:LANES]
            l_ref[hd] = r[:, LANES:]

        qk = [qkb_ref[i * n_heads + hd] for hd in range(n_heads)]

        def far_cond(c):
            top = i - c[0] * w_near
            need = qk[0] + c[1] >= -EXP_UNDERFLOW
            for hd in range(1, n_heads):
                need = jnp.logical_or(need, qk[hd] + c[1 + hd] >= -EXP_UNDERFLOW)
            return jnp.logical_and(top >= 0, need)

        def far_body(c):
            top = i - c[0] * w_near
            jc = jnp.maximum(top - (w_near - 1), 0)
            keepc = col < (top + 1 - jc) * ATT_BLK
            new = [c[0] + 1]
            for hd in range(n_heads):
                tt = [tot_at(top - e, hd) for e in range(w_near)]
                offs = []
                for p in range(w_near):
                    behind = top - (jc + p)
                    o = c[1 + hd]
                    for e in range(w_near):
                        o = o + jnp.where(behind >= e, tt[e], 0.0)
                    offs.append(o)
                s = jnp.where(keepc, scores(r0, hd, jc, offs), NEG)
                m_old = m_ref[hd]
                m_new = jnp.maximum(m_old, jnp.max(s, axis=1, keepdims=True))
                alpha = jnp.exp(m_old - m_new)
                p_ = jnp.exp(s - m_new)
                m_ref[hd] = m_new
                l_ref[hd] = alpha * l_ref[hd] + jnp.sum(p_, axis=1, keepdims=True)
                acc_ref[hd] = alpha * acc_ref[hd] + _dot(p_.astype(BF16), values(hd, jc))
                o = c[1 + hd]
                for t in tt:
                    o = o + t
                new.append(o)
            return tuple(new)

        lax.while_loop(far_cond, far_body, (jnp.int32(1),) + tuple(offs_far))

        for g in range(n_heads // PAIR):
            cols = slice(g * LANES, (g + 1) * LANES)
            o0 = acc_ref[PAIR * g] / l_ref[PAIR * g]
            o1 = acc_ref[PAIR * g + 1] / l_ref[PAIR * g + 1]
            o = jnp.where(half[0], o0, o1)
            z_ref[pl.ds(r0, ATT_BLK), cols] = (o * sa_ref[pl.ds(r0, ATT_BLK), cols].astype(F32)).astype(BF16)
        return carry

    lax.fori_loop(0, nsub, sub_body, 0)
    y_ref[...] = _merge_norm(x_ref[...], gate_ref[...], z_ref[...], zp_ref[...], wo_ref, fg_ref[...], a_w)


def _prompt_attention(tot, qkb, q, cq, ck, kb, vb, sa, zp, x2, gate, wo, fg, *, tm, n_heads):
    rows, d = x2.shape
    a_w = n_heads * HEAD_DIM
    assert rows % tm == 0 and tm % ATT_BLK == 0 and rows // ATT_BLK >= NEAR_BLOCKS
    row_blk = lambda w: pl.BlockSpec((tm, w), lambda i, *_: (i, 0))
    grid_spec = pltpu.PrefetchScalarGridSpec(
        num_scalar_prefetch=2,
        grid=(rows // tm,),
        in_specs=[row_blk(a_w), row_blk(n_heads), _resident(ck.shape), _resident(kb.shape), _resident(vb.shape),
                  row_blk(a_w), row_blk(zp.shape[1]), row_blk(d), _resident((1, d)), _resident(wo.shape),
                  _resident((1, d))],
        out_specs=row_blk(d),
        scratch_shapes=[pltpu.VMEM((tm, a_w), BF16),
                        pltpu.VMEM((n_heads, ATT_BLK, NEAR_BLOCKS * ATT_BLK), F32),
                        pltpu.VMEM((n_heads, ATT_BLK, NEAR_BLOCKS * ATT_BLK), BF16),
                        pltpu.VMEM((n_heads, ATT_BLK, 1), F32),
                        pltpu.VMEM((n_heads, ATT_BLK, LANES), F32),
                        pltpu.VMEM((n_heads, ATT_BLK, LANES), F32)],
    )
    return pl.pallas_call(
        functools.partial(_attn_kernel, tm=tm, n_heads=n_heads),
        grid_spec=grid_spec,
        out_shape=jax.ShapeDtypeStruct((rows, d), F32),
        compiler_params=pltpu.CompilerParams(dimension_semantics=("arbitrary",), vmem_limit_bytes=VMEM_LIMIT),
        name="attn",
    )(tot, qkb, q, cq, ck, kb, vb, sa, zp, x2, gate, wo, fg)


def _sattn_kernel(q_ref, cq_ref, ckn_ref, kn_ref, vn_ref, kc_ref, vc_ref, lfc_ref, sa_ref, zp_ref, x_ref,
                  gate_ref, wo_ref, fg_ref, y_ref, *, n_heads):
    a_w = n_heads * HEAD_DIM
    ln = q_ref.shape[0]
    past = kc_ref.shape[2]
    nb = past // LANES

    lfc = lfc_ref[0]
    triu = (lax.broadcasted_iota(jnp.int32, (LANES, LANES), 0)
            <= lax.broadcasted_iota(jnp.int32, (LANES, LANES), 1)).astype(BF16)
    zeros = jnp.zeros((8, LANES), F32)
    parts = []
    for b in range(nb):
        parts.extend(_split3(lfc[:, b * LANES:(b + 1) * LANES]))
        parts.append(zeros)
    cs = _dot(jnp.concatenate(parts, axis=0).astype(BF16), triu)
    after = jnp.zeros((n_heads, 1), F32)
    suffix = [None] * nb
    for b in reversed(range(nb)):
        cb = cs[32 * b:32 * b + 8] + cs[32 * b + 8:32 * b + 16] + cs[32 * b + 16:32 * b + 24]
        tot = cb[:, LANES - 1:LANES]
        suffix[b] = (tot - cb) + after
        after = after + tot
    dec_c = jnp.concatenate(suffix, axis=1)

    lane = lax.broadcasted_iota(jnp.int32, (ln, LANES), 1)
    half = [lane < HEAD_DIM, lane >= HEAD_DIM]
    causal = lax.broadcasted_iota(jnp.int32, (ln, ln), 1) <= lax.broadcasted_iota(jnp.int32, (ln, ln), 0)
    outs = []
    for g in range(n_heads // PAIR):
        cols = slice(g * LANES, (g + 1) * LANES)
        q2 = q_ref[:, cols]
        kct = kc_ref[0, cols, :].astype(BF16)
        vct = vc_ref[0, cols, :].astype(BF16)
        kn = kn_ref[:, cols]
        vn = vn_ref[:, cols]
        o_pair = []
        for e in range(PAIR):
            hd = PAIR * g + e
            qm = jnp.where(half[e], q2, jnp.zeros_like(q2))
            cqh = cq_ref[:, hd:hd + 1]
            s_c = _dot(qm, kct) + cqh + dec_c[hd:hd + 1, :]
            s_n = jnp.where(causal, _dot_nt(qm, kn) + cqh - ckn_ref[0, hd:hd + 1, :], NEG)
            m = jnp.maximum(jnp.max(s_c, axis=1, keepdims=True), jnp.max(s_n, axis=1, keepdims=True))
            p_c = jnp.exp(s_c - m)
            p_n = jnp.exp(s_n - m)
            l = jnp.sum(p_c, axis=1, keepdims=True) + jnp.sum(p_n, axis=1, keepdims=True)
            o_pair.append((_dot_nt(p_c.astype(BF16), vct) + _dot(p_n.astype(BF16), vn)) / l)
        o = jnp.where(half[0], o_pair[0], o_pair[1])
        outs.append((o * sa_ref[:, cols].astype(F32)).astype(BF16))
    za = jnp.concatenate(outs, axis=1)
    y_ref[...] = _merge_norm(x_ref[...], gate_ref[0], za, zp_ref[...], wo_ref, fg_ref[...], a_w)


def _sample_attention(q, cq, ckn, kb, vb, cache_k, cache_v, lfc, sa, zp, x2, gate, wo, fg, *, ln, n_heads):
    rows, d = x2.shape
    nbatch = rows // ln
    a_w = n_heads * HEAD_DIM
    past = cache_k.shape[2]
    row_blk = lambda w: pl.BlockSpec((ln, w), lambda b: (b, 0))
    per_b = lambda s: pl.BlockSpec((1,) + s, lambda b: (b, 0, 0))
    return pl.pallas_call(
        functools.partial(_sattn_kernel, n_heads=n_heads),
        grid=(nbatch,),
        in_specs=[row_blk(a_w), row_blk(n_heads), per_b((n_heads, ln)), row_blk(a_w), row_blk(a_w),
                  per_b((a_w, past)), per_b((a_w, past)), per_b((n_heads, past)),
                  row_blk(a_w), row_blk(zp.shape[1]), row_blk(d), per_b((1, d)), _resident(wo.shape),
                  _resident((1, d))],
        out_specs=row_blk(d),
        out_shape=jax.ShapeDtypeStruct((rows, d), F32),
        compiler_params=pltpu.CompilerParams(dimension_semantics=("arbitrary",), vmem_limit_bytes=VMEM_LIMIT),
        name="sattn",
    )(q, cq, ckn, kb, vb, cache_k, cache_v, lfc, sa, zp, x2, gate, wo, fg)


def kernel(x_prompt, x_sample, c_prompt, c_sample, cache_k, cache_v, cache_logf, state_pool, norm_g, w_ada, b_ada,
           w_in, b_f, w_pool, pool_scale, w_out, final_g):
    depth = norm_g.shape[0]
    assert depth == 1
    bp, seq, d = x_prompt.shape
    bs, ln, _ = x_sample.shape
    assert bp == 1
    n_heads = cache_k.shape[3]
    past = cache_k.shape[2]
    a_w = n_heads * HEAD_DIM
    pw = state_pool.shape[3]
    assert pw == len(POOL_WINDOWS) * LANES and cache_k.shape[4] == HEAD_DIM and n_heads <= 8

    n_c = bp + bs
    c_all = jnp.concatenate([c_prompt, c_sample, jnp.zeros((16 - n_c, d), F32)], axis=0)
    m_all = _ada_terms(c_all, w_ada[0], b_ada[0][None, :])
    shift, scale, gate = m_all[:, 0:d], m_all[:, d:2 * d], m_all[:, 2 * d:3 * d]
    s1 = (1.0 + scale)[:, None, :]
    sh = shift[:, None, :]

    wit = w_in[0].T
    wq = wit[0:4 * a_w].astype(BF16)
    wf = jnp.pad(wit[4 * a_w:4 * a_w + n_heads], ((0, LANES - n_heads), (0, 0))).astype(BF16)
    wu = wit[4 * a_w + n_heads:].astype(BF16)
    bfp = jnp.pad(b_f[0][None, :], ((0, 0), (0, LANES - n_heads)))
    wp = w_pool[0].astype(BF16)
    ps = pool_scale[0][None, :]
    wo = w_out[0].astype(BF16)
    ng = norm_g[0][None, :]
    fg = final_g[None, :]

    bm = 512
    xp2 = x_prompt.reshape(seq, d)
    hist_p = jnp.zeros((1, HIST_PAD, pw), F32)
    (q_p, k_p, v_p, kb_p, vb_p, sa_p, zp_p, lf_p, cq_p, ck_p, st_p, ho_p) = _project(
        xp2, s1[0:1], sh[0:1], ng, wq, wf, bfp, wu, wp, ps, hist_p,
        bm=bm, sb=ATT_BLK, steps_per_stream=seq // bm, start_pos=0, n_heads=n_heads)
    nsb = bm // ATT_BLK
    st = st_p[:, :, 0:nsb, 0:n_heads]
    tot = st[:, 0].reshape(-1)
    qn = st[:, 1].reshape(-1, n_heads)
    kn = st[:, 2].reshape(-1, n_heads)
    qkb = (2.0 * qn * jnp.max(kn, axis=0, keepdims=True)).reshape(-1)
    y_p = _prompt_attention(tot, qkb, q_p, cq_p, ck_p, kb_p, vb_p, sa_p, zp_p, xp2, gate[0:1], wo, fg,
                            tm=512, n_heads=n_heads)

    xs2 = x_sample.reshape(bs * ln, d)
    hist_s = jnp.pad(state_pool[0], ((0, 0), (HIST_PAD - POOL_HIST, 0), (0, 0)))
    (q_s, k_s, v_s, kb_s, vb_s, sa_s, zp_s, lf_s, cq_s, ck_s, _, ho_s) = _project(
        xs2, s1[bp:n_c], sh[bp:n_c], ng, wq, wf, bfp, wu, wp, ps, hist_s,
        bm=ln, sb=ln, steps_per_stream=1, start_pos=past, n_heads=n_heads)
    lfc = jnp.swapaxes(cache_logf[0], 1, 2)
    ckt = jnp.transpose(cache_k[0], (0, 2, 3, 1)).reshape(bs, a_w, past)
    cvt = jnp.transpose(cache_v[0], (0, 2, 3, 1)).reshape(bs, a_w, past)
    y_s = _sample_attention(q_s, cq_s, ck_s, kb_s, vb_s, ckt, cvt, lfc, sa_s, zp_s, xs2, gate[bp:n_c, None, :], wo, fg,
                            ln=ln, n_heads=n_heads)

    hd = (n_heads, HEAD_DIM)
    return (y_p.reshape(bp, seq, d), y_s.reshape(bs, ln, d),
            k_p.reshape((1, bp, seq) + hd), v_p.reshape((1, bp, seq) + hd), jnp.swapaxes(lf_p, 1, 2)[None],
            ho_p[:, 16 - POOL_HIST:, :][None],
            k_s.reshape((1, bs, ln) + hd), v_s.reshape((1, bs, ln) + hd), jnp.swapaxes(lf_s, 1, 2)[None],
            ho_s[:, 16 - POOL_HIST:, :][None])
```

```python
import functools

import jax
import jax.numpy as jnp
from jax import lax
from jax.experimental import pallas as pl
from jax.experimental.pallas import tpu as pltpu

HEAD_DIM = 64
POOL_WINDOWS = (2, 4, 8, 16)
EPS = 1e-6

LANES = 128
PAIR = LANES // HEAD_DIM
ATT_BLK = 128
NEAR_BLOCKS = 3
EXP_UNDERFLOW = 104.0
NORM_SLACK = 1.01
HIST_PAD = 32
POOL_HIST = max(POOL_WINDOWS) - 1
NEG = -1e30
VMEM_LIMIT = 60 * 1024 * 1024

F32 = jnp.float32
BF16 = jnp.bfloat16


def _silu(x):
    return x * jax.nn.sigmoid(x)


def _dot(a, b):
    return jnp.dot(a, b, preferred_element_type=F32)


def _dot_nt(a, b):
    return lax.dot_general(a, b, (((1,), (1,)), ((), ())), preferred_element_type=F32)


def _split3(x):
    hi = x.astype(BF16).astype(F32)
    r1 = x - hi
    mid = r1.astype(BF16).astype(F32)
    return hi, mid, r1 - mid


def _rows_to_lanes(x, n):
    rows = x.shape[0]
    if rows < LANES:
        x = jnp.concatenate([x, jnp.zeros((LANES - rows, LANES), x.dtype)], axis=0)
    return x.T[0:n, 0:rows]


def _resident(shape):
    return pl.BlockSpec(shape, lambda *_: (0,) * len(shape), pipeline_mode=pl.Buffered(1))


def _ada_kernel(c_ref, w_ref, b_ref, o_ref):
    a = _silu(c_ref[...]).astype(BF16)
    o_ref[...] = _dot(a, w_ref[...].astype(BF16)) + b_ref[...]


def _ada_terms(c_all, w_ada, b_ada):
    rows, d = c_all.shape
    n = w_ada.shape[1]
    bn = 1024
    return pl.pallas_call(
        _ada_kernel,
        grid=(n // bn,),
        in_specs=[pl.BlockSpec((rows, d), lambda j: (0, 0)),
                  pl.BlockSpec((d, bn), lambda j: (0, j)),
                  pl.BlockSpec((1, bn), lambda j: (0, j))],
        out_specs=pl.BlockSpec((rows, bn), lambda j: (0, j)),
        out_shape=jax.ShapeDtypeStruct((rows, n), F32),
        compiler_params=pltpu.CompilerParams(dimension_semantics=("arbitrary",), vmem_limit_bytes=VMEM_LIMIT),
        name="ada",
    )(c_all, w_ada, b_ada)


def _proj_kernel(x_ref, s1_ref, sh_ref, ng_ref, wq_ref, wf_ref, bf_ref, wu_ref, wp_ref, ps_ref, h0_ref,
                 q_ref, k32_ref, v32_ref, kb_ref, vb_ref, sa_ref, zp_ref, lf_ref, cq_ref, ck_ref, st_ref, ho_ref,
                 e_ref, t2_ref, t4_ref, t8_ref, *, bm, sb, segs, start_pos, n_heads, kv_head_major):
    a_w = n_heads * HEAD_DIM
    seg_rows = bm // segs
    step = pl.program_id(0)
    x = x_ref[...]
    ms = jnp.mean(x * x, axis=-1, keepdims=True)
    xn = x * lax.rsqrt(ms + EPS) * ng_ref[...]
    h = jnp.concatenate([xn[g * seg_rows:(g + 1) * seg_rows] * s1_ref[g] + sh_ref[g] for g in range(segs)],
                        axis=0).astype(BF16)

    sc = 1.0 / (HEAD_DIM ** 0.5)
    qs = _dot_nt(h, wq_ref[0:a_w, :]) * sc
    q_ref[...] = qs.astype(BF16)
    pk = _dot_nt(h, wq_ref[a_w:2 * a_w, :])
    pv = _dot_nt(h, wq_ref[2 * a_w:3 * a_w, :])
    kb_ref[...] = pk.astype(BF16)
    vb_ref[...] = pv.astype(BF16)
    if kv_head_major:
        for hd in range(n_heads):
            k32_ref[:, hd, :] = pk[:, hd * HEAD_DIM:(hd + 1) * HEAD_DIM]
            v32_ref[:, hd, :] = pv[:, hd * HEAD_DIM:(hd + 1) * HEAD_DIM]
    else:
        k32_ref[...] = pk
        v32_ref[...] = pv
    pa = _dot_nt(h, wq_ref[3 * a_w:4 * a_w, :])
    sa_ref[...] = _silu(pa).astype(BF16)

    z = _dot_nt(h, wf_ref[...]) + bf_ref[...]
    lf = jnp.minimum(z, 0.0) - jnp.log1p(jnp.exp(-jnp.abs(z)))

    sel = (lax.broadcasted_iota(jnp.int32, (a_w, LANES), 0) // HEAD_DIM
           == lax.broadcasted_iota(jnp.int32, (a_w, LANES), 1)).astype(BF16)
    nq = jnp.sqrt(_dot((qs * qs).astype(BF16), sel)) * NORM_SLACK
    nk = jnp.sqrt(_dot((pk * pk).astype(BF16), sel)) * NORM_SLACK

    tri = (lax.broadcasted_iota(jnp.int32, (sb, sb), 1)
           <= lax.broadcasted_iota(jnp.int32, (sb, sb), 0)).astype(BF16)
    tots, qmx, kmx = [], [], []
    for s in range(bm // sb):
        rows = slice(s * sb, (s + 1) * sb)
        hi, mid, lo = _split3(lf[rows])
        cb = _dot(tri, hi.astype(BF16)) + _dot(tri, mid.astype(BF16)) + _dot(tri, lo.astype(BF16))
        cq_ref[rows, :] = cb[:, 0:n_heads]
        ck_ref[s] = _rows_to_lanes(cb, n_heads)
        off = (s * sb) % seg_rows
        lf_ref[(s * sb) // seg_rows, :, off:off + sb] = _rows_to_lanes(lf[rows], n_heads)
        tots.append(cb[sb - 1:sb, :])
        qmx.append(jnp.max(nq[rows], axis=0, keepdims=True))
        kmx.append(jnp.max(nk[rows], axis=0, keepdims=True))
    pad = [jnp.zeros((8 - bm // sb, LANES), F32)] if bm // sb < 8 else []
    st_ref[0, 0] = jnp.concatenate(tots + pad, axis=0)
    st_ref[0, 1] = jnp.concatenate(qmx + pad, axis=0)
    st_ref[0, 2] = jnp.concatenate(kmx + pad, axis=0)

    pw = len(POOL_WINDOWS) * LANES
    pu = _dot_nt(h, wu_ref[0:pw, :])
    spg = _silu(_dot_nt(h, wu_ref[pw:2 * pw, :]))

    ext = HIST_PAD + seg_rows
    n = segs * ext

    def load_history():
        for g in range(segs):
            e_ref[g * ext:g * ext + HIST_PAD, :] = h0_ref[g]

    if segs > 1:
        load_history()
    else:
        pl.when(step == 0)(load_history)

    for g in range(segs):
        e_ref[g * ext + HIST_PAD:(g + 1) * ext, :] = pu[g * seg_rows:(g + 1) * seg_rows]
    t2_ref[8:n, :] = e_ref[8:n, :] + e_ref[7:n - 1, :]
    t4_ref[16:n, :] = t2_ref[16:n, LANES:] + t2_ref[14:n - 2, LANES:]
    t8_ref[24:n, :] = t4_ref[24:n, LANES:] + t4_ref[20:n - 4, LANES:]

    def seg_rows_of(ref, cols, back=0):
        return jnp.concatenate([ref[g * ext + HIST_PAD - back:(g + 1) * ext - back, cols] for g in range(segs)], axis=0)

    lane0, lane1 = slice(0, LANES), slice(LANES, 2 * LANES)
    sums = [seg_rows_of(t2_ref, lane0), seg_rows_of(t4_ref, lane0), seg_rows_of(t8_ref, lane0),
            seg_rows_of(t8_ref, lane1) + seg_rows_of(t8_ref, lane1, back=8)]
    row = lax.broadcasted_iota(jnp.int32, (bm, 1), 0)
    pos1 = start_pos + 1 + (step * bm + row if segs == 1 else row % seg_rows)
    for g, w in enumerate(POOL_WINDOWS):
        cols = slice(g * LANES, (g + 1) * LANES)
        rc = 1.0 / jnp.minimum(pos1, w).astype(F32)
        d = sums[g] * rc - pu[:, cols]
        y = _dot(d.astype(BF16), wp_ref[g]) * ps_ref[:, cols]
        zp_ref[:, cols] = (y * spg[:, cols]).astype(BF16)
    for g in range(segs):
        ho_ref[g] = e_ref[(g + 1) * ext - 16:(g + 1) * ext, :]
    if segs == 1:
        e_ref[0:HIST_PAD, :] = e_ref[bm:n, :]


def _project(x2, s1, sh, norm_g, wq, wf, bfp, wu, wp, ps, hist0, *, bm, sb, segs, start_pos, n_heads,
             kv_head_major):
    rows, d = x2.shape
    a_w = n_heads * HEAD_DIM
    pw = len(POOL_WINDOWS) * LANES
    n_steps = rows // bm
    assert segs == 1 or n_steps == 1
    n_streams = segs
    seg_rows = bm // segs
    nsb = bm // sb
    assert nsb <= 8 and seg_rows % sb == 0
    row_blk = lambda w: pl.BlockSpec((bm, w), lambda i: (i, 0))
    per_stream = lambda r, w: pl.BlockSpec((segs, r, w), lambda i: (0, 0, 0))
    kern = functools.partial(_proj_kernel, bm=bm, sb=sb, segs=segs, start_pos=start_pos, n_heads=n_heads,
                             kv_head_major=kv_head_major)
    kv_shape = (rows, n_heads, HEAD_DIM) if kv_head_major else (rows, a_w)
    kv_blk = pl.BlockSpec((bm,) + kv_shape[1:], lambda i: (i,) + (0,) * (len(kv_shape) - 1))
    out_shape = (
        jax.ShapeDtypeStruct((rows, a_w), BF16),
        jax.ShapeDtypeStruct(kv_shape, F32),
        jax.ShapeDtypeStruct(kv_shape, F32),
        jax.ShapeDtypeStruct((rows, a_w), BF16),
        jax.ShapeDtypeStruct((rows, a_w), BF16),
        jax.ShapeDtypeStruct((rows, a_w), BF16),
        jax.ShapeDtypeStruct((rows, pw), BF16),
        jax.ShapeDtypeStruct((n_streams, n_heads, rows // n_streams), F32),
        jax.ShapeDtypeStruct((rows, n_heads), F32),
        jax.ShapeDtypeStruct((rows // sb, n_heads, sb), F32),
        jax.ShapeDtypeStruct((n_steps, 3, 8, LANES), F32),
        jax.ShapeDtypeStruct((n_streams, 16, pw), F32),
    )
    out_specs = (
        row_blk(a_w), kv_blk, kv_blk, row_blk(a_w), row_blk(a_w), row_blk(a_w), row_blk(pw),
        pl.BlockSpec((segs, n_heads, seg_rows), lambda i: (0, 0, i)),
        row_blk(n_heads),
        pl.BlockSpec((nsb, n_heads, sb), lambda i: (i, 0, 0)),
        pl.BlockSpec((1, 3, 8, LANES), lambda i: (i, 0, 0, 0)),
        per_stream(16, pw),
    )
    in_specs = [
        row_blk(d),
        per_stream(1, d), per_stream(1, d),
        _resident((1, d)),
        _resident(wq.shape), _resident(wf.shape), _resident(bfp.shape), _resident(wu.shape),
        _resident(wp.shape), _resident(ps.shape),
        per_stream(HIST_PAD, pw),
    ]
    return pl.pallas_call(
        kern,
        grid=(n_steps,),
        in_specs=in_specs,
        out_specs=out_specs,
        out_shape=out_shape,
        scratch_shapes=[pltpu.VMEM((bm + segs * HIST_PAD, pw), F32),
                        pltpu.VMEM((bm + segs * HIST_PAD, pw), F32),
                        pltpu.VMEM((bm + segs * HIST_PAD, pw - LANES), F32),
                        pltpu.VMEM((bm + segs * HIST_PAD, pw - 2 * LANES), F32)],
        compiler_params=pltpu.CompilerParams(dimension_semantics=("arbitrary",), vmem_limit_bytes=VMEM_LIMIT),
        name="proj",
    )(x2, s1, sh, norm_g, wq, wf, bfp, wu, wp, ps, hist0)


def _merge_norm(x, gate, za, zp, wo_ref, fg, a_w):
    dy = _dot(za, wo_ref[0:a_w, :]) + _dot(zp, wo_ref[a_w:, :])
    out = x + gate * dy
    ms = jnp.mean(out * out, axis=-1, keepdims=True)
    return out * lax.rsqrt(ms + EPS) * fg


def _attn_kernel(tot_ref, qkb_ref,
                 q_ref, cq_ref, ck_ref, k_ref, v_ref, sa_ref, zp_ref, x_ref, gate_ref, wo_ref, fg_ref,
                 y_ref,
                 z_ref, s_ref, p_ref, m_ref, l_ref, acc_ref, *, tm, n_heads):
    a_w = n_heads * HEAD_DIM
    nsub = tm // ATT_BLK
    w_near = NEAR_BLOCKS
    cw = w_near * ATT_BLK
    step = pl.program_id(0)
    lane = lax.broadcasted_iota(jnp.int32, (ATT_BLK, LANES), 1)
    half = [lane < HEAD_DIM, lane >= HEAD_DIM]
    col = lax.broadcasted_iota(jnp.int32, (ATT_BLK, cw), 1)

    def tot_at(b, hd):
        return jnp.where(b >= 0, tot_ref[jnp.maximum(b, 0) * n_heads + hd], 0.0)

    def q_masked(r0, hd):
        q2 = q_ref[pl.ds(r0, ATT_BLK), (hd // PAIR) * LANES:(hd // PAIR + 1) * LANES]
        return jnp.where(half[hd % PAIR], q2, jnp.zeros_like(q2))

    def rows_of(ref, blocks, hd):
        g = hd // PAIR
        return jnp.concatenate(
            [ref[pl.ds(pl.multiple_of(b * ATT_BLK, ATT_BLK), ATT_BLK), g * LANES:(g + 1) * LANES] for b in blocks],
            axis=0)

    def scores(r0, hd, j0, offs):
        s = _dot_nt(q_masked(r0, hd), rows_of(k_ref, [j0 + p for p in range(w_near)], hd))
        dec = jnp.concatenate([offs[p] - ck_ref[j0 + p, hd:hd + 1, :] for p in range(w_near)], axis=1)
        return s + cq_ref[pl.ds(r0, ATT_BLK), hd:hd + 1] + dec

    def values(hd, j0):
        return rows_of(v_ref, [j0 + p for p in range(w_near)], hd)

    tri = (lax.broadcasted_iota(jnp.int32, (ATT_BLK, ATT_BLK), 1)
           <= lax.broadcasted_iota(jnp.int32, (ATT_BLK, ATT_BLK), 0))
    tri_bias = jnp.where(tri, 0.0, NEG).astype(F32)
    ones_v = jnp.ones((cw, LANES), BF16)

    def sub_body(sub, carry):
        i = step * nsub + sub
        r0 = pl.multiple_of(sub * ATT_BLK, ATT_BLK)

        near = [i - (w_near - 1) + p for p in range(w_near)]
        near_c = [jnp.maximum(b, 0) for b in near]
        offs_far = []
        for hd in range(n_heads):
            back = [tot_at(i - dd, hd) for dd in range(1, w_near)]
            s = _dot_nt(q_masked(r0, hd), rows_of(k_ref, near_c, hd))
            cqh = cq_ref[pl.ds(r0, ATT_BLK), hd:hd + 1]
            pieces = []
            for p in range(w_near):
                o = jnp.float32(0.0)
                for dd in range(1, w_near - p):
                    o = o + back[dd - 1]
                o = jnp.where(near[p] >= 0, o, NEG)
                sp = s[:, p * ATT_BLK:(p + 1) * ATT_BLK] + cqh + (o - ck_ref[near_c[p], hd:hd + 1, :])
                pieces.append(sp + tri_bias if p == w_near - 1 else sp)
            s_ref[hd] = jnp.concatenate(pieces, axis=1)
            o = jnp.float32(0.0)
            for t in back:
                o = o + t
            offs_far.append(o)
        for hd in range(n_heads):
            s = s_ref[hd]
            m = jnp.max(s, axis=1, keepdims=True)
            m_ref[hd] = m
            p_ref[hd] = jnp.exp(s - m).astype(BF16)
        for hd in range(n_heads):
            r = _dot(p_ref[hd], jnp.concatenate([rows_of(v_ref, near_c, hd), ones_v], axis=1))
            acc_ref[hd] = r[:, 0:LANES]
            l_ref[hd] = r[:, LANES:]

        qk = [qkb_ref[i * n_heads + hd] for hd in range(n_heads)]

        def far_cond(c):
            top = i - c[0] * w_near
            need = qk[0] + c[1] >= -EXP_UNDERFLOW
            for hd in range(1, n_heads):
                need = jnp.logical_or(need, qk[hd] + c[1 + hd] >= -EXP_UNDERFLOW)
            return jnp.logical_and(top >= 0, need)

        def far_body(c):
            top = i - c[0] * w_near
            jc = jnp.maximum(top - (w_near - 1), 0)
            keepc = col < (top + 1 - jc) * ATT_BLK
            new = [c[0] + 1]
            for hd in range(n_heads):
                tt = [tot_at(top - e, hd) for e in range(w_near)]
                offs = []
                for p in range(w_near):
                    behind = top - (jc + p)
                    o = c[1 + hd]
                    for e in range(w_near):
                        o = o + jnp.where(behind >= e, tt[e], 0.0)
                    offs.append(o)
                s = jnp.where(keepc, scores(r0, hd, jc, offs), NEG)
                m_old = m_ref[hd]
                m_new = jnp.maximum(m_old, jnp.max(s, axis=1, keepdims=True))
                alpha = jnp.exp(m_old - m_new)
                p_ = jnp.exp(s - m_new)
                m_ref[hd] = m_new
                l_ref[hd] = alpha * l_ref[hd] + jnp.sum(p_, axis=1, keepdims=True)
                acc_ref[hd] = alpha * acc_ref[hd] + _dot(p_.astype(BF16), values(hd, jc))
                o = c[1 + hd]
                for t in tt:
                    o = o + t
                new.append(o)
            return tuple(new)

        lax.while_loop(far_cond, far_body, (jnp.int32(1),) + tuple(offs_far))

        for g in range(n_heads // PAIR):
            cols = slice(g * LANES, (g + 1) * LANES)
            o0 = acc_ref[PAIR * g] / l_ref[PAIR * g]
            o1 = acc_ref[PAIR * g + 1] / l_ref[PAIR * g + 1]
            o = jnp.where(half[0], o0, o1)
            z_ref[pl.ds(r0, ATT_BLK), cols] = (o * sa_ref[pl.ds(r0, ATT_BLK), cols].astype(F32)).astype(BF16)
        return carry

    lax.fori_loop(0, nsub, sub_body, 0)
    y_ref[...] = _merge_norm(x_ref[...], gate_ref[...], z_ref[...], zp_ref[...], wo_ref, fg_ref[...], a_w)


def _prompt_attention(tot, qkb, q, cq, ck, kb, vb, sa, zp, x2, gate, wo, fg, *, tm, n_heads):
    rows, d = x2.shape
    a_w = n_heads * HEAD_DIM
    assert rows % tm == 0 and tm % ATT_BLK == 0 and rows // ATT_BLK >= NEAR_BLOCKS
    row_blk = lambda w: pl.BlockSpec((tm, w), lambda i, *_: (i, 0))
    grid_spec = pltpu.PrefetchScalarGridSpec(
        num_scalar_prefetch=2,
        grid=(rows // tm,),
        in_specs=[row_blk(a_w), row_blk(n_heads), _resident(ck.shape), _resident(kb.shape), _resident(vb.shape),
                  row_blk(a_w), row_blk(zp.shape[1]), row_blk(d), _resident((1, d)), _resident(wo.shape),
                  _resident((1, d))],
        out_specs=row_blk(d),
        scratch_shapes=[pltpu.VMEM((tm, a_w), BF16),
                        pltpu.VMEM((n_heads, ATT_BLK, NEAR_BLOCKS * ATT_BLK), F32),
                        pltpu.VMEM((n_heads, ATT_BLK, NEAR_BLOCKS * ATT_BLK), BF16),
                        pltpu.VMEM((n_heads, ATT_BLK, 1), F32),
                        pltpu.VMEM((n_heads, ATT_BLK, LANES), F32),
                        pltpu.VMEM((n_heads, ATT_BLK, LANES), F32)],
    )
    return pl.pallas_call(
        functools.partial(_attn_kernel, tm=tm, n_heads=n_heads),
        grid_spec=grid_spec,
        out_shape=jax.ShapeDtypeStruct((rows, d), F32),
        compiler_params=pltpu.CompilerParams(dimension_semantics=("arbitrary",), vmem_limit_bytes=VMEM_LIMIT),
        name="attn",
    )(tot, qkb, q, cq, ck, kb, vb, sa, zp, x2, gate, wo, fg)


def _sattn_kernel(q_ref, cq_ref, ckn_ref, kn_ref, vn_ref, kc_ref, vc_ref, lfc_ref, sa_ref, zp_ref, x_ref,
                  gate_ref, wo_ref, fg_ref, y_ref, *, n_heads):
    a_w = n_heads * HEAD_DIM
    ln = q_ref.shape[0]
    past = kc_ref.shape[2]
    nb = past // LANES

    lfc = lfc_ref[0]
    triu = (lax.broadcasted_iota(jnp.int32, (LANES, LANES), 0)
            <= lax.broadcasted_iota(jnp.int32, (LANES, LANES), 1)).astype(BF16)
    zeros = jnp.zeros((8, LANES), F32)
    parts = []
    for b in range(nb):
        parts.extend(_split3(lfc[:, b * LANES:(b + 1) * LANES]))
        parts.append(zeros)
    cs = _dot(jnp.concatenate(parts, axis=0).astype(BF16), triu)
    after = jnp.zeros((n_heads, 1), F32)
    suffix = [None] * nb
    for b in reversed(range(nb)):
        cb = cs[32 * b:32 * b + 8] + cs[32 * b + 8:32 * b + 16] + cs[32 * b + 16:32 * b + 24]
        tot = cb[:, LANES - 1:LANES]
        suffix[b] = (tot - cb) + after
        after = after + tot
    dec_c = jnp.concatenate(suffix, axis=1)

    lane = lax.broadcasted_iota(jnp.int32, (ln, LANES), 1)
    half = [lane < HEAD_DIM, lane >= HEAD_DIM]
    causal = lax.broadcasted_iota(jnp.int32, (ln, ln), 1) <= lax.broadcasted_iota(jnp.int32, (ln, ln), 0)
    outs = []
    for g in range(n_heads // PAIR):
        cols = slice(g * LANES, (g + 1) * LANES)
        q2 = q_ref[:, cols]
        kct = kc_ref[0, cols, :].astype(BF16)
        vct = vc_ref[0, cols, :].astype(BF16)
        kn = kn_ref[:, cols]
        vn = vn_ref[:, cols]
        qst = jnp.concatenate([jnp.where(half[e], q2, jnp.zeros_like(q2)) for e in range(PAIR)], axis=0)
        sc_st = _dot(qst, kct)
        sn_st = _dot_nt(qst, kn)
        pc, pn, ls = [], [], []
        for e in range(PAIR):
            hd = PAIR * g + e
            rows = slice(e * ln, (e + 1) * ln)
            cqh = cq_ref[:, hd:hd + 1]
            s_c = sc_st[rows] + cqh + dec_c[hd:hd + 1, :]
            s_n = jnp.where(causal, sn_st[rows] + cqh - ckn_ref[0, hd:hd + 1, :], NEG)
            m = jnp.maximum(jnp.max(s_c, axis=1, keepdims=True), jnp.max(s_n, axis=1, keepdims=True))
            p_c = jnp.exp(s_c - m)
            p_n = jnp.exp(s_n - m)
            ls.append(jnp.sum(p_c, axis=1, keepdims=True) + jnp.sum(p_n, axis=1, keepdims=True))
            pc.append(p_c.astype(BF16))
            pn.append(p_n.astype(BF16))
        acc = _dot_nt(jnp.concatenate(pc, axis=0), vct) + _dot(jnp.concatenate(pn, axis=0), vn)
        o = jnp.where(half[0], acc[0:ln] / ls[0], acc[ln:2 * ln] / ls[1])
        outs.append((o * sa_ref[:, cols].astype(F32)).astype(BF16))
    za = jnp.concatenate(outs, axis=1)
    y_ref[...] = _merge_norm(x_ref[...], gate_ref[0], za, zp_ref[...], wo_ref, fg_ref[...], a_w)


def _sample_attention(q, cq, ckn, kb, vb, cache_k, cache_v, lfc, sa, zp, x2, gate, wo, fg, *, ln, n_heads):
    rows, d = x2.shape
    nbatch = rows // ln
    a_w = n_heads * HEAD_DIM
    past = cache_k.shape[2]
    row_blk = lambda w: pl.BlockSpec((ln, w), lambda b: (b, 0))
    per_b = lambda s: pl.BlockSpec((1,) + s, lambda b: (b, 0, 0))
    return pl.pallas_call(
        functools.partial(_sattn_kernel, n_heads=n_heads),
        grid=(nbatch,),
        in_specs=[row_blk(a_w), row_blk(n_heads), per_b((n_heads, ln)), row_blk(a_w), row_blk(a_w),
                  per_b((a_w, past)), per_b((a_w, past)), per_b((n_heads, past)),
                  row_blk(a_w), row_blk(zp.shape[1]), row_blk(d), per_b((1, d)), _resident(wo.shape),
                  _resident((1, d))],
        out_specs=row_blk(d),
        out_shape=jax.ShapeDtypeStruct((rows, d), F32),
        compiler_params=pltpu.CompilerParams(dimension_semantics=("arbitrary",), vmem_limit_bytes=VMEM_LIMIT),
        name="sattn",
    )(q, cq, ckn, kb, vb, cache_k, cache_v, lfc, sa, zp, x2, gate, wo, fg)


def kernel(x_prompt, x_sample, c_prompt, c_sample, cache_k, cache_v, cache_logf, state_pool, norm_g, w_ada, b_ada,
           w_in, b_f, w_pool, pool_scale, w_out, final_g):
    depth = norm_g.shape[0]
    assert depth == 1
    bp, seq, d = x_prompt.shape
    bs, ln, _ = x_sample.shape
    assert bp == 1
    n_heads = cache_k.shape[3]
    past = cache_k.shape[2]
    a_w = n_heads * HEAD_DIM
    pw = state_pool.shape[3]
    assert pw == len(POOL_WINDOWS) * LANES and cache_k.shape[4] == HEAD_DIM and n_heads <= 8

    n_c = bp + bs
    c_all = jnp.concatenate([c_prompt, c_sample, jnp.zeros((16 - n_c, d), F32)], axis=0)
    m_all = _ada_terms(c_all, w_ada[0], b_ada[0][None, :])
    shift, scale, gate = m_all[:, 0:d], m_all[:, d:2 * d], m_all[:, 2 * d:3 * d]
    s1 = (1.0 + scale)[:, None, :]
    sh = shift[:, None, :]

    wit = w_in[0].T
    wq = wit[0:4 * a_w].astype(BF16)
    wf = jnp.pad(wit[4 * a_w:4 * a_w + n_heads], ((0, LANES - n_heads), (0, 0))).astype(BF16)
    wu = wit[4 * a_w + n_heads:].astype(BF16)
    bfp = jnp.pad(b_f[0][None, :], ((0, 0), (0, LANES - n_heads)))
    wp = w_pool[0].astype(BF16)
    ps = pool_scale[0][None, :]
    wo = w_out[0].astype(BF16)
    ng = norm_g[0][None, :]
    fg = final_g[None, :]

    bm = 512
    xp2 = x_prompt.reshape(seq, d)
    hist_p = jnp.zeros((1, HIST_PAD, pw), F32)
    (q_p, k_p, v_p, kb_p, vb_p, sa_p, zp_p, lf_p, cq_p, ck_p, st_p, ho_p) = _project(
        xp2, s1[0:1], sh[0:1], ng, wq, wf, bfp, wu, wp, ps, hist_p,
        bm=bm, sb=ATT_BLK, segs=1, start_pos=0, n_heads=n_heads, kv_head_major=False)
    nsb = bm // ATT_BLK
    st = st_p[:, :, 0:nsb, 0:n_heads]
    tot = st[:, 0].reshape(-1)
    qn = st[:, 1].reshape(-1, n_heads)
    kn = st[:, 2].reshape(-1, n_heads)
    qkb = (2.0 * qn * jnp.max(kn, axis=0, keepdims=True)).reshape(-1)
    y_p = _prompt_attention(tot, qkb, q_p, cq_p, ck_p, kb_p, vb_p, sa_p, zp_p, xp2, gate[0:1], wo, fg,
                            tm=512, n_heads=n_heads)

    xs2 = x_sample.reshape(bs * ln, d)
    hist_s = jnp.pad(state_pool[0], ((0, 0), (HIST_PAD - POOL_HIST, 0), (0, 0)))
    (q_s, k_s, v_s, kb_s, vb_s, sa_s, zp_s, lf_s, cq_s, ck_s, _, ho_s) = _project(
        xs2, s1[bp:n_c], sh[bp:n_c], ng, wq, wf, bfp, wu, wp, ps, hist_s,
        bm=bs * ln, sb=ln, segs=bs, start_pos=past, n_heads=n_heads, kv_head_major=True)
    lfc = jnp.swapaxes(cache_logf[0], 1, 2)
    ckt = jnp.transpose(cache_k[0], (0, 2, 3, 1)).reshape(bs, a_w, past)
    cvt = jnp.transpose(cache_v[0], (0, 2, 3, 1)).reshape(bs, a_w, past)
    y_s = _sample_attention(q_s, cq_s, ck_s, kb_s, vb_s, ckt, cvt, lfc, sa_s, zp_s, xs2, gate[bp:n_c, None, :], wo, fg,
                            ln=ln, n_heads=n_heads)

    hd = (n_heads, HEAD_DIM)
    return (y_p.reshape(bp, seq, d), y_s.reshape(bs, ln, d),
            k_p.reshape((1, bp, seq) + hd), v_p.reshape((1, bp, seq) + hd), jnp.swapaxes(lf_p, 1, 2)[None],
            ho_p[:, 16 - POOL_HIST:, :][None],
            k_s.reshape((1, bs, ln) + hd), v_s.reshape((1, bs, ln) + hd), jnp.swapaxes(lf_s, 1, 2)[None],
            ho_s[:, 16 - POOL_HIST:, :][None])
```

```python
import functools

import jax
import jax.numpy as jnp
from jax import lax
from jax.experimental import pallas as pl
from jax.experimental.pallas import tpu as pltpu

HEAD_DIM = 64
POOL_WINDOWS = (2, 4, 8, 16)
EPS = 1e-6

LANES = 128
PAIR = LANES // HEAD_DIM
ATT_BLK = 128
NEAR_BLOCKS = 3
EXP_UNDERFLOW = 104.0
NORM_SLACK = 1.01
HIST_PAD = 32
POOL_HIST = max(POOL_WINDOWS) - 1
NEG = -1e30
VMEM_LIMIT = 60 * 1024 * 1024

F32 = jnp.float32
BF16 = jnp.bfloat16


def _silu(x):
    return x * jax.nn.sigmoid(x)


def _dot(a, b):
    return jnp.dot(a, b, preferred_element_type=F32)


def _dot_nt(a, b):
    return lax.dot_general(a, b, (((1,), (1,)), ((), ())), preferred_element_type=F32)


def _split3(x):
    hi = x.astype(BF16).astype(F32)
    r1 = x - hi
    mid = r1.astype(BF16).astype(F32)
    return hi, mid, r1 - mid


def _rows_to_lanes(x, n):
    rows = x.shape[0]
    if rows < LANES:
        x = jnp.concatenate([x, jnp.zeros((LANES - rows, LANES), x.dtype)], axis=0)
    return x.T[0:n, 0:rows]


def _resident(shape):
    return pl.BlockSpec(shape, lambda *_: (0,) * len(shape), pipeline_mode=pl.Buffered(1))


def _ada_kernel(c_ref, w_ref, b_ref, o_ref):
    a = _silu(c_ref[...]).astype(BF16)
    o_ref[...] = _dot(a, w_ref[...].astype(BF16)) + b_ref[...]


def _ada_terms(c_all, w_ada, b_ada):
    rows, d = c_all.shape
    n = w_ada.shape[1]
    bn = 1024
    return pl.pallas_call(
        _ada_kernel,
        grid=(n // bn,),
        in_specs=[pl.BlockSpec((rows, d), lambda j: (0, 0)),
                  pl.BlockSpec((d, bn), lambda j: (0, j)),
                  pl.BlockSpec((1, bn), lambda j: (0, j))],
        out_specs=pl.BlockSpec((rows, bn), lambda j: (0, j)),
        out_shape=jax.ShapeDtypeStruct((rows, n), F32),
        compiler_params=pltpu.CompilerParams(dimension_semantics=("arbitrary",), vmem_limit_bytes=VMEM_LIMIT),
        name="ada",
    )(c_all, w_ada, b_ada)


def _proj_kernel(x_ref, s1_ref, sh_ref, ng_ref, wq_ref, wf_ref, bf_ref, wu_ref, wp_ref, ps_ref, h0_ref,
                 q_ref, k32_ref, v32_ref, kb_ref, vb_ref, sa_ref, zp_ref, lf_ref, cq_ref, ck_ref, st_ref, ho_ref,
                 e_ref, t2_ref, t4_ref, t8_ref, *, bm, sb, segs, start_pos, n_heads, kv_head_major):
    a_w = n_heads * HEAD_DIM
    seg_rows = bm // segs
    step = pl.program_id(0)
    x = x_ref[...]
    ms = jnp.mean(x * x, axis=-1, keepdims=True)
    xn = x * lax.rsqrt(ms + EPS) * ng_ref[...]
    h = jnp.concatenate([xn[g * seg_rows:(g + 1) * seg_rows] * s1_ref[g] + sh_ref[g] for g in range(segs)],
                        axis=0).astype(BF16)

    sc = 1.0 / (HEAD_DIM ** 0.5)
    qs = _dot_nt(h, wq_ref[0:a_w, :]) * sc
    q_ref[...] = qs.astype(BF16)
    pk = _dot_nt(h, wq_ref[a_w:2 * a_w, :])
    pv = _dot_nt(h, wq_ref[2 * a_w:3 * a_w, :])
    kb_ref[...] = pk.astype(BF16)
    vb_ref[...] = pv.astype(BF16)
    if kv_head_major:
        for hd in range(n_heads):
            k32_ref[:, hd, :] = pk[:, hd * HEAD_DIM:(hd + 1) * HEAD_DIM]
            v32_ref[:, hd, :] = pv[:, hd * HEAD_DIM:(hd + 1) * HEAD_DIM]
    else:
        k32_ref[...] = pk
        v32_ref[...] = pv
    pa = _dot_nt(h, wq_ref[3 * a_w:4 * a_w, :])
    sa_ref[...] = _silu(pa).astype(BF16)

    z = _dot_nt(h, wf_ref[...]) + bf_ref[...]
    lf = jnp.minimum(z, 0.0) - jnp.log1p(jnp.exp(-jnp.abs(z)))

    sel = (lax.broadcasted_iota(jnp.int32, (a_w, LANES), 0) // HEAD_DIM
           == lax.broadcasted_iota(jnp.int32, (a_w, LANES), 1)).astype(BF16)
    nq = jnp.sqrt(_dot((qs * qs).astype(BF16), sel)) * NORM_SLACK
    nk = jnp.sqrt(_dot((pk * pk).astype(BF16), sel)) * NORM_SLACK

    tri = (lax.broadcasted_iota(jnp.int32, (sb, sb), 1)
           <= lax.broadcasted_iota(jnp.int32, (sb, sb), 0)).astype(BF16)
    tots, qmx, kmx = [], [], []
    for s in range(bm // sb):
        rows = slice(s * sb, (s + 1) * sb)
        hi, mid, lo = _split3(lf[rows])
        cb = _dot(tri, hi.astype(BF16)) + _dot(tri, mid.astype(BF16)) + _dot(tri, lo.astype(BF16))
        cq_ref[rows, :] = cb[:, 0:n_heads]
        ck_ref[s] = _rows_to_lanes(cb, n_heads)
        off = (s * sb) % seg_rows
        lf_ref[(s * sb) // seg_rows, :, off:off + sb] = _rows_to_lanes(lf[rows], n_heads)
        tots.append(cb[sb - 1:sb, :])
        qmx.append(jnp.max(nq[rows], axis=0, keepdims=True))
        kmx.append(jnp.max(nk[rows], axis=0, keepdims=True))
    pad = [jnp.zeros((8 - bm // sb, LANES), F32)] if bm // sb < 8 else []
    st_ref[0, 0] = jnp.concatenate(tots + pad, axis=0)
    st_ref[0, 1] = jnp.concatenate(qmx + pad, axis=0)
    st_ref[0, 2] = jnp.concatenate(kmx + pad, axis=0)

    pw = len(POOL_WINDOWS) * LANES
    pu = _dot_nt(h, wu_ref[0:pw, :])
    spg = _silu(_dot_nt(h, wu_ref[pw:2 * pw, :]))

    ext = HIST_PAD + seg_rows
    n = segs * ext

    def load_history():
        for g in range(segs):
            e_ref[g * ext:g * ext + HIST_PAD, :] = h0_ref[g]

    if segs > 1:
        load_history()
    else:
        pl.when(step == 0)(load_history)

    for g in range(segs):
        e_ref[g * ext + HIST_PAD:(g + 1) * ext, :] = pu[g * seg_rows:(g + 1) * seg_rows]
    t2_ref[8:n, :] = e_ref[8:n, :] + e_ref[7:n - 1, :]
    t4_ref[16:n, :] = t2_ref[16:n, LANES:] + t2_ref[14:n - 2, LANES:]
    t8_ref[24:n, :] = t4_ref[24:n, LANES:] + t4_ref[20:n - 4, LANES:]

    def seg_rows_of(ref, cols, back=0):
        return jnp.concatenate([ref[g * ext + HIST_PAD - back:(g + 1) * ext - back, cols] for g in range(segs)], axis=0)

    lane0, lane1 = slice(0, LANES), slice(LANES, 2 * LANES)
    sums = [seg_rows_of(t2_ref, lane0), seg_rows_of(t4_ref, lane0), seg_rows_of(t8_ref, lane0),
            seg_rows_of(t8_ref, lane1) + seg_rows_of(t8_ref, lane1, back=8)]
    row = lax.broadcasted_iota(jnp.int32, (bm, 1), 0)
    pos1 = start_pos + 1 + (step * bm + row if segs == 1 else row % seg_rows)
    for g, w in enumerate(POOL_WINDOWS):
        cols = slice(g * LANES, (g + 1) * LANES)
        rc = 1.0 / jnp.minimum(pos1, w).astype(F32)
        d = sums[g] * rc - pu[:, cols]
        y = _dot(d.astype(BF16), wp_ref[g]) * ps_ref[:, cols]
        zp_ref[:, cols] = (y * spg[:, cols]).astype(BF16)
    for g in range(segs):
        ho_ref[g] = e_ref[(g + 1) * ext - 16:(g + 1) * ext, :]
    if segs == 1:
        e_ref[0:HIST_PAD, :] = e_ref[bm:n, :]


def _project(x2, s1, sh, norm_g, wq, wf, bfp, wu, wp, ps, hist0, *, bm, sb, segs, start_pos, n_heads,
             kv_head_major):
    rows, d = x2.shape
    a_w = n_heads * HEAD_DIM
    pw = len(POOL_WINDOWS) * LANES
    n_steps = rows // bm
    assert segs == 1 or n_steps == 1
    n_streams = segs
    seg_rows = bm // segs
    nsb = bm // sb
    assert nsb <= 8 and seg_rows % sb == 0
    row_blk = lambda w: pl.BlockSpec((bm, w), lambda i: (i, 0))
    per_stream = lambda r, w: pl.BlockSpec((segs, r, w), lambda i: (0, 0, 0))
    kern = functools.partial(_proj_kernel, bm=bm, sb=sb, segs=segs, start_pos=start_pos, n_heads=n_heads,
                             kv_head_major=kv_head_major)
    kv_shape = (rows, n_heads, HEAD_DIM) if kv_head_major else (rows, a_w)
    kv_blk = pl.BlockSpec((bm,) + kv_shape[1:], lambda i: (i,) + (0,) * (len(kv_shape) - 1))
    out_shape = (
        jax.ShapeDtypeStruct((rows, a_w), BF16),
        jax.ShapeDtypeStruct(kv_shape, F32),
        jax.ShapeDtypeStruct(kv_shape, F32),
        jax.ShapeDtypeStruct((rows, a_w), BF16),
        jax.ShapeDtypeStruct((rows, a_w), BF16),
        jax.ShapeDtypeStruct((rows, a_w), BF16),
        jax.ShapeDtypeStruct((rows, pw), BF16),
        jax.ShapeDtypeStruct((n_streams, n_heads, rows // n_streams), F32),
        jax.ShapeDtypeStruct((rows, n_heads), F32),
        jax.ShapeDtypeStruct((rows // sb, n_heads, sb), F32),
        jax.ShapeDtypeStruct((n_steps, 3, 8, LANES), F32),
        jax.ShapeDtypeStruct((n_streams, 16, pw), F32),
    )
    out_specs = (
        row_blk(a_w), kv_blk, kv_blk, row_blk(a_w), row_blk(a_w), row_blk(a_w), row_blk(pw),
        pl.BlockSpec((segs, n_heads, seg_rows), lambda i: (0, 0, i)),
        row_blk(n_heads),
        pl.BlockSpec((nsb, n_heads, sb), lambda i: (i, 0, 0)),
        pl.BlockSpec((1, 3, 8, LANES), lambda i: (i, 0, 0, 0)),
        per_stream(16, pw),
    )
    in_specs = [
        row_blk(d),
        per_stream(1, d), per_stream(1, d),
        _resident((1, d)),
        _resident(wq.shape), _resident(wf.shape), _resident(bfp.shape), _resident(wu.shape),
        _resident(wp.shape), _resident(ps.shape),
        per_stream(HIST_PAD, pw),
    ]
    return pl.pallas_call(
        kern,
        grid=(n_steps,),
        in_specs=in_specs,
        out_specs=out_specs,
        out_shape=out_shape,
        scratch_shapes=[pltpu.VMEM((bm + segs * HIST_PAD, pw), F32),
                        pltpu.VMEM((bm + segs * HIST_PAD, pw), F32),
                        pltpu.VMEM((bm + segs * HIST_PAD, pw - LANES), F32),
                        pltpu.VMEM((bm + segs * HIST_PAD, pw - 2 * LANES), F32)],
        compiler_params=pltpu.CompilerParams(dimension_semantics=("arbitrary",), vmem_limit_bytes=VMEM_LIMIT),
        name="proj",
    )(x2, s1, sh, norm_g, wq, wf, bfp, wu, wp, ps, hist0)


def _merge_norm(x, gate, za, zp, wo_ref, fg, a_w):
    dy = _dot(za, wo_ref[0:a_w, :]) + _dot(zp, wo_ref[a_w:, :])
    out = x + gate * dy
    ms = jnp.mean(out * out, axis=-1, keepdims=True)
    return out * lax.rsqrt(ms + EPS) * fg


def _attn_kernel(tot_ref, qkb_ref,
                 q_ref, cq_ref, ck_ref, knew_ref, vnew_ref, sa_ref, zp_ref, x_ref, gate_ref, wo_ref, fg_ref,
                 y_ref,
                 k_ref, v_ref, z_ref, s_ref, p_ref, m_ref, l_ref, acc_ref, *, tm, n_heads):
    a_w = n_heads * HEAD_DIM
    nsub = tm // ATT_BLK
    w_near = NEAR_BLOCKS
    cw = w_near * ATT_BLK
    step = pl.program_id(0)
    k_ref[pl.ds(pl.multiple_of(step * tm, tm), tm), :] = knew_ref[...]
    v_ref[pl.ds(pl.multiple_of(step * tm, tm), tm), :] = vnew_ref[...]
    lane = lax.broadcasted_iota(jnp.int32, (ATT_BLK, LANES), 1)
    half = [lane < HEAD_DIM, lane >= HEAD_DIM]
    col = lax.broadcasted_iota(jnp.int32, (ATT_BLK, cw), 1)

    def tot_at(b, hd):
        return jnp.where(b >= 0, tot_ref[jnp.maximum(b, 0) * n_heads + hd], 0.0)

    def q_masked(r0, hd):
        q2 = q_ref[pl.ds(r0, ATT_BLK), (hd // PAIR) * LANES:(hd // PAIR + 1) * LANES]
        return jnp.where(half[hd % PAIR], q2, jnp.zeros_like(q2))

    def rows_of(ref, blocks, hd):
        g = hd // PAIR
        return jnp.concatenate(
            [ref[pl.ds(pl.multiple_of(b * ATT_BLK, ATT_BLK), ATT_BLK), g * LANES:(g + 1) * LANES] for b in blocks],
            axis=0)

    def scores(r0, hd, j0, offs):
        s = _dot_nt(q_masked(r0, hd), rows_of(k_ref, [j0 + p for p in range(w_near)], hd))
        dec = jnp.concatenate([offs[p] - ck_ref[j0 + p, hd:hd + 1, :] for p in range(w_near)], axis=1)
        return s + cq_ref[pl.ds(r0, ATT_BLK), hd:hd + 1] + dec

    def values(hd, j0):
        return rows_of(v_ref, [j0 + p for p in range(w_near)], hd)

    tri = (lax.broadcasted_iota(jnp.int32, (ATT_BLK, ATT_BLK), 1)
           <= lax.broadcasted_iota(jnp.int32, (ATT_BLK, ATT_BLK), 0))
    tri_bias = jnp.where(tri, 0.0, NEG).astype(F32)
    ones_v = jnp.ones((cw, LANES), BF16)

    def sub_body(sub, carry):
        i = step * nsub + sub
        r0 = pl.multiple_of(sub * ATT_BLK, ATT_BLK)

        near = [i - (w_near - 1) + p for p in range(w_near)]
        near_c = [jnp.maximum(b, 0) for b in near]
        offs_far = []
        for hd in range(n_heads):
            back = [tot_at(i - dd, hd) for dd in range(1, w_near)]
            s = _dot_nt(q_masked(r0, hd), rows_of(k_ref, near_c, hd))
            cqh = cq_ref[pl.ds(r0, ATT_BLK), hd:hd + 1]
            pieces = []
            for p in range(w_near):
                o = jnp.float32(0.0)
                for dd in range(1, w_near - p):
                    o = o + back[dd - 1]
                o = jnp.where(near[p] >= 0, o, NEG)
                sp = s[:, p * ATT_BLK:(p + 1) * ATT_BLK] + cqh + (o - ck_ref[near_c[p], hd:hd + 1, :])
                pieces.append(sp + tri_bias if p == w_near - 1 else sp)
            s_ref[hd] = jnp.concatenate(pieces, axis=1)
            o = jnp.float32(0.0)
            for t in back:
                o = o + t
            offs_far.append(o)
        for hd in range(n_heads):
            s = s_ref[hd]
            m = jnp.max(s, axis=1, keepdims=True)
            m_ref[hd] = m
            p_ref[hd] = jnp.exp(s - m).astype(BF16)
        for hd in range(n_heads):
            r = _dot(p_ref[hd], jnp.concatenate([rows_of(v_ref, near_c, hd), ones_v], axis=1))
            acc_ref[hd] = r[:, 0:LANES]
            l_ref[hd] = r[:, LANES:]

        qk = [qkb_ref[i * n_heads + hd] for hd in range(n_heads)]

        def far_cond(c):
            top = i - c[0] * w_near
            need = qk[0] + c[1] >= -EXP_UNDERFLOW
            for hd in range(1, n_heads):
                need = jnp.logical_or(need, qk[hd] + c[1 + hd] >= -EXP_UNDERFLOW)
            return jnp.logical_and(top >= 0, need)

        def far_body(c):
            top = i - c[0] * w_near
            jc = jnp.maximum(top - (w_near - 1), 0)
            keepc = col < (top + 1 - jc) * ATT_BLK
            new = [c[0] + 1]
            for hd in range(n_heads):
                tt = [tot_at(top - e, hd) for e in range(w_near)]
                offs = []
                for p in range(w_near):
                    behind = top - (jc + p)
                    o = c[1 + hd]
                    for e in range(w_near):
                        o = o + jnp.where(behind >= e, tt[e], 0.0)
                    offs.append(o)
                s = jnp.where(keepc, scores(r0, hd, jc, offs), NEG)
                m_old = m_ref[hd]
                m_new = jnp.maximum(m_old, jnp.max(s, axis=1, keepdims=True))
                alpha = jnp.exp(m_old - m_new)
                p_ = jnp.exp(s - m_new)
                m_ref[hd] = m_new
                l_ref[hd] = alpha * l_ref[hd] + jnp.sum(p_, axis=1, keepdims=True)
                acc_ref[hd] = alpha * acc_ref[hd] + _dot(p_.astype(BF16), values(hd, jc))
                o = c[1 + hd]
                for t in tt:
                    o = o + t
                new.append(o)
            return tuple(new)

        lax.while_loop(far_cond, far_body, (jnp.int32(1),) + tuple(offs_far))

        for g in range(n_heads // PAIR):
            cols = slice(g * LANES, (g + 1) * LANES)
            o0 = acc_ref[PAIR * g] / l_ref[PAIR * g]
            o1 = acc_ref[PAIR * g + 1] / l_ref[PAIR * g + 1]
            o = jnp.where(half[0], o0, o1)
            z_ref[pl.ds(r0, ATT_BLK), cols] = (o * sa_ref[pl.ds(r0, ATT_BLK), cols].astype(F32)).astype(BF16)
        return carry

    lax.fori_loop(0, nsub, sub_body, 0)
    y_ref[...] = _merge_norm(x_ref[...], gate_ref[...], z_ref[...], zp_ref[...], wo_ref, fg_ref[...], a_w)


def _prompt_attention(tot, qkb, q, cq, ck, kb, vb, sa, zp, x2, gate, wo, fg, *, tm, n_heads):
    rows, d = x2.shape
    a_w = n_heads * HEAD_DIM
    assert rows % tm == 0 and tm % ATT_BLK == 0 and rows // ATT_BLK >= NEAR_BLOCKS
    row_blk = lambda w: pl.BlockSpec((tm, w), lambda i, *_: (i, 0))
    grid_spec = pltpu.PrefetchScalarGridSpec(
        num_scalar_prefetch=2,
        grid=(rows // tm,),
        in_specs=[row_blk(a_w), row_blk(n_heads), _resident(ck.shape), row_blk(a_w), row_blk(a_w),
                  row_blk(a_w), row_blk(zp.shape[1]), row_blk(d), _resident((1, d)), _resident(wo.shape),
                  _resident((1, d))],
        out_specs=row_blk(d),
        scratch_shapes=[pltpu.VMEM(kb.shape, BF16),
                        pltpu.VMEM(vb.shape, BF16),
                        pltpu.VMEM((tm, a_w), BF16),
                        pltpu.VMEM((n_heads, ATT_BLK, NEAR_BLOCKS * ATT_BLK), F32),
                        pltpu.VMEM((n_heads, ATT_BLK, NEAR_BLOCKS * ATT_BLK), BF16),
                        pltpu.VMEM((n_heads, ATT_BLK, 1), F32),
                        pltpu.VMEM((n_heads, ATT_BLK, LANES), F32),
                        pltpu.VMEM((n_heads, ATT_BLK, LANES), F32)],
    )
    return pl.pallas_call(
        functools.partial(_attn_kernel, tm=tm, n_heads=n_heads),
        grid_spec=grid_spec,
        out_shape=jax.ShapeDtypeStruct((rows, d), F32),
        compiler_params=pltpu.CompilerParams(dimension_semantics=("arbitrary",), vmem_limit_bytes=VMEM_LIMIT),
        name="attn",
    )(tot, qkb, q, cq, ck, kb, vb, sa, zp, x2, gate, wo, fg)


def _sattn_kernel(q_ref, cq_ref, ckn_ref, kn_ref, vn_ref, kc_ref, vc_ref, lfc_ref, sa_ref, zp_ref, x_ref,
                  gate_ref, wo_ref, fg_ref, y_ref, *, n_heads):
    a_w = n_heads * HEAD_DIM
    ln = q_ref.shape[0]
    past = kc_ref.shape[2]
    nb = past // LANES

    lfc = lfc_ref[0]
    triu = (lax.broadcasted_iota(jnp.int32, (LANES, LANES), 0)
            <= lax.broadcasted_iota(jnp.int32, (LANES, LANES), 1)).astype(BF16)
    zeros = jnp.zeros((8, LANES), F32)
    parts = []
    for b in range(nb):
        parts.extend(_split3(lfc[:, b * LANES:(b + 1) * LANES]))
        parts.append(zeros)
    cs = _dot(jnp.concatenate(parts, axis=0).astype(BF16), triu)
    after = jnp.zeros((n_heads, 1), F32)
    suffix = [None] * nb
    for b in reversed(range(nb)):
        cb = cs[32 * b:32 * b + 8] + cs[32 * b + 8:32 * b + 16] + cs[32 * b + 16:32 * b + 24]
        tot = cb[:, LANES - 1:LANES]
        suffix[b] = (tot - cb) + after
        after = after + tot
    dec_c = jnp.concatenate(suffix, axis=1)

    lane = lax.broadcasted_iota(jnp.int32, (ln, LANES), 1)
    half = [lane < HEAD_DIM, lane >= HEAD_DIM]
    causal = lax.broadcasted_iota(jnp.int32, (ln, ln), 1) <= lax.broadcasted_iota(jnp.int32, (ln, ln), 0)
    outs = []
    for g in range(n_heads // PAIR):
        cols = slice(g * LANES, (g + 1) * LANES)
        q2 = q_ref[:, cols]
        kct = kc_ref[0, cols, :].astype(BF16)
        vct = vc_ref[0, cols, :].astype(BF16)
        kn = kn_ref[:, cols]
        vn = vn_ref[:, cols]
        qst = jnp.concatenate([jnp.where(half[e], q2, jnp.zeros_like(q2)) for e in range(PAIR)], axis=0)
        sc_st = _dot(qst, kct)
        sn_st = _dot_nt(qst, kn)
        pc, pn, ls = [], [], []
        for e in range(PAIR):
            hd = PAIR * g + e
            rows = slice(e * ln, (e + 1) * ln)
            cqh = cq_ref[:, hd:hd + 1]
            s_c = sc_st[rows] + cqh + dec_c[hd:hd + 1, :]
            s_n = jnp.where(causal, sn_st[rows] + cqh - ckn_ref[0, hd:hd + 1, :], NEG)
            m = jnp.maximum(jnp.max(s_c, axis=1, keepdims=True), jnp.max(s_n, axis=1, keepdims=True))
            p_c = jnp.exp(s_c - m)
            p_n = jnp.exp(s_n - m)
            ls.append(jnp.sum(p_c, axis=1, keepdims=True) + jnp.sum(p_n, axis=1, keepdims=True))
            pc.append(p_c.astype(BF16))
            pn.append(p_n.astype(BF16))
        acc = _dot_nt(jnp.concatenate(pc, axis=0), vct) + _dot(jnp.concatenate(pn, axis=0), vn)
        o = jnp.where(half[0], acc[0:ln] / ls[0], acc[ln:2 * ln] / ls[1])
        outs.append((o * sa_ref[:, cols].astype(F32)).astype(BF16))
    za = jnp.concatenate(outs, axis=1)
    y_ref[...] = _merge_norm(x_ref[...], gate_ref[0], za, zp_ref[...], wo_ref, fg_ref[...], a_w)


def _sample_attention(q, cq, ckn, kb, vb, cache_k, cache_v, lfc, sa, zp, x2, gate, wo, fg, *, ln, n_heads):
    rows, d = x2.shape
    nbatch = rows // ln
    a_w = n_heads * HEAD_DIM
    past = cache_k.shape[2]
    row_blk = lambda w: pl.BlockSpec((ln, w), lambda b: (b, 0))
    per_b = lambda s: pl.BlockSpec((1,) + s, lambda b: (b, 0, 0))
    return pl.pallas_call(
        functools.partial(_sattn_kernel, n_heads=n_heads),
        grid=(nbatch,),
        in_specs=[row_blk(a_w), row_blk(n_heads), per_b((n_heads, ln)), row_blk(a_w), row_blk(a_w),
                  per_b((a_w, past)), per_b((a_w, past)), per_b((n_heads, past)),
                  row_blk(a_w), row_blk(zp.shape[1]), row_blk(d), per_b((1, d)), _resident(wo.shape),
                  _resident((1, d))],
        out_specs=row_blk(d),
        out_shape=jax.ShapeDtypeStruct((rows, d), F32),
        compiler_params=pltpu.CompilerParams(dimension_semantics=("arbitrary",), vmem_limit_bytes=VMEM_LIMIT),
        name="sattn",
    )(q, cq, ckn, kb, vb, cache_k, cache_v, lfc, sa, zp, x2, gate, wo, fg)


def kernel(x_prompt, x_sample, c_prompt, c_sample, cache_k, cache_v, cache_logf, state_pool, norm_g, w_ada, b_ada,
           w_in, b_f, w_pool, pool_scale, w_out, final_g):
    depth = norm_g.shape[0]
    assert depth == 1
    bp, seq, d = x_prompt.shape
    bs, ln, _ = x_sample.shape
    assert bp == 1
    n_heads = cache_k.shape[3]
    past = cache_k.shape[2]
    a_w = n_heads * HEAD_DIM
    pw = state_pool.shape[3]
    assert pw == len(POOL_WINDOWS) * LANES and cache_k.shape[4] == HEAD_DIM and n_heads <= 8

    n_c = bp + bs
    c_all = jnp.concatenate([c_prompt, c_sample, jnp.zeros((16 - n_c, d), F32)], axis=0)
    m_all = _ada_terms(c_all, w_ada[0], b_ada[0][None, :])
    shift, scale, gate = m_all[:, 0:d], m_all[:, d:2 * d], m_all[:, 2 * d:3 * d]
    s1 = (1.0 + scale)[:, None, :]
    sh = shift[:, None, :]

    wit = w_in[0].T
    wq = wit[0:4 * a_w].astype(BF16)
    wf = jnp.pad(wit[4 * a_w:4 * a_w + n_heads], ((0, LANES - n_heads), (0, 0))).astype(BF16)
    wu = wit[4 * a_w + n_heads:].astype(BF16)
    bfp = jnp.pad(b_f[0][None, :], ((0, 0), (0, LANES - n_heads)))
    wp = w_pool[0].astype(BF16)
    ps = pool_scale[0][None, :]
    wo = w_out[0].astype(BF16)
    ng = norm_g[0][None, :]
    fg = final_g[None, :]

    bm = 512
    xp2 = x_prompt.reshape(seq, d)
    hist_p = jnp.zeros((1, HIST_PAD, pw), F32)
    (q_p, k_p, v_p, kb_p, vb_p, sa_p, zp_p, lf_p, cq_p, ck_p, st_p, ho_p) = _project(
        xp2, s1[0:1], sh[0:1], ng, wq, wf, bfp, wu, wp, ps, hist_p,
        bm=bm, sb=ATT_BLK, segs=1, start_pos=0, n_heads=n_heads, kv_head_major=False)
    nsb = bm // ATT_BLK
    st = st_p[:, :, 0:nsb, 0:n_heads]
    tot = st[:, 0].reshape(-1)
    qn = st[:, 1].reshape(-1, n_heads)
    kn = st[:, 2].reshape(-1, n_heads)
    qkb = (2.0 * qn * jnp.max(kn, axis=0, keepdims=True)).reshape(-1)
    y_p = _prompt_attention(tot, qkb, q_p, cq_p, ck_p, kb_p, vb_p, sa_p, zp_p, xp2, gate[0:1], wo, fg,
                            tm=512, n_heads=n_heads)

    xs2 = x_sample.reshape(bs * ln, d)
    hist_s = jnp.pad(state_pool[0], ((0, 0), (HIST_PAD - POOL_HIST, 0), (0, 0)))
    (q_s, k_s, v_s, kb_s, vb_s, sa_s, zp_s, lf_s, cq_s, ck_s, _, ho_s) = _project(
        xs2, s1[bp:n_c], sh[bp:n_c], ng, wq, wf, bfp, wu, wp, ps, hist_s,
        bm=bs * ln, sb=ln, segs=bs, start_pos=past, n_heads=n_heads, kv_head_major=True)
    lfc = jnp.swapaxes(cache_logf[0], 1, 2)
    ckt = jnp.transpose(cache_k[0], (0, 2, 3, 1)).reshape(bs, a_w, past)
    cvt = jnp.transpose(cache_v[0], (0, 2, 3, 1)).reshape(bs, a_w, past)
    y_s = _sample_attention(q_s, cq_s, ck_s, kb_s, vb_s, ckt, cvt, lfc, sa_s, zp_s, xs2, gate[bp:n_c, None, :], wo, fg,
                            ln=ln, n_heads=n_heads)

    hd = (n_heads, HEAD_DIM)
    return (y_p.reshape(bp, seq, d), y_s.reshape(bs, ln, d),
            k_p.reshape((1, bp, seq) + hd), v_p.reshape((1, bp, seq) + hd), jnp.swapaxes(lf_p, 1, 2)[None],
            ho_p[:, 16 - POOL_HIST:, :][None],
            k_s.reshape((1, bs, ln) + hd), v_s.reshape((1, bs, ln) + hd), jnp.swapaxes(lf_s, 1, 2)[None],
            ho_s[:, 16 - POOL_HIST:, :][None])
```

```python
import functools

import jax
import jax.numpy as jnp
from jax import lax
from jax.experimental import pallas as pl
from jax.experimental.pallas import tpu as pltpu

HEAD_DIM = 64
POOL_WINDOWS = (2, 4, 8, 16)
EPS = 1e-6

LANES = 128
PAIR = LANES // HEAD_DIM
ATT_BLK = 128
NEAR_BLOCKS = 3
LOG2E = 1.4426950408889634
EXP2_UNDERFLOW = 151.0
EXP2_SAFE_SPAN = 100.0
NORM_SLACK = 1.01
HIST_PAD = 32
POOL_HIST = max(POOL_WINDOWS) - 1
NEG = -1e30
VMEM_LIMIT = 60 * 1024 * 1024

F32 = jnp.float32
BF16 = jnp.bfloat16


def _silu(x):
    return x * jax.nn.sigmoid(x)


def _dot(a, b):
    return jnp.dot(a, b, preferred_element_type=F32)


def _dot_nt(a, b):
    return lax.dot_general(a, b, (((1,), (1,)), ((), ())), preferred_element_type=F32)


def _split3(x):
    hi = x.astype(BF16).astype(F32)
    r1 = x - hi
    mid = r1.astype(BF16).astype(F32)
    return hi, mid, r1 - mid


def _rows_to_lanes(x, n):
    rows = x.shape[0]
    if rows < LANES:
        x = jnp.concatenate([x, jnp.zeros((LANES - rows, LANES), x.dtype)], axis=0)
    return x.T[0:n, 0:rows]


def _resident(shape):
    return pl.BlockSpec(shape, lambda *_: (0,) * len(shape), pipeline_mode=pl.Buffered(1))


def _ada_kernel(c_ref, w_ref, b_ref, o_ref):
    a = _silu(c_ref[...]).astype(BF16)
    o_ref[...] = _dot(a, w_ref[...].astype(BF16)) + b_ref[...]


def _ada_terms(c_all, w_ada, b_ada):
    rows, d = c_all.shape
    n = w_ada.shape[1]
    bn = 1024
    return pl.pallas_call(
        _ada_kernel,
        grid=(n // bn,),
        in_specs=[pl.BlockSpec((rows, d), lambda j: (0, 0)),
                  pl.BlockSpec((d, bn), lambda j: (0, j)),
                  pl.BlockSpec((1, bn), lambda j: (0, j))],
        out_specs=pl.BlockSpec((rows, bn), lambda j: (0, j)),
        out_shape=jax.ShapeDtypeStruct((rows, n), F32),
        compiler_params=pltpu.CompilerParams(dimension_semantics=("arbitrary",), vmem_limit_bytes=VMEM_LIMIT),
        name="ada",
    )(c_all, w_ada, b_ada)


def _proj_kernel(x_ref, s1_ref, sh_ref, ng_ref, wq_ref, wf_ref, bf_ref, wu_ref, wp_ref, ps_ref, h0_ref,
                 q_ref, k32_ref, v32_ref, kb_ref, vb_ref, sa_ref, zp_ref, lf_ref, cq_ref, nq_ref, ck_ref, st_ref, ho_ref,
                 e_ref, t2_ref, t4_ref, t8_ref, *, bm, sb, segs, start_pos, n_heads, kv_head_major):
    a_w = n_heads * HEAD_DIM
    seg_rows = bm // segs
    step = pl.program_id(0)
    x = x_ref[...]
    ms = jnp.mean(x * x, axis=-1, keepdims=True)
    xn = x * lax.rsqrt(ms + EPS) * ng_ref[...]
    h = jnp.concatenate([xn[g * seg_rows:(g + 1) * seg_rows] * s1_ref[g] + sh_ref[g] for g in range(segs)],
                        axis=0).astype(BF16)

    sc = LOG2E / (HEAD_DIM ** 0.5)
    qs = _dot_nt(h, wq_ref[0:a_w, :]) * sc
    q_ref[...] = qs.astype(BF16)
    pk = _dot_nt(h, wq_ref[a_w:2 * a_w, :])
    pv = _dot_nt(h, wq_ref[2 * a_w:3 * a_w, :])
    kb_ref[...] = pk.astype(BF16)
    vb_ref[...] = pv.astype(BF16)
    if kv_head_major:
        for hd in range(n_heads):
            k32_ref[:, hd, :] = pk[:, hd * HEAD_DIM:(hd + 1) * HEAD_DIM]
            v32_ref[:, hd, :] = pv[:, hd * HEAD_DIM:(hd + 1) * HEAD_DIM]
    else:
        k32_ref[...] = pk
        v32_ref[...] = pv
    pa = _dot_nt(h, wq_ref[3 * a_w:4 * a_w, :])
    sa_ref[...] = _silu(pa).astype(BF16)

    z = _dot_nt(h, wf_ref[...]) + bf_ref[...]
    lf = jnp.minimum(z, 0.0) - jnp.log1p(jnp.exp(-jnp.abs(z)))

    sel = (lax.broadcasted_iota(jnp.int32, (a_w, LANES), 0) // HEAD_DIM
           == lax.broadcasted_iota(jnp.int32, (a_w, LANES), 1)).astype(BF16)
    nq = jnp.sqrt(_dot((qs * qs).astype(BF16), sel)) * NORM_SLACK
    nk = jnp.sqrt(_dot((pk * pk).astype(BF16), sel)) * NORM_SLACK
    nq_ref[...] = nq[:, 0:n_heads]

    tri = (lax.broadcasted_iota(jnp.int32, (sb, sb), 1)
           <= lax.broadcasted_iota(jnp.int32, (sb, sb), 0)).astype(BF16)
    lf2 = lf * LOG2E
    tots, qmx, kmx = [], [], []
    for s in range(bm // sb):
        rows = slice(s * sb, (s + 1) * sb)
        hi, mid, lo = _split3(lf2[rows])
        cb = _dot(tri, hi.astype(BF16)) + _dot(tri, mid.astype(BF16)) + _dot(tri, lo.astype(BF16))
        cq_ref[rows, :] = cb[:, 0:n_heads]
        ck_ref[s] = _rows_to_lanes(cb, n_heads)
        off = (s * sb) % seg_rows
        lf_ref[(s * sb) // seg_rows, :, off:off + sb] = _rows_to_lanes(lf[rows], n_heads)
        tots.append(cb[sb - 1:sb, :])
        qmx.append(jnp.max(nq[rows], axis=0, keepdims=True))
        kmx.append(jnp.max(nk[rows], axis=0, keepdims=True))
    pad = [jnp.zeros((8 - bm // sb, LANES), F32)] if bm // sb < 8 else []
    st_ref[0, 0] = jnp.concatenate(tots + pad, axis=0)
    st_ref[0, 1] = jnp.concatenate(qmx + pad, axis=0)
    st_ref[0, 2] = jnp.concatenate(kmx + pad, axis=0)

    pw = len(POOL_WINDOWS) * LANES
    pu = _dot_nt(h, wu_ref[0:pw, :])
    spg = _silu(_dot_nt(h, wu_ref[pw:2 * pw, :]))

    ext = HIST_PAD + seg_rows
    n = segs * ext

    def load_history():
        for g in range(segs):
            e_ref[g * ext:g * ext + HIST_PAD, :] = h0_ref[g]

    if segs > 1:
        load_history()
    else:
        pl.when(step == 0)(load_history)

    for g in range(segs):
        e_ref[g * ext + HIST_PAD:(g + 1) * ext, :] = pu[g * seg_rows:(g + 1) * seg_rows]
    t2_ref[8:n, :] = e_ref[8:n, :] + e_ref[7:n - 1, :]
    t4_ref[16:n, :] = t2_ref[16:n, LANES:] + t2_ref[14:n - 2, LANES:]
    t8_ref[24:n, :] = t4_ref[24:n, LANES:] + t4_ref[20:n - 4, LANES:]

    def seg_rows_of(ref, cols, back=0):
        return jnp.concatenate([ref[g * ext + HIST_PAD - back:(g + 1) * ext - back, cols] for g in range(segs)], axis=0)

    lane0, lane1 = slice(0, LANES), slice(LANES, 2 * LANES)
    sums = [seg_rows_of(t2_ref, lane0), seg_rows_of(t4_ref, lane0), seg_rows_of(t8_ref, lane0),
            seg_rows_of(t8_ref, lane1) + seg_rows_of(t8_ref, lane1, back=8)]
    row = lax.broadcasted_iota(jnp.int32, (bm, 1), 0)
    pos1 = start_pos + 1 + (step * bm + row if segs == 1 else row % seg_rows)
    for g, w in enumerate(POOL_WINDOWS):
        cols = slice(g * LANES, (g + 1) * LANES)
        rc = 1.0 / jnp.minimum(pos1, w).astype(F32)
        d = sums[g] * rc - pu[:, cols]
        y = _dot(d.astype(BF16), wp_ref[g]) * ps_ref[:, cols]
        zp_ref[:, cols] = (y * spg[:, cols]).astype(BF16)
    for g in range(segs):
        ho_ref[g] = e_ref[(g + 1) * ext - 16:(g + 1) * ext, :]
    if segs == 1:
        e_ref[0:HIST_PAD, :] = e_ref[bm:n, :]


def _project(x2, s1, sh, norm_g, wq, wf, bfp, wu, wp, ps, hist0, *, bm, sb, segs, start_pos, n_heads,
             kv_head_major):
    rows, d = x2.shape
    a_w = n_heads * HEAD_DIM
    pw = len(POOL_WINDOWS) * LANES
    n_steps = rows // bm
    assert segs == 1 or n_steps == 1
    n_streams = segs
    seg_rows = bm // segs
    nsb = bm // sb
    assert nsb <= 8 and seg_rows % sb == 0
    row_blk = lambda w: pl.BlockSpec((bm, w), lambda i: (i, 0))
    per_stream = lambda r, w: pl.BlockSpec((segs, r, w), lambda i: (0, 0, 0))
    kern = functools.partial(_proj_kernel, bm=bm, sb=sb, segs=segs, start_pos=start_pos, n_heads=n_heads,
                             kv_head_major=kv_head_major)
    kv_shape = (rows, n_heads, HEAD_DIM) if kv_head_major else (rows, a_w)
    kv_blk = pl.BlockSpec((bm,) + kv_shape[1:], lambda i: (i,) + (0,) * (len(kv_shape) - 1))
    out_shape = (
        jax.ShapeDtypeStruct((rows, a_w), BF16),
        jax.ShapeDtypeStruct(kv_shape, F32),
        jax.ShapeDtypeStruct(kv_shape, F32),
        jax.ShapeDtypeStruct((rows, a_w), BF16),
        jax.ShapeDtypeStruct((rows, a_w), BF16),
        jax.ShapeDtypeStruct((rows, a_w), BF16),
        jax.ShapeDtypeStruct((rows, pw), BF16),
        jax.ShapeDtypeStruct((n_streams, n_heads, rows // n_streams), F32),
        jax.ShapeDtypeStruct((rows, n_heads), F32),
        jax.ShapeDtypeStruct((rows, n_heads), F32),
        jax.ShapeDtypeStruct((rows // sb, n_heads, sb), F32),
        jax.ShapeDtypeStruct((n_steps, 3, 8, LANES), F32),
        jax.ShapeDtypeStruct((n_streams, 16, pw), F32),
    )
    out_specs = (
        row_blk(a_w), kv_blk, kv_blk, row_blk(a_w), row_blk(a_w), row_blk(a_w), row_blk(pw),
        pl.BlockSpec((segs, n_heads, seg_rows), lambda i: (0, 0, i)),
        row_blk(n_heads), row_blk(n_heads),
        pl.BlockSpec((nsb, n_heads, sb), lambda i: (i, 0, 0)),
        pl.BlockSpec((1, 3, 8, LANES), lambda i: (i, 0, 0, 0)),
        per_stream(16, pw),
    )
    in_specs = [
        row_blk(d),
        per_stream(1, d), per_stream(1, d),
        _resident((1, d)),
        _resident(wq.shape), _resident(wf.shape), _resident(bfp.shape), _resident(wu.shape),
        _resident(wp.shape), _resident(ps.shape),
        per_stream(HIST_PAD, pw),
    ]
    return pl.pallas_call(
        kern,
        grid=(n_steps,),
        in_specs=in_specs,
        out_specs=out_specs,
        out_shape=out_shape,
        scratch_shapes=[pltpu.VMEM((bm + segs * HIST_PAD, pw), F32),
                        pltpu.VMEM((bm + segs * HIST_PAD, pw), F32),
                        pltpu.VMEM((bm + segs * HIST_PAD, pw - LANES), F32),
                        pltpu.VMEM((bm + segs * HIST_PAD, pw - 2 * LANES), F32)],
        compiler_params=pltpu.CompilerParams(dimension_semantics=("arbitrary",), vmem_limit_bytes=VMEM_LIMIT),
        name="proj",
    )(x2, s1, sh, norm_g, wq, wf, bfp, wu, wp, ps, hist0)


def _merge_norm(x, gate, za, zp, wo_ref, fg, a_w):
    dy = _dot(za, wo_ref[0:a_w, :]) + _dot(zp, wo_ref[a_w:, :])
    out = x + gate * dy
    ms = jnp.mean(out * out, axis=-1, keepdims=True)
    return out * lax.rsqrt(ms + EPS) * fg


def _attn_kernel(tot_ref, qkb_ref, kg_ref,
                 q_ref, cq_ref, nq_ref, ck_ref, knew_ref, vnew_ref, sa_ref, zp_ref, x_ref, gate_ref, wo_ref, fg_ref,
                 y_ref,
                 k_ref, v_ref, z_ref, m_ref, l_ref, acc_ref, *, tm, n_heads):
    a_w = n_heads * HEAD_DIM
    n_pairs = n_heads // PAIR
    nsub = tm // ATT_BLK
    w_near = NEAR_BLOCKS
    cw = w_near * ATT_BLK
    step = pl.program_id(0)
    k_ref[pl.ds(pl.multiple_of(step * tm, tm), tm), :] = knew_ref[...]
    v_ref[pl.ds(pl.multiple_of(step * tm, tm), tm), :] = vnew_ref[...]
    lo_q = lax.broadcasted_iota(jnp.int32, (ATT_BLK, LANES), 1) < HEAD_DIM
    lo_k = lax.broadcasted_iota(jnp.int32, (cw, LANES), 1) < HEAD_DIM
    col = lax.broadcasted_iota(jnp.int32, (ATT_BLK, cw), 1)
    tri = (lax.broadcasted_iota(jnp.int32, (ATT_BLK, ATT_BLK), 1)
           <= lax.broadcasted_iota(jnp.int32, (ATT_BLK, ATT_BLK), 0))
    tri_bias = jnp.where(tri, 0.0, NEG).astype(F32)
    zeros_k = jnp.zeros((cw, LANES), BF16)
    ind_lo = jnp.where(lo_k, 1.0, 0.0).astype(BF16)
    ind_hi = jnp.where(lo_k, 0.0, 1.0).astype(BF16)

    def tot_at(b, hd):
        return jnp.where(b >= 0, tot_ref[jnp.maximum(b, 0) * n_heads + hd], 0.0)

    def rows_of(ref, blocks, g):
        return jnp.concatenate(
            [ref[pl.ds(pl.multiple_of(b * ATT_BLK, ATT_BLK), ATT_BLK), g * LANES:(g + 1) * LANES] for b in blocks],
            axis=0)

    def pair_scores(r0, blocks, g):
        kc = rows_of(k_ref, blocks, g)
        keys = jnp.concatenate([jnp.where(lo_k, kc, zeros_k), jnp.where(lo_k, zeros_k, kc)], axis=0)
        return _dot_nt(q_ref[pl.ds(r0, ATT_BLK), g * LANES:(g + 1) * LANES], keys)

    def pair_values(p_pair, blocks, g):
        vc = rows_of(v_ref, blocks, g)
        vals = jnp.concatenate([jnp.concatenate([jnp.where(lo_k, vc, zeros_k), ind_lo], axis=1),
                                jnp.concatenate([jnp.where(lo_k, zeros_k, vc), ind_hi], axis=1)], axis=0)
        return _dot(p_pair, vals)

    def decay_row(blocks, offs, hd):
        return jnp.concatenate([offs[p] - ck_ref[blocks[p], hd:hd + 1, :] for p in range(w_near)], axis=1)

    def sub_body(sub, carry):
        i = step * nsub + sub
        r0 = pl.multiple_of(sub * ATT_BLK, ATT_BLK)

        rows = pl.ds(r0, ATT_BLK)
        qk = [qkb_ref[i * n_heads + hd] for hd in range(n_heads)]

        near = [i - (w_near - 1) + p for p in range(w_near)]
        near_c = [jnp.maximum(b, 0) for b in near]
        near_offs, offs_far = [], []
        for hd in range(n_heads):
            back = [tot_at(i - dd, hd) for dd in range(1, w_near)]
            offs = []
            for p in range(w_near):
                o = jnp.float32(0.0)
                for dd in range(1, w_near - p):
                    o = o + back[dd - 1]
                offs.append(jnp.where(near[p] >= 0, o, NEG))
            near_offs.append(offs)
            o = jnp.float32(0.0)
            for t in back:
                o = o + t
            offs_far.append(o)

        def near_chunk(bounded):
            def fn():
                s_next = pair_scores(r0, near_c, 0)
                for g in range(n_pairs):
                    s_pair = s_next
                    if g + 1 < n_pairs:
                        s_next = pair_scores(r0, near_c, g + 1)
                    ps = []
                    for e in range(PAIR):
                        hd = PAIR * g + e
                        cqh = cq_ref[rows, hd:hd + 1]
                        if bounded:
                            m = nq_ref[rows, hd:hd + 1] * kg_ref[hd]
                            cqh = cqh - m
                        dec = decay_row(near_c, near_offs[hd], hd)
                        pieces = []
                        for p in range(w_near):
                            lanes = slice(e * cw + p * ATT_BLK, e * cw + (p + 1) * ATT_BLK)
                            sp = s_pair[:, lanes] + cqh + dec[:, p * ATT_BLK:(p + 1) * ATT_BLK]
                            pieces.append(sp + tri_bias if p == w_near - 1 else sp)
                        s = jnp.concatenate(pieces, axis=1)
                        if not bounded:
                            m = jnp.max(s, axis=1, keepdims=True)
                            s = s - m
                        m_ref[hd] = m
                        ps.append(jnp.exp2(s).astype(BF16))
                    r = pair_values(jnp.concatenate(ps, axis=1), near_c, g)
                    acc_ref[g] = r[:, 0:LANES]
                    l_ref[g] = r[:, LANES:]
            return fn

        bounded_ok = qk[0] <= EXP2_SAFE_SPAN
        for hd in range(1, n_heads):
            bounded_ok = jnp.logical_and(bounded_ok, qk[hd] <= EXP2_SAFE_SPAN)
        pl.when(bounded_ok)(near_chunk(True))
        pl.when(jnp.logical_not(bounded_ok))(near_chunk(False))

        def far_cond(c):
            top = i - c[0] * w_near
            need = qk[0] + c[1] >= -EXP2_UNDERFLOW
            for hd in range(1, n_heads):
                need = jnp.logical_or(need, qk[hd] + c[1 + hd] >= -EXP2_UNDERFLOW)
            return jnp.logical_and(top >= 0, need)

        def far_body(c):
            top = i - c[0] * w_near
            jc = jnp.maximum(top - (w_near - 1), 0)
            blocks = [jc + p for p in range(w_near)]
            keepc = col < (top + 1 - jc) * ATT_BLK
            new = [c[0] + 1]
            for g in range(n_pairs):
                s_pair = pair_scores(r0, blocks, g)
                ps, alphas = [], []
                for e in range(PAIR):
                    hd = PAIR * g + e
                    tt = [tot_at(top - b, hd) for b in range(w_near)]
                    offs = []
                    for p in range(w_near):
                        behind = top - (jc + p)
                        o = c[1 + hd]
                        for b in range(w_near):
                            o = o + jnp.where(behind >= b, tt[b], 0.0)
                        offs.append(o)
                    s = s_pair[:, e * cw:(e + 1) * cw] + cq_ref[rows, hd:hd + 1] + decay_row(blocks, offs, hd)
                    s = jnp.where(keepc, s, NEG)
                    m_old = m_ref[hd]
                    m_new = jnp.maximum(m_old, jnp.max(s, axis=1, keepdims=True))
                    m_ref[hd] = m_new
                    alphas.append(jnp.broadcast_to(jnp.exp2(m_old - m_new), (ATT_BLK, LANES)))
                    ps.append(jnp.exp2(s - m_new).astype(BF16))
                    o = c[1 + hd]
                    for t in tt:
                        o = o + t
                    new.append(o)
                alpha = jnp.where(lo_q, alphas[0], alphas[1])
                r = pair_values(jnp.concatenate(ps, axis=1), blocks, g)
                acc_ref[g] = alpha * acc_ref[g] + r[:, 0:LANES]
                l_ref[g] = alpha * l_ref[g] + r[:, LANES:]
            return tuple(new)

        lax.while_loop(far_cond, far_body, (jnp.int32(1),) + tuple(offs_far))

        for g in range(n_pairs):
            cols = slice(g * LANES, (g + 1) * LANES)
            o = acc_ref[g] / l_ref[g]
            z_ref[rows, cols] = (o * sa_ref[rows, cols].astype(F32)).astype(BF16)
        return carry

    lax.fori_loop(0, nsub, sub_body, 0)
    y_ref[...] = _merge_norm(x_ref[...], gate_ref[...], z_ref[...], zp_ref[...], wo_ref, fg_ref[...], a_w)


def _prompt_attention(tot, qkb, kg, q, cq, nq, ck, kb, vb, sa, zp, x2, gate, wo, fg, *, tm, n_heads):
    rows, d = x2.shape
    a_w = n_heads * HEAD_DIM
    n_pairs = n_heads // PAIR
    assert rows % tm == 0 and tm % ATT_BLK == 0 and rows // ATT_BLK >= NEAR_BLOCKS and n_heads % PAIR == 0
    row_blk = lambda w: pl.BlockSpec((tm, w), lambda i, *_: (i, 0))
    grid_spec = pltpu.PrefetchScalarGridSpec(
        num_scalar_prefetch=3,
        grid=(rows // tm,),
        in_specs=[row_blk(a_w), row_blk(n_heads), row_blk(n_heads), _resident(ck.shape), row_blk(a_w), row_blk(a_w),
                  row_blk(a_w), row_blk(zp.shape[1]), row_blk(d), _resident((1, d)), _resident(wo.shape),
                  _resident((1, d))],
        out_specs=row_blk(d),
        scratch_shapes=[pltpu.VMEM(kb.shape, BF16),
                        pltpu.VMEM(vb.shape, BF16),
                        pltpu.VMEM((tm, a_w), BF16),
                        pltpu.VMEM((n_heads, ATT_BLK, 1), F32),
                        pltpu.VMEM((n_pairs, ATT_BLK, LANES), F32),
                        pltpu.VMEM((n_pairs, ATT_BLK, LANES), F32)],
    )
    return pl.pallas_call(
        functools.partial(_attn_kernel, tm=tm, n_heads=n_heads),
        grid_spec=grid_spec,
        out_shape=jax.ShapeDtypeStruct((rows, d), F32),
        compiler_params=pltpu.CompilerParams(dimension_semantics=("arbitrary",), vmem_limit_bytes=VMEM_LIMIT),
        name="attn",
    )(tot, qkb, kg, q, cq, nq, ck, kb, vb, sa, zp, x2, gate, wo, fg)


def _sattn_kernel(q_ref, cq_ref, ckn_ref, kn_ref, vn_ref, kc_ref, vc_ref, lfc_ref, sa_ref, zp_ref, x_ref,
                  gate_ref, wo_ref, fg_ref, y_ref, *, n_heads):
    a_w = n_heads * HEAD_DIM
    ln = q_ref.shape[0]
    past = kc_ref.shape[2]
    nb = past // LANES

    lfc = lfc_ref[0] * LOG2E
    triu = (lax.broadcasted_iota(jnp.int32, (LANES, LANES), 0)
            <= lax.broadcasted_iota(jnp.int32, (LANES, LANES), 1)).astype(BF16)
    zeros = jnp.zeros((8, LANES), F32)
    parts = []
    for b in range(nb):
        parts.extend(_split3(lfc[:, b * LANES:(b + 1) * LANES]))
        parts.append(zeros)
    cs = _dot(jnp.concatenate(parts, axis=0).astype(BF16), triu)
    after = jnp.zeros((n_heads, 1), F32)
    suffix = [None] * nb
    for b in reversed(range(nb)):
        cb = cs[32 * b:32 * b + 8] + cs[32 * b + 8:32 * b + 16] + cs[32 * b + 16:32 * b + 24]
        tot = cb[:, LANES - 1:LANES]
        suffix[b] = (tot - cb) + after
        after = after + tot
    dec_c = jnp.concatenate(suffix, axis=1)

    lane = lax.broadcasted_iota(jnp.int32, (ln, LANES), 1)
    half = [lane < HEAD_DIM, lane >= HEAD_DIM]
    causal = lax.broadcasted_iota(jnp.int32, (ln, ln), 1) <= lax.broadcasted_iota(jnp.int32, (ln, ln), 0)
    outs = []
    for g in range(n_heads // PAIR):
        cols = slice(g * LANES, (g + 1) * LANES)
        q2 = q_ref[:, cols]
        kct = kc_ref[0, cols, :].astype(BF16)
        vct = vc_ref[0, cols, :].astype(BF16)
        kn = kn_ref[:, cols]
        vn = vn_ref[:, cols]
        qst = jnp.concatenate([jnp.where(half[e], q2, jnp.zeros_like(q2)) for e in range(PAIR)], axis=0)
        sc_st = _dot(qst, kct)
        sn_st = _dot_nt(qst, kn)
        pc, pn, ls = [], [], []
        for e in range(PAIR):
            hd = PAIR * g + e
            rows = slice(e * ln, (e + 1) * ln)
            cqh = cq_ref[:, hd:hd + 1]
            s_c = sc_st[rows] + cqh + dec_c[hd:hd + 1, :]
            s_n = jnp.where(causal, sn_st[rows] + cqh - ckn_ref[0, hd:hd + 1, :], NEG)
            m = jnp.maximum(jnp.max(s_c, axis=1, keepdims=True), jnp.max(s_n, axis=1, keepdims=True))
            p_c = jnp.exp2(s_c - m)
            p_n = jnp.exp2(s_n - m)
            ls.append(jnp.sum(p_c, axis=1, keepdims=True) + jnp.sum(p_n, axis=1, keepdims=True))
            pc.append(p_c.astype(BF16))
            pn.append(p_n.astype(BF16))
        acc = _dot_nt(jnp.concatenate(pc, axis=0), vct) + _dot(jnp.concatenate(pn, axis=0), vn)
        o = jnp.where(half[0], acc[0:ln] / ls[0], acc[ln:2 * ln] / ls[1])
        outs.append((o * sa_ref[:, cols].astype(F32)).astype(BF16))
    za = jnp.concatenate(outs, axis=1)
    y_ref[...] = _merge_norm(x_ref[...], gate_ref[0], za, zp_ref[...], wo_ref, fg_ref[...], a_w)


def _sample_attention(q, cq, ckn, kb, vb, cache_k, cache_v, lfc, sa, zp, x2, gate, wo, fg, *, ln, n_heads):
    rows, d = x2.shape
    nbatch = rows // ln
    a_w = n_heads * HEAD_DIM
    past = cache_k.shape[2]
    row_blk = lambda w: pl.BlockSpec((ln, w), lambda b: (b, 0))
    per_b = lambda s: pl.BlockSpec((1,) + s, lambda b: (b, 0, 0))
    return pl.pallas_call(
        functools.partial(_sattn_kernel, n_heads=n_heads),
        grid=(nbatch,),
        in_specs=[row_blk(a_w), row_blk(n_heads), per_b((n_heads, ln)), row_blk(a_w), row_blk(a_w),
                  per_b((a_w, past)), per_b((a_w, past)), per_b((n_heads, past)),
                  row_blk(a_w), row_blk(zp.shape[1]), row_blk(d), per_b((1, d)), _resident(wo.shape),
                  _resident((1, d))],
        out_specs=row_blk(d),
        out_shape=jax.ShapeDtypeStruct((rows, d), F32),
        compiler_params=pltpu.CompilerParams(dimension_semantics=("arbitrary",), vmem_limit_bytes=VMEM_LIMIT),
        name="sattn",
    )(q, cq, ckn, kb, vb, cache_k, cache_v, lfc, sa, zp, x2, gate, wo, fg)


def kernel(x_prompt, x_sample, c_prompt, c_sample, cache_k, cache_v, cache_logf, state_pool, norm_g, w_ada, b_ada,
           w_in, b_f, w_pool, pool_scale, w_out, final_g):
    depth = norm_g.shape[0]
    assert depth == 1
    bp, seq, d = x_prompt.shape
    bs, ln, _ = x_sample.shape
    assert bp == 1
    n_heads = cache_k.shape[3]
    past = cache_k.shape[2]
    a_w = n_heads * HEAD_DIM
    pw = state_pool.shape[3]
    assert pw == len(POOL_WINDOWS) * LANES and cache_k.shape[4] == HEAD_DIM and n_heads <= 8

    n_c = bp + bs
    c_all = jnp.concatenate([c_prompt, c_sample, jnp.zeros((16 - n_c, d), F32)], axis=0)
    m_all = _ada_terms(c_all, w_ada[0], b_ada[0][None, :])
    shift, scale, gate = m_all[:, 0:d], m_all[:, d:2 * d], m_all[:, 2 * d:3 * d]
    s1 = (1.0 + scale)[:, None, :]
    sh = shift[:, None, :]

    wit = w_in[0].T
    wq = wit[0:4 * a_w].astype(BF16)
    wf = jnp.pad(wit[4 * a_w:4 * a_w + n_heads], ((0, LANES - n_heads), (0, 0))).astype(BF16)
    wu = wit[4 * a_w + n_heads:].astype(BF16)
    bfp = jnp.pad(b_f[0][None, :], ((0, 0), (0, LANES - n_heads)))
    wp = w_pool[0].astype(BF16)
    ps = pool_scale[0][None, :]
    wo = w_out[0].astype(BF16)
    ng = norm_g[0][None, :]
    fg = final_g[None, :]

    bm = 512
    xp2 = x_prompt.reshape(seq, d)
    hist_p = jnp.zeros((1, HIST_PAD, pw), F32)
    (q_p, k_p, v_p, kb_p, vb_p, sa_p, zp_p, lf_p, cq_p, nq_p, ck_p, st_p, ho_p) = _project(
        xp2, s1[0:1], sh[0:1], ng, wq, wf, bfp, wu, wp, ps, hist_p,
        bm=bm, sb=ATT_BLK, segs=1, start_pos=0, n_heads=n_heads, kv_head_major=False)
    nsb = bm // ATT_BLK
    st = st_p[:, :, 0:nsb, 0:n_heads]
    tot = st[:, 0].reshape(-1)
    qn = st[:, 1].reshape(-1, n_heads)
    kg = jnp.max(st[:, 2].reshape(-1, n_heads), axis=0)
    qkb = (2.0 * qn * kg[None, :]).reshape(-1)
    y_p = _prompt_attention(tot, qkb, kg, q_p, cq_p, nq_p, ck_p, kb_p, vb_p, sa_p, zp_p, xp2, gate[0:1], wo, fg,
                            tm=512, n_heads=n_heads)

    xs2 = x_sample.reshape(bs * ln, d)
    hist_s = jnp.pad(state_pool[0], ((0, 0), (HIST_PAD - POOL_HIST, 0), (0, 0)))
    (q_s, k_s, v_s, kb_s, vb_s, sa_s, zp_s, lf_s, cq_s, _, ck_s, _, ho_s) = _project(
        xs2, s1[bp:n_c], sh[bp:n_c], ng, wq, wf, bfp, wu, wp, ps, hist_s,
        bm=bs * ln, sb=ln, segs=bs, start_pos=past, n_heads=n_heads, kv_head_major=True)
    lfc = jnp.swapaxes(cache_logf[0], 1, 2)
    ckt = jnp.transpose(cache_k[0], (0, 2, 3, 1)).reshape(bs, a_w, past)
    cvt = jnp.transpose(cache_v[0], (0, 2, 3, 1)).reshape(bs, a_w, past)
    y_s = _sample_attention(q_s, cq_s, ck_s, kb_s, vb_s, ckt, cvt, lfc, sa_s, zp_s, xs2, gate[bp:n_c, None, :], wo, fg,
                            ln=ln, n_heads=n_heads)

    hd = (n_heads, HEAD_DIM)
    return (y_p.reshape(bp, seq, d), y_s.reshape(bs, ln, d),
            k_p.reshape((1, bp, seq) + hd), v_p.reshape((1, bp, seq) + hd), jnp.swapaxes(lf_p, 1, 2)[None],
            ho_p[:, 16 - POOL_HIST:, :][None],
            k_s.reshape((1, bs, ln) + hd), v_s.reshape((1, bs, ln) + hd), jnp.swapaxes(lf_s, 1, 2)[None],
            ho_s[:, 16 - POOL_HIST:, :][None])
```

```python
import functools

import jax
import jax.numpy as jnp
from jax import lax
from jax.experimental import pallas as pl
from jax.experimental.pallas import tpu as pltpu

HEAD_DIM = 64
POOL_WINDOWS = (2, 4, 8, 16)
EPS = 1e-6

LANES = 128
PAIR = LANES // HEAD_DIM
ATT_BLK = 128
NEAR_BLOCKS = 3
LOG2E = 1.4426950408889634
EXP2_UNDERFLOW = 151.0
EXP2_SAFE_SPAN = 100.0
NORM_SLACK = 1.01
HIST_PAD = 32
POOL_HIST = max(POOL_WINDOWS) - 1
NEG = -1e30
VMEM_LIMIT = 60 * 1024 * 1024

F32 = jnp.float32
BF16 = jnp.bfloat16


def _silu(x):
    return x * jax.nn.sigmoid(x)


def _dot(a, b):
    return jnp.dot(a, b, preferred_element_type=F32)


def _dot_nt(a, b):
    return lax.dot_general(a, b, (((1,), (1,)), ((), ())), preferred_element_type=F32)


def _split3(x):
    hi = x.astype(BF16).astype(F32)
    r1 = x - hi
    mid = r1.astype(BF16).astype(F32)
    return hi, mid, r1 - mid


def _rows_to_lanes(x, n):
    rows = x.shape[0]
    if rows < LANES:
        x = jnp.concatenate([x, jnp.zeros((LANES - rows, LANES), x.dtype)], axis=0)
    return x.T[0:n, 0:rows]


def _resident(shape):
    return pl.BlockSpec(shape, lambda *_: (0,) * len(shape), pipeline_mode=pl.Buffered(1))


def _ada_kernel(c_ref, w_ref, b_ref, o_ref):
    a = _silu(c_ref[...]).astype(BF16)
    o_ref[...] = _dot(a, w_ref[...].astype(BF16)) + b_ref[...]


def _ada_terms(c_all, w_ada, b_ada):
    rows, d = c_all.shape
    n = w_ada.shape[1]
    bn = 1024
    return pl.pallas_call(
        _ada_kernel,
        grid=(n // bn,),
        in_specs=[pl.BlockSpec((rows, d), lambda j: (0, 0)),
                  pl.BlockSpec((d, bn), lambda j: (0, j)),
                  pl.BlockSpec((1, bn), lambda j: (0, j))],
        out_specs=pl.BlockSpec((rows, bn), lambda j: (0, j)),
        out_shape=jax.ShapeDtypeStruct((rows, n), F32),
        compiler_params=pltpu.CompilerParams(dimension_semantics=("arbitrary",), vmem_limit_bytes=VMEM_LIMIT),
        name="ada",
    )(c_all, w_ada, b_ada)


def _proj_kernel(x_ref, s1_ref, sh_ref, ng_ref, wq_ref, wf_ref, bf_ref, wu_ref, wp_ref, ps_ref, h0_ref,
                 q_ref, k32_ref, v32_ref, kb_ref, vb_ref, sa_ref, zp_ref, lf_ref, cq_ref, nq_ref, ck_ref, st_ref, ho_ref,
                 e_ref, t2_ref, t4_ref, t8_ref, *, bm, sb, segs, start_pos, n_heads, kv_head_major):
    a_w = n_heads * HEAD_DIM
    seg_rows = bm // segs
    step = pl.program_id(0)
    x = x_ref[...]
    ms = jnp.mean(x * x, axis=-1, keepdims=True)
    xn = x * lax.rsqrt(ms + EPS) * ng_ref[...]
    h = jnp.concatenate([xn[g * seg_rows:(g + 1) * seg_rows] * s1_ref[g] + sh_ref[g] for g in range(segs)],
                        axis=0).astype(BF16)

    pw = len(POOL_WINDOWS) * LANES
    sc = LOG2E / (HEAD_DIM ** 0.5)
    pu = _dot_nt(h, wu_ref[0:pw, :])
    qs = _dot_nt(h, wq_ref[0:a_w, :]) * sc
    q_ref[...] = qs.astype(BF16)

    ext = HIST_PAD + seg_rows
    n = segs * ext

    def load_history():
        for g in range(segs):
            e_ref[g * ext:g * ext + HIST_PAD, :] = h0_ref[g]

    if segs > 1:
        load_history()
    else:
        pl.when(step == 0)(load_history)

    for g in range(segs):
        e_ref[g * ext + HIST_PAD:(g + 1) * ext, :] = pu[g * seg_rows:(g + 1) * seg_rows]
    t2_ref[8:n, :] = e_ref[8:n, :] + e_ref[7:n - 1, :]
    t4_ref[16:n, :] = t2_ref[16:n, LANES:] + t2_ref[14:n - 2, LANES:]
    t8_ref[24:n, :] = t4_ref[24:n, LANES:] + t4_ref[20:n - 4, LANES:]

    def seg_rows_of(ref, cols, back=0):
        return jnp.concatenate([ref[g * ext + HIST_PAD - back:(g + 1) * ext - back, cols] for g in range(segs)], axis=0)

    lane0, lane1 = slice(0, LANES), slice(LANES, 2 * LANES)
    sums = [seg_rows_of(t2_ref, lane0), seg_rows_of(t4_ref, lane0), seg_rows_of(t8_ref, lane0),
            seg_rows_of(t8_ref, lane1) + seg_rows_of(t8_ref, lane1, back=8)]
    row = lax.broadcasted_iota(jnp.int32, (bm, 1), 0)
    pos1 = start_pos + 1 + (step * bm + row if segs == 1 else row % seg_rows)
    pool_d = []
    for g, w in enumerate(POOL_WINDOWS):
        rc = 1.0 / jnp.minimum(pos1, w).astype(F32)
        pool_d.append((sums[g] * rc - pu[:, g * LANES:(g + 1) * LANES]).astype(BF16))
    for g in range(segs):
        ho_ref[g] = e_ref[(g + 1) * ext - 16:(g + 1) * ext, :]
    if segs == 1:
        e_ref[0:HIST_PAD, :] = e_ref[bm:n, :]

    pk = _dot_nt(h, wq_ref[a_w:2 * a_w, :])
    kb_ref[...] = pk.astype(BF16)
    spg = _silu(_dot_nt(h, wu_ref[pw:2 * pw, :]))

    z = _dot_nt(h, wf_ref[...]) + bf_ref[...]
    lf = jnp.minimum(z, 0.0) - jnp.log1p(jnp.exp(-jnp.abs(z)))

    sel = (lax.broadcasted_iota(jnp.int32, (a_w, LANES), 0) // HEAD_DIM
           == lax.broadcasted_iota(jnp.int32, (a_w, LANES), 1)).astype(BF16)
    nq = jnp.sqrt(_dot((qs * qs).astype(BF16), sel)) * NORM_SLACK
    nk = jnp.sqrt(_dot((pk * pk).astype(BF16), sel)) * NORM_SLACK
    nq_ref[...] = nq[:, 0:n_heads]

    pa = _dot_nt(h, wq_ref[3 * a_w:4 * a_w, :])
    sa_ref[...] = _silu(pa).astype(BF16)

    tri = (lax.broadcasted_iota(jnp.int32, (sb, sb), 1)
           <= lax.broadcasted_iota(jnp.int32, (sb, sb), 0)).astype(BF16)
    lf2 = lf * LOG2E
    tots, qmx, kmx = [], [], []
    for s in range(bm // sb):
        rows = slice(s * sb, (s + 1) * sb)
        hi, mid, lo = _split3(lf2[rows])
        cb = _dot(tri, hi.astype(BF16)) + _dot(tri, mid.astype(BF16)) + _dot(tri, lo.astype(BF16))
        cq_ref[rows, :] = cb[:, 0:n_heads]
        ck_ref[s] = _rows_to_lanes(cb, n_heads)
        off = (s * sb) % seg_rows
        lf_ref[(s * sb) // seg_rows, :, off:off + sb] = _rows_to_lanes(lf[rows], n_heads)
        tots.append(cb[sb - 1:sb, :])
        qmx.append(jnp.max(nq[rows], axis=0, keepdims=True))
        kmx.append(jnp.max(nk[rows], axis=0, keepdims=True))
    pad = [jnp.zeros((8 - bm // sb, LANES), F32)] if bm // sb < 8 else []
    st_ref[0, 0] = jnp.concatenate(tots + pad, axis=0)
    st_ref[0, 1] = jnp.concatenate(qmx + pad, axis=0)
    st_ref[0, 2] = jnp.concatenate(kmx + pad, axis=0)

    pv = _dot_nt(h, wq_ref[2 * a_w:3 * a_w, :])
    vb_ref[...] = pv.astype(BF16)
    if kv_head_major:
        for hd in range(n_heads):
            k32_ref[:, hd, :] = pk[:, hd * HEAD_DIM:(hd + 1) * HEAD_DIM]
            v32_ref[:, hd, :] = pv[:, hd * HEAD_DIM:(hd + 1) * HEAD_DIM]
    else:
        k32_ref[...] = pk.T
        v32_ref[...] = pv.T

    for g in range(len(POOL_WINDOWS)):
        cols = slice(g * LANES, (g + 1) * LANES)
        y = _dot(pool_d[g], wp_ref[g]) * ps_ref[:, cols]
        zp_ref[:, cols] = (y * spg[:, cols]).astype(BF16)


def _project(x2, s1, sh, norm_g, wq, wf, bfp, wu, wp, ps, hist0, *, bm, sb, segs, start_pos, n_heads,
             kv_head_major):
    rows, d = x2.shape
    a_w = n_heads * HEAD_DIM
    pw = len(POOL_WINDOWS) * LANES
    n_steps = rows // bm
    assert segs == 1 or n_steps == 1
    n_streams = segs
    seg_rows = bm // segs
    nsb = bm // sb
    assert nsb <= 8 and seg_rows % sb == 0
    row_blk = lambda w: pl.BlockSpec((bm, w), lambda i: (i, 0))
    per_stream = lambda r, w: pl.BlockSpec((segs, r, w), lambda i: (0, 0, 0))
    kern = functools.partial(_proj_kernel, bm=bm, sb=sb, segs=segs, start_pos=start_pos, n_heads=n_heads,
                             kv_head_major=kv_head_major)
    if kv_head_major:
        kv_shape = (rows, n_heads, HEAD_DIM)
        kv_blk = pl.BlockSpec((bm, n_heads, HEAD_DIM), lambda i: (i, 0, 0))
    else:
        kv_shape = (a_w, rows)
        kv_blk = pl.BlockSpec((a_w, bm), lambda i: (0, i))
    out_shape = (
        jax.ShapeDtypeStruct((rows, a_w), BF16),
        jax.ShapeDtypeStruct(kv_shape, F32),
        jax.ShapeDtypeStruct(kv_shape, F32),
        jax.ShapeDtypeStruct((rows, a_w), BF16),
        jax.ShapeDtypeStruct((rows, a_w), BF16),
        jax.ShapeDtypeStruct((rows, a_w), BF16),
        jax.ShapeDtypeStruct((rows, pw), BF16),
        jax.ShapeDtypeStruct((n_streams, n_heads, rows // n_streams), F32),
        jax.ShapeDtypeStruct((rows, n_heads), F32),
        jax.ShapeDtypeStruct((rows, n_heads), F32),
        jax.ShapeDtypeStruct((rows // sb, n_heads, sb), F32),
        jax.ShapeDtypeStruct((n_steps, 3, 8, LANES), F32),
        jax.ShapeDtypeStruct((n_streams, 16, pw), F32),
    )
    out_specs = (
        row_blk(a_w), kv_blk, kv_blk, row_blk(a_w), row_blk(a_w), row_blk(a_w), row_blk(pw),
        pl.BlockSpec((segs, n_heads, seg_rows), lambda i: (0, 0, i)),
        row_blk(n_heads), row_blk(n_heads),
        pl.BlockSpec((nsb, n_heads, sb), lambda i: (i, 0, 0)),
        pl.BlockSpec((1, 3, 8, LANES), lambda i: (i, 0, 0, 0)),
        per_stream(16, pw),
    )
    in_specs = [
        row_blk(d),
        per_stream(1, d), per_stream(1, d),
        _resident((1, d)),
        _resident(wq.shape), _resident(wf.shape), _resident(bfp.shape), _resident(wu.shape),
        _resident(wp.shape), _resident(ps.shape),
        per_stream(HIST_PAD, pw),
    ]
    return pl.pallas_call(
        kern,
        grid=(n_steps,),
        in_specs=in_specs,
        out_specs=out_specs,
        out_shape=out_shape,
        scratch_shapes=[pltpu.VMEM((bm + segs * HIST_PAD, pw), F32),
                        pltpu.VMEM((bm + segs * HIST_PAD, pw), F32),
                        pltpu.VMEM((bm + segs * HIST_PAD, pw - LANES), F32),
                        pltpu.VMEM((bm + segs * HIST_PAD, pw - 2 * LANES), F32)],
        compiler_params=pltpu.CompilerParams(dimension_semantics=("arbitrary",), vmem_limit_bytes=VMEM_LIMIT),
        name="proj",
    )(x2, s1, sh, norm_g, wq, wf, bfp, wu, wp, ps, hist0)


def _merge_norm(x, gate, za, zp, wo_ref, fg, a_w):
    dy = _dot(za, wo_ref[0:a_w, :]) + _dot(zp, wo_ref[a_w:, :])
    out = x + gate * dy
    ms = jnp.mean(out * out, axis=-1, keepdims=True)
    return out * lax.rsqrt(ms + EPS) * fg


def _attn_kernel(tot_ref, qkb_ref, kg_ref,
                 q_ref, cq_ref, nq_ref, ck_ref, knew_ref, vnew_ref, sa_ref, zp_ref, x_ref, gate_ref, wo_ref, fg_ref,
                 y_ref,
                 k_ref, v_ref, z_ref, m_ref, l_ref, acc_ref, *, tm, n_heads):
    a_w = n_heads * HEAD_DIM
    n_pairs = n_heads // PAIR
    nsub = tm // ATT_BLK
    w_near = NEAR_BLOCKS
    cw = w_near * ATT_BLK
    step = pl.program_id(0)
    k_ref[pl.ds(pl.multiple_of(step * tm, tm), tm), :] = knew_ref[...]
    v_ref[pl.ds(pl.multiple_of(step * tm, tm), tm), :] = vnew_ref[...]
    lo_q = lax.broadcasted_iota(jnp.int32, (ATT_BLK, LANES), 1) < HEAD_DIM
    lo_k = lax.broadcasted_iota(jnp.int32, (cw, LANES), 1) < HEAD_DIM
    col = lax.broadcasted_iota(jnp.int32, (ATT_BLK, cw), 1)
    tri = (lax.broadcasted_iota(jnp.int32, (ATT_BLK, ATT_BLK), 1)
           <= lax.broadcasted_iota(jnp.int32, (ATT_BLK, ATT_BLK), 0))
    tri_bias = jnp.where(tri, 0.0, NEG).astype(F32)
    zeros_k = jnp.zeros((cw, LANES), BF16)
    ind_lo = jnp.where(lo_k, 1.0, 0.0).astype(BF16)
    ind_hi = jnp.where(lo_k, 0.0, 1.0).astype(BF16)

    def tot_at(b, hd):
        return jnp.where(b >= 0, tot_ref[jnp.maximum(b, 0) * n_heads + hd], 0.0)

    def rows_of(ref, blocks, g):
        return jnp.concatenate(
            [ref[pl.ds(pl.multiple_of(b * ATT_BLK, ATT_BLK), ATT_BLK), g * LANES:(g + 1) * LANES] for b in blocks],
            axis=0)

    def pair_scores(r0, blocks, g):
        kc = rows_of(k_ref, blocks, g)
        keys = jnp.concatenate([jnp.where(lo_k, kc, zeros_k), jnp.where(lo_k, zeros_k, kc)], axis=0)
        return _dot_nt(q_ref[pl.ds(r0, ATT_BLK), g * LANES:(g + 1) * LANES], keys)

    def pair_values(p_pair, blocks, g):
        vc = rows_of(v_ref, blocks, g)
        vals = jnp.concatenate([jnp.concatenate([jnp.where(lo_k, vc, zeros_k), ind_lo], axis=1),
                                jnp.concatenate([jnp.where(lo_k, zeros_k, vc), ind_hi], axis=1)], axis=0)
        return _dot(p_pair, vals)

    def decay_row(blocks, offs, hd):
        return jnp.concatenate([offs[p] - ck_ref[blocks[p], hd:hd + 1, :] for p in range(w_near)], axis=1)

    def sub_body(sub, carry):
        i = step * nsub + sub
        r0 = pl.multiple_of(sub * ATT_BLK, ATT_BLK)

        rows = pl.ds(r0, ATT_BLK)
        qk = [qkb_ref[i * n_heads + hd] for hd in range(n_heads)]

        near = [i - (w_near - 1) + p for p in range(w_near)]
        near_c = [jnp.maximum(b, 0) for b in near]
        near_offs, offs_far = [], []
        for hd in range(n_heads):
            back = [tot_at(i - dd, hd) for dd in range(1, w_near)]
            offs = []
            for p in range(w_near):
                o = jnp.float32(0.0)
                for dd in range(1, w_near - p):
                    o = o + back[dd - 1]
                offs.append(jnp.where(near[p] >= 0, o, NEG))
            near_offs.append(offs)
            o = jnp.float32(0.0)
            for t in back:
                o = o + t
            offs_far.append(o)

        def near_chunk(bounded):
            def fn():
                s_next = pair_scores(r0, near_c, 0)
                for g in range(n_pairs):
                    s_pair = s_next
                    if g + 1 < n_pairs:
                        s_next = pair_scores(r0, near_c, g + 1)
                    ps = []
                    for e in range(PAIR):
                        hd = PAIR * g + e
                        cqh = cq_ref[rows, hd:hd + 1]
                        if bounded:
                            m = nq_ref[rows, hd:hd + 1] * kg_ref[hd]
                            cqh = cqh - m
                        dec = decay_row(near_c, near_offs[hd], hd)
                        pieces = []
                        for p in range(w_near):
                            lanes = slice(e * cw + p * ATT_BLK, e * cw + (p + 1) * ATT_BLK)
                            sp = s_pair[:, lanes] + cqh + dec[:, p * ATT_BLK:(p + 1) * ATT_BLK]
                            pieces.append(sp + tri_bias if p == w_near - 1 else sp)
                        s = jnp.concatenate(pieces, axis=1)
                        if not bounded:
                            m = jnp.max(s, axis=1, keepdims=True)
                            s = s - m
                        m_ref[hd] = m
                        ps.append(jnp.exp2(s).astype(BF16))
                    r = pair_values(jnp.concatenate(ps, axis=1), near_c, g)
                    acc_ref[g] = r[:, 0:LANES]
                    l_ref[g] = r[:, LANES:]
            return fn

        bounded_ok = qk[0] <= EXP2_SAFE_SPAN
        for hd in range(1, n_heads):
            bounded_ok = jnp.logical_and(bounded_ok, qk[hd] <= EXP2_SAFE_SPAN)
        pl.when(bounded_ok)(near_chunk(True))
        pl.when(jnp.logical_not(bounded_ok))(near_chunk(False))

        def far_cond(c):
            top = i - c[0] * w_near
            need = qk[0] + c[1] >= -EXP2_UNDERFLOW
            for hd in range(1, n_heads):
                need = jnp.logical_or(need, qk[hd] + c[1 + hd] >= -EXP2_UNDERFLOW)
            return jnp.logical_and(top >= 0, need)

        def far_body(c):
            top = i - c[0] * w_near
            jc = jnp.maximum(top - (w_near - 1), 0)
            blocks = [jc + p for p in range(w_near)]
            keepc = col < (top + 1 - jc) * ATT_BLK
            new = [c[0] + 1]
            for g in range(n_pairs):
                s_pair = pair_scores(r0, blocks, g)
                ps, alphas = [], []
                for e in range(PAIR):
                    hd = PAIR * g + e
                    tt = [tot_at(top - b, hd) for b in range(w_near)]
                    offs = []
                    for p in range(w_near):
                        behind = top - (jc + p)
                        o = c[1 + hd]
                        for b in range(w_near):
                            o = o + jnp.where(behind >= b, tt[b], 0.0)
                        offs.append(o)
                    s = s_pair[:, e * cw:(e + 1) * cw] + cq_ref[rows, hd:hd + 1] + decay_row(blocks, offs, hd)
                    s = jnp.where(keepc, s, NEG)
                    m_old = m_ref[hd]
                    m_new = jnp.maximum(m_old, jnp.max(s, axis=1, keepdims=True))
                    m_ref[hd] = m_new
                    alphas.append(jnp.broadcast_to(jnp.exp2(m_old - m_new), (ATT_BLK, LANES)))
                    ps.append(jnp.exp2(s - m_new).astype(BF16))
                    o = c[1 + hd]
                    for t in tt:
                        o = o + t
                    new.append(o)
                alpha = jnp.where(lo_q, alphas[0], alphas[1])
                r = pair_values(jnp.concatenate(ps, axis=1), blocks, g)
                acc_ref[g] = alpha * acc_ref[g] + r[:, 0:LANES]
                l_ref[g] = alpha * l_ref[g] + r[:, LANES:]
            return tuple(new)

        lax.while_loop(far_cond, far_body, (jnp.int32(1),) + tuple(offs_far))

        for g in range(n_pairs):
            cols = slice(g * LANES, (g + 1) * LANES)
            o = acc_ref[g] / l_ref[g]
            z_ref[rows, cols] = (o * sa_ref[rows, cols].astype(F32)).astype(BF16)
        return carry

    lax.fori_loop(0, nsub, sub_body, 0)
    y_ref[...] = _merge_norm(x_ref[...], gate_ref[...], z_ref[...], zp_ref[...], wo_ref, fg_ref[...], a_w)


def _prompt_attention(tot, qkb, kg, q, cq, nq, ck, kb, vb, sa, zp, x2, gate, wo, fg, *, tm, n_heads):
    rows, d = x2.shape
    a_w = n_heads * HEAD_DIM
    n_pairs = n_heads // PAIR
    assert rows % tm == 0 and tm % ATT_BLK == 0 and rows // ATT_BLK >= NEAR_BLOCKS and n_heads % PAIR == 0
    row_blk = lambda w: pl.BlockSpec((tm, w), lambda i, *_: (i, 0))
    grid_spec = pltpu.PrefetchScalarGridSpec(
        num_scalar_prefetch=3,
        grid=(rows // tm,),
        in_specs=[row_blk(a_w), row_blk(n_heads), row_blk(n_heads), _resident(ck.shape), row_blk(a_w), row_blk(a_w),
                  row_blk(a_w), row_blk(zp.shape[1]), row_blk(d), _resident((1, d)), _resident(wo.shape),
                  _resident((1, d))],
        out_specs=row_blk(d),
        scratch_shapes=[pltpu.VMEM(kb.shape, BF16),
                        pltpu.VMEM(vb.shape, BF16),
                        pltpu.VMEM((tm, a_w), BF16),
                        pltpu.VMEM((n_heads, ATT_BLK, 1), F32),
                        pltpu.VMEM((n_pairs, ATT_BLK, LANES), F32),
                        pltpu.VMEM((n_pairs, ATT_BLK, LANES), F32)],
    )
    return pl.pallas_call(
        functools.partial(_attn_kernel, tm=tm, n_heads=n_heads),
        grid_spec=grid_spec,
        out_shape=jax.ShapeDtypeStruct((rows, d), F32),
        compiler_params=pltpu.CompilerParams(dimension_semantics=("arbitrary",), vmem_limit_bytes=VMEM_LIMIT),
        name="attn",
    )(tot, qkb, kg, q, cq, nq, ck, kb, vb, sa, zp, x2, gate, wo, fg)


def _sattn_kernel(q_ref, cq_ref, ckn_ref, kn_ref, vn_ref, kc_ref, vc_ref, lfc_ref, sa_ref, zp_ref, x_ref,
                  gate_ref, wo_ref, fg_ref, y_ref, *, n_heads):
    a_w = n_heads * HEAD_DIM
    ln = q_ref.shape[0]
    past = kc_ref.shape[2]
    nb = past // LANES

    lfc = lfc_ref[0] * LOG2E
    triu = (lax.broadcasted_iota(jnp.int32, (LANES, LANES), 0)
            <= lax.broadcasted_iota(jnp.int32, (LANES, LANES), 1)).astype(BF16)
    zeros = jnp.zeros((8, LANES), F32)
    parts = []
    for b in range(nb):
        parts.extend(_split3(lfc[:, b * LANES:(b + 1) * LANES]))
        parts.append(zeros)
    cs = _dot(jnp.concatenate(parts, axis=0).astype(BF16), triu)
    after = jnp.zeros((n_heads, 1), F32)
    suffix = [None] * nb
    for b in reversed(range(nb)):
        cb = cs[32 * b:32 * b + 8] + cs[32 * b + 8:32 * b + 16] + cs[32 * b + 16:32 * b + 24]
        tot = cb[:, LANES - 1:LANES]
        suffix[b] = (tot - cb) + after
        after = after + tot
    dec_c = jnp.concatenate(suffix, axis=1)

    lane = lax.broadcasted_iota(jnp.int32, (ln, LANES), 1)
    half = [lane < HEAD_DIM, lane >= HEAD_DIM]
    causal = lax.broadcasted_iota(jnp.int32, (ln, ln), 1) <= lax.broadcasted_iota(jnp.int32, (ln, ln), 0)
    outs = []
    for g in range(n_heads // PAIR):
        cols = slice(g * LANES, (g + 1) * LANES)
        q2 = q_ref[:, cols]
        kct = kc_ref[0, cols, :].astype(BF16)
        vct = vc_ref[0, cols, :].astype(BF16)
        kn = kn_ref[:, cols]
        vn = vn_ref[:, cols]
        qst = jnp.concatenate([jnp.where(half[e], q2, jnp.zeros_like(q2)) for e in range(PAIR)], axis=0)
        sc_st = _dot(qst, kct)
        sn_st = _dot_nt(qst, kn)
        pc, pn, ls = [], [], []
        for e in range(PAIR):
            hd = PAIR * g + e
            rows = slice(e * ln, (e + 1) * ln)
            cqh = cq_ref[:, hd:hd + 1]
            s_c = sc_st[rows] + cqh + dec_c[hd:hd + 1, :]
            s_n = jnp.where(causal, sn_st[rows] + cqh - ckn_ref[0, hd:hd + 1, :], NEG)
            m = jnp.maximum(jnp.max(s_c, axis=1, keepdims=True), jnp.max(s_n, axis=1, keepdims=True))
            p_c = jnp.exp2(s_c - m)
            p_n = jnp.exp2(s_n - m)
            ls.append(jnp.sum(p_c, axis=1, keepdims=True) + jnp.sum(p_n, axis=1, keepdims=True))
            pc.append(p_c.astype(BF16))
            pn.append(p_n.astype(BF16))
        acc = _dot_nt(jnp.concatenate(pc, axis=0), vct) + _dot(jnp.concatenate(pn, axis=0), vn)
        o = jnp.where(half[0], acc[0:ln] / ls[0], acc[ln:2 * ln] / ls[1])
        outs.append((o * sa_ref[:, cols].astype(F32)).astype(BF16))
    za = jnp.concatenate(outs, axis=1)
    y_ref[...] = _merge_norm(x_ref[...], gate_ref[0], za, zp_ref[...], wo_ref, fg_ref[...], a_w)


def _sample_attention(q, cq, ckn, kb, vb, cache_k, cache_v, lfc, sa, zp, x2, gate, wo, fg, *, ln, n_heads):
    rows, d = x2.shape
    nbatch = rows // ln
    a_w = n_heads * HEAD_DIM
    past = cache_k.shape[2]
    row_blk = lambda w: pl.BlockSpec((ln, w), lambda b: (b, 0))
    per_b = lambda s: pl.BlockSpec((1,) + s, lambda b: (b, 0, 0))
    return pl.pallas_call(
        functools.partial(_sattn_kernel, n_heads=n_heads),
        grid=(nbatch,),
        in_specs=[row_blk(a_w), row_blk(n_heads), per_b((n_heads, ln)), row_blk(a_w), row_blk(a_w),
                  per_b((a_w, past)), per_b((a_w, past)), per_b((n_heads, past)),
                  row_blk(a_w), row_blk(zp.shape[1]), row_blk(d), per_b((1, d)), _resident(wo.shape),
                  _resident((1, d))],
        out_specs=row_blk(d),
        out_shape=jax.ShapeDtypeStruct((rows, d), F32),
        compiler_params=pltpu.CompilerParams(dimension_semantics=("arbitrary",), vmem_limit_bytes=VMEM_LIMIT),
        name="sattn",
    )(q, cq, ckn, kb, vb, cache_k, cache_v, lfc, sa, zp, x2, gate, wo, fg)


def kernel(x_prompt, x_sample, c_prompt, c_sample, cache_k, cache_v, cache_logf, state_pool, norm_g, w_ada, b_ada,
           w_in, b_f, w_pool, pool_scale, w_out, final_g):
    depth = norm_g.shape[0]
    assert depth == 1
    bp, seq, d = x_prompt.shape
    bs, ln, _ = x_sample.shape
    assert bp == 1
    n_heads = cache_k.shape[3]
    past = cache_k.shape[2]
    a_w = n_heads * HEAD_DIM
    pw = state_pool.shape[3]
    assert pw == len(POOL_WINDOWS) * LANES and cache_k.shape[4] == HEAD_DIM and n_heads <= 8

    n_c = bp + bs
    c_all = jnp.concatenate([c_prompt, c_sample, jnp.zeros((16 - n_c, d), F32)], axis=0)
    m_all = _ada_terms(c_all, w_ada[0], b_ada[0][None, :])
    shift, scale, gate = m_all[:, 0:d], m_all[:, d:2 * d], m_all[:, 2 * d:3 * d]
    s1 = (1.0 + scale)[:, None, :]
    sh = shift[:, None, :]

    wit = w_in[0].T
    wq = wit[0:4 * a_w].astype(BF16)
    wf = jnp.pad(wit[4 * a_w:4 * a_w + n_heads], ((0, LANES - n_heads), (0, 0))).astype(BF16)
    wu = wit[4 * a_w + n_heads:].astype(BF16)
    bfp = jnp.pad(b_f[0][None, :], ((0, 0), (0, LANES - n_heads)))
    wp = w_pool[0].astype(BF16)
    ps = pool_scale[0][None, :]
    wo = w_out[0].astype(BF16)
    ng = norm_g[0][None, :]
    fg = final_g[None, :]

    bm = 512
    xp2 = x_prompt.reshape(seq, d)
    hist_p = jnp.zeros((1, HIST_PAD, pw), F32)
    (q_p, k_p, v_p, kb_p, vb_p, sa_p, zp_p, lf_p, cq_p, nq_p, ck_p, st_p, ho_p) = _project(
        xp2, s1[0:1], sh[0:1], ng, wq, wf, bfp, wu, wp, ps, hist_p,
        bm=bm, sb=ATT_BLK, segs=1, start_pos=0, n_heads=n_heads, kv_head_major=False)
    nsb = bm // ATT_BLK
    st = st_p[:, :, 0:nsb, 0:n_heads]
    tot = st[:, 0].reshape(-1)
    qn = st[:, 1].reshape(-1, n_heads)
    kg = jnp.max(st[:, 2].reshape(-1, n_heads), axis=0)
    qkb = (2.0 * qn * kg[None, :]).reshape(-1)
    y_p = _prompt_attention(tot, qkb, kg, q_p, cq_p, nq_p, ck_p, kb_p, vb_p, sa_p, zp_p, xp2, gate[0:1], wo, fg,
                            tm=512, n_heads=n_heads)

    xs2 = x_sample.reshape(bs * ln, d)
    hist_s = jnp.pad(state_pool[0], ((0, 0), (HIST_PAD - POOL_HIST, 0), (0, 0)))
    (q_s, k_s, v_s, kb_s, vb_s, sa_s, zp_s, lf_s, cq_s, _, ck_s, _, ho_s) = _project(
        xs2, s1[bp:n_c], sh[bp:n_c], ng, wq, wf, bfp, wu, wp, ps, hist_s,
        bm=bs * ln, sb=ln, segs=bs, start_pos=past, n_heads=n_heads, kv_head_major=True)
    lfc = jnp.swapaxes(cache_logf[0], 1, 2)
    ckt = jnp.transpose(cache_k[0], (0, 2, 3, 1)).reshape(bs, a_w, past)
    cvt = jnp.transpose(cache_v[0], (0, 2, 3, 1)).reshape(bs, a_w, past)
    y_s = _sample_attention(q_s, cq_s, ck_s, kb_s, vb_s, ckt, cvt, lfc, sa_s, zp_s, xs2, gate[bp:n_c, None, :], wo, fg,
                            ln=ln, n_heads=n_heads)

    hd = (n_heads, HEAD_DIM)
    seq_minor = lambda t: jnp.transpose(t.reshape(hd + (bp, seq)), (2, 3, 0, 1))[None]
    return (y_p.reshape(bp, seq, d), y_s.reshape(bs, ln, d),
            seq_minor(k_p), seq_minor(v_p), jnp.swapaxes(lf_p, 1, 2)[None],
            ho_p[:, 16 - POOL_HIST:, :][None],
            k_s.reshape((1, bs, ln) + hd), v_s.reshape((1, bs, ln) + hd), jnp.swapaxes(lf_s, 1, 2)[None],
            ho_s[:, 16 - POOL_HIST:, :][None])
```

```python
import functools

import jax
import jax.numpy as jnp
from jax import lax
from jax.experimental import pallas as pl
from jax.experimental.pallas import tpu as pltpu

HEAD_DIM = 64
POOL_WINDOWS = (2, 4, 8, 16)
EPS = 1e-6

LANES = 128
PAIR = LANES // HEAD_DIM
ATT_BLK = 128
NEAR_BLOCKS = 3
LOG2E = 1.4426950408889634
EXP2_UNDERFLOW = 151.0
EXP2_SAFE_SPAN = 100.0
NORM_SLACK = 1.01
HIST_PAD = 32
POOL_HIST = max(POOL_WINDOWS) - 1
NEG = -1e30
VMEM_LIMIT = 60 * 1024 * 1024

F32 = jnp.float32
BF16 = jnp.bfloat16


def _silu(x):
    return x * jax.nn.sigmoid(x)


def _dot(a, b):
    return jnp.dot(a, b, preferred_element_type=F32)


def _dot_nt(a, b):
    return lax.dot_general(a, b, (((1,), (1,)), ((), ())), preferred_element_type=F32)


def _split3(x):
    hi = x.astype(BF16).astype(F32)
    r1 = x - hi
    mid = r1.astype(BF16).astype(F32)
    return hi, mid, r1 - mid


def _rows_to_lanes(x, n):
    rows = x.shape[0]
    if rows < LANES:
        x = jnp.concatenate([x, jnp.zeros((LANES - rows, LANES), x.dtype)], axis=0)
    return x.T[0:n, 0:rows]


def _resident(shape):
    return pl.BlockSpec(shape, lambda *_: (0,) * len(shape), pipeline_mode=pl.Buffered(1))


ADA_ROWS = 16
ADA_PROMPT_ROW, ADA_SAMPLE_ROW = 0, 8


def _ada_kernel(cp_ref, cs_ref, w_ref, b_ref, o_ref):
    ap = jnp.broadcast_to(_silu(cp_ref[...]), (ADA_SAMPLE_ROW, cp_ref.shape[1]))
    a = jnp.concatenate([ap, _silu(cs_ref[...])], axis=0).astype(BF16)
    o_ref[0] = _dot(a, w_ref[...].astype(BF16)) + b_ref[...]


def _ada_terms(c_prompt, c_sample, w_ada, b_ada):
    d = c_prompt.shape[1]
    assert c_prompt.shape[0] == 1 and c_sample.shape[0] == ADA_ROWS - ADA_SAMPLE_ROW and w_ada.shape[1] == 3 * d
    return pl.pallas_call(
        _ada_kernel,
        grid=(3,),
        in_specs=[pl.BlockSpec(c_prompt.shape, lambda j: (0, 0)),
                  pl.BlockSpec(c_sample.shape, lambda j: (0, 0)),
                  pl.BlockSpec((d, d), lambda j: (0, j)),
                  pl.BlockSpec((1, d), lambda j: (0, j))],
        out_specs=pl.BlockSpec((1, ADA_ROWS, d), lambda j: (j, 0, 0)),
        out_shape=jax.ShapeDtypeStruct((3, ADA_ROWS, d), F32),
        compiler_params=pltpu.CompilerParams(dimension_semantics=("arbitrary",), vmem_limit_bytes=VMEM_LIMIT),
        name="ada",
    )(c_prompt, c_sample, w_ada, b_ada)


def _proj_kernel(bf_ref, x_ref, ada_ref, ng_ref, wq_ref, wf_ref, wu_ref, wp_ref, ps_ref, h0_ref,
                 q_ref, k32_ref, v32_ref, kb_ref, vb_ref, sa_ref, zp_ref, lf_ref, cq_ref, nq_ref, ck_ref,
                 tot_ref, qmx_ref, kg_ref, ho_ref,
                 e_ref, t2_ref, t4_ref, t8_ref, kmx_ref, *, bm, sb, segs, ada_row, start_pos, n_heads, kv_head_major):
    a_w = n_heads * HEAD_DIM
    seg_rows = bm // segs
    step = pl.program_id(0)
    x = x_ref[...]
    ms = jnp.mean(x * x, axis=-1, keepdims=True)
    xn = x * lax.rsqrt(ms + EPS) * ng_ref[...]
    h = jnp.concatenate(
        [xn[g * seg_rows:(g + 1) * seg_rows] * (1.0 + ada_ref[1, ada_row + g:ada_row + g + 1, :])
         + ada_ref[0, ada_row + g:ada_row + g + 1, :] for g in range(segs)], axis=0).astype(BF16)

    pw = len(POOL_WINDOWS) * LANES
    sc = LOG2E / (HEAD_DIM ** 0.5)
    pu = _dot_nt(h, wu_ref[0:pw, :])
    qs = _dot_nt(h, wq_ref[0:a_w, :]) * sc
    q_ref[...] = qs.astype(BF16)

    ext = HIST_PAD + seg_rows
    n = segs * ext

    def load_history():
        for g in range(segs):
            e_ref[g * ext:g * ext + HIST_PAD, :] = h0_ref[g]

    if segs > 1:
        load_history()
    else:
        pl.when(step == 0)(load_history)

    for g in range(segs):
        e_ref[g * ext + HIST_PAD:(g + 1) * ext, :] = pu[g * seg_rows:(g + 1) * seg_rows]
    t2_ref[8:n, :] = e_ref[8:n, :] + e_ref[7:n - 1, :]
    t4_ref[16:n, :] = t2_ref[16:n, LANES:] + t2_ref[14:n - 2, LANES:]
    t8_ref[24:n, :] = t4_ref[24:n, LANES:] + t4_ref[20:n - 4, LANES:]

    def seg_rows_of(ref, cols, back=0):
        return jnp.concatenate([ref[g * ext + HIST_PAD - back:(g + 1) * ext - back, cols] for g in range(segs)], axis=0)

    lane0, lane1 = slice(0, LANES), slice(LANES, 2 * LANES)
    sums = [seg_rows_of(t2_ref, lane0), seg_rows_of(t4_ref, lane0), seg_rows_of(t8_ref, lane0),
            seg_rows_of(t8_ref, lane1) + seg_rows_of(t8_ref, lane1, back=8)]
    row = lax.broadcasted_iota(jnp.int32, (bm, 1), 0)
    pos1 = start_pos + 1 + (step * bm + row if segs == 1 else row % seg_rows)
    pool_d = []
    for g, w in enumerate(POOL_WINDOWS):
        rc = 1.0 / jnp.minimum(pos1, w).astype(F32)
        pool_d.append((sums[g] * rc - pu[:, g * LANES:(g + 1) * LANES]).astype(BF16))
    for g in range(segs):
        ho_ref[g] = e_ref[(g + 1) * ext - 16:(g + 1) * ext, :]
    if segs == 1:
        e_ref[0:HIST_PAD, :] = e_ref[bm:n, :]

    pk = _dot_nt(h, wq_ref[a_w:2 * a_w, :])
    kb_ref[...] = pk.astype(BF16)
    spg = _silu(_dot_nt(h, wu_ref[pw:2 * pw, :]))

    lane = lax.broadcasted_iota(jnp.int32, (1, LANES), 1)
    bias = jnp.zeros((1, LANES), F32)
    for hd in range(n_heads):
        bias = jnp.where(lane == hd, bf_ref[hd], bias)
    z = _dot_nt(h, wf_ref[...]) + bias
    lf = jnp.minimum(z, 0.0) - jnp.log1p(jnp.exp(-jnp.abs(z)))

    sel = (lax.broadcasted_iota(jnp.int32, (a_w, LANES), 0) // HEAD_DIM
           == lax.broadcasted_iota(jnp.int32, (a_w, LANES), 1)).astype(BF16)
    nq = jnp.sqrt(_dot((qs * qs).astype(BF16), sel)) * NORM_SLACK
    nk = jnp.sqrt(_dot((pk * pk).astype(BF16), sel)) * NORM_SLACK
    nq_ref[...] = nq[:, 0:n_heads]

    pa = _dot_nt(h, wq_ref[3 * a_w:4 * a_w, :])
    sa_ref[...] = _silu(pa).astype(BF16)

    tri = (lax.broadcasted_iota(jnp.int32, (sb, sb), 1)
           <= lax.broadcasted_iota(jnp.int32, (sb, sb), 0)).astype(BF16)
    lf2 = lf * LOG2E
    tots, qmx, kmx = [], [], []
    for s in range(bm // sb):
        rows = slice(s * sb, (s + 1) * sb)
        hi, mid, lo = _split3(lf2[rows])
        cb = _dot(tri, hi.astype(BF16)) + _dot(tri, mid.astype(BF16)) + _dot(tri, lo.astype(BF16))
        cq_ref[rows, :] = cb[:, 0:n_heads]
        ck_ref[s] = _rows_to_lanes(cb, n_heads)
        off = (s * sb) % seg_rows
        lf_ref[(s * sb) // seg_rows, :, off:off + sb] = _rows_to_lanes(lf[rows], n_heads)
        tots.append(cb[sb - 1:sb, :])
        qmx.append(jnp.max(nq[rows], axis=0, keepdims=True))
        kmx.append(jnp.max(nk[rows], axis=0, keepdims=True))
    tot_ref[...] = jnp.concatenate(tots, axis=0)[:, 0:n_heads]
    qmx_ref[...] = jnp.concatenate(qmx, axis=0)[:, 0:n_heads]
    kmax = kmx[0]
    for t in kmx[1:]:
        kmax = jnp.maximum(kmax, t)

    pv = _dot_nt(h, wq_ref[2 * a_w:3 * a_w, :])
    vb_ref[...] = pv.astype(BF16)
    if kv_head_major:
        for hd in range(n_heads):
            k32_ref[:, hd, :] = pk[:, hd * HEAD_DIM:(hd + 1) * HEAD_DIM]
            v32_ref[:, hd, :] = pv[:, hd * HEAD_DIM:(hd + 1) * HEAD_DIM]
    else:
        k32_ref[...] = pk.T
        v32_ref[...] = pv.T

    for g in range(len(POOL_WINDOWS)):
        cols = slice(g * LANES, (g + 1) * LANES)
        y = _dot(pool_d[g], wp_ref[g]) * ps_ref[:, cols]
        zp_ref[:, cols] = (y * spg[:, cols]).astype(BF16)

    @pl.when(step > 0)
    def _():
        kmx_ref[...] = jnp.maximum(kmx_ref[...], kmax)

    @pl.when(step == 0)
    def _():
        kmx_ref[...] = kmax

    kg_ref[...] = kmx_ref[...]


def _project(x2, ada, norm_g, wq, wf, b_f, wu, wp, ps, hist0, *, bm, sb, segs, ada_row, start_pos, n_heads,
             kv_head_major):
    rows, d = x2.shape
    a_w = n_heads * HEAD_DIM
    pw = len(POOL_WINDOWS) * LANES
    n_steps = rows // bm
    assert segs == 1 or n_steps == 1
    n_streams = segs
    seg_rows = bm // segs
    nsb = bm // sb
    assert nsb == 8 and seg_rows % sb == 0
    row_blk = lambda w: pl.BlockSpec((bm, w), lambda i, *_: (i, 0))
    per_stream = lambda r, w: pl.BlockSpec((segs, r, w), lambda i, *_: (0, 0, 0))
    kern = functools.partial(_proj_kernel, bm=bm, sb=sb, segs=segs, ada_row=ada_row, start_pos=start_pos,
                             n_heads=n_heads, kv_head_major=kv_head_major)
    if kv_head_major:
        kv_shape = (rows, n_heads, HEAD_DIM)
        kv_blk = pl.BlockSpec((bm, n_heads, HEAD_DIM), lambda i, *_: (i, 0, 0))
    else:
        kv_shape = (a_w, rows)
        kv_blk = pl.BlockSpec((a_w, bm), lambda i, *_: (0, i))
    out_shape = (
        jax.ShapeDtypeStruct((rows, a_w), BF16),
        jax.ShapeDtypeStruct(kv_shape, F32),
        jax.ShapeDtypeStruct(kv_shape, F32),
        jax.ShapeDtypeStruct((rows, a_w), BF16),
        jax.ShapeDtypeStruct((rows, a_w), BF16),
        jax.ShapeDtypeStruct((rows, a_w), BF16),
        jax.ShapeDtypeStruct((rows, pw), BF16),
        jax.ShapeDtypeStruct((n_streams, n_heads, rows // n_streams), F32),
        jax.ShapeDtypeStruct((rows, n_heads), F32),
        jax.ShapeDtypeStruct((rows, n_heads), F32),
        jax.ShapeDtypeStruct((rows // sb, n_heads, sb), F32),
        jax.ShapeDtypeStruct((rows // sb, n_heads), F32),
        jax.ShapeDtypeStruct((rows // sb, n_heads), F32),
        jax.ShapeDtypeStruct((1, LANES), F32),
        jax.ShapeDtypeStruct((n_streams, 16, pw), F32),
    )
    out_specs = (
        row_blk(a_w), kv_blk, kv_blk, row_blk(a_w), row_blk(a_w), row_blk(a_w), row_blk(pw),
        pl.BlockSpec((segs, n_heads, seg_rows), lambda i, *_: (0, 0, i)),
        row_blk(n_heads), row_blk(n_heads),
        pl.BlockSpec((nsb, n_heads, sb), lambda i, *_: (i, 0, 0)),
        pl.BlockSpec((nsb, n_heads), lambda i, *_: (i, 0)),
        pl.BlockSpec((nsb, n_heads), lambda i, *_: (i, 0)),
        pl.BlockSpec((1, LANES), lambda i, *_: (0, 0)),
        per_stream(16, pw),
    )
    in_specs = [
        row_blk(d),
        _resident(ada.shape),
        _resident((1, d)),
        _resident(wq.shape), _resident(wf.shape), _resident(wu.shape),
        _resident(wp.shape), _resident(ps.shape),
        per_stream(HIST_PAD, pw),
    ]
    return pl.pallas_call(
        kern,
        grid_spec=pltpu.PrefetchScalarGridSpec(
            num_scalar_prefetch=1,
            grid=(n_steps,),
            in_specs=in_specs,
            out_specs=out_specs,
            scratch_shapes=[pltpu.VMEM((bm + segs * HIST_PAD, pw), F32),
                            pltpu.VMEM((bm + segs * HIST_PAD, pw), F32),
                            pltpu.VMEM((bm + segs * HIST_PAD, pw - LANES), F32),
                            pltpu.VMEM((bm + segs * HIST_PAD, pw - 2 * LANES), F32),
                            pltpu.VMEM((1, LANES), F32)]),
        out_shape=out_shape,
        compiler_params=pltpu.CompilerParams(dimension_semantics=("arbitrary",), vmem_limit_bytes=VMEM_LIMIT),
        name="proj",
    )(b_f, x2, ada, norm_g, wq, wf, wu, wp, ps, hist0)


def _merge_norm(x, gate, za, zp, wo_ref, fg, a_w):
    dy = _dot(za, wo_ref[0:a_w, :]) + _dot(zp, wo_ref[a_w:, :])
    out = x + gate * dy
    ms = jnp.mean(out * out, axis=-1, keepdims=True)
    return out * lax.rsqrt(ms + EPS) * fg


def _attn_kernel(tot_ref, qmx_ref, kg_ref,
                 q_ref, cq_ref, nq_ref, ck_ref, knew_ref, vnew_ref, sa_ref, zp_ref, x_ref, ada_ref, wo_ref, fg_ref,
                 y_ref,
                 k_ref, v_ref, z_ref, m_ref, l_ref, acc_ref, *, tm, n_heads):
    a_w = n_heads * HEAD_DIM
    n_pairs = n_heads // PAIR
    nsub = tm // ATT_BLK
    w_near = NEAR_BLOCKS
    cw = w_near * ATT_BLK
    step = pl.program_id(0)
    k_ref[pl.ds(pl.multiple_of(step * tm, tm), tm), :] = knew_ref[...]
    v_ref[pl.ds(pl.multiple_of(step * tm, tm), tm), :] = vnew_ref[...]
    lo_q = lax.broadcasted_iota(jnp.int32, (ATT_BLK, LANES), 1) < HEAD_DIM
    lo_k = lax.broadcasted_iota(jnp.int32, (cw, LANES), 1) < HEAD_DIM
    col = lax.broadcasted_iota(jnp.int32, (ATT_BLK, cw), 1)
    tri = (lax.broadcasted_iota(jnp.int32, (ATT_BLK, ATT_BLK), 1)
           <= lax.broadcasted_iota(jnp.int32, (ATT_BLK, ATT_BLK), 0))
    tri_bias = jnp.where(tri, 0.0, NEG).astype(F32)
    zeros_k = jnp.zeros((cw, LANES), BF16)
    ind_lo = jnp.where(lo_k, 1.0, 0.0).astype(BF16)
    ind_hi = jnp.where(lo_k, 0.0, 1.0).astype(BF16)

    def tot_at(b, hd):
        return jnp.where(b >= 0, tot_ref[jnp.maximum(b, 0), hd], 0.0)

    def rows_of(ref, blocks, g):
        return jnp.concatenate(
            [ref[pl.ds(pl.multiple_of(b * ATT_BLK, ATT_BLK), ATT_BLK), g * LANES:(g + 1) * LANES] for b in blocks],
            axis=0)

    def pair_scores(r0, blocks, g):
        kc = rows_of(k_ref, blocks, g)
        keys = jnp.concatenate([jnp.where(lo_k, kc, zeros_k), jnp.where(lo_k, zeros_k, kc)], axis=0)
        return _dot_nt(q_ref[pl.ds(r0, ATT_BLK), g * LANES:(g + 1) * LANES], keys)

    def pair_values(p_pair, blocks, g):
        vc = rows_of(v_ref, blocks, g)
        vals = jnp.concatenate([jnp.concatenate([jnp.where(lo_k, vc, zeros_k), ind_lo], axis=1),
                                jnp.concatenate([jnp.where(lo_k, zeros_k, vc), ind_hi], axis=1)], axis=0)
        return _dot(p_pair, vals)

    def decay_row(blocks, offs, hd):
        return jnp.concatenate([offs[p] - ck_ref[blocks[p], hd:hd + 1, :] for p in range(w_near)], axis=1)

    def sub_body(sub, carry):
        i = step * nsub + sub
        r0 = pl.multiple_of(sub * ATT_BLK, ATT_BLK)

        rows = pl.ds(r0, ATT_BLK)
        qk = [2.0 * qmx_ref[i, hd] * kg_ref[0, hd] for hd in range(n_heads)]

        near = [i - (w_near - 1) + p for p in range(w_near)]
        near_c = [jnp.maximum(b, 0) for b in near]
        near_offs, offs_far = [], []
        for hd in range(n_heads):
            back = [tot_at(i - dd, hd) for dd in range(1, w_near)]
            offs = []
            for p in range(w_near):
                o = jnp.float32(0.0)
                for dd in range(1, w_near - p):
                    o = o + back[dd - 1]
                offs.append(jnp.where(near[p] >= 0, o, NEG))
            near_offs.append(offs)
            o = jnp.float32(0.0)
            for t in back:
                o = o + t
            offs_far.append(o)

        def gated_output(g, acc, l):
            cols = slice(g * LANES, (g + 1) * LANES)
            z_ref[rows, cols] = ((acc / l) * sa_ref[rows, cols].astype(F32)).astype(BF16)

        def near_chunk(bounded):
            def fn():
                s_next = pair_scores(r0, near_c, 0)
                for g in range(n_pairs):
                    s_pair = s_next
                    if g + 1 < n_pairs:
                        s_next = pair_scores(r0, near_c, g + 1)
                    ps = []
                    for e in range(PAIR):
                        hd = PAIR * g + e
                        cqh = cq_ref[rows, hd:hd + 1]
                        if bounded:
                            m = nq_ref[rows, hd:hd + 1] * kg_ref[0, hd]
                            cqh = cqh - m
                        dec = decay_row(near_c, near_offs[hd], hd)
                        pieces = []
                        for p in range(w_near):
                            lanes = slice(e * cw + p * ATT_BLK, e * cw + (p + 1) * ATT_BLK)
                            sp = s_pair[:, lanes] + cqh + dec[:, p * ATT_BLK:(p + 1) * ATT_BLK]
                            pieces.append(sp + tri_bias if p == w_near - 1 else sp)
                        s = jnp.concatenate(pieces, axis=1)
                        if not bounded:
                            m = jnp.max(s, axis=1, keepdims=True)
                            s = s - m
                        m_ref[hd] = m
                        ps.append(jnp.exp2(s).astype(BF16))
                    r = pair_values(jnp.concatenate(ps, axis=1), near_c, g)
                    acc_ref[g] = r[:, 0:LANES]
                    l_ref[g] = r[:, LANES:]
                    gated_output(g, r[:, 0:LANES], r[:, LANES:])
            return fn

        bounded_ok = qk[0] <= EXP2_SAFE_SPAN
        for hd in range(1, n_heads):
            bounded_ok = jnp.logical_and(bounded_ok, qk[hd] <= EXP2_SAFE_SPAN)
        pl.when(bounded_ok)(near_chunk(True))
        pl.when(jnp.logical_not(bounded_ok))(near_chunk(False))

        def far_cond(c):
            top = i - c[0] * w_near
            need = qk[0] + c[1] >= -EXP2_UNDERFLOW
            for hd in range(1, n_heads):
                need = jnp.logical_or(need, qk[hd] + c[1 + hd] >= -EXP2_UNDERFLOW)
            return jnp.logical_and(top >= 0, need)

        def far_body(c):
            top = i - c[0] * w_near
            jc = jnp.maximum(top - (w_near - 1), 0)
            blocks = [jc + p for p in range(w_near)]
            keepc = col < (top + 1 - jc) * ATT_BLK
            new = [c[0] + 1]
            for g in range(n_pairs):
                s_pair = pair_scores(r0, blocks, g)
                ps, alphas = [], []
                for e in range(PAIR):
                    hd = PAIR * g + e
                    tt = [tot_at(top - b, hd) for b in range(w_near)]
                    offs = []
                    for p in range(w_near):
                        behind = top - (jc + p)
                        o = c[1 + hd]
                        for b in range(w_near):
                            o = o + jnp.where(behind >= b, tt[b], 0.0)
                        offs.append(o)
                    s = s_pair[:, e * cw:(e + 1) * cw] + cq_ref[rows, hd:hd + 1] + decay_row(blocks, offs, hd)
                    s = jnp.where(keepc, s, NEG)
                    m_old = m_ref[hd]
                    m_new = jnp.maximum(m_old, jnp.max(s, axis=1, keepdims=True))
                    m_ref[hd] = m_new
                    alphas.append(jnp.broadcast_to(jnp.exp2(m_old - m_new), (ATT_BLK, LANES)))
                    ps.append(jnp.exp2(s - m_new).astype(BF16))
                    o = c[1 + hd]
                    for t in tt:
                        o = o + t
                    new.append(o)
                alpha = jnp.where(lo_q, alphas[0], alphas[1])
                r = pair_values(jnp.concatenate(ps, axis=1), blocks, g)
                acc_ref[g] = alpha * acc_ref[g] + r[:, 0:LANES]
                l_ref[g] = alpha * l_ref[g] + r[:, LANES:]
            return tuple(new)

        far = lax.while_loop(far_cond, far_body, (jnp.int32(1),) + tuple(offs_far))

        @pl.when(far[0] > 1)
        def _():
            for g in range(n_pairs):
                gated_output(g, acc_ref[g], l_ref[g])
        return carry

    lax.fori_loop(0, nsub, sub_body, 0)
    y_ref[...] = _merge_norm(x_ref[...], ada_ref[2, ADA_PROMPT_ROW:ADA_PROMPT_ROW + 1, :], z_ref[...], zp_ref[...], wo_ref, fg_ref[...], a_w)


def _prompt_attention(tot, qmx, kg, q, cq, nq, ck, kb, vb, sa, zp, x2, ada, wo, fg, *, tm, n_heads):
    rows, d = x2.shape
    a_w = n_heads * HEAD_DIM
    n_pairs = n_heads // PAIR
    assert rows % tm == 0 and tm % ATT_BLK == 0 and rows // ATT_BLK >= NEAR_BLOCKS and n_heads % PAIR == 0
    row_blk = lambda w: pl.BlockSpec((tm, w), lambda i, *_: (i, 0))
    grid_spec = pltpu.PrefetchScalarGridSpec(
        num_scalar_prefetch=3,
        grid=(rows // tm,),
        in_specs=[row_blk(a_w), row_blk(n_heads), row_blk(n_heads), _resident(ck.shape), row_blk(a_w), row_blk(a_w),
                  row_blk(a_w), row_blk(zp.shape[1]), row_blk(d), _resident(ada.shape), _resident(wo.shape),
                  _resident((1, d))],
        out_specs=row_blk(d),
        scratch_shapes=[pltpu.VMEM(kb.shape, BF16),
                        pltpu.VMEM(vb.shape, BF16),
                        pltpu.VMEM((tm, a_w), BF16),
                        pltpu.VMEM((n_heads, ATT_BLK, 1), F32),
                        pltpu.VMEM((n_pairs, ATT_BLK, LANES), F32),
                        pltpu.VMEM((n_pairs, ATT_BLK, LANES), F32)],
    )
    return pl.pallas_call(
        functools.partial(_attn_kernel, tm=tm, n_heads=n_heads),
        grid_spec=grid_spec,
        out_shape=jax.ShapeDtypeStruct((rows, d), F32),
        compiler_params=pltpu.CompilerParams(dimension_semantics=("arbitrary",), vmem_limit_bytes=VMEM_LIMIT),
        name="attn",
    )(tot, qmx, kg, q, cq, nq, ck, kb, vb, sa, zp, x2, ada, wo, fg)


def _sattn_kernel(q_ref, cq_ref, ckn_ref, kn_ref, vn_ref, kc_ref, vc_ref, lfc_ref, sa_ref, zp_ref, x_ref,
                  ada_ref, wo_ref, fg_ref, y_ref, *, n_heads):
    a_w = n_heads * HEAD_DIM
    ln = q_ref.shape[0]
    past = kc_ref.shape[2]
    nb = past // LANES

    lfc = lfc_ref[0] * LOG2E
    triu = (lax.broadcasted_iota(jnp.int32, (LANES, LANES), 0)
            <= lax.broadcasted_iota(jnp.int32, (LANES, LANES), 1)).astype(BF16)
    zeros = jnp.zeros((8, LANES), F32)
    parts = []
    for b in range(nb):
        parts.extend(_split3(lfc[:, b * LANES:(b + 1) * LANES]))
        parts.append(zeros)
    cs = _dot(jnp.concatenate(parts, axis=0).astype(BF16), triu)
    after = jnp.zeros((n_heads, 1), F32)
    suffix = [None] * nb
    for b in reversed(range(nb)):
        cb = cs[32 * b:32 * b + 8] + cs[32 * b + 8:32 * b + 16] + cs[32 * b + 16:32 * b + 24]
        tot = cb[:, LANES - 1:LANES]
        suffix[b] = (tot - cb) + after
        after = after + tot
    dec_c = jnp.concatenate(suffix, axis=1)

    lane = lax.broadcasted_iota(jnp.int32, (ln, LANES), 1)
    half = [lane < HEAD_DIM, lane >= HEAD_DIM]
    causal = lax.broadcasted_iota(jnp.int32, (ln, ln), 1) <= lax.broadcasted_iota(jnp.int32, (ln, ln), 0)
    outs = []
    for g in range(n_heads // PAIR):
        cols = slice(g * LANES, (g + 1) * LANES)
        q2 = q_ref[:, cols]
        kct = kc_ref[0, cols, :].astype(BF16)
        vct = vc_ref[0, cols, :].astype(BF16)
        kn = kn_ref[:, cols]
        vn = vn_ref[:, cols]
        qst = jnp.concatenate([jnp.where(half[e], q2, jnp.zeros_like(q2)) for e in range(PAIR)], axis=0)
        sc_st = _dot(qst, kct)
        sn_st = _dot_nt(qst, kn)
        pc, pn, ls = [], [], []
        for e in range(PAIR):
            hd = PAIR * g + e
            rows = slice(e * ln, (e + 1) * ln)
            cqh = cq_ref[:, hd:hd + 1]
            s_c = sc_st[rows] + cqh + dec_c[hd:hd + 1, :]
            s_n = jnp.where(causal, sn_st[rows] + cqh - ckn_ref[0, hd:hd + 1, :], NEG)
            m = jnp.maximum(jnp.max(s_c, axis=1, keepdims=True), jnp.max(s_n, axis=1, keepdims=True))
            p_c = jnp.exp2(s_c - m)
            p_n = jnp.exp2(s_n - m)
            ls.append(jnp.sum(p_c, axis=1, keepdims=True) + jnp.sum(p_n, axis=1, keepdims=True))
            pc.append(p_c.astype(BF16))
            pn.append(p_n.astype(BF16))
        acc = _dot_nt(jnp.concatenate(pc, axis=0), vct) + _dot(jnp.concatenate(pn, axis=0), vn)
        o = jnp.where(half[0], acc[0:ln] / ls[0], acc[ln:2 * ln] / ls[1])
        outs.append((o * sa_ref[:, cols].astype(F32)).astype(BF16))
    za = jnp.concatenate(outs, axis=1)
    y_ref[...] = _merge_norm(x_ref[...], ada_ref[2, pl.ds(ADA_SAMPLE_ROW + pl.program_id(0), 1), :], za, zp_ref[...], wo_ref, fg_ref[...], a_w)


def _sample_attention(q, cq, ckn, kb, vb, cache_k, cache_v, lfc, sa, zp, x2, ada, wo, fg, *, ln, n_heads):
    rows, d = x2.shape
    nbatch = rows // ln
    a_w = n_heads * HEAD_DIM
    past = cache_k.shape[2]
    row_blk = lambda w: pl.BlockSpec((ln, w), lambda b: (b, 0))
    per_b = lambda s: pl.BlockSpec((1,) + s, lambda b: (b, 0, 0))
    return pl.pallas_call(
        functools.partial(_sattn_kernel, n_heads=n_heads),
        grid=(nbatch,),
        in_specs=[row_blk(a_w), row_blk(n_heads), per_b((n_heads, ln)), row_blk(a_w), row_blk(a_w),
                  per_b((a_w, past)), per_b((a_w, past)), per_b((n_heads, past)),
                  row_blk(a_w), row_blk(zp.shape[1]), row_blk(d), _resident(ada.shape), _resident(wo.shape),
                  _resident((1, d))],
        out_specs=row_blk(d),
        out_shape=jax.ShapeDtypeStruct((rows, d), F32),
        compiler_params=pltpu.CompilerParams(dimension_semantics=("arbitrary",), vmem_limit_bytes=VMEM_LIMIT),
        name="sattn",
    )(q, cq, ckn, kb, vb, cache_k, cache_v, lfc, sa, zp, x2, ada, wo, fg)


def kernel(x_prompt, x_sample, c_prompt, c_sample, cache_k, cache_v, cache_logf, state_pool, norm_g, w_ada, b_ada,
           w_in, b_f, w_pool, pool_scale, w_out, final_g):
    depth = norm_g.shape[0]
    assert depth == 1
    bp, seq, d = x_prompt.shape
    bs, ln, _ = x_sample.shape
    assert bp == 1
    n_heads = cache_k.shape[3]
    past = cache_k.shape[2]
    a_w = n_heads * HEAD_DIM
    pw = state_pool.shape[3]
    assert pw == len(POOL_WINDOWS) * LANES and cache_k.shape[4] == HEAD_DIM and n_heads <= 8

    ada = _ada_terms(c_prompt, c_sample, w_ada[0], b_ada)

    wit = w_in[0].T
    wq = wit[0:4 * a_w].astype(BF16)
    wf = jnp.pad(wit[4 * a_w:4 * a_w + n_heads], ((0, LANES - n_heads), (0, 0))).astype(BF16)
    wu = wit[4 * a_w + n_heads:].astype(BF16)
    wp = w_pool[0].astype(BF16)
    ps = pool_scale[0][None, :]
    wo = w_out[0].astype(BF16)
    ng = norm_g[0][None, :]
    fg = final_g[None, :]

    bm = 1024
    xp2 = x_prompt.reshape(seq, d)
    hist_p = jnp.zeros((1, HIST_PAD, pw), F32)
    (q_p, k_p, v_p, kb_p, vb_p, sa_p, zp_p, lf_p, cq_p, nq_p, ck_p, tot, qmx, kg, ho_p) = _project(
        xp2, ada, ng, wq, wf, b_f[0], wu, wp, ps, hist_p,
        bm=bm, sb=ATT_BLK, segs=1, ada_row=ADA_PROMPT_ROW, start_pos=0, n_heads=n_heads, kv_head_major=False)
    y_p = _prompt_attention(tot, qmx, kg, q_p, cq_p, nq_p, ck_p, kb_p, vb_p, sa_p, zp_p, xp2, ada, wo, fg,
                            tm=512, n_heads=n_heads)

    xs2 = x_sample.reshape(bs * ln, d)
    hist_s = jnp.pad(state_pool[0], ((0, 0), (HIST_PAD - POOL_HIST, 0), (0, 0)))
    (q_s, k_s, v_s, kb_s, vb_s, sa_s, zp_s, lf_s, cq_s, _, ck_s, _, _, _, ho_s) = _project(
        xs2, ada, ng, wq, wf, b_f[0], wu, wp, ps, hist_s,
        bm=bs * ln, sb=ln, segs=bs, ada_row=ADA_SAMPLE_ROW, start_pos=past, n_heads=n_heads, kv_head_major=True)
    lfc = jnp.swapaxes(cache_logf[0], 1, 2)
    ckt = jnp.transpose(cache_k[0], (0, 2, 3, 1)).reshape(bs, a_w, past)
    cvt = jnp.transpose(cache_v[0], (0, 2, 3, 1)).reshape(bs, a_w, past)
    y_s = _sample_attention(q_s, cq_s, ck_s, kb_s, vb_s, ckt, cvt, lfc, sa_s, zp_s, xs2, ada, wo, fg,
                            ln=ln, n_heads=n_heads)

    hd = (n_heads, HEAD_DIM)
    seq_minor = lambda t: jnp.transpose(t.reshape(hd + (bp, seq)), (2, 3, 0, 1))[None]
    return (y_p.reshape(bp, seq, d), y_s.reshape(bs, ln, d),
            seq_minor(k_p), seq_minor(v_p), jnp.swapaxes(lf_p, 1, 2)[None],
            ho_p[:, 16 - POOL_HIST:, :][None],
            k_s.reshape((1, bs, ln) + hd), v_s.reshape((1, bs, ln) + hd), jnp.swapaxes(lf_s, 1, 2)[None],
            ho_s[:, 16 - POOL_HIST:, :][None])
```

```python
import functools

import jax
import jax.numpy as jnp
from jax import lax
from jax.experimental import pallas as pl
from jax.experimental.pallas import tpu as pltpu

HEAD_DIM = 64
POOL_WINDOWS = (2, 4, 8, 16)
EPS = 1e-6

LANES = 128
PAIR = LANES // HEAD_DIM
ATT_BLK = 128
NEAR_BLOCKS = 3
LOG2E = 1.4426950408889634
EXP2_UNDERFLOW = 151.0
EXP2_SAFE_SPAN = 100.0
NORM_SLACK = 1.01
HIST_PAD = 32
POOL_HIST = max(POOL_WINDOWS) - 1
NEG = -1e30
VMEM_LIMIT = 60 * 1024 * 1024

F32 = jnp.float32
BF16 = jnp.bfloat16


def _silu(x):
    return x * jax.nn.sigmoid(x)


def _dot(a, b):
    return jnp.dot(a, b, preferred_element_type=F32)


def _dot_nt(a, b):
    return lax.dot_general(a, b, (((1,), (1,)), ((), ())), preferred_element_type=F32)


def _split3(x):
    hi = x.astype(BF16).astype(F32)
    r1 = x - hi
    mid = r1.astype(BF16).astype(F32)
    return hi, mid, r1 - mid


def _rows_to_lanes(x, n):
    rows = x.shape[0]
    if rows < LANES:
        x = jnp.concatenate([x, jnp.zeros((LANES - rows, LANES), x.dtype)], axis=0)
    return x.T[0:n, 0:rows]


def _resident(shape):
    return pl.BlockSpec(shape, lambda *_: (0,) * len(shape), pipeline_mode=pl.Buffered(1))


ADA_ROWS = 16
ADA_PROMPT_ROW, ADA_SAMPLE_ROW = 0, 8


def _ada_kernel(cp_ref, cs_ref, w_ref, b_ref, o_ref):
    ap = jnp.broadcast_to(_silu(cp_ref[...]), (ADA_SAMPLE_ROW, cp_ref.shape[1]))
    a = jnp.concatenate([ap, _silu(cs_ref[...])], axis=0).astype(BF16)
    o_ref[0] = _dot(a, w_ref[...].astype(BF16)) + b_ref[...]


def _ada_terms(c_prompt, c_sample, w_ada, b_ada):
    d = c_prompt.shape[1]
    assert c_prompt.shape[0] == 1 and c_sample.shape[0] == ADA_ROWS - ADA_SAMPLE_ROW and w_ada.shape[1] == 3 * d
    return pl.pallas_call(
        _ada_kernel,
        grid=(3,),
        in_specs=[pl.BlockSpec(c_prompt.shape, lambda j: (0, 0)),
                  pl.BlockSpec(c_sample.shape, lambda j: (0, 0)),
                  pl.BlockSpec((d, d), lambda j: (0, j)),
                  pl.BlockSpec((1, d), lambda j: (0, j))],
        out_specs=pl.BlockSpec((1, ADA_ROWS, d), lambda j: (j, 0, 0)),
        out_shape=jax.ShapeDtypeStruct((3, ADA_ROWS, d), F32),
        compiler_params=pltpu.CompilerParams(dimension_semantics=("arbitrary",), vmem_limit_bytes=VMEM_LIMIT),
        name="ada",
    )(c_prompt, c_sample, w_ada, b_ada)


def _proj_kernel(bf_ref, x_ref, ada_ref, ng_ref, wq_ref, wf_ref, wu_ref, wp_ref, ps_ref, h0_ref,
                 q_ref, k32_ref, v32_ref, kb_ref, vb_ref, sa_ref, zp_ref, lf_ref, cq_ref, nq_ref, ck_ref,
                 tot_ref, qmx_ref, kg_ref, ho_ref,
                 e_ref, t2_ref, t4_ref, t8_ref, kmx_ref, *, bm, sb, segs, ada_row, start_pos, n_heads, kv_head_major):
    a_w = n_heads * HEAD_DIM
    seg_rows = bm // segs
    step = pl.program_id(0)
    x = x_ref[...]
    ms = jnp.mean(x * x, axis=-1, keepdims=True)
    xn = x * lax.rsqrt(ms + EPS) * ng_ref[...]
    h = jnp.concatenate(
        [xn[g * seg_rows:(g + 1) * seg_rows] * (1.0 + ada_ref[1, ada_row + g:ada_row + g + 1, :])
         + ada_ref[0, ada_row + g:ada_row + g + 1, :] for g in range(segs)], axis=0).astype(BF16)

    pw = len(POOL_WINDOWS) * LANES
    sc = LOG2E / (HEAD_DIM ** 0.5)
    pu = _dot_nt(h, wu_ref[0:pw, :])
    qs = _dot_nt(h, wq_ref[0:a_w, :]) * sc
    q_ref[...] = qs.astype(BF16)
    sel = (lax.broadcasted_iota(jnp.int32, (a_w, LANES), 0) // HEAD_DIM
           == lax.broadcasted_iota(jnp.int32, (a_w, LANES), 1)).astype(BF16)
    nq = jnp.sqrt(_dot((qs * qs).astype(BF16), sel)) * NORM_SLACK
    nq_ref[...] = nq[:, 0:n_heads]

    def store_kv(ref32, refb, p):
        refb[...] = p.astype(BF16)
        if kv_head_major:
            for hd in range(n_heads):
                ref32[:, hd, :] = p[:, hd * HEAD_DIM:(hd + 1) * HEAD_DIM]
        else:
            ref32[...] = p.T

    ext = HIST_PAD + seg_rows
    n = segs * ext

    def load_history():
        for g in range(segs):
            e_ref[g * ext:g * ext + HIST_PAD, :] = h0_ref[g]

    if segs > 1:
        load_history()
    else:
        pl.when(step == 0)(load_history)

    for g in range(segs):
        e_ref[g * ext + HIST_PAD:(g + 1) * ext, :] = pu[g * seg_rows:(g + 1) * seg_rows]
    t2_ref[8:n, :] = e_ref[8:n, :] + e_ref[7:n - 1, :]
    t4_ref[16:n, :] = t2_ref[16:n, LANES:] + t2_ref[14:n - 2, LANES:]
    t8_ref[24:n, :] = t4_ref[24:n, LANES:] + t4_ref[20:n - 4, LANES:]

    def seg_rows_of(ref, cols, back=0):
        return jnp.concatenate([ref[g * ext + HIST_PAD - back:(g + 1) * ext - back, cols] for g in range(segs)], axis=0)

    lane0, lane1 = slice(0, LANES), slice(LANES, 2 * LANES)
    sums = [seg_rows_of(t2_ref, lane0), seg_rows_of(t4_ref, lane0), seg_rows_of(t8_ref, lane0),
            seg_rows_of(t8_ref, lane1) + seg_rows_of(t8_ref, lane1, back=8)]
    row = lax.broadcasted_iota(jnp.int32, (bm, 1), 0)
    pos1 = start_pos + 1 + (step * bm + row if segs == 1 else row % seg_rows)
    pool_d = []
    for g, w in enumerate(POOL_WINDOWS):
        rc = 1.0 / jnp.minimum(pos1, w).astype(F32)
        pool_d.append((sums[g] * rc - pu[:, g * LANES:(g + 1) * LANES]).astype(BF16))
    for g in range(segs):
        ho_ref[g] = e_ref[(g + 1) * ext - 16:(g + 1) * ext, :]
    if segs == 1:
        e_ref[0:HIST_PAD, :] = e_ref[bm:n, :]

    pk = _dot_nt(h, wq_ref[a_w:2 * a_w, :])
    store_kv(k32_ref, kb_ref, pk)
    nk = jnp.sqrt(_dot((pk * pk).astype(BF16), sel)) * NORM_SLACK
    spg = _silu(_dot_nt(h, wu_ref[pw:2 * pw, :]))

    lane = lax.broadcasted_iota(jnp.int32, (1, LANES), 1)
    bias = jnp.zeros((1, LANES), F32)
    for hd in range(n_heads):
        bias = jnp.where(lane == hd, bf_ref[hd], bias)
    z = _dot_nt(h, wf_ref[...]) + bias
    lf = jnp.minimum(z, 0.0) - jnp.log1p(jnp.exp(-jnp.abs(z)))

    pa =_dot_nt(h, wq_ref[3 * a_w:4 * a_w, :])
    sa_ref[...] = _silu(pa).astype(BF16)

    tri = (lax.broadcasted_iota(jnp.int32, (sb, sb), 1)
           <= lax.broadcasted_iota(jnp.int32, (sb, sb), 0)).astype(BF16)
    lf2 = lf * LOG2E
    tots, qmx, kmx = [], [], []
    for s in range(bm // sb):
        rows = slice(s * sb, (s + 1) * sb)
        hi, mid, lo = _split3(lf2[rows])
        cb = _dot(tri, hi.astype(BF16)) + _dot(tri, mid.astype(BF16)) + _dot(tri, lo.astype(BF16))
        cq_ref[rows, :] = cb[:, 0:n_heads]
        ck_ref[s] = _rows_to_lanes(cb, n_heads)
        off = (s * sb) % seg_rows
        lf_ref[(s * sb) // seg_rows, :, off:off + sb] = _rows_to_lanes(lf[rows], n_heads)
        tots.append(cb[sb - 1:sb, :])
        qmx.append(jnp.max(nq[rows], axis=0, keepdims=True))
        kmx.append(jnp.max(nk[rows], axis=0, keepdims=True))
    tot_ref[...] = jnp.concatenate(tots, axis=0)[:, 0:n_heads]
    qmx_ref[...] = jnp.concatenate(qmx, axis=0)[:, 0:n_heads]
    kmax = kmx[0]
    for t in kmx[1:]:
        kmax = jnp.maximum(kmax, t)

    store_kv(v32_ref, vb_ref, _dot_nt(h, wq_ref[2 * a_w:3 * a_w, :]))

    for g in range(len(POOL_WINDOWS)):
        cols = slice(g * LANES, (g + 1) * LANES)
        y = _dot(pool_d[g], wp_ref[g]) * ps_ref[:, cols]
        zp_ref[:, cols] = (y * spg[:, cols]).astype(BF16)

    @pl.when(step > 0)
    def _():
        kmx_ref[...] = jnp.maximum(kmx_ref[...], kmax)

    @pl.when(step == 0)
    def _():
        kmx_ref[...] = kmax

    kg_ref[...] = kmx_ref[...]


def _project(x2, ada, norm_g, wq, wf, b_f, wu, wp, ps, hist0, *, bm, sb, segs, ada_row, start_pos, n_heads,
             kv_head_major):
    rows, d = x2.shape
    a_w = n_heads * HEAD_DIM
    pw = len(POOL_WINDOWS) * LANES
    n_steps = rows // bm
    assert segs == 1 or n_steps == 1
    n_streams = segs
    seg_rows = bm // segs
    nsb = bm // sb
    assert nsb == 8 and seg_rows % sb == 0
    row_blk = lambda w: pl.BlockSpec((bm, w), lambda i, *_: (i, 0))
    per_stream = lambda r, w: pl.BlockSpec((segs, r, w), lambda i, *_: (0, 0, 0))
    kern = functools.partial(_proj_kernel, bm=bm, sb=sb, segs=segs, ada_row=ada_row, start_pos=start_pos,
                             n_heads=n_heads, kv_head_major=kv_head_major)
    if kv_head_major:
        kv_shape = (rows, n_heads, HEAD_DIM)
        kv_blk = pl.BlockSpec((bm, n_heads, HEAD_DIM), lambda i, *_: (i, 0, 0))
    else:
        kv_shape = (a_w, rows)
        kv_blk = pl.BlockSpec((a_w, bm), lambda i, *_: (0, i))
    out_shape = (
        jax.ShapeDtypeStruct((rows, a_w), BF16),
        jax.ShapeDtypeStruct(kv_shape, F32),
        jax.ShapeDtypeStruct(kv_shape, F32),
        jax.ShapeDtypeStruct((rows, a_w), BF16),
        jax.ShapeDtypeStruct((rows, a_w), BF16),
        jax.ShapeDtypeStruct((rows, a_w), BF16),
        jax.ShapeDtypeStruct((rows, pw), BF16),
        jax.ShapeDtypeStruct((n_streams, n_heads, rows // n_streams), F32),
        jax.ShapeDtypeStruct((rows, n_heads), F32),
        jax.ShapeDtypeStruct((rows, n_heads), F32),
        jax.ShapeDtypeStruct((rows // sb, n_heads, sb), F32),
        jax.ShapeDtypeStruct((rows // sb, n_heads), F32),
        jax.ShapeDtypeStruct((rows // sb, n_heads), F32),
        jax.ShapeDtypeStruct((1, LANES), F32),
        jax.ShapeDtypeStruct((n_streams, 16, pw), F32),
    )
    out_specs = (
        row_blk(a_w), kv_blk, kv_blk, row_blk(a_w), row_blk(a_w), row_blk(a_w), row_blk(pw),
        pl.BlockSpec((segs, n_heads, seg_rows), lambda i, *_: (0, 0, i)),
        row_blk(n_heads), row_blk(n_heads),
        pl.BlockSpec((nsb, n_heads, sb), lambda i, *_: (i, 0, 0)),
        pl.BlockSpec((nsb, n_heads), lambda i, *_: (i, 0)),
        pl.BlockSpec((nsb, n_heads), lambda i, *_: (i, 0)),
        pl.BlockSpec((1, LANES), lambda i, *_: (0, 0)),
        per_stream(16, pw),
    )
    in_specs = [
        row_blk(d),
        _resident(ada.shape),
        _resident((1, d)),
        _resident(wq.shape), _resident(wf.shape), _resident(wu.shape),
        _resident(wp.shape), _resident(ps.shape),
        per_stream(HIST_PAD, pw),
    ]
    return pl.pallas_call(
        kern,
        grid_spec=pltpu.PrefetchScalarGridSpec(
            num_scalar_prefetch=1,
            grid=(n_steps,),
            in_specs=in_specs,
            out_specs=out_specs,
            scratch_shapes=[pltpu.VMEM((bm + segs * HIST_PAD, pw), F32),
                            pltpu.VMEM((bm + segs * HIST_PAD, pw), F32),
                            pltpu.VMEM((bm + segs * HIST_PAD, pw - LANES), F32),
                            pltpu.VMEM((bm + segs * HIST_PAD, pw - 2 * LANES), F32),
                            pltpu.VMEM((1, LANES), F32)]),
        out_shape=out_shape,
        compiler_params=pltpu.CompilerParams(dimension_semantics=("arbitrary",), vmem_limit_bytes=VMEM_LIMIT),
        name="proj",
    )(b_f, x2, ada, norm_g, wq, wf, wu, wp, ps, hist0)


def _merge_norm(x, gate, za, zp, wo_ref, fg, a_w):
    dy = _dot(za, wo_ref[0:a_w, :]) + _dot(zp, wo_ref[a_w:, :])
    out = x + gate * dy
    ms = jnp.mean(out * out, axis=-1, keepdims=True)
    return out * lax.rsqrt(ms + EPS) * fg


def _attn_kernel(tot_ref, qmx_ref, kg_ref,
                 q_ref, cq_ref, nq_ref, ck_ref, knew_ref, vnew_ref, sa_ref, zp_ref, x_ref, ada_ref, wo_ref, fg_ref,
                 y_ref,
                 k_ref, v_ref, z_ref, m_ref, l_ref, acc_ref, straight_ref, *, tm, n_heads):
    a_w = n_heads * HEAD_DIM
    n_pairs = n_heads // PAIR
    nsub = tm // ATT_BLK
    w_near = NEAR_BLOCKS
    cw = w_near * ATT_BLK
    step = pl.program_id(0)

    def keep_step_rows():
        k_ref[pl.ds(pl.multiple_of(step * tm, tm), tm), :] = knew_ref[...]
        v_ref[pl.ds(pl.multiple_of(step * tm, tm), tm), :] = vnew_ref[...]

    lo_q = lax.broadcasted_iota(jnp.int32, (ATT_BLK, LANES), 1) < HEAD_DIM
    lo_k = lax.broadcasted_iota(jnp.int32, (cw, LANES), 1) < HEAD_DIM
    col = lax.broadcasted_iota(jnp.int32, (ATT_BLK, cw), 1)
    tri = (lax.broadcasted_iota(jnp.int32, (ATT_BLK, ATT_BLK), 1)
           <= lax.broadcasted_iota(jnp.int32, (ATT_BLK, ATT_BLK), 0))
    tri_bias = jnp.where(tri, 0.0, NEG).astype(F32)
    zeros_k = jnp.zeros((cw, LANES), BF16)
    ind_lo = jnp.where(lo_k, 1.0, 0.0).astype(BF16)
    ind_hi = jnp.where(lo_k, 0.0, 1.0).astype(BF16)

    def tot_at(b, hd):
        return jnp.where(b >= 0, tot_ref[jnp.maximum(b, 0), hd], 0.0)

    def rows_of(ref, blocks, g):
        return jnp.concatenate(
            [ref[pl.ds(pl.multiple_of(b * ATT_BLK, ATT_BLK), ATT_BLK), g * LANES:(g + 1) * LANES] for b in blocks],
            axis=0)

    def step_rows_of(ref, new_ref, sub, g):
        parts = []
        for p in range(w_near):
            rel = sub - (w_near - 1) + p
            if rel >= 0:
                parts.append(new_ref[rel * ATT_BLK:(rel + 1) * ATT_BLK, g * LANES:(g + 1) * LANES])
            else:
                start = pl.multiple_of((step * nsub + rel) * ATT_BLK, ATT_BLK)
                parts.append(ref[pl.ds(start, ATT_BLK), g * LANES:(g + 1) * LANES])
        return jnp.concatenate(parts, axis=0)

    def pair_scores(r0, kc, g):
        keys = jnp.concatenate([jnp.where(lo_k, kc, zeros_k), jnp.where(lo_k, zeros_k, kc)], axis=0)
        return _dot_nt(q_ref[pl.ds(r0, ATT_BLK), g * LANES:(g + 1) * LANES], keys)

    def pair_values(p_pair, vc):
        vals = jnp.concatenate([jnp.concatenate([jnp.where(lo_k, vc, zeros_k), ind_lo], axis=1),
                                jnp.concatenate([jnp.where(lo_k, zeros_k, vc), ind_hi], axis=1)], axis=0)
        return _dot(p_pair, vals)

    def decay_row(blocks, offs, hd):
        return jnp.concatenate([offs[p] - ck_ref[blocks[p], hd:hd + 1, :] for p in range(w_near)], axis=1)

    def near_weights(s_pair, rows, g, near_c, near_offs, bounded):
        ps, ms = [], []
        for e in range(PAIR):
            hd = PAIR * g + e
            cqh = cq_ref[rows, hd:hd + 1]
            if bounded:
                m = nq_ref[rows, hd:hd + 1] * kg_ref[0, hd]
                cqh = cqh - m
            dec = decay_row(near_c, near_offs[hd], hd)
            pieces = []
            for p in range(w_near):
                lanes = slice(e * cw + p * ATT_BLK, e * cw + (p + 1) * ATT_BLK)
                sp = s_pair[:, lanes] + cqh + dec[:, p * ATT_BLK:(p + 1) * ATT_BLK]
                pieces.append(sp + tri_bias if p == w_near - 1 else sp)
            s = jnp.concatenate(pieces, axis=1)
            if not bounded:
                m = jnp.max(s, axis=1, keepdims=True)
                s = s - m
            ms.append(m)
            ps.append(jnp.exp2(s).astype(BF16))
        return jnp.concatenate(ps, axis=1), ms

    def gated_output(rows, g, acc, l):
        cols = slice(g * LANES, (g + 1) * LANES)
        z_ref[rows, cols] = ((acc / l) * sa_ref[rows, cols].astype(F32)).astype(BF16)

    def merge():
        y_ref[...] = _merge_norm(x_ref[...], ada_ref[2, ADA_PROMPT_ROW:ADA_PROMPT_ROW + 1, :], z_ref[...], zp_ref[...],
                                 wo_ref, fg_ref[...], a_w)

    def check_next_step():
        ok = jnp.bool_(True)
        last = qmx_ref.shape[0] - 1
        for sub in range(nsub):
            i = jnp.minimum((step + 1) * nsub + sub, last)
            for hd in range(n_heads):
                qk = 2.0 * qmx_ref[i, hd] * kg_ref[0, hd]
                back = qk
                for dd in range(1, w_near):
                    back = back + tot_ref[i - dd, hd]
                ok = jnp.logical_and(ok, jnp.logical_and(qk <= EXP2_SAFE_SPAN, back < -EXP2_UNDERFLOW))
        straight_ref[0] = ok.astype(jnp.int32)

    def straight_step():
        units =[(sub, g) for sub in range(nsub) for g in range(n_pairs)]
        near_of, offs_of = [], []
        for sub in range(nsub):
            i = step * nsub + sub
            near_of.append([i - (w_near - 1) + p for p in range(w_near)])
            offs = []
            for hd in range(n_heads):
                o, per_piece = jnp.float32(0.0), [jnp.float32(0.0)]
                for dd in range(1, w_near):
                    o = o + tot_ref[i - dd, hd]
                    per_piece.append(o)
                offs.append(per_piece[::-1])
            offs_of.append(offs)
        s_next = pair_scores(0, step_rows_of(k_ref, knew_ref, 0, 0), 0)
        for u, (sub, g) in enumerate(units):
            s_pair = s_next
            if u + 1 < len(units):
                sub1, g1 = units[u + 1]
                s_next = pair_scores(sub1 * ATT_BLK, step_rows_of(k_ref, knew_ref, sub1, g1), g1)
            rows = pl.ds(sub * ATT_BLK, ATT_BLK)
            p_pair, _ = near_weights(s_pair, rows, g, near_of[sub], offs_of[sub], True)
            r = pair_values(p_pair, step_rows_of(v_ref, vnew_ref, sub, g))
            gated_output(rows, g, r[:, 0:LANES], r[:, LANES:])
        check_next_step()
        keep_step_rows()
        merge()

    def sub_body(sub, carry):
        i = step * nsub + sub
        r0 = pl.multiple_of(sub * ATT_BLK, ATT_BLK)

        rows = pl.ds(r0, ATT_BLK)
        qk = [2.0 * qmx_ref[i, hd] * kg_ref[0, hd] for hd in range(n_heads)]

        near = [i - (w_near - 1) + p for p in range(w_near)]
        near_c = [jnp.maximum(b, 0) for b in near]
        near_offs, offs_far = [], []
        for hd in range(n_heads):
            back = [tot_at(i - dd, hd) for dd in range(1, w_near)]
            offs = []
            for p in range(w_near):
                o = jnp.float32(0.0)
                for dd in range(1, w_near - p):
                    o = o + back[dd - 1]
                offs.append(jnp.where(near[p] >= 0, o, NEG))
            near_offs.append(offs)
            o = jnp.float32(0.0)
            for t in back:
                o = o + t
            offs_far.append(o)

        def near_chunk(bounded):
            def fn():
                s_next = pair_scores(r0, rows_of(k_ref, near_c, 0), 0)
                for g in range(n_pairs):
                    s_pair = s_next
                    if g + 1 < n_pairs:
                        s_next = pair_scores(r0, rows_of(k_ref, near_c, g + 1), g + 1)
                    p_pair, ms = near_weights(s_pair, rows, g, near_c, near_offs, bounded)
                    for e in range(PAIR):
                        m_ref[PAIR * g + e] = ms[e]
                    r = pair_values(p_pair, rows_of(v_ref, near_c, g))
                    acc_ref[g] = r[:, 0:LANES]
                    l_ref[g] = r[:, LANES:]
                    gated_output(rows, g, r[:, 0:LANES], r[:, LANES:])
            return fn

        bounded_ok = qk[0] <= EXP2_SAFE_SPAN
        for hd in range(1, n_heads):
            bounded_ok = jnp.logical_and(bounded_ok, qk[hd] <= EXP2_SAFE_SPAN)
        pl.when(bounded_ok)(near_chunk(True))
        pl.when(jnp.logical_not(bounded_ok))(near_chunk(False))

        def far_cond(c):
            top = i - c[0] * w_near
            need = qk[0] + c[1] >= -EXP2_UNDERFLOW
            for hd in range(1, n_heads):
                need = jnp.logical_or(need, qk[hd] + c[1 + hd] >= -EXP2_UNDERFLOW)
            return jnp.logical_and(top >= 0, need)

        def far_body(c):
            top = i - c[0] * w_near
            jc = jnp.maximum(top - (w_near - 1), 0)
            blocks = [jc + p for p in range(w_near)]
            keepc = col < (top + 1 - jc) * ATT_BLK
            new = [c[0] + 1]
            for g in range(n_pairs):
                s_pair = pair_scores(r0, rows_of(k_ref, blocks, g), g)
                ps, alphas = [], []
                for e in range(PAIR):
                    hd = PAIR * g + e
                    tt = [tot_at(top - b, hd) for b in range(w_near)]
                    offs = []
                    for p in range(w_near):
                        behind = top - (jc + p)
                        o = c[1 + hd]
                        for b in range(w_near):
                            o = o + jnp.where(behind >= b, tt[b], 0.0)
                        offs.append(o)
                    s = s_pair[:, e * cw:(e + 1) * cw] + cq_ref[rows, hd:hd + 1] + decay_row(blocks, offs, hd)
                    s = jnp.where(keepc, s, NEG)
                    m_old = m_ref[hd]
                    m_new = jnp.maximum(m_old, jnp.max(s, axis=1, keepdims=True))
                    m_ref[hd] = m_new
                    alphas.append(jnp.broadcast_to(jnp.exp2(m_old - m_new), (ATT_BLK, LANES)))
                    ps.append(jnp.exp2(s - m_new).astype(BF16))
                    o = c[1 + hd]
                    for t in tt:
                        o = o + t
                    new.append(o)
                alpha = jnp.where(lo_q, alphas[0], alphas[1])
                r = pair_values(jnp.concatenate(ps, axis=1), rows_of(v_ref, blocks, g))
                acc_ref[g] = alpha * acc_ref[g] + r[:, 0:LANES]
                l_ref[g] = alpha * l_ref[g] + r[:, LANES:]
            return tuple(new)

        far = lax.while_loop(far_cond, far_body, (jnp.int32(1),) + tuple(offs_far))

        @pl.when(far[0] > 1)
        def _():
            for g in range(n_pairs):
                gated_output(rows, g, acc_ref[g], l_ref[g])
        return carry

    @pl.when(step == 0)
    def _():
        straight_ref[0] = 0

    straight = straight_ref[0] == 1
    pl.when(straight)(straight_step)

    @pl.when(jnp.logical_not(straight))
    def _():
        keep_step_rows()
        check_next_step()
        lax.fori_loop(0, nsub, sub_body, 0)
        merge()


def _prompt_attention(tot, qmx, kg, q, cq, nq, ck, kb, vb, sa, zp, x2, ada, wo, fg, *, tm, n_heads):
    rows, d = x2.shape
    a_w = n_heads * HEAD_DIM
    n_pairs = n_heads // PAIR
    assert rows % tm == 0 and tm % ATT_BLK == 0 and tm // ATT_BLK >= NEAR_BLOCKS - 1 and n_heads % PAIR == 0
    row_blk = lambda w: pl.BlockSpec((tm, w), lambda i, *_: (i, 0))
    grid_spec = pltpu.PrefetchScalarGridSpec(
        num_scalar_prefetch=3,
        grid=(rows // tm,),
        in_specs=[row_blk(a_w), row_blk(n_heads), row_blk(n_heads), _resident(ck.shape), row_blk(a_w), row_blk(a_w),
                  row_blk(a_w), row_blk(zp.shape[1]), row_blk(d), _resident(ada.shape), _resident(wo.shape),
                  _resident((1, d))],
        out_specs=row_blk(d),
        scratch_shapes=[pltpu.VMEM(kb.shape, BF16),
                        pltpu.VMEM(vb.shape, BF16),
                        pltpu.VMEM((tm, a_w), BF16),
                        pltpu.VMEM((n_heads, ATT_BLK, 1), F32),
                        pltpu.VMEM((n_pairs, ATT_BLK, LANES), F32),
                        pltpu.VMEM((n_pairs, ATT_BLK, LANES), F32),
                        pltpu.SMEM((1,), jnp.int32)],
    )
    return pl.pallas_call(
        functools.partial(_attn_kernel, tm=tm, n_heads=n_heads),
        grid_spec=grid_spec,
        out_shape=jax.ShapeDtypeStruct((rows, d), F32),
        compiler_params=pltpu.CompilerParams(dimension_semantics=("arbitrary",), vmem_limit_bytes=VMEM_LIMIT),
        name="attn",
    )(tot, qmx, kg, q, cq, nq, ck, kb, vb, sa, zp, x2, ada, wo, fg)


def _sattn_kernel(q_ref, cq_ref, ckn_ref, kn_ref, vn_ref, kc_ref, vc_ref, lfc_ref, sa_ref, zp_ref, x_ref,
                  ada_ref, wo_ref, fg_ref, y_ref, *, n_heads):
    a_w = n_heads * HEAD_DIM
    ln = q_ref.shape[0]
    past = kc_ref.shape[2]
    nb = past // LANES

    lfc = lfc_ref[0] * LOG2E
    triu = (lax.broadcasted_iota(jnp.int32, (LANES, LANES), 0)
            <= lax.broadcasted_iota(jnp.int32, (LANES, LANES), 1)).astype(BF16)
    zeros = jnp.zeros((8, LANES), F32)
    parts = []
    for b in range(nb):
        parts.extend(_split3(lfc[:, b * LANES:(b + 1) * LANES]))
        parts.append(zeros)
    cs = _dot(jnp.concatenate(parts, axis=0).astype(BF16), triu)
    after = jnp.zeros((n_heads, 1), F32)
    suffix = [None] * nb
    for b in reversed(range(nb)):
        cb = cs[32 * b:32 * b + 8] + cs[32 * b + 8:32 * b + 16] + cs[32 * b + 16:32 * b + 24]
        tot = cb[:, LANES - 1:LANES]
        suffix[b] = (tot - cb) + after
        after = after + tot
    dec_c = jnp.concatenate(suffix, axis=1)

    lane = lax.broadcasted_iota(jnp.int32, (ln, LANES), 1)
    half = [lane < HEAD_DIM, lane >= HEAD_DIM]
    causal = lax.broadcasted_iota(jnp.int32, (ln, ln), 1) <= lax.broadcasted_iota(jnp.int32, (ln, ln), 0)
    outs = []
    for g in range(n_heads // PAIR):
        cols = slice(g * LANES, (g + 1) * LANES)
        q2 = q_ref[:, cols]
        kct = kc_ref[0, cols, :].astype(BF16)
        vct = vc_ref[0, cols, :].astype(BF16)
        kn = kn_ref[:, cols]
        vn = vn_ref[:, cols]
        qst = jnp.concatenate([jnp.where(half[e], q2, jnp.zeros_like(q2)) for e in range(PAIR)], axis=0)
        sc_st = _dot(qst, kct)
        sn_st = _dot_nt(qst, kn)
        pc, pn, ls = [], [], []
        for e in range(PAIR):
            hd = PAIR * g + e
            rows = slice(e * ln, (e + 1) * ln)
            cqh = cq_ref[:, hd:hd + 1]
            s_c = sc_st[rows] + cqh + dec_c[hd:hd + 1, :]
            s_n = jnp.where(causal, sn_st[rows] + cqh - ckn_ref[0, hd:hd + 1, :], NEG)
            m = jnp.maximum(jnp.max(s_c, axis=1, keepdims=True), jnp.max(s_n, axis=1, keepdims=True))
            p_c = jnp.exp2(s_c - m)
            p_n = jnp.exp2(s_n - m)
            ls.append(jnp.sum(p_c, axis=1, keepdims=True) + jnp.sum(p_n, axis=1, keepdims=True))
            pc.append(p_c.astype(BF16))
            pn.append(p_n.astype(BF16))
        acc = _dot_nt(jnp.concatenate(pc, axis=0), vct) + _dot(jnp.concatenate(pn, axis=0), vn)
        o = jnp.where(half[0], acc[0:ln] / ls[0], acc[ln:2 * ln] / ls[1])
        outs.append((o * sa_ref[:, cols].astype(F32)).astype(BF16))
    za = jnp.concatenate(outs, axis=1)
    y_ref[...] = _merge_norm(x_ref[...], ada_ref[2, pl.ds(ADA_SAMPLE_ROW + pl.program_id(0), 1), :], za, zp_ref[...], wo_ref, fg_ref[...], a_w)


def _sample_attention(q, cq, ckn, kb, vb, cache_k, cache_v, lfc, sa, zp, x2, ada, wo, fg, *, ln, n_heads):
    rows, d = x2.shape
    nbatch = rows // ln
    a_w = n_heads * HEAD_DIM
    past = cache_k.shape[2]
    row_blk = lambda w: pl.BlockSpec((ln, w), lambda b: (b, 0))
    per_b = lambda s: pl.BlockSpec((1,) + s, lambda b: (b, 0, 0))
    return pl.pallas_call(
        functools.partial(_sattn_kernel, n_heads=n_heads),
        grid=(nbatch,),
        in_specs=[row_blk(a_w), row_blk(n_heads), per_b((n_heads, ln)), row_blk(a_w), row_blk(a_w),
                  per_b((a_w, past)), per_b((a_w, past)), per_b((n_heads, past)),
                  row_blk(a_w), row_blk(zp.shape[1]), row_blk(d), _resident(ada.shape), _resident(wo.shape),
                  _resident((1, d))],
        out_specs=row_blk(d),
        out_shape=jax.ShapeDtypeStruct((rows, d), F32),
        compiler_params=pltpu.CompilerParams(dimension_semantics=("arbitrary",), vmem_limit_bytes=VMEM_LIMIT),
        name="sattn",
    )(q, cq, ckn, kb, vb, cache_k, cache_v, lfc, sa, zp, x2, ada, wo, fg)


def kernel(x_prompt, x_sample, c_prompt, c_sample, cache_k, cache_v, cache_logf, state_pool, norm_g, w_ada, b_ada,
           w_in, b_f, w_pool, pool_scale, w_out, final_g):
    depth = norm_g.shape[0]
    assert depth == 1
    bp, seq, d = x_prompt.shape
    bs, ln, _ = x_sample.shape
    assert bp == 1
    n_heads = cache_k.shape[3]
    past = cache_k.shape[2]
    a_w = n_heads * HEAD_DIM
    pw = state_pool.shape[3]
    assert pw == len(POOL_WINDOWS) * LANES and cache_k.shape[4] == HEAD_DIM and n_heads <= 8

    ada = _ada_terms(c_prompt, c_sample, w_ada[0], b_ada)

    wit = w_in[0].T
    wq = wit[0:4 * a_w].astype(BF16)
    wf = jnp.pad(wit[4 * a_w:4 * a_w + n_heads], ((0, LANES - n_heads), (0, 0))).astype(BF16)
    wu = wit[4 * a_w + n_heads:].astype(BF16)
    wp = w_pool[0].astype(BF16)
    ps = pool_scale[0][None, :]
    wo = w_out[0].astype(BF16)
    ng = norm_g[0][None, :]
    fg = final_g[None, :]

    bm = 1024
    xp2 = x_prompt.reshape(seq, d)
    hist_p = jnp.zeros((1, HIST_PAD, pw), F32)
    (q_p, k_p, v_p, kb_p, vb_p, sa_p, zp_p, lf_p, cq_p, nq_p, ck_p, tot, qmx, kg, ho_p) = _project(
        xp2, ada, ng, wq, wf, b_f[0], wu, wp, ps, hist_p,
        bm=bm, sb=ATT_BLK, segs=1, ada_row=ADA_PROMPT_ROW, start_pos=0, n_heads=n_heads, kv_head_major=False)
    y_p = _prompt_attention(tot, qmx, kg, q_p, cq_p, nq_p, ck_p, kb_p, vb_p, sa_p, zp_p, xp2, ada, wo, fg,
                            tm=512, n_heads=n_heads)

    xs2 = x_sample.reshape(bs * ln, d)
    hist_s = jnp.pad(state_pool[0], ((0, 0), (HIST_PAD - POOL_HIST, 0), (0, 0)))
    (q_s, k_s, v_s, kb_s, vb_s, sa_s, zp_s, lf_s, cq_s, _, ck_s, _, _, _, ho_s) = _project(
        xs2, ada, ng, wq, wf, b_f[0], wu, wp, ps, hist_s,
        bm=bs * ln, sb=ln, segs=bs, ada_row=ADA_SAMPLE_ROW, start_pos=past, n_heads=n_heads, kv_head_major=True)
    lfc = jnp.swapaxes(cache_logf[0], 1, 2)
    ckt = jnp.transpose(cache_k[0], (0, 2, 3, 1)).reshape(bs, a_w, past)
    cvt = jnp.transpose(cache_v[0], (0, 2, 3, 1)).reshape(bs, a_w, past)
    y_s = _sample_attention(q_s, cq_s, ck_s, kb_s, vb_s, ckt, cvt, lfc, sa_s, zp_s, xs2, ada, wo, fg,
                            ln=ln, n_heads=n_heads)

    hd = (n_heads, HEAD_DIM)
    seq_minor = lambda t: jnp.transpose(t.reshape(hd + (bp, seq)), (2, 3, 0, 1))[None]
    return (y_p.reshape(bp, seq, d), y_s.reshape(bs, ln, d),
            seq_minor(k_p), seq_minor(v_p), jnp.swapaxes(lf_p, 1, 2)[None],
            ho_p[:, 16 - POOL_HIST:, :][None],
            k_s.reshape((1, bs, ln) + hd), v_s.reshape((1, bs, ln) + hd), jnp.swapaxes(lf_s, 1, 2)[None],
            ho_s[:, 16 - POOL_HIST:, :][None])
```

```python
import functools

import jax
import jax.numpy as jnp
from jax import lax
from jax.experimental import pallas as pl
from jax.experimental.pallas import tpu as pltpu

HEAD_DIM = 64
POOL_WINDOWS = (2, 4, 8, 16)
EPS = 1e-6

LANES = 128
PAIR = LANES // HEAD_DIM
ATT_BLK = 128
NEAR_BLOCKS = 3
LOG2E = 1.4426950408889634
EXP2_UNDERFLOW = 151.0
EXP2_SAFE_SPAN = 100.0
NORM_SLACK = 1.01
HIST_PAD = 32
POOL_HIST = max(POOL_WINDOWS) - 1
NEG = -1e30
VMEM_LIMIT = 60 * 1024 * 1024

F32 = jnp.float32
BF16 = jnp.bfloat16


def _silu(x):
    return x * jax.nn.sigmoid(x)


def _dot(a, b):
    return jnp.dot(a, b, preferred_element_type=F32)


def _dot_nt(a, b):
    return lax.dot_general(a, b, (((1,), (1,)), ((), ())), preferred_element_type=F32)


def _split3(x):
    hi = x.astype(BF16).astype(F32)
    r1 = x - hi
    mid = r1.astype(BF16).astype(F32)
    return hi, mid, r1 - mid


def _rows_to_lanes(x, n):
    rows = x.shape[0]
    if rows < LANES:
        x = jnp.concatenate([x, jnp.zeros((LANES - rows, LANES), x.dtype)], axis=0)
    return x.T[0:n, 0:rows]


def _resident(shape):
    return pl.BlockSpec(shape, lambda *_: (0,) * len(shape), pipeline_mode=pl.Buffered(1))


ADA_ROWS = 16
ADA_PROMPT_ROW, ADA_SAMPLE_ROW = 0, 8


def _ada_kernel(cp_ref, cs_ref, w_ref, b_ref, o_ref):
    ap = jnp.broadcast_to(_silu(cp_ref[...]), (ADA_SAMPLE_ROW, cp_ref.shape[1]))
    a = jnp.concatenate([ap, _silu(cs_ref[...])], axis=0).astype(BF16)
    o_ref[0] = _dot(a, w_ref[...].astype(BF16)) + b_ref[...]


def _ada_terms(c_prompt, c_sample, w_ada, b_ada):
    d = c_prompt.shape[1]
    assert c_prompt.shape[0] == 1 and c_sample.shape[0] == ADA_ROWS - ADA_SAMPLE_ROW and w_ada.shape[1] == 3 * d
    split = 2
    bn = d // split
    return pl.pallas_call(
        _ada_kernel,
        grid=(3 * split,),
        in_specs=[pl.BlockSpec(c_prompt.shape, lambda j: (0, 0)),
                  pl.BlockSpec(c_sample.shape, lambda j: (0, 0)),
                  pl.BlockSpec((d, bn), lambda j: (0, j)),
                  pl.BlockSpec((1, bn), lambda j: (0, j))],
        out_specs=pl.BlockSpec((1, ADA_ROWS, bn), lambda j: (j // split, 0, j % split)),
        out_shape=jax.ShapeDtypeStruct((3, ADA_ROWS, d), F32),
        compiler_params=pltpu.CompilerParams(dimension_semantics=("arbitrary",), vmem_limit_bytes=VMEM_LIMIT),
        name="ada",
    )(c_prompt, c_sample, w_ada, b_ada)


def _proj_kernel(bf_ref, x_ref, ada_ref, ng_ref, wq_ref, wf_ref, wu_ref, wp_ref, ps_ref, h0_ref,
                 q_ref, k32_ref, v32_ref, kb_ref, vb_ref, sa_ref, zp_ref, lf_ref, cq_ref, nq_ref, ck_ref,
                 tot_ref, qmx_ref, kg_ref, ho_ref,
                 e_ref, t2_ref, t4_ref, t8_ref, kmx_ref, *, bm, sb, segs, ada_row, start_pos, n_heads, kv_head_major):
    a_w = n_heads * HEAD_DIM
    seg_rows = bm // segs
    step = pl.program_id(0)
    x = x_ref[...]
    ms = jnp.mean(x * x, axis=-1, keepdims=True)
    xn = x * lax.rsqrt(ms + EPS) * ng_ref[...]
    h = jnp.concatenate(
        [xn[g * seg_rows:(g + 1) * seg_rows] * (1.0 + ada_ref[1, ada_row + g:ada_row + g + 1, :])
         + ada_ref[0, ada_row + g:ada_row + g + 1, :] for g in range(segs)], axis=0).astype(BF16)

    pw = len(POOL_WINDOWS) * LANES
    sc = LOG2E / (HEAD_DIM ** 0.5)
    pu = _dot_nt(h, wu_ref[0:pw, :])
    qs = _dot_nt(h, wq_ref[0:a_w, :]) * sc
    q_ref[...] = qs.astype(BF16)
    sel = (lax.broadcasted_iota(jnp.int32, (a_w, LANES), 0) // HEAD_DIM
           == lax.broadcasted_iota(jnp.int32, (a_w, LANES), 1)).astype(BF16)
    nq = jnp.sqrt(_dot((qs * qs).astype(BF16), sel)) * NORM_SLACK
    nq_ref[...] = nq[:, 0:n_heads]

    def store_kv(ref32, refb, p):
        refb[...] = p.astype(BF16)
        if kv_head_major:
            for hd in range(n_heads):
                ref32[:, hd, :] = p[:, hd * HEAD_DIM:(hd + 1) * HEAD_DIM]
        else:
            ref32[...] = p.T

    ext = HIST_PAD + seg_rows
    n = segs * ext

    def load_history():
        for g in range(segs):
            e_ref[g * ext:g * ext + HIST_PAD, :] = h0_ref[g]

    if segs > 1:
        load_history()
    else:
        pl.when(step == 0)(load_history)

    for g in range(segs):
        e_ref[g * ext + HIST_PAD:(g + 1) * ext, :] = pu[g * seg_rows:(g + 1) * seg_rows]
    t2_ref[8:n, :] = e_ref[8:n, :] + e_ref[7:n - 1, :]
    t4_ref[16:n, :] = t2_ref[16:n, LANES:] + t2_ref[14:n - 2, LANES:]
    t8_ref[24:n, :] = t4_ref[24:n, LANES:] + t4_ref[20:n - 4, LANES:]

    def seg_rows_of(ref, cols, back=0):
        return jnp.concatenate([ref[g * ext + HIST_PAD - back:(g + 1) * ext - back, cols] for g in range(segs)], axis=0)

    lane0, lane1 = slice(0, LANES), slice(LANES, 2 * LANES)
    sums = [seg_rows_of(t2_ref, lane0), seg_rows_of(t4_ref, lane0), seg_rows_of(t8_ref, lane0),
            seg_rows_of(t8_ref, lane1) + seg_rows_of(t8_ref, lane1, back=8)]
    row = lax.broadcasted_iota(jnp.int32, (bm, 1), 0)
    pos1 = start_pos + 1 + (step * bm + row if segs == 1 else row % seg_rows)
    pool_d = []
    for g, w in enumerate(POOL_WINDOWS):
        rc = 1.0 / jnp.minimum(pos1, w).astype(F32)
        pool_d.append((sums[g] * rc - pu[:, g * LANES:(g + 1) * LANES]).astype(BF16))
    for g in range(segs):
        ho_ref[g] = e_ref[(g + 1) * ext - 16:(g + 1) * ext, :]
    if segs == 1:
        e_ref[0:HIST_PAD, :] = e_ref[bm:n, :]

    pk = _dot_nt(h, wq_ref[a_w:2 * a_w, :])
    store_kv(k32_ref, kb_ref, pk)
    nk = jnp.sqrt(_dot((pk * pk).astype(BF16), sel)) * NORM_SLACK
    spg = _silu(_dot_nt(h, wu_ref[pw:2 * pw, :]))

    lane = lax.broadcasted_iota(jnp.int32, (1, LANES), 1)
    bias = jnp.zeros((1, LANES), F32)
    for hd in range(n_heads):
        bias = jnp.where(lane == hd, bf_ref[hd], bias)
    z = _dot_nt(h, wf_ref[...]) + bias
    lf = jnp.minimum(z, 0.0) - jnp.log1p(jnp.exp(-jnp.abs(z)))

    pa = _dot_nt(h, wq_ref[3 * a_w:4 * a_w, :])
    sa_ref[...] = _silu(pa).astype(BF16)

    tri = (lax.broadcasted_iota(jnp.int32, (sb, sb), 1)
           <= lax.broadcasted_iota(jnp.int32, (sb, sb), 0)).astype(BF16)
    lf2 = lf * LOG2E
    tots, qmx, kmx = [], [], []
    for s in range(bm // sb):
        rows = slice(s * sb, (s + 1) * sb)
        parts = _dot(tri, jnp.concatenate(_split3(lf2[rows]), axis=1).astype(BF16))
        cb = parts[:, 0:LANES] + parts[:, LANES:2 * LANES] + parts[:, 2 * LANES:]
        cq_ref[rows, :] = cb[:, 0:n_heads]
        ck_ref[s] = _rows_to_lanes(cb, n_heads)
        off = (s * sb) % seg_rows
        lf_ref[(s * sb) // seg_rows, :, off:off + sb] = _rows_to_lanes(lf[rows], n_heads)
        tots.append(cb[sb - 1:sb, :])
        qmx.append(jnp.max(nq[rows], axis=0, keepdims=True))
        kmx.append(jnp.max(nk[rows], axis=0, keepdims=True))
    tot_ref[...] = jnp.concatenate(tots, axis=0)[:, 0:n_heads]
    qmx_ref[...] = jnp.concatenate(qmx, axis=0)[:, 0:n_heads]
    kmax = kmx[0]
    for t in kmx[1:]:
        kmax = jnp.maximum(kmax, t)

    store_kv(v32_ref, vb_ref, _dot_nt(h, wq_ref[2 * a_w:3 * a_w, :]))

    zero_w = jnp.zeros((LANES, LANES), BF16)
    for g in range(0, len(POOL_WINDOWS), 2):
        cols = slice(g * LANES, (g + 2) * LANES)
        w2 = jnp.concatenate([jnp.concatenate([wp_ref[g].astype(BF16), zero_w], axis=1),
                              jnp.concatenate([zero_w, wp_ref[g + 1].astype(BF16)], axis=1)], axis=0)
        y = _dot(jnp.concatenate([pool_d[g], pool_d[g + 1]], axis=1), w2) * ps_ref[:, cols]
        zp_ref[:, cols] = (y * spg[:, cols]).astype(BF16)

    @pl.when(step > 0)
    def _():
        kmx_ref[...] = jnp.maximum(kmx_ref[...], kmax)

    @pl.when(step == 0)
    def _():
        kmx_ref[...] = kmax

    kg_ref[...] = kmx_ref[...]


def _project(x2, ada, norm_g, wq, wf, wu, b_f, wp, ps, hist0, *, bm, sb, segs, ada_row, start_pos, n_heads,
             kv_head_major):
    rows, d = x2.shape
    a_w = n_heads * HEAD_DIM
    pw = len(POOL_WINDOWS) * LANES
    n_steps = rows // bm
    assert segs == 1 or n_steps == 1
    n_streams = segs
    seg_rows = bm // segs
    nsb = bm // sb
    assert nsb == 8 and seg_rows % sb == 0
    row_blk = lambda w: pl.BlockSpec((bm, w), lambda i, *_: (i, 0))
    per_stream = lambda r, w: pl.BlockSpec((segs, r, w), lambda i, *_: (0, 0, 0))
    kern = functools.partial(_proj_kernel, bm=bm, sb=sb, segs=segs, ada_row=ada_row, start_pos=start_pos,
                             n_heads=n_heads, kv_head_major=kv_head_major)
    if kv_head_major:
        kv_shape = (rows, n_heads, HEAD_DIM)
        kv_blk = pl.BlockSpec((bm, n_heads, HEAD_DIM), lambda i, *_: (i, 0, 0))
    else:
        kv_shape = (a_w, rows)
        kv_blk = pl.BlockSpec((a_w, bm), lambda i, *_: (0, i))
    out_shape = (
        jax.ShapeDtypeStruct((rows, a_w), BF16),
        jax.ShapeDtypeStruct(kv_shape, F32),
        jax.ShapeDtypeStruct(kv_shape, F32),
        jax.ShapeDtypeStruct((rows, a_w), BF16),
        jax.ShapeDtypeStruct((rows, a_w), BF16),
        jax.ShapeDtypeStruct((rows, a_w), BF16),
        jax.ShapeDtypeStruct((rows, pw), BF16),
        jax.ShapeDtypeStruct((n_streams, n_heads, rows // n_streams), F32),
        jax.ShapeDtypeStruct((rows, n_heads), F32),
        jax.ShapeDtypeStruct((rows, n_heads), F32),
        jax.ShapeDtypeStruct((rows // sb, n_heads, sb), F32),
        jax.ShapeDtypeStruct((rows // sb, n_heads), F32),
        jax.ShapeDtypeStruct((rows // sb, n_heads), F32),
        jax.ShapeDtypeStruct((1, LANES), F32),
        jax.ShapeDtypeStruct((n_streams, 16, pw), F32),
    )
    out_specs = (
        row_blk(a_w), kv_blk, kv_blk, row_blk(a_w), row_blk(a_w), row_blk(a_w), row_blk(pw),
        pl.BlockSpec((segs, n_heads, seg_rows), lambda i, *_: (0, 0, i)),
        row_blk(n_heads), row_blk(n_heads),
        pl.BlockSpec((nsb, n_heads, sb), lambda i, *_: (i, 0, 0)),
        pl.BlockSpec((nsb, n_heads), lambda i, *_: (i, 0)),
        pl.BlockSpec((nsb, n_heads), lambda i, *_: (i, 0)),
        pl.BlockSpec((1, LANES), lambda i, *_: (0, 0)),
        per_stream(16, pw),
    )
    in_specs = [
        row_blk(d),
        _resident(ada.shape),
        _resident((1, d)),
        _resident(wq.shape), _resident(wf.shape), _resident(wu.shape), _resident(wp.shape), _resident(ps.shape),
        per_stream(HIST_PAD, pw),
    ]
    return pl.pallas_call(
        kern,
        grid_spec=pltpu.PrefetchScalarGridSpec(
            num_scalar_prefetch=1,
            grid=(n_steps,),
            in_specs=in_specs,
            out_specs=out_specs,
            scratch_shapes=[pltpu.VMEM((bm + segs * HIST_PAD, pw), F32),
                            pltpu.VMEM((bm + segs * HIST_PAD, pw), F32),
                            pltpu.VMEM((bm + segs * HIST_PAD, pw - LANES), F32),
                            pltpu.VMEM((bm + segs * HIST_PAD, pw - 2 * LANES), F32),
                            pltpu.VMEM((1, LANES), F32)]),
        out_shape=out_shape,
        compiler_params=pltpu.CompilerParams(dimension_semantics=("arbitrary",), vmem_limit_bytes=VMEM_LIMIT),
        name="proj",
    )(b_f, x2, ada, norm_g, wq, wf, wu, wp, ps, hist0)


def _merge_norm(x, gate, za, zp, wo_ref, fg, a_w):
    dy = _dot(za, wo_ref[0:a_w, :].astype(BF16)) + _dot(zp, wo_ref[a_w:, :].astype(BF16))
    out = x + gate * dy
    ms = jnp.mean(out * out, axis=-1, keepdims=True)
    return out * lax.rsqrt(ms + EPS) * fg


def _attn_kernel(tot_ref, qmx_ref, kg_ref,
                 q_ref, cq_ref, nq_ref, ck_ref, knew_ref, vnew_ref, sa_ref, zp_ref, x_ref, ada_ref, wo_ref, fg_ref,
                 y_ref,
                 k_ref, v_ref, z_ref, m_ref, l_ref, acc_ref, straight_ref, *, tm, n_heads):
    a_w = n_heads * HEAD_DIM
    n_pairs = n_heads // PAIR
    nsub = tm // ATT_BLK
    w_near = NEAR_BLOCKS
    cw = w_near * ATT_BLK
    step = pl.program_id(0)

    def keep_step_rows():
        k_ref[pl.ds(pl.multiple_of(step * tm, tm), tm), :] = knew_ref[...]
        v_ref[pl.ds(pl.multiple_of(step * tm, tm), tm), :] = vnew_ref[...]

    lo_q = lax.broadcasted_iota(jnp.int32, (ATT_BLK, LANES), 1) < HEAD_DIM
    lo_k = lax.broadcasted_iota(jnp.int32, (cw, LANES), 1) < HEAD_DIM
    col = lax.broadcasted_iota(jnp.int32, (ATT_BLK, cw), 1)
    tri = (lax.broadcasted_iota(jnp.int32, (ATT_BLK, ATT_BLK), 1)
           <= lax.broadcasted_iota(jnp.int32, (ATT_BLK, ATT_BLK), 0))
    tri_bias = jnp.where(tri, 0.0, NEG).astype(F32)
    zeros_k = jnp.zeros((cw, LANES), BF16)
    ind_lo = jnp.where(lo_k, 1.0, 0.0).astype(BF16)
    ind_hi = jnp.where(lo_k, 0.0, 1.0).astype(BF16)

    def tot_at(b, hd):
        return jnp.where(b >= 0, tot_ref[jnp.maximum(b, 0), hd], 0.0)

    def rows_of(ref, blocks, g):
        return jnp.concatenate(
            [ref[pl.ds(pl.multiple_of(b * ATT_BLK, ATT_BLK), ATT_BLK), g * LANES:(g + 1) * LANES] for b in blocks],
            axis=0)

    def step_rows_of(ref, new_ref, sub, g):
        parts = []
        for p in range(w_near):
            rel = sub - (w_near - 1) + p
            if rel >= 0:
                parts.append(new_ref[rel * ATT_BLK:(rel + 1) * ATT_BLK, g * LANES:(g + 1) * LANES])
            else:
                start = pl.multiple_of((step * nsub + rel) * ATT_BLK, ATT_BLK)
                parts.append(ref[pl.ds(start, ATT_BLK), g * LANES:(g + 1) * LANES])
        return jnp.concatenate(parts, axis=0)

    def pair_scores(r0, kc, g):
        keys = jnp.concatenate([jnp.where(lo_k, kc, zeros_k), jnp.where(lo_k, zeros_k, kc)], axis=0)
        return _dot_nt(q_ref[pl.ds(r0, ATT_BLK), g * LANES:(g + 1) * LANES], keys)

    def pair_values(p_pair, vc):
        vals = jnp.concatenate([jnp.concatenate([jnp.where(lo_k, vc, zeros_k), ind_lo], axis=1),
                                jnp.concatenate([jnp.where(lo_k, zeros_k, vc), ind_hi], axis=1)], axis=0)
        return _dot(p_pair, vals)

    def decay_row(blocks, offs, hd):
        return jnp.concatenate([offs[p] - ck_ref[blocks[p], hd:hd + 1, :] for p in range(w_near)], axis=1)

    def near_weights(s_pair, rows, g, near_c, near_offs, bounded):
        ps, ms = [], []
        for e in range(PAIR):
            hd = PAIR * g + e
            cqh = cq_ref[rows, hd:hd + 1]
            if bounded:
                m = nq_ref[rows, hd:hd + 1] * kg_ref[0, hd]
                cqh = cqh - m
            dec = decay_row(near_c, near_offs[hd], hd)
            pieces = []
            for p in range(w_near):
                lanes = slice(e * cw + p * ATT_BLK, e * cw + (p + 1) * ATT_BLK)
                sp = s_pair[:, lanes] + cqh + dec[:, p * ATT_BLK:(p + 1) * ATT_BLK]
                pieces.append(sp + tri_bias if p == w_near - 1 else sp)
            s = jnp.concatenate(pieces, axis=1)
            if not bounded:
                m = jnp.max(s, axis=1, keepdims=True)
                s = s - m
            ms.append(m)
            ps.append(jnp.exp2(s).astype(BF16))
        return jnp.concatenate(ps, axis=1), ms

    def gated_output(rows, g, acc, l):
        cols = slice(g * LANES, (g + 1) * LANES)
        z_ref[rows, cols] = ((acc / l) * sa_ref[rows, cols].astype(F32)).astype(BF16)

    def merge():
        y_ref[...] = _merge_norm(x_ref[...], ada_ref[2, ADA_PROMPT_ROW:ADA_PROMPT_ROW + 1, :], z_ref[...], zp_ref[...],
                                 wo_ref, fg_ref[...], a_w)

    def check_next_step():
        ok = jnp.bool_(True)
        last = qmx_ref.shape[0] - 1
        for sub in range(nsub):
            i = jnp.minimum((step + 1) * nsub + sub, last)
            for hd in range(n_heads):
                qk = 2.0 * qmx_ref[i, hd] * kg_ref[0, hd]
                back = qk
                for dd in range(1, w_near):
                    back = back + tot_ref[i - dd, hd]
                ok = jnp.logical_and(ok, jnp.logical_and(qk <= EXP2_SAFE_SPAN, back < -EXP2_UNDERFLOW))
        straight_ref[0] = ok.astype(jnp.int32)

    def straight_step():
        units =[(sub, g) for sub in range(nsub) for g in range(n_pairs)]
        near_of, offs_of = [], []
        for sub in range(nsub):
            i = step * nsub + sub
            near_of.append([i - (w_near - 1) + p for p in range(w_near)])
            offs = []
            for hd in range(n_heads):
                o, per_piece = jnp.float32(0.0), [jnp.float32(0.0)]
                for dd in range(1, w_near):
                    o = o + tot_ref[i - dd, hd]
                    per_piece.append(o)
                offs.append(per_piece[::-1])
            offs_of.append(offs)
        s_next = pair_scores(0, step_rows_of(k_ref, knew_ref, 0, 0), 0)
        for u, (sub, g) in enumerate(units):
            s_pair = s_next
            if u + 1 < len(units):
                sub1, g1 = units[u + 1]
                s_next = pair_scores(sub1 * ATT_BLK, step_rows_of(k_ref, knew_ref, sub1, g1), g1)
            rows = pl.ds(sub * ATT_BLK, ATT_BLK)
            p_pair, _ = near_weights(s_pair, rows, g, near_of[sub], offs_of[sub], True)
            r = pair_values(p_pair, step_rows_of(v_ref, vnew_ref, sub, g))
            gated_output(rows, g, r[:, 0:LANES], r[:, LANES:])
        check_next_step()
        keep_step_rows()
        merge()

    def sub_body(sub, carry):
        i = step * nsub + sub
        r0 = pl.multiple_of(sub * ATT_BLK, ATT_BLK)

        rows = pl.ds(r0, ATT_BLK)
        qk = [2.0 * qmx_ref[i, hd] * kg_ref[0, hd] for hd in range(n_heads)]

        near = [i - (w_near - 1) + p for p in range(w_near)]
        near_c = [jnp.maximum(b, 0) for b in near]
        near_offs, offs_far = [], []
        for hd in range(n_heads):
            back = [tot_at(i - dd, hd) for dd in range(1, w_near)]
            offs = []
            for p in range(w_near):
                o = jnp.float32(0.0)
                for dd in range(1, w_near - p):
                    o = o + back[dd - 1]
                offs.append(jnp.where(near[p] >= 0, o, NEG))
            near_offs.append(offs)
            o = jnp.float32(0.0)
            for t in back:
                o = o + t
            offs_far.append(o)

        def near_chunk(bounded):
            def fn():
                s_next = pair_scores(r0, rows_of(k_ref, near_c, 0), 0)
                for g in range(n_pairs):
                    s_pair = s_next
                    if g + 1 < n_pairs:
                        s_next = pair_scores(r0, rows_of(k_ref, near_c, g + 1), g + 1)
                    p_pair, ms = near_weights(s_pair, rows, g, near_c, near_offs, bounded)
                    for e in range(PAIR):
                        m_ref[PAIR * g + e] = ms[e]
                    r = pair_values(p_pair, rows_of(v_ref, near_c, g))
                    acc_ref[g] = r[:, 0:LANES]
                    l_ref[g] = r[:, LANES:]
                    gated_output(rows, g, r[:, 0:LANES], r[:, LANES:])
            return fn

        bounded_ok = qk[0] <= EXP2_SAFE_SPAN
        for hd in range(1, n_heads):
            bounded_ok = jnp.logical_and(bounded_ok, qk[hd] <= EXP2_SAFE_SPAN)
        pl.when(bounded_ok)(near_chunk(True))
        pl.when(jnp.logical_not(bounded_ok))(near_chunk(False))

        def far_cond(c):
            top = i - c[0] * w_near
            need = qk[0] + c[1] >= -EXP2_UNDERFLOW
            for hd in range(1, n_heads):
                need = jnp.logical_or(need, qk[hd] + c[1 + hd] >= -EXP2_UNDERFLOW)
            return jnp.logical_and(top >= 0, need)

        def far_body(c):
            top = i - c[0] * w_near
            jc = jnp.maximum(top - (w_near - 1), 0)
            blocks = [jc + p for p in range(w_near)]
            keepc = col < (top + 1 - jc) * ATT_BLK
            new = [c[0] + 1]
            for g in range(n_pairs):
                s_pair = pair_scores(r0, rows_of(k_ref, blocks, g), g)
                ps, alphas = [], []
                for e in range(PAIR):
                    hd = PAIR * g + e
                    tt = [tot_at(top - b, hd) for b in range(w_near)]
                    offs = []
                    for p in range(w_near):
                        behind = top - (jc + p)
                        o = c[1 + hd]
                        for b in range(w_near):
                            o = o + jnp.where(behind >= b, tt[b], 0.0)
                        offs.append(o)
                    s = s_pair[:, e * cw:(e + 1) * cw] + cq_ref[rows, hd:hd + 1] + decay_row(blocks, offs, hd)
                    s = jnp.where(keepc, s, NEG)
                    m_old = m_ref[hd]
                    m_new = jnp.maximum(m_old, jnp.max(s, axis=1, keepdims=True))
                    m_ref[hd] = m_new
                    alphas.append(jnp.broadcast_to(jnp.exp2(m_old - m_new), (ATT_BLK, LANES)))
                    ps.append(jnp.exp2(s - m_new).astype(BF16))
                    o = c[1 + hd]
                    for t in tt:
                        o = o + t
                    new.append(o)
                alpha = jnp.where(lo_q, alphas[0], alphas[1])
                r = pair_values(jnp.concatenate(ps, axis=1), rows_of(v_ref, blocks, g))
                acc_ref[g] = alpha * acc_ref[g] + r[:, 0:LANES]
                l_ref[g] = alpha * l_ref[g] + r[:, LANES:]
            return tuple(new)

        far = lax.while_loop(far_cond, far_body, (jnp.int32(1),) + tuple(offs_far))

        @pl.when(far[0] > 1)
        def _():
            for g in range(n_pairs):
                gated_output(rows, g, acc_ref[g], l_ref[g])
        return carry

    @pl.when(step == 0)
    def _():
        straight_ref[0] = 0

    straight = straight_ref[0] == 1
    pl.when(straight)(straight_step)

    @pl.when(jnp.logical_not(straight))
    def _():
        keep_step_rows()
        check_next_step()
        lax.fori_loop(0, nsub, sub_body, 0)
        merge()


def _prompt_attention(tot, qmx, kg, q, cq, nq, ck, kb, vb, sa, zp, x2, ada, wo, fg, *, tm, n_heads):
    rows, d = x2.shape
    a_w = n_heads * HEAD_DIM
    n_pairs = n_heads // PAIR
    assert rows % tm == 0 and tm % ATT_BLK == 0 and tm // ATT_BLK >= NEAR_BLOCKS - 1 and n_heads % PAIR == 0
    row_blk = lambda w: pl.BlockSpec((tm, w), lambda i, *_: (i, 0))
    grid_spec = pltpu.PrefetchScalarGridSpec(
        num_scalar_prefetch=3,
        grid=(rows // tm,),
        in_specs=[row_blk(a_w), row_blk(n_heads), row_blk(n_heads), _resident(ck.shape), row_blk(a_w), row_blk(a_w),
                  row_blk(a_w), row_blk(zp.shape[1]), row_blk(d), _resident(ada.shape), _resident(wo.shape),
                  _resident((1, d))],
        out_specs=row_blk(d),
        scratch_shapes=[pltpu.VMEM(kb.shape, BF16),
                        pltpu.VMEM(vb.shape, BF16),
                        pltpu.VMEM((tm, a_w), BF16),
                        pltpu.VMEM((n_heads, ATT_BLK, 1), F32),
                        pltpu.VMEM((n_pairs, ATT_BLK, LANES), F32),
                        pltpu.VMEM((n_pairs, ATT_BLK, LANES), F32),
                        pltpu.SMEM((1,), jnp.int32)],
    )
    return pl.pallas_call(
        functools.partial(_attn_kernel, tm=tm, n_heads=n_heads),
        grid_spec=grid_spec,
        out_shape=jax.ShapeDtypeStruct((rows, d), F32),
        compiler_params=pltpu.CompilerParams(dimension_semantics=("arbitrary",), vmem_limit_bytes=VMEM_LIMIT),
        name="attn",
    )(tot, qmx, kg, q, cq, nq, ck, kb, vb, sa, zp, x2, ada, wo, fg)


def _sattn_kernel(q_ref, cq_ref, ckn_ref, kn_ref, vn_ref, kc_ref, vc_ref, lfc_ref, sa_ref, zp_ref, x_ref,
                  ada_ref, wo_ref, fg_ref, y_ref, *, n_heads):
    a_w = n_heads * HEAD_DIM
    ln = q_ref.shape[0]
    past = kc_ref.shape[2]
    nb = past // LANES

    lfc = lfc_ref[0] * LOG2E
    triu = (lax.broadcasted_iota(jnp.int32, (LANES, LANES), 0)
            <= lax.broadcasted_iota(jnp.int32, (LANES, LANES), 1)).astype(BF16)
    zeros = jnp.zeros((8, LANES), F32)
    parts = []
    for b in range(nb):
        parts.extend(_split3(lfc[:, b * LANES:(b + 1) * LANES]))
        parts.append(zeros)
    cs = _dot(jnp.concatenate(parts, axis=0).astype(BF16), triu)
    after = jnp.zeros((n_heads, 1), F32)
    suffix = [None] * nb
    for b in reversed(range(nb)):
        cb = cs[32 * b:32 * b + 8] + cs[32 * b + 8:32 * b + 16] + cs[32 * b + 16:32 * b + 24]
        tot = cb[:, LANES - 1:LANES]
        suffix[b] = (tot - cb) + after
        after = after + tot
    dec_c = jnp.concatenate(suffix, axis=1)

    lane = lax.broadcasted_iota(jnp.int32, (ln, LANES), 1)
    half = [lane < HEAD_DIM, lane >= HEAD_DIM]
    causal = lax.broadcasted_iota(jnp.int32, (ln, ln), 1) <= lax.broadcasted_iota(jnp.int32, (ln, ln), 0)
    outs = []
    for g in range(n_heads // PAIR):
        cols = slice(g * LANES, (g + 1) * LANES)
        q2 = q_ref[:, cols]
        kct = kc_ref[0, cols, :].astype(BF16)
        vct = vc_ref[0, cols, :].astype(BF16)
        kn = kn_ref[:, cols]
        vn = vn_ref[:, cols]
        qst = jnp.concatenate([jnp.where(half[e], q2, jnp.zeros_like(q2)) for e in range(PAIR)], axis=0)
        sc_st = _dot(qst, kct)
        sn_st = _dot_nt(qst, kn)
        pc, pn, ls = [], [], []
        for e in range(PAIR):
            hd = PAIR * g + e
            rows = slice(e * ln, (e + 1) * ln)
            cqh = cq_ref[:, hd:hd + 1]
            s_c = sc_st[rows] + cqh + dec_c[hd:hd + 1, :]
            s_n = jnp.where(causal, sn_st[rows] + cqh - ckn_ref[0, hd:hd + 1, :], NEG)
            m = jnp.maximum(jnp.max(s_c, axis=1, keepdims=True), jnp.max(s_n, axis=1, keepdims=True))
            p_c = jnp.exp2(s_c - m)
            p_n = jnp.exp2(s_n - m)
            ls.append(jnp.sum(p_c, axis=1, keepdims=True) + jnp.sum(p_n, axis=1, keepdims=True))
            pc.append(p_c.astype(BF16))
            pn.append(p_n.astype(BF16))
        acc = _dot_nt(jnp.concatenate(pc, axis=0), vct) + _dot(jnp.concatenate(pn, axis=0), vn)
        o = jnp.where(half[0], acc[0:ln] / ls[0], acc[ln:2 * ln] / ls[1])
        outs.append((o * sa_ref[:, cols].astype(F32)).astype(BF16))
    za = jnp.concatenate(outs, axis=1)
    y_ref[...] = _merge_norm(x_ref[...], ada_ref[2, pl.ds(ADA_SAMPLE_ROW + pl.program_id(0), 1), :], za, zp_ref[...], wo_ref, fg_ref[...], a_w)


def _sample_attention(q, cq, ckn, kb, vb, cache_k, cache_v, lfc, sa, zp, x2, ada, wo, fg, *, ln, n_heads):
    rows, d = x2.shape
    nbatch = rows // ln
    a_w = n_heads * HEAD_DIM
    past = cache_k.shape[2]
    row_blk = lambda w: pl.BlockSpec((ln, w), lambda b: (b, 0))
    per_b = lambda s: pl.BlockSpec((1,) + s, lambda b: (b, 0, 0))
    return pl.pallas_call(
        functools.partial(_sattn_kernel, n_heads=n_heads),
        grid=(nbatch,),
        in_specs=[row_blk(a_w), row_blk(n_heads), per_b((n_heads, ln)), row_blk(a_w), row_blk(a_w),
                  per_b((a_w, past)), per_b((a_w, past)), per_b((n_heads, past)),
                  row_blk(a_w), row_blk(zp.shape[1]), row_blk(d), _resident(ada.shape), _resident(wo.shape),
                  _resident((1, d))],
        out_specs=row_blk(d),
        out_shape=jax.ShapeDtypeStruct((rows, d), F32),
        compiler_params=pltpu.CompilerParams(dimension_semantics=("arbitrary",), vmem_limit_bytes=VMEM_LIMIT),
        name="sattn",
    )(q, cq, ckn, kb, vb, cache_k, cache_v, lfc, sa, zp, x2, ada, wo, fg)


def kernel(x_prompt, x_sample, c_prompt, c_sample, cache_k, cache_v, cache_logf, state_pool, norm_g, w_ada, b_ada,
           w_in, b_f, w_pool, pool_scale, w_out, final_g):
    depth = norm_g.shape[0]
    assert depth == 1
    bp, seq, d = x_prompt.shape
    bs, ln, _ = x_sample.shape
    assert bp == 1
    n_heads = cache_k.shape[3]
    past = cache_k.shape[2]
    a_w = n_heads * HEAD_DIM
    pw = state_pool.shape[3]
    assert pw == len(POOL_WINDOWS) * LANES and cache_k.shape[4] == HEAD_DIM and n_heads <= 8

    ada = _ada_terms(c_prompt, c_sample, w_ada[0], b_ada)

    wit = w_in[0].T
    wq = wit[0:4 * a_w].astype(BF16)
    wf = jnp.pad(wit[4 * a_w:4 * a_w + n_heads], ((0, LANES - n_heads), (0, 0))).astype(BF16)
    wu = wit[4 * a_w + n_heads:].astype(BF16)
    wp = w_pool[0]
    ps = pool_scale[0][None, :]
    wo = w_out[0]
    ng = norm_g[0][None, :]
    fg = final_g[None, :]

    bm = 1024
    xp2 = x_prompt.reshape(seq, d)
    hist_p = jnp.zeros((1, HIST_PAD, pw), F32)
    (q_p, k_p, v_p, kb_p, vb_p, sa_p, zp_p, lf_p, cq_p, nq_p, ck_p, tot, qmx, kg, ho_p) = _project(
        xp2, ada, ng, wq, wf, wu, b_f[0], wp, ps, hist_p,
        bm=bm, sb=ATT_BLK, segs=1, ada_row=ADA_PROMPT_ROW, start_pos=0, n_heads=n_heads, kv_head_major=False)
    y_p = _prompt_attention(tot, qmx, kg, q_p, cq_p, nq_p, ck_p, kb_p, vb_p, sa_p, zp_p, xp2, ada, wo, fg,
                            tm=512, n_heads=n_heads)

    xs2 = x_sample.reshape(bs * ln, d)
    hist_s = jnp.pad(state_pool[0], ((0, 0), (HIST_PAD - POOL_HIST, 0), (0, 0)))
    (q_s, k_s, v_s, kb_s, vb_s, sa_s, zp_s, lf_s, cq_s, _, ck_s, _, _, _, ho_s) = _project(
        xs2, ada, ng, wq, wf, wu, b_f[0], wp, ps, hist_s,
        bm=bs * ln, sb=ln, segs=bs, ada_row=ADA_SAMPLE_ROW, start_pos=past, n_heads=n_heads, kv_head_major=True)
    lfc = jnp.swapaxes(cache_logf[0], 1, 2)
    ckt = jnp.transpose(cache_k[0], (0, 2, 3, 1)).reshape(bs, a_w, past)
    cvt = jnp.transpose(cache_v[0], (0, 2, 3, 1)).reshape(bs, a_w, past)
    y_s = _sample_attention(q_s, cq_s, ck_s, kb_s, vb_s, ckt, cvt, lfc, sa_s, zp_s, xs2, ada, wo, fg,
                            ln=ln, n_heads=n_heads)

    hd = (n_heads, HEAD_DIM)
    seq_minor = lambda t: jnp.transpose(t.reshape(hd + (bp, seq)), (2, 3, 0, 1))[None]
    return (y_p.reshape(bp, seq, d), y_s.reshape(bs, ln, d),
            seq_minor(k_p), seq_minor(v_p), jnp.swapaxes(lf_p, 1, 2)[None],
            ho_p[:, 16 - POOL_HIST:, :][None],
            k_s.reshape((1, bs, ln) + hd), v_s.reshape((1, bs, ln) + hd), jnp.swapaxes(lf_s, 1, 2)[None],
            ho_s[:, 16 - POOL_HIST:, :][None])
```

```python
import functools

import jax
import jax.numpy as jnp
from jax import lax
from jax.experimental import pallas as pl
from jax.experimental.pallas import tpu as pltpu

HEAD_DIM = 64
POOL_WINDOWS = (2, 4, 8, 16)
EPS = 1e-6

LANES = 128
PAIR = LANES // HEAD_DIM
ATT_BLK = 128
NEAR_BLOCKS = 3
LOG2E = 1.4426950408889634
EXP2_UNDERFLOW = 151.0
EXP2_SAFE_SPAN = 100.0
NORM_SLACK = 1.01
HIST_PAD = 32
POOL_HIST = max(POOL_WINDOWS) - 1
NEG = -1e30
VMEM_LIMIT = 60 * 1024 * 1024

F32 = jnp.float32
BF16 = jnp.bfloat16


def _silu(x):
    return x * jax.nn.sigmoid(x)


def _dot(a, b):
    return jnp.dot(a, b, preferred_element_type=F32)


def _dot_nt(a, b):
    return lax.dot_general(a, b, (((1,), (1,)), ((), ())), preferred_element_type=F32)


def _split3(x):
    hi = x.astype(BF16).astype(F32)
    r1 = x - hi
    mid = r1.astype(BF16).astype(F32)
    return hi, mid, r1 - mid


def _rows_to_lanes(x, n):
    rows = x.shape[0]
    if rows < LANES:
        x = jnp.concatenate([x, jnp.zeros((LANES - rows, LANES), x.dtype)], axis=0)
    return x.T[0:n, 0:rows]


def _resident(shape):
    return pl.BlockSpec(shape, lambda *_: (0,) * len(shape), pipeline_mode=pl.Buffered(1))


ADA_ROWS = 16
ADA_PROMPT_ROW, ADA_SAMPLE_ROW = 0, 8


def _ada_kernel(cp_ref, cs_ref, w_ref, b_ref, o_ref):
    ap = jnp.broadcast_to(_silu(cp_ref[...]), (ADA_SAMPLE_ROW, cp_ref.shape[1]))
    a = jnp.concatenate([ap, _silu(cs_ref[...])], axis=0).astype(BF16)
    o_ref[0] = _dot(a, w_ref[...].astype(BF16)) + b_ref[...]


def _ada_terms(c_prompt, c_sample, w_ada, b_ada):
    d = c_prompt.shape[1]
    assert c_prompt.shape[0] == 1 and c_sample.shape[0] == ADA_ROWS - ADA_SAMPLE_ROW and w_ada.shape[1] == 3 * d
    split = 2
    bn = d // split
    return pl.pallas_call(
        _ada_kernel,
        grid=(3 * split,),
        in_specs=[pl.BlockSpec(c_prompt.shape, lambda j: (0, 0)),
                  pl.BlockSpec(c_sample.shape, lambda j: (0, 0)),
                  pl.BlockSpec((d, bn), lambda j: (0, j)),
                  pl.BlockSpec((1, bn), lambda j: (0, j))],
        out_specs=pl.BlockSpec((1, ADA_ROWS, bn), lambda j: (j // split, 0, j % split)),
        out_shape=jax.ShapeDtypeStruct((3, ADA_ROWS, d), F32),
        compiler_params=pltpu.CompilerParams(dimension_semantics=("arbitrary",), vmem_limit_bytes=VMEM_LIMIT),
        name="ada",
    )(c_prompt, c_sample, w_ada, b_ada)


def _proj_kernel(bf_ref, x_ref, ada_ref, ng_ref, w_ref, wp_ref, ps_ref, h0_ref,
                 q_ref, k32_ref, v32_ref, kb_ref, vb_ref, sa_ref, zp_ref, lf_ref, cq_ref, nq_ref, ck_ref,
                 tot_ref, qmx_ref, kg_ref, ho_ref,
                 e_ref, t2_ref, t4_ref, t8_ref, kmx_ref, *, bm, sb, segs, ada_row, start_pos, n_heads, kv_head_major):
    a_w = n_heads * HEAD_DIM
    seg_rows = bm // segs
    step = pl.program_id(0)
    x = x_ref[...]
    ms = jnp.mean(x * x, axis=-1, keepdims=True)
    xn = x * lax.rsqrt(ms + EPS) * ng_ref[...]
    h = jnp.concatenate(
        [xn[g * seg_rows:(g + 1) * seg_rows] * (1.0 + ada_ref[1, ada_row + g:ada_row + g + 1, :])
         + ada_ref[0, ada_row + g:ada_row + g + 1, :] for g in range(segs)], axis=0).astype(BF16)

    pw = len(POOL_WINDOWS) * LANES
    sc = LOG2E / (HEAD_DIM ** 0.5)
    o_pool = 4 * a_w + n_heads

    def w_rows(lo, hi):
        return w_ref[lo:hi, :].astype(BF16)

    pu = _dot_nt(h, w_rows(o_pool, o_pool + pw))
    qs = _dot_nt(h, w_rows(0, a_w)) * sc
    q_ref[...] = qs.astype(BF16)
    sel = (lax.broadcasted_iota(jnp.int32, (a_w, LANES), 0) // HEAD_DIM
           == lax.broadcasted_iota(jnp.int32, (a_w, LANES), 1)).astype(BF16)
    nq = jnp.sqrt(_dot((qs * qs).astype(BF16), sel)) * NORM_SLACK
    nq_ref[...] = nq[:, 0:n_heads]

    def store_kv(ref32, refb, p):
        refb[...] = p.astype(BF16)
        if kv_head_major:
            for hd in range(n_heads):
                ref32[:, hd, :] = p[:, hd * HEAD_DIM:(hd + 1) * HEAD_DIM]
        else:
            ref32[...] = p.T

    ext = HIST_PAD + seg_rows
    n = segs * ext

    def load_history():
        for g in range(segs):
            e_ref[g * ext:g * ext + HIST_PAD, :] = h0_ref[g]

    if segs > 1:
        load_history()
    else:
        pl.when(step == 0)(load_history)

    for g in range(segs):
        e_ref[g * ext + HIST_PAD:(g + 1) * ext, :] = pu[g * seg_rows:(g + 1) * seg_rows]
    t2_ref[8:n, :] = e_ref[8:n, :] + e_ref[7:n - 1, :]
    t4_ref[16:n, :] = t2_ref[16:n, LANES:] + t2_ref[14:n - 2, LANES:]
    t8_ref[24:n, :] = t4_ref[24:n, LANES:] + t4_ref[20:n - 4, LANES:]

    def seg_rows_of(ref, cols, back=0):
        return jnp.concatenate([ref[g * ext + HIST_PAD - back:(g + 1) * ext - back, cols] for g in range(segs)], axis=0)

    lane0, lane1 = slice(0, LANES), slice(LANES, 2 * LANES)
    sums = [seg_rows_of(t2_ref, lane0), seg_rows_of(t4_ref, lane0), seg_rows_of(t8_ref, lane0),
            seg_rows_of(t8_ref, lane1) + seg_rows_of(t8_ref, lane1, back=8)]
    row = lax.broadcasted_iota(jnp.int32, (bm, 1), 0)
    pos1 = start_pos + 1 + (step * bm + row if segs == 1 else row % seg_rows)
    pool_d = []
    for g, w in enumerate(POOL_WINDOWS):
        rc = 1.0 / jnp.minimum(pos1, w).astype(F32)
        pool_d.append((sums[g] * rc - pu[:, g * LANES:(g + 1) * LANES]).astype(BF16))
    for g in range(segs):
        ho_ref[g] = e_ref[(g + 1) * ext - 16:(g + 1) * ext, :]
    if segs == 1:
        e_ref[0:HIST_PAD, :] = e_ref[bm:n, :]

    pk = _dot_nt(h, w_rows(a_w, 2 * a_w))
    store_kv(k32_ref, kb_ref, pk)
    nk = jnp.sqrt(_dot((pk * pk).astype(BF16), sel)) * NORM_SLACK
    spg = _silu(_dot_nt(h, w_rows(o_pool + pw, o_pool + 2 * pw)))

    lane = lax.broadcasted_iota(jnp.int32, (1, LANES), 1)
    bias = jnp.zeros((1, LANES), F32)
    for hd in range(n_heads):
        bias = jnp.where(lane == hd, bf_ref[hd], bias)
    wf = jnp.concatenate([w_ref[4 * a_w:o_pool, :], jnp.zeros((LANES - n_heads, w_ref.shape[1]), F32)], axis=0)
    z = _dot_nt(h, wf.astype(BF16)) + bias
    lf = jnp.minimum(z, 0.0) - jnp.log1p(jnp.exp(-jnp.abs(z)))

    sa_ref[...] = _silu(_dot_nt(h, w_rows(3 * a_w, 4 * a_w))).astype(BF16)

    tri = (lax.broadcasted_iota(jnp.int32, (sb, sb), 1)
           <= lax.broadcasted_iota(jnp.int32, (sb, sb), 0)).astype(BF16)
    lf2 = lf * LOG2E
    tots, qmx, kmx = [], [], []
    for s in range(bm // sb):
        rows = slice(s * sb, (s + 1) * sb)
        parts = _dot(tri, jnp.concatenate(_split3(lf2[rows]), axis=1).astype(BF16))
        cb = parts[:, 0:LANES] + parts[:, LANES:2 * LANES] + parts[:, 2 * LANES:]
        cq_ref[rows, :] = cb[:, 0:n_heads]
        ck_ref[s] = _rows_to_lanes(cb, n_heads)
        off = (s * sb) % seg_rows
        lf_ref[(s * sb) // seg_rows, :, off:off + sb] = _rows_to_lanes(lf[rows], n_heads)
        tots.append(cb[sb - 1:sb, :])
        qmx.append(jnp.max(nq[rows], axis=0, keepdims=True))
        kmx.append(jnp.max(nk[rows], axis=0, keepdims=True))
    tot_ref[...] = jnp.concatenate(tots, axis=0)[:, 0:n_heads]
    qmx_ref[...] = jnp.concatenate(qmx, axis=0)[:, 0:n_heads]
    kmax = kmx[0]
    for t in kmx[1:]:
        kmax = jnp.maximum(kmax, t)

    store_kv(v32_ref, vb_ref, _dot_nt(h, w_rows(2 * a_w, 3 * a_w)))

    zero_w = jnp.zeros((LANES, LANES), BF16)
    for g in range(0, len(POOL_WINDOWS), 2):
        cols = slice(g * LANES, (g + 2) * LANES)
        w2 = jnp.concatenate([jnp.concatenate([wp_ref[g].astype(BF16), zero_w], axis=1),
                              jnp.concatenate([zero_w, wp_ref[g + 1].astype(BF16)], axis=1)], axis=0)
        y = _dot(jnp.concatenate([pool_d[g], pool_d[g + 1]], axis=1), w2) * ps_ref[:, cols]
        zp_ref[:, cols] = (y * spg[:, cols]).astype(BF16)

    @pl.when(step > 0)
    def _():
        kmx_ref[...] = jnp.maximum(kmx_ref[...], kmax)

    @pl.when(step == 0)
    def _():
        kmx_ref[...] = kmax

    kg_ref[...] = kmx_ref[...]


def _project(x2, ada, norm_g, wit, b_f, wp, ps, hist0, *, bm, sb, segs, ada_row, start_pos, n_heads,
             kv_head_major):
    rows, d = x2.shape
    a_w = n_heads * HEAD_DIM
    pw = len(POOL_WINDOWS) * LANES
    n_steps = rows // bm
    assert segs == 1 or n_steps == 1
    n_streams = segs
    seg_rows = bm // segs
    nsb = bm // sb
    assert nsb == 8 and seg_rows % sb == 0
    row_blk = lambda w: pl.BlockSpec((bm, w), lambda i, *_: (i, 0))
    per_stream = lambda r, w: pl.BlockSpec((segs, r, w), lambda i, *_: (0, 0, 0))
    kern = functools.partial(_proj_kernel, bm=bm, sb=sb, segs=segs, ada_row=ada_row, start_pos=start_pos,
                             n_heads=n_heads, kv_head_major=kv_head_major)
    if kv_head_major:
        kv_shape = (rows, n_heads, HEAD_DIM)
        kv_blk = pl.BlockSpec((bm, n_heads, HEAD_DIM), lambda i, *_: (i, 0, 0))
    else:
        kv_shape = (a_w, rows)
        kv_blk = pl.BlockSpec((a_w, bm), lambda i, *_: (0, i))
    out_shape = (
        jax.ShapeDtypeStruct((rows, a_w), BF16),
        jax.ShapeDtypeStruct(kv_shape, F32),
        jax.ShapeDtypeStruct(kv_shape, F32),
        jax.ShapeDtypeStruct((rows, a_w), BF16),
        jax.ShapeDtypeStruct((rows, a_w), BF16),
        jax.ShapeDtypeStruct((rows, a_w), BF16),
        jax.ShapeDtypeStruct((rows, pw), BF16),
        jax.ShapeDtypeStruct((n_streams, n_heads, rows // n_streams), F32),
        jax.ShapeDtypeStruct((rows, n_heads), F32),
        jax.ShapeDtypeStruct((rows, n_heads), F32),
        jax.ShapeDtypeStruct((rows // sb, n_heads, sb), F32),
        jax.ShapeDtypeStruct((rows // sb, n_heads), F32),
        jax.ShapeDtypeStruct((rows // sb, n_heads), F32),
        jax.ShapeDtypeStruct((1, LANES), F32),
        jax.ShapeDtypeStruct((n_streams, 16, pw), F32),
    )
    out_specs = (
        row_blk(a_w), kv_blk, kv_blk, row_blk(a_w), row_blk(a_w), row_blk(a_w), row_blk(pw),
        pl.BlockSpec((segs, n_heads, seg_rows), lambda i, *_: (0, 0, i)),
        row_blk(n_heads), row_blk(n_heads),
        pl.BlockSpec((nsb, n_heads, sb), lambda i, *_: (i, 0, 0)),
        pl.BlockSpec((nsb, n_heads), lambda i, *_: (i, 0)),
        pl.BlockSpec((nsb, n_heads), lambda i, *_: (i, 0)),
        pl.BlockSpec((1, LANES), lambda i, *_: (0, 0)),
        per_stream(16, pw),
    )
    in_specs = [
        row_blk(d),
        _resident(ada.shape),
        _resident((1, d)),
        _resident(wit.shape), _resident(wp.shape), _resident(ps.shape),
        per_stream(HIST_PAD, pw),
    ]
    return pl.pallas_call(
        kern,
        grid_spec=pltpu.PrefetchScalarGridSpec(
            num_scalar_prefetch=1,
            grid=(n_steps,),
            in_specs=in_specs,
            out_specs=out_specs,
            scratch_shapes=[pltpu.VMEM((bm + segs * HIST_PAD, pw), F32),
                            pltpu.VMEM((bm + segs * HIST_PAD, pw), F32),
                            pltpu.VMEM((bm + segs * HIST_PAD, pw - LANES), F32),
                            pltpu.VMEM((bm + segs * HIST_PAD, pw - 2 * LANES), F32),
                            pltpu.VMEM((1, LANES), F32)]),
        out_shape=out_shape,
        compiler_params=pltpu.CompilerParams(dimension_semantics=("arbitrary",), vmem_limit_bytes=VMEM_LIMIT),
        name="proj",
    )(b_f, x2, ada, norm_g, wit, wp, ps, hist0)


def _merge_norm(x, gate, za, zp, wo_ref, fg, a_w):
    dy = _dot(za, wo_ref[0:a_w, :].astype(BF16)) + _dot(zp, wo_ref[a_w:, :].astype(BF16))
    out = x + gate * dy
    ms = jnp.mean(out * out, axis=-1, keepdims=True)
    return out * lax.rsqrt(ms + EPS) * fg


def _attn_kernel(tot_ref, qmx_ref, kg_ref,
                 q_ref, cq_ref, nq_ref, ck_ref, knew_ref, vnew_ref, sa_ref, zp_ref, x_ref, ada_ref, wo_ref, fg_ref,
                 y_ref,
                 k_ref, v_ref, z_ref, m_ref, l_ref, acc_ref, straight_ref, *, tm, n_heads):
    a_w = n_heads * HEAD_DIM
    n_pairs = n_heads // PAIR
    nsub = tm // ATT_BLK
    w_near = NEAR_BLOCKS
    cw = w_near * ATT_BLK
    step = pl.program_id(0)

    def keep_step_rows():
        k_ref[pl.ds(pl.multiple_of(step * tm, tm), tm), :] = knew_ref[...]
        v_ref[pl.ds(pl.multiple_of(step * tm, tm), tm), :] = vnew_ref[...]

    lo_q = lax.broadcasted_iota(jnp.int32, (ATT_BLK, LANES), 1) < HEAD_DIM
    lo_k = lax.broadcasted_iota(jnp.int32, (cw, LANES), 1) < HEAD_DIM
    col = lax.broadcasted_iota(jnp.int32, (ATT_BLK, cw), 1)
    tri = (lax.broadcasted_iota(jnp.int32, (ATT_BLK, ATT_BLK), 1)
           <= lax.broadcasted_iota(jnp.int32, (ATT_BLK, ATT_BLK), 0))
    tri_bias = jnp.where(tri, 0.0, NEG).astype(F32)
    zeros_k = jnp.zeros((cw, LANES), BF16)
    ind_lo = jnp.where(lo_k, 1.0, 0.0).astype(BF16)
    ind_hi = jnp.where(lo_k, 0.0, 1.0).astype(BF16)

    def tot_at(b, hd):
        return jnp.where(b >= 0, tot_ref[jnp.maximum(b, 0), hd], 0.0)

    def rows_of(ref, blocks, g):
        return jnp.concatenate(
            [ref[pl.ds(pl.multiple_of(b * ATT_BLK, ATT_BLK), ATT_BLK), g * LANES:(g + 1) * LANES] for b in blocks],
            axis=0)

    def step_rows_of(ref, new_ref, sub, g):
        parts = []
        for p in range(w_near):
            rel = sub - (w_near - 1) + p
            if rel >= 0:
                parts.append(new_ref[rel * ATT_BLK:(rel + 1) * ATT_BLK, g * LANES:(g + 1) * LANES])
            else:
                start = pl.multiple_of((step * nsub + rel) * ATT_BLK, ATT_BLK)
                parts.append(ref[pl.ds(start, ATT_BLK), g * LANES:(g + 1) * LANES])
        return jnp.concatenate(parts, axis=0)

    def pair_scores(r0, kc, g):
        keys = jnp.concatenate([jnp.where(lo_k, kc, zeros_k), jnp.where(lo_k, zeros_k, kc)], axis=0)
        return _dot_nt(q_ref[pl.ds(r0, ATT_BLK), g * LANES:(g + 1) * LANES], keys)

    def pair_values(p_pair, vc):
        vals = jnp.concatenate([jnp.concatenate([jnp.where(lo_k, vc, zeros_k), ind_lo], axis=1),
                                jnp.concatenate([jnp.where(lo_k, zeros_k, vc), ind_hi], axis=1)], axis=0)
        return _dot(p_pair, vals)

    def decay_row(blocks, offs, hd):
        return jnp.concatenate([offs[p] - ck_ref[blocks[p], hd:hd + 1, :] for p in range(w_near)], axis=1)

    def near_weights(s_pair, rows, g, near_c, near_offs, bounded):
        ps, ms = [], []
        for e in range(PAIR):
            hd = PAIR * g + e
            cqh = cq_ref[rows, hd:hd + 1]
            if bounded:
                m = nq_ref[rows, hd:hd + 1] * kg_ref[0, hd]
                cqh = cqh - m
            dec = decay_row(near_c, near_offs[hd], hd)
            pieces = []
            for p in range(w_near):
                lanes = slice(e * cw + p * ATT_BLK, e * cw + (p + 1) * ATT_BLK)
                sp = s_pair[:, lanes] + cqh + dec[:, p * ATT_BLK:(p + 1) * ATT_BLK]
                pieces.append(sp + tri_bias if p == w_near - 1 else sp)
            s = jnp.concatenate(pieces, axis=1)
            if not bounded:
                m = jnp.max(s, axis=1, keepdims=True)
                s = s - m
            ms.append(m)
            ps.append(jnp.exp2(s).astype(BF16))
        return jnp.concatenate(ps, axis=1), ms

    def gated_output(rows, g, acc, l):
        cols = slice(g * LANES, (g + 1) * LANES)
        z_ref[rows, cols] = ((acc / l) * sa_ref[rows, cols].astype(F32)).astype(BF16)

    def merge():
        y_ref[...] = _merge_norm(x_ref[...], ada_ref[2, ADA_PROMPT_ROW:ADA_PROMPT_ROW + 1, :], z_ref[...], zp_ref[...],
                                 wo_ref, fg_ref[...], a_w)

    def check_next_step():
        ok = jnp.bool_(True)
        last = qmx_ref.shape[0] - 1
        for sub in range(nsub):
            i = jnp.minimum((step + 1) * nsub + sub, last)
            for hd in range(n_heads):
                qk = 2.0 * qmx_ref[i, hd] * kg_ref[0, hd]
                back = qk
                for dd in range(1, w_near):
                    back = back + tot_ref[i - dd, hd]
                ok = jnp.logical_and(ok, jnp.logical_and(qk <= EXP2_SAFE_SPAN, back < -EXP2_UNDERFLOW))
        straight_ref[0] = ok.astype(jnp.int32)

    def straight_step():
        units = [(sub, g) for sub in range(nsub) for g in range(n_pairs)]
        near_of, offs_of = [], []
        for sub in range(nsub):
            i = step * nsub + sub
            near_of.append([i - (w_near - 1) + p for p in range(w_near)])
            offs = []
            for hd in range(n_heads):
                o, per_piece = jnp.float32(0.0), [jnp.float32(0.0)]
                for dd in range(1, w_near):
                    o = o + tot_ref[i - dd, hd]
                    per_piece.append(o)
                offs.append(per_piece[::-1])
            offs_of.append(offs)
        s_next = pair_scores(0, step_rows_of(k_ref, knew_ref, 0, 0), 0)
        for u, (sub, g) in enumerate(units):
            s_pair = s_next
            if u + 1 < len(units):
                sub1, g1 = units[u + 1]
                s_next = pair_scores(sub1 * ATT_BLK, step_rows_of(k_ref, knew_ref, sub1, g1), g1)
            rows = pl.ds(sub * ATT_BLK, ATT_BLK)
            p_pair, _ = near_weights(s_pair, rows, g, near_of[sub], offs_of[sub], True)
            r = pair_values(p_pair, step_rows_of(v_ref, vnew_ref, sub, g))
            gated_output(rows, g, r[:, 0:LANES], r[:, LANES:])
        check_next_step()
        keep_step_rows()
        merge()

    def sub_body(sub, carry):
        i = step * nsub + sub
        r0 = pl.multiple_of(sub * ATT_BLK, ATT_BLK)

        rows = pl.ds(r0, ATT_BLK)
        qk = [2.0 * qmx_ref[i, hd] * kg_ref[0, hd] for hd in range(n_heads)]

        near = [i - (w_near - 1) + p for p in range(w_near)]
        near_c = [jnp.maximum(b, 0) for b in near]
        near_offs, offs_far = [], []
        for hd in range(n_heads):
            back = [tot_at(i - dd, hd) for dd in range(1, w_near)]
            offs = []
            for p in range(w_near):
                o = jnp.float32(0.0)
                for dd in range(1, w_near - p):
                    o = o + back[dd - 1]
                offs.append(jnp.where(near[p] >= 0, o, NEG))
            near_offs.append(offs)
            o = jnp.float32(0.0)
            for t in back:
                o = o + t
            offs_far.append(o)

        def near_chunk(bounded):
            def fn():
                s_next = pair_scores(r0, rows_of(k_ref, near_c, 0), 0)
                for g in range(n_pairs):
                    s_pair = s_next
                    if g + 1 < n_pairs:
                        s_next = pair_scores(r0, rows_of(k_ref, near_c, g + 1), g + 1)
                    p_pair, ms = near_weights(s_pair, rows, g, near_c, near_offs, bounded)
                    for e in range(PAIR):
                        m_ref[PAIR * g + e] = ms[e]
                    r = pair_values(p_pair, rows_of(v_ref, near_c, g))
                    acc_ref[g] = r[:, 0:LANES]
                    l_ref[g] = r[:, LANES:]
                    gated_output(rows, g, r[:, 0:LANES], r[:, LANES:])
            return fn

        bounded_ok = qk[0] <= EXP2_SAFE_SPAN
        for hd in range(1, n_heads):
            bounded_ok = jnp.logical_and(bounded_ok, qk[hd] <= EXP2_SAFE_SPAN)
        pl.when(bounded_ok)(near_chunk(True))
        pl.when(jnp.logical_not(bounded_ok))(near_chunk(False))

        def far_cond(c):
            top = i - c[0] * w_near
            need = qk[0] + c[1] >= -EXP2_UNDERFLOW
            for hd in range(1, n_heads):
                need = jnp.logical_or(need, qk[hd] + c[1 + hd] >= -EXP2_UNDERFLOW)
            return jnp.logical_and(top >= 0, need)

        def far_body(c):
            top = i - c[0] * w_near
            jc = jnp.maximum(top - (w_near - 1), 0)
            blocks = [jc + p for p in range(w_near)]
            keepc = col < (top + 1 - jc) * ATT_BLK
            new = [c[0] + 1]
            for g in range(n_pairs):
                s_pair = pair_scores(r0, rows_of(k_ref, blocks, g), g)
                ps, alphas = [], []
                for e in range(PAIR):
                    hd = PAIR * g + e
                    tt = [tot_at(top - b, hd) for b in range(w_near)]
                    offs = []
                    for p in range(w_near):
                        behind = top - (jc + p)
                        o = c[1 + hd]
                        for b in range(w_near):
                            o = o + jnp.where(behind >= b, tt[b], 0.0)
                        offs.append(o)
                    s = s_pair[:, e * cw:(e + 1) * cw] + cq_ref[rows, hd:hd + 1] + decay_row(blocks, offs, hd)
                    s = jnp.where(keepc, s, NEG)
                    m_old = m_ref[hd]
                    m_new = jnp.maximum(m_old, jnp.max(s, axis=1, keepdims=True))
                    m_ref[hd] = m_new
                    alphas.append(jnp.broadcast_to(jnp.exp2(m_old - m_new), (ATT_BLK, LANES)))
                    ps.append(jnp.exp2(s - m_new).astype(BF16))
                    o = c[1 + hd]
                    for t in tt:
                        o = o + t
                    new.append(o)
                alpha = jnp.where(lo_q, alphas[0], alphas[1])
                r = pair_values(jnp.concatenate(ps, axis=1), rows_of(v_ref, blocks, g))
                acc_ref[g] = alpha * acc_ref[g] + r[:, 0:LANES]
                l_ref[g] = alpha * l_ref[g] + r[:, LANES:]
            return tuple(new)

        far = lax.while_loop(far_cond, far_body, (jnp.int32(1),) + tuple(offs_far))

        @pl.when(far[0] > 1)
        def _():
            for g in range(n_pairs):
                gated_output(rows, g, acc_ref[g], l_ref[g])
        return carry

    @pl.when(step == 0)
    def _():
        straight_ref[0] = 0

    straight = straight_ref[0] == 1
    pl.when(straight)(straight_step)

    @pl.when(jnp.logical_not(straight))
    def _():
        keep_step_rows()
        check_next_step()
        lax.fori_loop(0, nsub, sub_body, 0)
        merge()


def _prompt_attention(tot, qmx, kg, q, cq, nq, ck, kb, vb, sa, zp, x2, ada, wo, fg, *, tm, n_heads):
    rows, d = x2.shape
    a_w = n_heads * HEAD_DIM
    n_pairs = n_heads // PAIR
    assert rows % tm == 0 and tm % ATT_BLK == 0 and tm // ATT_BLK >= NEAR_BLOCKS - 1 and n_heads % PAIR == 0
    row_blk = lambda w: pl.BlockSpec((tm, w), lambda i, *_: (i, 0))
    grid_spec = pltpu.PrefetchScalarGridSpec(
        num_scalar_prefetch=3,
        grid=(rows // tm,),
        in_specs=[row_blk(a_w), row_blk(n_heads), row_blk(n_heads), _resident(ck.shape), row_blk(a_w), row_blk(a_w),
                  row_blk(a_w), row_blk(zp.shape[1]), row_blk(d), _resident(ada.shape), _resident(wo.shape),
                  _resident((1, d))],
        out_specs=row_blk(d),
        scratch_shapes=[pltpu.VMEM(kb.shape, BF16),
                        pltpu.VMEM(vb.shape, BF16),
                        pltpu.VMEM((tm, a_w), BF16),
                        pltpu.VMEM((n_heads, ATT_BLK, 1), F32),
                        pltpu.VMEM((n_pairs, ATT_BLK, LANES), F32),
                        pltpu.VMEM((n_pairs, ATT_BLK, LANES), F32),
                        pltpu.SMEM((1,), jnp.int32)],
    )
    return pl.pallas_call(
        functools.partial(_attn_kernel, tm=tm, n_heads=n_heads),
        grid_spec=grid_spec,
        out_shape=jax.ShapeDtypeStruct((rows, d), F32),
        compiler_params=pltpu.CompilerParams(dimension_semantics=("arbitrary",), vmem_limit_bytes=VMEM_LIMIT),
        name="attn",
    )(tot, qmx, kg, q, cq, nq, ck, kb, vb, sa, zp, x2, ada, wo, fg)


def _sattn_kernel(q_ref, cq_ref, ckn_ref, kn_ref, vn_ref, kc_ref, vc_ref, lfc_ref, sa_ref, zp_ref, x_ref,
                  ada_ref, wo_ref, fg_ref, y_ref, *, n_heads):
    a_w = n_heads * HEAD_DIM
    ln = q_ref.shape[0]
    past = kc_ref.shape[2]
    nb = past // LANES

    lfc = lfc_ref[0] * LOG2E
    triu = (lax.broadcasted_iota(jnp.int32, (LANES, LANES), 0)
            <= lax.broadcasted_iota(jnp.int32, (LANES, LANES), 1)).astype(BF16)
    zeros = jnp.zeros((8, LANES), F32)
    parts = []
    for b in range(nb):
        parts.extend(_split3(lfc[:, b * LANES:(b + 1) * LANES]))
        parts.append(zeros)
    cs = _dot(jnp.concatenate(parts, axis=0).astype(BF16), triu)
    after = jnp.zeros((n_heads, 1), F32)
    suffix = [None] * nb
    for b in reversed(range(nb)):
        cb = cs[32 * b:32 * b + 8] + cs[32 * b + 8:32 * b + 16] + cs[32 * b + 16:32 * b + 24]
        tot = cb[:, LANES - 1:LANES]
        suffix[b] = (tot - cb) + after
        after = after + tot
    dec_c = jnp.concatenate(suffix, axis=1)

    lane = lax.broadcasted_iota(jnp.int32, (ln, LANES), 1)
    half = [lane < HEAD_DIM, lane >= HEAD_DIM]
    causal = lax.broadcasted_iota(jnp.int32, (ln, ln), 1) <= lax.broadcasted_iota(jnp.int32, (ln, ln), 0)
    outs = []
    for g in range(n_heads // PAIR):
        cols = slice(g * LANES, (g + 1) * LANES)
        q2 = q_ref[:, cols]
        kct = kc_ref[0, cols, :].astype(BF16)
        vct = vc_ref[0, cols, :].astype(BF16)
        kn = kn_ref[:, cols]
        vn = vn_ref[:, cols]
        qst = jnp.concatenate([jnp.where(half[e], q2, jnp.zeros_like(q2)) for e in range(PAIR)], axis=0)
        sc_st = _dot(qst, kct)
        sn_st = _dot_nt(qst, kn)
        pc, pn, ls = [], [], []
        for e in range(PAIR):
            hd = PAIR * g + e
            rows = slice(e * ln, (e + 1) * ln)
            cqh = cq_ref[:, hd:hd + 1]
            s_c = sc_st[rows] + cqh + dec_c[hd:hd + 1, :]
            s_n = jnp.where(causal, sn_st[rows] + cqh - ckn_ref[0, hd:hd + 1, :], NEG)
            m = jnp.maximum(jnp.max(s_c, axis=1, keepdims=True), jnp.max(s_n, axis=1, keepdims=True))
            p_c = jnp.exp2(s_c - m)
            p_n = jnp.exp2(s_n - m)
            ls.append(jnp.sum(p_c, axis=1, keepdims=True) + jnp.sum(p_n, axis=1, keepdims=True))
            pc.append(p_c.astype(BF16))
            pn.append(p_n.astype(BF16))
        acc = _dot_nt(jnp.concatenate(pc, axis=0), vct) + _dot(jnp.concatenate(pn, axis=0), vn)
        o = jnp.where(half[0], acc[0:ln] / ls[0], acc[ln:2 * ln] / ls[1])
        outs.append((o * sa_ref[:, cols].astype(F32)).astype(BF16))
    za = jnp.concatenate(outs, axis=1)
    y_ref[...] = _merge_norm(x_ref[...], ada_ref[2, pl.ds(ADA_SAMPLE_ROW + pl.program_id(0), 1), :], za, zp_ref[...], wo_ref, fg_ref[...], a_w)


def _sample_attention(q, cq, ckn, kb, vb, cache_k, cache_v, lfc, sa, zp, x2, ada, wo, fg, *, ln, n_heads):
    rows, d = x2.shape
    nbatch = rows // ln
    a_w = n_heads * HEAD_DIM
    past = cache_k.shape[2]
    row_blk = lambda w: pl.BlockSpec((ln, w), lambda b: (b, 0))
    per_b = lambda s: pl.BlockSpec((1,) + s, lambda b: (b, 0, 0))
    return pl.pallas_call(
        functools.partial(_sattn_kernel, n_heads=n_heads),
        grid=(nbatch,),
        in_specs=[row_blk(a_w), row_blk(n_heads), per_b((n_heads, ln)), row_blk(a_w), row_blk(a_w),
                  per_b((a_w, past)), per_b((a_w, past)), per_b((n_heads, past)),
                  row_blk(a_w), row_blk(zp.shape[1]), row_blk(d), _resident(ada.shape), _resident(wo.shape),
                  _resident((1, d))],
        out_specs=row_blk(d),
        out_shape=jax.ShapeDtypeStruct((rows, d), F32),
        compiler_params=pltpu.CompilerParams(dimension_semantics=("arbitrary",), vmem_limit_bytes=VMEM_LIMIT),
        name="sattn",
    )(q, cq, ckn, kb, vb, cache_k, cache_v, lfc, sa, zp, x2, ada, wo, fg)


def kernel(x_prompt, x_sample, c_prompt, c_sample, cache_k, cache_v, cache_logf, state_pool, norm_g, w_ada, b_ada,
           w_in, b_f, w_pool, pool_scale, w_out, final_g):
    depth = norm_g.shape[0]
    assert depth == 1
    bp, seq, d = x_prompt.shape
    bs, ln, _ = x_sample.shape
    assert bp == 1
    n_heads = cache_k.shape[3]
    past = cache_k.shape[2]
    a_w = n_heads * HEAD_DIM
    pw = state_pool.shape[3]
    assert pw == len(POOL_WINDOWS) * LANES and cache_k.shape[4] == HEAD_DIM and n_heads <= 8

    ada = _ada_terms(c_prompt, c_sample, w_ada[0], b_ada)

    wit = w_in[0].T
    wp = w_pool[0]
    ps = pool_scale[0][None, :]
    wo = w_out[0]
    ng = norm_g[0][None, :]
    fg = final_g[None, :]

    bm = 1024
    xp2 = x_prompt.reshape(seq, d)
    hist_p = jnp.zeros((1, HIST_PAD, pw), F32)
    (q_p, k_p, v_p, kb_p, vb_p, sa_p, zp_p, lf_p, cq_p, nq_p, ck_p, tot, qmx, kg, ho_p) = _project(
        xp2, ada, ng, wit, b_f[0], wp, ps, hist_p,
        bm=bm, sb=ATT_BLK, segs=1, ada_row=ADA_PROMPT_ROW, start_pos=0, n_heads=n_heads, kv_head_major=False)
    y_p = _prompt_attention(tot, qmx, kg, q_p, cq_p, nq_p, ck_p, kb_p, vb_p, sa_p, zp_p, xp2, ada, wo, fg,
                            tm=512, n_heads=n_heads)

    xs2 = x_sample.reshape(bs * ln, d)
    hist_s = jnp.pad(state_pool[0], ((0, 0), (HIST_PAD - POOL_HIST, 0), (0, 0)))
    (q_s, k_s, v_s, kb_s, vb_s, sa_s, zp_s, lf_s, cq_s, _, ck_s, _, _, _, ho_s) = _project(
        xs2, ada, ng, wit, b_f[0], wp, ps, hist_s,
        bm=bs * ln, sb=ln, segs=bs, ada_row=ADA_SAMPLE_ROW, start_pos=past, n_heads=n_heads, kv_head_major=True)
    lfc = jnp.swapaxes(cache_logf[0], 1, 2)
    ckt = jnp.transpose(cache_k[0], (0, 2, 3, 1)).reshape(bs, a_w, past)
    cvt = jnp.transpose(cache_v[0], (0, 2, 3, 1)).reshape(bs, a_w, past)
    y_s = _sample_attention(q_s, cq_s, ck_s, kb_s, vb_s, ckt, cvt, lfc, sa_s, zp_s, xs2, ada, wo, fg,
                            ln=ln, n_heads=n_heads)

    hd = (n_heads, HEAD_DIM)
    seq_minor = lambda t: jnp.transpose(t.reshape(hd + (bp, seq)), (2, 3, 0, 1))[None]
    return (y_p.reshape(bp, seq, d), y_s.reshape(bs, ln, d),
            seq_minor(k_p), seq_minor(v_p), jnp.swapaxes(lf_p, 1, 2)[None],
            ho_p[:, 16 - POOL_HIST:, :][None],
            k_s.reshape((1, bs, ln) + hd), v_s.reshape((1, bs, ln) + hd), jnp.swapaxes(lf_s, 1, 2)[None],
            ho_s[:, 16 - POOL_HIST:, :][None])
```

```python
import functools

import jax
import jax.numpy as jnp
from jax import lax
from jax.experimental import pallas as pl
from jax.experimental.pallas import tpu as pltpu

HEAD_DIM = 64
POOL_WINDOWS = (2, 4, 8, 16)
EPS = 1e-6

LANES = 128
PAIR = LANES // HEAD_DIM
ATT_BLK = 128
NEAR_BLOCKS = 3
LOG2E = 1.4426950408889634
EXP2_UNDERFLOW = 151.0
EXP2_SAFE_SPAN = 100.0
NORM_SLACK = 1.01
HIST_PAD = 32
POOL_HIST = max(POOL_WINDOWS) - 1
NEG = -1e30
VMEM_LIMIT = 60 * 1024 * 1024

F32 = jnp.float32
BF16 = jnp.bfloat16


def _silu(x):
    return x * jax.nn.sigmoid(x)


def _dot(a, b):
    return jnp.dot(a, b, preferred_element_type=F32)


def _dot_nt(a, b):
    return lax.dot_general(a, b, (((1,), (1,)), ((), ())), preferred_element_type=F32)


def _split3(x):
    hi = x.astype(BF16).astype(F32)
    r1 = x - hi
    mid = r1.astype(BF16).astype(F32)
    return hi, mid, r1 - mid


def _rows_to_lanes(x, n):
    rows = x.shape[0]
    if rows < LANES:
        x = jnp.concatenate([x, jnp.zeros((LANES - rows, LANES), x.dtype)], axis=0)
    return x.T[0:n, 0:rows]


def _resident(shape):
    return pl.BlockSpec(shape, lambda *_: (0,) * len(shape), pipeline_mode=pl.Buffered(1))


ADA_ROWS = 16
ADA_PROMPT_ROW, ADA_SAMPLE_ROW = 0, 8


def _ada_kernel(cp_ref, cs_ref, w_ref, b_ref, o_ref):
    ap = jnp.broadcast_to(_silu(cp_ref[...]), (ADA_SAMPLE_ROW, cp_ref.shape[1]))
    a = jnp.concatenate([ap, _silu(cs_ref[...])], axis=0).astype(BF16)
    o_ref[0] = _dot(a, w_ref[...].astype(BF16)) + b_ref[...]


def _ada_terms(c_prompt, c_sample, w_ada, b_ada):
    d = c_prompt.shape[1]
    assert c_prompt.shape[0] == 1 and c_sample.shape[0] == ADA_ROWS - ADA_SAMPLE_ROW and w_ada.shape[1] == 3 * d
    split = 2
    bn = d // split
    return pl.pallas_call(
        _ada_kernel,
        grid=(3 * split,),
        in_specs=[pl.BlockSpec(c_prompt.shape, lambda j: (0, 0)),
                  pl.BlockSpec(c_sample.shape, lambda j: (0, 0)),
                  pl.BlockSpec((d, bn), lambda j: (0, j)),
                  pl.BlockSpec((1, bn), lambda j: (0, j))],
        out_specs=pl.BlockSpec((1, ADA_ROWS, bn), lambda j: (j // split, 0, j % split)),
        out_shape=jax.ShapeDtypeStruct((3, ADA_ROWS, d), F32),
        compiler_params=pltpu.CompilerParams(dimension_semantics=("arbitrary",), vmem_limit_bytes=VMEM_LIMIT),
        name="ada",
    )(c_prompt, c_sample, w_ada, b_ada)


def _proj_kernel(bf_ref, x_ref, ada_ref, ng_ref, w_ref, wp_ref, ps_ref, h0_ref,
                 q_ref, k32_ref, v32_ref, kb_ref, vb_ref, sa_ref, zp_ref, lf_ref, cq_ref, nq_ref, ck_ref,
                 tot_ref, qmx_ref, kg_ref, ho_ref,
                 e_ref, t2_ref, t4_ref, t8_ref, kmx_ref, *, bm, sb, segs, ada_row, start_pos, n_heads, kv_head_major):
    a_w = n_heads * HEAD_DIM
    seg_rows = bm // segs
    step = pl.program_id(0)
    x = x_ref[...]
    ms = jnp.mean(x * x, axis=-1, keepdims=True)
    xn = x * lax.rsqrt(ms + EPS)
    h = jnp.concatenate(
        [xn[g * seg_rows:(g + 1) * seg_rows] * (ng_ref[...] * (1.0 + ada_ref[1, ada_row + g:ada_row + g + 1, :]))
         + ada_ref[0, ada_row + g:ada_row + g + 1, :] for g in range(segs)], axis=0).astype(BF16)

    pw = len(POOL_WINDOWS) * LANES
    sc = LOG2E / (HEAD_DIM ** 0.5)
    o_pool = 4 * a_w + n_heads

    def w_rows(lo, hi):
        return w_ref[lo:hi, :].astype(BF16)

    pu = _dot_nt(h, w_rows(o_pool, o_pool + pw))
    sel = (lax.broadcasted_iota(jnp.int32, (a_w, LANES), 0) // HEAD_DIM
           == lax.broadcasted_iota(jnp.int32, (a_w, LANES), 1)).astype(BF16)

    def store_kv(ref32, refb, p):
        refb[...] = p.astype(BF16)
        if kv_head_major:
            for hd in range(n_heads):
                ref32[:, hd, :] = p[:, hd * HEAD_DIM:(hd + 1) * HEAD_DIM]
        else:
            ref32[...] = p.T

    ext = HIST_PAD + seg_rows
    n = segs * ext

    def load_history():
        for g in range(segs):
            e_ref[g * ext:g * ext + HIST_PAD, :] = h0_ref[g]

    if segs > 1:
        load_history()
    else:
        pl.when(step == 0)(load_history)

    for g in range(segs):
        e_ref[g * ext + HIST_PAD:(g + 1) * ext, :] = pu[g * seg_rows:(g + 1) * seg_rows]
    t2_ref[8:n, :] = e_ref[8:n, :] + e_ref[7:n - 1, :]
    t4_ref[16:n, :] = t2_ref[16:n, LANES:] + t2_ref[14:n - 2, LANES:]
    t8_ref[24:n, :] = t4_ref[24:n, LANES:] + t4_ref[20:n - 4, LANES:]

    def seg_rows_of(ref, cols, back=0):
        return jnp.concatenate([ref[g * ext + HIST_PAD - back:(g + 1) * ext - back, cols] for g in range(segs)], axis=0)

    lane0, lane1 = slice(0, LANES), slice(LANES, 2 * LANES)
    sums = [seg_rows_of(t2_ref, lane0), seg_rows_of(t4_ref, lane0), seg_rows_of(t8_ref, lane0),
            seg_rows_of(t8_ref, lane1) + seg_rows_of(t8_ref, lane1, back=8)]
    row = lax.broadcasted_iota(jnp.int32, (bm, 1), 0)
    pos1 = start_pos + 1 + (step * bm + row if segs == 1 else row % seg_rows)
    pool_d = []
    for g, w in enumerate(POOL_WINDOWS):
        rc = 1.0 / jnp.minimum(pos1, w).astype(F32)
        pool_d.append((sums[g] * rc - pu[:, g * LANES:(g + 1) * LANES]).astype(BF16))
    for g in range(segs):
        ho_ref[g] = e_ref[(g + 1) * ext - 16:(g + 1) * ext, :]
    if segs == 1:
        e_ref[0:HIST_PAD, :] = e_ref[bm:n, :]

    pk = _dot_nt(h, w_rows(a_w, 2 * a_w))
    store_kv(k32_ref, kb_ref, pk)
    nk = jnp.sqrt(_dot((pk * pk).astype(BF16), sel)) * NORM_SLACK
    spg = _silu(_dot_nt(h, w_rows(o_pool + pw, o_pool + 2 * pw)))

    lane = lax.broadcasted_iota(jnp.int32, (1, LANES), 1)
    bias = jnp.zeros((1, LANES), F32)
    for hd in range(n_heads):
        bias = jnp.where(lane == hd, bf_ref[hd], bias)
    wf = jnp.concatenate([w_ref[4 * a_w:o_pool, :], jnp.zeros((LANES - n_heads, w_ref.shape[1]), F32)], axis=0)
    z = _dot_nt(h, wf.astype(BF16)) + bias
    lf = jnp.minimum(z, 0.0) - jnp.log1p(jnp.exp(-jnp.abs(z)))

    sa_ref[...] = _silu(_dot_nt(h, w_rows(3 * a_w, 4 * a_w))).astype(BF16)

    tri = (lax.broadcasted_iota(jnp.int32, (sb, sb), 1)
           <= lax.broadcasted_iota(jnp.int32, (sb, sb), 0)).astype(BF16)
    lf2 = lf * LOG2E
    tots, kmx = [], []
    for s in range(bm // sb):
        rows = slice(s * sb, (s + 1) * sb)
        parts = _dot(tri, jnp.concatenate(_split3(lf2[rows]), axis=1).astype(BF16))
        cb = parts[:, 0:LANES] + parts[:, LANES:2 * LANES] + parts[:, 2 * LANES:]
        cq_ref[rows, :] = cb[:, 0:n_heads]
        ck_ref[s] = _rows_to_lanes(cb, n_heads)
        off = (s * sb) % seg_rows
        lf_ref[(s * sb) // seg_rows, :, off:off + sb] = _rows_to_lanes(lf[rows], n_heads)
        tots.append(cb[sb - 1:sb, :])
        kmx.append(jnp.max(nk[rows], axis=0, keepdims=True))
    tot_ref[...] = jnp.concatenate(tots, axis=0)[:, 0:n_heads]
    kmax = kmx[0]
    for t in kmx[1:]:
        kmax = jnp.maximum(kmax, t)

    store_kv(v32_ref, vb_ref, _dot_nt(h, w_rows(2 * a_w, 3 * a_w)))

    zero_w = jnp.zeros((LANES, LANES), BF16)
    for g in range(0, len(POOL_WINDOWS), 2):
        cols = slice(g * LANES, (g + 2) * LANES)
        w2 = jnp.concatenate([jnp.concatenate([wp_ref[g].astype(BF16), zero_w], axis=1),
                              jnp.concatenate([zero_w, wp_ref[g + 1].astype(BF16)], axis=1)], axis=0)
        y = _dot(jnp.concatenate([pool_d[g], pool_d[g + 1]], axis=1), w2) * ps_ref[:, cols]
        zp_ref[:, cols] = (y * spg[:, cols]).astype(BF16)

    qs = _dot_nt(h, w_rows(0, a_w)) * sc
    q_ref[...] = qs.astype(BF16)
    nq = jnp.sqrt(_dot((qs * qs).astype(BF16), sel)) * NORM_SLACK
    nq_ref[...] = nq[:, 0:n_heads]
    qmx_ref[...] = jnp.concatenate([jnp.max(nq[s * sb:(s + 1) * sb], axis=0, keepdims=True)
                                    for s in range(bm // sb)], axis=0)[:, 0:n_heads]

    @pl.when(step > 0)
    def _():
        kmx_ref[...] = jnp.maximum(kmx_ref[...], kmax)

    @pl.when(step == 0)
    def _():
        kmx_ref[...] = kmax

    kg_ref[...] = kmx_ref[...]


def _project(x2, ada, norm_g, wit, b_f, wp, ps, hist0, *, bm, sb, segs, ada_row, start_pos, n_heads,
             kv_head_major):
    rows, d = x2.shape
    a_w = n_heads * HEAD_DIM
    pw = len(POOL_WINDOWS) * LANES
    n_steps = rows // bm
    assert segs == 1 or n_steps == 1
    n_streams = segs
    seg_rows = bm // segs
    nsb = bm // sb
    assert nsb == 8 and seg_rows % sb == 0
    row_blk = lambda w: pl.BlockSpec((bm, w), lambda i, *_: (i, 0))
    per_stream = lambda r, w: pl.BlockSpec((segs, r, w), lambda i, *_: (0, 0, 0))
    kern = functools.partial(_proj_kernel, bm=bm, sb=sb, segs=segs, ada_row=ada_row, start_pos=start_pos,
                             n_heads=n_heads, kv_head_major=kv_head_major)
    if kv_head_major:
        kv_shape = (rows, n_heads, HEAD_DIM)
        kv_blk = pl.BlockSpec((bm, n_heads, HEAD_DIM), lambda i, *_: (i, 0, 0))
    else:
        kv_shape = (a_w, rows)
        kv_blk = pl.BlockSpec((a_w, bm), lambda i, *_: (0, i))
    out_shape = (
        jax.ShapeDtypeStruct((rows, a_w), BF16),
        jax.ShapeDtypeStruct(kv_shape, F32),
        jax.ShapeDtypeStruct(kv_shape, F32),
        jax.ShapeDtypeStruct((rows, a_w), BF16),
        jax.ShapeDtypeStruct((rows, a_w), BF16),
        jax.ShapeDtypeStruct((rows, a_w), BF16),
        jax.ShapeDtypeStruct((rows, pw), BF16),
        jax.ShapeDtypeStruct((n_streams, n_heads, rows // n_streams), F32),
        jax.ShapeDtypeStruct((rows, n_heads), F32),
        jax.ShapeDtypeStruct((rows, n_heads), F32),
        jax.ShapeDtypeStruct((rows // sb, n_heads, sb), F32),
        jax.ShapeDtypeStruct((rows // sb, n_heads), F32),
        jax.ShapeDtypeStruct((rows // sb, n_heads), F32),
        jax.ShapeDtypeStruct((1, LANES), F32),
        jax.ShapeDtypeStruct((n_streams, 16, pw), F32),
    )
    out_specs = (
        row_blk(a_w), kv_blk, kv_blk, row_blk(a_w), row_blk(a_w), row_blk(a_w), row_blk(pw),
        pl.BlockSpec((segs, n_heads, seg_rows), lambda i, *_: (0, 0, i)),
        row_blk(n_heads), row_blk(n_heads),
        pl.BlockSpec((nsb, n_heads, sb), lambda i, *_: (i, 0, 0)),
        pl.BlockSpec((nsb, n_heads), lambda i, *_: (i, 0)),
        pl.BlockSpec((nsb, n_heads), lambda i, *_: (i, 0)),
        pl.BlockSpec((1, LANES), lambda i, *_: (0, 0)),
        per_stream(16, pw),
    )
    in_specs = [
        row_blk(d),
        _resident(ada.shape),
        _resident((1, d)),
        _resident(wit.shape), _resident(wp.shape), _resident(ps.shape),
        per_stream(HIST_PAD, pw),
    ]
    return pl.pallas_call(
        kern,
        grid_spec=pltpu.PrefetchScalarGridSpec(
            num_scalar_prefetch=1,
            grid=(n_steps,),
            in_specs=in_specs,
            out_specs=out_specs,
            scratch_shapes=[pltpu.VMEM((bm + segs * HIST_PAD, pw), F32),
                            pltpu.VMEM((bm + segs * HIST_PAD, pw), F32),
                            pltpu.VMEM((bm + segs * HIST_PAD, pw - LANES), F32),
                            pltpu.VMEM((bm + segs * HIST_PAD, pw - 2 * LANES), F32),
                            pltpu.VMEM((1, LANES), F32)]),
        out_shape=out_shape,
        compiler_params=pltpu.CompilerParams(dimension_semantics=("arbitrary",), vmem_limit_bytes=VMEM_LIMIT),
        name="proj",
    )(b_f, x2, ada, norm_g, wit, wp, ps, hist0)


def _merge_norm(x, gate, za, zp, wo_ref, fg, a_w):
    dy = _dot(za, wo_ref[0:a_w, :].astype(BF16)) + _dot(zp, wo_ref[a_w:, :].astype(BF16))
    out = x + gate * dy
    ms = jnp.mean(out * out, axis=-1, keepdims=True)
    return out * lax.rsqrt(ms + EPS) * fg


def _attn_kernel(tot_ref, qmx_ref, kg_ref,
                 q_ref, cq_ref, nq_ref, ck_ref, knew_ref, vnew_ref, sa_ref, zp_ref, x_ref, ada_ref, wo_ref, fg_ref,
                 y_ref,
                 k_ref, v_ref, z_ref, m_ref, l_ref, acc_ref, straight_ref, *, tm, n_heads):
    a_w = n_heads * HEAD_DIM
    n_pairs = n_heads // PAIR
    nsub = tm // ATT_BLK
    w_near = NEAR_BLOCKS
    cw = w_near * ATT_BLK
    step = pl.program_id(0)

    def keep_step_rows():
        k_ref[pl.ds(pl.multiple_of(step * tm, tm), tm), :] = knew_ref[...]
        v_ref[pl.ds(pl.multiple_of(step * tm, tm), tm), :] = vnew_ref[...]

    lo_q = lax.broadcasted_iota(jnp.int32, (ATT_BLK, LANES), 1) < HEAD_DIM
    lo_k = lax.broadcasted_iota(jnp.int32, (cw, LANES), 1) < HEAD_DIM
    col = lax.broadcasted_iota(jnp.int32, (ATT_BLK, cw), 1)
    tri = (lax.broadcasted_iota(jnp.int32, (ATT_BLK, ATT_BLK), 1)
           <= lax.broadcasted_iota(jnp.int32, (ATT_BLK, ATT_BLK), 0))
    tri_bias = jnp.where(tri, 0.0, NEG).astype(F32)
    zeros_k = jnp.zeros((cw, LANES), BF16)
    ind_lo = jnp.where(lo_k, 1.0, 0.0).astype(BF16)
    ind_hi = jnp.where(lo_k, 0.0, 1.0).astype(BF16)

    def tot_at(b, hd):
        return jnp.where(b >= 0, tot_ref[jnp.maximum(b, 0), hd], 0.0)

    def rows_of(ref, blocks, g):
        return jnp.concatenate(
            [ref[pl.ds(pl.multiple_of(b * ATT_BLK, ATT_BLK), ATT_BLK), g * LANES:(g + 1) * LANES] for b in blocks],
            axis=0)

    def step_rows_of(ref, new_ref, sub, g):
        parts = []
        for p in range(w_near):
            rel = sub - (w_near - 1) + p
            if rel >= 0:
                parts.append(new_ref[rel * ATT_BLK:(rel + 1) * ATT_BLK, g * LANES:(g + 1) * LANES])
            else:
                start = pl.multiple_of((step * nsub + rel) * ATT_BLK, ATT_BLK)
                parts.append(ref[pl.ds(start, ATT_BLK), g * LANES:(g + 1) * LANES])
        return jnp.concatenate(parts, axis=0)

    def pair_scores(r0, kc, g):
        keys = jnp.concatenate([jnp.where(lo_k, kc, zeros_k), jnp.where(lo_k, zeros_k, kc)], axis=0)
        return _dot_nt(q_ref[pl.ds(r0, ATT_BLK), g * LANES:(g + 1) * LANES], keys)

    def pair_values(p_pair, vc):
        vals = jnp.concatenate([jnp.concatenate([jnp.where(lo_k, vc, zeros_k), ind_lo], axis=1),
                                jnp.concatenate([jnp.where(lo_k, zeros_k, vc), ind_hi], axis=1)], axis=0)
        return _dot(p_pair, vals)

    def decay_row(blocks, offs, hd):
        return jnp.concatenate([offs[p] - ck_ref[blocks[p], hd:hd + 1, :] for p in range(w_near)], axis=1)

    def near_weights(s_pair, rows, g, near_c, near_offs, bounded):
        ps, ms = [], []
        for e in range(PAIR):
            hd = PAIR * g + e
            cqh = cq_ref[rows, hd:hd + 1]
            if bounded:
                m = nq_ref[rows, hd:hd + 1] * kg_ref[0, hd]
                cqh = cqh - m
            dec = decay_row(near_c, near_offs[hd], hd)
            pieces = []
            for p in range(w_near):
                lanes = slice(e * cw + p * ATT_BLK, e * cw + (p + 1) * ATT_BLK)
                sp = s_pair[:, lanes] + cqh + dec[:, p * ATT_BLK:(p + 1) * ATT_BLK]
                pieces.append(sp + tri_bias if p == w_near - 1 else sp)
            s = jnp.concatenate(pieces, axis=1)
            if not bounded:
                m = jnp.max(s, axis=1, keepdims=True)
                s = s - m
            ms.append(m)
            ps.append(jnp.exp2(s).astype(BF16))
        return jnp.concatenate(ps, axis=1), ms

    def gated_output(rows, g, acc, l):
        cols = slice(g * LANES, (g + 1) * LANES)
        z_ref[rows, cols] = ((acc / l) * sa_ref[rows, cols].astype(F32)).astype(BF16)

    def merge():
        y_ref[...] = _merge_norm(x_ref[...], ada_ref[2, ADA_PROMPT_ROW:ADA_PROMPT_ROW + 1, :], z_ref[...], zp_ref[...],
                                 wo_ref, fg_ref[...], a_w)

    def check_next_step():
        ok = jnp.bool_(True)
        last = qmx_ref.shape[0] - 1
        for sub in range(nsub):
            i = jnp.minimum((step + 1) * nsub + sub, last)
            for hd in range(n_heads):
                qk = 2.0 * qmx_ref[i, hd] * kg_ref[0, hd]
                back = qk
                for dd in range(1, w_near):
                    back = back + tot_ref[i - dd, hd]
                ok = jnp.logical_and(ok, jnp.logical_and(qk <= EXP2_SAFE_SPAN, back < -EXP2_UNDERFLOW))
        straight_ref[0] = ok.astype(jnp.int32)

    def straight_step():
        units = [(sub, g) for sub in range(nsub) for g in range(n_pairs)]
        near_of, offs_of = [], []
        for sub in range(nsub):
            i = step * nsub + sub
            near_of.append([i - (w_near - 1) + p for p in range(w_near)])
            offs = []
            for hd in range(n_heads):
                o, per_piece = jnp.float32(0.0), [jnp.float32(0.0)]
                for dd in range(1, w_near):
                    o = o + tot_ref[i - dd, hd]
                    per_piece.append(o)
                offs.append(per_piece[::-1])
            offs_of.append(offs)
        s_next = pair_scores(0, step_rows_of(k_ref, knew_ref, 0, 0), 0)
        for u, (sub, g) in enumerate(units):
            s_pair = s_next
            if u + 1 < len(units):
                sub1, g1 = units[u + 1]
                s_next = pair_scores(sub1 * ATT_BLK, step_rows_of(k_ref, knew_ref, sub1, g1), g1)
            rows = pl.ds(sub * ATT_BLK, ATT_BLK)
            p_pair, _ = near_weights(s_pair, rows, g, near_of[sub], offs_of[sub], True)
            r = pair_values(p_pair, step_rows_of(v_ref, vnew_ref, sub, g))
            gated_output(rows, g, r[:, 0:LANES], r[:, LANES:])
        check_next_step()
        keep_step_rows()
        merge()

    def sub_body(sub, carry):
        i = step * nsub + sub
        r0 = pl.multiple_of(sub * ATT_BLK, ATT_BLK)

        rows = pl.ds(r0, ATT_BLK)
        qk = [2.0 * qmx_ref[i, hd] * kg_ref[0, hd] for hd in range(n_heads)]

        near = [i - (w_near - 1) + p for p in range(w_near)]
        near_c = [jnp.maximum(b, 0) for b in near]
        near_offs, offs_far = [], []
        for hd in range(n_heads):
            back = [tot_at(i - dd, hd) for dd in range(1, w_near)]
            offs = []
            for p in range(w_near):
                o = jnp.float32(0.0)
                for dd in range(1, w_near - p):
                    o = o + back[dd - 1]
                offs.append(jnp.where(near[p] >= 0, o, NEG))
            near_offs.append(offs)
            o = jnp.float32(0.0)
            for t in back:
                o = o + t
            offs_far.append(o)

        def near_chunk(bounded):
            def fn():
                s_next = pair_scores(r0, rows_of(k_ref, near_c, 0), 0)
                for g in range(n_pairs):
                    s_pair = s_next
                    if g + 1 < n_pairs:
                        s_next = pair_scores(r0, rows_of(k_ref, near_c, g + 1), g + 1)
                    p_pair, ms = near_weights(s_pair, rows, g, near_c, near_offs, bounded)
                    for e in range(PAIR):
                        m_ref[PAIR * g + e] = ms[e]
                    r = pair_values(p_pair, rows_of(v_ref, near_c, g))
                    acc_ref[g] = r[:, 0:LANES]
                    l_ref[g] = r[:, LANES:]
                    gated_output(rows, g, r[:, 0:LANES], r[:, LANES:])
            return fn

        bounded_ok = qk[0] <= EXP2_SAFE_SPAN
        for hd in range(1, n_heads):
            bounded_ok = jnp.logical_and(bounded_ok, qk[hd] <= EXP2_SAFE_SPAN)
        pl.when(bounded_ok)(near_chunk(True))
        pl.when(jnp.logical_not(bounded_ok))(near_chunk(False))

        def far_cond(c):
            top = i - c[0] * w_near
            need = qk[0] + c[1] >= -EXP2_UNDERFLOW
            for hd in range(1, n_heads):
                need = jnp.logical_or(need, qk[hd] + c[1 + hd] >= -EXP2_UNDERFLOW)
            return jnp.logical_and(top >= 0, need)

        def far_body(c):
            top = i - c[0] * w_near
            jc = jnp.maximum(top - (w_near - 1), 0)
            blocks = [jc + p for p in range(w_near)]
            keepc = col < (top + 1 - jc) * ATT_BLK
            new = [c[0] + 1]
            for g in range(n_pairs):
                s_pair = pair_scores(r0, rows_of(k_ref, blocks, g), g)
                ps, alphas = [], []
                for e in range(PAIR):
                    hd = PAIR * g + e
                    tt = [tot_at(top - b, hd) for b in range(w_near)]
                    offs = []
                    for p in range(w_near):
                        behind = top - (jc + p)
                        o = c[1 + hd]
                        for b in range(w_near):
                            o = o + jnp.where(behind >= b, tt[b], 0.0)
                        offs.append(o)
                    s = s_pair[:, e * cw:(e + 1) * cw] + cq_ref[rows, hd:hd + 1] + decay_row(blocks, offs, hd)
                    s = jnp.where(keepc, s, NEG)
                    m_old = m_ref[hd]
                    m_new = jnp.maximum(m_old, jnp.max(s, axis=1, keepdims=True))
                    m_ref[hd] = m_new
                    alphas.append(jnp.broadcast_to(jnp.exp2(m_old - m_new), (ATT_BLK, LANES)))
                    ps.append(jnp.exp2(s - m_new).astype(BF16))
                    o = c[1 + hd]
                    for t in tt:
                        o = o + t
                    new.append(o)
                alpha = jnp.where(lo_q, alphas[0], alphas[1])
                r = pair_values(jnp.concatenate(ps, axis=1), rows_of(v_ref, blocks, g))
                acc_ref[g] = alpha * acc_ref[g] + r[:, 0:LANES]
                l_ref[g] = alpha * l_ref[g] + r[:, LANES:]
            return tuple(new)

        far = lax.while_loop(far_cond, far_body, (jnp.int32(1),) + tuple(offs_far))

        @pl.when(far[0] > 1)
        def _():
            for g in range(n_pairs):
                gated_output(rows, g, acc_ref[g], l_ref[g])
        return carry

    @pl.when(step == 0)
    def _():
        straight_ref[0] = 0

    straight = straight_ref[0] == 1
    pl.when(straight)(straight_step)

    @pl.when(jnp.logical_not(straight))
    def _():
        keep_step_rows()
        check_next_step()
        lax.fori_loop(0, nsub, sub_body, 0)
        merge()


def _prompt_attention(tot, qmx, kg, q, cq, nq, ck, kb, vb, sa, zp, x2, ada, wo, fg, *, tm, n_heads):
    rows, d = x2.shape
    a_w = n_heads * HEAD_DIM
    n_pairs = n_heads // PAIR
    assert rows % tm == 0 and tm % ATT_BLK == 0 and tm // ATT_BLK >= NEAR_BLOCKS - 1 and n_heads % PAIR == 0
    row_blk = lambda w: pl.BlockSpec((tm, w), lambda i, *_: (i, 0))
    grid_spec = pltpu.PrefetchScalarGridSpec(
        num_scalar_prefetch=3,
        grid=(rows // tm,),
        in_specs=[row_blk(a_w), row_blk(n_heads), row_blk(n_heads), _resident(ck.shape), row_blk(a_w), row_blk(a_w),
                  row_blk(a_w), row_blk(zp.shape[1]), row_blk(d), _resident(ada.shape), _resident(wo.shape),
                  _resident((1, d))],
        out_specs=row_blk(d),
        scratch_shapes=[pltpu.VMEM(kb.shape, BF16),
                        pltpu.VMEM(vb.shape, BF16),
                        pltpu.VMEM((tm, a_w), BF16),
                        pltpu.VMEM((n_heads, ATT_BLK, 1), F32),
                        pltpu.VMEM((n_pairs, ATT_BLK, LANES), F32),
                        pltpu.VMEM((n_pairs, ATT_BLK, LANES), F32),
                        pltpu.SMEM((1,), jnp.int32)],
    )
    return pl.pallas_call(
        functools.partial(_attn_kernel, tm=tm, n_heads=n_heads),
        grid_spec=grid_spec,
        out_shape=jax.ShapeDtypeStruct((rows, d), F32),
        compiler_params=pltpu.CompilerParams(dimension_semantics=("arbitrary",), vmem_limit_bytes=VMEM_LIMIT),
        name="attn",
    )(tot, qmx, kg, q, cq, nq, ck, kb, vb, sa, zp, x2, ada, wo, fg)


def _sattn_kernel(q_ref, cq_ref, ckn_ref, kn_ref, vn_ref, kc_ref, vc_ref, lfc_ref, sa_ref, zp_ref, x_ref,
                  ada_ref, wo_ref, fg_ref, y_ref, *, n_heads):
    a_w = n_heads * HEAD_DIM
    ln = q_ref.shape[0]
    past = kc_ref.shape[2]
    nb = past // LANES

    lfc = lfc_ref[0] * LOG2E
    triu = (lax.broadcasted_iota(jnp.int32, (LANES, LANES), 0)
            <= lax.broadcasted_iota(jnp.int32, (LANES, LANES), 1)).astype(BF16)
    zeros = jnp.zeros((8, LANES), F32)
    parts = []
    for b in range(nb):
        parts.extend(_split3(lfc[:, b * LANES:(b + 1) * LANES]))
        parts.append(zeros)
    cs = _dot(jnp.concatenate(parts, axis=0).astype(BF16), triu)
    after = jnp.zeros((n_heads, 1), F32)
    suffix = [None] * nb
    for b in reversed(range(nb)):
        cb = cs[32 * b:32 * b + 8] + cs[32 * b + 8:32 * b + 16] + cs[32 * b + 16:32 * b + 24]
        tot = cb[:, LANES - 1:LANES]
        suffix[b] = (tot - cb) + after
        after = after + tot
    dec_c = jnp.concatenate(suffix, axis=1)

    lane = lax.broadcasted_iota(jnp.int32, (ln, LANES), 1)
    half = [lane < HEAD_DIM, lane >= HEAD_DIM]
    causal = lax.broadcasted_iota(jnp.int32, (ln, ln), 1) <= lax.broadcasted_iota(jnp.int32, (ln, ln), 0)
    n_pairs = n_heads // PAIR

    def pair_scores(g):
        cols = slice(g * LANES, (g + 1) * LANES)
        q2 = q_ref[:, cols]
        qst = jnp.concatenate([jnp.where(half[e], q2, jnp.zeros_like(q2)) for e in range(PAIR)], axis=0)
        return _dot(qst, kc_ref[0, cols, :].astype(BF16)), _dot_nt(qst, kn_ref[:, cols])

    outs = []
    s_next = pair_scores(0)
    for g in range(n_pairs):
        cols = slice(g * LANES, (g + 1) * LANES)
        sc_st, sn_st = s_next
        if g + 1 < n_pairs:
            s_next = pair_scores(g + 1)
        vct = vc_ref[0, cols, :].astype(BF16)
        vn = vn_ref[:, cols]
        pc, pn, ls = [], [], []
        for e in range(PAIR):
            hd = PAIR * g + e
            rows = slice(e * ln, (e + 1) * ln)
            cqh = cq_ref[:, hd:hd + 1]
            s_c = sc_st[rows] + cqh + dec_c[hd:hd + 1, :]
            s_n = jnp.where(causal, sn_st[rows] + cqh - ckn_ref[0, hd:hd + 1, :], NEG)
            m = jnp.maximum(jnp.max(s_c, axis=1, keepdims=True), jnp.max(s_n, axis=1, keepdims=True))
            p_c = jnp.exp2(s_c - m)
            p_n = jnp.exp2(s_n - m)
            ls.append(jnp.sum(p_c, axis=1, keepdims=True) + jnp.sum(p_n, axis=1, keepdims=True))
            pc.append(p_c.astype(BF16))
            pn.append(p_n.astype(BF16))
        acc = _dot_nt(jnp.concatenate(pc, axis=0), vct) + _dot(jnp.concatenate(pn, axis=0), vn)
        o = jnp.where(half[0], acc[0:ln] / ls[0], acc[ln:2 * ln] / ls[1])
        outs.append((o * sa_ref[:, cols].astype(F32)).astype(BF16))
    za = jnp.concatenate(outs, axis=1)
    y_ref[...] = _merge_norm(x_ref[...], ada_ref[2, pl.ds(ADA_SAMPLE_ROW + pl.program_id(0), 1), :], za, zp_ref[...], wo_ref, fg_ref[...], a_w)


def _sample_attention(q, cq, ckn, kb, vb, cache_k, cache_v, lfc, sa, zp, x2, ada, wo, fg, *, ln, n_heads):
    rows, d = x2.shape
    nbatch = rows // ln
    a_w = n_heads * HEAD_DIM
    past = cache_k.shape[2]
    row_blk = lambda w: pl.BlockSpec((ln, w), lambda b: (b, 0))
    per_b = lambda s: pl.BlockSpec((1,) + s, lambda b: (b, 0, 0))
    return pl.pallas_call(
        functools.partial(_sattn_kernel, n_heads=n_heads),
        grid=(nbatch,),
        in_specs=[row_blk(a_w), row_blk(n_heads), per_b((n_heads, ln)), row_blk(a_w), row_blk(a_w),
                  per_b((a_w, past)), per_b((a_w, past)), per_b((n_heads, past)),
                  row_blk(a_w), row_blk(zp.shape[1]), row_blk(d), _resident(ada.shape), _resident(wo.shape),
                  _resident((1, d))],
        out_specs=row_blk(d),
        out_shape=jax.ShapeDtypeStruct((rows, d), F32),
        compiler_params=pltpu.CompilerParams(dimension_semantics=("arbitrary",), vmem_limit_bytes=VMEM_LIMIT),
        name="sattn",
    )(q, cq, ckn, kb, vb, cache_k, cache_v, lfc, sa, zp, x2, ada, wo, fg)


def kernel(x_prompt, x_sample, c_prompt, c_sample, cache_k, cache_v, cache_logf, state_pool, norm_g, w_ada, b_ada,
           w_in, b_f, w_pool, pool_scale, w_out, final_g):
    depth = norm_g.shape[0]
    assert depth == 1
    bp, seq, d = x_prompt.shape
    bs, ln, _ = x_sample.shape
    assert bp == 1
    n_heads = cache_k.shape[3]
    past = cache_k.shape[2]
    a_w = n_heads * HEAD_DIM
    pw = state_pool.shape[3]
    assert pw == len(POOL_WINDOWS) * LANES and cache_k.shape[4] == HEAD_DIM and n_heads <= 8

    ada = _ada_terms(c_prompt, c_sample, w_ada[0], b_ada)

    wit = w_in[0].T
    wp = w_pool[0]
    ps = pool_scale[0][None, :]
    wo = w_out[0]
    ng = norm_g[0][None, :]
    fg = final_g[None, :]

    bm = 1024
    xp2 = x_prompt.reshape(seq, d)
    hist_p = jnp.zeros((1, HIST_PAD, pw), F32)
    (q_p, k_p, v_p, kb_p, vb_p, sa_p, zp_p, lf_p, cq_p, nq_p, ck_p, tot, qmx, kg, ho_p) = _project(
        xp2, ada, ng, wit, b_f[0], wp, ps, hist_p,
        bm=bm, sb=ATT_BLK, segs=1, ada_row=ADA_PROMPT_ROW, start_pos=0, n_heads=n_heads, kv_head_major=False)
    y_p = _prompt_attention(tot, qmx, kg, q_p, cq_p, nq_p, ck_p, kb_p, vb_p, sa_p, zp_p, xp2, ada, wo, fg,
                            tm=512, n_heads=n_heads)

    xs2 = x_sample.reshape(bs * ln, d)
    hist_s = jnp.pad(state_pool[0], ((0, 0), (HIST_PAD - POOL_HIST, 0), (0, 0)))
    (q_s, k_s, v_s, kb_s, vb_s, sa_s, zp_s, lf_s, cq_s, _, ck_s, _, _, _, ho_s) = _project(
        xs2, ada, ng, wit, b_f[0], wp, ps, hist_s,
        bm=bs * ln, sb=ln, segs=bs, ada_row=ADA_SAMPLE_ROW, start_pos=past, n_heads=n_heads, kv_head_major=True)
    lfc = jnp.swapaxes(cache_logf[0], 1, 2)
    ckt = jnp.transpose(cache_k[0], (0, 2, 3, 1)).reshape(bs, a_w, past)
    cvt = jnp.transpose(cache_v[0], (0, 2, 3, 1)).reshape(bs, a_w, past)
    y_s = _sample_attention(q_s, cq_s, ck_s, kb_s, vb_s, ckt, cvt, lfc, sa_s, zp_s, xs2, ada, wo, fg,
                            ln=ln, n_heads=n_heads)

    hd = (n_heads, HEAD_DIM)
    seq_minor = lambda t: jnp.transpose(t.reshape(hd + (bp, seq)), (2, 3, 0, 1))[None]
    return (y_p.reshape(bp, seq, d), y_s.reshape(bs, ln, d),
            seq_minor(k_p), seq_minor(v_p), jnp.swapaxes(lf_p, 1, 2)[None],
            ho_p[:, 16 - POOL_HIST:, :][None],
            k_s.reshape((1, bs, ln) + hd), v_s.reshape((1, bs, ln) + hd), jnp.swapaxes(lf_s, 1, 2)[None],
            ho_s[:, 16 - POOL_HIST:, :][None])
```

```python
import functools

import jax
import jax.numpy as jnp
from jax import lax
from jax.experimental import pallas as pl
from jax.experimental.pallas import tpu as pltpu

HEAD_DIM = 64
POOL_WINDOWS = (2, 4, 8, 16)
EPS = 1e-6

LANES = 128
PAIR = LANES // HEAD_DIM
ATT_BLK = 128
NEAR_BLOCKS = 3
LOG2E = 1.4426950408889634
EXP2_UNDERFLOW = 151.0
EXP2_SAFE_SPAN = 100.0
NORM_SLACK = 1.01
HIST_PAD = 32
POOL_HIST = max(POOL_WINDOWS) - 1
NEG = -1e30
VMEM_LIMIT = 60 * 1024 * 1024

F32 = jnp.float32
BF16 = jnp.bfloat16


def _silu(x):
    return x * jax.nn.sigmoid(x)


def _dot(a, b):
    return jnp.dot(a, b, preferred_element_type=F32)


def _dot_nt(a, b):
    return lax.dot_general(a, b, (((1,), (1,)), ((), ())), preferred_element_type=F32)


def _split3(x):
    hi = x.astype(BF16).astype(F32)
    r1 = x - hi
    mid = r1.astype(BF16).astype(F32)
    return hi, mid, r1 - mid


def _rows_to_lanes(x, n):
    rows = x.shape[0]
    if rows < LANES:
        x = jnp.concatenate([x, jnp.zeros((LANES - rows, LANES), x.dtype)], axis=0)
    return x.T[0:n, 0:rows]


def _resident(shape):
    return pl.BlockSpec(shape, lambda *_: (0,) * len(shape), pipeline_mode=pl.Buffered(1))


ADA_ROWS = 16
ADA_PROMPT_ROW, ADA_SAMPLE_ROW = 0, 8


def _ada_kernel(cp_ref, cs_ref, w_ref, b_ref, o_ref):
    ap = jnp.broadcast_to(_silu(cp_ref[...]), (ADA_SAMPLE_ROW, cp_ref.shape[1]))
    a = jnp.concatenate([ap, _silu(cs_ref[...])], axis=0).astype(BF16)
    o_ref[0] = _dot(a, w_ref[...].astype(BF16)) + b_ref[...]


def _ada_terms(c_prompt, c_sample, w_ada, b_ada):
    d = c_prompt.shape[1]
    assert c_prompt.shape[0] == 1 and c_sample.shape[0] == ADA_ROWS - ADA_SAMPLE_ROW and w_ada.shape[1] == 3 * d
    split = 2
    bn = d // split
    return pl.pallas_call(
        _ada_kernel,
        grid=(3 * split,),
        in_specs=[pl.BlockSpec(c_prompt.shape, lambda j: (0, 0)),
                  pl.BlockSpec(c_sample.shape, lambda j: (0, 0)),
                  pl.BlockSpec((d, bn), lambda j: (0, j)),
                  pl.BlockSpec((1, bn), lambda j: (0, j))],
        out_specs=pl.BlockSpec((1, ADA_ROWS, bn), lambda j: (j // split, 0, j % split)),
        out_shape=jax.ShapeDtypeStruct((3, ADA_ROWS, d), F32),
        compiler_params=pltpu.CompilerParams(dimension_semantics=("arbitrary",), vmem_limit_bytes=VMEM_LIMIT),
        name="ada",
    )(c_prompt, c_sample, w_ada, b_ada)


def _proj_kernel(bf_ref, x_ref, ada_ref, ng_ref, w_ref, wp_ref, ps_ref, h0_ref,
                 q_ref, k32_ref, v32_ref, kb_ref, vb_ref, sa_ref, zp_ref, lf_ref, cq_ref, nq_ref, ck_ref,
                 tot_ref, qmx_ref, kg_ref, ho_ref,
                 e_ref, t2_ref, t4_ref, t8_ref, kmx_ref, *, bm, sb, segs, ada_row, start_pos, n_heads, kv_head_major):
    a_w = n_heads * HEAD_DIM
    seg_rows = bm // segs
    step = pl.program_id(0)
    x = x_ref[...]
    ms = jnp.mean(x * x, axis=-1, keepdims=True)
    xn = x * lax.rsqrt(ms + EPS)
    h = jnp.concatenate(
        [xn[g * seg_rows:(g + 1) * seg_rows] * (ng_ref[...] * (1.0 + ada_ref[1, ada_row + g:ada_row + g + 1, :]))
         + ada_ref[0, ada_row + g:ada_row + g + 1, :] for g in range(segs)], axis=0).astype(BF16)

    pw = len(POOL_WINDOWS) * LANES
    sc = LOG2E / (HEAD_DIM ** 0.5)
    o_pool = 4 * a_w + n_heads

    def w_rows(lo, hi):
        return w_ref[lo:hi, :].astype(BF16)

    pu = _dot_nt(h, w_rows(o_pool, o_pool + pw))
    sel = (lax.broadcasted_iota(jnp.int32, (a_w, LANES), 0) // HEAD_DIM
           == lax.broadcasted_iota(jnp.int32, (a_w, LANES), 1)).astype(BF16)

    def store_kv(ref32, refb, p):
        refb[...] = p.astype(BF16)
        if kv_head_major:
            for hd in range(n_heads):
                ref32[:, hd, :] = p[:, hd * HEAD_DIM:(hd + 1) * HEAD_DIM]
        else:
            ref32[...] = p.T

    ext = HIST_PAD + seg_rows
    n = segs * ext

    def load_history():
        for g in range(segs):
            e_ref[g * ext:g * ext + HIST_PAD, :] = h0_ref[g]

    if segs > 1:
        load_history()
    else:
        pl.when(step == 0)(load_history)

    for g in range(segs):
        e_ref[g * ext + HIST_PAD:(g + 1) * ext, :] = pu[g * seg_rows:(g + 1) * seg_rows]
    t2_ref[8:n, :] = e_ref[8:n, :] + e_ref[7:n - 1, :]
    t4_ref[16:n, :] = t2_ref[16:n, LANES:] + t2_ref[14:n - 2, LANES:]
    t8_ref[24:n, :] = t4_ref[24:n, LANES:] + t4_ref[20:n - 4, LANES:]

    def seg_rows_of(ref, cols, back=0):
        return jnp.concatenate([ref[g * ext + HIST_PAD - back:(g + 1) * ext - back, cols] for g in range(segs)], axis=0)

    lane0, lane1 = slice(0, LANES), slice(LANES, 2 * LANES)
    sums = [seg_rows_of(t2_ref, lane0), seg_rows_of(t4_ref, lane0), seg_rows_of(t8_ref, lane0),
            seg_rows_of(t8_ref, lane1) + seg_rows_of(t8_ref, lane1, back=8)]
    row = lax.broadcasted_iota(jnp.int32, (bm, 1), 0)
    pos1 = start_pos + 1 + (step * bm + row if segs == 1 else row % seg_rows)
    pool_d = []
    for g, w in enumerate(POOL_WINDOWS):
        rc = 1.0 / jnp.minimum(pos1, w).astype(F32)
        pool_d.append((sums[g] * rc - pu[:, g * LANES:(g + 1) * LANES]).astype(BF16))
    for g in range(segs):
        ho_ref[g] = e_ref[(g + 1) * ext - 16:(g + 1) * ext, :]
    if segs == 1:
        e_ref[0:HIST_PAD, :] = e_ref[bm:n, :]

    pk = _dot_nt(h, w_rows(a_w, 2 * a_w))
    store_kv(k32_ref, kb_ref, pk)
    nk = jnp.sqrt(_dot((pk * pk).astype(BF16), sel)) * NORM_SLACK
    spg = _silu(_dot_nt(h, w_rows(o_pool + pw, o_pool + 2 * pw)))

    lane = lax.broadcasted_iota(jnp.int32, (1, LANES), 1)
    bias = jnp.zeros((1, LANES), F32)
    for hd in range(n_heads):
        bias = jnp.where(lane == hd, bf_ref[hd], bias)
    wf = jnp.concatenate([w_ref[4 * a_w:o_pool, :], jnp.zeros((LANES - n_heads, w_ref.shape[1]), F32)], axis=0)
    z = _dot_nt(h, wf.astype(BF16)) + bias
    lf = jnp.minimum(z, 0.0) - jnp.log1p(jnp.exp(-jnp.abs(z)))

    sa_ref[...] = _silu(_dot_nt(h, w_rows(3 * a_w, 4 * a_w))).astype(BF16)

    tri = (lax.broadcasted_iota(jnp.int32, (sb, sb), 1)
           <= lax.broadcasted_iota(jnp.int32, (sb, sb), 0)).astype(BF16)
    lf2 = lf * LOG2E
    tots, kmx = [], []
    for s in range(bm // sb):
        rows = slice(s * sb, (s + 1) * sb)
        parts = _dot(tri, jnp.concatenate(_split3(lf2[rows]), axis=1).astype(BF16))
        cb = parts[:, 0:LANES] + parts[:, LANES:2 * LANES] + parts[:, 2 * LANES:]
        cq_ref[rows, :] = cb[:, 0:n_heads]
        ck_ref[s] = _rows_to_lanes(cb, n_heads)
        off = (s * sb) % seg_rows
        lf_ref[(s * sb) // seg_rows, :, off:off + sb] = _rows_to_lanes(lf[rows], n_heads)
        tots.append(cb[sb - 1:sb, :])
        kmx.append(jnp.max(nk[rows], axis=0, keepdims=True))
    tot_ref[...] = jnp.concatenate(tots, axis=0)[:, 0:n_heads]
    kmax = kmx[0]
    for t in kmx[1:]:
        kmax = jnp.maximum(kmax, t)

    store_kv(v32_ref, vb_ref, _dot_nt(h, w_rows(2 * a_w, 3 * a_w)))

    zero_w = jnp.zeros((LANES, LANES), BF16)
    for g in range(0, len(POOL_WINDOWS), 2):
        cols = slice(g * LANES, (g + 2) * LANES)
        w2 = jnp.concatenate([jnp.concatenate([wp_ref[g].astype(BF16), zero_w], axis=1),
                              jnp.concatenate([zero_w, wp_ref[g + 1].astype(BF16)], axis=1)], axis=0)
        y = _dot(jnp.concatenate([pool_d[g], pool_d[g + 1]], axis=1), w2) * ps_ref[:, cols]
        zp_ref[:, cols] = (y * spg[:, cols]).astype(BF16)

    qs = _dot_nt(h, w_rows(0, a_w)) * sc
    q_ref[...] = qs.astype(BF16)
    nq = jnp.sqrt(_dot((qs * qs).astype(BF16), sel)) * NORM_SLACK
    nq_ref[...] = nq[:, 0:n_heads]
    qmx_ref[...] = jnp.concatenate([jnp.max(nq[s * sb:(s + 1) * sb], axis=0, keepdims=True)
                                    for s in range(bm // sb)], axis=0)[:, 0:n_heads]

    @pl.when(step > 0)
    def _():
        kmx_ref[...] = jnp.maximum(kmx_ref[...], kmax)

    @pl.when(step == 0)
    def _():
        kmx_ref[...] = kmax

    kg_ref[...] = kmx_ref[...]


def _project(x2, ada, norm_g, wit, b_f, wp, ps, hist0, *, bm, sb, segs, ada_row, start_pos, n_heads,
             kv_head_major):
    rows, d = x2.shape
    a_w = n_heads * HEAD_DIM
    pw = len(POOL_WINDOWS) * LANES
    n_steps = rows // bm
    assert segs == 1 or n_steps == 1
    n_streams = segs
    seg_rows = bm // segs
    nsb = bm // sb
    assert nsb == 8 and seg_rows % sb == 0
    row_blk = lambda w: pl.BlockSpec((bm, w), lambda i, *_: (i, 0))
    per_stream = lambda r, w: pl.BlockSpec((segs, r, w), lambda i, *_: (0, 0, 0))
    kern = functools.partial(_proj_kernel, bm=bm, sb=sb, segs=segs, ada_row=ada_row, start_pos=start_pos,
                             n_heads=n_heads, kv_head_major=kv_head_major)
    if kv_head_major:
        kv_shape = (rows, n_heads, HEAD_DIM)
        kv_blk = pl.BlockSpec((bm, n_heads, HEAD_DIM), lambda i, *_: (i, 0, 0))
    else:
        kv_shape = (a_w, rows)
        kv_blk = pl.BlockSpec((a_w, bm), lambda i, *_: (0, i))
    out_shape = (
        jax.ShapeDtypeStruct((rows, a_w), BF16),
        jax.ShapeDtypeStruct(kv_shape, F32),
        jax.ShapeDtypeStruct(kv_shape, F32),
        jax.ShapeDtypeStruct((rows, a_w), BF16),
        jax.ShapeDtypeStruct((rows, a_w), BF16),
        jax.ShapeDtypeStruct((rows, a_w), BF16),
        jax.ShapeDtypeStruct((rows, pw), BF16),
        jax.ShapeDtypeStruct((n_streams, n_heads, rows // n_streams), F32),
        jax.ShapeDtypeStruct((rows, n_heads), F32),
        jax.ShapeDtypeStruct((rows, n_heads), F32),
        jax.ShapeDtypeStruct((rows // sb, n_heads, sb), F32),
        jax.ShapeDtypeStruct((rows // sb, n_heads), F32),
        jax.ShapeDtypeStruct((rows // sb, n_heads), F32),
        jax.ShapeDtypeStruct((1, LANES), F32),
        jax.ShapeDtypeStruct((n_streams, 16, pw), F32),
    )
    out_specs = (
        row_blk(a_w), kv_blk, kv_blk, row_blk(a_w), row_blk(a_w), row_blk(a_w), row_blk(pw),
        pl.BlockSpec((segs, n_heads, seg_rows), lambda i, *_: (0, 0, i)),
        row_blk(n_heads), row_blk(n_heads),
        pl.BlockSpec((nsb, n_heads, sb), lambda i, *_: (i, 0, 0)),
        pl.BlockSpec((nsb, n_heads), lambda i, *_: (i, 0)),
        pl.BlockSpec((nsb, n_heads), lambda i, *_: (i, 0)),
        pl.BlockSpec((1, LANES), lambda i, *_: (0, 0)),
        per_stream(16, pw),
    )
    in_specs = [
        row_blk(d),
        _resident(ada.shape),
        _resident((1, d)),
        _resident(wit.shape), _resident(wp.shape), _resident(ps.shape),
        per_stream(HIST_PAD, pw),
    ]
    return pl.pallas_call(
        kern,
        grid_spec=pltpu.PrefetchScalarGridSpec(
            num_scalar_prefetch=1,
            grid=(n_steps,),
            in_specs=in_specs,
            out_specs=out_specs,
            scratch_shapes=[pltpu.VMEM((bm + segs * HIST_PAD, pw), F32),
                            pltpu.VMEM((bm + segs * HIST_PAD, pw), F32),
                            pltpu.VMEM((bm + segs * HIST_PAD, pw - LANES), F32),
                            pltpu.VMEM((bm + segs * HIST_PAD, pw - 2 * LANES), F32),
                            pltpu.VMEM((1, LANES), F32)]),
        out_shape=out_shape,
        compiler_params=pltpu.CompilerParams(dimension_semantics=("arbitrary",), vmem_limit_bytes=VMEM_LIMIT),
        name="proj",
    )(b_f, x2, ada, norm_g, wit, wp, ps, hist0)


def _merge_norm(x, gate, za, zp, wo_ref, fg, a_w):
    dy = _dot(za, wo_ref[0:a_w, :].astype(BF16)) + _dot(zp, wo_ref[a_w:, :].astype(BF16))
    out = x + gate * dy
    ms = jnp.mean(out * out, axis=-1, keepdims=True)
    return out * lax.rsqrt(ms + EPS) * fg


def _attn_kernel(tot_ref, qmx_ref, kg_ref,
                 q_ref, cq_ref, nq_ref, ck_ref, knew_ref, vnew_ref, sa_ref, zp_ref, x_ref, ada_ref, wo_ref, fg_ref,
                 y_ref,
                 k_ref, v_ref, z_ref, m_ref, l_ref, acc_ref, straight_ref, *, tm, n_heads):
    a_w = n_heads * HEAD_DIM
    n_pairs = n_heads // PAIR
    nsub = tm // ATT_BLK
    w_near = NEAR_BLOCKS
    cw = w_near * ATT_BLK
    step = pl.program_id(0)

    def keep_step_rows():
        k_ref[pl.ds(pl.multiple_of(step * tm, tm), tm), :] = knew_ref[...]
        v_ref[pl.ds(pl.multiple_of(step * tm, tm), tm), :] = vnew_ref[...]

    lo_q = lax.broadcasted_iota(jnp.int32, (ATT_BLK, LANES), 1) < HEAD_DIM
    lo_k = lax.broadcasted_iota(jnp.int32, (cw, LANES), 1) < HEAD_DIM
    col = lax.broadcasted_iota(jnp.int32, (ATT_BLK, cw), 1)
    tri = (lax.broadcasted_iota(jnp.int32, (ATT_BLK, ATT_BLK), 1)
           <= lax.broadcasted_iota(jnp.int32, (ATT_BLK, ATT_BLK), 0))
    tri_bias = jnp.where(tri, 0.0, NEG).astype(F32)
    zeros_k = jnp.zeros((cw, LANES), BF16)
    ind_lo = jnp.where(lo_k, 1.0, 0.0).astype(BF16)
    ind_hi = jnp.where(lo_k, 0.0, 1.0).astype(BF16)

    def tot_at(b, hd):
        return jnp.where(b >= 0, tot_ref[jnp.maximum(b, 0), hd], 0.0)

    def rows_of(ref, blocks, g):
        return jnp.concatenate(
            [ref[pl.ds(pl.multiple_of(b * ATT_BLK, ATT_BLK), ATT_BLK), g * LANES:(g + 1) * LANES] for b in blocks],
            axis=0)

    def step_rows_of(ref, new_ref, sub, g):
        parts = []
        for p in range(w_near):
            rel = sub - (w_near - 1) + p
            if rel >= 0:
                parts.append(new_ref[rel * ATT_BLK:(rel + 1) * ATT_BLK, g * LANES:(g + 1) * LANES])
            else:
                start = pl.multiple_of((step * nsub + rel) * ATT_BLK, ATT_BLK)
                parts.append(ref[pl.ds(start, ATT_BLK), g * LANES:(g + 1) * LANES])
        return jnp.concatenate(parts, axis=0)

    def pair_scores(r0, kc, g):
        keys = jnp.concatenate([jnp.where(lo_k, kc, zeros_k), jnp.where(lo_k, zeros_k, kc)], axis=0)
        return _dot_nt(q_ref[pl.ds(r0, ATT_BLK), g * LANES:(g + 1) * LANES], keys)

    def pair_values(p_pair, vc):
        vals = jnp.concatenate([jnp.concatenate([jnp.where(lo_k, vc, zeros_k), ind_lo], axis=1),
                                jnp.concatenate([jnp.where(lo_k, zeros_k, vc), ind_hi], axis=1)], axis=0)
        return _dot(p_pair, vals)

    def decay_row(blocks, offs, hd):
        return jnp.concatenate([offs[p] - ck_ref[blocks[p], hd:hd + 1, :] for p in range(w_near)], axis=1)

    def near_weights(s_pair, rows, g, near_c, near_offs, bounded):
        ps, ms = [], []
        for e in range(PAIR):
            hd = PAIR * g + e
            cqh = cq_ref[rows, hd:hd + 1]
            if bounded:
                m = nq_ref[rows, hd:hd + 1] * kg_ref[0, hd]
                cqh = cqh - m
            dec = decay_row(near_c, near_offs[hd], hd)
            pieces = []
            for p in range(w_near):
                lanes = slice(e * cw + p * ATT_BLK, e * cw + (p + 1) * ATT_BLK)
                sp = s_pair[:, lanes] + cqh + dec[:, p * ATT_BLK:(p + 1) * ATT_BLK]
                pieces.append(sp + tri_bias if p == w_near - 1 else sp)
            s = jnp.concatenate(pieces, axis=1)
            if not bounded:
                m = jnp.max(s, axis=1, keepdims=True)
                s = s - m
            ms.append(m)
            ps.append(jnp.exp2(s).astype(BF16))
        return jnp.concatenate(ps, axis=1), ms

    def gated_output(rows, g, acc, l):
        cols = slice(g * LANES, (g + 1) * LANES)
        z_ref[rows, cols] = ((acc / l) * sa_ref[rows, cols].astype(F32)).astype(BF16)

    def merge():
        y_ref[...] = _merge_norm(x_ref[...], ada_ref[2, ADA_PROMPT_ROW:ADA_PROMPT_ROW + 1, :], z_ref[...], zp_ref[...],
                                 wo_ref, fg_ref[...], a_w)

    def check_next_step():
        ok = jnp.bool_(True)
        last = qmx_ref.shape[0] - 1
        for sub in range(nsub):
            i = jnp.minimum((step + 1) * nsub + sub, last)
            for hd in range(n_heads):
                qk = 2.0 * qmx_ref[i, hd] * kg_ref[0, hd]
                back = qk
                for dd in range(1, w_near):
                    back = back + tot_ref[i - dd, hd]
                ok = jnp.logical_and(ok, jnp.logical_and(qk <= EXP2_SAFE_SPAN, back < -EXP2_UNDERFLOW))
        straight_ref[0] = ok.astype(jnp.int32)

    def straight_step():
        units = [(sub, g) for sub in range(nsub) for g in range(n_pairs)]
        near_of, offs_of = [], []
        for sub in range(nsub):
            i = step * nsub + sub
            near_of.append([i - (w_near - 1) + p for p in range(w_near)])
            offs = []
            for hd in range(n_heads):
                o, per_piece = jnp.float32(0.0), [jnp.float32(0.0)]
                for dd in range(1, w_near):
                    o = o + tot_ref[i - dd, hd]
                    per_piece.append(o)
                offs.append(per_piece[::-1])
            offs_of.append(offs)
        s_next = pair_scores(0, step_rows_of(k_ref, knew_ref, 0, 0), 0)
        for u, (sub, g) in enumerate(units):
            s_pair = s_next
            if u + 1 < len(units):
                sub1, g1 = units[u + 1]
                s_next = pair_scores(sub1 * ATT_BLK, step_rows_of(k_ref, knew_ref, sub1, g1), g1)
            rows = pl.ds(sub * ATT_BLK, ATT_BLK)
            p_pair, _ = near_weights(s_pair, rows, g, near_of[sub], offs_of[sub], True)
            r = pair_values(p_pair, step_rows_of(v_ref, vnew_ref, sub, g))
            gated_output(rows, g, r[:, 0:LANES], r[:, LANES:])
        check_next_step()
        keep_step_rows()
        merge()

    def sub_body(sub, carry):
        i = step * nsub + sub
        r0 = pl.multiple_of(sub * ATT_BLK, ATT_BLK)

        rows = pl.ds(r0, ATT_BLK)
        qk = [2.0 * qmx_ref[i, hd] * kg_ref[0, hd] for hd in range(n_heads)]

        near = [i - (w_near - 1) + p for p in range(w_near)]
        near_c = [jnp.maximum(b, 0) for b in near]
        near_offs, offs_far = [], []
        for hd in range(n_heads):
            back = [tot_at(i - dd, hd) for dd in range(1, w_near)]
            offs = []
            for p in range(w_near):
                o = jnp.float32(0.0)
                for dd in range(1, w_near - p):
                    o = o + back[dd - 1]
                offs.append(jnp.where(near[p] >= 0, o, NEG))
            near_offs.append(offs)
            o = jnp.float32(0.0)
            for t in back:
                o = o + t
            offs_far.append(o)

        def near_chunk(bounded):
            def fn():
                s_next = pair_scores(r0, rows_of(k_ref, near_c, 0), 0)
                for g in range(n_pairs):
                    s_pair = s_next
                    if g + 1 < n_pairs:
                        s_next = pair_scores(r0, rows_of(k_ref, near_c, g + 1), g + 1)
                    p_pair, ms = near_weights(s_pair, rows, g, near_c, near_offs, bounded)
                    for e in range(PAIR):
                        m_ref[PAIR * g + e] = ms[e]
                    r = pair_values(p_pair, rows_of(v_ref, near_c, g))
                    acc_ref[g] = r[:, 0:LANES]
                    l_ref[g] = r[:, LANES:]
                    gated_output(rows, g, r[:, 0:LANES], r[:, LANES:])
            return fn

        bounded_ok = qk[0] <= EXP2_SAFE_SPAN
        for hd in range(1, n_heads):
            bounded_ok = jnp.logical_and(bounded_ok, qk[hd] <= EXP2_SAFE_SPAN)
        pl.when(bounded_ok)(near_chunk(True))
        pl.when(jnp.logical_not(bounded_ok))(near_chunk(False))

        def far_cond(c):
            top = i - c[0] * w_near
            need = qk[0] + c[1] >= -EXP2_UNDERFLOW
            for hd in range(1, n_heads):
                need = jnp.logical_or(need, qk[hd] + c[1 + hd] >= -EXP2_UNDERFLOW)
            return jnp.logical_and(top >= 0, need)

        def far_body(c):
            top = i - c[0] * w_near
            jc = jnp.maximum(top - (w_near - 1), 0)
            blocks = [jc + p for p in range(w_near)]
            keepc = col < (top + 1 - jc) * ATT_BLK
            new = [c[0] + 1]
            for g in range(n_pairs):
                s_pair = pair_scores(r0, rows_of(k_ref, blocks, g), g)
                ps, alphas = [], []
                for e in range(PAIR):
                    hd = PAIR * g + e
                    tt = [tot_at(top - b, hd) for b in range(w_near)]
                    offs = []
                    for p in range(w_near):
                        behind = top - (jc + p)
                        o = c[1 + hd]
                        for b in range(w_near):
                            o = o + jnp.where(behind >= b, tt[b], 0.0)
                        offs.append(o)
                    s = s_pair[:, e * cw:(e + 1) * cw] + cq_ref[rows, hd:hd + 1] + decay_row(blocks, offs, hd)
                    s = jnp.where(keepc, s, NEG)
                    m_old = m_ref[hd]
                    m_new = jnp.maximum(m_old, jnp.max(s, axis=1, keepdims=True))
                    m_ref[hd] = m_new
                    alphas.append(jnp.broadcast_to(jnp.exp2(m_old - m_new), (ATT_BLK, LANES)))
                    ps.append(jnp.exp2(s - m_new).astype(BF16))
                    o = c[1 + hd]
                    for t in tt:
                        o = o + t
                    new.append(o)
                alpha = jnp.where(lo_q, alphas[0], alphas[1])
                r = pair_values(jnp.concatenate(ps, axis=1), rows_of(v_ref, blocks, g))
                acc_ref[g] = alpha * acc_ref[g] + r[:, 0:LANES]
                l_ref[g] = alpha * l_ref[g] + r[:, LANES:]
            return tuple(new)

        far = lax.while_loop(far_cond, far_body, (jnp.int32(1),) + tuple(offs_far))

        @pl.when(far[0] > 1)
        def _():
            for g in range(n_pairs):
                gated_output(rows, g, acc_ref[g], l_ref[g])
        return carry

    @pl.when(step == 0)
    def _():
        straight_ref[0] = 0

    straight = straight_ref[0] == 1
    pl.when(straight)(straight_step)

    @pl.when(jnp.logical_not(straight))
    def _():
        keep_step_rows()
        check_next_step()
        lax.fori_loop(0, nsub, sub_body, 0)
        merge()


def _prompt_attention(tot, qmx, kg, q, cq, nq, ck, kb, vb, sa, zp, x2, ada, wo, fg, *, tm, n_heads):
    rows, d = x2.shape
    a_w = n_heads * HEAD_DIM
    n_pairs = n_heads // PAIR
    assert rows % tm == 0 and tm % ATT_BLK == 0 and tm // ATT_BLK >= NEAR_BLOCKS - 1 and n_heads % PAIR == 0
    row_blk = lambda w: pl.BlockSpec((tm, w), lambda i, *_: (i, 0))
    grid_spec = pltpu.PrefetchScalarGridSpec(
        num_scalar_prefetch=3,
        grid=(rows // tm,),
        in_specs=[row_blk(a_w), row_blk(n_heads), row_blk(n_heads), _resident(ck.shape), row_blk(a_w), row_blk(a_w),
                  row_blk(a_w), row_blk(zp.shape[1]), row_blk(d), _resident(ada.shape), _resident(wo.shape),
                  _resident((1, d))],
        out_specs=row_blk(d),
        scratch_shapes=[pltpu.VMEM(kb.shape, BF16),
                        pltpu.VMEM(vb.shape, BF16),
                        pltpu.VMEM((tm, a_w), BF16),
                        pltpu.VMEM((n_heads, ATT_BLK, 1), F32),
                        pltpu.VMEM((n_pairs, ATT_BLK, LANES), F32),
                        pltpu.VMEM((n_pairs, ATT_BLK, LANES), F32),
                        pltpu.SMEM((1,), jnp.int32)],
    )
    return pl.pallas_call(
        functools.partial(_attn_kernel, tm=tm, n_heads=n_heads),
        grid_spec=grid_spec,
        out_shape=jax.ShapeDtypeStruct((rows, d), F32),
        compiler_params=pltpu.CompilerParams(dimension_semantics=("arbitrary",), vmem_limit_bytes=VMEM_LIMIT),
        name="attn",
    )(tot, qmx, kg, q, cq, nq, ck, kb, vb, sa, zp, x2, ada, wo, fg)


CACHE_SPLIT = 2


def _sattn_kernel(q_ref, cq_ref, ckn_ref, kn_ref, vn_ref, kc0_ref, kc1_ref, vc0_ref, vc1_ref, lfc_ref, sa_ref, zp_ref,
                  x_ref, ada_ref, wo_ref, fg_ref, y_ref, *, n_heads):
    a_w = n_heads * HEAD_DIM
    ln = q_ref.shape[0]
    past = kc0_ref.shape[2]
    kc_refs, vc_refs = (kc0_ref, kc1_ref), (vc0_ref, vc1_ref)
    pairs_per_part = n_heads // PAIR // CACHE_SPLIT

    def cached(refs, g):
        lo = (g % pairs_per_part) * LANES
        return refs[g // pairs_per_part][0, lo:lo + LANES, :].astype(BF16)

    nb = past // LANES

    lfc = lfc_ref[0] * LOG2E
    triu = (lax.broadcasted_iota(jnp.int32, (LANES, LANES), 0)
            <= lax.broadcasted_iota(jnp.int32, (LANES, LANES), 1)).astype(BF16)
    zeros = jnp.zeros((8, LANES), F32)
    parts = []
    for b in range(nb):
        parts.extend(_split3(lfc[:, b * LANES:(b + 1) * LANES]))
        parts.append(zeros)
    cs = _dot(jnp.concatenate(parts, axis=0).astype(BF16), triu)
    after = jnp.zeros((n_heads, 1), F32)
    suffix = [None] * nb
    for b in reversed(range(nb)):
        cb = cs[32 * b:32 * b + 8] + cs[32 * b + 8:32 * b + 16] + cs[32 * b + 16:32 * b + 24]
        tot = cb[:, LANES - 1:LANES]
        suffix[b] = (tot - cb) + after
        after = after + tot
    dec_c = jnp.concatenate(suffix, axis=1)

    lane = lax.broadcasted_iota(jnp.int32, (ln, LANES), 1)
    half = [lane < HEAD_DIM, lane >= HEAD_DIM]
    causal = lax.broadcasted_iota(jnp.int32, (ln, ln), 1) <= lax.broadcasted_iota(jnp.int32, (ln, ln), 0)
    n_pairs = n_heads // PAIR

    def pair_scores(g):
        cols = slice(g * LANES, (g + 1) * LANES)
        q2 = q_ref[:, cols]
        qst = jnp.concatenate([jnp.where(half[e], q2, jnp.zeros_like(q2)) for e in range(PAIR)], axis=0)
        return _dot(qst, cached(kc_refs, g)), _dot_nt(qst, kn_ref[:, cols])

    outs = []
    s_next = pair_scores(0)
    for g in range(n_pairs):
        cols = slice(g * LANES, (g + 1) * LANES)
        sc_st, sn_st = s_next
        if g + 1 < n_pairs:
            s_next = pair_scores(g + 1)
        vct = cached(vc_refs, g)
        vn = vn_ref[:, cols]
        pc, pn, ls = [], [], []
        for e in range(PAIR):
            hd = PAIR * g + e
            rows = slice(e * ln, (e + 1) * ln)
            cqh = cq_ref[:, hd:hd + 1]
            s_c = sc_st[rows] + cqh + dec_c[hd:hd + 1, :]
            s_n = jnp.where(causal, sn_st[rows] + cqh - ckn_ref[0, hd:hd + 1, :], NEG)
            m = jnp.maximum(jnp.max(s_c, axis=1, keepdims=True), jnp.max(s_n, axis=1, keepdims=True))
            p_c = jnp.exp2(s_c - m)
            p_n = jnp.exp2(s_n - m)
            ls.append(jnp.sum(p_c, axis=1, keepdims=True) + jnp.sum(p_n, axis=1, keepdims=True))
            pc.append(p_c.astype(BF16))
            pn.append(p_n.astype(BF16))
        acc = _dot_nt(jnp.concatenate(pc, axis=0), vct) + _dot(jnp.concatenate(pn, axis=0), vn)
        o = jnp.where(half[0], acc[0:ln] / ls[0], acc[ln:2 * ln] / ls[1])
        outs.append((o * sa_ref[:, cols].astype(F32)).astype(BF16))
    za = jnp.concatenate(outs, axis=1)
    y_ref[...] = _merge_norm(x_ref[...], ada_ref[2, pl.ds(ADA_SAMPLE_ROW + pl.program_id(0), 1), :], za, zp_ref[...], wo_ref, fg_ref[...], a_w)


def _sample_attention(q, cq, ckn, kb, vb, cache_k, cache_v, lfc, sa, zp, x2, ada, wo, fg, *, ln, n_heads):
    rows, d = x2.shape
    nbatch = rows // ln
    a_w = n_heads * HEAD_DIM
    past = cache_k.shape[2]
    row_blk = lambda w: pl.BlockSpec((ln, w), lambda b: (b, 0))
    per_b = lambda s: pl.BlockSpec((1,) + s, lambda b: (b, 0, 0))
    cache_part = lambda c: pl.BlockSpec((1, a_w // CACHE_SPLIT, past), lambda b: (b, c, 0))
    assert CACHE_SPLIT == 2 and (n_heads // PAIR) % CACHE_SPLIT == 0
    return pl.pallas_call(
        functools.partial(_sattn_kernel, n_heads=n_heads),
        grid=(nbatch,),
        in_specs=[row_blk(a_w), row_blk(n_heads), per_b((n_heads, ln)), row_blk(a_w), row_blk(a_w),
                  cache_part(0), cache_part(1), cache_part(0), cache_part(1), per_b((n_heads, past)),
                  row_blk(a_w), row_blk(zp.shape[1]), row_blk(d), _resident(ada.shape), _resident(wo.shape),
                  _resident((1, d))],
        out_specs=row_blk(d),
        out_shape=jax.ShapeDtypeStruct((rows, d), F32),
        compiler_params=pltpu.CompilerParams(dimension_semantics=("arbitrary",), vmem_limit_bytes=VMEM_LIMIT),
        name="sattn",
    )(q, cq, ckn, kb, vb, cache_k, cache_k, cache_v, cache_v, lfc, sa, zp, x2, ada, wo, fg)


def kernel(x_prompt, x_sample, c_prompt, c_sample, cache_k, cache_v, cache_logf, state_pool, norm_g, w_ada, b_ada,
           w_in, b_f, w_pool, pool_scale, w_out, final_g):
    depth = norm_g.shape[0]
    assert depth == 1
    bp, seq, d = x_prompt.shape
    bs, ln, _ = x_sample.shape
    assert bp == 1
    n_heads = cache_k.shape[3]
    past = cache_k.shape[2]
    a_w = n_heads * HEAD_DIM
    pw = state_pool.shape[3]
    assert pw == len(POOL_WINDOWS) * LANES and cache_k.shape[4] == HEAD_DIM and n_heads <= 8

    ada = _ada_terms(c_prompt, c_sample, w_ada[0], b_ada)

    wit = w_in[0].T
    wp = w_pool[0]
    ps = pool_scale[0][None, :]
    wo = w_out[0]
    ng = norm_g[0][None, :]
    fg = final_g[None, :]

    bm = 1024
    xp2 = x_prompt.reshape(seq, d)
    hist_p = jnp.zeros((1, HIST_PAD, pw), F32)
    (q_p, k_p, v_p, kb_p, vb_p, sa_p, zp_p, lf_p, cq_p, nq_p, ck_p, tot, qmx, kg, ho_p) = _project(
        xp2, ada, ng, wit, b_f[0], wp, ps, hist_p,
        bm=bm, sb=ATT_BLK, segs=1, ada_row=ADA_PROMPT_ROW, start_pos=0, n_heads=n_heads, kv_head_major=False)
    y_p = _prompt_attention(tot, qmx, kg, q_p, cq_p, nq_p, ck_p, kb_p, vb_p, sa_p, zp_p, xp2, ada, wo, fg,
                            tm=512, n_heads=n_heads)

    xs2 = x_sample.reshape(bs * ln, d)
    hist_s = jnp.pad(state_pool[0], ((0, 0), (HIST_PAD - POOL_HIST, 0), (0, 0)))
    (q_s, k_s, v_s, kb_s, vb_s, sa_s, zp_s, lf_s, cq_s, _, ck_s, _, _, _, ho_s) = _project(
        xs2, ada, ng, wit, b_f[0], wp, ps, hist_s,
        bm=bs * ln, sb=ln, segs=bs, ada_row=ADA_SAMPLE_ROW, start_pos=past, n_heads=n_heads, kv_head_major=True)
    lfc = jnp.swapaxes(cache_logf[0], 1, 2)
    ckt = jnp.transpose(cache_k[0], (0, 2, 3, 1)).reshape(bs, a_w, past)
    cvt = jnp.transpose(cache_v[0], (0, 2, 3, 1)).reshape(bs, a_w, past)
    y_s = _sample_attention(q_s, cq_s, ck_s, kb_s, vb_s, ckt, cvt, lfc, sa_s, zp_s, xs2, ada, wo, fg,
                            ln=ln, n_heads=n_heads)

    hd = (n_heads, HEAD_DIM)
    seq_minor = lambda t: jnp.transpose(t.reshape(hd + (bp, seq)), (2, 3, 0, 1))[None]
    return (y_p.reshape(bp, seq, d), y_s.reshape(bs, ln, d),
            seq_minor(k_p), seq_minor(v_p), jnp.swapaxes(lf_p, 1, 2)[None],
            ho_p[:, 16 - POOL_HIST:, :][None],
            k_s.reshape((1, bs, ln) + hd), v_s.reshape((1, bs, ln) + hd), jnp.swapaxes(lf_s, 1, 2)[None],
            ho_s[:, 16 - POOL_HIST:, :][None])
```

```python
import functools

import jax
import jax.numpy as jnp
from jax import lax
from jax.experimental import pallas as pl
from jax.experimental.pallas import tpu as pltpu

HEAD_DIM = 64
POOL_WINDOWS = (2, 4, 8, 16)
EPS = 1e-6

LANES = 128
PAIR = LANES // HEAD_DIM
ATT_BLK = 128
NEAR_BLOCKS = 3
LOG2E = 1.4426950408889634
EXP2_UNDERFLOW = 151.0
EXP2_SAFE_SPAN = 100.0
NORM_SLACK = 1.01
HIST_PAD = 32
POOL_HIST = max(POOL_WINDOWS) - 1
NEG = -1e30
VMEM_LIMIT = 60 * 1024 * 1024

F32 = jnp.float32
BF16 = jnp.bfloat16


def _silu(x):
    return x * jax.nn.sigmoid(x)


def _dot(a, b):
    return jnp.dot(a, b, preferred_element_type=F32)


def _dot_nt(a, b):
    return lax.dot_general(a, b, (((1,), (1,)), ((), ())), preferred_element_type=F32)


def _split3(x):
    hi = x.astype(BF16).astype(F32)
    r1 = x - hi
    mid = r1.astype(BF16).astype(F32)
    return hi, mid, r1 - mid


def _rows_to_lanes(x, n):
    rows = x.shape[0]
    if rows < LANES:
        x = jnp.concatenate([x, jnp.zeros((LANES - rows, LANES), x.dtype)], axis=0)
    return x.T[0:n, 0:rows]


ACT_PARTS = 5


def _act_views(act_ref, width):
    return [act_ref.at[:, p * width:(p + 1) * width] for p in range(ACT_PARTS)]


def _resident(shape):
    return pl.BlockSpec(shape, lambda *_: (0,) * len(shape), pipeline_mode=pl.Buffered(1))


ADA_ROWS = 16
ADA_PROMPT_ROW, ADA_SAMPLE_ROW = 0, 8


def _ada_kernel(cp_ref, cs_ref, w_ref, b_ref, o_ref):
    ap = jnp.broadcast_to(_silu(cp_ref[...]), (ADA_SAMPLE_ROW, cp_ref.shape[1]))
    a = jnp.concatenate([ap, _silu(cs_ref[...])], axis=0).astype(BF16)
    o_ref[0] = _dot(a, w_ref[...].astype(BF16)) + b_ref[...]


def _ada_terms(c_prompt, c_sample, w_ada, b_ada):
    d = c_prompt.shape[1]
    assert c_prompt.shape[0] == 1 and c_sample.shape[0] == ADA_ROWS - ADA_SAMPLE_ROW and w_ada.shape[1] == 3 * d
    split = 2
    bn = d // split
    return pl.pallas_call(
        _ada_kernel,
        grid=(3 * split,),
        in_specs=[pl.BlockSpec(c_prompt.shape, lambda j: (0, 0)),
                  pl.BlockSpec(c_sample.shape, lambda j: (0, 0)),
                  pl.BlockSpec((d, bn), lambda j: (0, j)),
                  pl.BlockSpec((1, bn), lambda j: (0, j))],
        out_specs=pl.BlockSpec((1, ADA_ROWS, bn), lambda j: (j // split, 0, j % split)),
        out_shape=jax.ShapeDtypeStruct((3, ADA_ROWS, d), F32),
        compiler_params=pltpu.CompilerParams(dimension_semantics=("arbitrary",), vmem_limit_bytes=VMEM_LIMIT),
        name="ada",
    )(c_prompt, c_sample, w_ada, b_ada)


def _proj_kernel(bf_ref, x_ref, ada_ref, ng_ref, w_ref, wp_ref, ps_ref, h0_ref,
                 act_ref, k32_ref, v32_ref, lf_ref, cq_ref, nq_ref, ck_ref,
                 tot_ref, qmx_ref, kg_ref, ho_ref,
                 e_ref, t2_ref, t4_ref, t8_ref, kmx_ref, *, bm, sb, segs, ada_row, start_pos, n_heads, kv_head_major):
    a_w = n_heads * HEAD_DIM
    q_ref, kb_ref, vb_ref, sa_ref, zp_ref = _act_views(act_ref, a_w)
    seg_rows = bm // segs
    step = pl.program_id(0)
    x = x_ref[...]
    ms = jnp.mean(x * x, axis=-1, keepdims=True)
    xn = x * lax.rsqrt(ms + EPS)
    h = jnp.concatenate(
        [xn[g * seg_rows:(g + 1) * seg_rows] * (ng_ref[...] * (1.0 + ada_ref[1, ada_row + g:ada_row + g + 1, :]))
         + ada_ref[0, ada_row + g:ada_row + g + 1, :] for g in range(segs)], axis=0).astype(BF16)

    pw = len(POOL_WINDOWS) * LANES
    sc = LOG2E / (HEAD_DIM ** 0.5)
    o_pool = 4 * a_w + n_heads

    def w_rows(lo, hi):
        return w_ref[lo:hi, :].astype(BF16)

    pu = _dot_nt(h, w_rows(o_pool, o_pool + pw))
    sel = (lax.broadcasted_iota(jnp.int32, (a_w, LANES), 0) // HEAD_DIM
           == lax.broadcasted_iota(jnp.int32, (a_w, LANES), 1)).astype(BF16)

    def store_kv(ref32, refb, p):
        refb[...] = p.astype(BF16)
        if kv_head_major:
            for hd in range(n_heads):
                ref32[:, hd, :] = p[:, hd * HEAD_DIM:(hd + 1) * HEAD_DIM]
        else:
            ref32[...] = p.T

    ext = HIST_PAD + seg_rows
    n = segs * ext

    def load_history():
        for g in range(segs):
            e_ref[g * ext:g * ext + HIST_PAD, :] = h0_ref[g]

    if segs > 1:
        load_history()
    else:
        pl.when(step == 0)(load_history)

    for g in range(segs):
        e_ref[g * ext + HIST_PAD:(g + 1) * ext, :] = pu[g * seg_rows:(g + 1) * seg_rows]
    t2_ref[8:n, :] = e_ref[8:n, :] + e_ref[7:n - 1, :]
    t4_ref[16:n, :] = t2_ref[16:n, LANES:] + t2_ref[14:n - 2, LANES:]
    t8_ref[24:n, :] = t4_ref[24:n, LANES:] + t4_ref[20:n - 4, LANES:]

    def seg_rows_of(ref, cols, back=0):
        return jnp.concatenate([ref[g * ext + HIST_PAD - back:(g + 1) * ext - back, cols] for g in range(segs)], axis=0)

    lane0, lane1 = slice(0, LANES), slice(LANES, 2 * LANES)
    sums = [seg_rows_of(t2_ref, lane0), seg_rows_of(t4_ref, lane0), seg_rows_of(t8_ref, lane0),
            seg_rows_of(t8_ref, lane1) + seg_rows_of(t8_ref, lane1, back=8)]
    row = lax.broadcasted_iota(jnp.int32, (bm, 1), 0)
    pos1 = start_pos + 1 + (step * bm + row if segs == 1 else row % seg_rows)
    pool_d = []
    for g, w in enumerate(POOL_WINDOWS):
        rc = 1.0 / jnp.minimum(pos1, w).astype(F32)
        pool_d.append((sums[g] * rc - pu[:, g * LANES:(g + 1) * LANES]).astype(BF16))
    for g in range(segs):
        ho_ref[g] = e_ref[(g + 1) * ext - 16:(g + 1) * ext, :]
    if segs == 1:
        e_ref[0:HIST_PAD, :] = e_ref[bm:n, :]

    pk = _dot_nt(h, w_rows(a_w, 2 * a_w))
    store_kv(k32_ref, kb_ref, pk)
    nk = jnp.sqrt(_dot((pk * pk).astype(BF16), sel)) * NORM_SLACK
    spg = _silu(_dot_nt(h, w_rows(o_pool + pw, o_pool + 2 * pw)))

    lane = lax.broadcasted_iota(jnp.int32, (1, LANES), 1)
    bias = jnp.zeros((1, LANES), F32)
    for hd in range(n_heads):
        bias = jnp.where(lane == hd, bf_ref[hd], bias)
    wf = jnp.concatenate([w_ref[4 * a_w:o_pool, :], jnp.zeros((LANES - n_heads, w_ref.shape[1]), F32)], axis=0)
    z = _dot_nt(h, wf.astype(BF16)) + bias
    lf = jnp.minimum(z, 0.0) - jnp.log1p(jnp.exp(-jnp.abs(z)))

    sa_ref[...] = _silu(_dot_nt(h, w_rows(3 * a_w, 4 * a_w))).astype(BF16)

    tri = (lax.broadcasted_iota(jnp.int32, (sb, sb), 1)
           <= lax.broadcasted_iota(jnp.int32, (sb, sb), 0)).astype(BF16)
    lf2 = lf * LOG2E
    tots, kmx = [], []
    for s in range(bm // sb):
        rows = slice(s * sb, (s + 1) * sb)
        parts = _dot(tri, jnp.concatenate(_split3(lf2[rows]), axis=1).astype(BF16))
        cb = parts[:, 0:LANES] + parts[:, LANES:2 * LANES] + parts[:, 2 * LANES:]
        cq_ref[rows, :] = cb[:, 0:n_heads]
        ck_ref[s] = _rows_to_lanes(cb, n_heads)
        off = (s * sb) % seg_rows
        lf_ref[(s * sb) // seg_rows, :, off:off + sb] = _rows_to_lanes(lf[rows], n_heads)
        tots.append(cb[sb - 1:sb, :])
        kmx.append(jnp.max(nk[rows], axis=0, keepdims=True))
    tot_ref[...] = jnp.concatenate(tots, axis=0)[:, 0:n_heads]
    kmax = kmx[0]
    for t in kmx[1:]:
        kmax = jnp.maximum(kmax, t)

    store_kv(v32_ref, vb_ref, _dot_nt(h, w_rows(2 * a_w, 3 * a_w)))

    zero_w = jnp.zeros((LANES, LANES), BF16)
    for g in range(0, len(POOL_WINDOWS), 2):
        cols = slice(g * LANES, (g + 2) * LANES)
        w2 = jnp.concatenate([jnp.concatenate([wp_ref[g].astype(BF16), zero_w], axis=1),
                              jnp.concatenate([zero_w, wp_ref[g + 1].astype(BF16)], axis=1)], axis=0)
        y = _dot(jnp.concatenate([pool_d[g], pool_d[g + 1]], axis=1), w2) * ps_ref[:, cols]
        zp_ref[:, cols] = (y * spg[:, cols]).astype(BF16)

    qs = _dot_nt(h, w_rows(0, a_w)) * sc
    q_ref[...] = qs.astype(BF16)
    nq = jnp.sqrt(_dot((qs * qs).astype(BF16), sel)) * NORM_SLACK
    nq_ref[...] = nq[:, 0:n_heads]
    qmx_ref[...] = jnp.concatenate([jnp.max(nq[s * sb:(s + 1) * sb], axis=0, keepdims=True)
                                    for s in range(bm // sb)], axis=0)[:, 0:n_heads]

    @pl.when(step > 0)
    def _():
        kmx_ref[...] = jnp.maximum(kmx_ref[...], kmax)

    @pl.when(step == 0)
    def _():
        kmx_ref[...] = kmax

    kg_ref[...] = kmx_ref[...]


def _project(x2, ada, norm_g, wit, b_f, wp, ps, hist0, *, bm, sb, segs, ada_row, start_pos, n_heads,
             kv_head_major):
    rows, d = x2.shape
    a_w = n_heads * HEAD_DIM
    pw = len(POOL_WINDOWS) * LANES
    n_steps = rows // bm
    assert segs == 1 or n_steps == 1
    n_streams = segs
    seg_rows = bm // segs
    nsb = bm // sb
    assert nsb == 8 and seg_rows % sb == 0
    row_blk = lambda w: pl.BlockSpec((bm, w), lambda i, *_: (i, 0))
    per_stream = lambda r, w: pl.BlockSpec((segs, r, w), lambda i, *_: (0, 0, 0))
    kern = functools.partial(_proj_kernel, bm=bm, sb=sb, segs=segs, ada_row=ada_row, start_pos=start_pos,
                             n_heads=n_heads, kv_head_major=kv_head_major)
    if kv_head_major:
        kv_shape = (rows, n_heads, HEAD_DIM)
        kv_blk = pl.BlockSpec((bm, n_heads, HEAD_DIM), lambda i, *_: (i, 0, 0))
    else:
        kv_shape = (a_w, rows)
        kv_blk = pl.BlockSpec((a_w, bm), lambda i, *_: (0, i))
    assert pw == a_w
    out_shape = (
        jax.ShapeDtypeStruct((rows, ACT_PARTS * a_w), BF16),
        jax.ShapeDtypeStruct(kv_shape, F32),
        jax.ShapeDtypeStruct(kv_shape, F32),
        jax.ShapeDtypeStruct((n_streams, n_heads, rows // n_streams), F32),
        jax.ShapeDtypeStruct((rows, n_heads), F32),
        jax.ShapeDtypeStruct((rows, n_heads), F32),
        jax.ShapeDtypeStruct((rows // sb, n_heads, sb), F32),
        jax.ShapeDtypeStruct((rows // sb, n_heads), F32),
        jax.ShapeDtypeStruct((rows // sb, n_heads), F32),
        jax.ShapeDtypeStruct((1, LANES), F32),
        jax.ShapeDtypeStruct((n_streams, 16, pw), F32),
    )
    out_specs = (
        row_blk(ACT_PARTS * a_w), kv_blk, kv_blk,
        pl.BlockSpec((segs, n_heads, seg_rows), lambda i, *_: (0, 0, i)),
        row_blk(n_heads), row_blk(n_heads),
        pl.BlockSpec((nsb, n_heads, sb), lambda i, *_: (i, 0, 0)),
        pl.BlockSpec((nsb, n_heads), lambda i, *_: (i, 0)),
        pl.BlockSpec((nsb, n_heads), lambda i, *_: (i, 0)),
        pl.BlockSpec((1, LANES), lambda i, *_: (0, 0)),
        per_stream(16, pw),
    )
    in_specs = [
        row_blk(d),
        _resident(ada.shape),
        _resident((1, d)),
        _resident(wit.shape), _resident(wp.shape), _resident(ps.shape),
        per_stream(HIST_PAD, pw),
    ]
    return pl.pallas_call(
        kern,
        grid_spec=pltpu.PrefetchScalarGridSpec(
            num_scalar_prefetch=1,
            grid=(n_steps,),
            in_specs=in_specs,
            out_specs=out_specs,
            scratch_shapes=[pltpu.VMEM((bm + segs * HIST_PAD, pw), F32),
                            pltpu.VMEM((bm + segs * HIST_PAD, pw), F32),
                            pltpu.VMEM((bm + segs * HIST_PAD, pw - LANES), F32),
                            pltpu.VMEM((bm + segs * HIST_PAD, pw - 2 * LANES), F32),
                            pltpu.VMEM((1, LANES), F32)]),
        out_shape=out_shape,
        compiler_params=pltpu.CompilerParams(dimension_semantics=("arbitrary",), vmem_limit_bytes=VMEM_LIMIT),
        name="proj",
    )(b_f, x2, ada, norm_g, wit, wp, ps, hist0)


def _merge_norm(x, gate, za, zp, wo_ref, fg, a_w):
    dy = _dot(za, wo_ref[0:a_w, :].astype(BF16)) + _dot(zp, wo_ref[a_w:, :].astype(BF16))
    out = x + gate * dy
    ms = jnp.mean(out * out, axis=-1, keepdims=True)
    return out * lax.rsqrt(ms + EPS) * fg


def _attn_kernel(tot_ref, qmx_ref, kg_ref,
                 act_ref, cq_ref, nq_ref, ck_ref, x_ref, ada_ref, wo_ref, fg_ref,
                 y_ref,
                 k_ref, v_ref, z_ref, m_ref, l_ref, acc_ref, straight_ref, *, tm, n_heads):
    a_w = n_heads * HEAD_DIM
    q_ref, knew_ref, vnew_ref, sa_ref, zp_ref = _act_views(act_ref, a_w)
    n_pairs = n_heads // PAIR
    nsub = tm // ATT_BLK
    w_near = NEAR_BLOCKS
    cw = w_near * ATT_BLK
    step = pl.program_id(0)

    def keep_step_rows():
        k_ref[pl.ds(pl.multiple_of(step * tm, tm), tm), :] = knew_ref[...]
        v_ref[pl.ds(pl.multiple_of(step * tm, tm), tm), :] = vnew_ref[...]

    lo_q = lax.broadcasted_iota(jnp.int32, (ATT_BLK, LANES), 1) < HEAD_DIM
    lo_k = lax.broadcasted_iota(jnp.int32, (cw, LANES), 1) < HEAD_DIM
    col = lax.broadcasted_iota(jnp.int32, (ATT_BLK, cw), 1)
    tri = (lax.broadcasted_iota(jnp.int32, (ATT_BLK, ATT_BLK), 1)
           <= lax.broadcasted_iota(jnp.int32, (ATT_BLK, ATT_BLK), 0))
    tri_bias = jnp.where(tri, 0.0, NEG).astype(F32)
    zeros_k = jnp.zeros((cw, LANES), BF16)
    ind_lo = jnp.where(lo_k, 1.0, 0.0).astype(BF16)
    ind_hi = jnp.where(lo_k, 0.0, 1.0).astype(BF16)

    def tot_at(b, hd):
        return jnp.where(b >= 0, tot_ref[jnp.maximum(b, 0), hd], 0.0)

    def rows_of(ref, blocks, g):
        return jnp.concatenate(
            [ref[pl.ds(pl.multiple_of(b * ATT_BLK, ATT_BLK), ATT_BLK), g * LANES:(g + 1) * LANES] for b in blocks],
            axis=0)

    def step_rows_of(ref, new_ref, sub, g):
        parts = []
        for p in range(w_near):
            rel = sub - (w_near - 1) + p
            if rel >= 0:
                parts.append(new_ref[rel * ATT_BLK:(rel + 1) * ATT_BLK, g * LANES:(g + 1) * LANES])
            else:
                start = pl.multiple_of((step * nsub + rel) * ATT_BLK, ATT_BLK)
                parts.append(ref[pl.ds(start, ATT_BLK), g * LANES:(g + 1) * LANES])
        return jnp.concatenate(parts, axis=0)

    def pair_scores(r0, kc, g):
        keys = jnp.concatenate([jnp.where(lo_k, kc, zeros_k), jnp.where(lo_k, zeros_k, kc)], axis=0)
        return _dot_nt(q_ref[pl.ds(r0, ATT_BLK), g * LANES:(g + 1) * LANES], keys)

    def pair_values(p_pair, vc):
        vals = jnp.concatenate([jnp.concatenate([jnp.where(lo_k, vc, zeros_k), ind_lo], axis=1),
                                jnp.concatenate([jnp.where(lo_k, zeros_k, vc), ind_hi], axis=1)], axis=0)
        return _dot(p_pair, vals)

    def decay_row(blocks, offs, hd):
        return jnp.concatenate([offs[p] - ck_ref[blocks[p], hd:hd + 1, :] for p in range(w_near)], axis=1)

    def near_weights(s_pair, rows, g, near_c, near_offs, bounded):
        ps, ms = [], []
        for e in range(PAIR):
            hd = PAIR * g + e
            cqh = cq_ref[rows, hd:hd + 1]
            if bounded:
                m = nq_ref[rows, hd:hd + 1] * kg_ref[0, hd]
                cqh = cqh - m
            dec = decay_row(near_c, near_offs[hd], hd)
            pieces = []
            for p in range(w_near):
                lanes = slice(e * cw + p * ATT_BLK, e * cw + (p + 1) * ATT_BLK)
                sp = s_pair[:, lanes] + cqh + dec[:, p * ATT_BLK:(p + 1) * ATT_BLK]
                pieces.append(sp + tri_bias if p == w_near - 1 else sp)
            s = jnp.concatenate(pieces, axis=1)
            if not bounded:
                m = jnp.max(s, axis=1, keepdims=True)
                s = s - m
            ms.append(m)
            ps.append(jnp.exp2(s).astype(BF16))
        return jnp.concatenate(ps, axis=1), ms

    def gated_output(rows, g, acc, l):
        cols = slice(g * LANES, (g + 1) * LANES)
        z_ref[rows, cols] = ((acc / l) * sa_ref[rows, cols].astype(F32)).astype(BF16)

    def merge(chunks=1):
        rows_c = tm // chunks
        for c in range(chunks):
            rows = slice(c * rows_c, (c + 1) * rows_c)
            y_ref[rows, :] = _merge_norm(x_ref[rows, :], ada_ref[2, ADA_PROMPT_ROW:ADA_PROMPT_ROW + 1, :],
                                         z_ref[rows, :], zp_ref[rows, :], wo_ref, fg_ref[...], a_w)

    def check_next_step():
        ok = jnp.bool_(True)
        last = qmx_ref.shape[0] - 1
        for sub in range(nsub):
            i = jnp.minimum((step + 1) * nsub + sub, last)
            for hd in range(n_heads):
                qk = 2.0 * qmx_ref[i, hd] * kg_ref[0, hd]
                back = qk
                for dd in range(1, w_near):
                    back = back + tot_ref[i - dd, hd]
                ok = jnp.logical_and(ok, jnp.logical_and(qk <= EXP2_SAFE_SPAN, back < -EXP2_UNDERFLOW))
        straight_ref[0] = ok.astype(jnp.int32)

    def straight_step():
        units = [(sub, g) for sub in range(nsub) for g in range(n_pairs)]
        near_of, offs_of = [], []
        for sub in range(nsub):
            i = step * nsub + sub
            near_of.append([i - (w_near - 1) + p for p in range(w_near)])
            offs = []
            for hd in range(n_heads):
                o, per_piece = jnp.float32(0.0), [jnp.float32(0.0)]
                for dd in range(1, w_near):
                    o = o + tot_ref[i - dd, hd]
                    per_piece.append(o)
                offs.append(per_piece[::-1])
            offs_of.append(offs)
        s_next = pair_scores(0, step_rows_of(k_ref, knew_ref, 0, 0), 0)
        for u, (sub, g) in enumerate(units):
            s_pair = s_next
            if u + 1 < len(units):
                sub1, g1 = units[u + 1]
                s_next = pair_scores(sub1 * ATT_BLK, step_rows_of(k_ref, knew_ref, sub1, g1), g1)
            rows = pl.ds(sub * ATT_BLK, ATT_BLK)
            p_pair, _ = near_weights(s_pair, rows, g, near_of[sub], offs_of[sub], True)
            r = pair_values(p_pair, step_rows_of(v_ref, vnew_ref, sub, g))
            gated_output(rows, g, r[:, 0:LANES], r[:, LANES:])
        check_next_step()
        keep_step_rows()
        merge(chunks=2)

    def sub_body(sub, carry):
        i = step * nsub + sub
        r0 = pl.multiple_of(sub * ATT_BLK, ATT_BLK)

        rows = pl.ds(r0, ATT_BLK)
        qk = [2.0 * qmx_ref[i, hd] * kg_ref[0, hd] for hd in range(n_heads)]

        near = [i - (w_near - 1) + p for p in range(w_near)]
        near_c = [jnp.maximum(b, 0) for b in near]
        near_offs, offs_far = [], []
        for hd in range(n_heads):
            back = [tot_at(i - dd, hd) for dd in range(1, w_near)]
            offs = []
            for p in range(w_near):
                o = jnp.float32(0.0)
                for dd in range(1, w_near - p):
                    o = o + back[dd - 1]
                offs.append(jnp.where(near[p] >= 0, o, NEG))
            near_offs.append(offs)
            o = jnp.float32(0.0)
            for t in back:
                o = o + t
            offs_far.append(o)

        def near_chunk(bounded):
            def fn():
                s_next = pair_scores(r0, rows_of(k_ref, near_c, 0), 0)
                for g in range(n_pairs):
                    s_pair = s_next
                    if g + 1 < n_pairs:
                        s_next = pair_scores(r0, rows_of(k_ref, near_c, g + 1), g + 1)
                    p_pair, ms = near_weights(s_pair, rows, g, near_c, near_offs, bounded)
                    for e in range(PAIR):
                        m_ref[PAIR * g + e] = ms[e]
                    r = pair_values(p_pair, rows_of(v_ref, near_c, g))
                    acc_ref[g] = r[:, 0:LANES]
                    l_ref[g] = r[:, LANES:]
                    gated_output(rows, g, r[:, 0:LANES], r[:, LANES:])
            return fn

        bounded_ok = qk[0] <= EXP2_SAFE_SPAN
        for hd in range(1, n_heads):
            bounded_ok = jnp.logical_and(bounded_ok, qk[hd] <= EXP2_SAFE_SPAN)
        pl.when(bounded_ok)(near_chunk(True))
        pl.when(jnp.logical_not(bounded_ok))(near_chunk(False))

        def far_cond(c):
            top = i - c[0] * w_near
            need = qk[0] + c[1] >= -EXP2_UNDERFLOW
            for hd in range(1, n_heads):
                need = jnp.logical_or(need, qk[hd] + c[1 + hd] >= -EXP2_UNDERFLOW)
            return jnp.logical_and(top >= 0, need)

        def far_body(c):
            top = i - c[0] * w_near
            jc = jnp.maximum(top - (w_near - 1), 0)
            blocks = [jc + p for p in range(w_near)]
            keepc = col < (top + 1 - jc) * ATT_BLK
            new = [c[0] + 1]
            for g in range(n_pairs):
                s_pair = pair_scores(r0, rows_of(k_ref, blocks, g), g)
                ps, alphas = [], []
                for e in range(PAIR):
                    hd = PAIR * g + e
                    tt = [tot_at(top - b, hd) for b in range(w_near)]
                    offs = []
                    for p in range(w_near):
                        behind = top - (jc + p)
                        o = c[1 + hd]
                        for b in range(w_near):
                            o = o + jnp.where(behind >= b, tt[b], 0.0)
                        offs.append(o)
                    s = s_pair[:, e * cw:(e + 1) * cw] + cq_ref[rows, hd:hd + 1] + decay_row(blocks, offs, hd)
                    s = jnp.where(keepc, s, NEG)
                    m_old = m_ref[hd]
                    m_new = jnp.maximum(m_old, jnp.max(s, axis=1, keepdims=True))
                    m_ref[hd] = m_new
                    alphas.append(jnp.broadcast_to(jnp.exp2(m_old - m_new), (ATT_BLK, LANES)))
                    ps.append(jnp.exp2(s - m_new).astype(BF16))
                    o = c[1 + hd]
                    for t in tt:
                        o = o + t
                    new.append(o)
                alpha = jnp.where(lo_q, alphas[0], alphas[1])
                r = pair_values(jnp.concatenate(ps, axis=1), rows_of(v_ref, blocks, g))
                acc_ref[g] = alpha * acc_ref[g] + r[:, 0:LANES]
                l_ref[g] = alpha * l_ref[g] + r[:, LANES:]
            return tuple(new)

        far = lax.while_loop(far_cond, far_body, (jnp.int32(1),) + tuple(offs_far))

        @pl.when(far[0] > 1)
        def _():
            for g in range(n_pairs):
                gated_output(rows, g, acc_ref[g], l_ref[g])
        return carry

    @pl.when(step == 0)
    def _():
        straight_ref[0] = 0

    straight = straight_ref[0] == 1
    pl.when(straight)(straight_step)

    @pl.when(jnp.logical_not(straight))
    def _():
        keep_step_rows()
        check_next_step()
        lax.fori_loop(0, nsub, sub_body, 0)
        merge()


def _prompt_attention(tot, qmx, kg, act, cq, nq, ck, x2, ada, wo, fg, *, tm, n_heads):
    rows, d = x2.shape
    a_w = n_heads * HEAD_DIM
    n_pairs = n_heads // PAIR
    assert rows % tm == 0 and tm % ATT_BLK == 0 and tm // ATT_BLK >= NEAR_BLOCKS - 1 and n_heads % PAIR == 0
    row_blk = lambda w: pl.BlockSpec((tm, w), lambda i, *_: (i, 0))
    grid_spec = pltpu.PrefetchScalarGridSpec(
        num_scalar_prefetch=3,
        grid=(rows // tm,),
        in_specs=[row_blk(ACT_PARTS * a_w), row_blk(n_heads), row_blk(n_heads), _resident(ck.shape), row_blk(d),
                  _resident(ada.shape), _resident(wo.shape), _resident((1, d))],
        out_specs=row_blk(d),
        scratch_shapes=[pltpu.VMEM((rows, a_w), BF16),
                        pltpu.VMEM((rows, a_w), BF16),
                        pltpu.VMEM((tm, a_w), BF16),
                        pltpu.VMEM((n_heads, ATT_BLK, 1), F32),
                        pltpu.VMEM((n_pairs, ATT_BLK, LANES), F32),
                        pltpu.VMEM((n_pairs, ATT_BLK, LANES), F32),
                        pltpu.SMEM((1,), jnp.int32)],
    )
    return pl.pallas_call(
        functools.partial(_attn_kernel, tm=tm, n_heads=n_heads),
        grid_spec=grid_spec,
        out_shape=jax.ShapeDtypeStruct((rows, d), F32),
        compiler_params=pltpu.CompilerParams(dimension_semantics=("arbitrary",), vmem_limit_bytes=VMEM_LIMIT),
        name="attn",
    )(tot, qmx, kg, act, cq, nq, ck, x2, ada, wo, fg)


CACHE_SPLIT = 2


def _sattn_kernel(act_ref, cq_ref, ckn_ref, kc0_ref, kc1_ref, vc0_ref, vc1_ref, lfc_ref,
                  x_ref, ada_ref, wo_ref, fg_ref, y_ref, *, n_heads):
    a_w = n_heads * HEAD_DIM
    q_ref, kn_ref, vn_ref, sa_ref, zp_ref = _act_views(act_ref, a_w)
    ln = act_ref.shape[0]
    past = kc0_ref.shape[2]
    kc_refs, vc_refs = (kc0_ref, kc1_ref), (vc0_ref, vc1_ref)
    pairs_per_part = n_heads // PAIR // CACHE_SPLIT

    def cached(refs, g):
        lo = (g % pairs_per_part) * LANES
        return refs[g // pairs_per_part][0, lo:lo + LANES, :].astype(BF16)

    nb = past // LANES

    lfc = lfc_ref[0] * LOG2E
    triu = (lax.broadcasted_iota(jnp.int32, (LANES, LANES), 0)
            <= lax.broadcasted_iota(jnp.int32, (LANES, LANES), 1)).astype(BF16)
    zeros = jnp.zeros((8, LANES), F32)
    parts = []
    for b in range(nb):
        parts.extend(_split3(lfc[:, b * LANES:(b + 1) * LANES]))
        parts.append(zeros)
    cs = _dot(jnp.concatenate(parts, axis=0).astype(BF16), triu)
    after = jnp.zeros((n_heads, 1), F32)
    suffix = [None] * nb
    for b in reversed(range(nb)):
        cb = cs[32 * b:32 * b + 8] + cs[32 * b + 8:32 * b + 16] + cs[32 * b + 16:32 * b + 24]
        tot = cb[:, LANES - 1:LANES]
        suffix[b] = (tot - cb) + after
        after = after + tot
    dec_c = jnp.concatenate(suffix, axis=1)

    lane = lax.broadcasted_iota(jnp.int32, (ln, LANES), 1)
    half = [lane < HEAD_DIM, lane >= HEAD_DIM]
    causal = lax.broadcasted_iota(jnp.int32, (ln, ln), 1) <= lax.broadcasted_iota(jnp.int32, (ln, ln), 0)
    n_pairs = n_heads // PAIR

    def pair_scores(g):
        cols = slice(g * LANES, (g + 1) * LANES)
        q2 = q_ref[:, cols]
        qst = jnp.concatenate([jnp.where(half[e], q2, jnp.zeros_like(q2)) for e in range(PAIR)], axis=0)
        return _dot(qst, cached(kc_refs, g)), _dot_nt(qst, kn_ref[:, cols])

    outs = []
    s_next = pair_scores(0)
    for g in range(n_pairs):
        cols = slice(g * LANES, (g + 1) * LANES)
        sc_st, sn_st = s_next
        if g + 1 < n_pairs:
            s_next = pair_scores(g + 1)
        vct = cached(vc_refs, g)
        vn = vn_ref[:, cols]
        pc, pn, ls = [], [], []
        for e in range(PAIR):
            hd = PAIR * g + e
            rows = slice(e * ln, (e + 1) * ln)
            cqh = cq_ref[:, hd:hd + 1]
            s_c = sc_st[rows] + cqh + dec_c[hd:hd + 1, :]
            s_n = jnp.where(causal, sn_st[rows] + cqh - ckn_ref[0, hd:hd + 1, :], NEG)
            m = jnp.maximum(jnp.max(s_c, axis=1, keepdims=True), jnp.max(s_n, axis=1, keepdims=True))
            p_c = jnp.exp2(s_c - m)
            p_n = jnp.exp2(s_n - m)
            ls.append(jnp.sum(p_c, axis=1, keepdims=True) + jnp.sum(p_n, axis=1, keepdims=True))
            pc.append(p_c.astype(BF16))
            pn.append(p_n.astype(BF16))
        acc = _dot_nt(jnp.concatenate(pc, axis=0), vct) + _dot(jnp.concatenate(pn, axis=0), vn)
        o = jnp.where(half[0], acc[0:ln] / ls[0], acc[ln:2 * ln] / ls[1])
        outs.append((o * sa_ref[:, cols].astype(F32)).astype(BF16))
    za = jnp.concatenate(outs, axis=1)
    y_ref[...] = _merge_norm(x_ref[...], ada_ref[2, pl.ds(ADA_SAMPLE_ROW + pl.program_id(0), 1), :], za, zp_ref[...], wo_ref, fg_ref[...], a_w)


def _sample_attention(act, cq, ckn, cache_k, cache_v, lfc, x2, ada, wo, fg, *, ln, n_heads):
    rows, d = x2.shape
    nbatch = rows // ln
    a_w = n_heads * HEAD_DIM
    past = cache_k.shape[2]
    row_blk = lambda w: pl.BlockSpec((ln, w), lambda b: (b, 0))
    per_b = lambda s: pl.BlockSpec((1,) + s, lambda b: (b, 0, 0))
    cache_part = lambda c: pl.BlockSpec((1, a_w // CACHE_SPLIT, past), lambda b: (b, c, 0))
    assert CACHE_SPLIT == 2 and (n_heads // PAIR) % CACHE_SPLIT == 0
    return pl.pallas_call(
        functools.partial(_sattn_kernel, n_heads=n_heads),
        grid=(nbatch,),
        in_specs=[row_blk(ACT_PARTS * a_w), row_blk(n_heads), per_b((n_heads, ln)),
                  cache_part(0), cache_part(1), cache_part(0), cache_part(1), per_b((n_heads, past)),
                  row_blk(d), _resident(ada.shape), _resident(wo.shape), _resident((1, d))],
        out_specs=row_blk(d),
        out_shape=jax.ShapeDtypeStruct((rows, d), F32),
        compiler_params=pltpu.CompilerParams(dimension_semantics=("arbitrary",), vmem_limit_bytes=VMEM_LIMIT),
        name="sattn",
    )(act, cq, ckn, cache_k, cache_k, cache_v, cache_v, lfc, x2, ada, wo, fg)


def kernel(x_prompt, x_sample, c_prompt, c_sample, cache_k, cache_v, cache_logf, state_pool, norm_g, w_ada, b_ada,
           w_in, b_f, w_pool, pool_scale, w_out, final_g):
    depth = norm_g.shape[0]
    assert depth == 1
    bp, seq, d = x_prompt.shape
    bs, ln, _ = x_sample.shape
    assert bp == 1
    n_heads = cache_k.shape[3]
    past = cache_k.shape[2]
    a_w = n_heads * HEAD_DIM
    pw = state_pool.shape[3]
    assert pw == len(POOL_WINDOWS) * LANES and cache_k.shape[4] == HEAD_DIM and n_heads <= 8

    ada = _ada_terms(c_prompt, c_sample, w_ada[0], b_ada)

    wit = w_in[0].T
    wp = w_pool[0]
    ps = pool_scale[0][None, :]
    wo = w_out[0]
    ng = norm_g[0][None, :]
    fg = final_g[None, :]

    bm = 1024
    xp2 = x_prompt.reshape(seq, d)
    hist_p = jnp.zeros((1, HIST_PAD, pw), F32)
    (act_p, k_p, v_p, lf_p, cq_p, nq_p, ck_p, tot, qmx, kg, ho_p) = _project(
        xp2, ada, ng, wit, b_f[0], wp, ps, hist_p,
        bm=bm, sb=ATT_BLK, segs=1, ada_row=ADA_PROMPT_ROW, start_pos=0, n_heads=n_heads, kv_head_major=False)
    y_p = _prompt_attention(tot, qmx, kg, act_p, cq_p, nq_p, ck_p, xp2, ada, wo, fg, tm=512, n_heads=n_heads)

    xs2 = x_sample.reshape(bs * ln, d)
    hist_s = jnp.pad(state_pool[0], ((0, 0), (HIST_PAD - POOL_HIST, 0), (0, 0)))
    (act_s, k_s, v_s, lf_s, cq_s, _, ck_s, _, _, _, ho_s) = _project(
        xs2, ada, ng, wit, b_f[0], wp, ps, hist_s,
        bm=bs * ln, sb=ln, segs=bs, ada_row=ADA_SAMPLE_ROW, start_pos=past, n_heads=n_heads, kv_head_major=True)
    lfc = jnp.swapaxes(cache_logf[0], 1, 2)
    ckt = jnp.transpose(cache_k[0], (0, 2, 3, 1)).reshape(bs, a_w, past)
    cvt = jnp.transpose(cache_v[0], (0, 2, 3, 1)).reshape(bs, a_w, past)
    y_s = _sample_attention(act_s, cq_s, ck_s, ckt, cvt, lfc, xs2, ada, wo, fg, ln=ln, n_heads=n_heads)

    hd = (n_heads, HEAD_DIM)
    seq_minor = lambda t: jnp.transpose(t.reshape(hd + (bp, seq)), (2, 3, 0, 1))[None]
    return (y_p.reshape(bp, seq, d), y_s.reshape(bs, ln, d),
            seq_minor(k_p), seq_minor(v_p), jnp.swapaxes(lf_p, 1, 2)[None],
            ho_p[:, 16 - POOL_HIST:, :][None],
            k_s.reshape((1, bs, ln) + hd), v_s.reshape((1, bs, ln) + hd), jnp.swapaxes(lf_s, 1, 2)[None],
            ho_s[:, 16 - POOL_HIST:, :][None])
```

```python
import functools

import jax
import jax.numpy as jnp
from jax import lax
from jax.experimental import pallas as pl
from jax.experimental.pallas import tpu as pltpu

HEAD_DIM = 64
POOL_WINDOWS = (2, 4, 8, 16)
EPS = 1e-6

LANES = 128
PAIR = LANES // HEAD_DIM
ATT_BLK = 128
NEAR_BLOCKS = 3
LOG2E = 1.4426950408889634
EXP2_UNDERFLOW = 151.0
EXP2_SAFE_SPAN = 100.0
NORM_SLACK = 1.01
HIST_PAD = 32
POOL_HIST = max(POOL_WINDOWS) - 1
NEG = -1e30
VMEM_LIMIT = 60 * 1024 * 1024
PROJ_ROWS = 8 * ATT_BLK
ATTN_ROWS = 4 * ATT_BLK

F32 = jnp.float32
BF16 = jnp.bfloat16


def _silu(x):
    return x * jax.nn.sigmoid(x)


def _dot(a, b):
    return jnp.dot(a, b, preferred_element_type=F32)


def _dot_nt(a, b):
    return lax.dot_general(a, b, (((1,), (1,)), ((), ())), preferred_element_type=F32)


def _split3(x):
    hi = x.astype(BF16).astype(F32)
    r1 = x - hi
    mid = r1.astype(BF16).astype(F32)
    return hi, mid, r1 - mid


def _rows_to_lanes(x, n):
    rows = x.shape[0]
    if rows < LANES:
        x = jnp.concatenate([x, jnp.zeros((LANES - rows, LANES), x.dtype)], axis=0)
    return x.T[0:n, 0:rows]


ACT_PARTS = 5


def _act_views(act_ref, width):
    return [act_ref.at[:, p * width:(p + 1) * width] for p in range(ACT_PARTS)]


def _resident(shape):
    return pl.BlockSpec(shape, lambda *_: (0,) * len(shape), pipeline_mode=pl.Buffered(1))


ADA_ROWS = 16
ADA_PROMPT_ROW, ADA_SAMPLE_ROW = 0, 8


def _ada_kernel(cp_ref, cs_ref, w_ref, b_ref, o_ref):
    ap = jnp.broadcast_to(_silu(cp_ref[...]), (ADA_SAMPLE_ROW, cp_ref.shape[1]))
    a = jnp.concatenate([ap, _silu(cs_ref[...])], axis=0).astype(BF16)
    o_ref[0] = _dot(a, w_ref[...].astype(BF16)) + b_ref[...]


def _ada_terms(c_prompt, c_sample, w_ada, b_ada):
    d = c_prompt.shape[1]
    assert c_prompt.shape[0] == 1 and c_sample.shape[0] == ADA_ROWS - ADA_SAMPLE_ROW and w_ada.shape[1] == 3 * d
    return pl.pallas_call(
        _ada_kernel,
        grid=(3,),
        in_specs=[pl.BlockSpec(c_prompt.shape, lambda j: (0, 0)),
                  pl.BlockSpec(c_sample.shape, lambda j: (0, 0)),
                  pl.BlockSpec((d, d), lambda j: (0, j)),
                  pl.BlockSpec((1, d), lambda j: (0, j))],
        out_specs=pl.BlockSpec((1, ADA_ROWS, d), lambda j: (j, 0, 0)),
        out_shape=jax.ShapeDtypeStruct((3, ADA_ROWS, d), F32),
        compiler_params=pltpu.CompilerParams(dimension_semantics=("arbitrary",), vmem_limit_bytes=VMEM_LIMIT),
        name="ada",
    )(c_prompt, c_sample, w_ada, b_ada)


def _proj_kernel(bf_ref, x_ref, ada_ref, ng_ref, w_ref, wp_ref, ps_ref, h0_ref,
                 act_ref, k32_ref, v32_ref, lf_ref, cq_ref, nq_ref, ck_ref,
                 tot_ref, qmx_ref, kg_ref, ho_ref,
                 e_ref, t2_ref, t4_ref, t8_ref, kmx_ref, *, bm, sb, segs, ada_row, start_pos, n_heads, kv_head_major):
    a_w = n_heads * HEAD_DIM
    q_ref, kb_ref, vb_ref, sa_ref, zp_ref = _act_views(act_ref, a_w)
    seg_rows = bm // segs
    step = pl.program_id(0)
    x = x_ref[...]
    ms = jnp.mean(x * x, axis=-1, keepdims=True)
    xn = x * lax.rsqrt(ms + EPS)
    h = jnp.concatenate(
        [xn[g * seg_rows:(g + 1) * seg_rows] * (ng_ref[...] * (1.0 + ada_ref[1, ada_row + g:ada_row + g + 1, :]))
         + ada_ref[0, ada_row + g:ada_row + g + 1, :] for g in range(segs)], axis=0).astype(BF16)

    pw = len(POOL_WINDOWS) * LANES
    sc = LOG2E / (HEAD_DIM ** 0.5)
    o_pool = 4 * a_w + n_heads

    def w_rows(lo, hi):
        return w_ref[lo:hi, :].astype(BF16)

    pu = _dot_nt(h, w_rows(o_pool, o_pool + pw))
    sel = (lax.broadcasted_iota(jnp.int32, (a_w, LANES), 0) // HEAD_DIM
           == lax.broadcasted_iota(jnp.int32, (a_w, LANES), 1)).astype(BF16)

    def store_kv(ref32, refb, p):
        refb[...] = p.astype(BF16)
        if kv_head_major:
            for hd in range(n_heads):
                ref32[:, hd, :] = p[:, hd * HEAD_DIM:(hd + 1) * HEAD_DIM]
        else:
            ref32[...] = p.T

    ext = HIST_PAD + seg_rows
    n = segs * ext

    def load_history():
        for g in range(segs):
            e_ref[g * ext:g * ext + HIST_PAD, :] = h0_ref[g]

    if segs > 1:
        load_history()
    else:
        pl.when(step == 0)(load_history)

    for g in range(segs):
        e_ref[g * ext + HIST_PAD:(g + 1) * ext, :] = pu[g * seg_rows:(g + 1) * seg_rows]
    t2_ref[8:n, :] = e_ref[8:n, :] + e_ref[7:n - 1, :]
    t4_ref[16:n, :] = t2_ref[16:n, LANES:] + t2_ref[14:n - 2, LANES:]
    t8_ref[24:n, :] = t4_ref[24:n, LANES:] + t4_ref[20:n - 4, LANES:]

    def seg_rows_of(ref, cols, back=0):
        return jnp.concatenate([ref[g * ext + HIST_PAD - back:(g + 1) * ext - back, cols] for g in range(segs)], axis=0)

    lane0, lane1 = slice(0, LANES), slice(LANES, 2 * LANES)
    sums = [seg_rows_of(t2_ref, lane0), seg_rows_of(t4_ref, lane0), seg_rows_of(t8_ref, lane0),
            seg_rows_of(t8_ref, lane1) + seg_rows_of(t8_ref, lane1, back=8)]
    row = lax.broadcasted_iota(jnp.int32, (bm, 1), 0)
    pos1 = start_pos + 1 + (step * bm + row if segs == 1 else row % seg_rows)
    pool_d = []
    for g, w in enumerate(POOL_WINDOWS):
        rc = 1.0 / jnp.minimum(pos1, w).astype(F32)
        pool_d.append((sums[g] * rc - pu[:, g * LANES:(g + 1) * LANES]).astype(BF16))
    for g in range(segs):
        ho_ref[g] = e_ref[(g + 1) * ext - 16:(g + 1) * ext, :]
    if segs == 1:
        e_ref[0:HIST_PAD, :] = e_ref[bm:n, :]

    pk = _dot_nt(h, w_rows(a_w, 2 * a_w))
    store_kv(k32_ref, kb_ref, pk)
    nk = jnp.sqrt(_dot((pk * pk).astype(BF16), sel)) * NORM_SLACK
    spg = _silu(_dot_nt(h, w_rows(o_pool + pw, o_pool + 2 * pw)))

    lane = lax.broadcasted_iota(jnp.int32, (1, LANES), 1)
    bias = jnp.zeros((1, LANES), F32)
    for hd in range(n_heads):
        bias = jnp.where(lane == hd, bf_ref[hd], bias)
    wf = jnp.concatenate([w_ref[4 * a_w:o_pool, :], jnp.zeros((LANES - n_heads, w_ref.shape[1]), F32)], axis=0)
    z = _dot_nt(h, wf.astype(BF16)) + bias
    lf = jnp.minimum(z, 0.0) - jnp.log1p(jnp.exp(-jnp.abs(z)))

    sa_ref[...] = _silu(_dot_nt(h, w_rows(3 * a_w, 4 * a_w))).astype(BF16)

    tri = (lax.broadcasted_iota(jnp.int32, (sb, sb), 1)
           <= lax.broadcasted_iota(jnp.int32, (sb, sb), 0)).astype(BF16)
    lf2 = lf * LOG2E
    tots, kmx = [], []
    for s in range(bm // sb):
        rows = slice(s * sb, (s + 1) * sb)
        parts = _dot(tri, jnp.concatenate(_split3(lf2[rows]), axis=1).astype(BF16))
        cb = parts[:, 0:LANES] + parts[:, LANES:2 * LANES] + parts[:, 2 * LANES:]
        cq_ref[rows, :] = cb[:, 0:n_heads]
        ck_ref[s] = _rows_to_lanes(cb, n_heads)
        off = (s * sb) % seg_rows
        lf_ref[(s * sb) // seg_rows, :, off:off + sb] = _rows_to_lanes(lf[rows], n_heads)
        tots.append(cb[sb - 1:sb, :])
        kmx.append(jnp.max(nk[rows], axis=0, keepdims=True))
    tot_ref[...] = jnp.concatenate(tots, axis=0)[:, 0:n_heads]
    kmax = kmx[0]
    for t in kmx[1:]:
        kmax = jnp.maximum(kmax, t)

    store_kv(v32_ref, vb_ref, _dot_nt(h, w_rows(2 * a_w, 3 * a_w)))

    zero_w = jnp.zeros((LANES, LANES), BF16)
    for g in range(0, len(POOL_WINDOWS), 2):
        cols = slice(g * LANES, (g + 2) * LANES)
        w2 = jnp.concatenate([jnp.concatenate([wp_ref[g].astype(BF16), zero_w], axis=1),
                              jnp.concatenate([zero_w, wp_ref[g + 1].astype(BF16)], axis=1)], axis=0)
        y = _dot(jnp.concatenate([pool_d[g], pool_d[g + 1]], axis=1), w2) * ps_ref[:, cols]
        zp_ref[:, cols] = (y * spg[:, cols]).astype(BF16)

    qs = _dot_nt(h, w_rows(0, a_w)) * sc
    q_ref[...] = qs.astype(BF16)
    nq = jnp.sqrt(_dot((qs * qs).astype(BF16), sel)) * NORM_SLACK
    nq_ref[...] = nq[:, 0:n_heads]
    qmx_ref[...] = jnp.concatenate([jnp.max(nq[s * sb:(s + 1) * sb], axis=0, keepdims=True)
                                    for s in range(bm // sb)], axis=0)[:, 0:n_heads]

    @pl.when(step > 0)
    def _():
        kmx_ref[...] = jnp.maximum(kmx_ref[...], kmax)

    @pl.when(step == 0)
    def _():
        kmx_ref[...] = kmax

    kg_ref[...] = kmx_ref[...]


def _project(x2, ada, norm_g, wit, b_f, wp, ps, hist0, *, bm, sb, segs, ada_row, start_pos, n_heads,
             kv_head_major):
    rows, d = x2.shape
    a_w = n_heads * HEAD_DIM
    pw = len(POOL_WINDOWS) * LANES
    n_steps = rows // bm
    assert segs == 1 or n_steps == 1
    n_streams = segs
    seg_rows = bm // segs
    nsb = bm // sb
    assert nsb == 8 and seg_rows % sb == 0
    row_blk = lambda w: pl.BlockSpec((bm, w), lambda i, *_: (i, 0))
    per_stream = lambda r, w: pl.BlockSpec((segs, r, w), lambda i, *_: (0, 0, 0))
    kern = functools.partial(_proj_kernel, bm=bm, sb=sb, segs=segs, ada_row=ada_row, start_pos=start_pos,
                             n_heads=n_heads, kv_head_major=kv_head_major)
    if kv_head_major:
        kv_shape = (rows, n_heads, HEAD_DIM)
        kv_blk = pl.BlockSpec((bm, n_heads, HEAD_DIM), lambda i, *_: (i, 0, 0))
    else:
        kv_shape = (a_w, rows)
        kv_blk = pl.BlockSpec((a_w, bm), lambda i, *_: (0, i))
    assert pw == a_w
    out_shape = (
        jax.ShapeDtypeStruct((rows, ACT_PARTS * a_w), BF16),
        jax.ShapeDtypeStruct(kv_shape, F32),
        jax.ShapeDtypeStruct(kv_shape, F32),
        jax.ShapeDtypeStruct((n_streams, n_heads, rows // n_streams), F32),
        jax.ShapeDtypeStruct((rows, n_heads), F32),
        jax.ShapeDtypeStruct((rows, n_heads), F32),
        jax.ShapeDtypeStruct((rows // sb, n_heads, sb), F32),
        jax.ShapeDtypeStruct((rows // sb, n_heads), F32),
        jax.ShapeDtypeStruct((rows // sb, n_heads), F32),
        jax.ShapeDtypeStruct((1, LANES), F32),
        jax.ShapeDtypeStruct((n_streams, 16, pw), F32),
    )
    out_specs = (
        row_blk(ACT_PARTS * a_w), kv_blk, kv_blk,
        pl.BlockSpec((segs, n_heads, seg_rows), lambda i, *_: (0, 0, i)),
        row_blk(n_heads), row_blk(n_heads),
        pl.BlockSpec((nsb, n_heads, sb), lambda i, *_: (i, 0, 0)),
        pl.BlockSpec((nsb, n_heads), lambda i, *_: (i, 0)),
        pl.BlockSpec((nsb, n_heads), lambda i, *_: (i, 0)),
        pl.BlockSpec((1, LANES), lambda i, *_: (0, 0)),
        per_stream(16, pw),
    )
    in_specs = [
        row_blk(d),
        _resident(ada.shape),
        _resident((1, d)),
        _resident(wit.shape), _resident(wp.shape), _resident(ps.shape),
        per_stream(HIST_PAD, pw),
    ]
    return pl.pallas_call(
        kern,
        grid_spec=pltpu.PrefetchScalarGridSpec(
            num_scalar_prefetch=1,
            grid=(n_steps,),
            in_specs=in_specs,
            out_specs=out_specs,
            scratch_shapes=[pltpu.VMEM((bm + segs * HIST_PAD, pw), F32),
                            pltpu.VMEM((bm + segs * HIST_PAD, pw), F32),
                            pltpu.VMEM((bm + segs * HIST_PAD, pw - LANES), F32),
                            pltpu.VMEM((bm + segs * HIST_PAD, pw - 2 * LANES), F32),
                            pltpu.VMEM((1, LANES), F32)]),
        out_shape=out_shape,
        compiler_params=pltpu.CompilerParams(dimension_semantics=("arbitrary",), vmem_limit_bytes=VMEM_LIMIT),
        name="proj",
    )(b_f, x2, ada, norm_g, wit, wp, ps, hist0)


def _merge_norm(x, gate, za, zp, wo_ref, fg, a_w):
    dy = _dot(za, wo_ref[0:a_w, :].astype(BF16)) + _dot(zp, wo_ref[a_w:, :].astype(BF16))
    out = x + gate * dy
    ms = jnp.mean(out * out, axis=-1, keepdims=True)
    return out * lax.rsqrt(ms + EPS) * fg


def _attn_kernel(tot_ref, qmx_ref, kg_ref,
                 act_ref, cq_ref, nq_ref, ck_ref, x_ref, ada_ref, wo_ref, fg_ref,
                 y_ref,
                 k_ref, v_ref, z_ref, m_ref, l_ref, acc_ref, straight_ref, *, tm, n_heads):
    a_w = n_heads * HEAD_DIM
    q_ref, knew_ref, vnew_ref, sa_ref, zp_ref = _act_views(act_ref, a_w)
    n_pairs = n_heads // PAIR
    nsub = tm // ATT_BLK
    w_near = NEAR_BLOCKS
    cw = w_near * ATT_BLK
    step = pl.program_id(0)

    def keep_step_rows():
        k_ref[pl.ds(pl.multiple_of(step * tm, tm), tm), :] = knew_ref[...]
        v_ref[pl.ds(pl.multiple_of(step * tm, tm), tm), :] = vnew_ref[...]

    lo_q = lax.broadcasted_iota(jnp.int32, (ATT_BLK, LANES), 1) < HEAD_DIM
    lo_k = lax.broadcasted_iota(jnp.int32, (cw, LANES), 1) < HEAD_DIM
    col = lax.broadcasted_iota(jnp.int32, (ATT_BLK, cw), 1)
    tri = (lax.broadcasted_iota(jnp.int32, (ATT_BLK, ATT_BLK), 1)
           <= lax.broadcasted_iota(jnp.int32, (ATT_BLK, ATT_BLK), 0))
    tri_bias = jnp.where(tri, 0.0, NEG).astype(F32)
    zeros_k = jnp.zeros((cw, LANES), BF16)
    ind_lo = jnp.where(lo_k, 1.0, 0.0).astype(BF16)
    ind_hi = jnp.where(lo_k, 0.0, 1.0).astype(BF16)

    def tot_at(b, hd):
        return jnp.where(b >= 0, tot_ref[jnp.maximum(b, 0), hd], 0.0)

    def rows_of(ref, blocks, g):
        return jnp.concatenate(
            [ref[pl.ds(pl.multiple_of(b * ATT_BLK, ATT_BLK), ATT_BLK), g * LANES:(g + 1) * LANES] for b in blocks],
            axis=0)

    def step_rows_of(ref, new_ref, sub, g):
        parts = []
        for p in range(w_near):
            rel = sub - (w_near - 1) + p
            if rel >= 0:
                parts.append(new_ref[rel * ATT_BLK:(rel + 1) * ATT_BLK, g * LANES:(g + 1) * LANES])
            else:
                start = pl.multiple_of((step * nsub + rel) * ATT_BLK, ATT_BLK)
                parts.append(ref[pl.ds(start, ATT_BLK), g * LANES:(g + 1) * LANES])
        return jnp.concatenate(parts, axis=0)

    def pair_scores(r0, kc, g):
        keys = jnp.concatenate([jnp.where(lo_k, kc, zeros_k), jnp.where(lo_k, zeros_k, kc)], axis=0)
        return _dot_nt(q_ref[pl.ds(r0, ATT_BLK), g * LANES:(g + 1) * LANES], keys)

    def pair_values(p_pair, vc):
        vals = jnp.concatenate([jnp.concatenate([jnp.where(lo_k, vc, zeros_k), ind_lo], axis=1),
                                jnp.concatenate([jnp.where(lo_k, zeros_k, vc), ind_hi], axis=1)], axis=0)
        return _dot(p_pair, vals)

    def decay_row(blocks, offs, hd):
        return jnp.concatenate([offs[p] - ck_ref[blocks[p], hd:hd + 1, :] for p in range(w_near)], axis=1)

    def near_weights(s_pair, rows, g, near_c, near_offs, bounded):
        ps, ms = [], []
        for e in range(PAIR):
            hd = PAIR * g + e
            cqh = cq_ref[rows, hd:hd + 1]
            if bounded:
                m = nq_ref[rows, hd:hd + 1] * kg_ref[0, hd]
                cqh = cqh - m
            dec = decay_row(near_c, near_offs[hd], hd)
            pieces = []
            for p in range(w_near):
                lanes = slice(e * cw + p * ATT_BLK, e * cw + (p + 1) * ATT_BLK)
                sp = s_pair[:, lanes] + cqh + dec[:, p * ATT_BLK:(p + 1) * ATT_BLK]
                pieces.append(sp + tri_bias if p == w_near - 1 else sp)
            s = jnp.concatenate(pieces, axis=1)
            if not bounded:
                m = jnp.max(s, axis=1, keepdims=True)
                s = s - m
            ms.append(m)
            ps.append(jnp.exp2(s).astype(BF16))
        return jnp.concatenate(ps, axis=1), ms

    def gated_output(rows, g, acc, l):
        cols = slice(g * LANES, (g + 1) * LANES)
        z_ref[rows, cols] = ((acc / l) * sa_ref[rows, cols].astype(F32)).astype(BF16)

    def merge(chunks=1):
        rows_c = tm // chunks
        for c in range(chunks):
            rows = slice(c * rows_c, (c + 1) * rows_c)
            y_ref[rows, :] = _merge_norm(x_ref[rows, :], ada_ref[2, ADA_PROMPT_ROW:ADA_PROMPT_ROW + 1, :],
                                         z_ref[rows, :], zp_ref[rows, :], wo_ref, fg_ref[...], a_w)

    def check_next_step():
        ok = jnp.bool_(True)
        last = qmx_ref.shape[0] - 1
        for sub in range(nsub):
            i = jnp.minimum((step + 1) * nsub + sub, last)
            for hd in range(n_heads):
                qk = 2.0 * qmx_ref[i, hd] * kg_ref[0, hd]
                back = qk
                for dd in range(1, w_near):
                    back = back + tot_ref[i - dd, hd]
                ok = jnp.logical_and(ok, jnp.logical_and(qk <= EXP2_SAFE_SPAN, back < -EXP2_UNDERFLOW))
        straight_ref[0] = ok.astype(jnp.int32)

    def straight_step():
        units = [(sub, g) for sub in range(nsub) for g in range(n_pairs)]
        near_of, offs_of = [], []
        for sub in range(nsub):
            i = step * nsub + sub
            near_of.append([i - (w_near - 1) + p for p in range(w_near)])
            offs = []
            for hd in range(n_heads):
                o, per_piece = jnp.float32(0.0), [jnp.float32(0.0)]
                for dd in range(1, w_near):
                    o = o + tot_ref[i - dd, hd]
                    per_piece.append(o)
                offs.append(per_piece[::-1])
            offs_of.append(offs)
        s_next = pair_scores(0, step_rows_of(k_ref, knew_ref, 0, 0), 0)
        for u, (sub, g) in enumerate(units):
            s_pair = s_next
            if u + 1 < len(units):
                sub1, g1 = units[u + 1]
                s_next = pair_scores(sub1 * ATT_BLK, step_rows_of(k_ref, knew_ref, sub1, g1), g1)
            rows = pl.ds(sub * ATT_BLK, ATT_BLK)
            p_pair, _ = near_weights(s_pair, rows, g, near_of[sub], offs_of[sub], True)
            r = pair_values(p_pair, step_rows_of(v_ref, vnew_ref, sub, g))
            gated_output(rows, g, r[:, 0:LANES], r[:, LANES:])
        check_next_step()
        keep_step_rows()
        merge(chunks=2)

    def sub_body(sub, carry):
        i = step * nsub + sub
        r0 = pl.multiple_of(sub * ATT_BLK, ATT_BLK)

        rows = pl.ds(r0, ATT_BLK)
        qk = [2.0 * qmx_ref[i, hd] * kg_ref[0, hd] for hd in range(n_heads)]

        near = [i - (w_near - 1) + p for p in range(w_near)]
        near_c = [jnp.maximum(b, 0) for b in near]
        near_offs, offs_far = [], []
        for hd in range(n_heads):
            back = [tot_at(i - dd, hd) for dd in range(1, w_near)]
            offs = []
            for p in range(w_near):
                o = jnp.float32(0.0)
                for dd in range(1, w_near - p):
                    o = o + back[dd - 1]
                offs.append(jnp.where(near[p] >= 0, o, NEG))
            near_offs.append(offs)
            o = jnp.float32(0.0)
            for t in back:
                o = o + t
            offs_far.append(o)

        def near_chunk(bounded):
            def fn():
                s_next = pair_scores(r0, rows_of(k_ref, near_c, 0), 0)
                for g in range(n_pairs):
                    s_pair = s_next
                    if g + 1 < n_pairs:
                        s_next = pair_scores(r0, rows_of(k_ref, near_c, g + 1), g + 1)
                    p_pair, ms = near_weights(s_pair, rows, g, near_c, near_offs, bounded)
                    for e in range(PAIR):
                        m_ref[PAIR * g + e] = ms[e]
                    r = pair_values(p_pair, rows_of(v_ref, near_c, g))
                    acc_ref[g] = r[:, 0:LANES]
                    l_ref[g] = r[:, LANES:]
                    gated_output(rows, g, r[:, 0:LANES], r[:, LANES:])
            return fn

        bounded_ok = qk[0] <= EXP2_SAFE_SPAN
        for hd in range(1, n_heads):
            bounded_ok = jnp.logical_and(bounded_ok, qk[hd] <= EXP2_SAFE_SPAN)
        pl.when(bounded_ok)(near_chunk(True))
        pl.when(jnp.logical_not(bounded_ok))(near_chunk(False))

        def far_cond(c):
            top = i - c[0] * w_near
            need = qk[0] + c[1] >= -EXP2_UNDERFLOW
            for hd in range(1, n_heads):
                need = jnp.logical_or(need, qk[hd] + c[1 + hd] >= -EXP2_UNDERFLOW)
            return jnp.logical_and(top >= 0, need)

        def far_body(c):
            top = i - c[0] * w_near
            jc = jnp.maximum(top - (w_near - 1), 0)
            blocks = [jc + p for p in range(w_near)]
            keepc = col < (top + 1 - jc) * ATT_BLK
            new = [c[0] + 1]
            for g in range(n_pairs):
                s_pair = pair_scores(r0, rows_of(k_ref, blocks, g), g)
                ps, alphas = [], []
                for e in range(PAIR):
                    hd = PAIR * g + e
                    tt = [tot_at(top - b, hd) for b in range(w_near)]
                    offs = []
                    for p in range(w_near):
                        behind = top - (jc + p)
                        o = c[1 + hd]
                        for b in range(w_near):
                            o = o + jnp.where(behind >= b, tt[b], 0.0)
                        offs.append(o)
                    s = s_pair[:, e * cw:(e + 1) * cw] + cq_ref[rows, hd:hd + 1] + decay_row(blocks, offs, hd)
                    s = jnp.where(keepc, s, NEG)
                    m_old = m_ref[hd]
                    m_new = jnp.maximum(m_old, jnp.max(s, axis=1, keepdims=True))
                    m_ref[hd] = m_new
                    alphas.append(jnp.broadcast_to(jnp.exp2(m_old - m_new), (ATT_BLK, LANES)))
                    ps.append(jnp.exp2(s - m_new).astype(BF16))
                    o = c[1 + hd]
                    for t in tt:
                        o = o + t
                    new.append(o)
                alpha = jnp.where(lo_q, alphas[0], alphas[1])
                r = pair_values(jnp.concatenate(ps, axis=1), rows_of(v_ref, blocks, g))
                acc_ref[g] = alpha * acc_ref[g] + r[:, 0:LANES]
                l_ref[g] = alpha * l_ref[g] + r[:, LANES:]
            return tuple(new)

        far = lax.while_loop(far_cond, far_body, (jnp.int32(1),) + tuple(offs_far))

        @pl.when(far[0] > 1)
        def _():
            for g in range(n_pairs):
                gated_output(rows, g, acc_ref[g], l_ref[g])
        return carry

    @pl.when(step == 0)
    def _():
        straight_ref[0] = 0

    straight = straight_ref[0] == 1
    pl.when(straight)(straight_step)

    @pl.when(jnp.logical_not(straight))
    def _():
        keep_step_rows()
        check_next_step()
        lax.fori_loop(0, nsub, sub_body, 0)
        merge()


def _prompt_attention(tot, qmx, kg, act, cq, nq, ck, x2, ada, wo, fg, *, tm, n_heads):
    rows, d = x2.shape
    a_w = n_heads * HEAD_DIM
    n_pairs = n_heads // PAIR
    assert rows % tm == 0 and tm % ATT_BLK == 0 and tm // ATT_BLK >= NEAR_BLOCKS - 1 and n_heads % PAIR == 0
    row_blk = lambda w: pl.BlockSpec((tm, w), lambda i, *_: (i, 0))
    grid_spec = pltpu.PrefetchScalarGridSpec(
        num_scalar_prefetch=3,
        grid=(rows // tm,),
        in_specs=[row_blk(ACT_PARTS * a_w), row_blk(n_heads), row_blk(n_heads), _resident(ck.shape), row_blk(d),
                  _resident(ada.shape), _resident(wo.shape), _resident((1, d))],
        out_specs=row_blk(d),
        scratch_shapes=[pltpu.VMEM((rows, a_w), BF16),
                        pltpu.VMEM((rows, a_w), BF16),
                        pltpu.VMEM((tm, a_w), BF16),
                        pltpu.VMEM((n_heads, ATT_BLK, 1), F32),
                        pltpu.VMEM((n_pairs, ATT_BLK, LANES), F32),
                        pltpu.VMEM((n_pairs, ATT_BLK, LANES), F32),
                        pltpu.SMEM((1,), jnp.int32)],
    )
    return pl.pallas_call(
        functools.partial(_attn_kernel, tm=tm, n_heads=n_heads),
        grid_spec=grid_spec,
        out_shape=jax.ShapeDtypeStruct((rows, d), F32),
        compiler_params=pltpu.CompilerParams(dimension_semantics=("arbitrary",), vmem_limit_bytes=VMEM_LIMIT),
        name="attn",
    )(tot, qmx, kg, act, cq, nq, ck, x2, ada, wo, fg)


CACHE_SPLIT = 2


def _sattn_kernel(act_ref, cq_ref, ckn_ref, kc0_ref, kc1_ref, vc0_ref, vc1_ref, lfc_ref,
                  x_ref, ada_ref, wo_ref, fg_ref, y_ref, *, n_heads):
    a_w = n_heads * HEAD_DIM
    q_ref, kn_ref, vn_ref, sa_ref, zp_ref = _act_views(act_ref, a_w)
    ln = act_ref.shape[0]
    past = kc0_ref.shape[2]
    kc_refs, vc_refs = (kc0_ref, kc1_ref), (vc0_ref, vc1_ref)
    pairs_per_part = n_heads // PAIR // CACHE_SPLIT

    def cached(refs, g):
        lo = (g % pairs_per_part) * LANES
        return refs[g // pairs_per_part][0, lo:lo + LANES, :].astype(BF16)

    nb = past // LANES

    lfc = lfc_ref[0] * LOG2E
    triu = (lax.broadcasted_iota(jnp.int32, (LANES, LANES), 0)
            <= lax.broadcasted_iota(jnp.int32, (LANES, LANES), 1)).astype(BF16)
    zeros = jnp.zeros((8, LANES), F32)
    parts = []
    for b in range(nb):
        parts.extend(_split3(lfc[:, b * LANES:(b + 1) * LANES]))
        parts.append(zeros)
    cs = _dot(jnp.concatenate(parts, axis=0).astype(BF16), triu)
    after = jnp.zeros((n_heads, 1), F32)
    suffix = [None] * nb
    for b in reversed(range(nb)):
        cb = cs[32 * b:32 * b + 8] + cs[32 * b + 8:32 * b + 16] + cs[32 * b + 16:32 * b + 24]
        tot = cb[:, LANES - 1:LANES]
        suffix[b] = (tot - cb) + after
        after = after + tot
    dec_c = jnp.concatenate(suffix, axis=1)

    lane = lax.broadcasted_iota(jnp.int32, (ln, LANES), 1)
    half = [lane < HEAD_DIM, lane >= HEAD_DIM]
    causal = lax.broadcasted_iota(jnp.int32, (ln, ln), 1) <= lax.broadcasted_iota(jnp.int32, (ln, ln), 0)
    n_pairs = n_heads // PAIR

    def pair_scores(g):
        cols = slice(g * LANES, (g + 1) * LANES)
        q2 = q_ref[:, cols]
        qst = jnp.concatenate([jnp.where(half[e], q2, jnp.zeros_like(q2)) for e in range(PAIR)], axis=0)
        return _dot(qst, cached(kc_refs, g)), _dot_nt(qst, kn_ref[:, cols])

    outs = []
    s_next = pair_scores(0)
    for g in range(n_pairs):
        cols = slice(g * LANES, (g + 1) * LANES)
        sc_st, sn_st = s_next
        if g + 1 < n_pairs:
            s_next = pair_scores(g + 1)
        vct = cached(vc_refs, g)
        vn = vn_ref[:, cols]
        pc, pn, ls = [], [], []
        for e in range(PAIR):
            hd = PAIR * g + e
            rows = slice(e * ln, (e + 1) * ln)
            cqh = cq_ref[:, hd:hd + 1]
            s_c = sc_st[rows] + cqh + dec_c[hd:hd + 1, :]
            s_n = jnp.where(causal, sn_st[rows] + cqh - ckn_ref[0, hd:hd + 1, :], NEG)
            m = jnp.maximum(jnp.max(s_c, axis=1, keepdims=True), jnp.max(s_n, axis=1, keepdims=True))
            p_c = jnp.exp2(s_c - m)
            p_n = jnp.exp2(s_n - m)
            ls.append(jnp.sum(p_c, axis=1, keepdims=True) + jnp.sum(p_n, axis=1, keepdims=True))
            pc.append(p_c.astype(BF16))
            pn.append(p_n.astype(BF16))
        acc = _dot_nt(jnp.concatenate(pc, axis=0), vct) + _dot(jnp.concatenate(pn, axis=0), vn)
        o = jnp.where(half[0], acc[0:ln] / ls[0], acc[ln:2 * ln] / ls[1])
        outs.append((o * sa_ref[:, cols].astype(F32)).astype(BF16))
    za = jnp.concatenate(outs, axis=1)
    y_ref[...] = _merge_norm(x_ref[...], ada_ref[2, pl.ds(ADA_SAMPLE_ROW + pl.program_id(0), 1), :], za, zp_ref[...], wo_ref, fg_ref[...], a_w)


def _sample_attention(act, cq, ckn, cache_k, cache_v, lfc, x2, ada, wo, fg, *, ln, n_heads):
    rows, d = x2.shape
    nbatch = rows // ln
    a_w = n_heads * HEAD_DIM
    past = cache_k.shape[2]
    row_blk = lambda w: pl.BlockSpec((ln, w), lambda b: (b, 0))
    per_b = lambda s: pl.BlockSpec((1,) + s, lambda b: (b, 0, 0))
    cache_part = lambda c: pl.BlockSpec((1, a_w // CACHE_SPLIT, past), lambda b: (b, c, 0))
    assert CACHE_SPLIT == 2 and (n_heads // PAIR) % CACHE_SPLIT == 0
    return pl.pallas_call(
        functools.partial(_sattn_kernel, n_heads=n_heads),
        grid=(nbatch,),
        in_specs=[row_blk(ACT_PARTS * a_w), row_blk(n_heads), per_b((n_heads, ln)),
                  cache_part(0), cache_part(1), cache_part(0), cache_part(1), per_b((n_heads, past)),
                  row_blk(d), _resident(ada.shape), _resident(wo.shape), _resident((1, d))],
        out_specs=row_blk(d),
        out_shape=jax.ShapeDtypeStruct((rows, d), F32),
        compiler_params=pltpu.CompilerParams(dimension_semantics=("arbitrary",), vmem_limit_bytes=VMEM_LIMIT),
        name="sattn",
    )(act, cq, ckn, cache_k, cache_k, cache_v, cache_v, lfc, x2, ada, wo, fg)


def kernel(x_prompt, x_sample, c_prompt, c_sample, cache_k, cache_v, cache_logf, state_pool, norm_g, w_ada, b_ada,
           w_in, b_f, w_pool, pool_scale, w_out, final_g):
    depth = norm_g.shape[0]
    assert depth == 1
    bp, seq, d = x_prompt.shape
    bs, ln, _ = x_sample.shape
    assert bp == 1
    n_heads = cache_k.shape[3]
    past = cache_k.shape[2]
    a_w = n_heads * HEAD_DIM
    pw = state_pool.shape[3]
    assert pw == len(POOL_WINDOWS) * LANES and cache_k.shape[4] == HEAD_DIM and n_heads <= 8

    ada = _ada_terms(c_prompt, c_sample, w_ada[0], b_ada)

    wit = w_in[0].T
    wp = w_pool[0]
    ps = pool_scale[0][None, :]
    wo = w_out[0]
    ng = norm_g[0][None, :]
    fg = final_g[None, :]

    assert seq % PROJ_ROWS == 0 and seq % ATTN_ROWS == 0
    xp2 = x_prompt.reshape(seq, d)
    hist_p = jnp.zeros((1, HIST_PAD, pw), F32)
    (act_p, k_p, v_p, lf_p, cq_p, nq_p, ck_p, tot, qmx, kg, ho_p) = _project(
        xp2, ada, ng, wit, b_f[0], wp, ps, hist_p,
        bm=PROJ_ROWS, sb=ATT_BLK, segs=1, ada_row=ADA_PROMPT_ROW, start_pos=0, n_heads=n_heads, kv_head_major=False)
    y_p = _prompt_attention(tot, qmx, kg, act_p, cq_p, nq_p, ck_p, xp2, ada, wo, fg, tm=ATTN_ROWS, n_heads=n_heads)

    xs2 = x_sample.reshape(bs * ln, d)
    hist_s = jnp.pad(state_pool[0], ((0, 0), (HIST_PAD - POOL_HIST, 0), (0, 0)))
    (act_s, k_s, v_s, lf_s, cq_s, _, ck_s, _, _, _, ho_s) = _project(
        xs2, ada, ng, wit, b_f[0], wp, ps, hist_s,
        bm=bs * ln, sb=ln, segs=bs, ada_row=ADA_SAMPLE_ROW, start_pos=past, n_heads=n_heads, kv_head_major=True)
    lfc = jnp.swapaxes(cache_logf[0], 1, 2)
    ckt = jnp.transpose(cache_k[0], (0, 2, 3, 1)).reshape(bs, a_w, past)
    cvt = jnp.transpose(cache_v[0], (0, 2, 3, 1)).reshape(bs, a_w, past)
    y_s = _sample_attention(act_s, cq_s, ck_s, ckt, cvt, lfc, xs2, ada, wo, fg, ln=ln, n_heads=n_heads)

    hd = (n_heads, HEAD_DIM)
    seq_minor = lambda t: jnp.transpose(t.reshape(hd + (bp, seq)), (2, 3, 0, 1))[None]
    return (y_p.reshape(bp, seq, d), y_s.reshape(bs, ln, d),
            seq_minor(k_p), seq_minor(v_p), jnp.swapaxes(lf_p, 1, 2)[None],
            ho_p[:, 16 - POOL_HIST:, :][None],
            k_s.reshape((1, bs, ln) + hd), v_s.reshape((1, bs, ln) + hd), jnp.swapaxes(lf_s, 1, 2)[None],
            ho_s[:, 16 - POOL_HIST:, :][None])
```

```python
import functools

import jax
import jax.numpy as jnp
from jax import lax
from jax.experimental import pallas as pl
from jax.experimental.pallas import tpu as pltpu

HEAD_DIM = 64
POOL_WINDOWS = (2, 4, 8, 16)
EPS = 1e-6

LANES = 128
PAIR = LANES // HEAD_DIM
ATT_BLK = 128
NEAR_BLOCKS = 3
LOG2E = 1.4426950408889634
EXP2_UNDERFLOW = 151.0
EXP2_SAFE_SPAN = 100.0
NORM_SLACK = 1.01
HIST_PAD = 32
POOL_HIST = max(POOL_WINDOWS) - 1
NEG = -1e30
VMEM_LIMIT = 60 * 1024 * 1024
PROJ_ROWS = 8 * ATT_BLK
ATTN_ROWS = 4 * ATT_BLK

F32 = jnp.float32
BF16 = jnp.bfloat16


def _silu(x):
    return x * jax.nn.sigmoid(x)


def _dot(a, b):
    return jnp.dot(a, b, preferred_element_type=F32)


def _dot_nt(a, b):
    return lax.dot_general(a, b, (((1,), (1,)), ((), ())), preferred_element_type=F32)


def _split3(x):
    hi = x.astype(BF16).astype(F32)
    r1 = x - hi
    mid = r1.astype(BF16).astype(F32)
    return hi, mid, r1 - mid


def _rows_to_lanes(x, n):
    rows = x.shape[0]
    if rows < LANES:
        x = jnp.concatenate([x, jnp.zeros((LANES - rows, LANES), x.dtype)], axis=0)
    return x.T[0:n, 0:rows]


ACT_PARTS = 5


def _act_views(act_ref, width):
    return [act_ref.at[:, p * width:(p + 1) * width] for p in range(ACT_PARTS)]


def _resident(shape):
    return pl.BlockSpec(shape, lambda *_: (0,) * len(shape), pipeline_mode=pl.Buffered(1))


ADA_ROWS = 16
ADA_PROMPT_ROW, ADA_SAMPLE_ROW = 0, 8


def _ada_kernel(cp_ref, cs_ref, w_ref, b_ref, o_ref):
    ap = jnp.broadcast_to(_silu(cp_ref[...]), (ADA_SAMPLE_ROW, cp_ref.shape[1]))
    a = jnp.concatenate([ap, _silu(cs_ref[...])], axis=0).astype(BF16)
    o_ref[0] = _dot(a, w_ref[...].astype(BF16)) + b_ref[...]


def _ada_terms(c_prompt, c_sample, w_ada, b_ada):
    d = c_prompt.shape[1]
    assert c_prompt.shape[0] == 1 and c_sample.shape[0] == ADA_ROWS - ADA_SAMPLE_ROW and w_ada.shape[1] == 3 * d
    return pl.pallas_call(
        _ada_kernel,
        grid=(3,),
        in_specs=[pl.BlockSpec(c_prompt.shape, lambda j: (0, 0)),
                  pl.BlockSpec(c_sample.shape, lambda j: (0, 0)),
                  pl.BlockSpec((d, d), lambda j: (0, j)),
                  pl.BlockSpec((1, d), lambda j: (0, j))],
        out_specs=pl.BlockSpec((1, ADA_ROWS, d), lambda j: (j, 0, 0)),
        out_shape=jax.ShapeDtypeStruct((3, ADA_ROWS, d), F32),
        compiler_params=pltpu.CompilerParams(dimension_semantics=("arbitrary",), vmem_limit_bytes=VMEM_LIMIT),
        name="ada",
    )(c_prompt, c_sample, w_ada, b_ada)


def _proj_kernel(bf_ref, x_ref, ada_ref, ng_ref, w_ref, wp_ref, ps_ref, h0_ref,
                 act_ref, k32_ref, v32_ref, lf_ref, cq_ref, nq_ref, ck_ref,
                 tot_ref, qmx_ref, kg_ref, ho_ref,
                 e_ref, t2_ref, t4_ref, t8_ref, kmx_ref, *, bm, sb, segs, ada_row, start_pos, n_heads, kv_head_major):
    a_w = n_heads * HEAD_DIM
    q_ref, kb_ref, vb_ref, sa_ref, zp_ref = _act_views(act_ref, a_w)
    seg_rows = bm // segs
    step = pl.program_id(0)
    pw = len(POOL_WINDOWS) * LANES
    sc = LOG2E / (HEAD_DIM ** 0.5)
    o_pool = 4 * a_w + n_heads

    def w_rows(lo, hi):
        return w_ref[lo:hi, :].astype(BF16)

    def normed(lo, hi):
        x = x_ref[lo:hi, :]
        xn = x * lax.rsqrt(jnp.mean(x * x, axis=-1, keepdims=True) + EPS)
        parts = []
        for g in range(lo // seg_rows, (hi - 1) // seg_rows + 1):
            r0, r1 = max(lo, g * seg_rows) - lo, min(hi, (g + 1) * seg_rows) - lo
            parts.append(xn[r0:r1] * (ng_ref[...] * (1.0 + ada_ref[1, ada_row + g:ada_row + g + 1, :]))
                         + ada_ref[0, ada_row + g:ada_row + g + 1, :])
        return jnp.concatenate(parts, axis=0).astype(BF16)

    n_lead = 4
    hs, pus = [], []
    for c in range(n_lead):
        hs.append(normed(c * bm // n_lead, (c + 1) * bm // n_lead))
        pus.append(_dot_nt(hs[-1], w_rows(o_pool, o_pool + pw)))
    pu = jnp.concatenate(pus, axis=0)
    h = jnp.concatenate(hs, axis=0)
    sel = (lax.broadcasted_iota(jnp.int32, (a_w, LANES), 0) // HEAD_DIM
           == lax.broadcasted_iota(jnp.int32, (a_w, LANES), 1)).astype(BF16)

    def store_kv(ref32, refb, p):
        refb[...] = p.astype(BF16)
        if kv_head_major:
            for hd in range(n_heads):
                ref32[:, hd, :] = p[:, hd * HEAD_DIM:(hd + 1) * HEAD_DIM]
        else:
            ref32[...] = p.T

    ext = HIST_PAD + seg_rows
    n = segs * ext

    def load_history():
        for g in range(segs):
            e_ref[g * ext:g * ext + HIST_PAD, :] = h0_ref[g]

    if segs > 1:
        load_history()
    else:
        pl.when(step == 0)(load_history)

    for g in range(segs):
        e_ref[g * ext + HIST_PAD:(g + 1) * ext, :] = pu[g * seg_rows:(g + 1) * seg_rows]
    t2_ref[8:n, :] = e_ref[8:n, :] + e_ref[7:n - 1, :]
    t4_ref[16:n, :] = t2_ref[16:n, LANES:] + t2_ref[14:n - 2, LANES:]
    t8_ref[24:n, :] = t4_ref[24:n, LANES:] + t4_ref[20:n - 4, LANES:]

    def seg_rows_of(ref, cols, back=0):
        return jnp.concatenate([ref[g * ext + HIST_PAD - back:(g + 1) * ext - back, cols] for g in range(segs)], axis=0)

    lane0, lane1 = slice(0, LANES), slice(LANES, 2 * LANES)
    sums = [seg_rows_of(t2_ref, lane0), seg_rows_of(t4_ref, lane0), seg_rows_of(t8_ref, lane0),
            seg_rows_of(t8_ref, lane1) + seg_rows_of(t8_ref, lane1, back=8)]
    row = lax.broadcasted_iota(jnp.int32, (bm, 1), 0)
    pos1 = start_pos + 1 + (step * bm + row if segs == 1 else row % seg_rows)
    pool_d = []
    for g, w in enumerate(POOL_WINDOWS):
        rc = 1.0 / jnp.minimum(pos1, w).astype(F32)
        pool_d.append((sums[g] * rc - pu[:, g * LANES:(g + 1) * LANES]).astype(BF16))
    for g in range(segs):
        ho_ref[g] = e_ref[(g + 1) * ext - 16:(g + 1) * ext, :]
    if segs == 1:
        e_ref[0:HIST_PAD, :] = e_ref[bm:n, :]

    pk = _dot_nt(h, w_rows(a_w, 2 * a_w))
    store_kv(k32_ref, kb_ref, pk)
    nk = jnp.sqrt(_dot((pk * pk).astype(BF16), sel)) * NORM_SLACK
    spg = _silu(_dot_nt(h, w_rows(o_pool + pw, o_pool + 2 * pw)))

    lane = lax.broadcasted_iota(jnp.int32, (1, LANES), 1)
    bias = jnp.zeros((1, LANES), F32)
    for hd in range(n_heads):
        bias = jnp.where(lane == hd, bf_ref[hd], bias)
    wf = jnp.concatenate([w_ref[4 * a_w:o_pool, :], jnp.zeros((LANES - n_heads, w_ref.shape[1]), F32)], axis=0)
    z = _dot_nt(h, wf.astype(BF16)) + bias
    lf = jnp.minimum(z, 0.0) - jnp.log1p(jnp.exp(-jnp.abs(z)))

    sa_ref[...] = _silu(_dot_nt(h, w_rows(3 * a_w, 4 * a_w))).astype(BF16)

    tri = (lax.broadcasted_iota(jnp.int32, (sb, sb), 1)
           <= lax.broadcasted_iota(jnp.int32, (sb, sb), 0)).astype(BF16)
    lf2 = lf * LOG2E
    tots, kmx = [], []
    for s in range(bm // sb):
        rows = slice(s * sb, (s + 1) * sb)
        parts = _dot(tri, jnp.concatenate(_split3(lf2[rows]), axis=1).astype(BF16))
        cb = parts[:, 0:LANES] + parts[:, LANES:2 * LANES] + parts[:, 2 * LANES:]
        cq_ref[rows, :] = cb[:, 0:n_heads]
        ck_ref[s] = _rows_to_lanes(cb, n_heads)
        off = (s * sb) % seg_rows
        lf_ref[(s * sb) // seg_rows, :, off:off + sb] = _rows_to_lanes(lf[rows], n_heads)
        tots.append(cb[sb - 1:sb, :])
        kmx.append(jnp.max(nk[rows], axis=0, keepdims=True))
    tot_ref[...] = jnp.concatenate(tots, axis=0)[:, 0:n_heads]
    kmax = kmx[0]
    for t in kmx[1:]:
        kmax = jnp.maximum(kmax, t)

    store_kv(v32_ref, vb_ref, _dot_nt(h, w_rows(2 * a_w, 3 * a_w)))

    zero_w = jnp.zeros((LANES, LANES), BF16)
    for g in range(0, len(POOL_WINDOWS), 2):
        cols = slice(g * LANES, (g + 2) * LANES)
        w2 = jnp.concatenate([jnp.concatenate([wp_ref[g].astype(BF16), zero_w], axis=1),
                              jnp.concatenate([zero_w, wp_ref[g + 1].astype(BF16)], axis=1)], axis=0)
        y = _dot(jnp.concatenate([pool_d[g], pool_d[g + 1]], axis=1), w2) * ps_ref[:, cols]
        zp_ref[:, cols] = (y * spg[:, cols]).astype(BF16)

    qs = _dot_nt(h, w_rows(0, a_w)) * sc
    q_ref[...] = qs.astype(BF16)
    nq = jnp.sqrt(_dot((qs * qs).astype(BF16), sel)) * NORM_SLACK
    nq_ref[...] = nq[:, 0:n_heads]
    qmx_ref[...] = jnp.concatenate([jnp.max(nq[s * sb:(s + 1) * sb], axis=0, keepdims=True)
                                    for s in range(bm // sb)], axis=0)[:, 0:n_heads]

    @pl.when(step > 0)
    def _():
        kmx_ref[...] = jnp.maximum(kmx_ref[...], kmax)

    @pl.when(step == 0)
    def _():
        kmx_ref[...] = kmax

    kg_ref[...] = kmx_ref[...]


def _project(x2, ada, norm_g, wit, b_f, wp, ps, hist0, *, bm, sb, segs, ada_row, start_pos, n_heads,
             kv_head_major):
    rows, d = x2.shape
    a_w = n_heads * HEAD_DIM
    pw = len(POOL_WINDOWS) * LANES
    n_steps = rows // bm
    assert segs == 1 or n_steps == 1
    n_streams = segs
    seg_rows = bm // segs
    nsb = bm // sb
    assert nsb == 8 and seg_rows % sb == 0
    row_blk = lambda w: pl.BlockSpec((bm, w), lambda i, *_: (i, 0))
    per_stream = lambda r, w: pl.BlockSpec((segs, r, w), lambda i, *_: (0, 0, 0))
    kern = functools.partial(_proj_kernel, bm=bm, sb=sb, segs=segs, ada_row=ada_row, start_pos=start_pos,
                             n_heads=n_heads, kv_head_major=kv_head_major)
    if kv_head_major:
        kv_shape = (rows, n_heads, HEAD_DIM)
        kv_blk = pl.BlockSpec((bm, n_heads, HEAD_DIM), lambda i, *_: (i, 0, 0))
    else:
        kv_shape = (a_w, rows)
        kv_blk = pl.BlockSpec((a_w, bm), lambda i, *_: (0, i))
    assert pw == a_w
    out_shape = (
        jax.ShapeDtypeStruct((rows, ACT_PARTS * a_w), BF16),
        jax.ShapeDtypeStruct(kv_shape, F32),
        jax.ShapeDtypeStruct(kv_shape, F32),
        jax.ShapeDtypeStruct((n_streams, n_heads, rows // n_streams), F32),
        jax.ShapeDtypeStruct((rows, n_heads), F32),
        jax.ShapeDtypeStruct((rows, n_heads), F32),
        jax.ShapeDtypeStruct((rows // sb, n_heads, sb), F32),
        jax.ShapeDtypeStruct((rows // sb, n_heads), F32),
        jax.ShapeDtypeStruct((rows // sb, n_heads), F32),
        jax.ShapeDtypeStruct((1, LANES), F32),
        jax.ShapeDtypeStruct((n_streams, 16, pw), F32),
    )
    out_specs = (
        row_blk(ACT_PARTS * a_w), kv_blk, kv_blk,
        pl.BlockSpec((segs, n_heads, seg_rows), lambda i, *_: (0, 0, i)),
        row_blk(n_heads), row_blk(n_heads),
        pl.BlockSpec((nsb, n_heads, sb), lambda i, *_: (i, 0, 0)),
        pl.BlockSpec((nsb, n_heads), lambda i, *_: (i, 0)),
        pl.BlockSpec((nsb, n_heads), lambda i, *_: (i, 0)),
        pl.BlockSpec((1, LANES), lambda i, *_: (0, 0)),
        per_stream(16, pw),
    )
    in_specs = [
        row_blk(d),
        _resident(ada.shape),
        _resident((1, d)),
        _resident(wit.shape), _resident(wp.shape), _resident(ps.shape),
        per_stream(HIST_PAD, pw),
    ]
    return pl.pallas_call(
        kern,
        grid_spec=pltpu.PrefetchScalarGridSpec(
            num_scalar_prefetch=1,
            grid=(n_steps,),
            in_specs=in_specs,
            out_specs=out_specs,
            scratch_shapes=[pltpu.VMEM((bm + segs * HIST_PAD, pw), F32),
                            pltpu.VMEM((bm + segs * HIST_PAD, pw), F32),
                            pltpu.VMEM((bm + segs * HIST_PAD, pw - LANES), F32),
                            pltpu.VMEM((bm + segs * HIST_PAD, pw - 2 * LANES), F32),
                            pltpu.VMEM((1, LANES), F32)]),
        out_shape=out_shape,
        compiler_params=pltpu.CompilerParams(dimension_semantics=("arbitrary",), vmem_limit_bytes=VMEM_LIMIT),
        name="proj",
    )(b_f, x2, ada, norm_g, wit, wp, ps, hist0)


def _merge_norm(x, gate, za, zp, wo_ref, fg, a_w):
    dy = _dot(za, wo_ref[0:a_w, :].astype(BF16)) + _dot(zp, wo_ref[a_w:, :].astype(BF16))
    out = x + gate * dy
    ms = jnp.mean(out * out, axis=-1, keepdims=True)
    return out * lax.rsqrt(ms + EPS) * fg


def _attn_kernel(tot_ref, qmx_ref, kg_ref,
                 act_ref, cq_ref, nq_ref, ck_ref, x_ref, ada_ref, wo_ref, fg_ref,
                 y_ref,
                 k_ref, v_ref, z_ref, m_ref, l_ref, acc_ref, straight_ref, *, tm, n_heads):
    a_w = n_heads * HEAD_DIM
    q_ref, knew_ref, vnew_ref, sa_ref, zp_ref = _act_views(act_ref, a_w)
    n_pairs = n_heads // PAIR
    nsub = tm // ATT_BLK
    w_near = NEAR_BLOCKS
    cw = w_near * ATT_BLK
    step = pl.program_id(0)

    def keep_step_rows():
        k_ref[pl.ds(pl.multiple_of(step * tm, tm), tm), :] = knew_ref[...]
        v_ref[pl.ds(pl.multiple_of(step * tm, tm), tm), :] = vnew_ref[...]

    lo_q = lax.broadcasted_iota(jnp.int32, (ATT_BLK, LANES), 1) < HEAD_DIM
    lo_k = lax.broadcasted_iota(jnp.int32, (cw, LANES), 1) < HEAD_DIM
    col = lax.broadcasted_iota(jnp.int32, (ATT_BLK, cw), 1)
    tri = (lax.broadcasted_iota(jnp.int32, (ATT_BLK, ATT_BLK), 1)
           <= lax.broadcasted_iota(jnp.int32, (ATT_BLK, ATT_BLK), 0))
    tri_bias = jnp.where(tri, 0.0, NEG).astype(F32)
    zeros_k = jnp.zeros((cw, LANES), BF16)
    ind_lo = jnp.where(lo_k, 1.0, 0.0).astype(BF16)
    ind_hi = jnp.where(lo_k, 0.0, 1.0).astype(BF16)

    def tot_at(b, hd):
        return jnp.where(b >= 0, tot_ref[jnp.maximum(b, 0), hd], 0.0)

    def rows_of(ref, blocks, g):
        return jnp.concatenate(
            [ref[pl.ds(pl.multiple_of(b * ATT_BLK, ATT_BLK), ATT_BLK), g * LANES:(g + 1) * LANES] for b in blocks],
            axis=0)

    def step_rows_of(ref, new_ref, sub, g):
        parts = []
        for p in range(w_near):
            rel = sub - (w_near - 1) + p
            if rel >= 0:
                parts.append(new_ref[rel * ATT_BLK:(rel + 1) * ATT_BLK, g * LANES:(g + 1) * LANES])
            else:
                start = pl.multiple_of((step * nsub + rel) * ATT_BLK, ATT_BLK)
                parts.append(ref[pl.ds(start, ATT_BLK), g * LANES:(g + 1) * LANES])
        return jnp.concatenate(parts, axis=0)

    def pair_scores(r0, kc, g):
        keys = jnp.concatenate([jnp.where(lo_k, kc, zeros_k), jnp.where(lo_k, zeros_k, kc)], axis=0)
        return _dot_nt(q_ref[pl.ds(r0, ATT_BLK), g * LANES:(g + 1) * LANES], keys)

    def pair_values(p_pair, vc):
        vals = jnp.concatenate([jnp.concatenate([jnp.where(lo_k, vc, zeros_k), ind_lo], axis=1),
                                jnp.concatenate([jnp.where(lo_k, zeros_k, vc), ind_hi], axis=1)], axis=0)
        return _dot(p_pair, vals)

    def decay_row(blocks, offs, hd):
        return jnp.concatenate([offs[p] - ck_ref[blocks[p], hd:hd + 1, :] for p in range(w_near)], axis=1)

    def near_weights(s_pair, rows, g, near_c, near_offs, bounded):
        ps, ms = [], []
        for e in range(PAIR):
            hd = PAIR * g + e
            cqh = cq_ref[rows, hd:hd + 1]
            if bounded:
                m = nq_ref[rows, hd:hd + 1] * kg_ref[0, hd]
                cqh = cqh - m
            dec = decay_row(near_c, near_offs[hd], hd)
            pieces = []
            for p in range(w_near):
                lanes = slice(e * cw + p * ATT_BLK, e * cw + (p + 1) * ATT_BLK)
                sp = s_pair[:, lanes] + cqh + dec[:, p * ATT_BLK:(p + 1) * ATT_BLK]
                pieces.append(sp + tri_bias if p == w_near - 1 else sp)
            s = jnp.concatenate(pieces, axis=1)
            if not bounded:
                m = jnp.max(s, axis=1, keepdims=True)
                s = s - m
            ms.append(m)
            ps.append(jnp.exp2(s).astype(BF16))
        return jnp.concatenate(ps, axis=1), ms

    def gated_output(rows, g, acc, l):
        cols = slice(g * LANES, (g + 1) * LANES)
        z_ref[rows, cols] = ((acc / l) * sa_ref[rows, cols].astype(F32)).astype(BF16)

    def merge(chunks=1):
        rows_c = tm // chunks
        for c in range(chunks):
            rows = slice(c * rows_c, (c + 1) * rows_c)
            y_ref[rows, :] = _merge_norm(x_ref[rows, :], ada_ref[2, ADA_PROMPT_ROW:ADA_PROMPT_ROW + 1, :],
                                         z_ref[rows, :], zp_ref[rows, :], wo_ref, fg_ref[...], a_w)

    def check_next_step():
        ok = jnp.bool_(True)
        last = qmx_ref.shape[0] - 1
        for sub in range(nsub):
            i = jnp.minimum((step + 1) * nsub + sub, last)
            for hd in range(n_heads):
                qk = 2.0 * qmx_ref[i, hd] * kg_ref[0, hd]
                back = qk
                for dd in range(1, w_near):
                    back = back + tot_ref[i - dd, hd]
                ok = jnp.logical_and(ok, jnp.logical_and(qk <= EXP2_SAFE_SPAN, back < -EXP2_UNDERFLOW))
        straight_ref[0] = ok.astype(jnp.int32)

    def straight_step():
        units = [(sub, g) for sub in range(nsub) for g in range(n_pairs)]
        near_of, offs_of = [], []
        for sub in range(nsub):
            i = step * nsub + sub
            near_of.append([i - (w_near - 1) + p for p in range(w_near)])
            offs = []
            for hd in range(n_heads):
                o, per_piece = jnp.float32(0.0), [jnp.float32(0.0)]
                for dd in range(1, w_near):
                    o = o + tot_ref[i - dd, hd]
                    per_piece.append(o)
                offs.append(per_piece[::-1])
            offs_of.append(offs)
        s_next = pair_scores(0, step_rows_of(k_ref, knew_ref, 0, 0), 0)
        for u, (sub, g) in enumerate(units):
            s_pair = s_next
            if u + 1 < len(units):
                sub1, g1 = units[u + 1]
                s_next = pair_scores(sub1 * ATT_BLK, step_rows_of(k_ref, knew_ref, sub1, g1), g1)
            rows = pl.ds(sub * ATT_BLK, ATT_BLK)
            p_pair, _ = near_weights(s_pair, rows, g, near_of[sub], offs_of[sub], True)
            r = pair_values(p_pair, step_rows_of(v_ref, vnew_ref, sub, g))
            gated_output(rows, g, r[:, 0:LANES], r[:, LANES:])
        check_next_step()
        keep_step_rows()
        merge(chunks=2)

    def sub_body(sub, carry):
        i = step * nsub + sub
        r0 = pl.multiple_of(sub * ATT_BLK, ATT_BLK)

        rows = pl.ds(r0, ATT_BLK)
        qk = [2.0 * qmx_ref[i, hd] * kg_ref[0, hd] for hd in range(n_heads)]

        near = [i - (w_near - 1) + p for p in range(w_near)]
        near_c = [jnp.maximum(b, 0) for b in near]
        near_offs, offs_far = [], []
        for hd in range(n_heads):
            back = [tot_at(i - dd, hd) for dd in range(1, w_near)]
            offs = []
            for p in range(w_near):
                o = jnp.float32(0.0)
                for dd in range(1, w_near - p):
                    o = o + back[dd - 1]
                offs.append(jnp.where(near[p] >= 0, o, NEG))
            near_offs.append(offs)
            o = jnp.float32(0.0)
            for t in back:
                o = o + t
            offs_far.append(o)

        def near_chunk(bounded):
            def fn():
                s_next = pair_scores(r0, rows_of(k_ref, near_c, 0), 0)
                for g in range(n_pairs):
                    s_pair = s_next
                    if g + 1 < n_pairs:
                        s_next = pair_scores(r0, rows_of(k_ref, near_c, g + 1), g + 1)
                    p_pair, ms = near_weights(s_pair, rows, g, near_c, near_offs, bounded)
                    for e in range(PAIR):
                        m_ref[PAIR * g + e] = ms[e]
                    r = pair_values(p_pair, rows_of(v_ref, near_c, g))
                    acc_ref[g] = r[:, 0:LANES]
                    l_ref[g] = r[:, LANES:]
                    gated_output(rows, g, r[:, 0:LANES], r[:, LANES:])
            return fn

        bounded_ok = qk[0] <= EXP2_SAFE_SPAN
        for hd in range(1, n_heads):
            bounded_ok = jnp.logical_and(bounded_ok, qk[hd] <= EXP2_SAFE_SPAN)
        pl.when(bounded_ok)(near_chunk(True))
        pl.when(jnp.logical_not(bounded_ok))(near_chunk(False))

        def far_cond(c):
            top = i - c[0] * w_near
            need = qk[0] + c[1] >= -EXP2_UNDERFLOW
            for hd in range(1, n_heads):
                need = jnp.logical_or(need, qk[hd] + c[1 + hd] >= -EXP2_UNDERFLOW)
            return jnp.logical_and(top >= 0, need)

        def far_body(c):
            top = i - c[0] * w_near
            jc = jnp.maximum(top - (w_near - 1), 0)
            blocks = [jc + p for p in range(w_near)]
            keepc = col < (top + 1 - jc) * ATT_BLK
            new = [c[0] + 1]
            for g in range(n_pairs):
                s_pair = pair_scores(r0, rows_of(k_ref, blocks, g), g)
                ps, alphas = [], []
                for e in range(PAIR):
                    hd = PAIR * g + e
                    tt = [tot_at(top - b, hd) for b in range(w_near)]
                    offs = []
                    for p in range(w_near):
                        behind = top - (jc + p)
                        o = c[1 + hd]
                        for b in range(w_near):
                            o = o + jnp.where(behind >= b, tt[b], 0.0)
                        offs.append(o)
                    s = s_pair[:, e * cw:(e + 1) * cw] + cq_ref[rows, hd:hd + 1] + decay_row(blocks, offs, hd)
                    s = jnp.where(keepc, s, NEG)
                    m_old = m_ref[hd]
                    m_new = jnp.maximum(m_old, jnp.max(s, axis=1, keepdims=True))
                    m_ref[hd] = m_new
                    alphas.append(jnp.broadcast_to(jnp.exp2(m_old - m_new), (ATT_BLK, LANES)))
                    ps.append(jnp.exp2(s - m_new).astype(BF16))
                    o = c[1 + hd]
                    for t in tt:
                        o = o + t
                    new.append(o)
                alpha = jnp.where(lo_q, alphas[0], alphas[1])
                r = pair_values(jnp.concatenate(ps, axis=1), rows_of(v_ref, blocks, g))
                acc_ref[g] = alpha * acc_ref[g] + r[:, 0:LANES]
                l_ref[g] = alpha * l_ref[g] + r[:, LANES:]
            return tuple(new)

        far = lax.while_loop(far_cond, far_body, (jnp.int32(1),) + tuple(offs_far))

        @pl.when(far[0] > 1)
        def _():
            for g in range(n_pairs):
                gated_output(rows, g, acc_ref[g], l_ref[g])
        return carry

    @pl.when(step == 0)
    def _():
        straight_ref[0] = 0

    straight = straight_ref[0] == 1
    pl.when(straight)(straight_step)

    @pl.when(jnp.logical_not(straight))
    def _():
        keep_step_rows()
        check_next_step()
        lax.fori_loop(0, nsub, sub_body, 0)
        merge()


def _prompt_attention(tot, qmx, kg, act, cq, nq, ck, x2, ada, wo, fg, *, tm, n_heads):
    rows, d = x2.shape
    a_w = n_heads * HEAD_DIM
    n_pairs = n_heads // PAIR
    assert rows % tm == 0 and tm % ATT_BLK == 0 and tm // ATT_BLK >= NEAR_BLOCKS - 1 and n_heads % PAIR == 0
    row_blk = lambda w: pl.BlockSpec((tm, w), lambda i, *_: (i, 0))
    grid_spec = pltpu.PrefetchScalarGridSpec(
        num_scalar_prefetch=3,
        grid=(rows // tm,),
        in_specs=[row_blk(ACT_PARTS * a_w), row_blk(n_heads), row_blk(n_heads), _resident(ck.shape), row_blk(d),
                  _resident(ada.shape), _resident(wo.shape), _resident((1, d))],
        out_specs=row_blk(d),
        scratch_shapes=[pltpu.VMEM((rows, a_w), BF16),
                        pltpu.VMEM((rows, a_w), BF16),
                        pltpu.VMEM((tm, a_w), BF16),
                        pltpu.VMEM((n_heads, ATT_BLK, 1), F32),
                        pltpu.VMEM((n_pairs, ATT_BLK, LANES), F32),
                        pltpu.VMEM((n_pairs, ATT_BLK, LANES), F32),
                        pltpu.SMEM((1,), jnp.int32)],
    )
    return pl.pallas_call(
        functools.partial(_attn_kernel, tm=tm, n_heads=n_heads),
        grid_spec=grid_spec,
        out_shape=jax.ShapeDtypeStruct((rows, d), F32),
        compiler_params=pltpu.CompilerParams(dimension_semantics=("arbitrary",), vmem_limit_bytes=VMEM_LIMIT),
        name="attn",
    )(tot, qmx, kg, act, cq, nq, ck, x2, ada, wo, fg)


CACHE_SPLIT = 2


def _sattn_kernel(act_ref, cq_ref, ckn_ref, kc0_ref, kc1_ref, vc0_ref, vc1_ref, lfc_ref,
                  x_ref, ada_ref, wo_ref, fg_ref, y_ref, *, n_heads):
    a_w = n_heads * HEAD_DIM
    q_ref, kn_ref, vn_ref, sa_ref, zp_ref = _act_views(act_ref, a_w)
    ln = act_ref.shape[0]
    past = kc0_ref.shape[2]
    kc_refs, vc_refs = (kc0_ref, kc1_ref), (vc0_ref, vc1_ref)
    pairs_per_part = n_heads // PAIR // CACHE_SPLIT

    def cached(refs, g):
        lo = (g % pairs_per_part) * LANES
        return refs[g // pairs_per_part][0, lo:lo + LANES, :].astype(BF16)

    nb = past // LANES

    lfc = lfc_ref[0] * LOG2E
    triu = (lax.broadcasted_iota(jnp.int32, (LANES, LANES), 0)
            <= lax.broadcasted_iota(jnp.int32, (LANES, LANES), 1)).astype(BF16)
    zeros = jnp.zeros((8, LANES), F32)
    parts = []
    for b in range(nb):
        parts.extend(_split3(lfc[:, b * LANES:(b + 1) * LANES]))
        parts.append(zeros)
    cs = _dot(jnp.concatenate(parts, axis=0).astype(BF16), triu)
    after = jnp.zeros((n_heads, 1), F32)
    suffix = [None] * nb
    for b in reversed(range(nb)):
        cb = cs[32 * b:32 * b + 8] + cs[32 * b + 8:32 * b + 16] + cs[32 * b + 16:32 * b + 24]
        tot = cb[:, LANES - 1:LANES]
        suffix[b] = (tot - cb) + after
        after = after + tot
    dec_c = jnp.concatenate(suffix, axis=1)

    lane = lax.broadcasted_iota(jnp.int32, (ln, LANES), 1)
    half = [lane < HEAD_DIM, lane >= HEAD_DIM]
    causal = lax.broadcasted_iota(jnp.int32, (ln, ln), 1) <= lax.broadcasted_iota(jnp.int32, (ln, ln), 0)
    n_pairs = n_heads // PAIR

    def pair_scores(g):
        cols = slice(g * LANES, (g + 1) * LANES)
        q2 = q_ref[:, cols]
        qst = jnp.concatenate([jnp.where(half[e], q2, jnp.zeros_like(q2)) for e in range(PAIR)], axis=0)
        return _dot(qst, cached(kc_refs, g)), _dot_nt(qst, kn_ref[:, cols])

    outs = []
    s_next = pair_scores(0)
    for g in range(n_pairs):
        cols = slice(g * LANES, (g + 1) * LANES)
        sc_st, sn_st = s_next
        if g + 1 < n_pairs:
            s_next = pair_scores(g + 1)
        vct = cached(vc_refs, g)
        vn = vn_ref[:, cols]
        pc, pn, ls = [], [], []
        for e in range(PAIR):
            hd = PAIR * g + e
            rows = slice(e * ln, (e + 1) * ln)
            cqh = cq_ref[:, hd:hd + 1]
            s_c = sc_st[rows] + cqh + dec_c[hd:hd + 1, :]
            s_n = jnp.where(causal, sn_st[rows] + cqh - ckn_ref[0, hd:hd + 1, :], NEG)
            m = jnp.maximum(jnp.max(s_c, axis=1, keepdims=True), jnp.max(s_n, axis=1, keepdims=True))
            p_c = jnp.exp2(s_c - m)
            p_n = jnp.exp2(s_n - m)
            ls.append(jnp.sum(p_c, axis=1, keepdims=True) + jnp.sum(p_n, axis=1, keepdims=True))
            pc.append(p_c.astype(BF16))
            pn.append(p_n.astype(BF16))
        acc = _dot_nt(jnp.concatenate(pc, axis=0), vct) + _dot(jnp.concatenate(pn, axis=0), vn)
        o = jnp.where(half[0], acc[0:ln] / ls[0], acc[ln:2 * ln] / ls[1])
        outs.append((o * sa_ref[:, cols].astype(F32)).astype(BF16))
    za = jnp.concatenate(outs, axis=1)
    y_ref[...] = _merge_norm(x_ref[...], ada_ref[2, pl.ds(ADA_SAMPLE_ROW + pl.program_id(0), 1), :], za, zp_ref[...], wo_ref, fg_ref[...], a_w)


def _sample_attention(act, cq, ckn, cache_k, cache_v, lfc, x2, ada, wo, fg, *, ln, n_heads):
    rows, d = x2.shape
    nbatch = rows // ln
    a_w = n_heads * HEAD_DIM
    past = cache_k.shape[2]
    row_blk = lambda w: pl.BlockSpec((ln, w), lambda b: (b, 0))
    per_b = lambda s: pl.BlockSpec((1,) + s, lambda b: (b, 0, 0))
    cache_part = lambda c: pl.BlockSpec((1, a_w // CACHE_SPLIT, past), lambda b: (b, c, 0))
    assert CACHE_SPLIT == 2 and (n_heads // PAIR) % CACHE_SPLIT == 0
    return pl.pallas_call(
        functools.partial(_sattn_kernel, n_heads=n_heads),
        grid=(nbatch,),
        in_specs=[row_blk(ACT_PARTS * a_w), row_blk(n_heads), per_b((n_heads, ln)),
                  cache_part(0), cache_part(1), cache_part(0), cache_part(1), per_b((n_heads, past)),
                  row_blk(d), _resident(ada.shape), _resident(wo.shape), _resident((1, d))],
        out_specs=row_blk(d),
        out_shape=jax.ShapeDtypeStruct((rows, d), F32),
        compiler_params=pltpu.CompilerParams(dimension_semantics=("arbitrary",), vmem_limit_bytes=VMEM_LIMIT),
        name="sattn",
    )(act, cq, ckn, cache_k, cache_k, cache_v, cache_v, lfc, x2, ada, wo, fg)


def kernel(x_prompt, x_sample, c_prompt, c_sample, cache_k, cache_v, cache_logf, state_pool, norm_g, w_ada, b_ada,
           w_in, b_f, w_pool, pool_scale, w_out, final_g):
    depth = norm_g.shape[0]
    assert depth == 1
    bp, seq, d = x_prompt.shape
    bs, ln, _ = x_sample.shape
    assert bp == 1
    n_heads = cache_k.shape[3]
    past = cache_k.shape[2]
    a_w = n_heads * HEAD_DIM
    pw = state_pool.shape[3]
    assert pw == len(POOL_WINDOWS) * LANES and cache_k.shape[4] == HEAD_DIM and n_heads <= 8

    ada = _ada_terms(c_prompt, c_sample, w_ada[0], b_ada)

    wit = w_in[0].T
    wp = w_pool[0]
    ps = pool_scale[0][None, :]
    wo = w_out[0]
    ng = norm_g[0][None, :]
    fg = final_g[None, :]

    assert seq % PROJ_ROWS == 0 and seq % ATTN_ROWS == 0
    xp2 = x_prompt.reshape(seq, d)
    hist_p = jnp.zeros((1, HIST_PAD, pw), F32)
    (act_p, k_p, v_p, lf_p, cq_p, nq_p, ck_p, tot, qmx, kg, ho_p) = _project(
        xp2, ada, ng, wit, b_f[0], wp, ps, hist_p,
        bm=PROJ_ROWS, sb=ATT_BLK, segs=1, ada_row=ADA_PROMPT_ROW, start_pos=0, n_heads=n_heads, kv_head_major=False)
    y_p = _prompt_attention(tot, qmx, kg, act_p, cq_p, nq_p, ck_p, xp2, ada, wo, fg, tm=ATTN_ROWS, n_heads=n_heads)

    xs2 = x_sample.reshape(bs * ln, d)
    hist_s = jnp.pad(state_pool[0], ((0, 0), (HIST_PAD - POOL_HIST, 0), (0, 0)))
    (act_s, k_s, v_s, lf_s, cq_s, _, ck_s, _, _, _, ho_s) = _project(
        xs2, ada, ng, wit, b_f[0], wp, ps, hist_s,
        bm=bs * ln, sb=ln, segs=bs, ada_row=ADA_SAMPLE_ROW, start_pos=past, n_heads=n_heads, kv_head_major=True)
    lfc = jnp.swapaxes(cache_logf[0], 1, 2)
    ckt = jnp.transpose(cache_k[0], (0, 2, 3, 1)).reshape(bs, a_w, past)
    cvt = jnp.transpose(cache_v[0], (0, 2, 3, 1)).reshape(bs, a_w, past)
    y_s = _sample_attention(act_s, cq_s, ck_s, ckt, cvt, lfc, xs2, ada, wo, fg, ln=ln, n_heads=n_heads)

    hd = (n_heads, HEAD_DIM)
    seq_minor = lambda t: jnp.transpose(t.reshape(hd + (bp, seq)), (2, 3, 0, 1))[None]
    return (y_p.reshape(bp, seq, d), y_s.reshape(bs, ln, d),
            seq_minor(k_p), seq_minor(v_p), jnp.swapaxes(lf_p, 1, 2)[None],
            ho_p[:, 16 - POOL_HIST:, :][None],
            k_s.reshape((1, bs, ln) + hd), v_s.reshape((1, bs, ln) + hd), jnp.swapaxes(lf_s, 1, 2)[None],
            ho_s[:, 16 - POOL_HIST:, :][None])
```

```python
import functools

import jax
import jax.numpy as jnp
from jax import lax
from jax.experimental import pallas as pl
from jax.experimental.pallas import tpu as pltpu

HEAD_DIM = 64
POOL_WINDOWS = (2, 4, 8, 16)
EPS = 1e-6

LANES = 128
PAIR = LANES // HEAD_DIM
ATT_BLK = 128
NEAR_BLOCKS = 3
LOG2E = 1.4426950408889634
EXP2_UNDERFLOW = 151.0
EXP2_SAFE_SPAN = 100.0
NORM_SLACK = 1.01
HIST_PAD = 32
POOL_HIST = max(POOL_WINDOWS) - 1
NEG = -1e30
VMEM_LIMIT = 60 * 1024 * 1024
PROJ_ROWS = 8 * ATT_BLK
ATTN_ROWS = 4 * ATT_BLK

F32 = jnp.float32
BF16 = jnp.bfloat16


def _silu(x):
    return x * jax.nn.sigmoid(x)


def _dot(a, b):
    return jnp.dot(a, b, preferred_element_type=F32)


def _dot_nt(a, b):
    return lax.dot_general(a, b, (((1,), (1,)), ((), ())), preferred_element_type=F32)


def _split3(x):
    hi = x.astype(BF16).astype(F32)
    r1 = x - hi
    mid = r1.astype(BF16).astype(F32)
    return hi, mid, r1 - mid


def _rows_to_lanes(x, n):
    rows = x.shape[0]
    if rows < LANES:
        x = jnp.concatenate([x, jnp.zeros((LANES - rows, LANES), x.dtype)], axis=0)
    return x.T[0:n, 0:rows]


def _lanes_to_rows(x):
    n, rows = x.shape
    if rows < LANES:
        x = jnp.concatenate([x, jnp.zeros((n, LANES - rows), x.dtype)], axis=1)
    x = jnp.concatenate([x, jnp.zeros((LANES - n, LANES), x.dtype)], axis=0)
    return x.T[0:rows, :]


ACT_PARTS = 5


def _act_views(act_ref, width):
    return [act_ref.at[:, p * width:(p + 1) * width] for p in range(ACT_PARTS)]


def _resident(shape):
    return pl.BlockSpec(shape, lambda *_: (0,) * len(shape), pipeline_mode=pl.Buffered(1))


ADA_ROWS = 16
ADA_PROMPT_ROW, ADA_SAMPLE_ROW = 0, 8


def _ada_kernel(cp_ref, cs_ref, w_ref, b_ref, o_ref):
    ap = jnp.broadcast_to(_silu(cp_ref[...]), (ADA_SAMPLE_ROW, cp_ref.shape[1]))
    a = jnp.concatenate([ap, _silu(cs_ref[...])], axis=0).astype(BF16)
    o_ref[0] = _dot(a, w_ref[...].astype(BF16)) + b_ref[...]


def _ada_terms(c_prompt, c_sample, w_ada, b_ada):
    d = c_prompt.shape[1]
    assert c_prompt.shape[0] == 1 and c_sample.shape[0] == ADA_ROWS - ADA_SAMPLE_ROW and w_ada.shape[1] == 3 * d
    return pl.pallas_call(
        _ada_kernel,
        grid=(3,),
        in_specs=[pl.BlockSpec(c_prompt.shape, lambda j: (0, 0)),
                  pl.BlockSpec(c_sample.shape, lambda j: (0, 0)),
                  pl.BlockSpec((d, d), lambda j: (0, j)),
                  pl.BlockSpec((1, d), lambda j: (0, j))],
        out_specs=pl.BlockSpec((1, ADA_ROWS, d), lambda j: (j, 0, 0)),
        out_shape=jax.ShapeDtypeStruct((3, ADA_ROWS, d), F32),
        compiler_params=pltpu.CompilerParams(dimension_semantics=("arbitrary",), vmem_limit_bytes=VMEM_LIMIT),
        name="ada",
    )(c_prompt, c_sample, w_ada, b_ada)


def _proj_kernel(bf_ref, x_ref, ada_ref, ng_ref, w_ref, wp_ref, ps_ref, h0_ref,
                 act_ref, k32_ref, v32_ref, lf_ref, cq_ref, nq_ref, ck_ref,
                 tot_ref, qmx_ref, kg_ref, ho_ref,
                 e_ref, t2_ref, t4_ref, t8_ref, kmx_ref, *, bm, sb, segs, ada_row, start_pos, n_heads, kv_head_major):
    a_w = n_heads * HEAD_DIM
    q_ref, kb_ref, vb_ref, sa_ref, zp_ref = _act_views(act_ref, a_w)
    seg_rows = bm // segs
    step = pl.program_id(0)
    pw = len(POOL_WINDOWS) * LANES
    sc = LOG2E / (HEAD_DIM ** 0.5)
    o_pool = 4 * a_w + n_heads

    def w_rows(lo, hi):
        return w_ref[lo:hi, :].astype(BF16)

    def normed(lo, hi):
        x = x_ref[lo:hi, :]
        xn = x * lax.rsqrt(jnp.mean(x * x, axis=-1, keepdims=True) + EPS)
        parts = []
        for g in range(lo // seg_rows, (hi - 1) // seg_rows + 1):
            r0, r1 = max(lo, g * seg_rows) - lo, min(hi, (g + 1) * seg_rows) - lo
            parts.append(xn[r0:r1] * (ng_ref[...] * (1.0 + ada_ref[1, ada_row + g:ada_row + g + 1, :]))
                         + ada_ref[0, ada_row + g:ada_row + g + 1, :])
        return jnp.concatenate(parts, axis=0).astype(BF16)

    n_lead = 4
    hs, pus = [], []
    for c in range(n_lead):
        hs.append(normed(c * bm // n_lead, (c + 1) * bm // n_lead))
        pus.append(_dot_nt(hs[-1], w_rows(o_pool, o_pool + pw)))
    pu = jnp.concatenate(pus, axis=0)
    h = jnp.concatenate(hs, axis=0)
    sel = (lax.broadcasted_iota(jnp.int32, (a_w, LANES), 0) // HEAD_DIM
           == lax.broadcasted_iota(jnp.int32, (a_w, LANES), 1)).astype(BF16)

    def store_kv(ref32, refb, p):
        refb[...] = p.astype(BF16)
        if kv_head_major:
            for hd in range(n_heads):
                ref32[:, hd, :] = p[:, hd * HEAD_DIM:(hd + 1) * HEAD_DIM]
        else:
            ref32[...] = p.T

    ext = HIST_PAD + seg_rows
    n = segs * ext

    def load_history():
        for g in range(segs):
            e_ref[g * ext:g * ext + HIST_PAD, :] = h0_ref[g]

    if segs > 1:
        load_history()
    else:
        pl.when(step == 0)(load_history)

    for g in range(segs):
        e_ref[g * ext + HIST_PAD:(g + 1) * ext, :] = pu[g * seg_rows:(g + 1) * seg_rows]
    t2_ref[8:n, :] = e_ref[8:n, :] + e_ref[7:n - 1, :]
    t4_ref[16:n, :] = t2_ref[16:n, LANES:] + t2_ref[14:n - 2, LANES:]
    t8_ref[24:n, :] = t4_ref[24:n, LANES:] + t4_ref[20:n - 4, LANES:]

    def seg_rows_of(ref, cols, back=0):
        return jnp.concatenate([ref[g * ext + HIST_PAD - back:(g + 1) * ext - back, cols] for g in range(segs)], axis=0)

    lane0, lane1 = slice(0, LANES), slice(LANES, 2 * LANES)
    sums = [seg_rows_of(t2_ref, lane0), seg_rows_of(t4_ref, lane0), seg_rows_of(t8_ref, lane0),
            seg_rows_of(t8_ref, lane1) + seg_rows_of(t8_ref, lane1, back=8)]
    row = lax.broadcasted_iota(jnp.int32, (bm, 1), 0)
    pos1 = start_pos + 1 + (step * bm + row if segs == 1 else row % seg_rows)
    pool_d = []
    for g, w in enumerate(POOL_WINDOWS):
        rc = 1.0 / jnp.minimum(pos1, w).astype(F32)
        pool_d.append((sums[g] * rc - pu[:, g * LANES:(g + 1) * LANES]).astype(BF16))
    for g in range(segs):
        ho_ref[g] = e_ref[(g + 1) * ext - 16:(g + 1) * ext, :]
    if segs == 1:
        e_ref[0:HIST_PAD, :] = e_ref[bm:n, :]

    pk = _dot_nt(h, w_rows(a_w, 2 * a_w))
    store_kv(k32_ref, kb_ref, pk)
    nk = jnp.sqrt(_dot((pk * pk).astype(BF16), sel)) * NORM_SLACK
    spg = _silu(_dot_nt(h, w_rows(o_pool + pw, o_pool + 2 * pw)))

    lane = lax.broadcasted_iota(jnp.int32, (1, LANES), 1)
    bias = jnp.zeros((1, LANES), F32)
    for hd in range(n_heads):
        bias = jnp.where(lane == hd, bf_ref[hd], bias)
    wf = jnp.concatenate([w_ref[4 * a_w:o_pool, :], jnp.zeros((LANES - n_heads, w_ref.shape[1]), F32)], axis=0)
    z = _dot_nt(h, wf.astype(BF16)) + bias
    lf = jnp.minimum(z, 0.0) - jnp.log1p(jnp.exp(-jnp.abs(z)))

    sa_ref[...] = _silu(_dot_nt(h, w_rows(3 * a_w, 4 * a_w))).astype(BF16)

    key_lane = lax.broadcasted_iota(jnp.int32, (n_heads, LANES), 1)
    tots, kmx = [], []
    for s in range(bm // sb):
        rows = slice(s * sb, (s + 1) * sb)
        lft = _rows_to_lanes(lf[rows], n_heads)
        off = (s * sb) % seg_rows
        lf_ref[(s * sb) // seg_rows, :, off:off + sb] = lft
        c = lft * LOG2E
        if sb < LANES:
            c = jnp.concatenate([c, jnp.zeros((n_heads, LANES - sb), F32)], axis=1)
        shift = 1
        while shift < sb:
            c = c + jnp.where(key_lane >= shift, pltpu.roll(c, shift, 1), 0.0)
            shift *= 2
        ck_ref[s] = c[:, 0:sb]
        cb = _lanes_to_rows(c)[0:sb]
        cq_ref[rows, :] = cb[:, 0:n_heads]
        tots.append(cb[sb - 1:sb, :])
        kmx.append(jnp.max(nk[rows], axis=0, keepdims=True))
    tot_ref[...] = jnp.concatenate(tots, axis=0)[:, 0:n_heads]
    kmax = kmx[0]
    for t in kmx[1:]:
        kmax = jnp.maximum(kmax, t)

    store_kv(v32_ref, vb_ref, _dot_nt(h, w_rows(2 * a_w, 3 * a_w)))

    zero_w = jnp.zeros((LANES, LANES), BF16)
    for g in range(0, len(POOL_WINDOWS), 2):
        cols = slice(g * LANES, (g + 2) * LANES)
        w2 = jnp.concatenate([jnp.concatenate([wp_ref[g].astype(BF16), zero_w], axis=1),
                              jnp.concatenate([zero_w, wp_ref[g + 1].astype(BF16)], axis=1)], axis=0)
        y = _dot(jnp.concatenate([pool_d[g], pool_d[g + 1]], axis=1), w2) * ps_ref[:, cols]
        zp_ref[:, cols] = (y * spg[:, cols]).astype(BF16)

    qs = _dot_nt(h, w_rows(0, a_w)) * sc
    q_ref[...] = qs.astype(BF16)
    nq = jnp.sqrt(_dot((qs * qs).astype(BF16), sel)) * NORM_SLACK
    nq_ref[...] = nq[:, 0:n_heads]
    qmx_ref[...] = jnp.concatenate([jnp.max(nq[s * sb:(s + 1) * sb], axis=0, keepdims=True)
                                    for s in range(bm // sb)], axis=0)[:, 0:n_heads]

    @pl.when(step > 0)
    def _():
        kmx_ref[...] = jnp.maximum(kmx_ref[...], kmax)

    @pl.when(step == 0)
    def _():
        kmx_ref[...] = kmax

    kg_ref[...] = kmx_ref[...]


def _project(x2, ada, norm_g, wit, b_f, wp, ps, hist0, *, bm, sb, segs, ada_row, start_pos, n_heads,
             kv_head_major):
    rows, d = x2.shape
    a_w = n_heads * HEAD_DIM
    pw = len(POOL_WINDOWS) * LANES
    n_steps = rows // bm
    assert segs == 1 or n_steps == 1
    n_streams = segs
    seg_rows = bm // segs
    nsb = bm // sb
    assert nsb == 8 and seg_rows % sb == 0
    row_blk = lambda w: pl.BlockSpec((bm, w), lambda i, *_: (i, 0))
    per_stream = lambda r, w: pl.BlockSpec((segs, r, w), lambda i, *_: (0, 0, 0))
    kern = functools.partial(_proj_kernel, bm=bm, sb=sb, segs=segs, ada_row=ada_row, start_pos=start_pos,
                             n_heads=n_heads, kv_head_major=kv_head_major)
    if kv_head_major:
        kv_shape = (rows, n_heads, HEAD_DIM)
        kv_blk = pl.BlockSpec((bm, n_heads, HEAD_DIM), lambda i, *_: (i, 0, 0))
    else:
        kv_shape = (a_w, rows)
        kv_blk = pl.BlockSpec((a_w, bm), lambda i, *_: (0, i))
    assert pw == a_w
    out_shape = (
        jax.ShapeDtypeStruct((rows, ACT_PARTS * a_w), BF16),
        jax.ShapeDtypeStruct(kv_shape, F32),
        jax.ShapeDtypeStruct(kv_shape, F32),
        jax.ShapeDtypeStruct((n_streams, n_heads, rows // n_streams), F32),
        jax.ShapeDtypeStruct((rows, n_heads), F32),
        jax.ShapeDtypeStruct((rows, n_heads), F32),
        jax.ShapeDtypeStruct((rows // sb, n_heads, sb), F32),
        jax.ShapeDtypeStruct((rows // sb, n_heads), F32),
        jax.ShapeDtypeStruct((rows // sb, n_heads), F32),
        jax.ShapeDtypeStruct((1, LANES), F32),
        jax.ShapeDtypeStruct((n_streams, 16, pw), F32),
    )
    out_specs = (
        row_blk(ACT_PARTS * a_w), kv_blk, kv_blk,
        pl.BlockSpec((segs, n_heads, seg_rows), lambda i, *_: (0, 0, i)),
        row_blk(n_heads), row_blk(n_heads),
        pl.BlockSpec((nsb, n_heads, sb), lambda i, *_: (i, 0, 0)),
        pl.BlockSpec((nsb, n_heads), lambda i, *_: (i, 0)),
        pl.BlockSpec((nsb, n_heads), lambda i, *_: (i, 0)),
        pl.BlockSpec((1, LANES), lambda i, *_: (0, 0)),
        per_stream(16, pw),
    )
    in_specs = [
        row_blk(d),
        _resident(ada.shape),
        _resident((1, d)),
        _resident(wit.shape), _resident(wp.shape), _resident(ps.shape),
        per_stream(HIST_PAD, pw),
    ]
    return pl.pallas_call(
        kern,
        grid_spec=pltpu.PrefetchScalarGridSpec(
            num_scalar_prefetch=1,
            grid=(n_steps,),
            in_specs=in_specs,
            out_specs=out_specs,
            scratch_shapes=[pltpu.VMEM((bm + segs * HIST_PAD, pw), F32),
                            pltpu.VMEM((bm + segs * HIST_PAD, pw), F32),
                            pltpu.VMEM((bm + segs * HIST_PAD, pw - LANES), F32),
                            pltpu.VMEM((bm + segs * HIST_PAD, pw - 2 * LANES), F32),
                            pltpu.VMEM((1, LANES), F32)]),
        out_shape=out_shape,
        compiler_params=pltpu.CompilerParams(dimension_semantics=("arbitrary",), vmem_limit_bytes=VMEM_LIMIT),
        name="proj",
    )(b_f, x2, ada, norm_g, wit, wp, ps, hist0)


def _merge_norm(x, gate, za, zp, wo_ref, fg, a_w):
    dy = _dot(za, wo_ref[0:a_w, :].astype(BF16)) + _dot(zp, wo_ref[a_w:, :].astype(BF16))
    out = x + gate * dy
    ms = jnp.mean(out * out, axis=-1, keepdims=True)
    return out * lax.rsqrt(ms + EPS) * fg


def _attn_kernel(tot_ref, qmx_ref, kg_ref,
                 act_ref, cq_ref, nq_ref, ck_ref, x_ref, ada_ref, wo_ref, fg_ref,
                 y_ref,
                 k_ref, v_ref, z_ref, m_ref, l_ref, acc_ref, straight_ref, *, tm, n_heads):
    a_w = n_heads * HEAD_DIM
    q_ref, knew_ref, vnew_ref, sa_ref, zp_ref = _act_views(act_ref, a_w)
    n_pairs = n_heads // PAIR
    nsub = tm // ATT_BLK
    w_near = NEAR_BLOCKS
    cw = w_near * ATT_BLK
    step = pl.program_id(0)

    def keep_step_rows():
        k_ref[pl.ds(pl.multiple_of(step * tm, tm), tm), :] = knew_ref[...]
        v_ref[pl.ds(pl.multiple_of(step * tm, tm), tm), :] = vnew_ref[...]

    lo_q = lax.broadcasted_iota(jnp.int32, (ATT_BLK, LANES), 1) < HEAD_DIM
    lo_k = lax.broadcasted_iota(jnp.int32, (cw, LANES), 1) < HEAD_DIM
    col = lax.broadcasted_iota(jnp.int32, (ATT_BLK, cw), 1)
    tri = (lax.broadcasted_iota(jnp.int32, (ATT_BLK, ATT_BLK), 1)
           <= lax.broadcasted_iota(jnp.int32, (ATT_BLK, ATT_BLK), 0))
    tri_bias = jnp.where(tri, 0.0, NEG).astype(F32)
    zeros_k = jnp.zeros((cw, LANES), BF16)
    ind_lo = jnp.where(lo_k, 1.0, 0.0).astype(BF16)
    ind_hi = jnp.where(lo_k, 0.0, 1.0).astype(BF16)

    def tot_at(b, hd):
        return jnp.where(b >= 0, tot_ref[jnp.maximum(b, 0), hd], 0.0)

    def rows_of(ref, blocks, g):
        return jnp.concatenate(
            [ref[pl.ds(pl.multiple_of(b * ATT_BLK, ATT_BLK), ATT_BLK), g * LANES:(g + 1) * LANES] for b in blocks],
            axis=0)

    def step_rows_of(ref, new_ref, sub, g):
        parts = []
        for p in range(w_near):
            rel = sub - (w_near - 1) + p
            if rel >= 0:
                parts.append(new_ref[rel * ATT_BLK:(rel + 1) * ATT_BLK, g * LANES:(g + 1) * LANES])
            else:
                start = pl.multiple_of((step * nsub + rel) * ATT_BLK, ATT_BLK)
                parts.append(ref[pl.ds(start, ATT_BLK), g * LANES:(g + 1) * LANES])
        return jnp.concatenate(parts, axis=0)

    def pair_scores(r0, kc, g):
        keys = jnp.concatenate([jnp.where(lo_k, kc, zeros_k), jnp.where(lo_k, zeros_k, kc)], axis=0)
        return _dot_nt(q_ref[pl.ds(r0, ATT_BLK), g * LANES:(g + 1) * LANES], keys)

    def pair_values(p_pair, vc):
        vals = jnp.concatenate([jnp.concatenate([jnp.where(lo_k, vc, zeros_k), ind_lo], axis=1),
                                jnp.concatenate([jnp.where(lo_k, zeros_k, vc), ind_hi], axis=1)], axis=0)
        return _dot(p_pair, vals)

    def decay_row(blocks, offs, hd):
        return jnp.concatenate([offs[p] - ck_ref[blocks[p], hd:hd + 1, :] for p in range(w_near)], axis=1)

    def near_weights(s_pair, rows, g, near_c, near_offs, bounded):
        ps, ms = [], []
        for e in range(PAIR):
            hd = PAIR * g + e
            cqh = cq_ref[rows, hd:hd + 1]
            if bounded:
                m = nq_ref[rows, hd:hd + 1] * kg_ref[0, hd]
                cqh = cqh - m
            dec = decay_row(near_c, near_offs[hd], hd)
            pieces = []
            for p in range(w_near):
                lanes = slice(e * cw + p * ATT_BLK, e * cw + (p + 1) * ATT_BLK)
                sp = s_pair[:, lanes] + cqh + dec[:, p * ATT_BLK:(p + 1) * ATT_BLK]
                pieces.append(sp + tri_bias if p == w_near - 1 else sp)
            s = jnp.concatenate(pieces, axis=1)
            if not bounded:
                m = jnp.max(s, axis=1, keepdims=True)
                s = s - m
            ms.append(m)
            ps.append(jnp.exp2(s).astype(BF16))
        return jnp.concatenate(ps, axis=1), ms

    def gated_output(rows, g, acc, l):
        cols = slice(g * LANES, (g + 1) * LANES)
        z_ref[rows, cols] = ((acc / l) * sa_ref[rows, cols].astype(F32)).astype(BF16)

    def merge(chunks=1):
        rows_c = tm // chunks
        for c in range(chunks):
            rows = slice(c * rows_c, (c + 1) * rows_c)
            y_ref[rows, :] = _merge_norm(x_ref[rows, :], ada_ref[2, ADA_PROMPT_ROW:ADA_PROMPT_ROW + 1, :],
                                         z_ref[rows, :], zp_ref[rows, :], wo_ref, fg_ref[...], a_w)

    def check_next_step():
        ok = jnp.bool_(True)
        last = qmx_ref.shape[0] - 1
        for sub in range(nsub):
            i = jnp.minimum((step + 1) * nsub + sub, last)
            for hd in range(n_heads):
                qk = 2.0 * qmx_ref[i, hd] * kg_ref[0, hd]
                back = qk
                for dd in range(1, w_near):
                    back = back + tot_ref[i - dd, hd]
                ok = jnp.logical_and(ok, jnp.logical_and(qk <= EXP2_SAFE_SPAN, back < -EXP2_UNDERFLOW))
        straight_ref[0] = ok.astype(jnp.int32)

    def straight_step():
        units = [(sub, g) for sub in range(nsub) for g in range(n_pairs)]
        near_of, offs_of = [], []
        for sub in range(nsub):
            i = step * nsub + sub
            near_of.append([i - (w_near - 1) + p for p in range(w_near)])
            offs = []
            for hd in range(n_heads):
                o, per_piece = jnp.float32(0.0), [jnp.float32(0.0)]
                for dd in range(1, w_near):
                    o = o + tot_ref[i - dd, hd]
                    per_piece.append(o)
                offs.append(per_piece[::-1])
            offs_of.append(offs)
        s_next = pair_scores(0, step_rows_of(k_ref, knew_ref, 0, 0), 0)
        for u, (sub, g) in enumerate(units):
            s_pair = s_next
            if u + 1 < len(units):
                sub1, g1 = units[u + 1]
                s_next = pair_scores(sub1 * ATT_BLK, step_rows_of(k_ref, knew_ref, sub1, g1), g1)
            rows = pl.ds(sub * ATT_BLK, ATT_BLK)
            p_pair, _ = near_weights(s_pair, rows, g, near_of[sub], offs_of[sub], True)
            r = pair_values(p_pair, step_rows_of(v_ref, vnew_ref, sub, g))
            gated_output(rows, g, r[:, 0:LANES], r[:, LANES:])
        check_next_step()
        keep_step_rows()
        merge(chunks=2)

    def sub_body(sub, carry):
        i = step * nsub + sub
        r0 = pl.multiple_of(sub * ATT_BLK, ATT_BLK)

        rows = pl.ds(r0, ATT_BLK)
        qk = [2.0 * qmx_ref[i, hd] * kg_ref[0, hd] for hd in range(n_heads)]

        near = [i - (w_near - 1) + p for p in range(w_near)]
        near_c = [jnp.maximum(b, 0) for b in near]
        near_offs, offs_far = [], []
        for hd in range(n_heads):
            back = [tot_at(i - dd, hd) for dd in range(1, w_near)]
            offs = []
            for p in range(w_near):
                o = jnp.float32(0.0)
                for dd in range(1, w_near - p):
                    o = o + back[dd - 1]
                offs.append(jnp.where(near[p] >= 0, o, NEG))
            near_offs.append(offs)
            o = jnp.float32(0.0)
            for t in back:
                o = o + t
            offs_far.append(o)

        def near_chunk(bounded):
            def fn():
                s_next = pair_scores(r0, rows_of(k_ref, near_c, 0), 0)
                for g in range(n_pairs):
                    s_pair = s_next
                    if g + 1 < n_pairs:
                        s_next = pair_scores(r0, rows_of(k_ref, near_c, g + 1), g + 1)
                    p_pair, ms = near_weights(s_pair, rows, g, near_c, near_offs, bounded)
                    for e in range(PAIR):
                        m_ref[PAIR * g + e] = ms[e]
                    r = pair_values(p_pair, rows_of(v_ref, near_c, g))
                    acc_ref[g] = r[:, 0:LANES]
                    l_ref[g] = r[:, LANES:]
                    gated_output(rows, g, r[:, 0:LANES], r[:, LANES:])
            return fn

        bounded_ok = qk[0] <= EXP2_SAFE_SPAN
        for hd in range(1, n_heads):
            bounded_ok = jnp.logical_and(bounded_ok, qk[hd] <= EXP2_SAFE_SPAN)
        pl.when(bounded_ok)(near_chunk(True))
        pl.when(jnp.logical_not(bounded_ok))(near_chunk(False))

        def far_cond(c):
            top = i - c[0] * w_near
            need = qk[0] + c[1] >= -EXP2_UNDERFLOW
            for hd in range(1, n_heads):
                need = jnp.logical_or(need, qk[hd] + c[1 + hd] >= -EXP2_UNDERFLOW)
            return jnp.logical_and(top >= 0, need)

        def far_body(c):
            top = i - c[0] * w_near
            jc = jnp.maximum(top - (w_near - 1), 0)
            blocks = [jc + p for p in range(w_near)]
            keepc = col < (top + 1 - jc) * ATT_BLK
            new = [c[0] + 1]
            for g in range(n_pairs):
                s_pair = pair_scores(r0, rows_of(k_ref, blocks, g), g)
                ps, alphas = [], []
                for e in range(PAIR):
                    hd = PAIR * g + e
                    tt = [tot_at(top - b, hd) for b in range(w_near)]
                    offs = []
                    for p in range(w_near):
                        behind = top - (jc + p)
                        o = c[1 + hd]
                        for b in range(w_near):
                            o = o + jnp.where(behind >= b, tt[b], 0.0)
                        offs.append(o)
                    s = s_pair[:, e * cw:(e + 1) * cw] + cq_ref[rows, hd:hd + 1] + decay_row(blocks, offs, hd)
                    s = jnp.where(keepc, s, NEG)
                    m_old = m_ref[hd]
                    m_new = jnp.maximum(m_old, jnp.max(s, axis=1, keepdims=True))
                    m_ref[hd] = m_new
                    alphas.append(jnp.broadcast_to(jnp.exp2(m_old - m_new), (ATT_BLK, LANES)))
                    ps.append(jnp.exp2(s - m_new).astype(BF16))
                    o = c[1 + hd]
                    for t in tt:
                        o = o + t
                    new.append(o)
                alpha = jnp.where(lo_q, alphas[0], alphas[1])
                r = pair_values(jnp.concatenate(ps, axis=1), rows_of(v_ref, blocks, g))
                acc_ref[g] = alpha * acc_ref[g] + r[:, 0:LANES]
                l_ref[g] = alpha * l_ref[g] + r[:, LANES:]
            return tuple(new)

        far = lax.while_loop(far_cond, far_body, (jnp.int32(1),) + tuple(offs_far))

        @pl.when(far[0] > 1)
        def _():
            for g in range(n_pairs):
                gated_output(rows, g, acc_ref[g], l_ref[g])
        return carry

    @pl.when(step == 0)
    def _():
        straight_ref[0] = 0

    straight = straight_ref[0] == 1
    pl.when(straight)(straight_step)

    @pl.when(jnp.logical_not(straight))
    def _():
        keep_step_rows()
        check_next_step()
        lax.fori_loop(0, nsub, sub_body, 0)
        merge()


def _prompt_attention(tot, qmx, kg, act, cq, nq, ck, x2, ada, wo, fg, *, tm, n_heads):
    rows, d = x2.shape
    a_w = n_heads * HEAD_DIM
    n_pairs = n_heads // PAIR
    assert rows % tm == 0 and tm % ATT_BLK == 0 and tm // ATT_BLK >= NEAR_BLOCKS - 1 and n_heads % PAIR == 0
    row_blk = lambda w: pl.BlockSpec((tm, w), lambda i, *_: (i, 0))
    grid_spec = pltpu.PrefetchScalarGridSpec(
        num_scalar_prefetch=3,
        grid=(rows // tm,),
        in_specs=[row_blk(ACT_PARTS * a_w), row_blk(n_heads), row_blk(n_heads), _resident(ck.shape), row_blk(d),
                  _resident(ada.shape), _resident(wo.shape), _resident((1, d))],
        out_specs=row_blk(d),
        scratch_shapes=[pltpu.VMEM((rows, a_w), BF16),
                        pltpu.VMEM((rows, a_w), BF16),
                        pltpu.VMEM((tm, a_w), BF16),
                        pltpu.VMEM((n_heads, ATT_BLK, 1), F32),
                        pltpu.VMEM((n_pairs, ATT_BLK, LANES), F32),
                        pltpu.VMEM((n_pairs, ATT_BLK, LANES), F32),
                        pltpu.SMEM((1,), jnp.int32)],
    )
    return pl.pallas_call(
        functools.partial(_attn_kernel, tm=tm, n_heads=n_heads),
        grid_spec=grid_spec,
        out_shape=jax.ShapeDtypeStruct((rows, d), F32),
        compiler_params=pltpu.CompilerParams(dimension_semantics=("arbitrary",), vmem_limit_bytes=VMEM_LIMIT),
        name="attn",
    )(tot, qmx, kg, act, cq, nq, ck, x2, ada, wo, fg)


CACHE_SPLIT = 2


def _sattn_kernel(act_ref, cq_ref, ckn_ref, kc0_ref, kc1_ref, vc0_ref, vc1_ref, lfc_ref,
                  x_ref, ada_ref, wo_ref, fg_ref, y_ref, *, n_heads):
    a_w = n_heads * HEAD_DIM
    q_ref, kn_ref, vn_ref, sa_ref, zp_ref = _act_views(act_ref, a_w)
    ln = act_ref.shape[0]
    past = kc0_ref.shape[2]
    kc_refs, vc_refs = (kc0_ref, kc1_ref), (vc0_ref, vc1_ref)
    pairs_per_part = n_heads // PAIR // CACHE_SPLIT

    def cached(refs, g):
        lo = (g % pairs_per_part) * LANES
        return refs[g // pairs_per_part][0, lo:lo + LANES, :].astype(BF16)

    nb = past // LANES

    lfc = lfc_ref[0] * LOG2E
    triu = (lax.broadcasted_iota(jnp.int32, (LANES, LANES), 0)
            <= lax.broadcasted_iota(jnp.int32, (LANES, LANES), 1)).astype(BF16)
    zeros = jnp.zeros((8, LANES), F32)
    parts = []
    for b in range(nb):
        parts.extend(_split3(lfc[:, b * LANES:(b + 1) * LANES]))
        parts.append(zeros)
    cs = _dot(jnp.concatenate(parts, axis=0).astype(BF16), triu)
    after = jnp.zeros((n_heads, 1), F32)
    suffix = [None] * nb
    for b in reversed(range(nb)):
        cb = cs[32 * b:32 * b + 8] + cs[32 * b + 8:32 * b + 16] + cs[32 * b + 16:32 * b + 24]
        tot = cb[:, LANES - 1:LANES]
        suffix[b] = (tot - cb) + after
        after = after + tot
    dec_c = jnp.concatenate(suffix, axis=1)

    lane = lax.broadcasted_iota(jnp.int32, (ln, LANES), 1)
    half = [lane < HEAD_DIM, lane >= HEAD_DIM]
    causal = lax.broadcasted_iota(jnp.int32, (ln, ln), 1) <= lax.broadcasted_iota(jnp.int32, (ln, ln), 0)
    n_pairs = n_heads // PAIR

    def pair_scores(g):
        cols = slice(g * LANES, (g + 1) * LANES)
        q2 = q_ref[:, cols]
        qst = jnp.concatenate([jnp.where(half[e], q2, jnp.zeros_like(q2)) for e in range(PAIR)], axis=0)
        return _dot(qst, cached(kc_refs, g)), _dot_nt(qst, kn_ref[:, cols])

    outs = []
    s_next = pair_scores(0)
    for g in range(n_pairs):
        cols = slice(g * LANES, (g + 1) * LANES)
        sc_st, sn_st = s_next
        if g + 1 < n_pairs:
            s_next = pair_scores(g + 1)
        vct = cached(vc_refs, g)
        vn = vn_ref[:, cols]
        pc, pn, ls = [], [], []
        for e in range(PAIR):
            hd = PAIR * g + e
            rows = slice(e * ln, (e + 1) * ln)
            cqh = cq_ref[:, hd:hd + 1]
            s_c = sc_st[rows] + cqh + dec_c[hd:hd + 1, :]
            s_n = jnp.where(causal, sn_st[rows] + cqh - ckn_ref[0, hd:hd + 1, :], NEG)
            m = jnp.maximum(jnp.max(s_c, axis=1, keepdims=True), jnp.max(s_n, axis=1, keepdims=True))
            p_c = jnp.exp2(s_c - m)
            p_n = jnp.exp2(s_n - m)
            ls.append(jnp.sum(p_c, axis=1, keepdims=True) + jnp.sum(p_n, axis=1, keepdims=True))
            pc.append(p_c.astype(BF16))
            pn.append(p_n.astype(BF16))
        acc = _dot_nt(jnp.concatenate(pc, axis=0), vct) + _dot(jnp.concatenate(pn, axis=0), vn)
        o = jnp.where(half[0], acc[0:ln] / ls[0], acc[ln:2 * ln] / ls[1])
        outs.append((o * sa_ref[:, cols].astype(F32)).astype(BF16))
    za = jnp.concatenate(outs, axis=1)
    y_ref[...] = _merge_norm(x_ref[...], ada_ref[2, pl.ds(ADA_SAMPLE_ROW + pl.program_id(0), 1), :], za, zp_ref[...], wo_ref, fg_ref[...], a_w)


def _sample_attention(act, cq, ckn, cache_k, cache_v, lfc, x2, ada, wo, fg, *, ln, n_heads):
    rows, d = x2.shape
    nbatch = rows // ln
    a_w = n_heads * HEAD_DIM
    past = cache_k.shape[2]
    row_blk = lambda w: pl.BlockSpec((ln, w), lambda b: (b, 0))
    per_b = lambda s: pl.BlockSpec((1,) + s, lambda b: (b, 0, 0))
    cache_part = lambda c: pl.BlockSpec((1, a_w // CACHE_SPLIT, past), lambda b: (b, c, 0))
    assert CACHE_SPLIT == 2 and (n_heads // PAIR) % CACHE_SPLIT == 0
    return pl.pallas_call(
        functools.partial(_sattn_kernel, n_heads=n_heads),
        grid=(nbatch,),
        in_specs=[row_blk(ACT_PARTS * a_w), row_blk(n_heads), per_b((n_heads, ln)),
                  cache_part(0), cache_part(1), cache_part(0), cache_part(1), per_b((n_heads, past)),
                  row_blk(d), _resident(ada.shape), _resident(wo.shape), _resident((1, d))],
        out_specs=row_blk(d),
        out_shape=jax.ShapeDtypeStruct((rows, d), F32),
        compiler_params=pltpu.CompilerParams(dimension_semantics=("arbitrary",), vmem_limit_bytes=VMEM_LIMIT),
        name="sattn",
    )(act, cq, ckn, cache_k, cache_k, cache_v, cache_v, lfc, x2, ada, wo, fg)


def kernel(x_prompt, x_sample, c_prompt, c_sample, cache_k, cache_v, cache_logf, state_pool, norm_g, w_ada, b_ada,
           w_in, b_f, w_pool, pool_scale, w_out, final_g):
    depth = norm_g.shape[0]
    assert depth == 1
    bp, seq, d = x_prompt.shape
    bs, ln, _ = x_sample.shape
    assert bp == 1
    n_heads = cache_k.shape[3]
    past = cache_k.shape[2]
    a_w = n_heads * HEAD_DIM
    pw = state_pool.shape[3]
    assert pw == len(POOL_WINDOWS) * LANES and cache_k.shape[4] == HEAD_DIM and n_heads <= 8

    ada = _ada_terms(c_prompt, c_sample, w_ada[0], b_ada)

    wit = w_in[0].T
    wp = w_pool[0]
    ps = pool_scale[0][None, :]
    wo = w_out[0]
    ng = norm_g[0][None, :]
    fg = final_g[None, :]

    assert seq % PROJ_ROWS == 0 and seq % ATTN_ROWS == 0
    xp2 = x_prompt.reshape(seq, d)
    hist_p = jnp.zeros((1, HIST_PAD, pw), F32)
    (act_p, k_p, v_p, lf_p, cq_p, nq_p, ck_p, tot, qmx, kg, ho_p) = _project(
        xp2, ada, ng, wit, b_f[0], wp, ps, hist_p,
        bm=PROJ_ROWS, sb=ATT_BLK, segs=1, ada_row=ADA_PROMPT_ROW, start_pos=0, n_heads=n_heads, kv_head_major=False)
    y_p = _prompt_attention(tot, qmx, kg, act_p, cq_p, nq_p, ck_p, xp2, ada, wo, fg, tm=ATTN_ROWS, n_heads=n_heads)

    xs2 = x_sample.reshape(bs * ln, d)
    hist_s = jnp.pad(state_pool[0], ((0, 0), (HIST_PAD - POOL_HIST, 0), (0, 0)))
    (act_s, k_s, v_s, lf_s, cq_s, _, ck_s, _, _, _, ho_s) = _project(
        xs2, ada, ng, wit, b_f[0], wp, ps, hist_s,
        bm=bs * ln, sb=ln, segs=bs, ada_row=ADA_SAMPLE_ROW, start_pos=past, n_heads=n_heads, kv_head_major=True)
    lfc = jnp.swapaxes(cache_logf[0], 1, 2)
    ckt = jnp.transpose(cache_k[0], (0, 2, 3, 1)).reshape(bs, a_w, past)
    cvt = jnp.transpose(cache_v[0], (0, 2, 3, 1)).reshape(bs, a_w, past)
    y_s = _sample_attention(act_s, cq_s, ck_s, ckt, cvt, lfc, xs2, ada, wo, fg, ln=ln, n_heads=n_heads)

    hd = (n_heads, HEAD_DIM)
    seq_minor = lambda t: jnp.transpose(t.reshape(hd + (bp, seq)), (2, 3, 0, 1))[None]
    return (y_p.reshape(bp, seq, d), y_s.reshape(bs, ln, d),
            seq_minor(k_p), seq_minor(v_p), jnp.swapaxes(lf_p, 1, 2)[None],
            ho_p[:, 16 - POOL_HIST:, :][None],
            k_s.reshape((1, bs, ln) + hd), v_s.reshape((1, bs, ln) + hd), jnp.swapaxes(lf_s, 1, 2)[None],
            ho_s[:, 16 - POOL_HIST:, :][None])
```

```python
import functools

import jax
import jax.numpy as jnp
from jax import lax
from jax.experimental import pallas as pl
from jax.experimental.pallas import tpu as pltpu

HEAD_DIM = 64
POOL_WINDOWS = (2, 4, 8, 16)
EPS = 1e-6

LANES = 128
PAIR = LANES // HEAD_DIM
ATT_BLK = 128
NEAR_BLOCKS = 3
LOG2E = 1.4426950408889634
EXP2_UNDERFLOW = 151.0
EXP2_SAFE_SPAN = 100.0
NORM_SLACK = 1.01
HIST_PAD = 32
POOL_HIST = max(POOL_WINDOWS) - 1
NEG = -1e30
VMEM_LIMIT = 60 * 1024 * 1024
PROJ_ROWS = 8 * ATT_BLK
ATTN_ROWS = 4 * ATT_BLK

F32 = jnp.float32
BF16 = jnp.bfloat16


def _silu(x):
    return x * jax.nn.sigmoid(x)


def _dot(a, b):
    return jnp.dot(a, b, preferred_element_type=F32)


def _dot_nt(a, b):
    return lax.dot_general(a, b, (((1,), (1,)), ((), ())), preferred_element_type=F32)


def _lane_cumsum(x, n):
    lane = lax.broadcasted_iota(jnp.int32, x.shape, 1) % LANES
    shift = 1
    while shift < n:
        x = x + jnp.where(lane >= shift, pltpu.roll(x, shift, 1), 0.0)
        shift *= 2
    return x


def _rows_to_lanes(x, n):
    rows = x.shape[0]
    if rows < LANES:
        x = jnp.concatenate([x, jnp.zeros((LANES - rows, LANES), x.dtype)], axis=0)
    return x.T[0:n, 0:rows]


def _lanes_to_rows(x):
    n, rows = x.shape
    if rows < LANES:
        x = jnp.concatenate([x, jnp.zeros((n, LANES - rows), x.dtype)], axis=1)
    x = jnp.concatenate([x, jnp.zeros((LANES - n, LANES), x.dtype)], axis=0)
    return x.T[0:rows, :]


ACT_PARTS = 5


def _act_views(act_ref, width):
    return [act_ref.at[:, p * width:(p + 1) * width] for p in range(ACT_PARTS)]


def _resident(shape):
    return pl.BlockSpec(shape, lambda *_: (0,) * len(shape), pipeline_mode=pl.Buffered(1))


ADA_ROWS = 16
ADA_PROMPT_ROW, ADA_SAMPLE_ROW = 0, 8


def _ada_kernel(cp_ref, cs_ref, w_ref, b_ref, o_ref):
    ap = jnp.broadcast_to(_silu(cp_ref[...]), (ADA_SAMPLE_ROW, cp_ref.shape[1]))
    a = jnp.concatenate([ap, _silu(cs_ref[...])], axis=0).astype(BF16)
    o_ref[0] = _dot(a, w_ref[...].astype(BF16)) + b_ref[...]


def _ada_terms(c_prompt, c_sample, w_ada, b_ada):
    d = c_prompt.shape[1]
    assert c_prompt.shape[0] == 1 and c_sample.shape[0] == ADA_ROWS - ADA_SAMPLE_ROW and w_ada.shape[1] == 3 * d
    return pl.pallas_call(
        _ada_kernel,
        grid=(3,),
        in_specs=[pl.BlockSpec(c_prompt.shape, lambda j: (0, 0)),
                  pl.BlockSpec(c_sample.shape, lambda j: (0, 0)),
                  pl.BlockSpec((d, d), lambda j: (0, j)),
                  pl.BlockSpec((1, d), lambda j: (0, j))],
        out_specs=pl.BlockSpec((1, ADA_ROWS, d), lambda j: (j, 0, 0)),
        out_shape=jax.ShapeDtypeStruct((3, ADA_ROWS, d), F32),
        compiler_params=pltpu.CompilerParams(dimension_semantics=("arbitrary",), vmem_limit_bytes=VMEM_LIMIT),
        name="ada",
    )(c_prompt, c_sample, w_ada, b_ada)


def _proj_kernel(bf_ref, x_ref, ada_ref, ng_ref, w_ref, wp_ref, ps_ref, h0_ref,
                 act_ref, k32_ref, v32_ref, lf_ref, cq_ref, nq_ref, ck_ref,
                 tot_ref, qmx_ref, kg_ref, ho_ref,
                 e_ref, t2_ref, t4_ref, t8_ref, kmx_ref, *, bm, sb, segs, ada_row, start_pos, n_heads, kv_head_major):
    a_w = n_heads * HEAD_DIM
    q_ref, kb_ref, vb_ref, sa_ref, zp_ref = _act_views(act_ref, a_w)
    seg_rows = bm // segs
    step = pl.program_id(0)
    pw = len(POOL_WINDOWS) * LANES
    sc = LOG2E / (HEAD_DIM ** 0.5)
    o_pool = 4 * a_w + n_heads

    def w_rows(lo, hi):
        return w_ref[lo:hi, :].astype(BF16)

    def normed(lo, hi):
        x = x_ref[lo:hi, :]
        xn = x * lax.rsqrt(jnp.mean(x * x, axis=-1, keepdims=True) + EPS)
        parts = []
        for g in range(lo // seg_rows, (hi - 1) // seg_rows + 1):
            r0, r1 = max(lo, g * seg_rows) - lo, min(hi, (g + 1) * seg_rows) - lo
            parts.append(xn[r0:r1] * (ng_ref[...] * (1.0 + ada_ref[1, ada_row + g:ada_row + g + 1, :]))
                         + ada_ref[0, ada_row + g:ada_row + g + 1, :])
        return jnp.concatenate(parts, axis=0).astype(BF16)

    n_lead = 4 if segs == 1 else 1
    hs, pus = [], []
    for c in range(n_lead):
        hs.append(normed(c * bm // n_lead, (c + 1) * bm // n_lead))
        pus.append(_dot_nt(hs[-1], w_rows(o_pool, o_pool + pw)))
    pu = jnp.concatenate(pus, axis=0)
    h = jnp.concatenate(hs, axis=0)
    sel = (lax.broadcasted_iota(jnp.int32, (a_w, LANES), 0) // HEAD_DIM
           == lax.broadcasted_iota(jnp.int32, (a_w, LANES), 1)).astype(BF16)

    def store_kv(ref32, refb, p):
        refb[...] = p.astype(BF16)
        if kv_head_major:
            for hd in range(n_heads):
                ref32[:, hd, :] = p[:, hd * HEAD_DIM:(hd + 1) * HEAD_DIM]
        else:
            ref32[...] = p.T

    ext = HIST_PAD + seg_rows
    n = segs * ext

    def load_history():
        for g in range(segs):
            e_ref[g * ext:g * ext + HIST_PAD, :] = h0_ref[g]

    if segs > 1:
        load_history()
    else:
        pl.when(step == 0)(load_history)

    for g in range(segs):
        e_ref[g * ext + HIST_PAD:(g + 1) * ext, :] = pu[g * seg_rows:(g + 1) * seg_rows]
    t2_ref[8:n, :] = e_ref[8:n, :] + e_ref[7:n - 1, :]
    t4_ref[16:n, :] = t2_ref[16:n, LANES:] + t2_ref[14:n - 2, LANES:]
    t8_ref[24:n, :] = t4_ref[24:n, LANES:] + t4_ref[20:n - 4, LANES:]

    def seg_rows_of(ref, cols, back=0):
        return jnp.concatenate([ref[g * ext + HIST_PAD - back:(g + 1) * ext - back, cols] for g in range(segs)], axis=0)

    lane0, lane1 = slice(0, LANES), slice(LANES, 2 * LANES)
    sums = [seg_rows_of(t2_ref, lane0), seg_rows_of(t4_ref, lane0), seg_rows_of(t8_ref, lane0),
            seg_rows_of(t8_ref, lane1) + seg_rows_of(t8_ref, lane1, back=8)]
    row = lax.broadcasted_iota(jnp.int32, (bm, 1), 0)
    pos1 = start_pos + 1 + (step * bm + row if segs == 1 else row % seg_rows)
    pool_d = []
    for g, w in enumerate(POOL_WINDOWS):
        rc = 1.0 / jnp.minimum(pos1, w).astype(F32)
        pool_d.append((sums[g] * rc - pu[:, g * LANES:(g + 1) * LANES]).astype(BF16))
    for g in range(segs):
        ho_ref[g] = e_ref[(g + 1) * ext - 16:(g + 1) * ext, :]
    if segs == 1:
        e_ref[0:HIST_PAD, :] = e_ref[bm:n, :]

    pk = _dot_nt(h, w_rows(a_w, 2 * a_w))
    store_kv(k32_ref, kb_ref, pk)
    nk = jnp.sqrt(_dot((pk * pk).astype(BF16), sel)) * NORM_SLACK
    spg = _silu(_dot_nt(h, w_rows(o_pool + pw, o_pool + 2 * pw)))

    lane = lax.broadcasted_iota(jnp.int32, (1, LANES), 1)
    bias = jnp.zeros((1, LANES), F32)
    for hd in range(n_heads):
        bias = jnp.where(lane == hd, bf_ref[hd], bias)
    wf = jnp.concatenate([w_ref[4 * a_w:o_pool, :], jnp.zeros((LANES - n_heads, w_ref.shape[1]), F32)], axis=0)
    z = _dot_nt(h, wf.astype(BF16)) + bias
    lf = jnp.minimum(z, 0.0) - jnp.log1p(jnp.exp(-jnp.abs(z)))

    sa_ref[...] = _silu(_dot_nt(h, w_rows(3 * a_w, 4 * a_w))).astype(BF16)

    lfts = []
    for s in range(bm // sb):
        lft = _rows_to_lanes(lf[s * sb:(s + 1) * sb], n_heads)
        off = (s * sb) % seg_rows
        lf_ref[(s * sb) // seg_rows, :, off:off + sb] = lft
        lfts.append(lft if sb == LANES else jnp.concatenate([lft, jnp.zeros((n_heads, LANES - sb), F32)], axis=1))
    c_all = _lane_cumsum(jnp.concatenate(lfts, axis=1) * LOG2E, sb)
    tots, kmx = [], []
    for s in range(bm // sb):
        rows = slice(s * sb, (s + 1) * sb)
        c = c_all[:, s * LANES:(s + 1) * LANES]
        ck_ref[s] = c[:, 0:sb]
        cb = _lanes_to_rows(c)[0:sb]
        cq_ref[rows, :] = cb[:, 0:n_heads]
        tots.append(cb[sb - 1:sb, :])
        kmx.append(jnp.max(nk[rows], axis=0, keepdims=True))
    tot_ref[...] = jnp.concatenate(tots, axis=0)[:, 0:n_heads]
    kmax = kmx[0]
    for t in kmx[1:]:
        kmax = jnp.maximum(kmax, t)

    store_kv(v32_ref, vb_ref, _dot_nt(h, w_rows(2 * a_w, 3 * a_w)))

    zero_w = jnp.zeros((LANES, LANES), BF16)
    for g in range(0, len(POOL_WINDOWS), 2):
        cols = slice(g * LANES, (g + 2) * LANES)
        w2 = jnp.concatenate([jnp.concatenate([wp_ref[g].astype(BF16), zero_w], axis=1),
                              jnp.concatenate([zero_w, wp_ref[g + 1].astype(BF16)], axis=1)], axis=0)
        y = _dot(jnp.concatenate([pool_d[g], pool_d[g + 1]], axis=1), w2) * ps_ref[:, cols]
        zp_ref[:, cols] = (y * spg[:, cols]).astype(BF16)

    qs = _dot_nt(h, w_rows(0, a_w)) * sc
    q_ref[...] = qs.astype(BF16)
    nq = jnp.sqrt(_dot((qs * qs).astype(BF16), sel)) * NORM_SLACK
    nq_ref[...] = nq[:, 0:n_heads]
    qmx_ref[...] = jnp.concatenate([jnp.max(nq[s * sb:(s + 1) * sb], axis=0, keepdims=True)
                                    for s in range(bm // sb)], axis=0)[:, 0:n_heads]

    @pl.when(step > 0)
    def _():
        kmx_ref[...] = jnp.maximum(kmx_ref[...], kmax)

    @pl.when(step == 0)
    def _():
        kmx_ref[...] = kmax

    kg_ref[...] = kmx_ref[...]


def _project(x2, ada, norm_g, wit, b_f, wp, ps, hist0, *, bm, sb, segs, ada_row, start_pos, n_heads,
             kv_head_major):
    rows, d = x2.shape
    a_w = n_heads * HEAD_DIM
    pw = len(POOL_WINDOWS) * LANES
    n_steps = rows // bm
    assert segs == 1 or n_steps == 1
    n_streams = segs
    seg_rows = bm // segs
    nsb = bm // sb
    assert nsb == 8 and seg_rows % sb == 0
    row_blk = lambda w: pl.BlockSpec((bm, w), lambda i, *_: (i, 0))
    per_stream = lambda r, w: pl.BlockSpec((segs, r, w), lambda i, *_: (0, 0, 0))
    kern = functools.partial(_proj_kernel, bm=bm, sb=sb, segs=segs, ada_row=ada_row, start_pos=start_pos,
                             n_heads=n_heads, kv_head_major=kv_head_major)
    if kv_head_major:
        kv_shape = (rows, n_heads, HEAD_DIM)
        kv_blk = pl.BlockSpec((bm, n_heads, HEAD_DIM), lambda i, *_: (i, 0, 0))
    else:
        kv_shape = (a_w, rows)
        kv_blk = pl.BlockSpec((a_w, bm), lambda i, *_: (0, i))
    assert pw == a_w
    out_shape = (
        jax.ShapeDtypeStruct((rows, ACT_PARTS * a_w), BF16),
        jax.ShapeDtypeStruct(kv_shape, F32),
        jax.ShapeDtypeStruct(kv_shape, F32),
        jax.ShapeDtypeStruct((n_streams, n_heads, rows // n_streams), F32),
        jax.ShapeDtypeStruct((rows, n_heads), F32),
        jax.ShapeDtypeStruct((rows, n_heads), F32),
        jax.ShapeDtypeStruct((rows // sb, n_heads, sb), F32),
        jax.ShapeDtypeStruct((rows // sb, n_heads), F32),
        jax.ShapeDtypeStruct((rows // sb, n_heads), F32),
        jax.ShapeDtypeStruct((1, LANES), F32),
        jax.ShapeDtypeStruct((n_streams, 16, pw), F32),
    )
    out_specs = (
        row_blk(ACT_PARTS * a_w), kv_blk, kv_blk,
        pl.BlockSpec((segs, n_heads, seg_rows), lambda i, *_: (0, 0, i)),
        row_blk(n_heads), row_blk(n_heads),
        pl.BlockSpec((nsb, n_heads, sb), lambda i, *_: (i, 0, 0)),
        pl.BlockSpec((nsb, n_heads), lambda i, *_: (i, 0)),
        pl.BlockSpec((nsb, n_heads), lambda i, *_: (i, 0)),
        pl.BlockSpec((1, LANES), lambda i, *_: (0, 0)),
        per_stream(16, pw),
    )
    in_specs = [
        row_blk(d),
        _resident(ada.shape),
        _resident((1, d)),
        _resident(wit.shape), _resident(wp.shape), _resident(ps.shape),
        per_stream(HIST_PAD, pw),
    ]
    return pl.pallas_call(
        kern,
        grid_spec=pltpu.PrefetchScalarGridSpec(
            num_scalar_prefetch=1,
            grid=(n_steps,),
            in_specs=in_specs,
            out_specs=out_specs,
            scratch_shapes=[pltpu.VMEM((bm + segs * HIST_PAD, pw), F32),
                            pltpu.VMEM((bm + segs * HIST_PAD, pw), F32),
                            pltpu.VMEM((bm + segs * HIST_PAD, pw - LANES), F32),
                            pltpu.VMEM((bm + segs * HIST_PAD, pw - 2 * LANES), F32),
                            pltpu.VMEM((1, LANES), F32)]),
        out_shape=out_shape,
        compiler_params=pltpu.CompilerParams(dimension_semantics=("arbitrary",), vmem_limit_bytes=VMEM_LIMIT),
        name="proj",
    )(b_f, x2, ada, norm_g, wit, wp, ps, hist0)


def _merge_norm(x, gate, za, zp, wo_ref, fg, a_w):
    dy = _dot(za, wo_ref[0:a_w, :].astype(BF16)) + _dot(zp, wo_ref[a_w:, :].astype(BF16))
    out = x + gate * dy
    ms = jnp.mean(out * out, axis=-1, keepdims=True)
    return out * lax.rsqrt(ms + EPS) * fg


def _attn_kernel(tot_ref, qmx_ref, kg_ref,
                 act_ref, cq_ref, nq_ref, ck_ref, x_ref, ada_ref, wo_ref, fg_ref,
                 y_ref,
                 k_ref, v_ref, z_ref, m_ref, l_ref, acc_ref, straight_ref, *, tm, n_heads):
    a_w = n_heads * HEAD_DIM
    q_ref, knew_ref, vnew_ref, sa_ref, zp_ref = _act_views(act_ref, a_w)
    n_pairs = n_heads // PAIR
    nsub = tm // ATT_BLK
    w_near = NEAR_BLOCKS
    cw = w_near * ATT_BLK
    step = pl.program_id(0)

    def keep_step_rows():
        k_ref[pl.ds(pl.multiple_of(step * tm, tm), tm), :] = knew_ref[...]
        v_ref[pl.ds(pl.multiple_of(step * tm, tm), tm), :] = vnew_ref[...]

    lo_q = lax.broadcasted_iota(jnp.int32, (ATT_BLK, LANES), 1) < HEAD_DIM
    lo_k = lax.broadcasted_iota(jnp.int32, (cw, LANES), 1) < HEAD_DIM
    col = lax.broadcasted_iota(jnp.int32, (ATT_BLK, cw), 1)
    tri = (lax.broadcasted_iota(jnp.int32, (ATT_BLK, ATT_BLK), 1)
           <= lax.broadcasted_iota(jnp.int32, (ATT_BLK, ATT_BLK), 0))
    tri_bias = jnp.where(tri, 0.0, NEG).astype(F32)
    zeros_k = jnp.zeros((cw, LANES), BF16)
    ind_lo = jnp.where(lo_k, 1.0, 0.0).astype(BF16)
    ind_hi = jnp.where(lo_k, 0.0, 1.0).astype(BF16)

    def tot_at(b, hd):
        return jnp.where(b >= 0, tot_ref[jnp.maximum(b, 0), hd], 0.0)

    def rows_of(ref, blocks, g):
        return jnp.concatenate(
            [ref[pl.ds(pl.multiple_of(b * ATT_BLK, ATT_BLK), ATT_BLK), g * LANES:(g + 1) * LANES] for b in blocks],
            axis=0)

    def step_rows_of(ref, new_ref, sub, g):
        parts = []
        for p in range(w_near):
            rel = sub - (w_near - 1) + p
            if rel >= 0:
                parts.append(new_ref[rel * ATT_BLK:(rel + 1) * ATT_BLK, g * LANES:(g + 1) * LANES])
            else:
                start = pl.multiple_of((step * nsub + rel) * ATT_BLK, ATT_BLK)
                parts.append(ref[pl.ds(start, ATT_BLK), g * LANES:(g + 1) * LANES])
        return jnp.concatenate(parts, axis=0)

    def pair_scores(r0, kc, g):
        keys = jnp.concatenate([jnp.where(lo_k, kc, zeros_k), jnp.where(lo_k, zeros_k, kc)], axis=0)
        return _dot_nt(q_ref[pl.ds(r0, ATT_BLK), g * LANES:(g + 1) * LANES], keys)

    def pair_values(p_pair, vc):
        vals = jnp.concatenate([jnp.concatenate([jnp.where(lo_k, vc, zeros_k), ind_lo], axis=1),
                                jnp.concatenate([jnp.where(lo_k, zeros_k, vc), ind_hi], axis=1)], axis=0)
        return _dot(p_pair, vals)

    def decay_row(blocks, offs, hd):
        return jnp.concatenate([offs[p] - ck_ref[blocks[p], hd:hd + 1, :] for p in range(w_near)], axis=1)

    def near_weights(s_pair, rows, g, near_c, near_offs, bounded):
        ps, ms = [], []
        for e in range(PAIR):
            hd = PAIR * g + e
            cqh = cq_ref[rows, hd:hd + 1]
            if bounded:
                m = nq_ref[rows, hd:hd + 1] * kg_ref[0, hd]
                cqh = cqh - m
            dec = decay_row(near_c, near_offs[hd], hd)
            pieces = []
            for p in range(w_near):
                lanes = slice(e * cw + p * ATT_BLK, e * cw + (p + 1) * ATT_BLK)
                sp = s_pair[:, lanes] + cqh + dec[:, p * ATT_BLK:(p + 1) * ATT_BLK]
                pieces.append(sp + tri_bias if p == w_near - 1 else sp)
            s = jnp.concatenate(pieces, axis=1)
            if not bounded:
                m = jnp.max(s, axis=1, keepdims=True)
                s = s - m
            ms.append(m)
            ps.append(jnp.exp2(s).astype(BF16))
        return jnp.concatenate(ps, axis=1), ms

    def gated_output(rows, g, acc, l):
        cols = slice(g * LANES, (g + 1) * LANES)
        z_ref[rows, cols] = ((acc / l) * sa_ref[rows, cols].astype(F32)).astype(BF16)

    def merge(chunks=1):
        rows_c = tm // chunks
        for c in range(chunks):
            rows = slice(c * rows_c, (c + 1) * rows_c)
            y_ref[rows, :] = _merge_norm(x_ref[rows, :], ada_ref[2, ADA_PROMPT_ROW:ADA_PROMPT_ROW + 1, :],
                                         z_ref[rows, :], zp_ref[rows, :], wo_ref, fg_ref[...], a_w)

    def check_next_step():
        ok = jnp.bool_(True)
        last = qmx_ref.shape[0] - 1
        for sub in range(nsub):
            i = jnp.minimum((step + 1) * nsub + sub, last)
            for hd in range(n_heads):
                qk = 2.0 * qmx_ref[i, hd] * kg_ref[0, hd]
                back = qk
                for dd in range(1, w_near):
                    back = back + tot_ref[i - dd, hd]
                ok = jnp.logical_and(ok, jnp.logical_and(qk <= EXP2_SAFE_SPAN, back < -EXP2_UNDERFLOW))
        straight_ref[0] = ok.astype(jnp.int32)

    def straight_step():
        units = [(sub, g) for sub in range(nsub) for g in range(n_pairs)]
        near_of, offs_of = [], []
        for sub in range(nsub):
            i = step * nsub + sub
            near_of.append([i - (w_near - 1) + p for p in range(w_near)])
            offs = []
            for hd in range(n_heads):
                o, per_piece = jnp.float32(0.0), [jnp.float32(0.0)]
                for dd in range(1, w_near):
                    o = o + tot_ref[i - dd, hd]
                    per_piece.append(o)
                offs.append(per_piece[::-1])
            offs_of.append(offs)
        s_next = pair_scores(0, step_rows_of(k_ref, knew_ref, 0, 0), 0)
        for u, (sub, g) in enumerate(units):
            s_pair = s_next
            if u + 1 < len(units):
                sub1, g1 = units[u + 1]
                s_next = pair_scores(sub1 * ATT_BLK, step_rows_of(k_ref, knew_ref, sub1, g1), g1)
            rows = pl.ds(sub * ATT_BLK, ATT_BLK)
            p_pair, _ = near_weights(s_pair, rows, g, near_of[sub], offs_of[sub], True)
            r = pair_values(p_pair, step_rows_of(v_ref, vnew_ref, sub, g))
            gated_output(rows, g, r[:, 0:LANES], r[:, LANES:])
        check_next_step()
        keep_step_rows()
        merge(chunks=2)

    def sub_body(sub, carry):
        i = step * nsub + sub
        r0 = pl.multiple_of(sub * ATT_BLK, ATT_BLK)

        rows = pl.ds(r0, ATT_BLK)
        qk = [2.0 * qmx_ref[i, hd] * kg_ref[0, hd] for hd in range(n_heads)]

        near = [i - (w_near - 1) + p for p in range(w_near)]
        near_c = [jnp.maximum(b, 0) for b in near]
        near_offs, offs_far = [], []
        for hd in range(n_heads):
            back = [tot_at(i - dd, hd) for dd in range(1, w_near)]
            offs = []
            for p in range(w_near):
                o = jnp.float32(0.0)
                for dd in range(1, w_near - p):
                    o = o + back[dd - 1]
                offs.append(jnp.where(near[p] >= 0, o, NEG))
            near_offs.append(offs)
            o = jnp.float32(0.0)
            for t in back:
                o = o + t
            offs_far.append(o)

        def near_chunk(bounded):
            def fn():
                s_next = pair_scores(r0, rows_of(k_ref, near_c, 0), 0)
                for g in range(n_pairs):
                    s_pair = s_next
                    if g + 1 < n_pairs:
                        s_next = pair_scores(r0, rows_of(k_ref, near_c, g + 1), g + 1)
                    p_pair, ms = near_weights(s_pair, rows, g, near_c, near_offs, bounded)
                    for e in range(PAIR):
                        m_ref[PAIR * g + e] = ms[e]
                    r = pair_values(p_pair, rows_of(v_ref, near_c, g))
                    acc_ref[g] = r[:, 0:LANES]
                    l_ref[g] = r[:, LANES:]
                    gated_output(rows, g, r[:, 0:LANES], r[:, LANES:])
            return fn

        bounded_ok = qk[0] <= EXP2_SAFE_SPAN
        for hd in range(1, n_heads):
            bounded_ok = jnp.logical_and(bounded_ok, qk[hd] <= EXP2_SAFE_SPAN)
        pl.when(bounded_ok)(near_chunk(True))
        pl.when(jnp.logical_not(bounded_ok))(near_chunk(False))

        def far_cond(c):
            top = i - c[0] * w_near
            need = qk[0] + c[1] >= -EXP2_UNDERFLOW
            for hd in range(1, n_heads):
                need = jnp.logical_or(need, qk[hd] + c[1 + hd] >= -EXP2_UNDERFLOW)
            return jnp.logical_and(top >= 0, need)

        def far_body(c):
            top = i - c[0] * w_near
            jc = jnp.maximum(top - (w_near - 1), 0)
            blocks = [jc + p for p in range(w_near)]
            keepc = col < (top + 1 - jc) * ATT_BLK
            new = [c[0] + 1]
            for g in range(n_pairs):
                s_pair = pair_scores(r0, rows_of(k_ref, blocks, g), g)
                ps, alphas = [], []
                for e in range(PAIR):
                    hd = PAIR * g + e
                    tt = [tot_at(top - b, hd) for b in range(w_near)]
                    offs = []
                    for p in range(w_near):
                        behind = top - (jc + p)
                        o = c[1 + hd]
                        for b in range(w_near):
                            o = o + jnp.where(behind >= b, tt[b], 0.0)
                        offs.append(o)
                    s = s_pair[:, e * cw:(e + 1) * cw] + cq_ref[rows, hd:hd + 1] + decay_row(blocks, offs, hd)
                    s = jnp.where(keepc, s, NEG)
                    m_old = m_ref[hd]
                    m_new = jnp.maximum(m_old, jnp.max(s, axis=1, keepdims=True))
                    m_ref[hd] = m_new
                    alphas.append(jnp.broadcast_to(jnp.exp2(m_old - m_new), (ATT_BLK, LANES)))
                    ps.append(jnp.exp2(s - m_new).astype(BF16))
                    o = c[1 + hd]
                    for t in tt:
                        o = o + t
                    new.append(o)
                alpha = jnp.where(lo_q, alphas[0], alphas[1])
                r = pair_values(jnp.concatenate(ps, axis=1), rows_of(v_ref, blocks, g))
                acc_ref[g] = alpha * acc_ref[g] + r[:, 0:LANES]
                l_ref[g] = alpha * l_ref[g] + r[:, LANES:]
            return tuple(new)

        far = lax.while_loop(far_cond, far_body, (jnp.int32(1),) + tuple(offs_far))

        @pl.when(far[0] > 1)
        def _():
            for g in range(n_pairs):
                gated_output(rows, g, acc_ref[g], l_ref[g])
        return carry

    @pl.when(step == 0)
    def _():
        straight_ref[0] = 0

    straight = straight_ref[0] == 1
    pl.when(straight)(straight_step)

    @pl.when(jnp.logical_not(straight))
    def _():
        keep_step_rows()
        check_next_step()
        lax.fori_loop(0, nsub, sub_body, 0)
        merge()


def _prompt_attention(tot, qmx, kg, act, cq, nq, ck, x2, ada, wo, fg, *, tm, n_heads):
    rows, d = x2.shape
    a_w = n_heads * HEAD_DIM
    n_pairs = n_heads // PAIR
    assert rows % tm == 0 and tm % ATT_BLK == 0 and tm // ATT_BLK >= NEAR_BLOCKS - 1 and n_heads % PAIR == 0
    row_blk = lambda w: pl.BlockSpec((tm, w), lambda i, *_: (i, 0))
    grid_spec = pltpu.PrefetchScalarGridSpec(
        num_scalar_prefetch=3,
        grid=(rows // tm,),
        in_specs=[row_blk(ACT_PARTS * a_w), row_blk(n_heads), row_blk(n_heads), _resident(ck.shape), row_blk(d),
                  _resident(ada.shape), _resident(wo.shape), _resident((1, d))],
        out_specs=row_blk(d),
        scratch_shapes=[pltpu.VMEM((rows, a_w), BF16),
                        pltpu.VMEM((rows, a_w), BF16),
                        pltpu.VMEM((tm, a_w), BF16),
                        pltpu.VMEM((n_heads, ATT_BLK, 1), F32),
                        pltpu.VMEM((n_pairs, ATT_BLK, LANES), F32),
                        pltpu.VMEM((n_pairs, ATT_BLK, LANES), F32),
                        pltpu.SMEM((1,), jnp.int32)],
    )
    return pl.pallas_call(
        functools.partial(_attn_kernel, tm=tm, n_heads=n_heads),
        grid_spec=grid_spec,
        out_shape=jax.ShapeDtypeStruct((rows, d), F32),
        compiler_params=pltpu.CompilerParams(dimension_semantics=("arbitrary",), vmem_limit_bytes=VMEM_LIMIT),
        name="attn",
    )(tot, qmx, kg, act, cq, nq, ck, x2, ada, wo, fg)


CACHE_SPLIT = 2


def _sattn_kernel(act_ref, cq_ref, ckn_ref, kc0_ref, kc1_ref, vc0_ref, vc1_ref, lfc_ref,
                  x_ref, ada_ref, wo_ref, fg_ref, y_ref, *, n_heads):
    a_w = n_heads * HEAD_DIM
    q_ref, kn_ref, vn_ref, sa_ref, zp_ref = _act_views(act_ref, a_w)
    ln = act_ref.shape[0]
    past = kc0_ref.shape[2]
    kc_refs, vc_refs = (kc0_ref, kc1_ref), (vc0_ref, vc1_ref)
    pairs_per_part = n_heads // PAIR // CACHE_SPLIT

    def cached(refs, g):
        lo = (g % pairs_per_part) * LANES
        return refs[g // pairs_per_part][0, lo:lo + LANES, :].astype(BF16)

    nb = past // LANES
    lane = lax.broadcasted_iota(jnp.int32, (ln, LANES), 1)
    half = [lane < HEAD_DIM, lane >= HEAD_DIM]
    causal = lax.broadcasted_iota(jnp.int32, (ln, ln), 1) <= lax.broadcasted_iota(jnp.int32, (ln, ln), 0)
    n_pairs = n_heads // PAIR

    def pair_scores(g):
        cols = slice(g * LANES, (g + 1) * LANES)
        q2 = q_ref[:, cols]
        qst = jnp.concatenate([jnp.where(half[e], q2, jnp.zeros_like(q2)) for e in range(PAIR)], axis=0)
        return _dot(qst, cached(kc_refs, g)), _dot_nt(qst, kn_ref[:, cols])

    outs = []
    s_next = pair_scores(0)

    lfc = lfc_ref[0] * LOG2E
    cs = _lane_cumsum(lfc, LANES)
    after = jnp.zeros((n_heads, 1), F32)
    suffix = [None] * nb
    for b in reversed(range(nb)):
        cb = cs[:, b * LANES:(b + 1) * LANES]
        tot = cb[:, LANES - 1:LANES]
        suffix[b] = (tot - cb) + after
        after = after + tot
    dec_c = jnp.concatenate(suffix, axis=1)

    for g in range(n_pairs):
        cols = slice(g * LANES, (g + 1) * LANES)
        sc_st, sn_st = s_next
        if g + 1 < n_pairs:
            s_next = pair_scores(g + 1)
        vct = cached(vc_refs, g)
        vn = vn_ref[:, cols]
        pc, pn, ls = [], [], []
        for e in range(PAIR):
            hd = PAIR * g + e
            rows = slice(e * ln, (e + 1) * ln)
            cqh = cq_ref[:, hd:hd + 1]
            s_c = sc_st[rows] + cqh + dec_c[hd:hd + 1, :]
            s_n = jnp.where(causal, sn_st[rows] + cqh - ckn_ref[0, hd:hd + 1, :], NEG)
            m = jnp.maximum(jnp.max(s_c, axis=1, keepdims=True), jnp.max(s_n, axis=1, keepdims=True))
            p_c = jnp.exp2(s_c - m)
            p_n = jnp.exp2(s_n - m)
            ls.append(jnp.sum(p_c, axis=1, keepdims=True) + jnp.sum(p_n, axis=1, keepdims=True))
            pc.append(p_c.astype(BF16))
            pn.append(p_n.astype(BF16))
        acc = _dot_nt(jnp.concatenate(pc, axis=0), vct) + _dot(jnp.concatenate(pn, axis=0), vn)
        o = jnp.where(half[0], acc[0:ln] / ls[0], acc[ln:2 * ln] / ls[1])
        outs.append((o * sa_ref[:, cols].astype(F32)).astype(BF16))
    za = jnp.concatenate(outs, axis=1)
    y_ref[...] = _merge_norm(x_ref[...], ada_ref[2, pl.ds(ADA_SAMPLE_ROW + pl.program_id(0), 1), :], za, zp_ref[...], wo_ref, fg_ref[...], a_w)


def _sample_attention(act, cq, ckn, cache_k, cache_v, lfc, x2, ada, wo, fg, *, ln, n_heads):
    rows, d = x2.shape
    nbatch = rows // ln
    a_w = n_heads * HEAD_DIM
    past = cache_k.shape[2]
    row_blk = lambda w: pl.BlockSpec((ln, w), lambda b: (b, 0))
    per_b = lambda s: pl.BlockSpec((1,) + s, lambda b: (b, 0, 0))
    cache_part = lambda c: pl.BlockSpec((1, a_w // CACHE_SPLIT, past), lambda b: (b, c, 0))
    assert CACHE_SPLIT == 2 and (n_heads // PAIR) % CACHE_SPLIT == 0
    return pl.pallas_call(
        functools.partial(_sattn_kernel, n_heads=n_heads),
        grid=(nbatch,),
        in_specs=[row_blk(ACT_PARTS * a_w), row_blk(n_heads), per_b((n_heads, ln)),
                  cache_part(0), cache_part(1), cache_part(0), cache_part(1), per_b((n_heads, past)),
                  row_blk(d), _resident(ada.shape), _resident(wo.shape), _resident((1, d))],
        out_specs=row_blk(d),
        out_shape=jax.ShapeDtypeStruct((rows, d), F32),
        compiler_params=pltpu.CompilerParams(dimension_semantics=("arbitrary",), vmem_limit_bytes=VMEM_LIMIT),
        name="sattn",
    )(act, cq, ckn, cache_k, cache_k, cache_v, cache_v, lfc, x2, ada, wo, fg)


def kernel(x_prompt, x_sample, c_prompt, c_sample, cache_k, cache_v, cache_logf, state_pool, norm_g, w_ada, b_ada,
           w_in, b_f, w_pool, pool_scale, w_out, final_g):
    depth = norm_g.shape[0]
    assert depth == 1
    bp, seq, d = x_prompt.shape
    bs, ln, _ = x_sample.shape
    assert bp == 1
    n_heads = cache_k.shape[3]
    past = cache_k.shape[2]
    a_w = n_heads * HEAD_DIM
    pw = state_pool.shape[3]
    assert pw == len(POOL_WINDOWS) * LANES and cache_k.shape[4] == HEAD_DIM and n_heads <= 8

    ada = _ada_terms(c_prompt, c_sample, w_ada[0], b_ada)

    wit = w_in[0].T
    wp = w_pool[0]
    ps = pool_scale[0][None, :]
    wo = w_out[0]
    ng = norm_g[0][None, :]
    fg = final_g[None, :]

    assert seq % PROJ_ROWS == 0 and seq % ATTN_ROWS == 0
    xp2 = x_prompt.reshape(seq, d)
    hist_p = jnp.zeros((1, HIST_PAD, pw), F32)
    (act_p, k_p, v_p, lf_p, cq_p, nq_p, ck_p, tot, qmx, kg, ho_p) = _project(
        xp2, ada, ng, wit, b_f[0], wp, ps, hist_p,
        bm=PROJ_ROWS, sb=ATT_BLK, segs=1, ada_row=ADA_PROMPT_ROW, start_pos=0, n_heads=n_heads, kv_head_major=False)
    y_p = _prompt_attention(tot, qmx, kg, act_p, cq_p, nq_p, ck_p, xp2, ada, wo, fg, tm=ATTN_ROWS, n_heads=n_heads)

    xs2 = x_sample.reshape(bs * ln, d)
    hist_s = jnp.pad(state_pool[0], ((0, 0), (HIST_PAD - POOL_HIST, 0), (0, 0)))
    (act_s, k_s, v_s, lf_s, cq_s, _, ck_s, _, _, _, ho_s) = _project(
        xs2, ada, ng, wit, b_f[0], wp, ps, hist_s,
        bm=bs * ln, sb=ln, segs=bs, ada_row=ADA_SAMPLE_ROW, start_pos=past, n_heads=n_heads, kv_head_major=True)
    lfc = jnp.swapaxes(cache_logf[0], 1, 2)
    ckt = jnp.transpose(cache_k[0], (0, 2, 3, 1)).reshape(bs, a_w, past)
    cvt = jnp.transpose(cache_v[0], (0, 2, 3, 1)).reshape(bs, a_w, past)
    y_s = _sample_attention(act_s, cq_s, ck_s, ckt, cvt, lfc, xs2, ada, wo, fg, ln=ln, n_heads=n_heads)

    hd = (n_heads, HEAD_DIM)
    seq_minor = lambda t: jnp.transpose(t.reshape(hd + (bp, seq)), (2, 3, 0, 1))[None]
    return (y_p.reshape(bp, seq, d), y_s.reshape(bs, ln, d),
            seq_minor(k_p), seq_minor(v_p), jnp.swapaxes(lf_p, 1, 2)[None],
            ho_p[:, 16 - POOL_HIST:, :][None],
            k_s.reshape((1, bs, ln) + hd), v_s.reshape((1, bs, ln) + hd), jnp.swapaxes(lf_s, 1, 2)[None],
            ho_s[:, 16 - POOL_HIST:, :][None])
```

```python
import functools

import jax
import jax.numpy as jnp
from jax import lax
from jax.experimental import pallas as pl
from jax.experimental.pallas import tpu as pltpu

HEAD_DIM = 64
POOL_WINDOWS = (2, 4, 8, 16)
EPS = 1e-6

LANES = 128
PAIR = LANES // HEAD_DIM
ATT_BLK = 128
NEAR_BLOCKS = 3
LOG2E = 1.4426950408889634
EXP2_UNDERFLOW = 151.0
EXP2_SAFE_SPAN = 100.0
NORM_SLACK = 1.01
HIST_PAD = 32
POOL_HIST = max(POOL_WINDOWS) - 1
NEG = -1e30
VMEM_LIMIT = 60 * 1024 * 1024
PROJ_ROWS = 8 * ATT_BLK
ATTN_ROWS = 4 * ATT_BLK

F32 = jnp.float32
BF16 = jnp.bfloat16


def _silu(x):
    return x * jax.nn.sigmoid(x)


def _dot(a, b):
    return jnp.dot(a, b, preferred_element_type=F32)


def _dot_nt(a, b):
    return lax.dot_general(a, b, (((1,), (1,)), ((), ())), preferred_element_type=F32)


def _lane_cumsum(x, n):
    lane = lax.broadcasted_iota(jnp.int32, x.shape, 1) % LANES
    shift = 1
    while shift < n:
        x = x + jnp.where(lane >= shift, pltpu.roll(x, shift, 1), 0.0)
        shift *= 2
    return x


def _rows_to_lanes(x, n):
    rows = x.shape[0]
    if rows < LANES:
        x = jnp.concatenate([x, jnp.zeros((LANES - rows, LANES), x.dtype)], axis=0)
    return x.T[0:n, 0:rows]


def _lanes_to_rows(x):
    n, rows = x.shape
    if rows < LANES:
        x = jnp.concatenate([x, jnp.zeros((n, LANES - rows), x.dtype)], axis=1)
    x = jnp.concatenate([x, jnp.zeros((LANES - n, LANES), x.dtype)], axis=0)
    return x.T[0:rows, :]


ACT_PARTS = 5


def _act_views(act_ref, width):
    return [act_ref.at[:, p * width:(p + 1) * width] for p in range(ACT_PARTS)]


def _resident(shape):
    return pl.BlockSpec(shape, lambda *_: (0,) * len(shape), pipeline_mode=pl.Buffered(1))


ADA_ROWS = 16
ADA_PROMPT_ROW, ADA_SAMPLE_ROW = 0, 8


def _ada_kernel(cp_ref, cs_ref, w_ref, b_ref, o_ref):
    ap = jnp.broadcast_to(_silu(cp_ref[...]), (ADA_SAMPLE_ROW, cp_ref.shape[1]))
    a = jnp.concatenate([ap, _silu(cs_ref[...])], axis=0).astype(BF16)
    o_ref[0] = _dot(a, w_ref[...].astype(BF16)) + b_ref[...]


def _ada_terms(c_prompt, c_sample, w_ada, b_ada):
    d = c_prompt.shape[1]
    assert c_prompt.shape[0] == 1 and c_sample.shape[0] == ADA_ROWS - ADA_SAMPLE_ROW and w_ada.shape[1] == 3 * d
    return pl.pallas_call(
        _ada_kernel,
        grid=(3,),
        in_specs=[pl.BlockSpec(c_prompt.shape, lambda j: (0, 0)),
                  pl.BlockSpec(c_sample.shape, lambda j: (0, 0)),
                  pl.BlockSpec((d, d), lambda j: (0, j)),
                  pl.BlockSpec((1, d), lambda j: (0, j))],
        out_specs=pl.BlockSpec((1, ADA_ROWS, d), lambda j: (j, 0, 0)),
        out_shape=jax.ShapeDtypeStruct((3, ADA_ROWS, d), F32),
        compiler_params=pltpu.CompilerParams(dimension_semantics=("arbitrary",), vmem_limit_bytes=VMEM_LIMIT),
        name="ada",
    )(c_prompt, c_sample, w_ada, b_ada)


def _proj_kernel(bf_ref, x_ref, ada_ref, ng_ref, w_ref, wp_ref, ps_ref, h0_ref,
                 act_ref, k32_ref, v32_ref, lf_ref, cq_ref, ck_ref,
                 tot_ref, qmx_ref, kg_ref, ho_ref,
                 e_ref, t2_ref, t4_ref, t8_ref, kmx_ref, *, bm, sb, segs, ada_row, start_pos, n_heads, kv_head_major):
    a_w = n_heads * HEAD_DIM
    q_ref, kb_ref, vb_ref, sa_ref, zp_ref = _act_views(act_ref, a_w)
    seg_rows = bm // segs
    step = pl.program_id(0)
    pw = len(POOL_WINDOWS) * LANES
    sc = LOG2E / (HEAD_DIM ** 0.5)
    o_pool = 4 * a_w + n_heads

    def w_rows(lo, hi):
        return w_ref[lo:hi, :].astype(BF16)

    def normed(lo, hi):
        x = x_ref[lo:hi, :]
        xn = x * lax.rsqrt(jnp.mean(x * x, axis=-1, keepdims=True) + EPS)
        parts = []
        for g in range(lo // seg_rows, (hi - 1) // seg_rows + 1):
            r0, r1 = max(lo, g * seg_rows) - lo, min(hi, (g + 1) * seg_rows) - lo
            parts.append(xn[r0:r1] * (ng_ref[...] * (1.0 + ada_ref[1, ada_row + g:ada_row + g + 1, :]))
                         + ada_ref[0, ada_row + g:ada_row + g + 1, :])
        return jnp.concatenate(parts, axis=0).astype(BF16)

    n_lead = 4 if segs == 1 else 1
    hs, pus = [], []
    for c in range(n_lead):
        hs.append(normed(c * bm // n_lead, (c + 1) * bm // n_lead))
        pus.append(_dot_nt(hs[-1], w_rows(o_pool, o_pool + pw)))
    pu = jnp.concatenate(pus, axis=0)
    h = jnp.concatenate(hs, axis=0)
    sel = (lax.broadcasted_iota(jnp.int32, (a_w, LANES), 0) // HEAD_DIM
           == lax.broadcasted_iota(jnp.int32, (a_w, LANES), 1)).astype(BF16)

    def store_kv(ref32, refb, p):
        refb[...] = p.astype(BF16)
        if kv_head_major:
            for hd in range(n_heads):
                ref32[:, hd, :] = p[:, hd * HEAD_DIM:(hd + 1) * HEAD_DIM]
        else:
            ref32[...] = p.T

    ext = HIST_PAD + seg_rows
    n = segs * ext

    def load_history():
        for g in range(segs):
            e_ref[g * ext:g * ext + HIST_PAD, :] = h0_ref[g]

    if segs > 1:
        load_history()
    else:
        pl.when(step == 0)(load_history)

    for g in range(segs):
        e_ref[g * ext + HIST_PAD:(g + 1) * ext, :] = pu[g * seg_rows:(g + 1) * seg_rows]
    t2_ref[8:n, :] = e_ref[8:n, :] + e_ref[7:n - 1, :]
    t4_ref[16:n, :] = t2_ref[16:n, LANES:] + t2_ref[14:n - 2, LANES:]
    t8_ref[24:n, :] = t4_ref[24:n, LANES:] + t4_ref[20:n - 4, LANES:]

    def seg_rows_of(ref, cols, back=0):
        return jnp.concatenate([ref[g * ext + HIST_PAD - back:(g + 1) * ext - back, cols] for g in range(segs)], axis=0)

    lane0, lane1 = slice(0, LANES), slice(LANES, 2 * LANES)
    sums = [seg_rows_of(t2_ref, lane0), seg_rows_of(t4_ref, lane0), seg_rows_of(t8_ref, lane0),
            seg_rows_of(t8_ref, lane1) + seg_rows_of(t8_ref, lane1, back=8)]
    row = lax.broadcasted_iota(jnp.int32, (bm, 1), 0)
    pos1 = start_pos + 1 + (step * bm + row if segs == 1 else row % seg_rows)
    pool_d = []
    for g, w in enumerate(POOL_WINDOWS):
        rc = 1.0 / jnp.minimum(pos1, w).astype(F32)
        pool_d.append((sums[g] * rc - pu[:, g * LANES:(g + 1) * LANES]).astype(BF16))
    for g in range(segs):
        ho_ref[g] = e_ref[(g + 1) * ext - 16:(g + 1) * ext, :]
    if segs == 1:
        e_ref[0:HIST_PAD, :] = e_ref[bm:n, :]

    pk = _dot_nt(h, w_rows(a_w, 2 * a_w))
    store_kv(k32_ref, kb_ref, pk)
    nk = jnp.sqrt(_dot((pk * pk).astype(BF16), sel)) * NORM_SLACK
    spg = _silu(_dot_nt(h, w_rows(o_pool + pw, o_pool + 2 * pw)))

    lane = lax.broadcasted_iota(jnp.int32, (1, LANES), 1)
    bias = jnp.zeros((1, LANES), F32)
    for hd in range(n_heads):
        bias = jnp.where(lane == hd, bf_ref[hd], bias)
    wf = jnp.concatenate([w_ref[4 * a_w:o_pool, :], jnp.zeros((LANES - n_heads, w_ref.shape[1]), F32)], axis=0)
    z = _dot_nt(h, wf.astype(BF16)) + bias
    lf = jnp.minimum(z, 0.0) - jnp.log1p(jnp.exp(-jnp.abs(z)))

    sa_ref[...] = _silu(_dot_nt(h, w_rows(3 * a_w, 4 * a_w))).astype(BF16)

    lfts = []
    for s in range(bm // sb):
        lft = _rows_to_lanes(lf[s * sb:(s + 1) * sb], n_heads)
        off = (s * sb) % seg_rows
        lf_ref[(s * sb) // seg_rows, :, off:off + sb] = lft
        lfts.append(lft if sb == LANES else jnp.concatenate([lft, jnp.zeros((n_heads, LANES - sb), F32)], axis=1))
    c_all = _lane_cumsum(jnp.concatenate(lfts, axis=1) * LOG2E, sb)
    tots, kmx = [], []
    for s in range(bm // sb):
        rows = slice(s * sb, (s + 1) * sb)
        c = c_all[:, s * LANES:(s + 1) * LANES]
        ck_ref[s] = c[:, 0:sb]
        cb = _lanes_to_rows(c)[0:sb]
        cq_ref[rows, :] = cb[:, 0:n_heads]
        tots.append(cb[sb - 1:sb, :])
        kmx.append(jnp.max(nk[rows], axis=0, keepdims=True))
    tot_ref[...] = jnp.concatenate(tots, axis=0)[:, 0:n_heads]
    kmax = kmx[0]
    for t in kmx[1:]:
        kmax = jnp.maximum(kmax, t)

    store_kv(v32_ref, vb_ref, _dot_nt(h, w_rows(2 * a_w, 3 * a_w)))

    zero_w = jnp.zeros((LANES, LANES), BF16)
    for g in range(0, len(POOL_WINDOWS), 2):
        cols = slice(g * LANES, (g + 2) * LANES)
        w2 = jnp.concatenate([jnp.concatenate([wp_ref[g].astype(BF16), zero_w], axis=1),
                              jnp.concatenate([zero_w, wp_ref[g + 1].astype(BF16)], axis=1)], axis=0)
        y = _dot(jnp.concatenate([pool_d[g], pool_d[g + 1]], axis=1), w2) * ps_ref[:, cols]
        zp_ref[:, cols] = (y * spg[:, cols]).astype(BF16)

    qs = _dot_nt(h, w_rows(0, a_w)) * sc
    q_ref[...] = qs.astype(BF16)
    nq = jnp.sqrt(_dot((qs * qs).astype(BF16), sel)) * NORM_SLACK
    qmx_ref[...] = jnp.concatenate([jnp.max(nq[s * sb:(s + 1) * sb], axis=0, keepdims=True)
                                    for s in range(bm // sb)], axis=0)[:, 0:n_heads]

    @pl.when(step > 0)
    def _():
        kmx_ref[...] = jnp.maximum(kmx_ref[...], kmax)

    @pl.when(step == 0)
    def _():
        kmx_ref[...] = kmax

    kg_ref[...] = kmx_ref[...]


def _project(x2, ada, norm_g, wit, b_f, wp, ps, hist0, *, bm, sb, segs, ada_row, start_pos, n_heads,
             kv_head_major):
    rows, d = x2.shape
    a_w = n_heads * HEAD_DIM
    pw = len(POOL_WINDOWS) * LANES
    n_steps = rows // bm
    assert segs == 1 or n_steps == 1
    n_streams = segs
    seg_rows = bm // segs
    nsb = bm // sb
    assert nsb == 8 and seg_rows % sb == 0
    row_blk = lambda w: pl.BlockSpec((bm, w), lambda i, *_: (i, 0))
    per_stream = lambda r, w: pl.BlockSpec((segs, r, w), lambda i, *_: (0, 0, 0))
    kern = functools.partial(_proj_kernel, bm=bm, sb=sb, segs=segs, ada_row=ada_row, start_pos=start_pos,
                             n_heads=n_heads, kv_head_major=kv_head_major)
    if kv_head_major:
        kv_shape = (rows, n_heads, HEAD_DIM)
        kv_blk = pl.BlockSpec((bm, n_heads, HEAD_DIM), lambda i, *_: (i, 0, 0))
    else:
        kv_shape = (a_w, rows)
        kv_blk = pl.BlockSpec((a_w, bm), lambda i, *_: (0, i))
    assert pw == a_w
    out_shape = (
        jax.ShapeDtypeStruct((rows, ACT_PARTS * a_w), BF16),
        jax.ShapeDtypeStruct(kv_shape, F32),
        jax.ShapeDtypeStruct(kv_shape, F32),
        jax.ShapeDtypeStruct((n_streams, n_heads, rows // n_streams), F32),
        jax.ShapeDtypeStruct((rows, n_heads), F32),
        jax.ShapeDtypeStruct((rows // sb, n_heads, sb), F32),
        jax.ShapeDtypeStruct((rows // sb, n_heads), F32),
        jax.ShapeDtypeStruct((rows // sb, n_heads), F32),
        jax.ShapeDtypeStruct((1, LANES), F32),
        jax.ShapeDtypeStruct((n_streams, 16, pw), F32),
    )
    out_specs = (
        row_blk(ACT_PARTS * a_w), kv_blk, kv_blk,
        pl.BlockSpec((segs, n_heads, seg_rows), lambda i, *_: (0, 0, i)),
        row_blk(n_heads),
        pl.BlockSpec((nsb, n_heads, sb), lambda i, *_: (i, 0, 0)),
        pl.BlockSpec((nsb, n_heads), lambda i, *_: (i, 0)),
        pl.BlockSpec((nsb, n_heads), lambda i, *_: (i, 0)),
        pl.BlockSpec((1, LANES), lambda i, *_: (0, 0)),
        per_stream(16, pw),
    )
    in_specs = [
        row_blk(d),
        _resident(ada.shape),
        _resident((1, d)),
        _resident(wit.shape), _resident(wp.shape), _resident(ps.shape),
        per_stream(HIST_PAD, pw),
    ]
    return pl.pallas_call(
        kern,
        grid_spec=pltpu.PrefetchScalarGridSpec(
            num_scalar_prefetch=1,
            grid=(n_steps,),
            in_specs=in_specs,
            out_specs=out_specs,
            scratch_shapes=[pltpu.VMEM((bm + segs * HIST_PAD, pw), F32),
                            pltpu.VMEM((bm + segs * HIST_PAD, pw), F32),
                            pltpu.VMEM((bm + segs * HIST_PAD, pw - LANES), F32),
                            pltpu.VMEM((bm + segs * HIST_PAD, pw - 2 * LANES), F32),
                            pltpu.VMEM((1, LANES), F32)]),
        out_shape=out_shape,
        compiler_params=pltpu.CompilerParams(dimension_semantics=("arbitrary",), vmem_limit_bytes=VMEM_LIMIT),
        name="proj",
    )(b_f, x2, ada, norm_g, wit, wp, ps, hist0)


def _merge_norm(x, gate, za, zp, wo_ref, fg, a_w):
    dy = _dot(za, wo_ref[0:a_w, :].astype(BF16)) + _dot(zp, wo_ref[a_w:, :].astype(BF16))
    out = x + gate * dy
    ms = jnp.mean(out * out, axis=-1, keepdims=True)
    return out * lax.rsqrt(ms + EPS) * fg


def _attn_kernel(tot_ref, qmx_ref, kg_ref,
                 act_ref, cq_ref, ck_ref, x_ref, ada_ref, wo_ref, fg_ref,
                 y_ref,
                 k_ref, v_ref, z_ref, m_ref, l_ref, acc_ref, straight_ref, *, tm, n_heads):
    a_w = n_heads * HEAD_DIM
    q_ref, knew_ref, vnew_ref, sa_ref, zp_ref = _act_views(act_ref, a_w)
    n_pairs = n_heads // PAIR
    nsub = tm // ATT_BLK
    w_near = NEAR_BLOCKS
    cw = w_near * ATT_BLK
    step = pl.program_id(0)

    def keep_step_rows():
        k_ref[pl.ds(pl.multiple_of(step * tm, tm), tm), :] = knew_ref[...]
        v_ref[pl.ds(pl.multiple_of(step * tm, tm), tm), :] = vnew_ref[...]

    lo_q = lax.broadcasted_iota(jnp.int32, (ATT_BLK, LANES), 1) < HEAD_DIM
    lo_k = lax.broadcasted_iota(jnp.int32, (cw, LANES), 1) < HEAD_DIM
    col = lax.broadcasted_iota(jnp.int32, (ATT_BLK, cw), 1)
    tri = (lax.broadcasted_iota(jnp.int32, (ATT_BLK, ATT_BLK), 1)
           <= lax.broadcasted_iota(jnp.int32, (ATT_BLK, ATT_BLK), 0))
    tri_bias = jnp.where(tri, 0.0, NEG).astype(F32)
    zeros_k = jnp.zeros((cw, LANES), BF16)
    ind_lo = jnp.where(lo_k, 1.0, 0.0).astype(BF16)
    ind_hi = jnp.where(lo_k, 0.0, 1.0).astype(BF16)

    def tot_at(b, hd):
        return jnp.where(b >= 0, tot_ref[jnp.maximum(b, 0), hd], 0.0)

    def rows_of(ref, blocks, g):
        return jnp.concatenate(
            [ref[pl.ds(pl.multiple_of(b * ATT_BLK, ATT_BLK), ATT_BLK), g * LANES:(g + 1) * LANES] for b in blocks],
            axis=0)

    def step_rows_of(ref, new_ref, sub, g):
        parts = []
        for p in range(w_near):
            rel = sub - (w_near - 1) + p
            if rel >= 0:
                parts.append(new_ref[rel * ATT_BLK:(rel + 1) * ATT_BLK, g * LANES:(g + 1) * LANES])
            else:
                start = pl.multiple_of((step * nsub + rel) * ATT_BLK, ATT_BLK)
                parts.append(ref[pl.ds(start, ATT_BLK), g * LANES:(g + 1) * LANES])
        return jnp.concatenate(parts, axis=0)

    def pair_scores(r0, kc, g):
        keys = jnp.concatenate([jnp.where(lo_k, kc, zeros_k), jnp.where(lo_k, zeros_k, kc)], axis=0)
        return _dot_nt(q_ref[pl.ds(r0, ATT_BLK), g * LANES:(g + 1) * LANES], keys)

    def pair_values(p_pair, vc):
        vals = jnp.concatenate([jnp.concatenate([jnp.where(lo_k, vc, zeros_k), ind_lo], axis=1),
                                jnp.concatenate([jnp.where(lo_k, zeros_k, vc), ind_hi], axis=1)], axis=0)
        return _dot(p_pair, vals)

    def decay_row(blocks, offs, hd):
        return jnp.concatenate([offs[p] - ck_ref[blocks[p], hd:hd + 1, :] for p in range(w_near)], axis=1)

    def near_weights(s_pair, rows, g, near_c, near_offs, m_bound):
        bounded = m_bound is not None
        ps, ms = [], []
        for e in range(PAIR):
            hd = PAIR * g + e
            cqh = cq_ref[rows, hd:hd + 1]
            if bounded:
                m = jnp.full((ATT_BLK, 1), m_bound[hd], F32)
                cqh = cqh - m_bound[hd]
            dec = decay_row(near_c, near_offs[hd], hd)
            pieces = []
            for p in range(w_near):
                lanes = slice(e * cw + p * ATT_BLK, e * cw + (p + 1) * ATT_BLK)
                sp = s_pair[:, lanes] + cqh + dec[:, p * ATT_BLK:(p + 1) * ATT_BLK]
                pieces.append(sp + tri_bias if p == w_near - 1 else sp)
            s = jnp.concatenate(pieces, axis=1)
            if not bounded:
                m = jnp.max(s, axis=1, keepdims=True)
                s = s - m
            ms.append(m)
            ps.append(jnp.exp2(s).astype(BF16))
        return jnp.concatenate(ps, axis=1), ms

    def gated_output(rows, g, acc, l):
        cols = slice(g * LANES, (g + 1) * LANES)
        z_ref[rows, cols] = ((acc / l) * sa_ref[rows, cols].astype(F32)).astype(BF16)

    def merge(chunks=1):
        rows_c = tm // chunks
        for c in range(chunks):
            rows = slice(c * rows_c, (c + 1) * rows_c)
            y_ref[rows, :] = _merge_norm(x_ref[rows, :], ada_ref[2, ADA_PROMPT_ROW:ADA_PROMPT_ROW + 1, :],
                                         z_ref[rows, :], zp_ref[rows, :], wo_ref, fg_ref[...], a_w)

    def check_next_step():
        ok = jnp.bool_(True)
        last = qmx_ref.shape[0] - 1
        for sub in range(nsub):
            i = jnp.minimum((step + 1) * nsub + sub, last)
            for hd in range(n_heads):
                qk = 2.0 * qmx_ref[i, hd] * kg_ref[0, hd]
                back = qk
                for dd in range(1, w_near):
                    back = back + tot_ref[i - dd, hd]
                ok = jnp.logical_and(ok, jnp.logical_and(qk <= EXP2_SAFE_SPAN, back < -EXP2_UNDERFLOW))
        straight_ref[0] = ok.astype(jnp.int32)

    def straight_step():
        units = [(sub, g) for sub in range(nsub) for g in range(n_pairs)]
        near_of, offs_of, bound_of = [], [], []
        for sub in range(nsub):
            i = step * nsub + sub
            near_of.append([i - (w_near - 1) + p for p in range(w_near)])
            bound_of.append([qmx_ref[i, hd] * kg_ref[0, hd] for hd in range(n_heads)])
            offs = []
            for hd in range(n_heads):
                o, per_piece = jnp.float32(0.0), [jnp.float32(0.0)]
                for dd in range(1, w_near):
                    o = o + tot_ref[i - dd, hd]
                    per_piece.append(o)
                offs.append(per_piece[::-1])
            offs_of.append(offs)
        s_next = pair_scores(0, step_rows_of(k_ref, knew_ref, 0, 0), 0)
        for u, (sub, g) in enumerate(units):
            s_pair = s_next
            if u + 1 < len(units):
                sub1, g1 = units[u + 1]
                s_next = pair_scores(sub1 * ATT_BLK, step_rows_of(k_ref, knew_ref, sub1, g1), g1)
            rows = pl.ds(sub * ATT_BLK, ATT_BLK)
            p_pair, _ = near_weights(s_pair, rows, g, near_of[sub], offs_of[sub], bound_of[sub])
            r = pair_values(p_pair, step_rows_of(v_ref, vnew_ref, sub, g))
            gated_output(rows, g, r[:, 0:LANES], r[:, LANES:])
        check_next_step()
        keep_step_rows()
        merge(chunks=2)

    def sub_body(sub, carry):
        i = step * nsub + sub
        r0 = pl.multiple_of(sub * ATT_BLK, ATT_BLK)

        rows = pl.ds(r0, ATT_BLK)
        qk = [2.0 * qmx_ref[i, hd] * kg_ref[0, hd] for hd in range(n_heads)]

        near = [i - (w_near - 1) + p for p in range(w_near)]
        near_c = [jnp.maximum(b, 0) for b in near]
        near_offs, offs_far = [], []
        for hd in range(n_heads):
            back = [tot_at(i - dd, hd) for dd in range(1, w_near)]
            offs = []
            for p in range(w_near):
                o = jnp.float32(0.0)
                for dd in range(1, w_near - p):
                    o = o + back[dd - 1]
                offs.append(jnp.where(near[p] >= 0, o, NEG))
            near_offs.append(offs)
            o = jnp.float32(0.0)
            for t in back:
                o = o + t
            offs_far.append(o)

        def near_chunk(bounded):
            def fn():
                s_next = pair_scores(r0, rows_of(k_ref, near_c, 0), 0)
                for g in range(n_pairs):
                    s_pair = s_next
                    if g + 1 < n_pairs:
                        s_next = pair_scores(r0, rows_of(k_ref, near_c, g + 1), g + 1)
                    p_pair, ms = near_weights(s_pair, rows, g, near_c, near_offs,
                                              [0.5 * b for b in qk] if bounded else None)
                    for e in range(PAIR):
                        m_ref[PAIR * g + e] = ms[e]
                    r = pair_values(p_pair, rows_of(v_ref, near_c, g))
                    acc_ref[g] = r[:, 0:LANES]
                    l_ref[g] = r[:, LANES:]
                    gated_output(rows, g, r[:, 0:LANES], r[:, LANES:])
            return fn

        bounded_ok = qk[0] <= EXP2_SAFE_SPAN
        for hd in range(1, n_heads):
            bounded_ok = jnp.logical_and(bounded_ok, qk[hd] <= EXP2_SAFE_SPAN)
        pl.when(bounded_ok)(near_chunk(True))
        pl.when(jnp.logical_not(bounded_ok))(near_chunk(False))

        def far_cond(c):
            top = i - c[0] * w_near
            need = qk[0] + c[1] >= -EXP2_UNDERFLOW
            for hd in range(1, n_heads):
                need = jnp.logical_or(need, qk[hd] + c[1 + hd] >= -EXP2_UNDERFLOW)
            return jnp.logical_and(top >= 0, need)

        def far_body(c):
            top = i - c[0] * w_near
            jc = jnp.maximum(top - (w_near - 1), 0)
            blocks = [jc + p for p in range(w_near)]
            keepc = col < (top + 1 - jc) * ATT_BLK
            new = [c[0] + 1]
            for g in range(n_pairs):
                s_pair = pair_scores(r0, rows_of(k_ref, blocks, g), g)
                ps, alphas = [], []
                for e in range(PAIR):
                    hd = PAIR * g + e
                    tt = [tot_at(top - b, hd) for b in range(w_near)]
                    offs = []
                    for p in range(w_near):
                        behind = top - (jc + p)
                        o = c[1 + hd]
                        for b in range(w_near):
                            o = o + jnp.where(behind >= b, tt[b], 0.0)
                        offs.append(o)
                    s = s_pair[:, e * cw:(e + 1) * cw] + cq_ref[rows, hd:hd + 1] + decay_row(blocks, offs, hd)
                    s = jnp.where(keepc, s, NEG)
                    m_old = m_ref[hd]
                    m_new = jnp.maximum(m_old, jnp.max(s, axis=1, keepdims=True))
                    m_ref[hd] = m_new
                    alphas.append(jnp.broadcast_to(jnp.exp2(m_old - m_new), (ATT_BLK, LANES)))
                    ps.append(jnp.exp2(s - m_new).astype(BF16))
                    o = c[1 + hd]
                    for t in tt:
                        o = o + t
                    new.append(o)
                alpha = jnp.where(lo_q, alphas[0], alphas[1])
                r = pair_values(jnp.concatenate(ps, axis=1), rows_of(v_ref, blocks, g))
                acc_ref[g] = alpha * acc_ref[g] + r[:, 0:LANES]
                l_ref[g] = alpha * l_ref[g] + r[:, LANES:]
            return tuple(new)

        far = lax.while_loop(far_cond, far_body, (jnp.int32(1),) + tuple(offs_far))

        @pl.when(far[0] > 1)
        def _():
            for g in range(n_pairs):
                gated_output(rows, g, acc_ref[g], l_ref[g])
        return carry

    @pl.when(step == 0)
    def _():
        straight_ref[0] = 0

    straight = straight_ref[0] == 1
    pl.when(straight)(straight_step)

    @pl.when(jnp.logical_not(straight))
    def _():
        keep_step_rows()
        check_next_step()
        lax.fori_loop(0, nsub, sub_body, 0)
        merge()


def _prompt_attention(tot, qmx, kg, act, cq, ck, x2, ada, wo, fg, *, tm, n_heads):
    rows, d = x2.shape
    a_w = n_heads * HEAD_DIM
    n_pairs = n_heads // PAIR
    assert rows % tm == 0 and tm % ATT_BLK == 0 and tm // ATT_BLK >= NEAR_BLOCKS - 1 and n_heads % PAIR == 0
    row_blk = lambda w: pl.BlockSpec((tm, w), lambda i, *_: (i, 0))
    grid_spec = pltpu.PrefetchScalarGridSpec(
        num_scalar_prefetch=3,
        grid=(rows // tm,),
        in_specs=[row_blk(ACT_PARTS * a_w), row_blk(n_heads), _resident(ck.shape), row_blk(d),
                  _resident(ada.shape), _resident(wo.shape), _resident((1, d))],
        out_specs=row_blk(d),
        scratch_shapes=[pltpu.VMEM((rows, a_w), BF16),
                        pltpu.VMEM((rows, a_w), BF16),
                        pltpu.VMEM((tm, a_w), BF16),
                        pltpu.VMEM((n_heads, ATT_BLK, 1), F32),
                        pltpu.VMEM((n_pairs, ATT_BLK, LANES), F32),
                        pltpu.VMEM((n_pairs, ATT_BLK, LANES), F32),
                        pltpu.SMEM((1,), jnp.int32)],
    )
    return pl.pallas_call(
        functools.partial(_attn_kernel, tm=tm, n_heads=n_heads),
        grid_spec=grid_spec,
        out_shape=jax.ShapeDtypeStruct((rows, d), F32),
        compiler_params=pltpu.CompilerParams(dimension_semantics=("arbitrary",), vmem_limit_bytes=VMEM_LIMIT),
        name="attn",
    )(tot, qmx, kg, act, cq, ck, x2, ada, wo, fg)


CACHE_SPLIT = 2


def _sattn_kernel(act_ref, cq_ref, ckn_ref, kc0_ref, kc1_ref, vc0_ref, vc1_ref, lfc_ref,
                  x_ref, ada_ref, wo_ref, fg_ref, y_ref, *, n_heads):
    a_w = n_heads * HEAD_DIM
    q_ref, kn_ref, vn_ref, sa_ref, zp_ref = _act_views(act_ref, a_w)
    ln = act_ref.shape[0]
    past = kc0_ref.shape[2]
    kc_refs, vc_refs = (kc0_ref, kc1_ref), (vc0_ref, vc1_ref)
    pairs_per_part = n_heads // PAIR // CACHE_SPLIT

    def cached(refs, g):
        lo = (g % pairs_per_part) * LANES
        return refs[g // pairs_per_part][0, lo:lo + LANES, :].astype(BF16)

    nb = past // LANES
    lane = lax.broadcasted_iota(jnp.int32, (ln, LANES), 1)
    half = [lane < HEAD_DIM, lane >= HEAD_DIM]
    causal = lax.broadcasted_iota(jnp.int32, (ln, ln), 1) <= lax.broadcasted_iota(jnp.int32, (ln, ln), 0)
    n_pairs = n_heads // PAIR

    def pair_scores(g):
        cols = slice(g * LANES, (g + 1) * LANES)
        q2 = q_ref[:, cols]
        qst = jnp.concatenate([jnp.where(half[e], q2, jnp.zeros_like(q2)) for e in range(PAIR)], axis=0)
        return _dot(qst, cached(kc_refs, g)), _dot_nt(qst, kn_ref[:, cols])

    outs = []
    s_next = pair_scores(0)

    lfc = lfc_ref[0] * LOG2E
    cs = _lane_cumsum(lfc, LANES)
    after = jnp.zeros((n_heads, 1), F32)
    suffix = [None] * nb
    for b in reversed(range(nb)):
        cb = cs[:, b * LANES:(b + 1) * LANES]
        tot = cb[:, LANES - 1:LANES]
        suffix[b] = (tot - cb) + after
        after = after + tot
    dec_c = jnp.concatenate(suffix, axis=1)

    for g in range(n_pairs):
        cols = slice(g * LANES, (g + 1) * LANES)
        sc_st, sn_st = s_next
        if g + 1 < n_pairs:
            s_next = pair_scores(g + 1)
        vct = cached(vc_refs, g)
        vn = vn_ref[:, cols]
        pc, pn, ls = [], [], []
        for e in range(PAIR):
            hd = PAIR * g + e
            rows = slice(e * ln, (e + 1) * ln)
            cqh = cq_ref[:, hd:hd + 1]
            s_c = sc_st[rows] + cqh + dec_c[hd:hd + 1, :]
            s_n = jnp.where(causal, sn_st[rows] + cqh - ckn_ref[0, hd:hd + 1, :], NEG)
            m = jnp.maximum(jnp.max(s_c, axis=1, keepdims=True), jnp.max(s_n, axis=1, keepdims=True))
            p_c = jnp.exp2(s_c - m)
            p_n = jnp.exp2(s_n - m)
            ls.append(jnp.sum(p_c, axis=1, keepdims=True) + jnp.sum(p_n, axis=1, keepdims=True))
            pc.append(p_c.astype(BF16))
            pn.append(p_n.astype(BF16))
        acc = _dot_nt(jnp.concatenate(pc, axis=0), vct) + _dot(jnp.concatenate(pn, axis=0), vn)
        o = jnp.where(half[0], acc[0:ln] / ls[0], acc[ln:2 * ln] / ls[1])
        outs.append((o * sa_ref[:, cols].astype(F32)).astype(BF16))
    za = jnp.concatenate(outs, axis=1)
    y_ref[...] = _merge_norm(x_ref[...], ada_ref[2, pl.ds(ADA_SAMPLE_ROW + pl.program_id(0), 1), :], za, zp_ref[...], wo_ref, fg_ref[...], a_w)


def _sample_attention(act, cq, ckn, cache_k, cache_v, lfc, x2, ada, wo, fg, *, ln, n_heads):
    rows, d = x2.shape
    nbatch = rows // ln
    a_w = n_heads * HEAD_DIM
    past = cache_k.shape[2]
    row_blk = lambda w: pl.BlockSpec((ln, w), lambda b: (b, 0))
    per_b = lambda s: pl.BlockSpec((1,) + s, lambda b: (b, 0, 0))
    cache_part = lambda c: pl.BlockSpec((1, a_w // CACHE_SPLIT, past), lambda b: (b, c, 0))
    assert CACHE_SPLIT == 2 and (n_heads // PAIR) % CACHE_SPLIT == 0
    return pl.pallas_call(
        functools.partial(_sattn_kernel, n_heads=n_heads),
        grid=(nbatch,),
        in_specs=[row_blk(ACT_PARTS * a_w), row_blk(n_heads), per_b((n_heads, ln)),
                  cache_part(0), cache_part(1), cache_part(0), cache_part(1), per_b((n_heads, past)),
                  row_blk(d), _resident(ada.shape), _resident(wo.shape), _resident((1, d))],
        out_specs=row_blk(d),
        out_shape=jax.ShapeDtypeStruct((rows, d), F32),
        compiler_params=pltpu.CompilerParams(dimension_semantics=("arbitrary",), vmem_limit_bytes=VMEM_LIMIT),
        name="sattn",
    )(act, cq, ckn, cache_k, cache_k, cache_v, cache_v, lfc, x2, ada, wo, fg)


def kernel(x_prompt, x_sample, c_prompt, c_sample, cache_k, cache_v, cache_logf, state_pool, norm_g, w_ada, b_ada,
           w_in, b_f, w_pool, pool_scale, w_out, final_g):
    depth = norm_g.shape[0]
    assert depth == 1
    bp, seq, d = x_prompt.shape
    bs, ln, _ = x_sample.shape
    assert bp == 1
    n_heads = cache_k.shape[3]
    past = cache_k.shape[2]
    a_w = n_heads * HEAD_DIM
    pw = state_pool.shape[3]
    assert pw == len(POOL_WINDOWS) * LANES and cache_k.shape[4] == HEAD_DIM and n_heads <= 8

    ada = _ada_terms(c_prompt, c_sample, w_ada[0], b_ada)

    wit = w_in[0].T
    wp = w_pool[0]
    ps = pool_scale[0][None, :]
    wo = w_out[0]
    ng = norm_g[0][None, :]
    fg = final_g[None, :]

    assert seq % PROJ_ROWS == 0 and seq % ATTN_ROWS == 0
    xp2 = x_prompt.reshape(seq, d)
    hist_p = jnp.zeros((1, HIST_PAD, pw), F32)
    (act_p, k_p, v_p, lf_p, cq_p, ck_p, tot, qmx, kg, ho_p) = _project(
        xp2, ada, ng, wit, b_f[0], wp, ps, hist_p,
        bm=PROJ_ROWS, sb=ATT_BLK, segs=1, ada_row=ADA_PROMPT_ROW, start_pos=0, n_heads=n_heads, kv_head_major=False)
    y_p = _prompt_attention(tot, qmx, kg, act_p, cq_p, ck_p, xp2, ada, wo, fg, tm=ATTN_ROWS, n_heads=n_heads)

    xs2 = x_sample.reshape(bs * ln, d)
    hist_s = jnp.pad(state_pool[0], ((0, 0), (HIST_PAD - POOL_HIST, 0), (0, 0)))
    (act_s, k_s, v_s, lf_s, cq_s, ck_s, _, _, _, ho_s) = _project(
        xs2, ada, ng, wit, b_f[0], wp, ps, hist_s,
        bm=bs * ln, sb=ln, segs=bs, ada_row=ADA_SAMPLE_ROW, start_pos=past, n_heads=n_heads, kv_head_major=True)
    lfc = jnp.swapaxes(cache_logf[0], 1, 2)
    ckt = jnp.transpose(cache_k[0], (0, 2, 3, 1)).reshape(bs, a_w, past)
    cvt = jnp.transpose(cache_v[0], (0, 2, 3, 1)).reshape(bs, a_w, past)
    y_s = _sample_attention(act_s, cq_s, ck_s, ckt, cvt, lfc, xs2, ada, wo, fg, ln=ln, n_heads=n_heads)

    hd = (n_heads, HEAD_DIM)
    seq_minor = lambda t: jnp.transpose(t.reshape(hd + (bp, seq)), (2, 3, 0, 1))[None]
    return (y_p.reshape(bp, seq, d), y_s.reshape(bs, ln, d),
            seq_minor(k_p), seq_minor(v_p), jnp.swapaxes(lf_p, 1, 2)[None],
            ho_p[:, 16 - POOL_HIST:, :][None],
            k_s.reshape((1, bs, ln) + hd), v_s.reshape((1, bs, ln) + hd), jnp.swapaxes(lf_s, 1, 2)[None],
            ho_s[:, 16 - POOL_HIST:, :][None])
```

```python
import functools

import jax
import jax.numpy as jnp
from jax import lax
from jax.experimental import pallas as pl
from jax.experimental.pallas import tpu as pltpu

HEAD_DIM = 64
POOL_WINDOWS = (2, 4, 8, 16)
EPS = 1e-6

LANES = 128
PAIR = LANES // HEAD_DIM
ATT_BLK = 128
NEAR_BLOCKS = 3
LOG2E = 1.4426950408889634
EXP2_UNDERFLOW = 151.0
EXP2_SAFE_SPAN = 100.0
NORM_SLACK = 1.01
HIST_PAD = 32
POOL_HIST = max(POOL_WINDOWS) - 1
NEG = -1e30
VMEM_LIMIT = 60 * 1024 * 1024
PROJ_ROWS = 8 * ATT_BLK
ATTN_ROWS = 4 * ATT_BLK

F32 = jnp.float32
BF16 = jnp.bfloat16


def _silu(x):
    return x * jax.nn.sigmoid(x)


def _dot(a, b):
    return jnp.dot(a, b, preferred_element_type=F32)


def _dot_nt(a, b):
    return lax.dot_general(a, b, (((1,), (1,)), ((), ())), preferred_element_type=F32)


def _lane_cumsum(x, n):
    lane = lax.broadcasted_iota(jnp.int32, x.shape, 1) % LANES
    shift = 1
    while shift < n:
        x = x + jnp.where(lane >= shift, pltpu.roll(x, shift, 1), 0.0)
        shift *= 2
    return x


def _rows_to_lanes(x, n):
    rows = x.shape[0]
    if rows < LANES:
        x = jnp.concatenate([x, jnp.zeros((LANES - rows, LANES), x.dtype)], axis=0)
    return x.T[0:n, 0:rows]


def _lanes_to_rows(x):
    n, rows = x.shape
    if rows < LANES:
        x = jnp.concatenate([x, jnp.zeros((n, LANES - rows), x.dtype)], axis=1)
    x = jnp.concatenate([x, jnp.zeros((LANES - n, LANES), x.dtype)], axis=0)
    return x.T[0:rows, :]


ACT_PARTS = 5


def _act_views(act_ref, width):
    return [act_ref.at[:, p * width:(p + 1) * width] for p in range(ACT_PARTS)]


def _resident(shape):
    return pl.BlockSpec(shape, lambda *_: (0,) * len(shape), pipeline_mode=pl.Buffered(1))


ADA_ROWS = 16
ADA_PROMPT_ROW, ADA_SAMPLE_ROW = 0, 8


def _ada_kernel(cp_ref, cs_ref, w_ref, b_ref, o_ref):
    ap = jnp.broadcast_to(_silu(cp_ref[...]), (ADA_SAMPLE_ROW, cp_ref.shape[1]))
    a = jnp.concatenate([ap, _silu(cs_ref[...])], axis=0).astype(BF16)
    o_ref[0] = _dot(a, w_ref[...].astype(BF16)) + b_ref[...]


def _ada_terms(c_prompt, c_sample, w_ada, b_ada):
    d = c_prompt.shape[1]
    assert c_prompt.shape[0] == 1 and c_sample.shape[0] == ADA_ROWS - ADA_SAMPLE_ROW and w_ada.shape[1] == 3 * d
    return pl.pallas_call(
        _ada_kernel,
        grid=(3,),
        in_specs=[pl.BlockSpec(c_prompt.shape, lambda j: (0, 0)),
                  pl.BlockSpec(c_sample.shape, lambda j: (0, 0)),
                  pl.BlockSpec((d, d), lambda j: (0, j)),
                  pl.BlockSpec((1, d), lambda j: (0, j))],
        out_specs=pl.BlockSpec((1, ADA_ROWS, d), lambda j: (j, 0, 0)),
        out_shape=jax.ShapeDtypeStruct((3, ADA_ROWS, d), F32),
        compiler_params=pltpu.CompilerParams(dimension_semantics=("arbitrary",), vmem_limit_bytes=VMEM_LIMIT),
        name="ada",
    )(c_prompt, c_sample, w_ada, b_ada)


def _proj_kernel(bf_ref, x_ref, ada_ref, ng_ref, w_ref, wp_ref, ps_ref, h0_ref,
                 act_ref, k32_ref, v32_ref, lf_ref, ck_ref,
                 tot_ref, qmx_ref, kg_ref, ho_ref,
                 e_ref, t2_ref, t4_ref, t8_ref, kmx_ref, *, bm, sb, segs, ada_row, start_pos, n_heads, kv_head_major):
    a_w = n_heads * HEAD_DIM
    q_ref, kb_ref, vb_ref, sa_ref, zp_ref = _act_views(act_ref, a_w)
    seg_rows = bm // segs
    step = pl.program_id(0)
    pw = len(POOL_WINDOWS) * LANES
    sc = LOG2E / (HEAD_DIM ** 0.5)
    o_pool = 4 * a_w + n_heads

    def w_rows(lo, hi):
        return w_ref[lo:hi, :].astype(BF16)

    def normed(lo, hi):
        x = x_ref[lo:hi, :]
        xn = x * lax.rsqrt(jnp.mean(x * x, axis=-1, keepdims=True) + EPS)
        parts = []
        for g in range(lo // seg_rows, (hi - 1) // seg_rows + 1):
            r0, r1 = max(lo, g * seg_rows) - lo, min(hi, (g + 1) * seg_rows) - lo
            parts.append(xn[r0:r1] * (ng_ref[...] * (1.0 + ada_ref[1, ada_row + g:ada_row + g + 1, :]))
                         + ada_ref[0, ada_row + g:ada_row + g + 1, :])
        return jnp.concatenate(parts, axis=0).astype(BF16)

    n_lead = 4 if segs == 1 else 1
    hs, pus = [], []
    for c in range(n_lead):
        hs.append(normed(c * bm // n_lead, (c + 1) * bm // n_lead))
        pus.append(_dot_nt(hs[-1], w_rows(o_pool, o_pool + pw)))
    pu = jnp.concatenate(pus, axis=0)
    h = jnp.concatenate(hs, axis=0)
    sel = (lax.broadcasted_iota(jnp.int32, (a_w, LANES), 0) // HEAD_DIM
           == lax.broadcasted_iota(jnp.int32, (a_w, LANES), 1)).astype(BF16)

    def store_kv(ref32, refb, p):
        refb[...] = p.astype(BF16)
        if kv_head_major:
            for hd in range(n_heads):
                ref32[:, hd, :] = p[:, hd * HEAD_DIM:(hd + 1) * HEAD_DIM]
        else:
            ref32[...] = p.T

    ext = HIST_PAD + seg_rows
    n = segs * ext

    def load_history():
        for g in range(segs):
            e_ref[g * ext:g * ext + HIST_PAD, :] = h0_ref[g]

    if segs > 1:
        load_history()
    else:
        pl.when(step == 0)(load_history)

    for g in range(segs):
        e_ref[g * ext + HIST_PAD:(g + 1) * ext, :] = pu[g * seg_rows:(g + 1) * seg_rows]
    t2_ref[8:n, :] = e_ref[8:n, :] + e_ref[7:n - 1, :]
    t4_ref[16:n, :] = t2_ref[16:n, LANES:] + t2_ref[14:n - 2, LANES:]
    t8_ref[24:n, :] = t4_ref[24:n, LANES:] + t4_ref[20:n - 4, LANES:]

    def seg_rows_of(ref, cols, back=0):
        return jnp.concatenate([ref[g * ext + HIST_PAD - back:(g + 1) * ext - back, cols] for g in range(segs)], axis=0)

    lane0, lane1 = slice(0, LANES), slice(LANES, 2 * LANES)
    sums = [seg_rows_of(t2_ref, lane0), seg_rows_of(t4_ref, lane0), seg_rows_of(t8_ref, lane0),
            seg_rows_of(t8_ref, lane1) + seg_rows_of(t8_ref, lane1, back=8)]
    row = lax.broadcasted_iota(jnp.int32, (bm, 1), 0)
    pos1 = start_pos + 1 + (step * bm + row if segs == 1 else row % seg_rows)
    pool_d = []
    for g, w in enumerate(POOL_WINDOWS):
        rc = 1.0 / jnp.minimum(pos1, w).astype(F32)
        pool_d.append((sums[g] * rc - pu[:, g * LANES:(g + 1) * LANES]).astype(BF16))
    for g in range(segs):
        ho_ref[g] = e_ref[(g + 1) * ext - 16:(g + 1) * ext, :]
    if segs == 1:
        e_ref[0:HIST_PAD, :] = e_ref[bm:n, :]

    pk = _dot_nt(h, w_rows(a_w, 2 * a_w))
    store_kv(k32_ref, kb_ref, pk)
    nk = jnp.sqrt(_dot((pk * pk).astype(BF16), sel)) * NORM_SLACK
    spg = _silu(_dot_nt(h, w_rows(o_pool + pw, o_pool + 2 * pw)))

    lane = lax.broadcasted_iota(jnp.int32, (1, LANES), 1)
    bias = jnp.zeros((1, LANES), F32)
    for hd in range(n_heads):
        bias = jnp.where(lane == hd, bf_ref[hd], bias)
    wf = jnp.concatenate([w_ref[4 * a_w:o_pool, :], jnp.zeros((LANES - n_heads, w_ref.shape[1]), F32)], axis=0)
    z = _dot_nt(h, wf.astype(BF16)) + bias
    lf = jnp.minimum(z, 0.0) - jnp.log1p(jnp.exp(-jnp.abs(z)))

    sa_ref[...] = _silu(_dot_nt(h, w_rows(3 * a_w, 4 * a_w))).astype(BF16)

    lfts = []
    for s in range(bm // sb):
        lft = _rows_to_lanes(lf[s * sb:(s + 1) * sb], n_heads)
        off = (s * sb) % seg_rows
        lf_ref[(s * sb) // seg_rows, :, off:off + sb] = lft
        lfts.append(lft if sb == LANES else jnp.concatenate([lft, jnp.zeros((n_heads, LANES - sb), F32)], axis=1))
    c_all = _lane_cumsum(jnp.concatenate(lfts, axis=1) * LOG2E, sb)
    tots, kmx = [], []
    for s in range(bm // sb):
        c = c_all[:, s * LANES:(s + 1) * LANES]
        ck_ref[s] = c[:, 0:sb]
        tots.append(c[:, sb - 1:sb])
        kmx.append(jnp.max(nk[s * sb:(s + 1) * sb], axis=0, keepdims=True))
    tot_ref[...] = _lanes_to_rows(jnp.concatenate(tots, axis=1))[:, 0:n_heads]
    kmax = kmx[0]
    for t in kmx[1:]:
        kmax = jnp.maximum(kmax, t)

    store_kv(v32_ref, vb_ref, _dot_nt(h, w_rows(2 * a_w, 3 * a_w)))

    zero_w = jnp.zeros((LANES, LANES), BF16)
    for g in range(0, len(POOL_WINDOWS), 2):
        cols = slice(g * LANES, (g + 2) * LANES)
        w2 = jnp.concatenate([jnp.concatenate([wp_ref[g].astype(BF16), zero_w], axis=1),
                              jnp.concatenate([zero_w, wp_ref[g + 1].astype(BF16)], axis=1)], axis=0)
        y = _dot(jnp.concatenate([pool_d[g], pool_d[g + 1]], axis=1), w2) * ps_ref[:, cols]
        zp_ref[:, cols] = (y * spg[:, cols]).astype(BF16)

    qs = _dot_nt(h, w_rows(0, a_w)) * sc
    q_ref[...] = qs.astype(BF16)
    nq = jnp.sqrt(_dot((qs * qs).astype(BF16), sel)) * NORM_SLACK
    qmx_ref[...] = jnp.concatenate([jnp.max(nq[s * sb:(s + 1) * sb], axis=0, keepdims=True)
                                    for s in range(bm // sb)], axis=0)[:, 0:n_heads]

    @pl.when(step > 0)
    def _():
        kmx_ref[...] = jnp.maximum(kmx_ref[...], kmax)

    @pl.when(step == 0)
    def _():
        kmx_ref[...] = kmax

    kg_ref[...] = kmx_ref[...]


def _project(x2, ada, norm_g, wit, b_f, wp, ps, hist0, *, bm, sb, segs, ada_row, start_pos, n_heads,
             kv_head_major):
    rows, d = x2.shape
    a_w = n_heads * HEAD_DIM
    pw = len(POOL_WINDOWS) * LANES
    n_steps = rows // bm
    assert segs == 1 or n_steps == 1
    n_streams = segs
    seg_rows = bm // segs
    nsb = bm // sb
    assert nsb == 8 and seg_rows % sb == 0
    row_blk = lambda w: pl.BlockSpec((bm, w), lambda i, *_: (i, 0))
    per_stream = lambda r, w: pl.BlockSpec((segs, r, w), lambda i, *_: (0, 0, 0))
    kern = functools.partial(_proj_kernel, bm=bm, sb=sb, segs=segs, ada_row=ada_row, start_pos=start_pos,
                             n_heads=n_heads, kv_head_major=kv_head_major)
    if kv_head_major:
        kv_shape = (rows, n_heads, HEAD_DIM)
        kv_blk = pl.BlockSpec((bm, n_heads, HEAD_DIM), lambda i, *_: (i, 0, 0))
    else:
        kv_shape = (a_w, rows)
        kv_blk = pl.BlockSpec((a_w, bm), lambda i, *_: (0, i))
    assert pw == a_w
    out_shape = (
        jax.ShapeDtypeStruct((rows, ACT_PARTS * a_w), BF16),
        jax.ShapeDtypeStruct(kv_shape, F32),
        jax.ShapeDtypeStruct(kv_shape, F32),
        jax.ShapeDtypeStruct((n_streams, n_heads, rows // n_streams), F32),
        jax.ShapeDtypeStruct((rows // sb, n_heads, sb), F32),
        jax.ShapeDtypeStruct((rows // sb, n_heads), F32),
        jax.ShapeDtypeStruct((rows // sb, n_heads), F32),
        jax.ShapeDtypeStruct((1, LANES), F32),
        jax.ShapeDtypeStruct((n_streams, 16, pw), F32),
    )
    out_specs = (
        row_blk(ACT_PARTS * a_w), kv_blk, kv_blk,
        pl.BlockSpec((segs, n_heads, seg_rows), lambda i, *_: (0, 0, i)),
        pl.BlockSpec((nsb, n_heads, sb), lambda i, *_: (i, 0, 0)),
        pl.BlockSpec((nsb, n_heads), lambda i, *_: (i, 0)),
        pl.BlockSpec((nsb, n_heads), lambda i, *_: (i, 0)),
        pl.BlockSpec((1, LANES), lambda i, *_: (0, 0)),
        per_stream(16, pw),
    )
    in_specs = [
        row_blk(d),
        _resident(ada.shape),
        _resident((1, d)),
        _resident(wit.shape), _resident(wp.shape), _resident(ps.shape),
        per_stream(HIST_PAD, pw),
    ]
    return pl.pallas_call(
        kern,
        grid_spec=pltpu.PrefetchScalarGridSpec(
            num_scalar_prefetch=1,
            grid=(n_steps,),
            in_specs=in_specs,
            out_specs=out_specs,
            scratch_shapes=[pltpu.VMEM((bm + segs * HIST_PAD, pw), F32),
                            pltpu.VMEM((bm + segs * HIST_PAD, pw), F32),
                            pltpu.VMEM((bm + segs * HIST_PAD, pw - LANES), F32),
                            pltpu.VMEM((bm + segs * HIST_PAD, pw - 2 * LANES), F32),
                            pltpu.VMEM((1, LANES), F32)]),
        out_shape=out_shape,
        compiler_params=pltpu.CompilerParams(dimension_semantics=("arbitrary",), vmem_limit_bytes=VMEM_LIMIT),
        name="proj",
    )(b_f, x2, ada, norm_g, wit, wp, ps, hist0)


def _merge_norm(x, gate, za, zp, wo_ref, fg, a_w):
    dy = _dot(za, wo_ref[0:a_w, :].astype(BF16)) + _dot(zp, wo_ref[a_w:, :].astype(BF16))
    out = x + gate * dy
    ms = jnp.mean(out * out, axis=-1, keepdims=True)
    return out * lax.rsqrt(ms + EPS) * fg


def _attn_kernel(tot_ref, qmx_ref, kg_ref,
                 act_ref, ck_ref, x_ref, ada_ref, wo_ref, fg_ref,
                 y_ref,
                 k_ref, v_ref, z_ref, m_ref, l_ref, acc_ref, straight_ref, *, tm, n_heads):
    a_w = n_heads * HEAD_DIM
    q_ref, knew_ref, vnew_ref, sa_ref, zp_ref = _act_views(act_ref, a_w)
    n_pairs = n_heads // PAIR
    nsub = tm // ATT_BLK
    w_near = NEAR_BLOCKS
    cw = w_near * ATT_BLK
    step = pl.program_id(0)

    def keep_step_rows():
        k_ref[pl.ds(pl.multiple_of(step * tm, tm), tm), :] = knew_ref[...]
        v_ref[pl.ds(pl.multiple_of(step * tm, tm), tm), :] = vnew_ref[...]

    lo_q = lax.broadcasted_iota(jnp.int32, (ATT_BLK, LANES), 1) < HEAD_DIM
    lo_k = lax.broadcasted_iota(jnp.int32, (cw, LANES), 1) < HEAD_DIM
    col = lax.broadcasted_iota(jnp.int32, (ATT_BLK, cw), 1)
    tri = (lax.broadcasted_iota(jnp.int32, (ATT_BLK, ATT_BLK), 1)
           <= lax.broadcasted_iota(jnp.int32, (ATT_BLK, ATT_BLK), 0))
    tri_bias = jnp.where(tri, 0.0, NEG).astype(F32)
    zeros_k = jnp.zeros((cw, LANES), BF16)
    ind_lo = jnp.where(lo_k, 1.0, 0.0).astype(BF16)
    ind_hi = jnp.where(lo_k, 0.0, 1.0).astype(BF16)

    def tot_at(b, hd):
        return jnp.where(b >= 0, tot_ref[jnp.maximum(b, 0), hd], 0.0)

    def rows_of(ref, blocks, g):
        return jnp.concatenate(
            [ref[pl.ds(pl.multiple_of(b * ATT_BLK, ATT_BLK), ATT_BLK), g * LANES:(g + 1) * LANES] for b in blocks],
            axis=0)

    def step_rows_of(ref, new_ref, sub, g):
        parts = []
        for p in range(w_near):
            rel = sub - (w_near - 1) + p
            if rel >= 0:
                parts.append(new_ref[rel * ATT_BLK:(rel + 1) * ATT_BLK, g * LANES:(g + 1) * LANES])
            else:
                start = pl.multiple_of((step * nsub + rel) * ATT_BLK, ATT_BLK)
                parts.append(ref[pl.ds(start, ATT_BLK), g * LANES:(g + 1) * LANES])
        return jnp.concatenate(parts, axis=0)

    def pair_scores(r0, kc, g):
        keys = jnp.concatenate([jnp.where(lo_k, kc, zeros_k), jnp.where(lo_k, zeros_k, kc)], axis=0)
        return _dot_nt(q_ref[pl.ds(r0, ATT_BLK), g * LANES:(g + 1) * LANES], keys)

    def pair_values(p_pair, vc):
        vals = jnp.concatenate([jnp.concatenate([jnp.where(lo_k, vc, zeros_k), ind_lo], axis=1),
                                jnp.concatenate([jnp.where(lo_k, zeros_k, vc), ind_hi], axis=1)], axis=0)
        return _dot(p_pair, vals)

    def decay_row(blocks, offs, hd):
        return jnp.concatenate([offs[p] - ck_ref[blocks[p], hd:hd + 1, :] for p in range(w_near)], axis=1)

    def query_cumsum(i):
        return _lanes_to_rows(ck_ref[i])

    def near_weights(s_pair, cq, g, near_c, near_offs, m_bound):
        bounded = m_bound is not None
        ps, ms = [], []
        for e in range(PAIR):
            hd = PAIR * g + e
            cqh = cq[:, hd:hd + 1]
            if bounded:
                m = jnp.full((ATT_BLK, 1), m_bound[hd], F32)
                cqh = cqh - m_bound[hd]
            dec = decay_row(near_c, near_offs[hd], hd)
            pieces = []
            for p in range(w_near):
                lanes = slice(e * cw + p * ATT_BLK, e * cw + (p + 1) * ATT_BLK)
                sp = s_pair[:, lanes] + cqh + dec[:, p * ATT_BLK:(p + 1) * ATT_BLK]
                pieces.append(sp + tri_bias if p == w_near - 1 else sp)
            s = jnp.concatenate(pieces, axis=1)
            if not bounded:
                m = jnp.max(s, axis=1, keepdims=True)
                s = s - m
            ms.append(m)
            ps.append(jnp.exp2(s).astype(BF16))
        return jnp.concatenate(ps, axis=1), ms

    def gated_output(rows, g, acc, l):
        cols = slice(g * LANES, (g + 1) * LANES)
        z_ref[rows, cols] = ((acc / l) * sa_ref[rows, cols].astype(F32)).astype(BF16)

    def merge(chunks=1):
        rows_c = tm // chunks
        for c in range(chunks):
            rows = slice(c * rows_c, (c + 1) * rows_c)
            y_ref[rows, :] = _merge_norm(x_ref[rows, :], ada_ref[2, ADA_PROMPT_ROW:ADA_PROMPT_ROW + 1, :],
                                         z_ref[rows, :], zp_ref[rows, :], wo_ref, fg_ref[...], a_w)

    def check_next_step():
        ok = jnp.bool_(True)
        last = qmx_ref.shape[0] - 1
        for sub in range(nsub):
            i = jnp.minimum((step + 1) * nsub + sub, last)
            for hd in range(n_heads):
                qk = 2.0 * qmx_ref[i, hd] * kg_ref[0, hd]
                back = qk
                for dd in range(1, w_near):
                    back = back + tot_ref[i - dd, hd]
                ok = jnp.logical_and(ok, jnp.logical_and(qk <= EXP2_SAFE_SPAN, back < -EXP2_UNDERFLOW))
        straight_ref[0] = ok.astype(jnp.int32)

    def straight_step():
        units = [(sub, g) for sub in range(nsub) for g in range(n_pairs)]
        near_of, offs_of, bound_of, cq_of = [], [], [], []
        for sub in range(nsub):
            i = step * nsub + sub
            cq_of.append(query_cumsum(i))
            near_of.append([i - (w_near - 1) + p for p in range(w_near)])
            bound_of.append([qmx_ref[i, hd] * kg_ref[0, hd] for hd in range(n_heads)])
            offs = []
            for hd in range(n_heads):
                o, per_piece = jnp.float32(0.0), [jnp.float32(0.0)]
                for dd in range(1, w_near):
                    o = o + tot_ref[i - dd, hd]
                    per_piece.append(o)
                offs.append(per_piece[::-1])
            offs_of.append(offs)
        s_next = pair_scores(0, step_rows_of(k_ref, knew_ref, 0, 0), 0)
        for u, (sub, g) in enumerate(units):
            s_pair = s_next
            if u + 1 < len(units):
                sub1, g1 = units[u + 1]
                s_next = pair_scores(sub1 * ATT_BLK, step_rows_of(k_ref, knew_ref, sub1, g1), g1)
            rows = pl.ds(sub * ATT_BLK, ATT_BLK)
            p_pair, _ = near_weights(s_pair, cq_of[sub], g, near_of[sub], offs_of[sub], bound_of[sub])
            r = pair_values(p_pair, step_rows_of(v_ref, vnew_ref, sub, g))
            gated_output(rows, g, r[:, 0:LANES], r[:, LANES:])
        check_next_step()
        keep_step_rows()
        merge(chunks=2)

    def sub_body(sub, carry):
        i = step * nsub + sub
        r0 = pl.multiple_of(sub * ATT_BLK, ATT_BLK)

        rows = pl.ds(r0, ATT_BLK)
        qk = [2.0 * qmx_ref[i, hd] * kg_ref[0, hd] for hd in range(n_heads)]
        cq = query_cumsum(i)

        near = [i - (w_near - 1) + p for p in range(w_near)]
        near_c = [jnp.maximum(b, 0) for b in near]
        near_offs, offs_far = [], []
        for hd in range(n_heads):
            back = [tot_at(i - dd, hd) for dd in range(1, w_near)]
            offs = []
            for p in range(w_near):
                o = jnp.float32(0.0)
                for dd in range(1, w_near - p):
                    o = o + back[dd - 1]
                offs.append(jnp.where(near[p] >= 0, o, NEG))
            near_offs.append(offs)
            o = jnp.float32(0.0)
            for t in back:
                o = o + t
            offs_far.append(o)

        def near_chunk(bounded):
            def fn():
                s_next = pair_scores(r0, rows_of(k_ref, near_c, 0), 0)
                for g in range(n_pairs):
                    s_pair = s_next
                    if g + 1 < n_pairs:
                        s_next = pair_scores(r0, rows_of(k_ref, near_c, g + 1), g + 1)
                    p_pair, ms = near_weights(s_pair, cq, g, near_c, near_offs,
                                              [0.5 * b for b in qk] if bounded else None)
                    for e in range(PAIR):
                        m_ref[PAIR * g + e] = ms[e]
                    r = pair_values(p_pair, rows_of(v_ref, near_c, g))
                    acc_ref[g] = r[:, 0:LANES]
                    l_ref[g] = r[:, LANES:]
                    gated_output(rows, g, r[:, 0:LANES], r[:, LANES:])
            return fn

        bounded_ok = qk[0] <= EXP2_SAFE_SPAN
        for hd in range(1, n_heads):
            bounded_ok = jnp.logical_and(bounded_ok, qk[hd] <= EXP2_SAFE_SPAN)
        pl.when(bounded_ok)(near_chunk(True))
        pl.when(jnp.logical_not(bounded_ok))(near_chunk(False))

        def far_cond(c):
            top = i - c[0] * w_near
            need = qk[0] + c[1] >= -EXP2_UNDERFLOW
            for hd in range(1, n_heads):
                need = jnp.logical_or(need, qk[hd] + c[1 + hd] >= -EXP2_UNDERFLOW)
            return jnp.logical_and(top >= 0, need)

        def far_body(c):
            top = i - c[0] * w_near
            jc = jnp.maximum(top - (w_near - 1), 0)
            blocks = [jc + p for p in range(w_near)]
            keepc = col < (top + 1 - jc) * ATT_BLK
            new = [c[0] + 1]
            for g in range(n_pairs):
                s_pair = pair_scores(r0, rows_of(k_ref, blocks, g), g)
                ps, alphas = [], []
                for e in range(PAIR):
                    hd = PAIR * g + e
                    tt = [tot_at(top - b, hd) for b in range(w_near)]
                    offs = []
                    for p in range(w_near):
                        behind = top - (jc + p)
                        o = c[1 + hd]
                        for b in range(w_near):
                            o = o + jnp.where(behind >= b, tt[b], 0.0)
                        offs.append(o)
                    s = s_pair[:, e * cw:(e + 1) * cw] + cq[:, hd:hd + 1] + decay_row(blocks, offs, hd)
                    s = jnp.where(keepc, s, NEG)
                    m_old = m_ref[hd]
                    m_new = jnp.maximum(m_old, jnp.max(s, axis=1, keepdims=True))
                    m_ref[hd] = m_new
                    alphas.append(jnp.broadcast_to(jnp.exp2(m_old - m_new), (ATT_BLK, LANES)))
                    ps.append(jnp.exp2(s - m_new).astype(BF16))
                    o = c[1 + hd]
                    for t in tt:
                        o = o + t
                    new.append(o)
                alpha = jnp.where(lo_q, alphas[0], alphas[1])
                r = pair_values(jnp.concatenate(ps, axis=1), rows_of(v_ref, blocks, g))
                acc_ref[g] = alpha * acc_ref[g] + r[:, 0:LANES]
                l_ref[g] = alpha * l_ref[g] + r[:, LANES:]
            return tuple(new)

        far = lax.while_loop(far_cond, far_body, (jnp.int32(1),) + tuple(offs_far))

        @pl.when(far[0] > 1)
        def _():
            for g in range(n_pairs):
                gated_output(rows, g, acc_ref[g], l_ref[g])
        return carry

    @pl.when(step == 0)
    def _():
        straight_ref[0] = 0

    straight = straight_ref[0] == 1
    pl.when(straight)(straight_step)

    @pl.when(jnp.logical_not(straight))
    def _():
        keep_step_rows()
        check_next_step()
        lax.fori_loop(0, nsub, sub_body, 0)
        merge()


def _prompt_attention(tot, qmx, kg, act, ck, x2, ada, wo, fg, *, tm, n_heads):
    rows, d = x2.shape
    a_w = n_heads * HEAD_DIM
    n_pairs = n_heads // PAIR
    assert rows % tm == 0 and tm % ATT_BLK == 0 and tm // ATT_BLK >= NEAR_BLOCKS - 1 and n_heads % PAIR == 0
    row_blk = lambda w: pl.BlockSpec((tm, w), lambda i, *_: (i, 0))
    grid_spec = pltpu.PrefetchScalarGridSpec(
        num_scalar_prefetch=3,
        grid=(rows // tm,),
        in_specs=[row_blk(ACT_PARTS * a_w), _resident(ck.shape), row_blk(d),
                  _resident(ada.shape), _resident(wo.shape), _resident((1, d))],
        out_specs=row_blk(d),
        scratch_shapes=[pltpu.VMEM((rows, a_w), BF16),
                        pltpu.VMEM((rows, a_w), BF16),
                        pltpu.VMEM((tm, a_w), BF16),
                        pltpu.VMEM((n_heads, ATT_BLK, 1), F32),
                        pltpu.VMEM((n_pairs, ATT_BLK, LANES), F32),
                        pltpu.VMEM((n_pairs, ATT_BLK, LANES), F32),
                        pltpu.SMEM((1,), jnp.int32)],
    )
    return pl.pallas_call(
        functools.partial(_attn_kernel, tm=tm, n_heads=n_heads),
        grid_spec=grid_spec,
        out_shape=jax.ShapeDtypeStruct((rows, d), F32),
        compiler_params=pltpu.CompilerParams(dimension_semantics=("arbitrary",), vmem_limit_bytes=VMEM_LIMIT),
        name="attn",
    )(tot, qmx, kg, act, ck, x2, ada, wo, fg)


CACHE_SPLIT = 2


def _sattn_kernel(act_ref, ckn_ref, kc0_ref, kc1_ref, vc0_ref, vc1_ref, lfc_ref,
                  x_ref, ada_ref, wo_ref, fg_ref, y_ref, *, n_heads):
    a_w = n_heads * HEAD_DIM
    q_ref, kn_ref, vn_ref, sa_ref, zp_ref = _act_views(act_ref, a_w)
    ln = act_ref.shape[0]
    past = kc0_ref.shape[2]
    kc_refs, vc_refs = (kc0_ref, kc1_ref), (vc0_ref, vc1_ref)
    pairs_per_part = n_heads // PAIR // CACHE_SPLIT

    def cached(refs, g):
        lo = (g % pairs_per_part) * LANES
        return refs[g // pairs_per_part][0, lo:lo + LANES, :].astype(BF16)

    nb = past // LANES
    lane = lax.broadcasted_iota(jnp.int32, (ln, LANES), 1)
    half = [lane < HEAD_DIM, lane >= HEAD_DIM]
    causal = lax.broadcasted_iota(jnp.int32, (ln, ln), 1) <= lax.broadcasted_iota(jnp.int32, (ln, ln), 0)
    n_pairs = n_heads // PAIR

    def pair_scores(g):
        cols = slice(g * LANES, (g + 1) * LANES)
        q2 = q_ref[:, cols]
        qst = jnp.concatenate([jnp.where(half[e], q2, jnp.zeros_like(q2)) for e in range(PAIR)], axis=0)
        return _dot(qst, cached(kc_refs, g)), _dot_nt(qst, kn_ref[:, cols])

    outs = []
    s_next = pair_scores(0)
    cq = _lanes_to_rows(ckn_ref[0])

    lfc = lfc_ref[0] * LOG2E
    cs = _lane_cumsum(lfc, LANES)
    after = jnp.zeros((n_heads, 1), F32)
    suffix = [None] * nb
    for b in reversed(range(nb)):
        cb = cs[:, b * LANES:(b + 1) * LANES]
        tot = cb[:, LANES - 1:LANES]
        suffix[b] = (tot - cb) + after
        after = after + tot
    dec_c = jnp.concatenate(suffix, axis=1)

    for g in range(n_pairs):
        cols = slice(g * LANES, (g + 1) * LANES)
        sc_st, sn_st = s_next
        if g + 1 < n_pairs:
            s_next = pair_scores(g + 1)
        vct = cached(vc_refs, g)
        vn = vn_ref[:, cols]
        pc, pn, ls = [], [], []
        for e in range(PAIR):
            hd = PAIR * g + e
            rows = slice(e * ln, (e + 1) * ln)
            cqh = cq[:, hd:hd + 1]
            s_c = sc_st[rows] + cqh + dec_c[hd:hd + 1, :]
            s_n = jnp.where(causal, sn_st[rows] + cqh - ckn_ref[0, hd:hd + 1, :], NEG)
            m = jnp.maximum(jnp.max(s_c, axis=1, keepdims=True), jnp.max(s_n, axis=1, keepdims=True))
            p_c = jnp.exp2(s_c - m)
            p_n = jnp.exp2(s_n - m)
            ls.append(jnp.sum(p_c, axis=1, keepdims=True) + jnp.sum(p_n, axis=1, keepdims=True))
            pc.append(p_c.astype(BF16))
            pn.append(p_n.astype(BF16))
        acc = _dot_nt(jnp.concatenate(pc, axis=0), vct) + _dot(jnp.concatenate(pn, axis=0), vn)
        o = jnp.where(half[0], acc[0:ln] / ls[0], acc[ln:2 * ln] / ls[1])
        outs.append((o * sa_ref[:, cols].astype(F32)).astype(BF16))
    za = jnp.concatenate(outs, axis=1)
    y_ref[...] = _merge_norm(x_ref[...], ada_ref[2, pl.ds(ADA_SAMPLE_ROW + pl.program_id(0), 1), :], za, zp_ref[...], wo_ref, fg_ref[...], a_w)


def _sample_attention(act, ckn, cache_k, cache_v, lfc, x2, ada, wo, fg, *, ln, n_heads):
    rows, d = x2.shape
    nbatch = rows // ln
    a_w = n_heads * HEAD_DIM
    past = cache_k.shape[2]
    row_blk = lambda w: pl.BlockSpec((ln, w), lambda b: (b, 0))
    per_b = lambda s: pl.BlockSpec((1,) + s, lambda b: (b, 0, 0))
    cache_part = lambda c: pl.BlockSpec((1, a_w // CACHE_SPLIT, past), lambda b: (b, c, 0))
    assert CACHE_SPLIT == 2 and (n_heads // PAIR) % CACHE_SPLIT == 0
    return pl.pallas_call(
        functools.partial(_sattn_kernel, n_heads=n_heads),
        grid=(nbatch,),
        in_specs=[row_blk(ACT_PARTS * a_w), per_b((n_heads, ln)),
                  cache_part(0), cache_part(1), cache_part(0), cache_part(1), per_b((n_heads, past)),
                  row_blk(d), _resident(ada.shape), _resident(wo.shape), _resident((1, d))],
        out_specs=row_blk(d),
        out_shape=jax.ShapeDtypeStruct((rows, d), F32),
        compiler_params=pltpu.CompilerParams(dimension_semantics=("arbitrary",), vmem_limit_bytes=VMEM_LIMIT),
        name="sattn",
    )(act, ckn, cache_k, cache_k, cache_v, cache_v, lfc, x2, ada, wo, fg)


def kernel(x_prompt, x_sample, c_prompt, c_sample, cache_k, cache_v, cache_logf, state_pool, norm_g, w_ada, b_ada,
           w_in, b_f, w_pool, pool_scale, w_out, final_g):
    depth = norm_g.shape[0]
    assert depth == 1
    bp, seq, d = x_prompt.shape
    bs, ln, _ = x_sample.shape
    assert bp == 1
    n_heads = cache_k.shape[3]
    past = cache_k.shape[2]
    a_w = n_heads * HEAD_DIM
    pw = state_pool.shape[3]
    assert pw == len(POOL_WINDOWS) * LANES and cache_k.shape[4] == HEAD_DIM and n_heads <= 8

    ada = _ada_terms(c_prompt, c_sample, w_ada[0], b_ada)

    wit = w_in[0].T
    wp = w_pool[0]
    ps = pool_scale[0][None, :]
    wo = w_out[0]
    ng = norm_g[0][None, :]
    fg = final_g[None, :]

    assert seq % PROJ_ROWS == 0 and seq % ATTN_ROWS == 0
    xp2 = x_prompt.reshape(seq, d)
    hist_p = jnp.zeros((1, HIST_PAD, pw), F32)
    (act_p, k_p, v_p, lf_p, ck_p, tot, qmx, kg, ho_p) = _project(
        xp2, ada, ng, wit, b_f[0], wp, ps, hist_p,
        bm=PROJ_ROWS, sb=ATT_BLK, segs=1, ada_row=ADA_PROMPT_ROW, start_pos=0, n_heads=n_heads, kv_head_major=False)
    y_p = _prompt_attention(tot, qmx, kg, act_p, ck_p, xp2, ada, wo, fg, tm=ATTN_ROWS, n_heads=n_heads)

    xs2 = x_sample.reshape(bs * ln, d)
    hist_s = jnp.pad(state_pool[0], ((0, 0), (HIST_PAD - POOL_HIST, 0), (0, 0)))
    (act_s, k_s, v_s, lf_s, ck_s, _, _, _, ho_s) = _project(
        xs2, ada, ng, wit, b_f[0], wp, ps, hist_s,
        bm=bs * ln, sb=ln, segs=bs, ada_row=ADA_SAMPLE_ROW, start_pos=past, n_heads=n_heads, kv_head_major=True)
    lfc = jnp.swapaxes(cache_logf[0], 1, 2)
    ckt = jnp.transpose(cache_k[0], (0, 2, 3, 1)).reshape(bs, a_w, past)
    cvt = jnp.transpose(cache_v[0], (0, 2, 3, 1)).reshape(bs, a_w, past)
    y_s = _sample_attention(act_s, ck_s, ckt, cvt, lfc, xs2, ada, wo, fg, ln=ln, n_heads=n_heads)

    hd = (n_heads, HEAD_DIM)
    seq_minor = lambda t: jnp.transpose(t.reshape(hd + (bp, seq)), (2, 3, 0, 1))[None]
    return (y_p.reshape(bp, seq, d), y_s.reshape(bs, ln, d),
            seq_minor(k_p), seq_minor(v_p), jnp.swapaxes(lf_p, 1, 2)[None],
            ho_p[:, 16 - POOL_HIST:, :][None],
            k_s.reshape((1, bs, ln) + hd), v_s.reshape((1, bs, ln) + hd), jnp.swapaxes(lf_s, 1, 2)[None],
            ho_s[:, 16 - POOL_HIST:, :][None])
```

```python
import functools

import jax
import jax.numpy as jnp
from jax import lax
from jax.experimental import pallas as pl
from jax.experimental.pallas import tpu as pltpu

HEAD_DIM = 64
POOL_WINDOWS = (2, 4, 8, 16)
EPS = 1e-6

LANES = 128
PAIR = LANES // HEAD_DIM
ATT_BLK = 128
NEAR_BLOCKS = 3
LOG2E = 1.4426950408889634
EXP2_UNDERFLOW = 151.0
EXP2_SAFE_SPAN = 100.0
NORM_SLACK = 1.01
HIST_PAD = 32
POOL_HIST = max(POOL_WINDOWS) - 1
NEG = -1e30
VMEM_LIMIT = 60 * 1024 * 1024
PROJ_ROWS = 8 * ATT_BLK
ATTN_ROWS = 4 * ATT_BLK

F32 = jnp.float32
BF16 = jnp.bfloat16


def _silu(x):
    return x * jax.nn.sigmoid(x)


def _dot(a, b):
    return jnp.dot(a, b, preferred_element_type=F32)


def _dot_nt(a, b):
    return lax.dot_general(a, b, (((1,), (1,)), ((), ())), preferred_element_type=F32)


def _lane_cumsum(x, n):
    lane = lax.broadcasted_iota(jnp.int32, x.shape, 1) % LANES
    shift = 1
    while shift < n:
        x = x + jnp.where(lane >= shift, pltpu.roll(x, shift, 1), 0.0)
        shift *= 2
    return x


def _rows_to_lanes(x, n):
    rows = x.shape[0]
    if rows < LANES:
        x = jnp.concatenate([x, jnp.zeros((LANES - rows, LANES), x.dtype)], axis=0)
    return x.T[0:n, 0:rows]


def _lanes_to_rows(x):
    n, rows = x.shape
    if rows < LANES:
        x = jnp.concatenate([x, jnp.zeros((n, LANES - rows), x.dtype)], axis=1)
    x = jnp.concatenate([x, jnp.zeros((LANES - n, LANES), x.dtype)], axis=0)
    return x.T[0:rows, :]


def _rows_to_tiles(srcs):
    n = len(srcs)
    rows = srcs[0].shape[0]
    srcs = list(srcs) + [jnp.zeros_like(srcs[0])] * (8 - n)
    v = [s.reshape(rows // 8, 8, LANES) for s in srcs]
    sub = lax.broadcasted_iota(jnp.int32, (rows // 8, 8, LANES), 1)
    for d in (4, 2, 1):
        low = (sub & d) == 0
        nxt = list(v)
        for s in range(8):
            if s & d == 0:
                nxt[s] = jnp.where(low, v[s], pltpu.roll(v[s + d], d, 1))
                nxt[s + d] = jnp.where(low, pltpu.roll(v[s], 8 - d, 1), v[s + d])
        v = nxt
    return jnp.stack(v, axis=1).reshape(rows, 8, LANES)[:, 0:n, :]


ACT_PARTS = 5


def _act_views(act_ref, width):
    return [act_ref.at[:, p * width:(p + 1) * width] for p in range(ACT_PARTS)]


def _resident(shape):
    return pl.BlockSpec(shape, lambda *_: (0,) * len(shape), pipeline_mode=pl.Buffered(1))


ADA_ROWS = 16
ADA_PROMPT_ROW, ADA_SAMPLE_ROW = 0, 8


def _ada_kernel(cp_ref, cs_ref, w_ref, b_ref, o_ref):
    ap = jnp.broadcast_to(_silu(cp_ref[...]), (ADA_SAMPLE_ROW, cp_ref.shape[1]))
    a = jnp.concatenate([ap, _silu(cs_ref[...])], axis=0).astype(BF16)
    o_ref[0] = _dot(a, w_ref[...].astype(BF16)) + b_ref[...]


def _ada_terms(c_prompt, c_sample, w_ada, b_ada):
    d = c_prompt.shape[1]
    assert c_prompt.shape[0] == 1 and c_sample.shape[0] == ADA_ROWS - ADA_SAMPLE_ROW and w_ada.shape[1] == 3 * d
    return pl.pallas_call(
        _ada_kernel,
        grid=(3,),
        in_specs=[pl.BlockSpec(c_prompt.shape, lambda j: (0, 0)),
                  pl.BlockSpec(c_sample.shape, lambda j: (0, 0)),
                  pl.BlockSpec((d, d), lambda j: (0, j)),
                  pl.BlockSpec((1, d), lambda j: (0, j))],
        out_specs=pl.BlockSpec((1, ADA_ROWS, d), lambda j: (j, 0, 0)),
        out_shape=jax.ShapeDtypeStruct((3, ADA_ROWS, d), F32),
        compiler_params=pltpu.CompilerParams(dimension_semantics=("arbitrary",), vmem_limit_bytes=VMEM_LIMIT),
        name="ada",
    )(c_prompt, c_sample, w_ada, b_ada)


def _proj_kernel(bf_ref, x_ref, ada_ref, ng_ref, w_ref, wp_ref, ps_ref, h0_ref,
                 act_ref, k32_ref, v32_ref, lf_ref, cq_ref, ck_ref,
                 tot_ref, qmx_ref, kg_ref, ho_ref,
                 e_ref, t2_ref, t4_ref, t8_ref, kmx_ref, *, bm, sb, segs, ada_row, start_pos, n_heads, kv_head_major):
    a_w = n_heads * HEAD_DIM
    q_ref, kb_ref, vb_ref, sa_ref, zp_ref = _act_views(act_ref, a_w)
    seg_rows = bm // segs
    step = pl.program_id(0)
    pw = len(POOL_WINDOWS) * LANES
    sc = LOG2E / (HEAD_DIM ** 0.5)
    o_pool = 4 * a_w + n_heads

    def w_rows(lo, hi):
        return w_ref[lo:hi, :].astype(BF16)

    def normed(lo, hi):
        x = x_ref[lo:hi, :]
        xn = x * lax.rsqrt(jnp.mean(x * x, axis=-1, keepdims=True) + EPS)
        parts = []
        for g in range(lo // seg_rows, (hi - 1) // seg_rows + 1):
            r0, r1 = max(lo, g * seg_rows) - lo, min(hi, (g + 1) * seg_rows) - lo
            parts.append(xn[r0:r1] * (ng_ref[...] * (1.0 + ada_ref[1, ada_row + g:ada_row + g + 1, :]))
                         + ada_ref[0, ada_row + g:ada_row + g + 1, :])
        return jnp.concatenate(parts, axis=0).astype(BF16)

    n_lead = 4 if segs == 1 else 1
    hs, pus = [], []
    for c in range(n_lead):
        hs.append(normed(c * bm // n_lead, (c + 1) * bm // n_lead))
        pus.append(_dot_nt(hs[-1], w_rows(o_pool, o_pool + pw)))
    pu = jnp.concatenate(pus, axis=0)
    h = jnp.concatenate(hs, axis=0)
    sel = (lax.broadcasted_iota(jnp.int32, (a_w, LANES), 0) // HEAD_DIM
           == lax.broadcasted_iota(jnp.int32, (a_w, LANES), 1)).astype(BF16)

    def store_kv(ref32, refb, p):
        refb[...] = p.astype(BF16)
        if kv_head_major:
            srcs = []
            for g in range(n_heads // PAIR):
                pg = p[:, g * LANES:(g + 1) * LANES]
                srcs += [pg, pltpu.roll(pg, HEAD_DIM, 1)]
            ref32[...] = _rows_to_tiles(srcs)[:, :, 0:HEAD_DIM]
        else:
            ref32[...] = p.T

    ext = HIST_PAD + seg_rows
    n = segs * ext

    def load_history():
        for g in range(segs):
            e_ref[g * ext:g * ext + HIST_PAD, :] = h0_ref[g]

    if segs > 1:
        load_history()
    else:
        pl.when(step == 0)(load_history)

    for g in range(segs):
        e_ref[g * ext + HIST_PAD:(g + 1) * ext, :] = pu[g * seg_rows:(g + 1) * seg_rows]
    t2_ref[8:n, :] = e_ref[8:n, :] + e_ref[7:n - 1, :]
    t4_ref[16:n, :] = t2_ref[16:n, LANES:] + t2_ref[14:n - 2, LANES:]
    t8_ref[24:n, :] = t4_ref[24:n, LANES:] + t4_ref[20:n - 4, LANES:]

    def seg_rows_of(ref, cols, back=0):
        return jnp.concatenate([ref[g * ext + HIST_PAD - back:(g + 1) * ext - back, cols] for g in range(segs)], axis=0)

    lane0, lane1 = slice(0, LANES), slice(LANES, 2 * LANES)
    sums = [seg_rows_of(t2_ref, lane0), seg_rows_of(t4_ref, lane0), seg_rows_of(t8_ref, lane0),
            seg_rows_of(t8_ref, lane1) + seg_rows_of(t8_ref, lane1, back=8)]
    row = lax.broadcasted_iota(jnp.int32, (bm, 1), 0)
    pos1 = start_pos + 1 + (step * bm + row if segs == 1 else row % seg_rows)
    pool_d = []
    for g, w in enumerate(POOL_WINDOWS):
        rc = 1.0 / jnp.minimum(pos1, w).astype(F32)
        pool_d.append((sums[g] * rc - pu[:, g * LANES:(g + 1) * LANES]).astype(BF16))
    for g in range(segs):
        ho_ref[g] = e_ref[(g + 1) * ext - 16:(g + 1) * ext, :]
    if segs == 1:
        e_ref[0:HIST_PAD, :] = e_ref[bm:n, :]

    pk = _dot_nt(h, w_rows(a_w, 2 * a_w))
    store_kv(k32_ref, kb_ref, pk)
    nk = jnp.sqrt(_dot((pk * pk).astype(BF16), sel)) * NORM_SLACK
    spg = _silu(_dot_nt(h, w_rows(o_pool + pw, o_pool + 2 * pw)))

    lane = lax.broadcasted_iota(jnp.int32, (1, LANES), 1)
    bias = jnp.zeros((1, LANES), F32)
    for hd in range(n_heads):
        bias = jnp.where(lane == hd, bf_ref[hd], bias)
    wf = jnp.concatenate([w_ref[4 * a_w:o_pool, :], jnp.zeros((LANES - n_heads, w_ref.shape[1]), F32)], axis=0)
    z = _dot_nt(h, wf.astype(BF16)) + bias
    lf = jnp.minimum(z, 0.0) - jnp.log1p(jnp.exp(-jnp.abs(z)))

    sa_ref[...] = _silu(_dot_nt(h, w_rows(3 * a_w, 4 * a_w))).astype(BF16)

    lfts = []
    for s in range(bm // sb):
        lft = _rows_to_lanes(lf[s * sb:(s + 1) * sb], n_heads)
        off = (s * sb) % seg_rows
        lf_ref[(s * sb) // seg_rows, :, off:off + sb] = lft
        lfts.append(lft if sb == LANES else jnp.concatenate([lft, jnp.zeros((n_heads, LANES - sb), F32)], axis=1))
    c_all = _lane_cumsum(jnp.concatenate(lfts, axis=1) * LOG2E, sb)
    tots, kmx = [], []
    for s in range(bm // sb):
        rows = slice(s * sb, (s + 1) * sb)
        c = c_all[:, s * LANES:(s + 1) * LANES]
        ck_ref[s] = c[:, 0:sb]
        cb = _lanes_to_rows(c)[0:sb]
        cq_ref[rows, :] = cb[:, 0:n_heads]
        tots.append(cb[sb - 1:sb, :])
        kmx.append(jnp.max(nk[rows], axis=0, keepdims=True))
    tot_ref[...] = jnp.concatenate(tots, axis=0)[:, 0:n_heads]
    kmax = kmx[0]
    for t in kmx[1:]:
        kmax = jnp.maximum(kmax, t)

    store_kv(v32_ref, vb_ref, _dot_nt(h, w_rows(2 * a_w, 3 * a_w)))

    zero_w = jnp.zeros((LANES, LANES), BF16)
    for g in range(0, len(POOL_WINDOWS), 2):
        cols = slice(g * LANES, (g + 2) * LANES)
        w2 = jnp.concatenate([jnp.concatenate([wp_ref[g].astype(BF16), zero_w], axis=1),
                              jnp.concatenate([zero_w, wp_ref[g + 1].astype(BF16)], axis=1)], axis=0)
        y = _dot(jnp.concatenate([pool_d[g], pool_d[g + 1]], axis=1), w2) * ps_ref[:, cols]
        zp_ref[:, cols] = (y * spg[:, cols]).astype(BF16)

    qs = _dot_nt(h, w_rows(0, a_w)) * sc
    q_ref[...] = qs.astype(BF16)
    nq = jnp.sqrt(_dot((qs * qs).astype(BF16), sel)) * NORM_SLACK
    qmx_ref[...] = jnp.concatenate([jnp.max(nq[s * sb:(s + 1) * sb], axis=0, keepdims=True)
                                    for s in range(bm // sb)], axis=0)[:, 0:n_heads]

    @pl.when(step > 0)
    def _():
        kmx_ref[...] = jnp.maximum(kmx_ref[...], kmax)

    @pl.when(step == 0)
    def _():
        kmx_ref[...] = kmax

    kg_ref[...] = kmx_ref[...]


def _project(x2, ada, norm_g, wit, b_f, wp, ps, hist0, *, bm, sb, segs, ada_row, start_pos, n_heads,
             kv_head_major):
    rows, d = x2.shape
    a_w = n_heads * HEAD_DIM
    pw = len(POOL_WINDOWS) * LANES
    n_steps = rows // bm
    assert segs == 1 or n_steps == 1
    n_streams = segs
    seg_rows = bm // segs
    nsb = bm // sb
    assert nsb == 8 and seg_rows % sb == 0
    row_blk = lambda w: pl.BlockSpec((bm, w), lambda i, *_: (i, 0))
    per_stream = lambda r, w: pl.BlockSpec((segs, r, w), lambda i, *_: (0, 0, 0))
    kern = functools.partial(_proj_kernel, bm=bm, sb=sb, segs=segs, ada_row=ada_row, start_pos=start_pos,
                             n_heads=n_heads, kv_head_major=kv_head_major)
    if kv_head_major:
        kv_shape = (rows, n_heads, HEAD_DIM)
        kv_blk = pl.BlockSpec((bm, n_heads, HEAD_DIM), lambda i, *_: (i, 0, 0))
    else:
        kv_shape = (a_w, rows)
        kv_blk = pl.BlockSpec((a_w, bm), lambda i, *_: (0, i))
    assert pw == a_w
    out_shape = (
        jax.ShapeDtypeStruct((rows, ACT_PARTS * a_w), BF16),
        jax.ShapeDtypeStruct(kv_shape, F32),
        jax.ShapeDtypeStruct(kv_shape, F32),
        jax.ShapeDtypeStruct((n_streams, n_heads, rows // n_streams), F32),
        jax.ShapeDtypeStruct((rows, n_heads), F32),
        jax.ShapeDtypeStruct((rows // sb, n_heads, sb), F32),
        jax.ShapeDtypeStruct((rows // sb, n_heads), F32),
        jax.ShapeDtypeStruct((rows // sb, n_heads), F32),
        jax.ShapeDtypeStruct((1, LANES), F32),
        jax.ShapeDtypeStruct((n_streams, 16, pw), F32),
    )
    out_specs = (
        row_blk(ACT_PARTS * a_w), kv_blk, kv_blk,
        pl.BlockSpec((segs, n_heads, seg_rows), lambda i, *_: (0, 0, i)),
        row_blk(n_heads),
        pl.BlockSpec((nsb, n_heads, sb), lambda i, *_: (i, 0, 0)),
        pl.BlockSpec((nsb, n_heads), lambda i, *_: (i, 0)),
        pl.BlockSpec((nsb, n_heads), lambda i, *_: (i, 0)),
        pl.BlockSpec((1, LANES), lambda i, *_: (0, 0)),
        per_stream(16, pw),
    )
    in_specs = [
        row_blk(d),
        _resident(ada.shape),
        _resident((1, d)),
        _resident(wit.shape), _resident(wp.shape), _resident(ps.shape),
        per_stream(HIST_PAD, pw),
    ]
    return pl.pallas_call(
        kern,
        grid_spec=pltpu.PrefetchScalarGridSpec(
            num_scalar_prefetch=1,
            grid=(n_steps,),
            in_specs=in_specs,
            out_specs=out_specs,
            scratch_shapes=[pltpu.VMEM((bm + segs * HIST_PAD, pw), F32),
                            pltpu.VMEM((bm + segs * HIST_PAD, pw), F32),
                            pltpu.VMEM((bm + segs * HIST_PAD, pw - LANES), F32),
                            pltpu.VMEM((bm + segs * HIST_PAD, pw - 2 * LANES), F32),
                            pltpu.VMEM((1, LANES), F32)]),
        out_shape=out_shape,
        compiler_params=pltpu.CompilerParams(dimension_semantics=("arbitrary",), vmem_limit_bytes=VMEM_LIMIT),
        name="proj",
    )(b_f, x2, ada, norm_g, wit, wp, ps, hist0)


def _merge_norm(x, gate, za, zp, wo_ref, fg, a_w):
    dy = _dot(za, wo_ref[0:a_w, :].astype(BF16)) + _dot(zp, wo_ref[a_w:, :].astype(BF16))
    out = x + gate * dy
    ms = jnp.mean(out * out, axis=-1, keepdims=True)
    return out * lax.rsqrt(ms + EPS) * fg


def _attn_kernel(tot_ref, qmx_ref, kg_ref,
                 act_ref, cq_ref, ck_ref, x_ref, ada_ref, wo_ref, fg_ref,
                 y_ref,
                 k_ref, v_ref, z_ref, m_ref, l_ref, acc_ref, straight_ref, *, tm, n_heads):
    a_w = n_heads * HEAD_DIM
    q_ref, knew_ref, vnew_ref, sa_ref, zp_ref = _act_views(act_ref, a_w)
    n_pairs = n_heads // PAIR
    nsub = tm // ATT_BLK
    w_near = NEAR_BLOCKS
    cw = w_near * ATT_BLK
    step = pl.program_id(0)

    def keep_step_rows():
        k_ref[pl.ds(pl.multiple_of(step * tm, tm), tm), :] = knew_ref[...]
        v_ref[pl.ds(pl.multiple_of(step * tm, tm), tm), :] = vnew_ref[...]

    lo_q = lax.broadcasted_iota(jnp.int32, (ATT_BLK, LANES), 1) < HEAD_DIM
    lo_k = lax.broadcasted_iota(jnp.int32, (cw, LANES), 1) < HEAD_DIM
    col = lax.broadcasted_iota(jnp.int32, (ATT_BLK, cw), 1)
    tri = (lax.broadcasted_iota(jnp.int32, (ATT_BLK, ATT_BLK), 1)
           <= lax.broadcasted_iota(jnp.int32, (ATT_BLK, ATT_BLK), 0))
    tri_bias = jnp.where(tri, 0.0, NEG).astype(F32)
    zeros_k = jnp.zeros((cw, LANES), BF16)
    ind_lo = jnp.where(lo_k, 1.0, 0.0).astype(BF16)
    ind_hi = jnp.where(lo_k, 0.0, 1.0).astype(BF16)

    def tot_at(b, hd):
        return jnp.where(b >= 0, tot_ref[jnp.maximum(b, 0), hd], 0.0)

    def rows_of(ref, blocks, g):
        return jnp.concatenate(
            [ref[pl.ds(pl.multiple_of(b * ATT_BLK, ATT_BLK), ATT_BLK), g * LANES:(g + 1) * LANES] for b in blocks],
            axis=0)

    def step_rows_of(ref, new_ref, sub, g):
        parts = []
        for p in range(w_near):
            rel = sub - (w_near - 1) + p
            if rel >= 0:
                parts.append(new_ref[rel * ATT_BLK:(rel + 1) * ATT_BLK, g * LANES:(g + 1) * LANES])
            else:
                start = pl.multiple_of((step * nsub + rel) * ATT_BLK, ATT_BLK)
                parts.append(ref[pl.ds(start, ATT_BLK), g * LANES:(g + 1) * LANES])
        return jnp.concatenate(parts, axis=0)

    def pair_scores(r0, kc, g):
        keys = jnp.concatenate([jnp.where(lo_k, kc, zeros_k), jnp.where(lo_k, zeros_k, kc)], axis=0)
        return _dot_nt(q_ref[pl.ds(r0, ATT_BLK), g * LANES:(g + 1) * LANES], keys)

    def pair_values(p_pair, vc):
        vals = jnp.concatenate([jnp.concatenate([jnp.where(lo_k, vc, zeros_k), ind_lo], axis=1),
                                jnp.concatenate([jnp.where(lo_k, zeros_k, vc), ind_hi], axis=1)], axis=0)
        return _dot(p_pair, vals)

    def decay_row(blocks, offs, hd):
        return jnp.concatenate([offs[p] - ck_ref[blocks[p], hd:hd + 1, :] for p in range(w_near)], axis=1)

    def near_weights(s_pair, rows, g, near_c, near_offs, m_bound):
        bounded = m_bound is not None
        ps, ms = [], []
        for e in range(PAIR):
            hd = PAIR * g + e
            cqh = cq_ref[rows, hd:hd + 1]
            if bounded:
                m = jnp.full((ATT_BLK, 1), m_bound[hd], F32)
                cqh = cqh - m_bound[hd]
            dec = decay_row(near_c, near_offs[hd], hd)
            pieces = []
            for p in range(w_near):
                lanes = slice(e * cw + p * ATT_BLK, e * cw + (p + 1) * ATT_BLK)
                sp = s_pair[:, lanes] + cqh + dec[:, p * ATT_BLK:(p + 1) * ATT_BLK]
                pieces.append(sp + tri_bias if p == w_near - 1 else sp)
            s = jnp.concatenate(pieces, axis=1)
            if not bounded:
                m = jnp.max(s, axis=1, keepdims=True)
                s = s - m
            ms.append(m)
            ps.append(jnp.exp2(s).astype(BF16))
        return jnp.concatenate(ps, axis=1), ms

    def gated_output(rows, g, acc, l):
        cols = slice(g * LANES, (g + 1) * LANES)
        z_ref[rows, cols] = ((acc / l) * sa_ref[rows, cols].astype(F32)).astype(BF16)

    def merge(chunks=1):
        rows_c = tm // chunks
        for c in range(chunks):
            rows = slice(c * rows_c, (c + 1) * rows_c)
            y_ref[rows, :] = _merge_norm(x_ref[rows, :], ada_ref[2, ADA_PROMPT_ROW:ADA_PROMPT_ROW + 1, :],
                                         z_ref[rows, :], zp_ref[rows, :], wo_ref, fg_ref[...], a_w)

    def check_next_step():
        ok = jnp.bool_(True)
        last = qmx_ref.shape[0] - 1
        for sub in range(nsub):
            i = jnp.minimum((step + 1) * nsub + sub, last)
            for hd in range(n_heads):
                qk = 2.0 * qmx_ref[i, hd] * kg_ref[0, hd]
                back = qk
                for dd in range(1, w_near):
                    back = back + tot_ref[i - dd, hd]
                ok = jnp.logical_and(ok, jnp.logical_and(qk <= EXP2_SAFE_SPAN, back < -EXP2_UNDERFLOW))
        straight_ref[0] = ok.astype(jnp.int32)

    def straight_step():
        units = [(sub, g) for sub in range(nsub) for g in range(n_pairs)]
        near_of, offs_of, bound_of = [], [], []
        for sub in range(nsub):
            i = step * nsub + sub
            near_of.append([i - (w_near - 1) + p for p in range(w_near)])
            bound_of.append([qmx_ref[i, hd] * kg_ref[0, hd] for hd in range(n_heads)])
            offs = []
            for hd in range(n_heads):
                o, per_piece = jnp.float32(0.0), [jnp.float32(0.0)]
                for dd in range(1, w_near):
                    o = o + tot_ref[i - dd, hd]
                    per_piece.append(o)
                offs.append(per_piece[::-1])
            offs_of.append(offs)
        s_next = pair_scores(0, step_rows_of(k_ref, knew_ref, 0, 0), 0)
        for u, (sub, g) in enumerate(units):
            s_pair = s_next
            if u + 1 < len(units):
                sub1, g1 = units[u + 1]
                s_next = pair_scores(sub1 * ATT_BLK, step_rows_of(k_ref, knew_ref, sub1, g1), g1)
            rows = pl.ds(sub * ATT_BLK, ATT_BLK)
            p_pair, _ = near_weights(s_pair, rows, g, near_of[sub], offs_of[sub], bound_of[sub])
            r = pair_values(p_pair, step_rows_of(v_ref, vnew_ref, sub, g))
            gated_output(rows, g, r[:, 0:LANES], r[:, LANES:])
        check_next_step()
        keep_step_rows()
        merge(chunks=2)

    def sub_body(sub, carry):
        i = step * nsub + sub
        r0 = pl.multiple_of(sub * ATT_BLK, ATT_BLK)

        rows = pl.ds(r0, ATT_BLK)
        qk = [2.0 * qmx_ref[i, hd] * kg_ref[0, hd] for hd in range(n_heads)]

        near = [i - (w_near - 1) + p for p in range(w_near)]
        near_c = [jnp.maximum(b, 0) for b in near]
        near_offs, offs_far = [], []
        for hd in range(n_heads):
            back = [tot_at(i - dd, hd) for dd in range(1, w_near)]
            offs = []
            for p in range(w_near):
                o = jnp.float32(0.0)
                for dd in range(1, w_near - p):
                    o = o + back[dd - 1]
                offs.append(jnp.where(near[p] >= 0, o, NEG))
            near_offs.append(offs)
            o = jnp.float32(0.0)
            for t in back:
                o = o + t
            offs_far.append(o)

        def near_chunk(bounded):
            def fn():
                s_next = pair_scores(r0, rows_of(k_ref, near_c, 0), 0)
                for g in range(n_pairs):
                    s_pair = s_next
                    if g + 1 < n_pairs:
                        s_next = pair_scores(r0, rows_of(k_ref, near_c, g + 1), g + 1)
                    p_pair, ms = near_weights(s_pair, rows, g, near_c, near_offs,
                                              [0.5 * b for b in qk] if bounded else None)
                    for e in range(PAIR):
                        m_ref[PAIR * g + e] = ms[e]
                    r = pair_values(p_pair, rows_of(v_ref, near_c, g))
                    acc_ref[g] = r[:, 0:LANES]
                    l_ref[g] = r[:, LANES:]
                    gated_output(rows, g, r[:, 0:LANES], r[:, LANES:])
            return fn

        bounded_ok = qk[0] <= EXP2_SAFE_SPAN
        for hd in range(1, n_heads):
            bounded_ok = jnp.logical_and(bounded_ok, qk[hd] <= EXP2_SAFE_SPAN)
        pl.when(bounded_ok)(near_chunk(True))
        pl.when(jnp.logical_not(bounded_ok))(near_chunk(False))

        def far_cond(c):
            top = i - c[0] * w_near
            need = qk[0] + c[1] >= -EXP2_UNDERFLOW
            for hd in range(1, n_heads):
                need = jnp.logical_or(need, qk[hd] + c[1 + hd] >= -EXP2_UNDERFLOW)
            return jnp.logical_and(top >= 0, need)

        def far_body(c):
            top = i - c[0] * w_near
            jc = jnp.maximum(top - (w_near - 1), 0)
            blocks = [jc + p for p in range(w_near)]
            keepc = col < (top + 1 - jc) * ATT_BLK
            new = [c[0] + 1]
            for g in range(n_pairs):
                s_pair = pair_scores(r0, rows_of(k_ref, blocks, g), g)
                ps, alphas = [], []
                for e in range(PAIR):
                    hd = PAIR * g + e
                    tt = [tot_at(top - b, hd) for b in range(w_near)]
                    offs = []
                    for p in range(w_near):
                        behind = top - (jc + p)
                        o = c[1 + hd]
                        for b in range(w_near):
                            o = o + jnp.where(behind >= b, tt[b], 0.0)
                        offs.append(o)
                    s = s_pair[:, e * cw:(e + 1) * cw] + cq_ref[rows, hd:hd + 1] + decay_row(blocks, offs, hd)
                    s = jnp.where(keepc, s, NEG)
                    m_old = m_ref[hd]
                    m_new = jnp.maximum(m_old, jnp.max(s, axis=1, keepdims=True))
                    m_ref[hd] = m_new
                    alphas.append(jnp.broadcast_to(jnp.exp2(m_old - m_new), (ATT_BLK, LANES)))
                    ps.append(jnp.exp2(s - m_new).astype(BF16))
                    o = c[1 + hd]
                    for t in tt:
                        o = o + t
                    new.append(o)
                alpha = jnp.where(lo_q, alphas[0], alphas[1])
                r = pair_values(jnp.concatenate(ps, axis=1), rows_of(v_ref, blocks, g))
                acc_ref[g] = alpha * acc_ref[g] + r[:, 0:LANES]
                l_ref[g] = alpha * l_ref[g] + r[:, LANES:]
            return tuple(new)

        far = lax.while_loop(far_cond, far_body, (jnp.int32(1),) + tuple(offs_far))

        @pl.when(far[0] > 1)
        def _():
            for g in range(n_pairs):
                gated_output(rows, g, acc_ref[g], l_ref[g])
        return carry

    @pl.when(step == 0)
    def _():
        straight_ref[0] = 0

    straight = straight_ref[0] == 1
    pl.when(straight)(straight_step)

    @pl.when(jnp.logical_not(straight))
    def _():
        keep_step_rows()
        check_next_step()
        lax.fori_loop(0, nsub, sub_body, 0)
        merge()


def _prompt_attention(tot, qmx, kg, act, cq, ck, x2, ada, wo, fg, *, tm, n_heads):
    rows, d = x2.shape
    a_w = n_heads * HEAD_DIM
    n_pairs = n_heads // PAIR
    assert rows % tm == 0 and tm % ATT_BLK == 0 and tm // ATT_BLK >= NEAR_BLOCKS - 1 and n_heads % PAIR == 0
    row_blk = lambda w: pl.BlockSpec((tm, w), lambda i, *_: (i, 0))
    grid_spec = pltpu.PrefetchScalarGridSpec(
        num_scalar_prefetch=3,
        grid=(rows // tm,),
        in_specs=[row_blk(ACT_PARTS * a_w), row_blk(n_heads), _resident(ck.shape), row_blk(d),
                  _resident(ada.shape), _resident(wo.shape), _resident((1, d))],
        out_specs=row_blk(d),
        scratch_shapes=[pltpu.VMEM((rows, a_w), BF16),
                        pltpu.VMEM((rows, a_w), BF16),
                        pltpu.VMEM((tm, a_w), BF16),
                        pltpu.VMEM((n_heads, ATT_BLK, 1), F32),
                        pltpu.VMEM((n_pairs, ATT_BLK, LANES), F32),
                        pltpu.VMEM((n_pairs, ATT_BLK, LANES), F32),
                        pltpu.SMEM((1,), jnp.int32)],
    )
    return pl.pallas_call(
        functools.partial(_attn_kernel, tm=tm, n_heads=n_heads),
        grid_spec=grid_spec,
        out_shape=jax.ShapeDtypeStruct((rows, d), F32),
        compiler_params=pltpu.CompilerParams(dimension_semantics=("arbitrary",), vmem_limit_bytes=VMEM_LIMIT),
        name="attn",
    )(tot, qmx, kg, act, cq, ck, x2, ada, wo, fg)


CACHE_SPLIT = 2


def _sattn_kernel(act_ref, cq_ref, ckn_ref, kc0_ref, kc1_ref, vc0_ref, vc1_ref, lfc_ref,
                  x_ref, ada_ref, wo_ref, fg_ref, y_ref, *, n_heads):
    a_w = n_heads * HEAD_DIM
    q_ref, kn_ref, vn_ref, sa_ref, zp_ref = _act_views(act_ref, a_w)
    ln = act_ref.shape[0]
    past = kc0_ref.shape[2]
    kc_refs, vc_refs = (kc0_ref, kc1_ref), (vc0_ref, vc1_ref)
    pairs_per_part = n_heads // PAIR // CACHE_SPLIT

    def cached(refs, g):
        lo = (g % pairs_per_part) * LANES
        return refs[g // pairs_per_part][0, lo:lo + LANES, :].astype(BF16)

    nb = past // LANES
    lane = lax.broadcasted_iota(jnp.int32, (ln, LANES), 1)
    half = [lane < HEAD_DIM, lane >= HEAD_DIM]
    causal = lax.broadcasted_iota(jnp.int32, (ln, ln), 1) <= lax.broadcasted_iota(jnp.int32, (ln, ln), 0)
    n_pairs = n_heads // PAIR

    def pair_scores(g):
        cols = slice(g * LANES, (g + 1) * LANES)
        q2 = q_ref[:, cols]
        qst = jnp.concatenate([jnp.where(half[e], q2, jnp.zeros_like(q2)) for e in range(PAIR)], axis=0)
        return _dot(qst, cached(kc_refs, g)), _dot_nt(qst, kn_ref[:, cols])

    outs = []
    s_next = pair_scores(0)

    lfc = lfc_ref[0] * LOG2E
    cs = _lane_cumsum(lfc, LANES)
    after = jnp.zeros((n_heads, 1), F32)
    suffix = [None] * nb
    for b in reversed(range(nb)):
        cb = cs[:, b * LANES:(b + 1) * LANES]
        tot = cb[:, LANES - 1:LANES]
        suffix[b] = (tot - cb) + after
        after = after + tot
    dec_c = jnp.concatenate(suffix, axis=1)

    for g in range(n_pairs):
        cols = slice(g * LANES, (g + 1) * LANES)
        sc_st, sn_st = s_next
        if g + 1 < n_pairs:
            s_next = pair_scores(g + 1)
        vct = cached(vc_refs, g)
        vn = vn_ref[:, cols]
        pc, pn, ls = [], [], []
        for e in range(PAIR):
            hd = PAIR * g + e
            rows = slice(e * ln, (e + 1) * ln)
            cqh = cq_ref[:, hd:hd + 1]
            s_c = sc_st[rows] + cqh + dec_c[hd:hd + 1, :]
            s_n = jnp.where(causal, sn_st[rows] + cqh - ckn_ref[0, hd:hd + 1, :], NEG)
            m = jnp.maximum(jnp.max(s_c, axis=1, keepdims=True), jnp.max(s_n, axis=1, keepdims=True))
            p_c = jnp.exp2(s_c - m)
            p_n = jnp.exp2(s_n - m)
            ls.append(jnp.sum(p_c, axis=1, keepdims=True) + jnp.sum(p_n, axis=1, keepdims=True))
            pc.append(p_c.astype(BF16))
            pn.append(p_n.astype(BF16))
        acc = _dot_nt(jnp.concatenate(pc, axis=0), vct) + _dot(jnp.concatenate(pn, axis=0), vn)
        o = jnp.where(half[0], acc[0:ln] / ls[0], acc[ln:2 * ln] / ls[1])
        outs.append((o * sa_ref[:, cols].astype(F32)).astype(BF16))
    za = jnp.concatenate(outs, axis=1)
    y_ref[...] = _merge_norm(x_ref[...], ada_ref[2, pl.ds(ADA_SAMPLE_ROW + pl.program_id(0), 1), :], za, zp_ref[...], wo_ref, fg_ref[...], a_w)


def _sample_attention(act, cq, ckn, cache_k, cache_v, lfc, x2, ada, wo, fg, *, ln, n_heads):
    rows, d = x2.shape
    nbatch = rows // ln
    a_w = n_heads * HEAD_DIM
    past = cache_k.shape[2]
    row_blk = lambda w: pl.BlockSpec((ln, w), lambda b: (b, 0))
    per_b = lambda s: pl.BlockSpec((1,) + s, lambda b: (b, 0, 0))
    cache_part = lambda c: pl.BlockSpec((1, a_w // CACHE_SPLIT, past), lambda b: (b, c, 0))
    assert CACHE_SPLIT == 2 and (n_heads // PAIR) % CACHE_SPLIT == 0
    return pl.pallas_call(
        functools.partial(_sattn_kernel, n_heads=n_heads),
        grid=(nbatch,),
        in_specs=[row_blk(ACT_PARTS * a_w), row_blk(n_heads), per_b((n_heads, ln)),
                  cache_part(0), cache_part(1), cache_part(0), cache_part(1), per_b((n_heads, past)),
                  row_blk(d), _resident(ada.shape), _resident(wo.shape), _resident((1, d))],
        out_specs=row_blk(d),
        out_shape=jax.ShapeDtypeStruct((rows, d), F32),
        compiler_params=pltpu.CompilerParams(dimension_semantics=("arbitrary",), vmem_limit_bytes=VMEM_LIMIT),
        name="sattn",
    )(act, cq, ckn, cache_k, cache_k, cache_v, cache_v, lfc, x2, ada, wo, fg)


def kernel(x_prompt, x_sample, c_prompt, c_sample, cache_k, cache_v, cache_logf, state_pool, norm_g, w_ada, b_ada,
           w_in, b_f, w_pool, pool_scale, w_out, final_g):
    depth = norm_g.shape[0]
    assert depth == 1
    bp, seq, d = x_prompt.shape
    bs, ln, _ = x_sample.shape
    assert bp == 1
    n_heads = cache_k.shape[3]
    past = cache_k.shape[2]
    a_w = n_heads * HEAD_DIM
    pw = state_pool.shape[3]
    assert pw == len(POOL_WINDOWS) * LANES and cache_k.shape[4] == HEAD_DIM and n_heads <= 8

    ada = _ada_terms(c_prompt, c_sample, w_ada[0], b_ada)

    wit = w_in[0].T
    wp = w_pool[0]
    ps = pool_scale[0][None, :]
    wo = w_out[0]
    ng = norm_g[0][None, :]
    fg = final_g[None, :]

    assert seq % PROJ_ROWS == 0 and seq % ATTN_ROWS == 0
    xp2 = x_prompt.reshape(seq, d)
    hist_p = jnp.zeros((1, HIST_PAD, pw), F32)
    (act_p, k_p, v_p, lf_p, cq_p, ck_p, tot, qmx, kg, ho_p) = _project(
        xp2, ada, ng, wit, b_f[0], wp, ps, hist_p,
        bm=PROJ_ROWS, sb=ATT_BLK, segs=1, ada_row=ADA_PROMPT_ROW, start_pos=0, n_heads=n_heads, kv_head_major=False)
    y_p = _prompt_attention(tot, qmx, kg, act_p, cq_p, ck_p, xp2, ada, wo, fg, tm=ATTN_ROWS, n_heads=n_heads)

    xs2 = x_sample.reshape(bs * ln, d)
    hist_s = jnp.pad(state_pool[0], ((0, 0), (HIST_PAD - POOL_HIST, 0), (0, 0)))
    (act_s, k_s, v_s, lf_s, cq_s, ck_s, _, _, _, ho_s) = _project(
        xs2, ada, ng, wit, b_f[0], wp, ps, hist_s,
        bm=bs * ln, sb=ln, segs=bs, ada_row=ADA_SAMPLE_ROW, start_pos=past, n_heads=n_heads, kv_head_major=True)
    lfc = jnp.swapaxes(cache_logf[0], 1, 2)
    ckt = jnp.transpose(cache_k[0], (0, 2, 3, 1)).reshape(bs, a_w, past)
    cvt = jnp.transpose(cache_v[0], (0, 2, 3, 1)).reshape(bs, a_w, past)
    y_s = _sample_attention(act_s, cq_s, ck_s, ckt, cvt, lfc, xs2, ada, wo, fg, ln=ln, n_heads=n_heads)

    hd = (n_heads, HEAD_DIM)
    seq_minor = lambda t: jnp.transpose(t.reshape(hd + (bp, seq)), (2, 3, 0, 1))[None]
    return (y_p.reshape(bp, seq, d), y_s.reshape(bs, ln, d),
            seq_minor(k_p), seq_minor(v_p), jnp.swapaxes(lf_p, 1, 2)[None],
            ho_p[:, 16 - POOL_HIST:, :][None],
            k_s.reshape((1, bs, ln) + hd), v_s.reshape((1, bs, ln) + hd), jnp.swapaxes(lf_s, 1, 2)[None],
            ho_s[:, 16 - POOL_HIST:, :][None])
```

```python
import functools

import jax
import jax.numpy as jnp
from jax import lax
from jax.experimental import pallas as pl
from jax.experimental.pallas import tpu as pltpu

HEAD_DIM = 64
POOL_WINDOWS = (2, 4, 8, 16)
EPS = 1e-6

LANES = 128
PAIR = LANES // HEAD_DIM
ATT_BLK = 128
NEAR_BLOCKS = 3
LOG2E = 1.4426950408889634
EXP2_UNDERFLOW = 151.0
EXP2_SAFE_SPAN = 100.0
NORM_SLACK = 1.01
HIST_PAD = 32
POOL_HIST = max(POOL_WINDOWS) - 1
NEG = -1e30
VMEM_LIMIT = 60 * 1024 * 1024
PROJ_ROWS = 8 * ATT_BLK
ATTN_ROWS = 4 * ATT_BLK

F32 = jnp.float32
BF16 = jnp.bfloat16


def _silu(x):
    return x * jax.nn.sigmoid(x)


def _dot(a, b):
    return jnp.dot(a, b, preferred_element_type=F32)


def _dot_nt(a, b):
    return lax.dot_general(a, b, (((1,), (1,)), ((), ())), preferred_element_type=F32)


def _lane_cumsum(x, n):
    lane = lax.broadcasted_iota(jnp.int32, x.shape, 1) % LANES
    shift = 1
    while shift < n:
        x = x + jnp.where(lane >= shift, pltpu.roll(x, shift, 1), 0.0)
        shift *= 2
    return x


def _rows_to_lanes(x, n):
    rows = x.shape[0]
    if rows < LANES:
        x = jnp.concatenate([x, jnp.zeros((LANES - rows, LANES), x.dtype)], axis=0)
    return x.T[0:n, 0:rows]


def _lanes_to_rows(x):
    n, rows = x.shape
    if rows < LANES:
        x = jnp.concatenate([x, jnp.zeros((n, LANES - rows), x.dtype)], axis=1)
    x = jnp.concatenate([x, jnp.zeros((LANES - n, LANES), x.dtype)], axis=0)
    return x.T[0:rows, :]


def _rows_to_tiles(srcs):
    n = len(srcs)
    rows = srcs[0].shape[0]
    srcs = list(srcs) + [jnp.zeros_like(srcs[0])] * (8 - n)
    v = [s.reshape(rows // 8, 8, LANES) for s in srcs]
    sub = lax.broadcasted_iota(jnp.int32, (rows // 8, 8, LANES), 1)
    for d in (4, 2, 1):
        low = (sub & d) == 0
        nxt = list(v)
        for s in range(8):
            if s & d == 0:
                nxt[s] = jnp.where(low, v[s], pltpu.roll(v[s + d], d, 1))
                nxt[s + d] = jnp.where(low, pltpu.roll(v[s], 8 - d, 1), v[s + d])
        v = nxt
    return jnp.stack(v, axis=1).reshape(rows, 8, LANES)[:, 0:n, :]


ACT_PARTS = 5


def _act_views(act_ref, width):
    return [act_ref.at[:, p * width:(p + 1) * width] for p in range(ACT_PARTS)]


def _resident(shape):
    return pl.BlockSpec(shape, lambda *_: (0,) * len(shape), pipeline_mode=pl.Buffered(1))


ADA_ROWS = 16
ADA_PROMPT_ROW, ADA_SAMPLE_ROW = 0, 8


def _ada_kernel(cp_ref, cs_ref, w_ref, b_ref, o_ref):
    ap = jnp.broadcast_to(_silu(cp_ref[...]), (ADA_SAMPLE_ROW, cp_ref.shape[1]))
    a = jnp.concatenate([ap, _silu(cs_ref[...])], axis=0).astype(BF16)
    o_ref[0] = _dot(a, w_ref[...].astype(BF16)) + b_ref[...]


def _ada_terms(c_prompt, c_sample, w_ada, b_ada):
    d = c_prompt.shape[1]
    assert c_prompt.shape[0] == 1 and c_sample.shape[0] == ADA_ROWS - ADA_SAMPLE_ROW and w_ada.shape[1] == 3 * d
    return pl.pallas_call(
        _ada_kernel,
        grid=(3,),
        in_specs=[pl.BlockSpec(c_prompt.shape, lambda j: (0, 0)),
                  pl.BlockSpec(c_sample.shape, lambda j: (0, 0)),
                  pl.BlockSpec((d, d), lambda j: (0, j)),
                  pl.BlockSpec((1, d), lambda j: (0, j))],
        out_specs=pl.BlockSpec((1, ADA_ROWS, d), lambda j: (j, 0, 0)),
        out_shape=jax.ShapeDtypeStruct((3, ADA_ROWS, d), F32),
        compiler_params=pltpu.CompilerParams(dimension_semantics=("arbitrary",), vmem_limit_bytes=VMEM_LIMIT),
        name="ada",
    )(c_prompt, c_sample, w_ada, b_ada)


def _proj_kernel(bf_ref, x_ref, ada_ref, ng_ref, w_ref, wp_ref, ps_ref, h0_ref,
                 act_ref, k32_ref, v32_ref, lf_ref, cq_ref, ck_ref,
                 tot_ref, qmx_ref, kg_ref, ho_ref,
                 e_ref, t2_ref, t4_ref, t8_ref, kmx_ref, *, bm, sb, segs, ada_row, start_pos, n_heads, kv_head_major):
    a_w = n_heads * HEAD_DIM
    q_ref, kb_ref, vb_ref, sa_ref, zp_ref = _act_views(act_ref, a_w)
    seg_rows = bm // segs
    step = pl.program_id(0)
    pw = len(POOL_WINDOWS) * LANES
    sc = LOG2E / (HEAD_DIM ** 0.5)
    o_pool = 4 * a_w + n_heads

    def w_rows(lo, hi):
        return w_ref[lo:hi, :].astype(BF16)

    def normed(lo, hi):
        x = x_ref[lo:hi, :]
        xn = x * lax.rsqrt(jnp.mean(x * x, axis=-1, keepdims=True) + EPS)
        parts = []
        for g in range(lo // seg_rows, (hi - 1) // seg_rows + 1):
            r0, r1 = max(lo, g * seg_rows) - lo, min(hi, (g + 1) * seg_rows) - lo
            parts.append(xn[r0:r1] * (ng_ref[...] * (1.0 + ada_ref[1, ada_row + g:ada_row + g + 1, :]))
                         + ada_ref[0, ada_row + g:ada_row + g + 1, :])
        return jnp.concatenate(parts, axis=0).astype(BF16)

    n_lead = 4 if segs == 1 else 1
    hs, pus = [], []
    for c in range(n_lead):
        hs.append(normed(c * bm // n_lead, (c + 1) * bm // n_lead))
        pus.append(_dot_nt(hs[-1], w_rows(o_pool, o_pool + pw)))
    pu = jnp.concatenate(pus, axis=0)
    h = jnp.concatenate(hs, axis=0)
    sel = (lax.broadcasted_iota(jnp.int32, (a_w, LANES), 0) // HEAD_DIM
           == lax.broadcasted_iota(jnp.int32, (a_w, LANES), 1)).astype(BF16)

    def store_kv(ref32, refb, p):
        refb[...] = p.astype(BF16)
        if kv_head_major:
            srcs = []
            for g in range(n_heads // PAIR):
                pg = p[:, g * LANES:(g + 1) * LANES]
                srcs += [pg, pltpu.roll(pg, HEAD_DIM, 1)]
            ref32[...] = _rows_to_tiles(srcs)[:, :, 0:HEAD_DIM]
        else:
            ref32[...] = p.T

    ext = HIST_PAD + seg_rows
    n = segs * ext

    def load_history():
        for g in range(segs):
            e_ref[g * ext:g * ext + HIST_PAD, :] = h0_ref[g]

    if segs > 1:
        load_history()
    else:
        pl.when(step == 0)(load_history)

    for g in range(segs):
        e_ref[g * ext + HIST_PAD:(g + 1) * ext, :] = pu[g * seg_rows:(g + 1) * seg_rows]
    t2_ref[8:n, :] = e_ref[8:n, :] + e_ref[7:n - 1, :]
    t4_ref[16:n, :] = t2_ref[16:n, LANES:] + t2_ref[14:n - 2, LANES:]
    t8_ref[24:n, :] = t4_ref[24:n, LANES:] + t4_ref[20:n - 4, LANES:]

    def seg_rows_of(ref, cols, back=0):
        return jnp.concatenate([ref[g * ext + HIST_PAD - back:(g + 1) * ext - back, cols] for g in range(segs)], axis=0)

    lane0, lane1 = slice(0, LANES), slice(LANES, 2 * LANES)
    sums = [seg_rows_of(t2_ref, lane0), seg_rows_of(t4_ref, lane0), seg_rows_of(t8_ref, lane0),
            seg_rows_of(t8_ref, lane1) + seg_rows_of(t8_ref, lane1, back=8)]
    row = lax.broadcasted_iota(jnp.int32, (bm, 1), 0)
    pos1 = start_pos + 1 + (step * bm + row if segs == 1 else row % seg_rows)
    pool_d = []
    inv_pos = 1.0 / pos1.astype(F32)
    for g, w in enumerate(POOL_WINDOWS):
        rc = jnp.maximum(inv_pos, 1.0 / w)
        pool_d.append((sums[g] * rc - pu[:, g * LANES:(g + 1) * LANES]).astype(BF16))
    for g in range(segs):
        ho_ref[g] = e_ref[(g + 1) * ext - 16:(g + 1) * ext, :]
    if segs == 1:
        e_ref[0:HIST_PAD, :] = e_ref[bm:n, :]

    pk = _dot_nt(h, w_rows(a_w, 2 * a_w))
    store_kv(k32_ref, kb_ref, pk)
    nk2 = _dot((pk * pk).astype(BF16), sel)
    spg = _silu(_dot_nt(h, w_rows(o_pool + pw, o_pool + 2 * pw)))

    head = lax.broadcasted_iota(jnp.int32, (n_heads, 1), 0)
    bias = jnp.zeros((n_heads, 1), F32)
    for hd in range(n_heads):
        bias = jnp.where(head == hd, bf_ref[hd], bias)
    wf = jnp.concatenate([w_ref[4 * a_w:o_pool, :], jnp.zeros((LANES - n_heads, w_ref.shape[1]), F32)], axis=0)
    z = _dot_nt(h, wf.astype(BF16))

    sa_ref[...] = _silu(_dot_nt(h, w_rows(3 * a_w, 4 * a_w))).astype(BF16)

    lfts = []
    for s in range(bm // sb):
        zt = _rows_to_lanes(z[s * sb:(s + 1) * sb], n_heads) + bias
        lft = jnp.minimum(zt, 0.0) - jnp.log1p(jnp.exp(-jnp.abs(zt)))
        off = (s * sb) % seg_rows
        lf_ref[(s * sb) // seg_rows, :, off:off + sb] = lft
        lfts.append(lft if sb == LANES else jnp.concatenate([lft, jnp.zeros((n_heads, LANES - sb), F32)], axis=1))
    c_all = _lane_cumsum(jnp.concatenate(lfts, axis=1) * LOG2E, sb)
    tots, kmx = [], []
    for s in range(bm // sb):
        rows = slice(s * sb, (s + 1) * sb)
        c = c_all[:, s * LANES:(s + 1) * LANES]
        ck_ref[s] = c[:, 0:sb]
        cb = _lanes_to_rows(c)[0:sb]
        cq_ref[rows, :] = cb[:, 0:n_heads]
        tots.append(cb[sb - 1:sb, :])
        kmx.append(jnp.max(nk2[rows], axis=0, keepdims=True))
    tot_ref[...] = jnp.concatenate(tots, axis=0)[:, 0:n_heads]
    kmax = kmx[0]
    for t in kmx[1:]:
        kmax = jnp.maximum(kmax, t)
    kmax = jnp.sqrt(kmax) * NORM_SLACK

    store_kv(v32_ref, vb_ref, _dot_nt(h, w_rows(2 * a_w, 3 * a_w)))

    zero_w = jnp.zeros((LANES, LANES), BF16)
    for g in range(0, len(POOL_WINDOWS), 2):
        cols = slice(g * LANES, (g + 2) * LANES)
        w2 = jnp.concatenate([jnp.concatenate([wp_ref[g].astype(BF16), zero_w], axis=1),
                              jnp.concatenate([zero_w, wp_ref[g + 1].astype(BF16)], axis=1)], axis=0)
        y = _dot(jnp.concatenate([pool_d[g], pool_d[g + 1]], axis=1), w2) * ps_ref[:, cols]
        zp_ref[:, cols] = (y * spg[:, cols]).astype(BF16)

    qs = _dot_nt(h, w_rows(0, a_w)) * sc
    q_ref[...] = qs.astype(BF16)
    nq2 = _dot((qs * qs).astype(BF16), sel)
    qmx2 = jnp.concatenate([jnp.max(nq2[s * sb:(s + 1) * sb], axis=0, keepdims=True) for s in range(bm // sb)], axis=0)
    qmx_ref[...] = (jnp.sqrt(qmx2) * NORM_SLACK)[:, 0:n_heads]

    @pl.when(step > 0)
    def _():
        kmx_ref[...] = jnp.maximum(kmx_ref[...], kmax)

    @pl.when(step == 0)
    def _():
        kmx_ref[...] = kmax

    kg_ref[...] = kmx_ref[...]


def _project(x2, ada, norm_g, wit, b_f, wp, ps, hist0, *, bm, sb, segs, ada_row, start_pos, n_heads,
             kv_head_major):
    rows, d = x2.shape
    a_w = n_heads * HEAD_DIM
    pw = len(POOL_WINDOWS) * LANES
    n_steps = rows // bm
    assert segs == 1 or n_steps == 1
    n_streams = segs
    seg_rows = bm // segs
    nsb = bm // sb
    assert nsb == 8 and seg_rows % sb == 0
    row_blk = lambda w: pl.BlockSpec((bm, w), lambda i, *_: (i, 0))
    per_stream = lambda r, w: pl.BlockSpec((segs, r, w), lambda i, *_: (0, 0, 0))
    kern = functools.partial(_proj_kernel, bm=bm, sb=sb, segs=segs, ada_row=ada_row, start_pos=start_pos,
                             n_heads=n_heads, kv_head_major=kv_head_major)
    if kv_head_major:
        kv_shape = (rows, n_heads, HEAD_DIM)
        kv_blk = pl.BlockSpec((bm, n_heads, HEAD_DIM), lambda i, *_: (i, 0, 0))
    else:
        kv_shape = (a_w, rows)
        kv_blk = pl.BlockSpec((a_w, bm), lambda i, *_: (0, i))
    assert pw == a_w
    out_shape = (
        jax.ShapeDtypeStruct((rows, ACT_PARTS * a_w), BF16),
        jax.ShapeDtypeStruct(kv_shape, F32),
        jax.ShapeDtypeStruct(kv_shape, F32),
        jax.ShapeDtypeStruct((n_streams, n_heads, rows // n_streams), F32),
        jax.ShapeDtypeStruct((rows, n_heads), F32),
        jax.ShapeDtypeStruct((rows // sb, n_heads, sb), F32),
        jax.ShapeDtypeStruct((rows // sb, n_heads), F32),
        jax.ShapeDtypeStruct((rows // sb, n_heads), F32),
        jax.ShapeDtypeStruct((1, LANES), F32),
        jax.ShapeDtypeStruct((n_streams, 16, pw), F32),
    )
    out_specs = (
        row_blk(ACT_PARTS * a_w), kv_blk, kv_blk,
        pl.BlockSpec((segs, n_heads, seg_rows), lambda i, *_: (0, 0, i)),
        row_blk(n_heads),
        pl.BlockSpec((nsb, n_heads, sb), lambda i, *_: (i, 0, 0)),
        pl.BlockSpec((nsb, n_heads), lambda i, *_: (i, 0)),
        pl.BlockSpec((nsb, n_heads), lambda i, *_: (i, 0)),
        pl.BlockSpec((1, LANES), lambda i, *_: (0, 0)),
        per_stream(16, pw),
    )
    in_specs = [
        row_blk(d),
        _resident(ada.shape),
        _resident((1, d)),
        _resident(wit.shape), _resident(wp.shape), _resident(ps.shape),
        per_stream(HIST_PAD, pw),
    ]
    return pl.pallas_call(
        kern,
        grid_spec=pltpu.PrefetchScalarGridSpec(
            num_scalar_prefetch=1,
            grid=(n_steps,),
            in_specs=in_specs,
            out_specs=out_specs,
            scratch_shapes=[pltpu.VMEM((bm + segs * HIST_PAD, pw), F32),
                            pltpu.VMEM((bm + segs * HIST_PAD, pw), F32),
                            pltpu.VMEM((bm + segs * HIST_PAD, pw - LANES), F32),
                            pltpu.VMEM((bm + segs * HIST_PAD, pw - 2 * LANES), F32),
                            pltpu.VMEM((1, LANES), F32)]),
        out_shape=out_shape,
        compiler_params=pltpu.CompilerParams(dimension_semantics=("arbitrary",), vmem_limit_bytes=VMEM_LIMIT),
        name="proj",
    )(b_f, x2, ada, norm_g, wit, wp, ps, hist0)


def _merge_norm(x, gate, za, zp, wo_ref, fg, a_w):
    dy = _dot(za, wo_ref[0:a_w, :].astype(BF16)) + _dot(zp, wo_ref[a_w:, :].astype(BF16))
    out = x + gate * dy
    ms = jnp.mean(out * out, axis=-1, keepdims=True)
    return out * lax.rsqrt(ms + EPS) * fg


def _attn_kernel(tot_ref, qmx_ref, kg_ref,
                 act_ref, cq_ref, ck_ref, x_ref, ada_ref, wo_ref, fg_ref,
                 y_ref,
                 k_ref, v_ref, z_ref, m_ref, l_ref, acc_ref, straight_ref, *, tm, n_heads):
    a_w = n_heads * HEAD_DIM
    q_ref, knew_ref, vnew_ref, sa_ref, zp_ref = _act_views(act_ref, a_w)
    n_pairs = n_heads // PAIR
    nsub = tm // ATT_BLK
    w_near = NEAR_BLOCKS
    cw = w_near * ATT_BLK
    step = pl.program_id(0)

    def keep_step_rows():
        k_ref[pl.ds(pl.multiple_of(step * tm, tm), tm), :] = knew_ref[...]
        v_ref[pl.ds(pl.multiple_of(step * tm, tm), tm), :] = vnew_ref[...]

    lo_q = lax.broadcasted_iota(jnp.int32, (ATT_BLK, LANES), 1) < HEAD_DIM
    lo_k = lax.broadcasted_iota(jnp.int32, (cw, LANES), 1) < HEAD_DIM
    col = lax.broadcasted_iota(jnp.int32, (ATT_BLK, cw), 1)
    tri = (lax.broadcasted_iota(jnp.int32, (ATT_BLK, ATT_BLK), 1)
           <= lax.broadcasted_iota(jnp.int32, (ATT_BLK, ATT_BLK), 0))
    tri_bias = jnp.where(tri, 0.0, NEG).astype(F32)
    zeros_k = jnp.zeros((cw, LANES), BF16)
    ind_lo = jnp.where(lo_k, 1.0, 0.0).astype(BF16)
    ind_hi = jnp.where(lo_k, 0.0, 1.0).astype(BF16)

    def tot_at(b, hd):
        return jnp.where(b >= 0, tot_ref[jnp.maximum(b, 0), hd], 0.0)

    def rows_of(ref, blocks, g):
        return jnp.concatenate(
            [ref[pl.ds(pl.multiple_of(b * ATT_BLK, ATT_BLK), ATT_BLK), g * LANES:(g + 1) * LANES] for b in blocks],
            axis=0)

    def step_rows_of(ref, new_ref, sub, g):
        parts = []
        for p in range(w_near):
            rel = sub - (w_near - 1) + p
            if rel >= 0:
                parts.append(new_ref[rel * ATT_BLK:(rel + 1) * ATT_BLK, g * LANES:(g + 1) * LANES])
            else:
                start = pl.multiple_of((step * nsub + rel) * ATT_BLK, ATT_BLK)
                parts.append(ref[pl.ds(start, ATT_BLK), g * LANES:(g + 1) * LANES])
        return jnp.concatenate(parts, axis=0)

    def pair_scores(r0, kc, g):
        keys = jnp.concatenate([jnp.where(lo_k, kc, zeros_k), jnp.where(lo_k, zeros_k, kc)], axis=0)
        return _dot_nt(q_ref[pl.ds(r0, ATT_BLK), g * LANES:(g + 1) * LANES], keys)

    def pair_values(p_pair, vc):
        vals = jnp.concatenate([jnp.concatenate([jnp.where(lo_k, vc, zeros_k), ind_lo], axis=1),
                                jnp.concatenate([jnp.where(lo_k, zeros_k, vc), ind_hi], axis=1)], axis=0)
        return _dot(p_pair, vals)

    def decay_row(blocks, offs, hd):
        return jnp.concatenate([offs[p] - ck_ref[blocks[p], hd:hd + 1, :] for p in range(w_near)], axis=1)

    def near_weights(s_pair, rows, g, near_c, near_offs, m_bound):
        bounded = m_bound is not None
        ps, ms = [], []
        for e in range(PAIR):
            hd = PAIR * g + e
            cqh = cq_ref[rows, hd:hd + 1]
            if bounded:
                m = jnp.full((ATT_BLK, 1), m_bound[hd], F32)
                cqh = cqh - m_bound[hd]
            dec = decay_row(near_c, near_offs[hd], hd)
            pieces = []
            for p in range(w_near):
                lanes = slice(e * cw + p * ATT_BLK, e * cw + (p + 1) * ATT_BLK)
                sp = s_pair[:, lanes] + cqh + dec[:, p * ATT_BLK:(p + 1) * ATT_BLK]
                pieces.append(sp + tri_bias if p == w_near - 1 else sp)
            s = jnp.concatenate(pieces, axis=1)
            if not bounded:
                m = jnp.max(s, axis=1, keepdims=True)
                s = s - m
            ms.append(m)
            ps.append(jnp.exp2(s).astype(BF16))
        return jnp.concatenate(ps, axis=1), ms

    def gated_output(rows, g, acc, l):
        cols = slice(g * LANES, (g + 1) * LANES)
        z_ref[rows, cols] = ((acc / l) * sa_ref[rows, cols].astype(F32)).astype(BF16)

    def merge(chunks=1):
        rows_c = tm // chunks
        for c in range(chunks):
            rows = slice(c * rows_c, (c + 1) * rows_c)
            y_ref[rows, :] = _merge_norm(x_ref[rows, :], ada_ref[2, ADA_PROMPT_ROW:ADA_PROMPT_ROW + 1, :],
                                         z_ref[rows, :], zp_ref[rows, :], wo_ref, fg_ref[...], a_w)

    def check_next_step():
        ok = jnp.bool_(True)
        last = qmx_ref.shape[0] - 1
        for sub in range(nsub):
            i = jnp.minimum((step + 1) * nsub + sub, last)
            for hd in range(n_heads):
                qk = 2.0 * qmx_ref[i, hd] * kg_ref[0, hd]
                back = qk
                for dd in range(1, w_near):
                    back = back + tot_ref[i - dd, hd]
                ok = jnp.logical_and(ok, jnp.logical_and(qk <= EXP2_SAFE_SPAN, back < -EXP2_UNDERFLOW))
        straight_ref[0] = ok.astype(jnp.int32)

    def straight_step():
        units = [(sub, g) for sub in range(nsub) for g in range(n_pairs)]
        near_of, offs_of, bound_of = [], [], []
        for sub in range(nsub):
            i = step * nsub + sub
            near_of.append([i - (w_near - 1) + p for p in range(w_near)])
            bound_of.append([qmx_ref[i, hd] * kg_ref[0, hd] for hd in range(n_heads)])
            offs = []
            for hd in range(n_heads):
                o, per_piece = jnp.float32(0.0), [jnp.float32(0.0)]
                for dd in range(1, w_near):
                    o = o + tot_ref[i - dd, hd]
                    per_piece.append(o)
                offs.append(per_piece[::-1])
            offs_of.append(offs)
        s_next = pair_scores(0, step_rows_of(k_ref, knew_ref, 0, 0), 0)
        for u, (sub, g) in enumerate(units):
            s_pair = s_next
            if u + 1 < len(units):
                sub1, g1 = units[u + 1]
                s_next = pair_scores(sub1 * ATT_BLK, step_rows_of(k_ref, knew_ref, sub1, g1), g1)
            rows = pl.ds(sub * ATT_BLK, ATT_BLK)
            p_pair, _ = near_weights(s_pair, rows, g, near_of[sub], offs_of[sub], bound_of[sub])
            r = pair_values(p_pair, step_rows_of(v_ref, vnew_ref, sub, g))
            gated_output(rows, g, r[:, 0:LANES], r[:, LANES:])
        check_next_step()
        keep_step_rows()
        merge(chunks=2)

    def sub_body(sub, carry):
        i = step * nsub + sub
        r0 = pl.multiple_of(sub * ATT_BLK, ATT_BLK)

        rows = pl.ds(r0, ATT_BLK)
        qk = [2.0 * qmx_ref[i, hd] * kg_ref[0, hd] for hd in range(n_heads)]

        near = [i - (w_near - 1) + p for p in range(w_near)]
        near_c = [jnp.maximum(b, 0) for b in near]
        near_offs, offs_far = [], []
        for hd in range(n_heads):
            back = [tot_at(i - dd, hd) for dd in range(1, w_near)]
            offs = []
            for p in range(w_near):
                o = jnp.float32(0.0)
                for dd in range(1, w_near - p):
                    o = o + back[dd - 1]
                offs.append(jnp.where(near[p] >= 0, o, NEG))
            near_offs.append(offs)
            o = jnp.float32(0.0)
            for t in back:
                o = o + t
            offs_far.append(o)

        def near_chunk(bounded):
            def fn():
                s_next = pair_scores(r0, rows_of(k_ref, near_c, 0), 0)
                for g in range(n_pairs):
                    s_pair = s_next
                    if g + 1 < n_pairs:
                        s_next = pair_scores(r0, rows_of(k_ref, near_c, g + 1), g + 1)
                    p_pair, ms = near_weights(s_pair, rows, g, near_c, near_offs,
                                              [0.5 * b for b in qk] if bounded else None)
                    for e in range(PAIR):
                        m_ref[PAIR * g + e] = ms[e]
                    r = pair_values(p_pair, rows_of(v_ref, near_c, g))
                    acc_ref[g] = r[:, 0:LANES]
                    l_ref[g] = r[:, LANES:]
                    gated_output(rows, g, r[:, 0:LANES], r[:, LANES:])
            return fn

        bounded_ok = qk[0] <= EXP2_SAFE_SPAN
        for hd in range(1, n_heads):
            bounded_ok = jnp.logical_and(bounded_ok, qk[hd] <= EXP2_SAFE_SPAN)
        pl.when(bounded_ok)(near_chunk(True))
        pl.when(jnp.logical_not(bounded_ok))(near_chunk(False))

        def far_cond(c):
            top = i - c[0] * w_near
            need = qk[0] + c[1] >= -EXP2_UNDERFLOW
            for hd in range(1, n_heads):
                need = jnp.logical_or(need, qk[hd] + c[1 + hd] >= -EXP2_UNDERFLOW)
            return jnp.logical_and(top >= 0, need)

        def far_body(c):
            top = i - c[0] * w_near
            jc = jnp.maximum(top - (w_near - 1), 0)
            blocks = [jc + p for p in range(w_near)]
            keepc = col < (top + 1 - jc) * ATT_BLK
            new = [c[0] + 1]
            for g in range(n_pairs):
                s_pair = pair_scores(r0, rows_of(k_ref, blocks, g), g)
                ps, alphas = [], []
                for e in range(PAIR):
                    hd = PAIR * g + e
                    tt = [tot_at(top - b, hd) for b in range(w_near)]
                    offs = []
                    for p in range(w_near):
                        behind = top - (jc + p)
                        o = c[1 + hd]
                        for b in range(w_near):
                            o = o + jnp.where(behind >= b, tt[b], 0.0)
                        offs.append(o)
                    s = s_pair[:, e * cw:(e + 1) * cw] + cq_ref[rows, hd:hd + 1] + decay_row(blocks, offs, hd)
                    s = jnp.where(keepc, s, NEG)
                    m_old = m_ref[hd]
                    m_new = jnp.maximum(m_old, jnp.max(s, axis=1, keepdims=True))
                    m_ref[hd] = m_new
                    alphas.append(jnp.broadcast_to(jnp.exp2(m_old - m_new), (ATT_BLK, LANES)))
                    ps.append(jnp.exp2(s - m_new).astype(BF16))
                    o = c[1 + hd]
                    for t in tt:
                        o = o + t
                    new.append(o)
                alpha = jnp.where(lo_q, alphas[0], alphas[1])
                r = pair_values(jnp.concatenate(ps, axis=1), rows_of(v_ref, blocks, g))
                acc_ref[g] = alpha * acc_ref[g] + r[:, 0:LANES]
                l_ref[g] = alpha * l_ref[g] + r[:, LANES:]
            return tuple(new)

        far = lax.while_loop(far_cond, far_body, (jnp.int32(1),) + tuple(offs_far))

        @pl.when(far[0] > 1)
        def _():
            for g in range(n_pairs):
                gated_output(rows, g, acc_ref[g], l_ref[g])
        return carry

    @pl.when(step == 0)
    def _():
        straight_ref[0] = 0

    straight = straight_ref[0] == 1
    pl.when(straight)(straight_step)

    @pl.when(jnp.logical_not(straight))
    def _():
        keep_step_rows()
        check_next_step()
        lax.fori_loop(0, nsub, sub_body, 0)
        merge()


def _prompt_attention(tot, qmx, kg, act, cq, ck, x2, ada, wo, fg, *, tm, n_heads):
    rows, d = x2.shape
    a_w = n_heads * HEAD_DIM
    n_pairs = n_heads // PAIR
    assert rows % tm == 0 and tm % ATT_BLK == 0 and tm // ATT_BLK >= NEAR_BLOCKS - 1 and n_heads % PAIR == 0
    row_blk = lambda w: pl.BlockSpec((tm, w), lambda i, *_: (i, 0))
    grid_spec = pltpu.PrefetchScalarGridSpec(
        num_scalar_prefetch=3,
        grid=(rows // tm,),
        in_specs=[row_blk(ACT_PARTS * a_w), row_blk(n_heads), _resident(ck.shape), row_blk(d),
                  _resident(ada.shape), _resident(wo.shape), _resident((1, d))],
        out_specs=row_blk(d),
        scratch_shapes=[pltpu.VMEM((rows, a_w), BF16),
                        pltpu.VMEM((rows, a_w), BF16),
                        pltpu.VMEM((tm, a_w), BF16),
                        pltpu.VMEM((n_heads, ATT_BLK, 1), F32),
                        pltpu.VMEM((n_pairs, ATT_BLK, LANES), F32),
                        pltpu.VMEM((n_pairs, ATT_BLK, LANES), F32),
                        pltpu.SMEM((1,), jnp.int32)],
    )
    return pl.pallas_call(
        functools.partial(_attn_kernel, tm=tm, n_heads=n_heads),
        grid_spec=grid_spec,
        out_shape=jax.ShapeDtypeStruct((rows, d), F32),
        compiler_params=pltpu.CompilerParams(dimension_semantics=("arbitrary",), vmem_limit_bytes=VMEM_LIMIT),
        name="attn",
    )(tot, qmx, kg, act, cq, ck, x2, ada, wo, fg)


CACHE_SPLIT = 2


def _sattn_kernel(act_ref, cq_ref, ckn_ref, kc0_ref, kc1_ref, vc0_ref, vc1_ref, lfc_ref,
                  x_ref, ada_ref, wo_ref, fg_ref, y_ref, *, n_heads):
    a_w = n_heads * HEAD_DIM
    q_ref, kn_ref, vn_ref, sa_ref, zp_ref = _act_views(act_ref, a_w)
    ln = act_ref.shape[0]
    past = kc0_ref.shape[2]
    kc_refs, vc_refs = (kc0_ref, kc1_ref), (vc0_ref, vc1_ref)
    pairs_per_part = n_heads // PAIR // CACHE_SPLIT

    def cached(refs, g):
        lo = (g % pairs_per_part) * LANES
        return refs[g // pairs_per_part][0, lo:lo + LANES, :].astype(BF16)

    nb = past // LANES
    lane = lax.broadcasted_iota(jnp.int32, (ln, LANES), 1)
    half = [lane < HEAD_DIM, lane >= HEAD_DIM]
    causal = lax.broadcasted_iota(jnp.int32, (ln, ln), 1) <= lax.broadcasted_iota(jnp.int32, (ln, ln), 0)
    n_pairs = n_heads // PAIR

    def pair_scores(g):
        cols = slice(g * LANES, (g + 1) * LANES)
        q2 = q_ref[:, cols]
        qst = jnp.concatenate([jnp.where(half[e], q2, jnp.zeros_like(q2)) for e in range(PAIR)], axis=0)
        return _dot(qst, cached(kc_refs, g)), _dot_nt(qst, kn_ref[:, cols])

    outs = []
    s_next = pair_scores(0)

    lfc = lfc_ref[0] * LOG2E
    cs = _lane_cumsum(lfc, LANES)
    after = jnp.zeros((n_heads, 1), F32)
    suffix = [None] * nb
    for b in reversed(range(nb)):
        cb = cs[:, b * LANES:(b + 1) * LANES]
        tot = cb[:, LANES - 1:LANES]
        suffix[b] = (tot - cb) + after
        after = after + tot
    dec_c = jnp.concatenate(suffix, axis=1)

    for g in range(n_pairs):
        cols = slice(g * LANES, (g + 1) * LANES)
        sc_st, sn_st = s_next
        if g + 1 < n_pairs:
            s_next = pair_scores(g + 1)
        vct = cached(vc_refs, g)
        vn = vn_ref[:, cols]
        pc, pn, ls = [], [], []
        for e in range(PAIR):
            hd = PAIR * g + e
            rows = slice(e * ln, (e + 1) * ln)
            cqh = cq_ref[:, hd:hd + 1]
            s_c = sc_st[rows] + cqh + dec_c[hd:hd + 1, :]
            s_n = jnp.where(causal, sn_st[rows] + cqh - ckn_ref[0, hd:hd + 1, :], NEG)
            m = jnp.maximum(jnp.max(s_c, axis=1, keepdims=True), jnp.max(s_n, axis=1, keepdims=True))
            p_c = jnp.exp2(s_c - m)
            p_n = jnp.exp2(s_n - m)
            ls.append(jnp.sum(p_c, axis=1, keepdims=True) + jnp.sum(p_n, axis=1, keepdims=True))
            pc.append(p_c.astype(BF16))
            pn.append(p_n.astype(BF16))
        acc = _dot_nt(jnp.concatenate(pc, axis=0), vct) + _dot(jnp.concatenate(pn, axis=0), vn)
        o = jnp.where(half[0], acc[0:ln] / ls[0], acc[ln:2 * ln] / ls[1])
        outs.append((o * sa_ref[:, cols].astype(F32)).astype(BF16))
    za = jnp.concatenate(outs, axis=1)
    y_ref[...] = _merge_norm(x_ref[...], ada_ref[2, pl.ds(ADA_SAMPLE_ROW + pl.program_id(0), 1), :], za, zp_ref[...], wo_ref, fg_ref[...], a_w)


def _sample_attention(act, cq, ckn, cache_k, cache_v, lfc, x2, ada, wo, fg, *, ln, n_heads):
    rows, d = x2.shape
    nbatch = rows // ln
    a_w = n_heads * HEAD_DIM
    past = cache_k.shape[2]
    row_blk = lambda w: pl.BlockSpec((ln, w), lambda b: (b, 0))
    per_b = lambda s: pl.BlockSpec((1,) + s, lambda b: (b, 0, 0))
    cache_part = lambda c: pl.BlockSpec((1, a_w // CACHE_SPLIT, past), lambda b: (b, c, 0))
    assert CACHE_SPLIT == 2 and (n_heads // PAIR) % CACHE_SPLIT == 0
    return pl.pallas_call(
        functools.partial(_sattn_kernel, n_heads=n_heads),
        grid=(nbatch,),
        in_specs=[row_blk(ACT_PARTS * a_w), row_blk(n_heads), per_b((n_heads, ln)),
                  cache_part(0), cache_part(1), cache_part(0), cache_part(1), per_b((n_heads, past)),
                  row_blk(d), _resident(ada.shape), _resident(wo.shape), _resident((1, d))],
        out_specs=row_blk(d),
        out_shape=jax.ShapeDtypeStruct((rows, d), F32),
        compiler_params=pltpu.CompilerParams(dimension_semantics=("arbitrary",), vmem_limit_bytes=VMEM_LIMIT),
        name="sattn",
    )(act, cq, ckn, cache_k, cache_k, cache_v, cache_v, lfc, x2, ada, wo, fg)


def kernel(x_prompt, x_sample, c_prompt, c_sample, cache_k, cache_v, cache_logf, state_pool, norm_g, w_ada, b_ada,
           w_in, b_f, w_pool, pool_scale, w_out, final_g):
    depth = norm_g.shape[0]
    assert depth == 1
    bp, seq, d = x_prompt.shape
    bs, ln, _ = x_sample.shape
    assert bp == 1
    n_heads = cache_k.shape[3]
    past = cache_k.shape[2]
    a_w = n_heads * HEAD_DIM
    pw = state_pool.shape[3]
    assert pw == len(POOL_WINDOWS) * LANES and cache_k.shape[4] == HEAD_DIM and n_heads <= 8

    ada = _ada_terms(c_prompt, c_sample, w_ada[0], b_ada)

    wit = w_in[0].T
    wp = w_pool[0]
    ps = pool_scale[0][None, :]
    wo = w_out[0]
    ng = norm_g[0][None, :]
    fg = final_g[None, :]

    assert seq % PROJ_ROWS == 0 and seq % ATTN_ROWS == 0
    xp2 = x_prompt.reshape(seq, d)
    hist_p = jnp.zeros((1, HIST_PAD, pw), F32)
    (act_p, k_p, v_p, lf_p, cq_p, ck_p, tot, qmx, kg, ho_p) = _project(
        xp2, ada, ng, wit, b_f[0], wp, ps, hist_p,
        bm=PROJ_ROWS, sb=ATT_BLK, segs=1, ada_row=ADA_PROMPT_ROW, start_pos=0, n_heads=n_heads, kv_head_major=False)
    y_p = _prompt_attention(tot, qmx, kg, act_p, cq_p, ck_p, xp2, ada, wo, fg, tm=ATTN_ROWS, n_heads=n_heads)

    xs2 = x_sample.reshape(bs * ln, d)
    hist_s = jnp.pad(state_pool[0], ((0, 0), (HIST_PAD - POOL_HIST, 0), (0, 0)))
    (act_s, k_s, v_s, lf_s, cq_s, ck_s, _, _, _, ho_s) = _project(
        xs2, ada, ng, wit, b_f[0], wp, ps, hist_s,
        bm=bs * ln, sb=ln, segs=bs, ada_row=ADA_SAMPLE_ROW, start_pos=past, n_heads=n_heads, kv_head_major=True)
    lfc = jnp.swapaxes(cache_logf[0], 1, 2)
    ckt = jnp.transpose(cache_k[0], (0, 2, 3, 1)).reshape(bs, a_w, past)
    cvt = jnp.transpose(cache_v[0], (0, 2, 3, 1)).reshape(bs, a_w, past)
    y_s = _sample_attention(act_s, cq_s, ck_s, ckt, cvt, lfc, xs2, ada, wo, fg, ln=ln, n_heads=n_heads)

    hd = (n_heads, HEAD_DIM)
    seq_minor = lambda t: jnp.transpose(t.reshape(hd + (bp, seq)), (2, 3, 0, 1))[None]
    return (y_p.reshape(bp, seq, d), y_s.reshape(bs, ln, d),
            seq_minor(k_p), seq_minor(v_p), jnp.swapaxes(lf_p, 1, 2)[None],
            ho_p[:, 16 - POOL_HIST:, :][None],
            k_s.reshape((1, bs, ln) + hd), v_s.reshape((1, bs, ln) + hd), jnp.swapaxes(lf_s, 1, 2)[None],
            ho_s[:, 16 - POOL_HIST:, :][None])
```

```python
import functools

import jax
import jax.numpy as jnp
from jax import lax
from jax.experimental import pallas as pl
from jax.experimental.pallas import tpu as pltpu

HEAD_DIM = 64
POOL_WINDOWS = (2, 4, 8, 16)
EPS = 1e-6

LANES = 128
PAIR = LANES // HEAD_DIM
ATT_BLK = 128
NEAR_BLOCKS = 3
LOG2E = 1.4426950408889634
EXP2_UNDERFLOW = 151.0
EXP2_SAFE_SPAN = 100.0
NORM_SLACK = 1.01
HIST_PAD = 32
POOL_HIST = max(POOL_WINDOWS) - 1
NEG = -1e30
VMEM_LIMIT = 60 * 1024 * 1024
PROJ_ROWS = 8 * ATT_BLK
ATTN_ROWS = 4 * ATT_BLK

F32 = jnp.float32
BF16 = jnp.bfloat16


def _silu(x):
    return x * jax.nn.sigmoid(x)


def _dot(a, b):
    return jnp.dot(a, b, preferred_element_type=F32)


def _dot_nt(a, b):
    return lax.dot_general(a, b, (((1,), (1,)), ((), ())), preferred_element_type=F32)


def _lane_cumsum(x, n):
    lane = lax.broadcasted_iota(jnp.int32, x.shape, 1) % LANES
    shift = 1
    while shift < n:
        x = x + jnp.where(lane >= shift, pltpu.roll(x, shift, 1), 0.0)
        shift *= 2
    return x


def _rows_to_lanes(x, n):
    rows = x.shape[0]
    if rows < LANES:
        x = jnp.concatenate([x, jnp.zeros((LANES - rows, LANES), x.dtype)], axis=0)
    return x.T[0:n, 0:rows]


def _lanes_to_rows(x):
    n, rows = x.shape
    if rows < LANES:
        x = jnp.concatenate([x, jnp.zeros((n, LANES - rows), x.dtype)], axis=1)
    x = jnp.concatenate([x, jnp.zeros((LANES - n, LANES), x.dtype)], axis=0)
    return x.T[0:rows, :]


def _rows_to_tiles(srcs):
    n = len(srcs)
    rows = srcs[0].shape[0]
    srcs = list(srcs) + [jnp.zeros_like(srcs[0])] * (8 - n)
    v = [s.reshape(rows // 8, 8, LANES) for s in srcs]
    sub = lax.broadcasted_iota(jnp.int32, (rows // 8, 8, LANES), 1)
    for d in (4, 2, 1):
        low = (sub & d) == 0
        nxt = list(v)
        for s in range(8):
            if s & d == 0:
                nxt[s] = jnp.where(low, v[s], pltpu.roll(v[s + d], d, 1))
                nxt[s + d] = jnp.where(low, pltpu.roll(v[s], 8 - d, 1), v[s + d])
        v = nxt
    return jnp.stack(v, axis=1).reshape(rows, 8, LANES)[:, 0:n, :]


ACT_PARTS = 5


def _act_views(act_ref, width):
    return [act_ref.at[:, p * width:(p + 1) * width] for p in range(ACT_PARTS)]


def _resident(shape):
    return pl.BlockSpec(shape, lambda *_: (0,) * len(shape), pipeline_mode=pl.Buffered(1))


ADA_ROWS = 16
ADA_PROMPT_ROW, ADA_SAMPLE_ROW = 0, 8


def _ada_kernel(cp_ref, cs_ref, w_ref, b_ref, o_ref):
    ap = jnp.broadcast_to(_silu(cp_ref[...]), (ADA_SAMPLE_ROW, cp_ref.shape[1]))
    a = jnp.concatenate([ap, _silu(cs_ref[...])], axis=0).astype(BF16)
    o_ref[0] = _dot(a, w_ref[...].astype(BF16)) + b_ref[...]


def _ada_terms(c_prompt, c_sample, w_ada, b_ada):
    d = c_prompt.shape[1]
    assert c_prompt.shape[0] == 1 and c_sample.shape[0] == ADA_ROWS - ADA_SAMPLE_ROW and w_ada.shape[1] == 3 * d
    return pl.pallas_call(
        _ada_kernel,
        grid=(3,),
        in_specs=[pl.BlockSpec(c_prompt.shape, lambda j: (0, 0)),
                  pl.BlockSpec(c_sample.shape, lambda j: (0, 0)),
                  pl.BlockSpec((d, d), lambda j: (0, j)),
                  pl.BlockSpec((1, d), lambda j: (0, j))],
        out_specs=pl.BlockSpec((1, ADA_ROWS, d), lambda j: (j, 0, 0)),
        out_shape=jax.ShapeDtypeStruct((3, ADA_ROWS, d), F32),
        compiler_params=pltpu.CompilerParams(dimension_semantics=("arbitrary",), vmem_limit_bytes=VMEM_LIMIT),
        name="ada",
    )(c_prompt, c_sample, w_ada, b_ada)


def _proj_kernel(bf_ref, x_ref, ada_ref, ng_ref, w_ref, wp_ref, ps_ref, h0_ref,
                 act_ref, k32_ref, v32_ref, lf_ref, cq_ref, ck_ref,
                 tot_ref, qmx_ref, kg_ref, ho_ref,
                 e_ref, t2_ref, t4_ref, t8_ref, kmx_ref, *, bm, sb, segs, ada_row, start_pos, n_heads, kv_head_major):
    a_w = n_heads * HEAD_DIM
    q_ref, kb_ref, vb_ref, sa_ref, zp_ref = _act_views(act_ref, a_w)
    seg_rows = bm // segs
    step = pl.program_id(0)
    pw = len(POOL_WINDOWS) * LANES
    sc = LOG2E / (HEAD_DIM ** 0.5)
    o_pool = 4 * a_w + n_heads

    def w_rows(lo, hi):
        return w_ref[lo:hi, :].astype(BF16)

    def normed(lo, hi):
        x = x_ref[lo:hi, :]
        xn = x * lax.rsqrt(jnp.mean(x * x, axis=-1, keepdims=True) + EPS)
        parts = []
        for g in range(lo // seg_rows, (hi - 1) // seg_rows + 1):
            r0, r1 = max(lo, g * seg_rows) - lo, min(hi, (g + 1) * seg_rows) - lo
            parts.append(xn[r0:r1] * (ng_ref[...] * (1.0 + ada_ref[1, ada_row + g:ada_row + g + 1, :]))
                         + ada_ref[0, ada_row + g:ada_row + g + 1, :])
        return jnp.concatenate(parts, axis=0).astype(BF16)

    n_lead = 4 if segs == 1 else 1
    hs, pus = [], []
    for c in range(n_lead):
        hs.append(normed(c * bm // n_lead, (c + 1) * bm // n_lead))
        pus.append(_dot_nt(hs[-1], w_rows(o_pool, o_pool + pw)))
    pu = jnp.concatenate(pus, axis=0)
    h = jnp.concatenate(hs, axis=0)
    sel = (lax.broadcasted_iota(jnp.int32, (a_w, LANES), 0) // HEAD_DIM
           == lax.broadcasted_iota(jnp.int32, (a_w, LANES), 1)).astype(BF16)

    def store_kv(ref32, refb, p):
        refb[...] = p.astype(BF16)
        if kv_head_major:
            srcs = []
            for g in range(n_heads // PAIR):
                pg = p[:, g * LANES:(g + 1) * LANES]
                srcs += [pg, pltpu.roll(pg, HEAD_DIM, 1)]
            ref32[...] = _rows_to_tiles(srcs)[:, :, 0:HEAD_DIM]
        else:
            ref32[...] = p.T

    ext = HIST_PAD + seg_rows
    n = segs * ext

    if segs > 1:
        for g in range(segs):
            e_ref[g * ext:g * ext + HIST_PAD, :] = h0_ref[g]
    else:
        e_ref[0:HIST_PAD, :] = jnp.where(step == 0, h0_ref[0], e_ref[0:HIST_PAD, :])

    for g in range(segs):
        e_ref[g * ext + HIST_PAD:(g + 1) * ext, :] = pu[g * seg_rows:(g + 1) * seg_rows]
    t2_ref[8:n, :] = e_ref[8:n, :] + e_ref[7:n - 1, :]
    t4_ref[16:n, :] = t2_ref[16:n, LANES:] + t2_ref[14:n - 2, LANES:]
    t8_ref[24:n, :] = t4_ref[24:n, LANES:] + t4_ref[20:n - 4, LANES:]

    def seg_rows_of(ref, cols, back=0):
        return jnp.concatenate([ref[g * ext + HIST_PAD - back:(g + 1) * ext - back, cols] for g in range(segs)], axis=0)

    lane0, lane1 = slice(0, LANES), slice(LANES, 2 * LANES)
    sums = [seg_rows_of(t2_ref, lane0), seg_rows_of(t4_ref, lane0), seg_rows_of(t8_ref, lane0),
            seg_rows_of(t8_ref, lane1) + seg_rows_of(t8_ref, lane1, back=8)]
    row = lax.broadcasted_iota(jnp.int32, (bm, 1), 0)
    pos1 = start_pos + 1 + (step * bm + row if segs == 1 else row % seg_rows)
    pool_d = []
    inv_pos = 1.0 / pos1.astype(F32)
    for g, w in enumerate(POOL_WINDOWS):
        rc = jnp.maximum(inv_pos, 1.0 / w)
        pool_d.append((sums[g] * rc - pu[:, g * LANES:(g + 1) * LANES]).astype(BF16))
    for g in range(segs):
        ho_ref[g] = e_ref[(g + 1) * ext - 16:(g + 1) * ext, :]
    if segs == 1:
        e_ref[0:HIST_PAD, :] = e_ref[bm:n, :]

    pk = _dot_nt(h, w_rows(a_w, 2 * a_w))
    store_kv(k32_ref, kb_ref, pk)
    nk2 = _dot((pk * pk).astype(BF16), sel)
    spg = _silu(_dot_nt(h, w_rows(o_pool + pw, o_pool + 2 * pw)))

    head = lax.broadcasted_iota(jnp.int32, (n_heads, 1), 0)
    bias = jnp.zeros((n_heads, 1), F32)
    for hd in range(n_heads):
        bias = jnp.where(head == hd, bf_ref[hd], bias)
    wf = jnp.concatenate([w_ref[4 * a_w:o_pool, :], jnp.zeros((LANES - n_heads, w_ref.shape[1]), F32)], axis=0)
    z = _dot_nt(h, wf.astype(BF16))

    sa_ref[...] = _silu(_dot_nt(h, w_rows(3 * a_w, 4 * a_w))).astype(BF16)

    lfts = []
    for s in range(bm // sb):
        zt = _rows_to_lanes(z[s * sb:(s + 1) * sb], n_heads) + bias
        lft = jnp.minimum(zt, 0.0) - jnp.log1p(jnp.exp(-jnp.abs(zt)))
        off = (s * sb) % seg_rows
        lf_ref[(s * sb) // seg_rows, :, off:off + sb] = lft
        lfts.append(lft if sb == LANES else jnp.concatenate([lft, jnp.zeros((n_heads, LANES - sb), F32)], axis=1))
    c_all = _lane_cumsum(jnp.concatenate(lfts, axis=1) * LOG2E, sb)
    tots, kmx = [], []
    for s in range(bm // sb):
        rows = slice(s * sb, (s + 1) * sb)
        c = c_all[:, s * LANES:(s + 1) * LANES]
        ck_ref[s] = c[:, 0:sb]
        cb = _lanes_to_rows(c)[0:sb]
        cq_ref[rows, :] = cb[:, 0:n_heads]
        tots.append(cb[sb - 1:sb, :])
        kmx.append(jnp.max(nk2[rows], axis=0, keepdims=True))
    tot_ref[...] = jnp.concatenate(tots, axis=0)[:, 0:n_heads]
    kmax = kmx[0]
    for t in kmx[1:]:
        kmax = jnp.maximum(kmax, t)
    kmax = jnp.sqrt(kmax) * NORM_SLACK

    store_kv(v32_ref, vb_ref, _dot_nt(h, w_rows(2 * a_w, 3 * a_w)))

    zero_w = jnp.zeros((LANES, LANES), BF16)
    for g in range(0, len(POOL_WINDOWS), 2):
        cols = slice(g * LANES, (g + 2) * LANES)
        w2 = jnp.concatenate([jnp.concatenate([wp_ref[g].astype(BF16), zero_w], axis=1),
                              jnp.concatenate([zero_w, wp_ref[g + 1].astype(BF16)], axis=1)], axis=0)
        y = _dot(jnp.concatenate([pool_d[g], pool_d[g + 1]], axis=1), w2) * ps_ref[:, cols]
        zp_ref[:, cols] = (y * spg[:, cols]).astype(BF16)

    qs = _dot_nt(h, w_rows(0, a_w)) * sc
    q_ref[...] = qs.astype(BF16)
    nq2 = _dot((qs * qs).astype(BF16), sel)
    qmx2 = jnp.concatenate([jnp.max(nq2[s * sb:(s + 1) * sb], axis=0, keepdims=True) for s in range(bm // sb)], axis=0)
    qmx_ref[...] = (jnp.sqrt(qmx2) * NORM_SLACK)[:, 0:n_heads]

    kmax = jnp.where(step == 0, kmax, jnp.maximum(kmx_ref[...], kmax))
    kmx_ref[...] = kmax
    kg_ref[...] = kmax


def _project(x2, ada, norm_g, wit, b_f, wp, ps, hist0, *, bm, sb, segs, ada_row, start_pos, n_heads,
             kv_head_major):
    rows, d = x2.shape
    a_w = n_heads * HEAD_DIM
    pw = len(POOL_WINDOWS) * LANES
    n_steps = rows // bm
    assert segs == 1 or n_steps == 1
    n_streams = segs
    seg_rows = bm // segs
    nsb = bm // sb
    assert nsb == 8 and seg_rows % sb == 0
    row_blk = lambda w: pl.BlockSpec((bm, w), lambda i, *_: (i, 0))
    per_stream = lambda r, w: pl.BlockSpec((segs, r, w), lambda i, *_: (0, 0, 0))
    kern = functools.partial(_proj_kernel, bm=bm, sb=sb, segs=segs, ada_row=ada_row, start_pos=start_pos,
                             n_heads=n_heads, kv_head_major=kv_head_major)
    if kv_head_major:
        kv_shape = (rows, n_heads, HEAD_DIM)
        kv_blk = pl.BlockSpec((bm, n_heads, HEAD_DIM), lambda i, *_: (i, 0, 0))
    else:
        kv_shape = (a_w, rows)
        kv_blk = pl.BlockSpec((a_w, bm), lambda i, *_: (0, i))
    assert pw == a_w
    out_shape = (
        jax.ShapeDtypeStruct((rows, ACT_PARTS * a_w), BF16),
        jax.ShapeDtypeStruct(kv_shape, F32),
        jax.ShapeDtypeStruct(kv_shape, F32),
        jax.ShapeDtypeStruct((n_streams, n_heads, rows // n_streams), F32),
        jax.ShapeDtypeStruct((rows, n_heads), F32),
        jax.ShapeDtypeStruct((rows // sb, n_heads, sb), F32),
        jax.ShapeDtypeStruct((rows // sb, n_heads), F32),
        jax.ShapeDtypeStruct((rows // sb, n_heads), F32),
        jax.ShapeDtypeStruct((1, LANES), F32),
        jax.ShapeDtypeStruct((n_streams, 16, pw), F32),
    )
    out_specs = (
        row_blk(ACT_PARTS * a_w), kv_blk, kv_blk,
        pl.BlockSpec((segs, n_heads, seg_rows), lambda i, *_: (0, 0, i)),
        row_blk(n_heads),
        pl.BlockSpec((nsb, n_heads, sb), lambda i, *_: (i, 0, 0)),
        pl.BlockSpec((nsb, n_heads), lambda i, *_: (i, 0)),
        pl.BlockSpec((nsb, n_heads), lambda i, *_: (i, 0)),
        pl.BlockSpec((1, LANES), lambda i, *_: (0, 0)),
        per_stream(16, pw),
    )
    in_specs = [
        row_blk(d),
        _resident(ada.shape),
        _resident((1, d)),
        _resident(wit.shape), _resident(wp.shape), _resident(ps.shape),
        per_stream(HIST_PAD, pw),
    ]
    return pl.pallas_call(
        kern,
        grid_spec=pltpu.PrefetchScalarGridSpec(
            num_scalar_prefetch=1,
            grid=(n_steps,),
            in_specs=in_specs,
            out_specs=out_specs,
            scratch_shapes=[pltpu.VMEM((bm + segs * HIST_PAD, pw), F32),
                            pltpu.VMEM((bm + segs * HIST_PAD, pw), F32),
                            pltpu.VMEM((bm + segs * HIST_PAD, pw - LANES), F32),
                            pltpu.VMEM((bm + segs * HIST_PAD, pw - 2 * LANES), F32),
                            pltpu.VMEM((1, LANES), F32)]),
        out_shape=out_shape,
        compiler_params=pltpu.CompilerParams(dimension_semantics=("arbitrary",), vmem_limit_bytes=VMEM_LIMIT),
        name="proj",
    )(b_f, x2, ada, norm_g, wit, wp, ps, hist0)


def _merge_norm(x, gate, za, zp, wo_ref, fg, a_w):
    dy = _dot(za, wo_ref[0:a_w, :].astype(BF16)) + _dot(zp, wo_ref[a_w:, :].astype(BF16))
    out = x + gate * dy
    ms = jnp.mean(out * out, axis=-1, keepdims=True)
    return out * lax.rsqrt(ms + EPS) * fg


def _attn_kernel(tot_ref, qmx_ref, kg_ref,
                 act_ref, cq_ref, ck_ref, x_ref, ada_ref, wo_ref, fg_ref,
                 y_ref,
                 k_ref, v_ref, z_ref, m_ref, l_ref, acc_ref, straight_ref, *, tm, n_heads):
    a_w = n_heads * HEAD_DIM
    q_ref, knew_ref, vnew_ref, sa_ref, zp_ref = _act_views(act_ref, a_w)
    n_pairs = n_heads // PAIR
    nsub = tm // ATT_BLK
    w_near = NEAR_BLOCKS
    cw = w_near * ATT_BLK
    step = pl.program_id(0)

    def keep_step_rows():
        k_ref[pl.ds(pl.multiple_of(step * tm, tm), tm), :] = knew_ref[...]
        v_ref[pl.ds(pl.multiple_of(step * tm, tm), tm), :] = vnew_ref[...]

    lo_q = lax.broadcasted_iota(jnp.int32, (ATT_BLK, LANES), 1) < HEAD_DIM
    lo_k = lax.broadcasted_iota(jnp.int32, (cw, LANES), 1) < HEAD_DIM
    col = lax.broadcasted_iota(jnp.int32, (ATT_BLK, cw), 1)
    tri = (lax.broadcasted_iota(jnp.int32, (ATT_BLK, ATT_BLK), 1)
           <= lax.broadcasted_iota(jnp.int32, (ATT_BLK, ATT_BLK), 0))
    tri_bias = jnp.where(tri, 0.0, NEG).astype(F32)
    zeros_k = jnp.zeros((cw, LANES), BF16)
    ind_lo = jnp.where(lo_k, 1.0, 0.0).astype(BF16)
    ind_hi = jnp.where(lo_k, 0.0, 1.0).astype(BF16)

    def tot_at(b, hd):
        return jnp.where(b >= 0, tot_ref[jnp.maximum(b, 0), hd], 0.0)

    def rows_of(ref, blocks, g):
        return jnp.concatenate(
            [ref[pl.ds(pl.multiple_of(b * ATT_BLK, ATT_BLK), ATT_BLK), g * LANES:(g + 1) * LANES] for b in blocks],
            axis=0)

    def step_rows_of(ref, new_ref, sub, g):
        parts = []
        for p in range(w_near):
            rel = sub - (w_near - 1) + p
            if rel >= 0:
                parts.append(new_ref[rel * ATT_BLK:(rel + 1) * ATT_BLK, g * LANES:(g + 1) * LANES])
            else:
                start = pl.multiple_of((step * nsub + rel) * ATT_BLK, ATT_BLK)
                parts.append(ref[pl.ds(start, ATT_BLK), g * LANES:(g + 1) * LANES])
        return jnp.concatenate(parts, axis=0)

    def pair_scores(r0, kc, g):
        keys = jnp.concatenate([jnp.where(lo_k, kc, zeros_k), jnp.where(lo_k, zeros_k, kc)], axis=0)
        return _dot_nt(q_ref[pl.ds(r0, ATT_BLK), g * LANES:(g + 1) * LANES], keys)

    def pair_values(p_pair, vc):
        vals = jnp.concatenate([jnp.concatenate([jnp.where(lo_k, vc, zeros_k), ind_lo], axis=1),
                                jnp.concatenate([jnp.where(lo_k, zeros_k, vc), ind_hi], axis=1)], axis=0)
        return _dot(p_pair, vals)

    def decay_row(blocks, offs, hd):
        return jnp.concatenate([offs[p] - ck_ref[blocks[p], hd:hd + 1, :] for p in range(w_near)], axis=1)

    def near_weights(s_pair, rows, g, near_c, near_offs, m_bound):
        bounded = m_bound is not None
        ps, ms = [], []
        for e in range(PAIR):
            hd = PAIR * g + e
            cqh = cq_ref[rows, hd:hd + 1]
            if bounded:
                m = jnp.full((ATT_BLK, 1), m_bound[hd], F32)
                cqh = cqh - m_bound[hd]
            dec = decay_row(near_c, near_offs[hd], hd)
            pieces = []
            for p in range(w_near):
                lanes = slice(e * cw + p * ATT_BLK, e * cw + (p + 1) * ATT_BLK)
                sp = s_pair[:, lanes] + cqh + dec[:, p * ATT_BLK:(p + 1) * ATT_BLK]
                pieces.append(sp + tri_bias if p == w_near - 1 else sp)
            s = jnp.concatenate(pieces, axis=1)
            if not bounded:
                m = jnp.max(s, axis=1, keepdims=True)
                s = s - m
            ms.append(m)
            ps.append(jnp.exp2(s).astype(BF16))
        return jnp.concatenate(ps, axis=1), ms

    def gated_output(rows, g, acc, l):
        cols = slice(g * LANES, (g + 1) * LANES)
        z_ref[rows, cols] = ((acc / l) * sa_ref[rows, cols].astype(F32)).astype(BF16)

    def merge(chunks=1):
        rows_c = tm // chunks
        for c in range(chunks):
            rows = slice(c * rows_c, (c + 1) * rows_c)
            y_ref[rows, :] = _merge_norm(x_ref[rows, :], ada_ref[2, ADA_PROMPT_ROW:ADA_PROMPT_ROW + 1, :],
                                         z_ref[rows, :], zp_ref[rows, :], wo_ref, fg_ref[...], a_w)

    def check_next_step():
        ok = jnp.bool_(True)
        last = qmx_ref.shape[0] - 1
        for sub in range(nsub):
            i = jnp.minimum((step + 1) * nsub + sub, last)
            for hd in range(n_heads):
                qk = 2.0 * qmx_ref[i, hd] * kg_ref[0, hd]
                back = qk
                for dd in range(1, w_near):
                    back = back + tot_ref[i - dd, hd]
                ok = jnp.logical_and(ok, jnp.logical_and(qk <= EXP2_SAFE_SPAN, back < -EXP2_UNDERFLOW))
        straight_ref[0] = ok.astype(jnp.int32)

    def straight_step():
        units = [(sub, g) for sub in range(nsub) for g in range(n_pairs)]
        near_of, offs_of, bound_of = [], [], []
        for sub in range(nsub):
            i = step * nsub + sub
            near_of.append([i - (w_near - 1) + p for p in range(w_near)])
            bound_of.append([qmx_ref[i, hd] * kg_ref[0, hd] for hd in range(n_heads)])
            offs = []
            for hd in range(n_heads):
                o, per_piece = jnp.float32(0.0), [jnp.float32(0.0)]
                for dd in range(1, w_near):
                    o = o + tot_ref[i - dd, hd]
                    per_piece.append(o)
                offs.append(per_piece[::-1])
            offs_of.append(offs)
        s_next = pair_scores(0, step_rows_of(k_ref, knew_ref, 0, 0), 0)
        for u, (sub, g) in enumerate(units):
            s_pair = s_next
            if u + 1 < len(units):
                sub1, g1 = units[u + 1]
                s_next = pair_scores(sub1 * ATT_BLK, step_rows_of(k_ref, knew_ref, sub1, g1), g1)
            rows = pl.ds(sub * ATT_BLK, ATT_BLK)
            p_pair, _ = near_weights(s_pair, rows, g, near_of[sub], offs_of[sub], bound_of[sub])
            r = pair_values(p_pair, step_rows_of(v_ref, vnew_ref, sub, g))
            gated_output(rows, g, r[:, 0:LANES], r[:, LANES:])
        check_next_step()
        keep_step_rows()
        merge(chunks=2)

    def sub_body(sub, carry):
        i = step * nsub + sub
        r0 = pl.multiple_of(sub * ATT_BLK, ATT_BLK)

        rows = pl.ds(r0, ATT_BLK)
        qk = [2.0 * qmx_ref[i, hd] * kg_ref[0, hd] for hd in range(n_heads)]

        near = [i - (w_near - 1) + p for p in range(w_near)]
        near_c = [jnp.maximum(b, 0) for b in near]
        near_offs, offs_far = [], []
        for hd in range(n_heads):
            back = [tot_at(i - dd, hd) for dd in range(1, w_near)]
            offs = []
            for p in range(w_near):
                o = jnp.float32(0.0)
                for dd in range(1, w_near - p):
                    o = o + back[dd - 1]
                offs.append(jnp.where(near[p] >= 0, o, NEG))
            near_offs.append(offs)
            o = jnp.float32(0.0)
            for t in back:
                o = o + t
            offs_far.append(o)

        def near_chunk(bounded):
            def fn():
                s_next = pair_scores(r0, rows_of(k_ref, near_c, 0), 0)
                for g in range(n_pairs):
                    s_pair = s_next
                    if g + 1 < n_pairs:
                        s_next = pair_scores(r0, rows_of(k_ref, near_c, g + 1), g + 1)
                    p_pair, ms = near_weights(s_pair, rows, g, near_c, near_offs,
                                              [0.5 * b for b in qk] if bounded else None)
                    for e in range(PAIR):
                        m_ref[PAIR * g + e] = ms[e]
                    r = pair_values(p_pair, rows_of(v_ref, near_c, g))
                    acc_ref[g] = r[:, 0:LANES]
                    l_ref[g] = r[:, LANES:]
                    gated_output(rows, g, r[:, 0:LANES], r[:, LANES:])
            return fn

        bounded_ok = qk[0] <= EXP2_SAFE_SPAN
        for hd in range(1, n_heads):
            bounded_ok = jnp.logical_and(bounded_ok, qk[hd] <= EXP2_SAFE_SPAN)
        pl.when(bounded_ok)(near_chunk(True))
        pl.when(jnp.logical_not(bounded_ok))(near_chunk(False))

        def far_cond(c):
            top = i - c[0] * w_near
            need = qk[0] + c[1] >= -EXP2_UNDERFLOW
            for hd in range(1, n_heads):
                need = jnp.logical_or(need, qk[hd] + c[1 + hd] >= -EXP2_UNDERFLOW)
            return jnp.logical_and(top >= 0, need)

        def far_body(c):
            top = i - c[0] * w_near
            jc = jnp.maximum(top - (w_near - 1), 0)
            blocks = [jc + p for p in range(w_near)]
            keepc = col < (top + 1 - jc) * ATT_BLK
            new = [c[0] + 1]
            for g in range(n_pairs):
                s_pair = pair_scores(r0, rows_of(k_ref, blocks, g), g)
                ps, alphas = [], []
                for e in range(PAIR):
                    hd = PAIR * g + e
                    tt = [tot_at(top - b, hd) for b in range(w_near)]
                    offs = []
                    for p in range(w_near):
                        behind = top - (jc + p)
                        o = c[1 + hd]
                        for b in range(w_near):
                            o = o + jnp.where(behind >= b, tt[b], 0.0)
                        offs.append(o)
                    s = s_pair[:, e * cw:(e + 1) * cw] + cq_ref[rows, hd:hd + 1] + decay_row(blocks, offs, hd)
                    s = jnp.where(keepc, s, NEG)
                    m_old = m_ref[hd]
                    m_new = jnp.maximum(m_old, jnp.max(s, axis=1, keepdims=True))
                    m_ref[hd] = m_new
                    alphas.append(jnp.broadcast_to(jnp.exp2(m_old - m_new), (ATT_BLK, LANES)))
                    ps.append(jnp.exp2(s - m_new).astype(BF16))
                    o = c[1 + hd]
                    for t in tt:
                        o = o + t
                    new.append(o)
                alpha = jnp.where(lo_q, alphas[0], alphas[1])
                r = pair_values(jnp.concatenate(ps, axis=1), rows_of(v_ref, blocks, g))
                acc_ref[g] = alpha * acc_ref[g] + r[:, 0:LANES]
                l_ref[g] = alpha * l_ref[g] + r[:, LANES:]
            return tuple(new)

        far = lax.while_loop(far_cond, far_body, (jnp.int32(1),) + tuple(offs_far))

        @pl.when(far[0] > 1)
        def _():
            for g in range(n_pairs):
                gated_output(rows, g, acc_ref[g], l_ref[g])
        return carry

    @pl.when(step == 0)
    def _():
        straight_ref[0] = 0

    straight = straight_ref[0] == 1
    pl.when(straight)(straight_step)

    @pl.when(jnp.logical_not(straight))
    def _():
        keep_step_rows()
        check_next_step()
        lax.fori_loop(0, nsub, sub_body, 0)
        merge()


def _prompt_attention(tot, qmx, kg, act, cq, ck, x2, ada, wo, fg, *, tm, n_heads):
    rows, d = x2.shape
    a_w = n_heads * HEAD_DIM
    n_pairs = n_heads // PAIR
    assert rows % tm == 0 and tm % ATT_BLK == 0 and tm // ATT_BLK >= NEAR_BLOCKS - 1 and n_heads % PAIR == 0
    row_blk = lambda w: pl.BlockSpec((tm, w), lambda i, *_: (i, 0))
    grid_spec = pltpu.PrefetchScalarGridSpec(
        num_scalar_prefetch=3,
        grid=(rows // tm,),
        in_specs=[row_blk(ACT_PARTS * a_w), row_blk(n_heads), _resident(ck.shape), row_blk(d),
                  _resident(ada.shape), _resident(wo.shape), _resident((1, d))],
        out_specs=row_blk(d),
        scratch_shapes=[pltpu.VMEM((rows, a_w), BF16),
                        pltpu.VMEM((rows, a_w), BF16),
                        pltpu.VMEM((tm, a_w), BF16),
                        pltpu.VMEM((n_heads, ATT_BLK, 1), F32),
                        pltpu.VMEM((n_pairs, ATT_BLK, LANES), F32),
                        pltpu.VMEM((n_pairs, ATT_BLK, LANES), F32),
                        pltpu.SMEM((1,), jnp.int32)],
    )
    return pl.pallas_call(
        functools.partial(_attn_kernel, tm=tm, n_heads=n_heads),
        grid_spec=grid_spec,
        out_shape=jax.ShapeDtypeStruct((rows, d), F32),
        compiler_params=pltpu.CompilerParams(dimension_semantics=("arbitrary",), vmem_limit_bytes=VMEM_LIMIT),
        name="attn",
    )(tot, qmx, kg, act, cq, ck, x2, ada, wo, fg)


CACHE_SPLIT = 2


def _sattn_kernel(act_ref, cq_ref, ckn_ref, kc0_ref, kc1_ref, vc0_ref, vc1_ref, lfc_ref,
                  x_ref, ada_ref, wo_ref, fg_ref, y_ref, *, n_heads):
    a_w = n_heads * HEAD_DIM
    q_ref, kn_ref, vn_ref, sa_ref, zp_ref = _act_views(act_ref, a_w)
    ln = act_ref.shape[0]
    past = kc0_ref.shape[2]
    kc_refs, vc_refs = (kc0_ref, kc1_ref), (vc0_ref, vc1_ref)
    pairs_per_part = n_heads // PAIR // CACHE_SPLIT

    def cached(refs, g):
        lo = (g % pairs_per_part) * LANES
        return refs[g // pairs_per_part][0, lo:lo + LANES, :].astype(BF16)

    nb = past // LANES
    lane = lax.broadcasted_iota(jnp.int32, (ln, LANES), 1)
    half = [lane < HEAD_DIM, lane >= HEAD_DIM]
    causal = lax.broadcasted_iota(jnp.int32, (ln, ln), 1) <= lax.broadcasted_iota(jnp.int32, (ln, ln), 0)
    n_pairs = n_heads // PAIR

    def pair_scores(g):
        cols = slice(g * LANES, (g + 1) * LANES)
        q2 = q_ref[:, cols]
        qst = jnp.concatenate([jnp.where(half[e], q2, jnp.zeros_like(q2)) for e in range(PAIR)], axis=0)
        return _dot(qst, cached(kc_refs, g)), _dot_nt(qst, kn_ref[:, cols])

    outs = []
    s_next = pair_scores(0)

    lfc = lfc_ref[0] * LOG2E
    cs = _lane_cumsum(lfc, LANES)
    after = jnp.zeros((n_heads, 1), F32)
    suffix = [None] * nb
    for b in reversed(range(nb)):
        cb = cs[:, b * LANES:(b + 1) * LANES]
        tot = cb[:, LANES - 1:LANES]
        suffix[b] = (tot - cb) + after
        after = after + tot
    dec_c = jnp.concatenate(suffix, axis=1)

    for g in range(n_pairs):
        cols = slice(g * LANES, (g + 1) * LANES)
        sc_st, sn_st = s_next
        if g + 1 < n_pairs:
            s_next = pair_scores(g + 1)
        vct = cached(vc_refs, g)
        vn = vn_ref[:, cols]
        pc, pn, ls = [], [], []
        for e in range(PAIR):
            hd = PAIR * g + e
            rows = slice(e * ln, (e + 1) * ln)
            cqh = cq_ref[:, hd:hd + 1]
            s_c = sc_st[rows] + cqh + dec_c[hd:hd + 1, :]
            s_n = jnp.where(causal, sn_st[rows] + cqh - ckn_ref[0, hd:hd + 1, :], NEG)
            m = jnp.maximum(jnp.max(s_c, axis=1, keepdims=True), jnp.max(s_n, axis=1, keepdims=True))
            p_c = jnp.exp2(s_c - m)
            p_n = jnp.exp2(s_n - m)
            ls.append(jnp.sum(p_c, axis=1, keepdims=True) + jnp.sum(p_n, axis=1, keepdims=True))
            pc.append(p_c.astype(BF16))
            pn.append(p_n.astype(BF16))
        acc = _dot_nt(jnp.concatenate(pc, axis=0), vct) + _dot(jnp.concatenate(pn, axis=0), vn)
        o = jnp.where(half[0], acc[0:ln] / ls[0], acc[ln:2 * ln] / ls[1])
        outs.append((o * sa_ref[:, cols].astype(F32)).astype(BF16))
    za = jnp.concatenate(outs, axis=1)
    y_ref[...] = _merge_norm(x_ref[...], ada_ref[2, pl.ds(ADA_SAMPLE_ROW + pl.program_id(0), 1), :], za, zp_ref[...], wo_ref, fg_ref[...], a_w)


def _sample_attention(act, cq, ckn, cache_k, cache_v, lfc, x2, ada, wo, fg, *, ln, n_heads):
    rows, d = x2.shape
    nbatch = rows // ln
    a_w = n_heads * HEAD_DIM
    past = cache_k.shape[2]
    row_blk = lambda w: pl.BlockSpec((ln, w), lambda b: (b, 0))
    per_b = lambda s: pl.BlockSpec((1,) + s, lambda b: (b, 0, 0))
    cache_part = lambda c: pl.BlockSpec((1, a_w // CACHE_SPLIT, past), lambda b: (b, c, 0))
    assert CACHE_SPLIT == 2 and (n_heads // PAIR) % CACHE_SPLIT == 0
    return pl.pallas_call(
        functools.partial(_sattn_kernel, n_heads=n_heads),
        grid=(nbatch,),
        in_specs=[row_blk(ACT_PARTS * a_w), row_blk(n_heads), per_b((n_heads, ln)),
                  cache_part(0), cache_part(1), cache_part(0), cache_part(1), per_b((n_heads, past)),
                  row_blk(d), _resident(ada.shape), _resident(wo.shape), _resident((1, d))],
        out_specs=row_blk(d),
        out_shape=jax.ShapeDtypeStruct((rows, d), F32),
        compiler_params=pltpu.CompilerParams(dimension_semantics=("arbitrary",), vmem_limit_bytes=VMEM_LIMIT),
        name="sattn",
    )(act, cq, ckn, cache_k, cache_k, cache_v, cache_v, lfc, x2, ada, wo, fg)


def kernel(x_prompt, x_sample, c_prompt, c_sample, cache_k, cache_v, cache_logf, state_pool, norm_g, w_ada, b_ada,
           w_in, b_f, w_pool, pool_scale, w_out, final_g):
    depth = norm_g.shape[0]
    assert depth == 1
    bp, seq, d = x_prompt.shape
    bs, ln, _ = x_sample.shape
    assert bp == 1
    n_heads = cache_k.shape[3]
    past = cache_k.shape[2]
    a_w = n_heads * HEAD_DIM
    pw = state_pool.shape[3]
    assert pw == len(POOL_WINDOWS) * LANES and cache_k.shape[4] == HEAD_DIM and n_heads <= 8

    ada = _ada_terms(c_prompt, c_sample, w_ada[0], b_ada)

    wit = w_in[0].T
    wp = w_pool[0]
    ps = pool_scale[0][None, :]
    wo = w_out[0]
    ng = norm_g[0][None, :]
    fg = final_g[None, :]

    assert seq % PROJ_ROWS == 0 and seq % ATTN_ROWS == 0
    xp2 = x_prompt.reshape(seq, d)
    hist_p = jnp.zeros((1, HIST_PAD, pw), F32)
    (act_p, k_p, v_p, lf_p, cq_p, ck_p, tot, qmx, kg, ho_p) = _project(
        xp2, ada, ng, wit, b_f[0], wp, ps, hist_p,
        bm=PROJ_ROWS, sb=ATT_BLK, segs=1, ada_row=ADA_PROMPT_ROW, start_pos=0, n_heads=n_heads, kv_head_major=False)
    y_p = _prompt_attention(tot, qmx, kg, act_p, cq_p, ck_p, xp2, ada, wo, fg, tm=ATTN_ROWS, n_heads=n_heads)

    xs2 = x_sample.reshape(bs * ln, d)
    hist_s = jnp.pad(state_pool[0], ((0, 0), (HIST_PAD - POOL_HIST, 0), (0, 0)))
    (act_s, k_s, v_s, lf_s, cq_s, ck_s, _, _, _, ho_s) = _project(
        xs2, ada, ng, wit, b_f[0], wp, ps, hist_s,
        bm=bs * ln, sb=ln, segs=bs, ada_row=ADA_SAMPLE_ROW, start_pos=past, n_heads=n_heads, kv_head_major=True)
    lfc = jnp.swapaxes(cache_logf[0], 1, 2)
    ckt = jnp.transpose(cache_k[0], (0, 2, 3, 1)).reshape(bs, a_w, past)
    cvt = jnp.transpose(cache_v[0], (0, 2, 3, 1)).reshape(bs, a_w, past)
    y_s = _sample_attention(act_s, cq_s, ck_s, ckt, cvt, lfc, xs2, ada, wo, fg, ln=ln, n_heads=n_heads)

    hd = (n_heads, HEAD_DIM)
    seq_minor = lambda t: jnp.transpose(t.reshape(hd + (bp, seq)), (2, 3, 0, 1))[None]
    return (y_p.reshape(bp, seq, d), y_s.reshape(bs, ln, d),
            seq_minor(k_p), seq_minor(v_p), jnp.swapaxes(lf_p, 1, 2)[None],
            ho_p[:, 16 - POOL_HIST:, :][None],
            k_s.reshape((1, bs, ln) + hd), v_s.reshape((1, bs, ln) + hd), jnp.swapaxes(lf_s, 1, 2)[None],
            ho_s[:, 16 - POOL_HIST:, :][None])
```

```python
import functools

import jax
import jax.numpy as jnp
from jax import lax
from jax.experimental import pallas as pl
from jax.experimental.pallas import tpu as pltpu

HEAD_DIM = 64
POOL_WINDOWS = (2, 4, 8, 16)
EPS = 1e-6

LANES = 128
PAIR = LANES // HEAD_DIM
ATT_BLK = 128
NEAR_BLOCKS = 3
LOG2E = 1.4426950408889634
EXP2_UNDERFLOW = 151.0
EXP2_SAFE_SPAN = 100.0
NORM_SLACK = 1.01
HIST_PAD = 32
POOL_HIST = max(POOL_WINDOWS) - 1
NEG = -1e30
VMEM_LIMIT = 60 * 1024 * 1024
PROJ_ROWS = 8 * ATT_BLK
ATTN_ROWS = 4 * ATT_BLK

F32 = jnp.float32
BF16 = jnp.bfloat16


def _silu(x):
    return x * jax.nn.sigmoid(x)


def _dot(a, b):
    return jnp.dot(a, b, preferred_element_type=F32)


def _dot_nt(a, b):
    return lax.dot_general(a, b, (((1,), (1,)), ((), ())), preferred_element_type=F32)


def _lane_cumsum(x, n):
    lane = lax.broadcasted_iota(jnp.int32, x.shape, 1) % LANES
    shift = 1
    while shift < n:
        x = x + jnp.where(lane >= shift, pltpu.roll(x, shift, 1), 0.0)
        shift *= 2
    return x


def _rows_to_lanes(x, n):
    rows = x.shape[0]
    if rows < LANES:
        x = jnp.concatenate([x, jnp.zeros((LANES - rows, LANES), x.dtype)], axis=0)
    return x.T[0:n, 0:rows]


def _lanes_to_rows(x):
    n, rows = x.shape
    if rows < LANES:
        x = jnp.concatenate([x, jnp.zeros((n, LANES - rows), x.dtype)], axis=1)
    x = jnp.concatenate([x, jnp.zeros((LANES - n, LANES), x.dtype)], axis=0)
    return x.T[0:rows, :]


def _rows_to_tiles(srcs):
    n = len(srcs)
    rows = srcs[0].shape[0]
    srcs = list(srcs) + [jnp.zeros_like(srcs[0])] * (8 - n)
    v = [s.reshape(rows // 8, 8, LANES) for s in srcs]
    sub = lax.broadcasted_iota(jnp.int32, (rows // 8, 8, LANES), 1)
    for d in (4, 2, 1):
        low = (sub & d) == 0
        nxt = list(v)
        for s in range(8):
            if s & d == 0:
                nxt[s] = jnp.where(low, v[s], pltpu.roll(v[s + d], d, 1))
                nxt[s + d] = jnp.where(low, pltpu.roll(v[s], 8 - d, 1), v[s + d])
        v = nxt
    return jnp.stack(v, axis=1).reshape(rows, 8, LANES)[:, 0:n, :]


ACT_PARTS = 5


def _act_views(act_ref, width):
    return [act_ref.at[:, p * width:(p + 1) * width] for p in range(ACT_PARTS)]


def _resident(shape):
    return pl.BlockSpec(shape, lambda *_: (0,) * len(shape), pipeline_mode=pl.Buffered(1))


ADA_ROWS = 16
ADA_PROMPT_ROW, ADA_SAMPLE_ROW = 0, 8


def _ada_kernel(cp_ref, cs_ref, w_ref, b_ref, o_ref):
    ap = jnp.broadcast_to(_silu(cp_ref[...]), (ADA_SAMPLE_ROW, cp_ref.shape[1]))
    a = jnp.concatenate([ap, _silu(cs_ref[...])], axis=0).astype(BF16)
    o_ref[0] = _dot(a, w_ref[...].astype(BF16)) + b_ref[...]


def _ada_terms(c_prompt, c_sample, w_ada, b_ada):
    d = c_prompt.shape[1]
    assert c_prompt.shape[0] == 1 and c_sample.shape[0] == ADA_ROWS - ADA_SAMPLE_ROW and w_ada.shape[1] == 3 * d
    return pl.pallas_call(
        _ada_kernel,
        grid=(3,),
        in_specs=[pl.BlockSpec(c_prompt.shape, lambda j: (0, 0)),
                  pl.BlockSpec(c_sample.shape, lambda j: (0, 0)),
                  pl.BlockSpec((d, d), lambda j: (0, j)),
                  pl.BlockSpec((1, d), lambda j: (0, j))],
        out_specs=pl.BlockSpec((1, ADA_ROWS, d), lambda j: (j, 0, 0)),
        out_shape=jax.ShapeDtypeStruct((3, ADA_ROWS, d), F32),
        compiler_params=pltpu.CompilerParams(dimension_semantics=("arbitrary",), vmem_limit_bytes=VMEM_LIMIT),
        name="ada",
    )(c_prompt, c_sample, w_ada, b_ada)


def _proj_kernel(bf_ref, x_ref, ada_ref, ng_ref, w_ref, wp_ref, ps_ref, h0_ref,
                 act_ref, k32_ref, v32_ref, lf_ref, cq_ref, ck_ref,
                 tot_ref, qmx_ref, kg_ref, ho_ref,
                 e_ref, t2_ref, t4_ref, t8_ref, kmx_ref, *, bm, sb, segs, ada_row, start_pos, n_heads, kv_head_major):
    a_w = n_heads * HEAD_DIM
    q_ref, kb_ref, vb_ref, sa_ref, zp_ref = _act_views(act_ref, a_w)
    seg_rows = bm // segs
    step = pl.program_id(0)
    pw = len(POOL_WINDOWS) * LANES
    sc = LOG2E / (HEAD_DIM ** 0.5)
    o_pool = 4 * a_w + n_heads

    def w_rows(lo, hi):
        return w_ref[lo:hi, :].astype(BF16)

    def normed(lo, hi):
        x = x_ref[lo:hi, :]
        xn = x * lax.rsqrt(jnp.mean(x * x, axis=-1, keepdims=True) + EPS)
        parts = []
        for g in range(lo // seg_rows, (hi - 1) // seg_rows + 1):
            r0, r1 = max(lo, g * seg_rows) - lo, min(hi, (g + 1) * seg_rows) - lo
            parts.append(xn[r0:r1] * (ng_ref[...] * (1.0 + ada_ref[1, ada_row + g:ada_row + g + 1, :]))
                         + ada_ref[0, ada_row + g:ada_row + g + 1, :])
        return jnp.concatenate(parts, axis=0).astype(BF16)

    n_lead = 4 if segs == 1 else 1
    hs, pus = [], []
    for c in range(n_lead):
        hs.append(normed(c * bm // n_lead, (c + 1) * bm // n_lead))
        pus.append(_dot_nt(hs[-1], w_rows(o_pool, o_pool + pw)))
    pu = jnp.concatenate(pus, axis=0)
    h = jnp.concatenate(hs, axis=0)
    sel = (lax.broadcasted_iota(jnp.int32, (a_w, LANES), 0) // HEAD_DIM
           == lax.broadcasted_iota(jnp.int32, (a_w, LANES), 1)).astype(BF16)

    def store_kv(ref32, refb, p):
        refb[...] = p.astype(BF16)
        if kv_head_major:
            srcs = []
            for g in range(n_heads // PAIR):
                pg = p[:, g * LANES:(g + 1) * LANES]
                srcs += [pg, pltpu.roll(pg, HEAD_DIM, 1)]
            ref32[...] = _rows_to_tiles(srcs)[:, :, 0:HEAD_DIM]
        else:
            ref32[...] = p.T

    ext = HIST_PAD + seg_rows
    n = segs * ext

    if segs > 1:
        for g in range(segs):
            e_ref[g * ext:g * ext + HIST_PAD, :] = h0_ref[g]
    else:
        e_ref[0:HIST_PAD, :] = jnp.where(step == 0, h0_ref[0], e_ref[0:HIST_PAD, :])

    for g in range(segs):
        e_ref[g * ext + HIST_PAD:(g + 1) * ext, :] = pu[g * seg_rows:(g + 1) * seg_rows]
    t2_ref[8:n, :] = e_ref[8:n, :] + e_ref[7:n - 1, :]
    t4_ref[16:n, :] = t2_ref[16:n, LANES:] + t2_ref[14:n - 2, LANES:]
    t8_ref[24:n, :] = t4_ref[24:n, LANES:] + t4_ref[20:n - 4, LANES:]

    def seg_rows_of(ref, cols, back=0):
        return jnp.concatenate([ref[g * ext + HIST_PAD - back:(g + 1) * ext - back, cols] for g in range(segs)], axis=0)

    lane0, lane1 = slice(0, LANES), slice(LANES, 2 * LANES)
    sums = [seg_rows_of(t2_ref, lane0), seg_rows_of(t4_ref, lane0), seg_rows_of(t8_ref, lane0),
            seg_rows_of(t8_ref, lane1) + seg_rows_of(t8_ref, lane1, back=8)]
    row = lax.broadcasted_iota(jnp.int32, (bm, 1), 0)
    pos1 = start_pos + 1 + (step * bm + row if segs == 1 else row % seg_rows)
    pool_d = []
    inv_pos = 1.0 / pos1.astype(F32)
    for g, w in enumerate(POOL_WINDOWS):
        rc = jnp.maximum(inv_pos, 1.0 / w)
        pool_d.append((sums[g] * rc - pu[:, g * LANES:(g + 1) * LANES]).astype(BF16))
    for g in range(segs):
        ho_ref[g] = e_ref[(g + 1) * ext - 16:(g + 1) * ext, :]
    if segs == 1:
        e_ref[0:HIST_PAD, :] = e_ref[bm:n, :]

    pk = _dot_nt(h, w_rows(a_w, 2 * a_w))
    store_kv(k32_ref, kb_ref, pk)
    nk2 = _dot((pk * pk).astype(BF16), sel)
    spg = _silu(_dot_nt(h, w_rows(o_pool + pw, o_pool + 2 * pw)))

    head = lax.broadcasted_iota(jnp.int32, (n_heads, 1), 0)
    bias = jnp.zeros((n_heads, 1), F32)
    for hd in range(n_heads):
        bias = jnp.where(head == hd, bf_ref[hd], bias)
    wf = jnp.concatenate([w_ref[4 * a_w:o_pool, :], jnp.zeros((LANES - n_heads, w_ref.shape[1]), F32)], axis=0)
    z = _dot_nt(h, wf.astype(BF16))

    sa_ref[...] = _silu(_dot_nt(h, w_rows(3 * a_w, 4 * a_w))).astype(BF16)

    lfts = []
    for s in range(bm // sb):
        zt = _rows_to_lanes(z[s * sb:(s + 1) * sb], n_heads) + bias
        lft = jnp.minimum(zt, 0.0) - jnp.log1p(jnp.exp(-jnp.abs(zt)))
        off = (s * sb) % seg_rows
        lf_ref[(s * sb) // seg_rows, :, off:off + sb] = lft
        lfts.append(lft if sb == LANES else jnp.concatenate([lft, jnp.zeros((n_heads, LANES - sb), F32)], axis=1))
    c_all = _lane_cumsum(jnp.concatenate(lfts, axis=1) * LOG2E, sb)
    tots, kmx = [], []
    for s in range(bm // sb):
        rows = slice(s * sb, (s + 1) * sb)
        c = c_all[:, s * LANES:(s + 1) * LANES]
        ck_ref[s] = c[:, 0:sb]
        cb = _lanes_to_rows(c)[0:sb]
        cq_ref[rows, :] = cb[:, 0:n_heads]
        tots.append(cb[sb - 1:sb, :])
        kmx.append(jnp.max(nk2[rows], axis=0, keepdims=True))
    tot_ref[...] = jnp.concatenate(tots, axis=0)[:, 0:n_heads]
    kmax = kmx[0]
    for t in kmx[1:]:
        kmax = jnp.maximum(kmax, t)
    kmax = jnp.sqrt(kmax) * NORM_SLACK

    store_kv(v32_ref, vb_ref, _dot_nt(h, w_rows(2 * a_w, 3 * a_w)))

    qs = _dot_nt(h, w_rows(0, a_w)) * sc
    q_ref[...] = qs.astype(BF16)
    nq2 = _dot((qs * qs).astype(BF16), sel)
    qmx2 = jnp.concatenate([jnp.max(nq2[s * sb:(s + 1) * sb], axis=0, keepdims=True) for s in range(bm // sb)], axis=0)
    qmx_ref[...] = (jnp.sqrt(qmx2) * NORM_SLACK)[:, 0:n_heads]

    zero_w = jnp.zeros((LANES, LANES), BF16)
    for g in range(0, len(POOL_WINDOWS), 2):
        cols = slice(g * LANES, (g + 2) * LANES)
        w2 = jnp.concatenate([jnp.concatenate([wp_ref[g].astype(BF16), zero_w], axis=1),
                              jnp.concatenate([zero_w, wp_ref[g + 1].astype(BF16)], axis=1)], axis=0)
        y = _dot(jnp.concatenate([pool_d[g], pool_d[g + 1]], axis=1), w2) * ps_ref[:, cols]
        zp_ref[:, cols] = (y * spg[:, cols]).astype(BF16)

    kmax = jnp.where(step == 0, kmax, jnp.maximum(kmx_ref[...], kmax))
    kmx_ref[...] = kmax
    kg_ref[...] = kmax


def _project(x2, ada, norm_g, wit, b_f, wp, ps, hist0, *, bm, sb, segs, ada_row, start_pos, n_heads,
             kv_head_major):
    rows, d = x2.shape
    a_w = n_heads * HEAD_DIM
    pw = len(POOL_WINDOWS) * LANES
    n_steps = rows // bm
    assert segs == 1 or n_steps == 1
    n_streams = segs
    seg_rows = bm // segs
    nsb = bm // sb
    assert nsb == 8 and seg_rows % sb == 0
    row_blk = lambda w: pl.BlockSpec((bm, w), lambda i, *_: (i, 0))
    per_stream = lambda r, w: pl.BlockSpec((segs, r, w), lambda i, *_: (0, 0, 0))
    kern = functools.partial(_proj_kernel, bm=bm, sb=sb, segs=segs, ada_row=ada_row, start_pos=start_pos,
                             n_heads=n_heads, kv_head_major=kv_head_major)
    if kv_head_major:
        kv_shape = (rows, n_heads, HEAD_DIM)
        kv_blk = pl.BlockSpec((bm, n_heads, HEAD_DIM), lambda i, *_: (i, 0, 0))
    else:
        kv_shape = (a_w, rows)
        kv_blk = pl.BlockSpec((a_w, bm), lambda i, *_: (0, i))
    assert pw == a_w
    out_shape = (
        jax.ShapeDtypeStruct((rows, ACT_PARTS * a_w), BF16),
        jax.ShapeDtypeStruct(kv_shape, F32),
        jax.ShapeDtypeStruct(kv_shape, F32),
        jax.ShapeDtypeStruct((n_streams, n_heads, rows // n_streams), F32),
        jax.ShapeDtypeStruct((rows, n_heads), F32),
        jax.ShapeDtypeStruct((rows // sb, n_heads, sb), F32),
        jax.ShapeDtypeStruct((rows // sb, n_heads), F32),
        jax.ShapeDtypeStruct((rows // sb, n_heads), F32),
        jax.ShapeDtypeStruct((1, LANES), F32),
        jax.ShapeDtypeStruct((n_streams, 16, pw), F32),
    )
    out_specs = (
        row_blk(ACT_PARTS * a_w), kv_blk, kv_blk,
        pl.BlockSpec((segs, n_heads, seg_rows), lambda i, *_: (0, 0, i)),
        row_blk(n_heads),
        pl.BlockSpec((nsb, n_heads, sb), lambda i, *_: (i, 0, 0)),
        pl.BlockSpec((nsb, n_heads), lambda i, *_: (i, 0)),
        pl.BlockSpec((nsb, n_heads), lambda i, *_: (i, 0)),
        pl.BlockSpec((1, LANES), lambda i, *_: (0, 0)),
        per_stream(16, pw),
    )
    in_specs = [
        row_blk(d),
        _resident(ada.shape),
        _resident((1, d)),
        _resident(wit.shape), _resident(wp.shape), _resident(ps.shape),
        per_stream(HIST_PAD, pw),
    ]
    return pl.pallas_call(
        kern,
        grid_spec=pltpu.PrefetchScalarGridSpec(
            num_scalar_prefetch=1,
            grid=(n_steps,),
            in_specs=in_specs,
            out_specs=out_specs,
            scratch_shapes=[pltpu.VMEM((bm + segs * HIST_PAD, pw), F32),
                            pltpu.VMEM((bm + segs * HIST_PAD, pw), F32),
                            pltpu.VMEM((bm + segs * HIST_PAD, pw - LANES), F32),
                            pltpu.VMEM((bm + segs * HIST_PAD, pw - 2 * LANES), F32),
                            pltpu.VMEM((1, LANES), F32)]),
        out_shape=out_shape,
        compiler_params=pltpu.CompilerParams(dimension_semantics=("arbitrary",), vmem_limit_bytes=VMEM_LIMIT),
        name="proj",
    )(b_f, x2, ada, norm_g, wit, wp, ps, hist0)


def _merge_norm(x, gate, za, zp, wo_ref, fg, a_w):
    dy = _dot(za, wo_ref[0:a_w, :].astype(BF16)) + _dot(zp, wo_ref[a_w:, :].astype(BF16))
    out = x + gate * dy
    ms = jnp.mean(out * out, axis=-1, keepdims=True)
    return out * lax.rsqrt(ms + EPS) * fg


def _attn_kernel(tot_ref, qmx_ref, kg_ref,
                 act_ref, cq_ref, ck_ref, x_ref, ada_ref, wo_ref, fg_ref,
                 y_ref,
                 k_ref, v_ref, z_ref, m_ref, l_ref, acc_ref, straight_ref, *, tm, n_heads):
    a_w = n_heads * HEAD_DIM
    q_ref, knew_ref, vnew_ref, sa_ref, zp_ref = _act_views(act_ref, a_w)
    n_pairs = n_heads // PAIR
    nsub = tm // ATT_BLK
    w_near = NEAR_BLOCKS
    cw = w_near * ATT_BLK
    step = pl.program_id(0)

    def keep_step_rows():
        k_ref[pl.ds(pl.multiple_of(step * tm, tm), tm), :] = knew_ref[...]
        v_ref[pl.ds(pl.multiple_of(step * tm, tm), tm), :] = vnew_ref[...]

    lo_q = lax.broadcasted_iota(jnp.int32, (ATT_BLK, LANES), 1) < HEAD_DIM
    lo_k = lax.broadcasted_iota(jnp.int32, (cw, LANES), 1) < HEAD_DIM
    col = lax.broadcasted_iota(jnp.int32, (ATT_BLK, cw), 1)
    tri = (lax.broadcasted_iota(jnp.int32, (ATT_BLK, ATT_BLK), 1)
           <= lax.broadcasted_iota(jnp.int32, (ATT_BLK, ATT_BLK), 0))
    tri_bias = jnp.where(tri, 0.0, NEG).astype(F32)
    zeros_k = jnp.zeros((cw, LANES), BF16)
    ind_lo = jnp.where(lo_k, 1.0, 0.0).astype(BF16)
    ind_hi = jnp.where(lo_k, 0.0, 1.0).astype(BF16)

    def tot_at(b, hd):
        return jnp.where(b >= 0, tot_ref[jnp.maximum(b, 0), hd], 0.0)

    def rows_of(ref, blocks, g):
        return jnp.concatenate(
            [ref[pl.ds(pl.multiple_of(b * ATT_BLK, ATT_BLK), ATT_BLK), g * LANES:(g + 1) * LANES] for b in blocks],
            axis=0)

    def step_rows_of(ref, new_ref, sub, g):
        parts = []
        for p in range(w_near):
            rel = sub - (w_near - 1) + p
            if rel >= 0:
                parts.append(new_ref[rel * ATT_BLK:(rel + 1) * ATT_BLK, g * LANES:(g + 1) * LANES])
            else:
                start = pl.multiple_of((step * nsub + rel) * ATT_BLK, ATT_BLK)
                parts.append(ref[pl.ds(start, ATT_BLK), g * LANES:(g + 1) * LANES])
        return jnp.concatenate(parts, axis=0)

    def pair_scores(r0, kc, g):
        keys = jnp.concatenate([jnp.where(lo_k, kc, zeros_k), jnp.where(lo_k, zeros_k, kc)], axis=0)
        return _dot_nt(q_ref[pl.ds(r0, ATT_BLK), g * LANES:(g + 1) * LANES], keys)

    def pair_values(p_pair, vc):
        vals = jnp.concatenate([jnp.concatenate([jnp.where(lo_k, vc, zeros_k), ind_lo], axis=1),
                                jnp.concatenate([jnp.where(lo_k, zeros_k, vc), ind_hi], axis=1)], axis=0)
        return _dot(p_pair, vals)

    def decay_row(blocks, offs, hd):
        return jnp.concatenate([offs[p] - ck_ref[blocks[p], hd:hd + 1, :] for p in range(w_near)], axis=1)

    def near_weights(s_pair, rows, g, near_c, near_offs, m_bound):
        bounded = m_bound is not None
        ps, ms = [], []
        for e in range(PAIR):
            hd = PAIR * g + e
            cqh = cq_ref[rows, hd:hd + 1]
            if bounded:
                m = jnp.full((ATT_BLK, 1), m_bound[hd], F32)
                cqh = cqh - m_bound[hd]
            dec = decay_row(near_c, near_offs[hd], hd)
            pieces = []
            for p in range(w_near):
                lanes = slice(e * cw + p * ATT_BLK, e * cw + (p + 1) * ATT_BLK)
                sp = s_pair[:, lanes] + cqh + dec[:, p * ATT_BLK:(p + 1) * ATT_BLK]
                pieces.append(sp + tri_bias if p == w_near - 1 else sp)
            s = jnp.concatenate(pieces, axis=1)
            if not bounded:
                m = jnp.max(s, axis=1, keepdims=True)
                s = s - m
            ms.append(m)
            ps.append(jnp.exp2(s).astype(BF16))
        return jnp.concatenate(ps, axis=1), ms

    def gated_output(rows, g, acc, l):
        cols = slice(g * LANES, (g + 1) * LANES)
        z_ref[rows, cols] = ((acc / l) * sa_ref[rows, cols].astype(F32)).astype(BF16)

    def merge(chunks=1):
        rows_c = tm // chunks
        for c in range(chunks):
            rows = slice(c * rows_c, (c + 1) * rows_c)
            y_ref[rows, :] = _merge_norm(x_ref[rows, :], ada_ref[2, ADA_PROMPT_ROW:ADA_PROMPT_ROW + 1, :],
                                         z_ref[rows, :], zp_ref[rows, :], wo_ref, fg_ref[...], a_w)

    def check_next_step():
        ok = jnp.bool_(True)
        last = qmx_ref.shape[0] - 1
        for sub in range(nsub):
            i = jnp.minimum((step + 1) * nsub + sub, last)
            for hd in range(n_heads):
                qk = 2.0 * qmx_ref[i, hd] * kg_ref[0, hd]
                back = qk
                for dd in range(1, w_near):
                    back = back + tot_ref[i - dd, hd]
                ok = jnp.logical_and(ok, jnp.logical_and(qk <= EXP2_SAFE_SPAN, back < -EXP2_UNDERFLOW))
        straight_ref[0] = ok.astype(jnp.int32)

    def straight_step():
        units = [(sub, g) for sub in range(nsub) for g in range(n_pairs)]
        near_of, offs_of, bound_of = [], [], []
        for sub in range(nsub):
            i = step * nsub + sub
            near_of.append([i - (w_near - 1) + p for p in range(w_near)])
            bound_of.append([qmx_ref[i, hd] * kg_ref[0, hd] for hd in range(n_heads)])
            offs = []
            for hd in range(n_heads):
                o, per_piece = jnp.float32(0.0), [jnp.float32(0.0)]
                for dd in range(1, w_near):
                    o = o + tot_ref[i - dd, hd]
                    per_piece.append(o)
                offs.append(per_piece[::-1])
            offs_of.append(offs)
        s_next = pair_scores(0, step_rows_of(k_ref, knew_ref, 0, 0), 0)
        for u, (sub, g) in enumerate(units):
            s_pair = s_next
            if u + 1 < len(units):
                sub1, g1 = units[u + 1]
                s_next = pair_scores(sub1 * ATT_BLK, step_rows_of(k_ref, knew_ref, sub1, g1), g1)
            rows = pl.ds(sub * ATT_BLK, ATT_BLK)
            p_pair, _ = near_weights(s_pair, rows, g, near_of[sub], offs_of[sub], bound_of[sub])
            r = pair_values(p_pair, step_rows_of(v_ref, vnew_ref, sub, g))
            gated_output(rows, g, r[:, 0:LANES], r[:, LANES:])
        check_next_step()
        keep_step_rows()
        merge(chunks=2)

    def sub_body(sub, carry):
        i = step * nsub + sub
        r0 = pl.multiple_of(sub * ATT_BLK, ATT_BLK)

        rows = pl.ds(r0, ATT_BLK)
        qk = [2.0 * qmx_ref[i, hd] * kg_ref[0, hd] for hd in range(n_heads)]

        near = [i - (w_near - 1) + p for p in range(w_near)]
        near_c = [jnp.maximum(b, 0) for b in near]
        near_offs, offs_far = [], []
        for hd in range(n_heads):
            back = [tot_at(i - dd, hd) for dd in range(1, w_near)]
            offs = []
            for p in range(w_near):
                o = jnp.float32(0.0)
                for dd in range(1, w_near - p):
                    o = o + back[dd - 1]
                offs.append(jnp.where(near[p] >= 0, o, NEG))
            near_offs.append(offs)
            o = jnp.float32(0.0)
            for t in back:
                o = o + t
            offs_far.append(o)

        def near_chunk(bounded):
            def fn():
                s_next = pair_scores(r0, rows_of(k_ref, near_c, 0), 0)
                for g in range(n_pairs):
                    s_pair = s_next
                    if g + 1 < n_pairs:
                        s_next = pair_scores(r0, rows_of(k_ref, near_c, g + 1), g + 1)
                    p_pair, ms = near_weights(s_pair, rows, g, near_c, near_offs,
                                              [0.5 * b for b in qk] if bounded else None)
                    for e in range(PAIR):
                        m_ref[PAIR * g + e] = ms[e]
                    r = pair_values(p_pair, rows_of(v_ref, near_c, g))
                    acc_ref[g] = r[:, 0:LANES]
                    l_ref[g] = r[:, LANES:]
                    gated_output(rows, g, r[:, 0:LANES], r[:, LANES:])
            return fn

        bounded_ok = qk[0] <= EXP2_SAFE_SPAN
        for hd in range(1, n_heads):
            bounded_ok = jnp.logical_and(bounded_ok, qk[hd] <= EXP2_SAFE_SPAN)
        pl.when(bounded_ok)(near_chunk(True))
        pl.when(jnp.logical_not(bounded_ok))(near_chunk(False))

        def far_cond(c):
            top = i - c[0] * w_near
            need = qk[0] + c[1] >= -EXP2_UNDERFLOW
            for hd in range(1, n_heads):
                need = jnp.logical_or(need, qk[hd] + c[1 + hd] >= -EXP2_UNDERFLOW)
            return jnp.logical_and(top >= 0, need)

        def far_body(c):
            top = i - c[0] * w_near
            jc = jnp.maximum(top - (w_near - 1), 0)
            blocks = [jc + p for p in range(w_near)]
            keepc = col < (top + 1 - jc) * ATT_BLK
            new = [c[0] + 1]
            for g in range(n_pairs):
                s_pair = pair_scores(r0, rows_of(k_ref, blocks, g), g)
                ps, alphas = [], []
                for e in range(PAIR):
                    hd = PAIR * g + e
                    tt = [tot_at(top - b, hd) for b in range(w_near)]
                    offs = []
                    for p in range(w_near):
                        behind = top - (jc + p)
                        o = c[1 + hd]
                        for b in range(w_near):
                            o = o + jnp.where(behind >= b, tt[b], 0.0)
                        offs.append(o)
                    s = s_pair[:, e * cw:(e + 1) * cw] + cq_ref[rows, hd:hd + 1] + decay_row(blocks, offs, hd)
                    s = jnp.where(keepc, s, NEG)
                    m_old = m_ref[hd]
                    m_new = jnp.maximum(m_old, jnp.max(s, axis=1, keepdims=True))
                    m_ref[hd] = m_new
                    alphas.append(jnp.broadcast_to(jnp.exp2(m_old - m_new), (ATT_BLK, LANES)))
                    ps.append(jnp.exp2(s - m_new).astype(BF16))
                    o = c[1 + hd]
                    for t in tt:
                        o = o + t
                    new.append(o)
                alpha = jnp.where(lo_q, alphas[0], alphas[1])
                r = pair_values(jnp.concatenate(ps, axis=1), rows_of(v_ref, blocks, g))
                acc_ref[g] = alpha * acc_ref[g] + r[:, 0:LANES]
                l_ref[g] = alpha * l_ref[g] + r[:, LANES:]
            return tuple(new)

        far = lax.while_loop(far_cond, far_body, (jnp.int32(1),) + tuple(offs_far))

        @pl.when(far[0] > 1)
        def _():
            for g in range(n_pairs):
                gated_output(rows, g, acc_ref[g], l_ref[g])
        return carry

    @pl.when(step == 0)
    def _():
        straight_ref[0] = 0

    straight = straight_ref[0] == 1
    pl.when(straight)(straight_step)

    @pl.when(jnp.logical_not(straight))
    def _():
        keep_step_rows()
        check_next_step()
        lax.fori_loop(0, nsub, sub_body, 0)
        merge()


def _prompt_attention(tot, qmx, kg, act, cq, ck, x2, ada, wo, fg, *, tm, n_heads):
    rows, d = x2.shape
    a_w = n_heads * HEAD_DIM
    n_pairs = n_heads // PAIR
    assert rows % tm == 0 and tm % ATT_BLK == 0 and tm // ATT_BLK >= NEAR_BLOCKS - 1 and n_heads % PAIR == 0
    row_blk = lambda w: pl.BlockSpec((tm, w), lambda i, *_: (i, 0))
    grid_spec = pltpu.PrefetchScalarGridSpec(
        num_scalar_prefetch=3,
        grid=(rows // tm,),
        in_specs=[row_blk(ACT_PARTS * a_w), row_blk(n_heads), _resident(ck.shape), row_blk(d),
                  _resident(ada.shape), _resident(wo.shape), _resident((1, d))],
        out_specs=row_blk(d),
        scratch_shapes=[pltpu.VMEM((rows, a_w), BF16),
                        pltpu.VMEM((rows, a_w), BF16),
                        pltpu.VMEM((tm, a_w), BF16),
                        pltpu.VMEM((n_heads, ATT_BLK, 1), F32),
                        pltpu.VMEM((n_pairs, ATT_BLK, LANES), F32),
                        pltpu.VMEM((n_pairs, ATT_BLK, LANES), F32),
                        pltpu.SMEM((1,), jnp.int32)],
    )
    return pl.pallas_call(
        functools.partial(_attn_kernel, tm=tm, n_heads=n_heads),
        grid_spec=grid_spec,
        out_shape=jax.ShapeDtypeStruct((rows, d), F32),
        compiler_params=pltpu.CompilerParams(dimension_semantics=("arbitrary",), vmem_limit_bytes=VMEM_LIMIT),
        name="attn",
    )(tot, qmx, kg, act, cq, ck, x2, ada, wo, fg)


CACHE_SPLIT = 2


def _sattn_kernel(act_ref, cq_ref, ckn_ref, kc0_ref, kc1_ref, vc0_ref, vc1_ref, lfc_ref,
                  x_ref, ada_ref, wo_ref, fg_ref, y_ref, *, n_heads):
    a_w = n_heads * HEAD_DIM
    q_ref, kn_ref, vn_ref, sa_ref, zp_ref = _act_views(act_ref, a_w)
    ln = act_ref.shape[0]
    past = kc0_ref.shape[2]
    kc_refs, vc_refs = (kc0_ref, kc1_ref), (vc0_ref, vc1_ref)
    pairs_per_part = n_heads // PAIR // CACHE_SPLIT

    def cached(refs, g):
        lo = (g % pairs_per_part) * LANES
        return refs[g // pairs_per_part][0, lo:lo + LANES, :].astype(BF16)

    nb = past // LANES
    lane = lax.broadcasted_iota(jnp.int32, (ln, LANES), 1)
    half = [lane < HEAD_DIM, lane >= HEAD_DIM]
    causal = lax.broadcasted_iota(jnp.int32, (ln, ln), 1) <= lax.broadcasted_iota(jnp.int32, (ln, ln), 0)
    n_pairs = n_heads // PAIR

    def pair_scores(g):
        cols = slice(g * LANES, (g + 1) * LANES)
        q2 = q_ref[:, cols]
        qst = jnp.concatenate([jnp.where(half[e], q2, jnp.zeros_like(q2)) for e in range(PAIR)], axis=0)
        return _dot(qst, cached(kc_refs, g)), _dot_nt(qst, kn_ref[:, cols])

    outs = []
    s_next = pair_scores(0)

    lfc = lfc_ref[0] * LOG2E
    cs = _lane_cumsum(lfc, LANES)
    after = jnp.zeros((n_heads, 1), F32)
    suffix = [None] * nb
    for b in reversed(range(nb)):
        cb = cs[:, b * LANES:(b + 1) * LANES]
        tot = cb[:, LANES - 1:LANES]
        suffix[b] = (tot - cb) + after
        after = after + tot
    dec_c = jnp.concatenate(suffix, axis=1)

    for g in range(n_pairs):
        cols = slice(g * LANES, (g + 1) * LANES)
        sc_st, sn_st = s_next
        if g + 1 < n_pairs:
            s_next = pair_scores(g + 1)
        vct = cached(vc_refs, g)
        vn = vn_ref[:, cols]
        pc, pn, ls = [], [], []
        for e in range(PAIR):
            hd = PAIR * g + e
            rows = slice(e * ln, (e + 1) * ln)
            cqh = cq_ref[:, hd:hd + 1]
            s_c = sc_st[rows] + cqh + dec_c[hd:hd + 1, :]
            s_n = jnp.where(causal, sn_st[rows] + cqh - ckn_ref[0, hd:hd + 1, :], NEG)
            m = jnp.maximum(jnp.max(s_c, axis=1, keepdims=True), jnp.max(s_n, axis=1, keepdims=True))
            p_c = jnp.exp2(s_c - m)
            p_n = jnp.exp2(s_n - m)
            ls.append(jnp.sum(p_c, axis=1, keepdims=True) + jnp.sum(p_n, axis=1, keepdims=True))
            pc.append(p_c.astype(BF16))
            pn.append(p_n.astype(BF16))
        acc = _dot_nt(jnp.concatenate(pc, axis=0), vct) + _dot(jnp.concatenate(pn, axis=0), vn)
        o = jnp.where(half[0], acc[0:ln] / ls[0], acc[ln:2 * ln] / ls[1])
        outs.append((o * sa_ref[:, cols].astype(F32)).astype(BF16))
    za = jnp.concatenate(outs, axis=1)
    y_ref[...] = _merge_norm(x_ref[...], ada_ref[2, pl.ds(ADA_SAMPLE_ROW + pl.program_id(0), 1), :], za, zp_ref[...], wo_ref, fg_ref[...], a_w)


def _sample_attention(act, cq, ckn, cache_k, cache_v, lfc, x2, ada, wo, fg, *, ln, n_heads):
    rows, d = x2.shape
    nbatch = rows // ln
    a_w = n_heads * HEAD_DIM
    past = cache_k.shape[2]
    row_blk = lambda w: pl.BlockSpec((ln, w), lambda b: (b, 0))
    per_b = lambda s: pl.BlockSpec((1,) + s, lambda b: (b, 0, 0))
    cache_part = lambda c: pl.BlockSpec((1, a_w // CACHE_SPLIT, past), lambda b: (b, c, 0))
    assert CACHE_SPLIT == 2 and (n_heads // PAIR) % CACHE_SPLIT == 0
    return pl.pallas_call(
        functools.partial(_sattn_kernel, n_heads=n_heads),
        grid=(nbatch,),
        in_specs=[row_blk(ACT_PARTS * a_w), row_blk(n_heads), per_b((n_heads, ln)),
                  cache_part(0), cache_part(1), cache_part(0), cache_part(1), per_b((n_heads, past)),
                  row_blk(d), _resident(ada.shape), _resident(wo.shape), _resident((1, d))],
        out_specs=row_blk(d),
        out_shape=jax.ShapeDtypeStruct((rows, d), F32),
        compiler_params=pltpu.CompilerParams(dimension_semantics=("arbitrary",), vmem_limit_bytes=VMEM_LIMIT),
        name="sattn",
    )(act, cq, ckn, cache_k, cache_k, cache_v, cache_v, lfc, x2, ada, wo, fg)


def kernel(x_prompt, x_sample, c_prompt, c_sample, cache_k, cache_v, cache_logf, state_pool, norm_g, w_ada, b_ada,
           w_in, b_f, w_pool, pool_scale, w_out, final_g):
    depth = norm_g.shape[0]
    assert depth == 1
    bp, seq, d = x_prompt.shape
    bs, ln, _ = x_sample.shape
    assert bp == 1
    n_heads = cache_k.shape[3]
    past = cache_k.shape[2]
    a_w = n_heads * HEAD_DIM
    pw = state_pool.shape[3]
    assert pw == len(POOL_WINDOWS) * LANES and cache_k.shape[4] == HEAD_DIM and n_heads <= 8

    ada = _ada_terms(c_prompt, c_sample, w_ada[0], b_ada)

    wit = w_in[0].T
    wp = w_pool[0]
    ps = pool_scale[0][None, :]
    wo = w_out[0]
    ng = norm_g[0][None, :]
    fg = final_g[None, :]

    assert seq % PROJ_ROWS == 0 and seq % ATTN_ROWS == 0
    xp2 = x_prompt.reshape(seq, d)
    hist_p = jnp.zeros((1, HIST_PAD, pw), F32)
    (act_p, k_p, v_p, lf_p, cq_p, ck_p, tot, qmx, kg, ho_p) = _project(
        xp2, ada, ng, wit, b_f[0], wp, ps, hist_p,
        bm=PROJ_ROWS, sb=ATT_BLK, segs=1, ada_row=ADA_PROMPT_ROW, start_pos=0, n_heads=n_heads, kv_head_major=False)
    y_p = _prompt_attention(tot, qmx, kg, act_p, cq_p, ck_p, xp2, ada, wo, fg, tm=ATTN_ROWS, n_heads=n_heads)

    xs2 = x_sample.reshape(bs * ln, d)
    hist_s = jnp.pad(state_pool[0], ((0, 0), (HIST_PAD - POOL_HIST, 0), (0, 0)))
    (act_s, k_s, v_s, lf_s, cq_s, ck_s, _, _, _, ho_s) = _project(
        xs2, ada, ng, wit, b_f[0], wp, ps, hist_s,
        bm=bs * ln, sb=ln, segs=bs, ada_row=ADA_SAMPLE_ROW, start_pos=past, n_heads=n_heads, kv_head_major=True)
    lfc = jnp.swapaxes(cache_logf[0], 1, 2)
    ckt = jnp.transpose(cache_k[0], (0, 2, 3, 1)).reshape(bs, a_w, past)
    cvt = jnp.transpose(cache_v[0], (0, 2, 3, 1)).reshape(bs, a_w, past)
    y_s = _sample_attention(act_s, cq_s, ck_s, ckt, cvt, lfc, xs2, ada, wo, fg, ln=ln, n_heads=n_heads)

    hd = (n_heads, HEAD_DIM)
    seq_minor = lambda t: jnp.transpose(t.reshape(hd + (bp, seq)), (2, 3, 0, 1))[None]
    return (y_p.reshape(bp, seq, d), y_s.reshape(bs, ln, d),
            seq_minor(k_p), seq_minor(v_p), jnp.swapaxes(lf_p, 1, 2)[None],
            ho_p[:, 16 - POOL_HIST:, :][None],
            k_s.reshape((1, bs, ln) + hd), v_s.reshape((1, bs, ln) + hd), jnp.swapaxes(lf_s, 1, 2)[None],
            ho_s[:, 16 - POOL_HIST:, :][None])
```

```python
import functools

import jax
import jax.numpy as jnp
from jax import lax
from jax.experimental import pallas as pl
from jax.experimental.pallas import tpu as pltpu

HEAD_DIM = 64
POOL_WINDOWS = (2, 4, 8, 16)
EPS = 1e-6

LANES = 128
PAIR = LANES // HEAD_DIM
ATT_BLK = 128
NEAR_BLOCKS = 3
LOG2E = 1.4426950408889634
EXP2_UNDERFLOW = 151.0
EXP2_SAFE_SPAN = 100.0
NORM_SLACK = 1.01
HIST_PAD = 32
POOL_HIST = max(POOL_WINDOWS) - 1
NEG = -1e30
VMEM_LIMIT = 60 * 1024 * 1024
PROJ_ROWS = 8 * ATT_BLK
ATTN_ROWS = 4 * ATT_BLK

F32 = jnp.float32
BF16 = jnp.bfloat16


def _silu(x):
    return x * jax.nn.sigmoid(x)


def _dot(a, b):
    return jnp.dot(a, b, preferred_element_type=F32)


def _dot_nt(a, b):
    return lax.dot_general(a, b, (((1,), (1,)), ((), ())), preferred_element_type=F32)


def _lane_cumsum(x, n):
    lane = lax.broadcasted_iota(jnp.int32, x.shape, 1) % LANES
    shift = 1
    while shift < n:
        x = x + jnp.where(lane >= shift, pltpu.roll(x, shift, 1), 0.0)
        shift *= 2
    return x


def _rows_to_lanes(x, n):
    rows = x.shape[0]
    if rows < LANES:
        x = jnp.concatenate([x, jnp.zeros((LANES - rows, LANES), x.dtype)], axis=0)
    return x.T[0:n, 0:rows]


def _lanes_to_rows(x):
    n, rows = x.shape
    if rows < LANES:
        x = jnp.concatenate([x, jnp.zeros((n, LANES - rows), x.dtype)], axis=1)
    x = jnp.concatenate([x, jnp.zeros((LANES - n, LANES), x.dtype)], axis=0)
    return x.T[0:rows, :]


def _rows_to_tiles(srcs):
    n = len(srcs)
    rows = srcs[0].shape[0]
    srcs = list(srcs) + [jnp.zeros_like(srcs[0])] * (8 - n)
    v = [s.reshape(rows // 8, 8, LANES) for s in srcs]
    sub = lax.broadcasted_iota(jnp.int32, (rows // 8, 8, LANES), 1)
    for d in (4, 2, 1):
        low = (sub & d) == 0
        nxt = list(v)
        for s in range(8):
            if s & d == 0:
                nxt[s] = jnp.where(low, v[s], pltpu.roll(v[s + d], d, 1))
                nxt[s + d] = jnp.where(low, pltpu.roll(v[s], 8 - d, 1), v[s + d])
        v = nxt
    return jnp.stack(v, axis=1).reshape(rows, 8, LANES)[:, 0:n, :]


ACT_PARTS = 5


def _act_views(act_ref, width):
    return [act_ref.at[:, p * width:(p + 1) * width] for p in range(ACT_PARTS)]


def _resident(shape):
    return pl.BlockSpec(shape, lambda *_: (0,) * len(shape), pipeline_mode=pl.Buffered(1))


ADA_ROWS = 16
ADA_PROMPT_ROW, ADA_SAMPLE_ROW = 0, 8


def _ada_kernel(cp_ref, cs_ref, w_ref, b_ref, o_ref):
    ap = jnp.broadcast_to(_silu(cp_ref[...]), (ADA_SAMPLE_ROW, cp_ref.shape[1]))
    a = jnp.concatenate([ap, _silu(cs_ref[...])], axis=0).astype(BF16)
    o_ref[0] = _dot(a, w_ref[...].astype(BF16)) + b_ref[...]


def _ada_terms(c_prompt, c_sample, w_ada, b_ada):
    d = c_prompt.shape[1]
    assert c_prompt.shape[0] == 1 and c_sample.shape[0] == ADA_ROWS - ADA_SAMPLE_ROW and w_ada.shape[1] == 3 * d
    return pl.pallas_call(
        _ada_kernel,
        grid=(3,),
        in_specs=[pl.BlockSpec(c_prompt.shape, lambda j: (0, 0)),
                  pl.BlockSpec(c_sample.shape, lambda j: (0, 0)),
                  pl.BlockSpec((d, d), lambda j: (0, j)),
                  pl.BlockSpec((1, d), lambda j: (0, j))],
        out_specs=pl.BlockSpec((1, ADA_ROWS, d), lambda j: (j, 0, 0)),
        out_shape=jax.ShapeDtypeStruct((3, ADA_ROWS, d), F32),
        compiler_params=pltpu.CompilerParams(dimension_semantics=("arbitrary",), vmem_limit_bytes=VMEM_LIMIT),
        name="ada",
    )(c_prompt, c_sample, w_ada, b_ada)


def _proj_kernel(bf_ref, x_ref, ada_ref, ng_ref, w_ref, wp_ref, ps_ref, h0_ref,
                 act_ref, k32_ref, v32_ref, lf_ref, cq_ref, ck_ref,
                 tot_ref, qmx_ref, kg_ref, ho_ref,
                 e_ref, t2_ref, t4_ref, t8_ref, kmx_ref, h_ref, *, bm, sb, segs, ada_row, start_pos, n_heads, kv_head_major):
    a_w = n_heads * HEAD_DIM
    q_ref, kb_ref, vb_ref, sa_ref, zp_ref = _act_views(act_ref, a_w)
    seg_rows = bm // segs
    step = pl.program_id(0)
    pw = len(POOL_WINDOWS) * LANES
    sc = LOG2E / (HEAD_DIM ** 0.5)
    o_pool = 4 * a_w + n_heads

    def w_rows(lo, hi):
        return w_ref[lo:hi, :].astype(BF16)

    def normed(lo, hi):
        x = x_ref[lo:hi, :]
        xn = x * lax.rsqrt(jnp.mean(x * x, axis=-1, keepdims=True) + EPS)
        parts = []
        for g in range(lo // seg_rows, (hi - 1) // seg_rows + 1):
            r0, r1 = max(lo, g * seg_rows) - lo, min(hi, (g + 1) * seg_rows) - lo
            parts.append(xn[r0:r1] * (ng_ref[...] * (1.0 + ada_ref[1, ada_row + g:ada_row + g + 1, :]))
                         + ada_ref[0, ada_row + g:ada_row + g + 1, :])
        return jnp.concatenate(parts, axis=0).astype(BF16)

    n_lead = 4 if segs == 1 else 1
    pus = []
    for c in range(n_lead):
        rows = slice(c * bm // n_lead, (c + 1) * bm // n_lead)
        h_ref[rows, :] = normed(rows.start, rows.stop)
        pus.append(_dot_nt(h_ref[rows, :], w_rows(o_pool, o_pool + pw)))
    pu = jnp.concatenate(pus, axis=0)
    sel = (lax.broadcasted_iota(jnp.int32, (a_w, LANES), 0) // HEAD_DIM
           == lax.broadcasted_iota(jnp.int32, (a_w, LANES), 1)).astype(BF16)

    def store_kv(ref32, refb, p):
        refb[...] = p.astype(BF16)
        if kv_head_major:
            srcs = []
            for g in range(n_heads // PAIR):
                pg = p[:, g * LANES:(g + 1) * LANES]
                srcs += [pg, pltpu.roll(pg, HEAD_DIM, 1)]
            ref32[...] = _rows_to_tiles(srcs)[:, :, 0:HEAD_DIM]
        else:
            ref32[...] = p.T

    ext = HIST_PAD + seg_rows
    n = segs * ext

    if segs > 1:
        for g in range(segs):
            e_ref[g * ext:g * ext + HIST_PAD, :] = h0_ref[g]
    else:
        e_ref[0:HIST_PAD, :] = jnp.where(step == 0, h0_ref[0], e_ref[0:HIST_PAD, :])

    for g in range(segs):
        e_ref[g * ext + HIST_PAD:(g + 1) * ext, :] = pu[g * seg_rows:(g + 1) * seg_rows]
    t2_ref[8:n, :] = e_ref[8:n, :] + e_ref[7:n - 1, :]
    t4_ref[16:n, :] = t2_ref[16:n, LANES:] + t2_ref[14:n - 2, LANES:]
    t8_ref[24:n, :] = t4_ref[24:n, LANES:] + t4_ref[20:n - 4, LANES:]

    def seg_rows_of(ref, cols, back=0):
        return jnp.concatenate([ref[g * ext + HIST_PAD - back:(g + 1) * ext - back, cols] for g in range(segs)], axis=0)

    lane0, lane1 = slice(0, LANES), slice(LANES, 2 * LANES)
    sums = [seg_rows_of(t2_ref, lane0), seg_rows_of(t4_ref, lane0), seg_rows_of(t8_ref, lane0),
            seg_rows_of(t8_ref, lane1) + seg_rows_of(t8_ref, lane1, back=8)]
    row = lax.broadcasted_iota(jnp.int32, (bm, 1), 0)
    pos1 = start_pos + 1 + (step * bm + row if segs == 1 else row % seg_rows)
    pool_d = []
    inv_pos = 1.0 / pos1.astype(F32)
    for g, w in enumerate(POOL_WINDOWS):
        rc = jnp.maximum(inv_pos, 1.0 / w)
        pool_d.append((sums[g] * rc - pu[:, g * LANES:(g + 1) * LANES]).astype(BF16))
    for g in range(segs):
        ho_ref[g] = e_ref[(g + 1) * ext - 16:(g + 1) * ext, :]
    if segs == 1:
        e_ref[0:HIST_PAD, :] = e_ref[bm:n, :]

    pk = _dot_nt(h_ref[...], w_rows(a_w, 2 * a_w))
    store_kv(k32_ref, kb_ref, pk)
    nk2 = _dot((pk * pk).astype(BF16), sel)
    spg = _silu(_dot_nt(h_ref[...], w_rows(o_pool + pw, o_pool + 2 * pw)))

    head = lax.broadcasted_iota(jnp.int32, (n_heads, 1), 0)
    bias = jnp.zeros((n_heads, 1), F32)
    for hd in range(n_heads):
        bias = jnp.where(head == hd, bf_ref[hd], bias)
    wf = jnp.concatenate([w_ref[4 * a_w:o_pool, :], jnp.zeros((LANES - n_heads, w_ref.shape[1]), F32)], axis=0)
    z = _dot_nt(h_ref[...], wf.astype(BF16))

    sa_ref[...] = _silu(_dot_nt(h_ref[...], w_rows(3 * a_w, 4 * a_w))).astype(BF16)

    lfts = []
    for s in range(bm // sb):
        zt = _rows_to_lanes(z[s * sb:(s + 1) * sb], n_heads) + bias
        lft = jnp.minimum(zt, 0.0) - jnp.log1p(jnp.exp(-jnp.abs(zt)))
        off = (s * sb) % seg_rows
        lf_ref[(s * sb) // seg_rows, :, off:off + sb] = lft
        lfts.append(lft if sb == LANES else jnp.concatenate([lft, jnp.zeros((n_heads, LANES - sb), F32)], axis=1))
    c_all = _lane_cumsum(jnp.concatenate(lfts, axis=1) * LOG2E, sb)
    tots, kmx = [], []
    for s in range(bm // sb):
        rows = slice(s * sb, (s + 1) * sb)
        c = c_all[:, s * LANES:(s + 1) * LANES]
        ck_ref[s] = c[:, 0:sb]
        cb = _lanes_to_rows(c)[0:sb]
        cq_ref[rows, :] = cb[:, 0:n_heads]
        tots.append(cb[sb - 1:sb, :])
        kmx.append(jnp.max(nk2[rows], axis=0, keepdims=True))
    tot_ref[...] = jnp.concatenate(tots, axis=0)[:, 0:n_heads]
    kmax = kmx[0]
    for t in kmx[1:]:
        kmax = jnp.maximum(kmax, t)
    kmax = jnp.sqrt(kmax) * NORM_SLACK

    store_kv(v32_ref, vb_ref, _dot_nt(h_ref[...], w_rows(2 * a_w, 3 * a_w)))

    qs = _dot_nt(h_ref[...], w_rows(0, a_w)) * sc
    q_ref[...] = qs.astype(BF16)
    nq2 = _dot((qs * qs).astype(BF16), sel)
    qmx2 = jnp.concatenate([jnp.max(nq2[s * sb:(s + 1) * sb], axis=0, keepdims=True) for s in range(bm // sb)], axis=0)
    qmx_ref[...] = (jnp.sqrt(qmx2) * NORM_SLACK)[:, 0:n_heads]

    zero_w = jnp.zeros((LANES, LANES), BF16)
    for g in range(0, len(POOL_WINDOWS), 2):
        cols = slice(g * LANES, (g + 2) * LANES)
        w2 = jnp.concatenate([jnp.concatenate([wp_ref[g].astype(BF16), zero_w], axis=1),
                              jnp.concatenate([zero_w, wp_ref[g + 1].astype(BF16)], axis=1)], axis=0)
        y = _dot(jnp.concatenate([pool_d[g], pool_d[g + 1]], axis=1), w2) * ps_ref[:, cols]
        zp_ref[:, cols] = (y * spg[:, cols]).astype(BF16)

    kmax = jnp.where(step == 0, kmax, jnp.maximum(kmx_ref[...], kmax))
    kmx_ref[...] = kmax
    kg_ref[...] = kmax


def _project(x2, ada, norm_g, wit, b_f, wp, ps, hist0, *, bm, sb, segs, ada_row, start_pos, n_heads,
             kv_head_major):
    rows, d = x2.shape
    a_w = n_heads * HEAD_DIM
    pw = len(POOL_WINDOWS) * LANES
    n_steps = rows // bm
    assert segs == 1 or n_steps == 1
    n_streams = segs
    seg_rows = bm // segs
    nsb = bm // sb
    assert nsb == 8 and seg_rows % sb == 0
    row_blk = lambda w: pl.BlockSpec((bm, w), lambda i, *_: (i, 0))
    per_stream = lambda r, w: pl.BlockSpec((segs, r, w), lambda i, *_: (0, 0, 0))
    kern = functools.partial(_proj_kernel, bm=bm, sb=sb, segs=segs, ada_row=ada_row, start_pos=start_pos,
                             n_heads=n_heads, kv_head_major=kv_head_major)
    if kv_head_major:
        kv_shape = (rows, n_heads, HEAD_DIM)
        kv_blk = pl.BlockSpec((bm, n_heads, HEAD_DIM), lambda i, *_: (i, 0, 0))
    else:
        kv_shape = (a_w, rows)
        kv_blk = pl.BlockSpec((a_w, bm), lambda i, *_: (0, i))
    assert pw == a_w
    out_shape = (
        jax.ShapeDtypeStruct((rows, ACT_PARTS * a_w), BF16),
        jax.ShapeDtypeStruct(kv_shape, F32),
        jax.ShapeDtypeStruct(kv_shape, F32),
        jax.ShapeDtypeStruct((n_streams, n_heads, rows // n_streams), F32),
        jax.ShapeDtypeStruct((rows, n_heads), F32),
        jax.ShapeDtypeStruct((rows // sb, n_heads, sb), F32),
        jax.ShapeDtypeStruct((rows // sb, n_heads), F32),
        jax.ShapeDtypeStruct((rows // sb, n_heads), F32),
        jax.ShapeDtypeStruct((1, LANES), F32),
        jax.ShapeDtypeStruct((n_streams, 16, pw), F32),
    )
    out_specs = (
        row_blk(ACT_PARTS * a_w), kv_blk, kv_blk,
        pl.BlockSpec((segs, n_heads, seg_rows), lambda i, *_: (0, 0, i)),
        row_blk(n_heads),
        pl.BlockSpec((nsb, n_heads, sb), lambda i, *_: (i, 0, 0)),
        pl.BlockSpec((nsb, n_heads), lambda i, *_: (i, 0)),
        pl.BlockSpec((nsb, n_heads), lambda i, *_: (i, 0)),
        pl.BlockSpec((1, LANES), lambda i, *_: (0, 0)),
        per_stream(16, pw),
    )
    in_specs = [
        row_blk(d),
        _resident(ada.shape),
        _resident((1, d)),
        _resident(wit.shape), _resident(wp.shape), _resident(ps.shape),
        per_stream(HIST_PAD, pw),
    ]
    return pl.pallas_call(
        kern,
        grid_spec=pltpu.PrefetchScalarGridSpec(
            num_scalar_prefetch=1,
            grid=(n_steps,),
            in_specs=in_specs,
            out_specs=out_specs,
            scratch_shapes=[pltpu.VMEM((bm + segs * HIST_PAD, pw), F32),
                            pltpu.VMEM((bm + segs * HIST_PAD, pw), F32),
                            pltpu.VMEM((bm + segs * HIST_PAD, pw - LANES), F32),
                            pltpu.VMEM((bm + segs * HIST_PAD, pw - 2 * LANES), F32),
                            pltpu.VMEM((1, LANES), F32),
                            pltpu.VMEM((bm, d), BF16)]),
        out_shape=out_shape,
        compiler_params=pltpu.CompilerParams(dimension_semantics=("arbitrary",), vmem_limit_bytes=VMEM_LIMIT),
        name="proj",
    )(b_f, x2, ada, norm_g, wit, wp, ps, hist0)


def _merge_norm(x, gate, za, zp, wo_ref, fg, a_w):
    dy = _dot(za, wo_ref[0:a_w, :].astype(BF16)) + _dot(zp, wo_ref[a_w:, :].astype(BF16))
    out = x + gate * dy
    ms = jnp.mean(out * out, axis=-1, keepdims=True)
    return out * lax.rsqrt(ms + EPS) * fg


def _attn_kernel(tot_ref, qmx_ref, kg_ref,
                 act_ref, cq_ref, ck_ref, x_ref, ada_ref, wo_ref, fg_ref,
                 y_ref,
                 k_ref, v_ref, z_ref, m_ref, l_ref, acc_ref, straight_ref, *, tm, n_heads):
    a_w = n_heads * HEAD_DIM
    q_ref, knew_ref, vnew_ref, sa_ref, zp_ref = _act_views(act_ref, a_w)
    n_pairs = n_heads // PAIR
    nsub = tm // ATT_BLK
    w_near = NEAR_BLOCKS
    cw = w_near * ATT_BLK
    step = pl.program_id(0)

    def keep_step_rows():
        k_ref[pl.ds(pl.multiple_of(step * tm, tm), tm), :] = knew_ref[...]
        v_ref[pl.ds(pl.multiple_of(step * tm, tm), tm), :] = vnew_ref[...]

    lo_q = lax.broadcasted_iota(jnp.int32, (ATT_BLK, LANES), 1) < HEAD_DIM
    lo_k = lax.broadcasted_iota(jnp.int32, (cw, LANES), 1) < HEAD_DIM
    col = lax.broadcasted_iota(jnp.int32, (ATT_BLK, cw), 1)
    tri = (lax.broadcasted_iota(jnp.int32, (ATT_BLK, ATT_BLK), 1)
           <= lax.broadcasted_iota(jnp.int32, (ATT_BLK, ATT_BLK), 0))
    tri_bias = jnp.where(tri, 0.0, NEG).astype(F32)
    zeros_k = jnp.zeros((cw, LANES), BF16)
    ind_lo = jnp.where(lo_k, 1.0, 0.0).astype(BF16)
    ind_hi = jnp.where(lo_k, 0.0, 1.0).astype(BF16)

    def tot_at(b, hd):
        return jnp.where(b >= 0, tot_ref[jnp.maximum(b, 0), hd], 0.0)

    def rows_of(ref, blocks, g):
        return jnp.concatenate(
            [ref[pl.ds(pl.multiple_of(b * ATT_BLK, ATT_BLK), ATT_BLK), g * LANES:(g + 1) * LANES] for b in blocks],
            axis=0)

    def step_rows_of(ref, new_ref, sub, g):
        parts = []
        for p in range(w_near):
            rel = sub - (w_near - 1) + p
            if rel >= 0:
                parts.append(new_ref[rel * ATT_BLK:(rel + 1) * ATT_BLK, g * LANES:(g + 1) * LANES])
            else:
                start = pl.multiple_of((step * nsub + rel) * ATT_BLK, ATT_BLK)
                parts.append(ref[pl.ds(start, ATT_BLK), g * LANES:(g + 1) * LANES])
        return jnp.concatenate(parts, axis=0)

    def pair_scores(r0, kc, g):
        keys = jnp.concatenate([jnp.where(lo_k, kc, zeros_k), jnp.where(lo_k, zeros_k, kc)], axis=0)
        return _dot_nt(q_ref[pl.ds(r0, ATT_BLK), g * LANES:(g + 1) * LANES], keys)

    def pair_values(p_pair, vc):
        vals = jnp.concatenate([jnp.concatenate([jnp.where(lo_k, vc, zeros_k), ind_lo], axis=1),
                                jnp.concatenate([jnp.where(lo_k, zeros_k, vc), ind_hi], axis=1)], axis=0)
        return _dot(p_pair, vals)

    def decay_row(blocks, offs, hd):
        return jnp.concatenate([offs[p] - ck_ref[blocks[p], hd:hd + 1, :] for p in range(w_near)], axis=1)

    def near_weights(s_pair, rows, g, near_c, near_offs, m_bound):
        bounded = m_bound is not None
        ps, ms = [], []
        for e in range(PAIR):
            hd = PAIR * g + e
            cqh = cq_ref[rows, hd:hd + 1]
            if bounded:
                m = jnp.full((ATT_BLK, 1), m_bound[hd], F32)
                cqh = cqh - m_bound[hd]
            dec = decay_row(near_c, near_offs[hd], hd)
            pieces = []
            for p in range(w_near):
                lanes = slice(e * cw + p * ATT_BLK, e * cw + (p + 1) * ATT_BLK)
                sp = s_pair[:, lanes] + cqh + dec[:, p * ATT_BLK:(p + 1) * ATT_BLK]
                pieces.append(sp + tri_bias if p == w_near - 1 else sp)
            s = jnp.concatenate(pieces, axis=1)
            if not bounded:
                m = jnp.max(s, axis=1, keepdims=True)
                s = s - m
            ms.append(m)
            ps.append(jnp.exp2(s).astype(BF16))
        return jnp.concatenate(ps, axis=1), ms

    def gated_output(rows, g, acc, l):
        cols = slice(g * LANES, (g + 1) * LANES)
        z_ref[rows, cols] = ((acc / l) * sa_ref[rows, cols].astype(F32)).astype(BF16)

    def merge(chunks=1):
        rows_c = tm // chunks
        for c in range(chunks):
            rows = slice(c * rows_c, (c + 1) * rows_c)
            y_ref[rows, :] = _merge_norm(x_ref[rows, :], ada_ref[2, ADA_PROMPT_ROW:ADA_PROMPT_ROW + 1, :],
                                         z_ref[rows, :], zp_ref[rows, :], wo_ref, fg_ref[...], a_w)

    def check_next_step():
        ok = jnp.bool_(True)
        last = qmx_ref.shape[0] - 1
        for sub in range(nsub):
            i = jnp.minimum((step + 1) * nsub + sub, last)
            for hd in range(n_heads):
                qk = 2.0 * qmx_ref[i, hd] * kg_ref[0, hd]
                back = qk
                for dd in range(1, w_near):
                    back = back + tot_ref[i - dd, hd]
                ok = jnp.logical_and(ok, jnp.logical_and(qk <= EXP2_SAFE_SPAN, back < -EXP2_UNDERFLOW))
        straight_ref[0] = ok.astype(jnp.int32)

    def straight_step():
        units = [(sub, g) for sub in range(nsub) for g in range(n_pairs)]
        near_of, offs_of, bound_of = [], [], []
        for sub in range(nsub):
            i = step * nsub + sub
            near_of.append([i - (w_near - 1) + p for p in range(w_near)])
            bound_of.append([qmx_ref[i, hd] * kg_ref[0, hd] for hd in range(n_heads)])
            offs = []
            for hd in range(n_heads):
                o, per_piece = jnp.float32(0.0), [jnp.float32(0.0)]
                for dd in range(1, w_near):
                    o = o + tot_ref[i - dd, hd]
                    per_piece.append(o)
                offs.append(per_piece[::-1])
            offs_of.append(offs)
        s_next = pair_scores(0, step_rows_of(k_ref, knew_ref, 0, 0), 0)
        for u, (sub, g) in enumerate(units):
            s_pair = s_next
            if u + 1 < len(units):
                sub1, g1 = units[u + 1]
                s_next = pair_scores(sub1 * ATT_BLK, step_rows_of(k_ref, knew_ref, sub1, g1), g1)
            rows = pl.ds(sub * ATT_BLK, ATT_BLK)
            p_pair, _ = near_weights(s_pair, rows, g, near_of[sub], offs_of[sub], bound_of[sub])
            r = pair_values(p_pair, step_rows_of(v_ref, vnew_ref, sub, g))
            gated_output(rows, g, r[:, 0:LANES], r[:, LANES:])
        check_next_step()
        keep_step_rows()
        merge(chunks=2)

    def sub_body(sub, carry):
        i = step * nsub + sub
        r0 = pl.multiple_of(sub * ATT_BLK, ATT_BLK)

        rows = pl.ds(r0, ATT_BLK)
        qk = [2.0 * qmx_ref[i, hd] * kg_ref[0, hd] for hd in range(n_heads)]

        near = [i - (w_near - 1) + p for p in range(w_near)]
        near_c = [jnp.maximum(b, 0) for b in near]
        near_offs, offs_far = [], []
        for hd in range(n_heads):
            back = [tot_at(i - dd, hd) for dd in range(1, w_near)]
            offs = []
            for p in range(w_near):
                o = jnp.float32(0.0)
                for dd in range(1, w_near - p):
                    o = o + back[dd - 1]
                offs.append(jnp.where(near[p] >= 0, o, NEG))
            near_offs.append(offs)
            o = jnp.float32(0.0)
            for t in back:
                o = o + t
            offs_far.append(o)

        def near_chunk(bounded):
            def fn():
                s_next = pair_scores(r0, rows_of(k_ref, near_c, 0), 0)
                for g in range(n_pairs):
                    s_pair = s_next
                    if g + 1 < n_pairs:
                        s_next = pair_scores(r0, rows_of(k_ref, near_c, g + 1), g + 1)
                    p_pair, ms = near_weights(s_pair, rows, g, near_c, near_offs,
                                              [0.5 * b for b in qk] if bounded else None)
                    for e in range(PAIR):
                        m_ref[PAIR * g + e] = ms[e]
                    r = pair_values(p_pair, rows_of(v_ref, near_c, g))
                    acc_ref[g] = r[:, 0:LANES]
                    l_ref[g] = r[:, LANES:]
                    gated_output(rows, g, r[:, 0:LANES], r[:, LANES:])
            return fn

        bounded_ok = qk[0] <= EXP2_SAFE_SPAN
        for hd in range(1, n_heads):
            bounded_ok = jnp.logical_and(bounded_ok, qk[hd] <= EXP2_SAFE_SPAN)
        pl.when(bounded_ok)(near_chunk(True))
        pl.when(jnp.logical_not(bounded_ok))(near_chunk(False))

        def far_cond(c):
            top = i - c[0] * w_near
            need = qk[0] + c[1] >= -EXP2_UNDERFLOW
            for hd in range(1, n_heads):
                need = jnp.logical_or(need, qk[hd] + c[1 + hd] >= -EXP2_UNDERFLOW)
            return jnp.logical_and(top >= 0, need)

        def far_body(c):
            top = i - c[0] * w_near
            jc = jnp.maximum(top - (w_near - 1), 0)
            blocks = [jc + p for p in range(w_near)]
            keepc = col < (top + 1 - jc) * ATT_BLK
            new = [c[0] + 1]
            for g in range(n_pairs):
                s_pair = pair_scores(r0, rows_of(k_ref, blocks, g), g)
                ps, alphas = [], []
                for e in range(PAIR):
                    hd = PAIR * g + e
                    tt = [tot_at(top - b, hd) for b in range(w_near)]
                    offs = []
                    for p in range(w_near):
                        behind = top - (jc + p)
                        o = c[1 + hd]
                        for b in range(w_near):
                            o = o + jnp.where(behind >= b, tt[b], 0.0)
                        offs.append(o)
                    s = s_pair[:, e * cw:(e + 1) * cw] + cq_ref[rows, hd:hd + 1] + decay_row(blocks, offs, hd)
                    s = jnp.where(keepc, s, NEG)
                    m_old = m_ref[hd]
                    m_new = jnp.maximum(m_old, jnp.max(s, axis=1, keepdims=True))
                    m_ref[hd] = m_new
                    alphas.append(jnp.broadcast_to(jnp.exp2(m_old - m_new), (ATT_BLK, LANES)))
                    ps.append(jnp.exp2(s - m_new).astype(BF16))
                    o = c[1 + hd]
                    for t in tt:
                        o = o + t
                    new.append(o)
                alpha = jnp.where(lo_q, alphas[0], alphas[1])
                r = pair_values(jnp.concatenate(ps, axis=1), rows_of(v_ref, blocks, g))
                acc_ref[g] = alpha * acc_ref[g] + r[:, 0:LANES]
                l_ref[g] = alpha * l_ref[g] + r[:, LANES:]
            return tuple(new)

        far = lax.while_loop(far_cond, far_body, (jnp.int32(1),) + tuple(offs_far))

        @pl.when(far[0] > 1)
        def _():
            for g in range(n_pairs):
                gated_output(rows, g, acc_ref[g], l_ref[g])
        return carry

    @pl.when(step == 0)
    def _():
        straight_ref[0] = 0

    straight = straight_ref[0] == 1
    pl.when(straight)(straight_step)

    @pl.when(jnp.logical_not(straight))
    def _():
        keep_step_rows()
        check_next_step()
        lax.fori_loop(0, nsub, sub_body, 0)
        merge()


def _prompt_attention(tot, qmx, kg, act, cq, ck, x2, ada, wo, fg, *, tm, n_heads):
    rows, d = x2.shape
    a_w = n_heads * HEAD_DIM
    n_pairs = n_heads // PAIR
    assert rows % tm == 0 and tm % ATT_BLK == 0 and tm // ATT_BLK >= NEAR_BLOCKS - 1 and n_heads % PAIR == 0
    row_blk = lambda w: pl.BlockSpec((tm, w), lambda i, *_: (i, 0))
    grid_spec = pltpu.PrefetchScalarGridSpec(
        num_scalar_prefetch=3,
        grid=(rows // tm,),
        in_specs=[row_blk(ACT_PARTS * a_w), row_blk(n_heads), _resident(ck.shape), row_blk(d),
                  _resident(ada.shape), _resident(wo.shape), _resident((1, d))],
        out_specs=row_blk(d),
        scratch_shapes=[pltpu.VMEM((rows, a_w), BF16),
                        pltpu.VMEM((rows, a_w), BF16),
                        pltpu.VMEM((tm, a_w), BF16),
                        pltpu.VMEM((n_heads, ATT_BLK, 1), F32),
                        pltpu.VMEM((n_pairs, ATT_BLK, LANES), F32),
                        pltpu.VMEM((n_pairs, ATT_BLK, LANES), F32),
                        pltpu.SMEM((1,), jnp.int32)],
    )
    return pl.pallas_call(
        functools.partial(_attn_kernel, tm=tm, n_heads=n_heads),
        grid_spec=grid_spec,
        out_shape=jax.ShapeDtypeStruct((rows, d), F32),
        compiler_params=pltpu.CompilerParams(dimension_semantics=("arbitrary",), vmem_limit_bytes=VMEM_LIMIT),
        name="attn",
    )(tot, qmx, kg, act, cq, ck, x2, ada, wo, fg)


CACHE_SPLIT = 2


def _sattn_kernel(act_ref, cq_ref, ckn_ref, kc0_ref, kc1_ref, vc0_ref, vc1_ref, lfc_ref,
                  x_ref, ada_ref, wo_ref, fg_ref, y_ref, *, n_heads):
    a_w = n_heads * HEAD_DIM
    q_ref, kn_ref, vn_ref, sa_ref, zp_ref = _act_views(act_ref, a_w)
    ln = act_ref.shape[0]
    past = kc0_ref.shape[2]
    kc_refs, vc_refs = (kc0_ref, kc1_ref), (vc0_ref, vc1_ref)
    pairs_per_part = n_heads // PAIR // CACHE_SPLIT

    def cached(refs, g):
        lo = (g % pairs_per_part) * LANES
        return refs[g // pairs_per_part][0, lo:lo + LANES, :].astype(BF16)

    nb = past // LANES
    lane = lax.broadcasted_iota(jnp.int32, (ln, LANES), 1)
    half = [lane < HEAD_DIM, lane >= HEAD_DIM]
    causal = lax.broadcasted_iota(jnp.int32, (ln, ln), 1) <= lax.broadcasted_iota(jnp.int32, (ln, ln), 0)
    n_pairs = n_heads // PAIR

    def pair_scores(g):
        cols = slice(g * LANES, (g + 1) * LANES)
        q2 = q_ref[:, cols]
        qst = jnp.concatenate([jnp.where(half[e], q2, jnp.zeros_like(q2)) for e in range(PAIR)], axis=0)
        return _dot(qst, cached(kc_refs, g)), _dot_nt(qst, kn_ref[:, cols])

    outs = []
    s_next = pair_scores(0)

    lfc = lfc_ref[0] * LOG2E
    cs = _lane_cumsum(lfc, LANES)
    after = jnp.zeros((n_heads, 1), F32)
    suffix = [None] * nb
    for b in reversed(range(nb)):
        cb = cs[:, b * LANES:(b + 1) * LANES]
        tot = cb[:, LANES - 1:LANES]
        suffix[b] = (tot - cb) + after
        after = after + tot
    dec_c = jnp.concatenate(suffix, axis=1)

    for g in range(n_pairs):
        cols = slice(g * LANES, (g + 1) * LANES)
        sc_st, sn_st = s_next
        if g + 1 < n_pairs:
            s_next = pair_scores(g + 1)
        vct = cached(vc_refs, g)
        vn = vn_ref[:, cols]
        pc, pn, ls = [], [], []
        for e in range(PAIR):
            hd = PAIR * g + e
            rows = slice(e * ln, (e + 1) * ln)
            cqh = cq_ref[:, hd:hd + 1]
            s_c = sc_st[rows] + cqh + dec_c[hd:hd + 1, :]
            s_n = jnp.where(causal, sn_st[rows] + cqh - ckn_ref[0, hd:hd + 1, :], NEG)
            m = jnp.maximum(jnp.max(s_c, axis=1, keepdims=True), jnp.max(s_n, axis=1, keepdims=True))
            p_c = jnp.exp2(s_c - m)
            p_n = jnp.exp2(s_n - m)
            ls.append(jnp.sum(p_c, axis=1, keepdims=True) + jnp.sum(p_n, axis=1, keepdims=True))
            pc.append(p_c.astype(BF16))
            pn.append(p_n.astype(BF16))
        acc = _dot_nt(jnp.concatenate(pc, axis=0), vct) + _dot(jnp.concatenate(pn, axis=0), vn)
        o = jnp.where(half[0], acc[0:ln] / ls[0], acc[ln:2 * ln] / ls[1])
        outs.append((o * sa_ref[:, cols].astype(F32)).astype(BF16))
    za = jnp.concatenate(outs, axis=1)
    y_ref[...] = _merge_norm(x_ref[...], ada_ref[2, pl.ds(ADA_SAMPLE_ROW + pl.program_id(0), 1), :], za, zp_ref[...], wo_ref, fg_ref[...], a_w)


def _sample_attention(act, cq, ckn, cache_k, cache_v, lfc, x2, ada, wo, fg, *, ln, n_heads):
    rows, d = x2.shape
    nbatch = rows // ln
    a_w = n_heads * HEAD_DIM
    past = cache_k.shape[2]
    row_blk = lambda w: pl.BlockSpec((ln, w), lambda b: (b, 0))
    per_b = lambda s: pl.BlockSpec((1,) + s, lambda b: (b, 0, 0))
    cache_part = lambda c: pl.BlockSpec((1, a_w // CACHE_SPLIT, past), lambda b: (b, c, 0))
    assert CACHE_SPLIT == 2 and (n_heads // PAIR) % CACHE_SPLIT == 0
    return pl.pallas_call(
        functools.partial(_sattn_kernel, n_heads=n_heads),
        grid=(nbatch,),
        in_specs=[row_blk(ACT_PARTS * a_w), row_blk(n_heads), per_b((n_heads, ln)),
                  cache_part(0), cache_part(1), cache_part(0), cache_part(1), per_b((n_heads, past)),
                  row_blk(d), _resident(ada.shape), _resident(wo.shape), _resident((1, d))],
        out_specs=row_blk(d),
        out_shape=jax.ShapeDtypeStruct((rows, d), F32),
        compiler_params=pltpu.CompilerParams(dimension_semantics=("arbitrary",), vmem_limit_bytes=VMEM_LIMIT),
        name="sattn",
    )(act, cq, ckn, cache_k, cache_k, cache_v, cache_v, lfc, x2, ada, wo, fg)


def kernel(x_prompt, x_sample, c_prompt, c_sample, cache_k, cache_v, cache_logf, state_pool, norm_g, w_ada, b_ada,
           w_in, b_f, w_pool, pool_scale, w_out, final_g):
    depth = norm_g.shape[0]
    assert depth == 1
    bp, seq, d = x_prompt.shape
    bs, ln, _ = x_sample.shape
    assert bp == 1
    n_heads = cache_k.shape[3]
    past = cache_k.shape[2]
    a_w = n_heads * HEAD_DIM
    pw = state_pool.shape[3]
    assert pw == len(POOL_WINDOWS) * LANES and cache_k.shape[4] == HEAD_DIM and n_heads <= 8

    ada = _ada_terms(c_prompt, c_sample, w_ada[0], b_ada)

    wit = w_in[0].T
    wp = w_pool[0]
    ps = pool_scale[0][None, :]
    wo = w_out[0]
    ng = norm_g[0][None, :]
    fg = final_g[None, :]

    assert seq % PROJ_ROWS == 0 and seq % ATTN_ROWS == 0
    xp2 = x_prompt.reshape(seq, d)
    hist_p = jnp.zeros((1, HIST_PAD, pw), F32)
    (act_p, k_p, v_p, lf_p, cq_p, ck_p, tot, qmx, kg, ho_p) = _project(
        xp2, ada, ng, wit, b_f[0], wp, ps, hist_p,
        bm=PROJ_ROWS, sb=ATT_BLK, segs=1, ada_row=ADA_PROMPT_ROW, start_pos=0, n_heads=n_heads, kv_head_major=False)
    y_p = _prompt_attention(tot, qmx, kg, act_p, cq_p, ck_p, xp2, ada, wo, fg, tm=ATTN_ROWS, n_heads=n_heads)

    xs2 = x_sample.reshape(bs * ln, d)
    hist_s = jnp.pad(state_pool[0], ((0, 0), (HIST_PAD - POOL_HIST, 0), (0, 0)))
    (act_s, k_s, v_s, lf_s, cq_s, ck_s, _, _, _, ho_s) = _project(
        xs2, ada, ng, wit, b_f[0], wp, ps, hist_s,
        bm=bs * ln, sb=ln, segs=bs, ada_row=ADA_SAMPLE_ROW, start_pos=past, n_heads=n_heads, kv_head_major=True)
    lfc = jnp.swapaxes(cache_logf[0], 1, 2)
    ckt = jnp.transpose(cache_k[0], (0, 2, 3, 1)).reshape(bs, a_w, past)
    cvt = jnp.transpose(cache_v[0], (0, 2, 3, 1)).reshape(bs, a_w, past)
    y_s = _sample_attention(act_s, cq_s, ck_s, ckt, cvt, lfc, xs2, ada, wo, fg, ln=ln, n_heads=n_heads)

    hd = (n_heads, HEAD_DIM)
    seq_minor = lambda t: jnp.transpose(t.reshape(hd + (bp, seq)), (2, 3, 0, 1))[None]
    return (y_p.reshape(bp, seq, d), y_s.reshape(bs, ln, d),
            seq_minor(k_p), seq_minor(v_p), jnp.swapaxes(lf_p, 1, 2)[None],
            ho_p[:, 16 - POOL_HIST:, :][None],
            k_s.reshape((1, bs, ln) + hd), v_s.reshape((1, bs, ln) + hd), jnp.swapaxes(lf_s, 1, 2)[None],
            ho_s[:, 16 - POOL_HIST:, :][None])
```
